```python
import math
import jax, jax.numpy as jnp
from jax import lax
import numpy as np

D_MODEL = 1024
BATCH = 8
SEQ = 2048
DEPTH = 1
DEC_BATCH = 128
DEC_SEQ = 4
PAST_LEN = 16384
PAGE_SIZE = 128

MIX_WIDTH = D_MODEL
DN_HEADS = 4
DN_DK = 128
DN_DV = 128
DN_WIDTH = DN_HEADS * DN_DV
SC_WIDTH = MIX_WIDTH - DN_WIDTH
SC_GROUPS = 8
DN_CONV = 4
SC_CONV = 3
CHUNK = 64
QKV_DIM = DN_HEADS * (2 * DN_DK + DN_DV)
PROJ_DIM = QKV_DIM + 2 * DN_HEADS + DN_WIDTH + 3 * SC_WIDTH
D_FF = 2816
N_SUB = 3
ALPHA = (2 * DEPTH) ** 0.25
BETA = (8 * DEPTH) ** -0.25
LN_EPS = 1e-5
RMS_EPS = 1e-6

kernel_name = "hymba_gdn_shortconv_macaron_deepnorm_step"


def layer_norm(x, g, b):
    xf = x.astype(jnp.float32)
    mu = jnp.mean(xf, -1, keepdims=True)
    var = jnp.mean(jnp.square(xf - mu), -1, keepdims=True)
    return ((xf - mu) * lax.rsqrt(var + LN_EPS) * g + b).astype(x.dtype)


def swiglu(u, wg, wu, wd):
    return (jax.nn.silu(u @ wg) * (u @ wu)) @ wd


def causal_dwconv(x, buf, w):
    W = w.shape[0]
    T = x.shape[1]
    xp = jnp.concatenate([buf.astype(x.dtype), x], axis=1)
    y = sum(xp[:, j:j + T] * w[j] for j in range(W))
    return y, xp[:, xp.shape[1] - (W - 1):]


def l2norm(x):
    xf = x.astype(jnp.float32)
    return xf * lax.rsqrt(jnp.sum(xf * xf, -1, keepdims=True) + RMS_EPS)


def gated_delta_rule(q, k, v, g, beta, s0):
    B, T, H, _ = q.shape
    C = math.gcd(T, CHUNK)
    N = T // C
    f32 = jnp.float32

    def chunks(a):
        a = a.astype(f32).reshape((B, N, C, H) + a.shape[3:])
        return jnp.moveaxis(a, 3, 1)

    q, k, v, g, beta = chunks(q), chunks(k), chunks(v), chunks(g), chunks(beta)
    gc = jnp.cumsum(g, axis=-1)
    tril = jnp.tril(jnp.ones((C, C), bool))
    strict = jnp.tril(jnp.ones((C, C), bool), -1)
    diff = gc[..., :, None] - gc[..., None, :]
    decay = jnp.where(tril, jnp.exp(jnp.where(tril, diff, 0.0)), 0.0)
    kb = k * beta[..., None]
    L = jnp.where(strict, jnp.einsum('bhnik,bhnjk->bhnij', kb, k) * decay, 0.0)
    eye = jnp.eye(C, dtype=f32)
    Tm = lax.linalg.triangular_solve(eye + L, jnp.broadcast_to(eye, L.shape),
                                     left_side=True, lower=True)
    u = jnp.einsum('bhnij,bhnjv->bhniv', Tm, v * beta[..., None])
    w = jnp.einsum('bhnij,bhnjk->bhnik', Tm, kb * jnp.exp(gc)[..., None])
    qk = jnp.where(tril, jnp.einsum('bhnik,bhnjk->bhnij', q, k) * decay, 0.0)
    q_dec = q * jnp.exp(gc)[..., None]
    k_dec = k * jnp.exp(gc[..., -1:] - gc)[..., None]
    g_last = jnp.exp(gc[..., -1])

    def step(S, xs):
        u_n, w_n, qk_n, qd_n, kd_n, gl_n = xs
        v_new = u_n - jnp.einsum('bhck,bhkv->bhcv', w_n, S)
        o = jnp.einsum('bhck,bhkv->bhcv', qd_n, S) + jnp.einsum('bhij,bhjv->bhiv', qk_n, v_new)
        S = S * gl_n[..., None, None] + jnp.einsum('bhck,bhcv->bhkv', kd_n, v_new)
        return S, o

    xs = tuple(jnp.moveaxis(a, 2, 0) for a in (u, w, qk, q_dec, k_dec, g_last))
    S, o = lax.scan(step, s0.astype(f32), xs)
    o = jnp.transpose(o, (1, 0, 3, 2, 4)).reshape(B, T, H, -1)
    return o, S.astype(s0.dtype)


def hybrid_layer(x, c, s_ssm, s_cqkv, s_cmix, w_ada, b_ada, ln_g, ln_b,
                 f1_wg, f1_wu, f1_wd, f2_wg, f2_wu, f2_wd,
                 w_in, conv_qkv_w, a_log, dt_bias, dn_norm_g, conv_mix_w, w_out):
    B, T, D = x.shape
    mod = (jax.nn.silu(c) @ w_ada + b_ada).reshape(B, N_SUB, 3, D)
    shift, scale, gate = mod[:, :, 0], mod[:, :, 1], mod[:, :, 2]

    def modulate(h, i):
        return h * (1.0 + scale[:, None, i]) + shift[:, None, i]

    def post(h, delta, i):
        return layer_norm(ALPHA * h + gate[:, None, i] * delta, ln_g[i], ln_b[i])

    x = post(x, 0.5 * swiglu(modulate(x, 0), f1_wg, f1_wu, f1_wd), 0)

    u = modulate(x, 1)
    p = u @ w_in
    cuts = np.cumsum([QKV_DIM, DN_HEADS, DN_HEADS, DN_WIDTH, SC_WIDTH, SC_WIDTH]).tolist()
    qkv, a, b, og, sc_b, sc_c, sc_h = jnp.split(p, cuts, axis=-1)

    qkv, new_cqkv = causal_dwconv(qkv, s_cqkv, conv_qkv_w)
    qkv = jax.nn.silu(qkv)
    q, k, v = jnp.split(qkv, [DN_HEADS * DN_DK, 2 * DN_HEADS * DN_DK], axis=-1)
    q = l2norm(q.reshape(B, T, DN_HEADS, DN_DK)) * (DN_DK ** -0.5)
    k = l2norm(k.reshape(B, T, DN_HEADS, DN_DK))
    v = v.reshape(B, T, DN_HEADS, DN_DV)
    beta = jax.nn.sigmoid(b.astype(jnp.float32))
    g = -jnp.exp(a_log.astype(jnp.float32)) * jax.nn.softplus(a.astype(jnp.float32) + dt_bias)
    o, new_ssm = gated_delta_rule(q, k, v, g, beta, s_ssm)
    o = o * lax.rsqrt(jnp.mean(o * o, -1, keepdims=True) + RMS_EPS) * dn_norm_g
    o = o * jax.nn.silu(og.reshape(B, T, DN_HEADS, DN_DV).astype(jnp.float32))
    o_dn = o.reshape(B, T, DN_WIDTH).astype(x.dtype)

    z = sc_c * sc_h
    zc, new_cmix = causal_dwconv(z, s_cmix, conv_mix_w)
    o_sc = sc_b * zc

    mix = jnp.concatenate([o_dn, o_sc], axis=-1) @ w_out
    x = post(x, mix, 1)

    x = post(x, 0.5 * swiglu(modulate(x, 2), f2_wg, f2_wu, f2_wd), 2)
    return x, new_ssm, new_cqkv.astype(s_cqkv.dtype), new_cmix.astype(s_cmix.dtype)


def setup_inputs(seed: int = 0) -> dict:
    key = jax.random.key(seed)
    ks = iter(jax.random.split(key, 32))
    nrm = lambda shape, s: jax.random.normal(next(ks), shape, jnp.float32) * s
    D, L = D_MODEL, DEPTH
    return {
        "x_prompt": nrm((BATCH, SEQ, D), 1.0),
        "x_sample": nrm((DEC_BATCH, DEC_SEQ, D), 1.0),
        "state_ssm": nrm((L, DEC_BATCH, DN_HEADS, DN_DK, DN_DV), 0.1),
        "state_conv_qkv": nrm((L, DEC_BATCH, DN_CONV - 1, QKV_DIM), 1.0),
        "state_conv_mix": nrm((L, DEC_BATCH, SC_CONV - 1, SC_WIDTH), 1.0),
        "c_prompt": nrm((BATCH, D), 1.0),
        "c_sample": nrm((DEC_BATCH, D), 1.0),
        "w_ada": nrm((L, D, N_SUB * 3 * D), 0.5 * D ** -0.5),
        "b_ada": nrm((L, N_SUB * 3 * D), 0.02),
        "ln_g": 1.0 + nrm((L, N_SUB, D), 0.02),
        "ln_b": nrm((L, N_SUB, D), 0.02),
        "ffn1_wg": nrm((L, D, D_FF), D ** -0.5),
        "ffn1_wu": nrm((L, D, D_FF), D ** -0.5),
        "ffn1_wd": nrm((L, D_FF, D), BETA * D_FF ** -0.5),
        "ffn2_wg": nrm((L, D, D_FF), D ** -0.5),
        "ffn2_wu": nrm((L, D, D_FF), D ** -0.5),
        "ffn2_wd": nrm((L, D_FF, D), BETA * D_FF ** -0.5),
        "w_in": nrm((L, D, PROJ_DIM), D ** -0.5),
        "conv_qkv_w": nrm((L, DN_CONV, QKV_DIM), DN_CONV ** -0.5),
        "a_log": jnp.log(jax.random.uniform(next(ks), (L, DN_HEADS), jnp.float32, 1.0, 16.0)),
        "dt_bias": nrm((L, DN_HEADS), 0.1),
        "dn_norm_g": 1.0 + nrm((L, DN_DV), 0.02),
        "conv_mix_w": nrm((L, SC_CONV, SC_WIDTH), SC_CONV ** -0.5),
        "w_out": nrm((L, MIX_WIDTH, D), BETA * MIX_WIDTH ** -0.5),
    }


def reference(x_prompt, x_sample, state_ssm, state_conv_qkv, state_conv_mix, c_prompt, c_sample,
              w_ada, b_ada, ln_g, ln_b, ffn1_wg, ffn1_wu, ffn1_wd, ffn2_wg, ffn2_wu, ffn2_wd,
              w_in, conv_qkv_w, a_log, dt_bias, dn_norm_g, conv_mix_w, w_out):
    Bp = x_prompt.shape[0]
    dt = x_prompt.dtype
    hp, hs = x_prompt, x_sample
    sp_ssm, sp_cqkv, sp_cmix = [], [], []
    ss_ssm, ss_cqkv, ss_cmix = [], [], []
    for l in range(DEPTH):
        w = (w_ada[l], b_ada[l], ln_g[l], ln_b[l], ffn1_wg[l], ffn1_wu[l], ffn1_wd[l],
             ffn2_wg[l], ffn2_wu[l], ffn2_wd[l], w_in[l], conv_qkv_w[l], a_log[l],
             dt_bias[l], dn_norm_g[l], conv_mix_w[l], w_out[l])
        z_ssm = jnp.zeros((Bp, DN_HEADS, DN_DK, DN_DV), dt)
        z_cqkv = jnp.zeros((Bp, DN_CONV - 1, QKV_DIM), dt)
        z_cmix = jnp.zeros((Bp, SC_CONV - 1, SC_WIDTH), dt)
        hp, a1, a2, a3 = hybrid_layer(hp, c_prompt, z_ssm, z_cqkv, z_cmix, *w)
        hs, b1, b2, b3 = hybrid_layer(hs, c_sample, state_ssm[l], state_conv_qkv[l],
                                      state_conv_mix[l], *w)
        sp_ssm.append(a1); sp_cqkv.append(a2); sp_cmix.append(a3)
        ss_ssm.append(b1); ss_cqkv.append(b2); ss_cmix.append(b3)
    return (hp, hs, jnp.stack(sp_ssm), jnp.stack(sp_cqkv), jnp.stack(sp_cmix),
            jnp.stack(ss_ssm), jnp.stack(ss_cqkv), jnp.stack(ss_cmix))
```

```python
import functools

import jax
import jax.numpy as jnp
from jax import lax
from jax.experimental import pallas as pl
from jax.experimental.pallas import tpu as pltpu

F32 = jnp.float32
BF16 = jnp.bfloat16

LN_EPS = 1e-5
RMS_EPS = 1e-6
N_SUB = 3
DN_HEADS = 4
DN_DK = 128
DN_DV = 128
CHUNK = 64
LANES = 128
SUBLANES = 8
AB_PAD = LANES

V7X_VMEM_LIMIT = 60000 * 1024


def _mm(a, b):
    return jnp.dot(a.astype(BF16), b.astype(BF16), preferred_element_type=F32)


def _mm_bt(a, b):
    return lax.dot_general(a.astype(BF16), b.astype(BF16), (((1,), (1,)), ((), ())),
                           preferred_element_type=F32)


def _mm_at(a, b):
    return lax.dot_general(a.astype(BF16), b.astype(BF16), (((0,), (0,)), ((), ())),
                           preferred_element_type=F32)


def _mm_f32(a, b):
    return jnp.dot(a, b, preferred_element_type=F32, precision=lax.Precision.HIGHEST)


def _silu(x):
    return x * jax.nn.sigmoid(x)


def _softplus(x):
    return jnp.maximum(x, 0.0) + jnp.log1p(jnp.exp(-jnp.abs(x)))


def _layer_norm(y, g, b):
    mu = jnp.mean(y, axis=-1, keepdims=True)
    yc = y - mu
    var = jnp.mean(yc * yc, axis=-1, keepdims=True)
    return yc * lax.rsqrt(var + LN_EPS) * g + b


def _post(x, delta, gate, g, b, alpha):
    return _layer_norm(alpha * x + gate * delta, g, b)


def _rowsum(x):
    return jnp.sum(x, axis=-1, keepdims=True)


def _l2norm(x):
    return x * lax.rsqrt(_rowsum(x * x) + RMS_EPS)


def _const_spec(shape):
    nd = len(shape)
    return pl.BlockSpec(shape, lambda *_: (0,) * nd, pipeline_mode=pl.Buffered(1))


def _adaln_kernel(c_ref, w_ref, b_ref, o_ref):
    s = _silu(c_ref[...])
    o_ref[...] = _mm(s, w_ref[...]) + b_ref[...]


def _adaln(c, w, b, tn=1024):
    m, d = c.shape
    n = w.shape[1]
    assert n % tn == 0
    return pl.pallas_call(
        _adaln_kernel,
        grid=(n // tn,),
        in_specs=[_const_spec((m, d)),
                  pl.BlockSpec((d, tn), lambda j: (0, j)),
                  pl.BlockSpec((1, tn), lambda j: (0, j))],
        out_specs=pl.BlockSpec((m, tn), lambda j: (0, j)),
        out_shape=jax.ShapeDtypeStruct((m, n), F32),
        compiler_params=pltpu.CompilerParams(dimension_semantics=("arbitrary",)),
        name="adaln",
    )(c, w, b)


def _ffn_rows(x, shift, scale, gate, wg_ref, wu_ref, wd_ref, g, b, alpha):
    u = (x * (1.0 + scale) + shift).astype(BF16)
    hg = jnp.dot(u, wg_ref[...], preferred_element_type=F32)
    hu = jnp.dot(u, wu_ref[...], preferred_element_type=F32)
    h = (_silu(hg) * hu).astype(BF16)
    d = jnp.dot(h, wd_ref[...], preferred_element_type=F32)
    return _post(x, 0.5 * d, gate, g, b, alpha)


def _ffn_prompt_kernel(x_ref, mod_ref, wg_ref, wu_ref, wd_ref, lng_ref, lnb_ref, o_ref, *, sub, alpha):
    shift = mod_ref[0, 3 * sub + 0:3 * sub + 1, :]
    scale = mod_ref[0, 3 * sub + 1:3 * sub + 2, :]
    gate = mod_ref[0, 3 * sub + 2:3 * sub + 3, :]
    o_ref[...] = _ffn_rows(x_ref[...], shift, scale, gate, wg_ref, wu_ref, wd_ref,
                           lng_ref[sub:sub + 1, :], lnb_ref[sub:sub + 1, :], alpha)


def _ffn_prompt(x, mod, wg, wu, wd, lng, lnb, *, sub, alpha, tm):
    bsz, t, d = x.shape
    f = wg.shape[1]
    assert t % tm == 0
    per_seq = t // tm
    x2 = x.reshape(bsz * t, d)
    out = pl.pallas_call(
        functools.partial(_ffn_prompt_kernel, sub=sub, alpha=alpha),
        grid=(bsz * per_seq,),
        in_specs=[pl.BlockSpec((tm, d), lambda i: (i, 0)),
                  pl.BlockSpec((1, 3 * N_SUB, d), lambda i: (i // per_seq, 0, 0)),
                  _const_spec((d, f)), _const_spec((d, f)), _const_spec((f, d)),
                  _const_spec(lng.shape), _const_spec(lnb.shape)],
        out_specs=pl.BlockSpec((tm, d), lambda i: (i, 0)),
        out_shape=jax.ShapeDtypeStruct((bsz * t, d), F32),
        compiler_params=pltpu.CompilerParams(dimension_semantics=("arbitrary",),
                                             vmem_limit_bytes=V7X_VMEM_LIMIT),
        name=f"ffn_prompt{sub}",
    )(x2, mod, wg, wu, wd, lng, lnb)
    return out.reshape(bsz, t, d)


def _stack_time(ref, t_len, d):
    return jnp.concatenate([ref[:, t * d:(t + 1) * d] for t in range(t_len)], axis=0)


def _mod_rows(mod_ref, idx, t_len, d):
    m = mod_ref[:, idx * d:(idx + 1) * d]
    return jnp.concatenate([m] * t_len, axis=0)


def _ffn_sample_kernel(x_ref, mod_ref, wg_ref, wu_ref, wd_ref, lng_ref, lnb_ref, o_ref, *,
                       sub, alpha, t_len, d):
    bsz = x_ref.shape[0]
    x = _stack_time(x_ref, t_len, d)
    y = _ffn_rows(x, _mod_rows(mod_ref, 0, t_len, d), _mod_rows(mod_ref, 1, t_len, d),
                  _mod_rows(mod_ref, 2, t_len, d), wg_ref, wu_ref, wd_ref,
                  lng_ref[sub:sub + 1, :], lnb_ref[sub:sub + 1, :], alpha)
    for t in range(t_len):
        o_ref[:, t * d:(t + 1) * d] = y[t * bsz:(t + 1) * bsz]


def _sub_mod_spec(bsz, d, sub):
    return pl.BlockSpec((bsz, 3 * d), lambda *_: (0, sub), pipeline_mode=pl.Buffered(1))


def _ffn_sample(x, mod, wg, wu, wd, lng, lnb, *, sub, alpha):
    bsz, t, d = x.shape
    f = wg.shape[1]
    x2 = x.reshape(bsz, t * d)
    out = pl.pallas_call(
        functools.partial(_ffn_sample_kernel, sub=sub, alpha=alpha, t_len=t, d=d),
        grid=(1,),
        in_specs=[_const_spec((bsz, t * d)), _sub_mod_spec(bsz, d, sub),
                  _const_spec((d, f)), _const_spec((d, f)), _const_spec((f, d)),
                  _const_spec(lng.shape), _const_spec(lnb.shape)],
        out_specs=pl.BlockSpec((bsz, t * d), lambda i: (0, 0)),
        out_shape=jax.ShapeDtypeStruct((bsz, t * d), F32),
        compiler_params=pltpu.CompilerParams(dimension_semantics=("arbitrary",),
                                             vmem_limit_bytes=V7X_VMEM_LIMIT),
        name=f"ffn_sample{sub}",
    )(x2, mod, wg, wu, wd, lng, lnb)
    return out.reshape(bsz, t, d)


def _tri_inverse(lmat, c):
    m = _mm(lmat, lmat)
    q = m
    power = 2
    while 2 * power < c:
        m = _mm(m, m)
        q = q + m + _mm(q, m)
        power *= 2
    rows = lax.broadcasted_iota(jnp.int32, (c, c), 0)
    cols = lax.broadcasted_iota(jnp.int32, (c, c), 1)
    eye = (rows == cols).astype(F32)
    return eye - lmat + q - _mm(lmat, q)


def _gdn_chunk(q, k, v, beta, gcol, grow, s, tril, strict):
    c = q.shape[0]
    kb = k * beta
    a = _mm_bt(jnp.concatenate([kb, q], axis=0), k)
    decay = jnp.where(tril, jnp.exp(jnp.where(tril, gcol - grow, 0.0)), 0.0)
    lmat = jnp.where(strict, a[:c] * decay, 0.0)
    qk = jnp.where(tril, a[c:] * decay, 0.0)
    tm = _tri_inverse(lmat, c)
    eg = jnp.exp(gcol)
    uw = _mm(tm, jnp.concatenate([v * beta, kb * eg], axis=1))
    u, w = uw[:, :DN_DV], uw[:, DN_DV:]
    q_dec = q * eg
    glast = gcol[c - 1:c, :]
    k_dec = k * jnp.exp(glast - gcol)
    r = _mm(jnp.concatenate([w, q_dec], axis=0), s)
    v_new = u - r[:c]
    o = r[c:] + _mm(qk, v_new)
    s_new = s * jnp.exp(glast) + _mm_at(k_dec, v_new)
    return o, s_new


def _gate_norm(o, og, norm_g):
    o = o * lax.rsqrt(jnp.mean(o * o, axis=-1, keepdims=True) + RMS_EPS) * norm_g
    return o * _silu(og)


def _proj_layout(qkv_dim, dn_width, sc_width):
    off = {}
    pos = 0
    for name, width in (("qkv", qkv_dim), ("og", dn_width), ("sc_b", sc_width), ("sc_c", sc_width),
                        ("sc_h", sc_width), ("ab", AB_PAD)):
        off[name] = (pos, pos + width)
        pos += width
    return off


def _cols(p, off, name):
    return p[:, off[name][0]:off[name][1]]


def _mix_out(x, o_dn, o_sc, gate, w_out_ref, g, b, alpha):
    dn_w = o_dn.shape[1]
    mix = (jnp.dot(o_dn.astype(BF16), w_out_ref[0:dn_w, :], preferred_element_type=F32)
           + jnp.dot(o_sc.astype(BF16), w_out_ref[dn_w:, :], preferred_element_type=F32))
    return _post(x, mix, gate, g, b, alpha)


def _mixer_prompt_kernel(x_ref, mod_ref, w_in_ref, w_out_ref, cw_ref, mw_ref, alog_ref, dtb_ref, ng_ref,
                         lng_ref, lnb_ref,
                         y_ref, s_out_ref, cq_out_ref, cm_out_ref,
                         s_scr, xq_scr, act_scr, z_scr, gb_scr, gc_scr, gct_scr, o_scr,
                         *, alpha, tt, off, n_conv, n_mconv):
    t_idx = pl.program_id(1)
    nchunk = tt // CHUNK
    hk = DN_HEADS * DN_DK
    halo = SUBLANES
    sub = 1

    @pl.when(t_idx == 0)
    def _():
        s_scr[...] = jnp.zeros_like(s_scr)
        xq_scr[0:halo, :] = jnp.zeros((halo, xq_scr.shape[1]), F32)
        z_scr[0:halo, :] = jnp.zeros((halo, z_scr.shape[1]), F32)

    x = x_ref[...]
    shift = mod_ref[0, 3 * sub + 0:3 * sub + 1, :]
    scale = mod_ref[0, 3 * sub + 1:3 * sub + 2, :]
    gate = mod_ref[0, 3 * sub + 2:3 * sub + 3, :]
    u = (x * (1.0 + scale) + shift).astype(BF16)
    p = jnp.dot(u, w_in_ref[...], preferred_element_type=F32)

    xq_scr[halo:halo + tt, :] = _cols(p, off, "qkv")
    acc = None
    for j in range(n_conv):
        start = halo - (n_conv - 1) + j
        term = xq_scr[start:start + tt, :] * cw_ref[j:j + 1, :]
        acc = term if acc is None else acc + term
    act_scr[...] = _silu(acc)
    tail = xq_scr[halo + tt - (n_conv - 1):halo + tt, :]
    cq_out_ref[0] = tail
    xq_scr[halo - (n_conv - 1):halo, :] = tail

    ab = _cols(p, off, "ab")
    gb_scr[...] = jax.nn.sigmoid(ab)
    g = -jnp.exp(alog_ref[...]) * _softplus(ab + dtb_ref[...])
    rows = lax.broadcasted_iota(jnp.int32, (CHUNK, CHUNK), 0)
    cols = lax.broadcasted_iota(jnp.int32, (CHUNK, CHUNK), 1)
    tril = rows >= cols
    strict = rows > cols
    tril_f = tril.astype(F32)
    for c in range(nchunk):
        gc_scr[c * CHUNK:(c + 1) * CHUNK, :] = _mm_f32(tril_f, g[c * CHUNK:(c + 1) * CHUNK, :])
    gc_t = gc_scr[...].T
    for c in range(nchunk):
        gct_scr[c] = gc_t[0:SUBLANES, c * CHUNK:(c + 1) * CHUNK]

    def chunk_body(c, carry):
        r0 = pl.multiple_of(c * CHUNK, CHUNK)
        for h in range(DN_HEADS):
            q = act_scr[pl.ds(r0, CHUNK), h * DN_DK:(h + 1) * DN_DK]
            k = act_scr[pl.ds(r0, CHUNK), hk + h * DN_DK:hk + (h + 1) * DN_DK]
            v = act_scr[pl.ds(r0, CHUNK), 2 * hk + h * DN_DV:2 * hk + (h + 1) * DN_DV]
            q = _l2norm(q) * (DN_DK ** -0.5)
            k = _l2norm(k)
            beta = gb_scr[pl.ds(r0, CHUNK), DN_HEADS + h:DN_HEADS + h + 1]
            gcol = gc_scr[pl.ds(r0, CHUNK), h:h + 1]
            grow = gct_scr[c, h:h + 1, :]
            o, s_new = _gdn_chunk(q, k, v, beta, gcol, grow, s_scr[h], tril, strict)
            s_scr[h] = s_new
            o_scr[pl.ds(r0, CHUNK), h * DN_DV:(h + 1) * DN_DV] = o
        return carry

    lax.fori_loop(0, nchunk, chunk_body, 0)
    s_out_ref[0] = s_scr[...]

    og = _cols(p, off, "og")
    o_dn = jnp.concatenate(
        [_gate_norm(o_scr[:, h * DN_DV:(h + 1) * DN_DV], og[:, h * DN_DV:(h + 1) * DN_DV], ng_ref[...])
         for h in range(DN_HEADS)], axis=1)

    z_scr[halo:halo + tt, :] = _cols(p, off, "sc_c") * _cols(p, off, "sc_h")
    zc = None
    for j in range(n_mconv):
        start = halo - (n_mconv - 1) + j
        term = z_scr[start:start + tt, :] * mw_ref[j:j + 1, :]
        zc = term if zc is None else zc + term
    ztail = z_scr[halo + tt - (n_mconv - 1):halo + tt, :]
    cm_out_ref[0] = ztail
    z_scr[halo - (n_mconv - 1):halo, :] = ztail
    o_sc = _cols(p, off, "sc_b") * zc

    y_ref[...] = _mix_out(x, o_dn, o_sc, gate, w_out_ref,
                          lng_ref[sub:sub + 1, :], lnb_ref[sub:sub + 1, :], alpha)


def _mixer_prompt(x, mod, w_in, w_out, conv_w, mconv_w, alog, dtb, norm_g, lng, lnb, *, alpha, tt, off):
    bsz, t, d = x.shape
    assert t % tt == 0 and tt % LANES == 0
    nt = t // tt
    proj = w_in.shape[1]
    qkv_dim = off["qkv"][1] - off["qkv"][0]
    sc_w = off["sc_c"][1] - off["sc_c"][0]
    dn_w = DN_HEADS * DN_DV
    n_conv, n_mconv = conv_w.shape[0], mconv_w.shape[0]
    x2 = x.reshape(bsz * t, d)
    kern = functools.partial(_mixer_prompt_kernel, alpha=alpha, tt=tt, off=off,
                             n_conv=n_conv, n_mconv=n_mconv)
    y, s_out, cq, cm = pl.pallas_call(
        kern,
        grid=(bsz, nt),
        in_specs=[pl.BlockSpec((tt, d), lambda b, i: (b * nt + i, 0)),
                  pl.BlockSpec((1, 3 * N_SUB, d), lambda b, i: (b, 0, 0)),
                  _const_spec((d, proj)), _const_spec(w_out.shape),
                  _const_spec(conv_w.shape), _const_spec(mconv_w.shape),
                  _const_spec(alog.shape), _const_spec(dtb.shape), _const_spec(norm_g.shape),
                  _const_spec(lng.shape), _const_spec(lnb.shape)],
        out_specs=[pl.BlockSpec((tt, d), lambda b, i: (b * nt + i, 0)),
                   pl.BlockSpec((1, DN_HEADS, DN_DK, DN_DV), lambda b, i: (b, 0, 0, 0)),
                   pl.BlockSpec((1, n_conv - 1, qkv_dim), lambda b, i: (b, 0, 0)),
                   pl.BlockSpec((1, n_mconv - 1, sc_w), lambda b, i: (b, 0, 0))],
        out_shape=[jax.ShapeDtypeStruct((bsz * t, d), F32),
                   jax.ShapeDtypeStruct((bsz, DN_HEADS, DN_DK, DN_DV), F32),
                   jax.ShapeDtypeStruct((bsz, n_conv - 1, qkv_dim), F32),
                   jax.ShapeDtypeStruct((bsz, n_mconv - 1, sc_w), F32)],
        scratch_shapes=[pltpu.VMEM((DN_HEADS, DN_DK, DN_DV), F32),
                        pltpu.VMEM((tt + SUBLANES, qkv_dim), F32),
                        pltpu.VMEM((tt, qkv_dim), F32),
                        pltpu.VMEM((tt + SUBLANES, sc_w), F32),
                        pltpu.VMEM((tt, AB_PAD), F32),
                        pltpu.VMEM((tt, AB_PAD), F32),
                        pltpu.VMEM((tt // CHUNK, SUBLANES, CHUNK), F32),
                        pltpu.VMEM((tt, dn_w), F32)],
        compiler_params=pltpu.CompilerParams(dimension_semantics=("arbitrary", "arbitrary"),
                                             vmem_limit_bytes=V7X_VMEM_LIMIT),
        name="mixer_prompt",
    )(x2, mod, w_in, w_out, conv_w, mconv_w, alog, dtb, norm_g, lng, lnb)
    return y.reshape(bsz, t, d), s_out, cq, cm


_PACK = SUBLANES


def _sample_phase_a(x_ref, mod_ref, w_in_ref, cw_ref, mw_ref, alog_ref, dtb_ref, cq_in_ref, cm_in_ref,
                    cq_out_ref, cm_out_ref, og_scr, osc_scr, lhs_scr, u_scr, kd_scr, gl_scr, qk_scr,
                    *, t_len, d, off, n_conv, n_mconv):
    bsz = x_ref.shape[0]
    hk = DN_HEADS * DN_DK
    qkv_dim = off["qkv"][1] - off["qkv"][0]
    sc_w = off["sc_c"][1] - off["sc_c"][0]

    x = _stack_time(x_ref, t_len, d)
    u = (x * (1.0 + _mod_rows(mod_ref, 1, t_len, d)) + _mod_rows(mod_ref, 0, t_len, d)).astype(BF16)
    p = jnp.dot(u, w_in_ref[...], preferred_element_type=F32)

    def rows(arr, t):
        return arr[t * bsz:(t + 1) * bsz]

    pq = _cols(p, off, "qkv")
    seq = ([cq_in_ref[:, j * qkv_dim:(j + 1) * qkv_dim] for j in range(n_conv - 1)]
           + [rows(pq, t) for t in range(t_len)])
    act = []
    for t in range(t_len):
        acc = None
        for j in range(n_conv):
            term = seq[t + j] * cw_ref[j:j + 1, :]
            acc = term if acc is None else acc + term
        act.append(_silu(acc))
    for j in range(n_conv - 1):
        cq_out_ref[:, j * qkv_dim:(j + 1) * qkv_dim] = seq[t_len + j]

    z = _cols(p, off, "sc_c") * _cols(p, off, "sc_h")
    zseq = ([cm_in_ref[:, j * sc_w:(j + 1) * sc_w] for j in range(n_mconv - 1)]
            + [rows(z, t) for t in range(t_len)])
    zc = []
    for t in range(t_len):
        acc = None
        for j in range(n_mconv):
            term = zseq[t + j] * mw_ref[j:j + 1, :]
            acc = term if acc is None else acc + term
        zc.append(acc)
    for j in range(n_mconv - 1):
        cm_out_ref[:, j * sc_w:(j + 1) * sc_w] = zseq[t_len + j]
    osc_scr[...] = _cols(p, off, "sc_b") * jnp.concatenate(zc, axis=0)
    og_scr[...] = _cols(p, off, "og")

    ab = _cols(p, off, "ab")
    beta_all = jax.nn.sigmoid(ab)
    g_all = -jnp.exp(alog_ref[...]) * _softplus(ab + dtb_ref[...])
    zero = jnp.zeros((bsz, LANES), F32)

    for h in range(DN_HEADS):
        q = [_l2norm(a[:, h * DN_DK:(h + 1) * DN_DK]) * (DN_DK ** -0.5) for a in act]
        k = [_l2norm(a[:, hk + h * DN_DK:hk + (h + 1) * DN_DK]) for a in act]
        v = [a[:, 2 * hk + h * DN_DV:2 * hk + (h + 1) * DN_DV] for a in act]
        beta = [rows(beta_all, t)[:, DN_HEADS + h:DN_HEADS + h + 1] for t in range(t_len)]
        g = [rows(g_all, t)[:, h:h + 1] for t in range(t_len)]
        gc = [g[0]]
        for t in range(1, t_len):
            gc.append(gc[-1] + g[t])
        kb = [k[t] * beta[t] for t in range(t_len)]
        vb = [v[t] * beta[t] for t in range(t_len)]
        eg = [jnp.exp(gc[t]) for t in range(t_len)]
        kbe = [kb[t] * eg[t] for t in range(t_len)]
        lm = [[None] * t_len for _ in range(t_len)]
        n_qk = 0
        for i in range(t_len):
            for j in range(i + 1):
                dec = jnp.exp(gc[i] - gc[j])
                qk_scr[h, n_qk] = jnp.broadcast_to(_rowsum(q[i] * k[j]) * dec, (bsz, LANES))
                n_qk += 1
                if j < i:
                    lm[i][j] = _rowsum(kb[i] * k[j]) * dec
        tm = [[None] * t_len for _ in range(t_len)]
        for i in range(t_len):
            for j in range(i):
                acc = lm[i][j]
                for m in range(j + 1, i):
                    acc = acc + lm[i][m] * tm[m][j]
                tm[i][j] = -acc
        for i in range(t_len):
            ui, wi = vb[i], kbe[i]
            for j in range(i):
                ui = ui + tm[i][j] * vb[j]
                wi = wi + tm[i][j] * kbe[j]
            lhs_scr[h, pl.ds(i, bsz, stride=_PACK), :] = wi
            lhs_scr[h, pl.ds(t_len + i, bsz, stride=_PACK), :] = q[i] * eg[i]
            u_scr[h, pl.ds(i, bsz, stride=_PACK), :] = ui
            u_scr[h, pl.ds(t_len + i, bsz, stride=_PACK), :] = zero
            kd_scr[h, pl.ds(i, bsz, stride=_PACK), :] = k[i] * jnp.exp(gc[t_len - 1] - gc[i])
            kd_scr[h, pl.ds(t_len + i, bsz, stride=_PACK), :] = zero
        gl_scr[h] = jnp.broadcast_to(jnp.exp(gc[t_len - 1]), (bsz, LANES))


def _sample_phase_c(x_ref, mod_ref, w_out_ref, ng_ref, lng_ref, lnb_ref, y_ref,
                    og_scr, osc_scr, res_scr, vn_scr, qk_scr, *, alpha, t_len, d):
    bsz = x_ref.shape[0]
    sub = 1
    x = _stack_time(x_ref, t_len, d)
    gate = _mod_rows(mod_ref, 2, t_len, d)
    og = og_scr[...]
    per_t = []
    for i in range(t_len):
        heads = []
        for h in range(DN_HEADS):
            oi = res_scr[h, pl.ds(t_len + i, bsz, stride=_PACK), :]
            base = i * (i + 1) // 2
            for j in range(i + 1):
                oi = oi + qk_scr[h, base + j] * vn_scr[h, pl.ds(j, bsz, stride=_PACK), :]
            heads.append(_gate_norm(oi, og[i * bsz:(i + 1) * bsz, h * DN_DV:(h + 1) * DN_DV], ng_ref[...]))
        per_t.append(jnp.concatenate(heads, axis=1))
    o_dn = jnp.concatenate(per_t, axis=0)
    y = _mix_out(x, o_dn, osc_scr[...], gate, w_out_ref,
                 lng_ref[sub:sub + 1, :], lnb_ref[sub:sub + 1, :], alpha)
    for t in range(t_len):
        y_ref[:, t * d:(t + 1) * d] = y[t * bsz:(t + 1) * bsz]


def _mixer_sample_kernel(x_ref, mod_ref, w_in_ref, w_out_ref, cw_ref, mw_ref, alog_ref, dtb_ref, ng_ref,
                         lng_ref, lnb_ref, cq_in_ref, cm_in_ref, s_in_ref,
                         y_ref, cq_out_ref, cm_out_ref, s_out_ref,
                         og_scr, osc_scr, lhs_scr, u_scr, kd_scr, gl_scr, qk_scr, res_scr, vn_scr,
                         *, alpha, t_len, d, off, n_conv, n_mconv, bt):
    step = pl.program_id(0)
    assert 2 * t_len == _PACK, "packed buffers hold [T rows | T rows] per sequence"

    @pl.when(step == 0)
    def _():
        _sample_phase_a(x_ref, mod_ref, w_in_ref, cw_ref, mw_ref, alog_ref, dtb_ref, cq_in_ref, cm_in_ref,
                        cq_out_ref, cm_out_ref, og_scr, osc_scr, lhs_scr, u_scr, kd_scr, gl_scr, qk_scr,
                        t_len=t_len, d=d, off=off, n_conv=n_conv, n_mconv=n_mconv)

    top = (lax.broadcasted_iota(jnp.int32, (_PACK, LANES), 0) < t_len).astype(F32)

    def seq_body(bl, carry):
        b = step * bt + bl
        r0 = pl.multiple_of(b * _PACK, _PACK)
        for h in range(DN_HEADS):
            s = s_in_ref[bl, h]
            r = _mm(lhs_scr[h, pl.ds(r0, _PACK), :], s)
            res_scr[h, pl.ds(r0, _PACK), :] = r
            vn = (u_scr[h, pl.ds(r0, _PACK), :] - r) * top
            vn_scr[h, pl.ds(r0, _PACK), :] = vn
            s_out_ref[bl, h] = (s * gl_scr[h, pl.ds(b, 1), :]
                                + _mm_at(kd_scr[h, pl.ds(r0, _PACK), :], vn))
        return carry

    lax.fori_loop(0, bt, seq_body, 0)

    @pl.when(step == pl.num_programs(0) - 1)
    def _():
        _sample_phase_c(x_ref, mod_ref, w_out_ref, ng_ref, lng_ref, lnb_ref, y_ref,
                        og_scr, osc_scr, res_scr, vn_scr, qk_scr, alpha=alpha, t_len=t_len, d=d)


def _mixer_sample(x, mod, w_in, w_out, conv_w, mconv_w, alog, dtb, norm_g, lng, lnb, s_in, cq_in, cm_in,
                  *, alpha, off, bt):
    bsz, t, d = x.shape
    assert bsz % bt == 0
    qkv_dim = off["qkv"][1] - off["qkv"][0]
    sc_w = off["sc_c"][1] - off["sc_c"][0]
    dn_w = DN_HEADS * DN_DV
    n_conv, n_mconv = conv_w.shape[0], mconv_w.shape[0]
    n_qk = t * (t + 1) // 2
    x2 = x.reshape(bsz, t * d)
    cq2 = cq_in.reshape(bsz, (n_conv - 1) * qkv_dim)
    cm2 = cm_in.reshape(bsz, (n_mconv - 1) * sc_w)
    kern = functools.partial(_mixer_sample_kernel, alpha=alpha, t_len=t, d=d, off=off,
                             n_conv=n_conv, n_mconv=n_mconv, bt=bt)
    consts = (w_in, w_out, conv_w, mconv_w, alog, dtb, norm_g, lng, lnb, cq2, cm2)
    state_spec = pl.BlockSpec((bt, DN_HEADS, DN_DK, DN_DV), lambda i: (i, 0, 0, 0))
    packed = pltpu.VMEM((DN_HEADS, _PACK * bsz, LANES), F32)
    y, cq, cm, s_out = pl.pallas_call(
        kern,
        grid=(bsz // bt,),
        in_specs=([_const_spec(x2.shape), _sub_mod_spec(bsz, d, 1)]
                  + [_const_spec(a.shape) for a in consts] + [state_spec]),
        out_specs=[pl.BlockSpec(x2.shape, lambda i: (0, 0)),
                   pl.BlockSpec(cq2.shape, lambda i: (0, 0)),
                   pl.BlockSpec(cm2.shape, lambda i: (0, 0)),
                   state_spec],
        out_shape=[jax.ShapeDtypeStruct(x2.shape, F32),
                   jax.ShapeDtypeStruct(cq2.shape, F32),
                   jax.ShapeDtypeStruct(cm2.shape, F32),
                   jax.ShapeDtypeStruct(s_in.shape, F32)],
        scratch_shapes=[pltpu.VMEM((t * bsz, dn_w), F32),
                        pltpu.VMEM((t * bsz, sc_w), F32),
                        packed,
                        packed,
                        packed,
                        pltpu.VMEM((DN_HEADS, bsz, LANES), F32),
                        pltpu.VMEM((DN_HEADS, n_qk, bsz, LANES), F32),
                        packed,
                        packed],
        compiler_params=pltpu.CompilerParams(dimension_semantics=("arbitrary",),
                                             vmem_limit_bytes=V7X_VMEM_LIMIT),
        name="mixer_sample",
    )(x2, mod, *consts, s_in)
    return (y.reshape(bsz, t, d), s_out, cq.reshape(bsz, n_conv - 1, qkv_dim),
            cm.reshape(bsz, n_mconv - 1, sc_w))


def _pack_w_in(w_in, qkv_dim, dn_width, sc_width):
    c0 = qkv_dim
    a = w_in[:, c0:c0 + DN_HEADS]
    b = w_in[:, c0 + DN_HEADS:c0 + 2 * DN_HEADS]
    c1 = c0 + 2 * DN_HEADS
    og = w_in[:, c1:c1 + dn_width]
    sc_b = w_in[:, c1 + dn_width:c1 + dn_width + sc_width]
    sc_c = w_in[:, c1 + dn_width + sc_width:c1 + dn_width + 2 * sc_width]
    sc_h = w_in[:, c1 + dn_width + 2 * sc_width:c1 + dn_width + 3 * sc_width]
    pad = jnp.zeros((w_in.shape[0], AB_PAD - 2 * DN_HEADS), w_in.dtype)
    return jnp.concatenate([w_in[:, :qkv_dim], og, sc_b, sc_c, sc_h, a, b, pad], axis=1).astype(BF16)


def _pad_lanes(v):
    return jnp.zeros((1, AB_PAD), F32).at[0, :v.shape[0]].set(v)


def kernel(x_prompt, x_sample, state_ssm, state_conv_qkv, state_conv_mix, c_prompt, c_sample, w_ada, b_ada, ln_g, ln_b, ffn1_wg, ffn1_wu, ffn1_wd, ffn2_wg, ffn2_wu, ffn2_wd, w_in, conv_qkv_w, a_log, dt_bias, dn_norm_g, conv_mix_w, w_out):
    depth = w_ada.shape[0]
    alpha = (2 * depth) ** 0.25
    bp, tp, d = x_prompt.shape
    qkv_dim = conv_qkv_w.shape[-1]
    sc_width = conv_mix_w.shape[-1]
    dn_width = DN_HEADS * DN_DV
    off = _proj_layout(qkv_dim, dn_width, sc_width)
    tm = min(512, tp)
    tt = min(256, tp)
    bt = min(8, x_sample.shape[0])

    hp, hs = x_prompt, x_sample
    outs = [[] for _ in range(6)]
    for l in range(depth):
        c_all = jnp.concatenate([c_prompt, c_sample], axis=0)
        mod = _adaln(c_all, w_ada[l], b_ada[l].reshape(1, -1), tn=d)
        mod_p = mod[:bp].reshape(bp, 3 * N_SUB, d)
        mod_s = mod[bp:]
        lng, lnb = ln_g[l], ln_b[l]
        w1 = (ffn1_wg[l].astype(BF16), ffn1_wu[l].astype(BF16), ffn1_wd[l].astype(BF16))
        w2 = (ffn2_wg[l].astype(BF16), ffn2_wu[l].astype(BF16), ffn2_wd[l].astype(BF16))
        w_in_p = _pack_w_in(w_in[l], qkv_dim, dn_width, sc_width)
        w_out_b = w_out[l].astype(BF16)
        alog, dtb = _pad_lanes(a_log[l]), _pad_lanes(dt_bias[l])
        norm_g = dn_norm_g[l].reshape(1, -1)
        mix_args = (w_in_p, w_out_b, conv_qkv_w[l], conv_mix_w[l], alog, dtb, norm_g, lng, lnb)

        hp = _ffn_prompt(hp, mod_p, *w1, lng, lnb, sub=0, alpha=alpha, tm=tm)
        hp, a1, a2, a3 = _mixer_prompt(hp, mod_p, *mix_args, alpha=alpha, tt=tt, off=off)
        hp = _ffn_prompt(hp, mod_p, *w2, lng, lnb, sub=2, alpha=alpha, tm=tm)

        hs = _ffn_sample(hs, mod_s, *w1, lng, lnb, sub=0, alpha=alpha)
        hs, b1, b2, b3 = _mixer_sample(hs, mod_s, *mix_args, state_ssm[l], state_conv_qkv[l],
                                       state_conv_mix[l], alpha=alpha, off=off, bt=bt)
        hs = _ffn_sample(hs, mod_s, *w2, lng, lnb, sub=2, alpha=alpha)
        for lst, val in zip(outs, (a1, a2, a3, b1, b2, b3)):
            lst.append(val)
    return (hp, hs) + tuple(jnp.stack(o) for o in outs)
```

```python
import functools

import jax
import jax.numpy as jnp
from jax import lax
from jax.experimental import pallas as pl
from jax.experimental.pallas import tpu as pltpu

F32 = jnp.float32
BF16 = jnp.bfloat16

LN_EPS = 1e-5
RMS_EPS = 1e-6
N_SUB = 3
DN_HEADS = 4
DN_DK = 128
DN_DV = 128
CHUNK = 64
LANES = 128
SUBLANES = 8
AB_PAD = LANES

V7X_VMEM_LIMIT = 60000 * 1024


def _mm(a, b):
    return jnp.dot(a.astype(BF16), b.astype(BF16), preferred_element_type=F32)


def _mm_bt(a, b):
    return lax.dot_general(a.astype(BF16), b.astype(BF16), (((1,), (1,)), ((), ())),
                           preferred_element_type=F32)


def _mm_at(a, b):
    return lax.dot_general(a.astype(BF16), b.astype(BF16), (((0,), (0,)), ((), ())),
                           preferred_element_type=F32)


def _mm_f32(a, b):
    return jnp.dot(a, b, preferred_element_type=F32, precision=lax.Precision.HIGHEST)


def _silu(x):
    return x * jax.nn.sigmoid(x)


def _softplus(x):
    return jnp.maximum(x, 0.0) + jnp.log1p(jnp.exp(-jnp.abs(x)))


def _layer_norm(y, g, b):
    mu = jnp.mean(y, axis=-1, keepdims=True)
    yc = y - mu
    var = jnp.mean(yc * yc, axis=-1, keepdims=True)
    return yc * lax.rsqrt(var + LN_EPS) * g + b


def _post(x, delta, gate, g, b, alpha):
    return _layer_norm(alpha * x + gate * delta, g, b)


def _rowsum(x):
    return jnp.sum(x, axis=-1, keepdims=True)


def _l2norm(x):
    return x * lax.rsqrt(_rowsum(x * x) + RMS_EPS)


def _const_spec(shape):
    nd = len(shape)
    return pl.BlockSpec(shape, lambda *_: (0,) * nd, pipeline_mode=pl.Buffered(1))


def _adaln_kernel(c_ref, w_ref, b_ref, o_ref):
    s = _silu(c_ref[...])
    o_ref[...] = _mm(s, w_ref[...]) + b_ref[...]


def _adaln(c, w, b, tn=1024):
    m, d = c.shape
    n = w.shape[1]
    assert n % tn == 0
    return pl.pallas_call(
        _adaln_kernel,
        grid=(n // tn,),
        in_specs=[_const_spec((m, d)),
                  pl.BlockSpec((d, tn), lambda j: (0, j)),
                  pl.BlockSpec((1, tn), lambda j: (0, j))],
        out_specs=pl.BlockSpec((m, tn), lambda j: (0, j)),
        out_shape=jax.ShapeDtypeStruct((m, n), F32),
        compiler_params=pltpu.CompilerParams(dimension_semantics=("arbitrary",)),
        name="adaln",
    )(c, w, b)


def _ffn_rows(x, shift, scale, gate, wg_ref, wu_ref, wd_ref, g, b, alpha):
    u = (x * (1.0 + scale) + shift).astype(BF16)
    hg = jnp.dot(u, wg_ref[...], preferred_element_type=F32)
    hu = jnp.dot(u, wu_ref[...], preferred_element_type=F32)
    h = (_silu(hg) * hu).astype(BF16)
    d = jnp.dot(h, wd_ref[...], preferred_element_type=F32)
    return _post(x, 0.5 * d, gate, g, b, alpha)


def _ffn_prompt_kernel(x_ref, mod_ref, wg_ref, wu_ref, wd_ref, lng_ref, lnb_ref, o_ref, *, sub, alpha):
    shift = mod_ref[0, 3 * sub + 0:3 * sub + 1, :]
    scale = mod_ref[0, 3 * sub + 1:3 * sub + 2, :]
    gate = mod_ref[0, 3 * sub + 2:3 * sub + 3, :]
    o_ref[...] = _ffn_rows(x_ref[...], shift, scale, gate, wg_ref, wu_ref, wd_ref,
                           lng_ref[sub:sub + 1, :], lnb_ref[sub:sub + 1, :], alpha)


def _ffn_prompt(x, mod, wg, wu, wd, lng, lnb, *, sub, alpha, tm):
    bsz, t, d = x.shape
    f = wg.shape[1]
    assert t % tm == 0
    per_seq = t // tm
    x2 = x.reshape(bsz * t, d)
    out = pl.pallas_call(
        functools.partial(_ffn_prompt_kernel, sub=sub, alpha=alpha),
        grid=(bsz * per_seq,),
        in_specs=[pl.BlockSpec((tm, d), lambda i: (i, 0)),
                  pl.BlockSpec((1, 3 * N_SUB, d), lambda i: (i // per_seq, 0, 0)),
                  _const_spec((d, f)), _const_spec((d, f)), _const_spec((f, d)),
                  _const_spec(lng.shape), _const_spec(lnb.shape)],
        out_specs=pl.BlockSpec((tm, d), lambda i: (i, 0)),
        out_shape=jax.ShapeDtypeStruct((bsz * t, d), F32),
        compiler_params=pltpu.CompilerParams(dimension_semantics=("arbitrary",),
                                             vmem_limit_bytes=V7X_VMEM_LIMIT),
        name=f"ffn_prompt{sub}",
    )(x2, mod, wg, wu, wd, lng, lnb)
    return out.reshape(bsz, t, d)


def _stack_time(ref, t_len, d):
    return jnp.concatenate([ref[:, t * d:(t + 1) * d] for t in range(t_len)], axis=0)


def _mod_rows(mod_ref, idx, t_len, d):
    m = mod_ref[:, idx * d:(idx + 1) * d]
    return jnp.concatenate([m] * t_len, axis=0)


def _ffn_sample_kernel(x_ref, mod_ref, wg_ref, wu_ref, wd_ref, lng_ref, lnb_ref, o_ref, *,
                       sub, alpha, t_len, d):
    bsz = x_ref.shape[0]
    x = _stack_time(x_ref, t_len, d)
    y = _ffn_rows(x, _mod_rows(mod_ref, 0, t_len, d), _mod_rows(mod_ref, 1, t_len, d),
                  _mod_rows(mod_ref, 2, t_len, d), wg_ref, wu_ref, wd_ref,
                  lng_ref[sub:sub + 1, :], lnb_ref[sub:sub + 1, :], alpha)
    for t in range(t_len):
        o_ref[:, t * d:(t + 1) * d] = y[t * bsz:(t + 1) * bsz]


def _sub_mod_spec(bsz, d, sub):
    return pl.BlockSpec((bsz, 3 * d), lambda *_: (0, sub), pipeline_mode=pl.Buffered(1))


def _ffn_sample(x, mod, wg, wu, wd, lng, lnb, *, sub, alpha):
    bsz, t, d = x.shape
    f = wg.shape[1]
    x2 = x.reshape(bsz, t * d)
    out = pl.pallas_call(
        functools.partial(_ffn_sample_kernel, sub=sub, alpha=alpha, t_len=t, d=d),
        grid=(1,),
        in_specs=[_const_spec((bsz, t * d)), _sub_mod_spec(bsz, d, sub),
                  _const_spec((d, f)), _const_spec((d, f)), _const_spec((f, d)),
                  _const_spec(lng.shape), _const_spec(lnb.shape)],
        out_specs=pl.BlockSpec((bsz, t * d), lambda i: (0, 0)),
        out_shape=jax.ShapeDtypeStruct((bsz, t * d), F32),
        compiler_params=pltpu.CompilerParams(dimension_semantics=("arbitrary",),
                                             vmem_limit_bytes=V7X_VMEM_LIMIT),
        name=f"ffn_sample{sub}",
    )(x2, mod, wg, wu, wd, lng, lnb)
    return out.reshape(bsz, t, d)


def _tri_inverse(lmats, c):
    ls = [l.astype(BF16) for l in lmats]
    ms = [jnp.dot(l, l, preferred_element_type=F32) for l in ls]
    qs = ms
    power = 2
    while 2 * power < c:
        mb = [m.astype(BF16) for m in ms]
        ms = [jnp.dot(m, m, preferred_element_type=F32) for m in mb]
        qm = [jnp.dot(q.astype(BF16), m.astype(BF16), preferred_element_type=F32) for q, m in zip(qs, ms)]
        qs = [q + m + x for q, m, x in zip(qs, ms, qm)]
        power *= 2
    rows = lax.broadcasted_iota(jnp.int32, (c, c), 0)
    cols = lax.broadcasted_iota(jnp.int32, (c, c), 1)
    eye = (rows == cols).astype(F32)
    lq = [jnp.dot(l, q.astype(BF16), preferred_element_type=F32) for l, q in zip(ls, qs)]
    return [eye - l + q - x for l, q, x in zip(lmats, qs, lq)]


def _gdn_chunks_local(units, tril, strict):
    c = units[0][1].shape[0]
    a = [lax.dot_general(jnp.concatenate([kb, q], axis=0), k, (((1,), (1,)), ((), ())),
                         preferred_element_type=F32)
         for kb, q, k, _, _, _, _ in units]
    lmats, qks = [], []
    for a_u, (_, _, _, _, _, gcol, grow) in zip(a, units):
        decay = jnp.where(tril, jnp.exp(jnp.where(tril, gcol - grow, 0.0)), 0.0)
        lmats.append(jnp.where(strict, a_u[:c] * decay, 0.0))
        qks.append(jnp.where(tril, a_u[c:] * decay, 0.0))
    tms = _tri_inverse(lmats, c)
    uw = [jnp.dot(tm.astype(BF16), jnp.concatenate([vb, kbe], axis=1), preferred_element_type=F32)
          for tm, (_, _, _, vb, kbe, _, _) in zip(tms, units)]
    return [(x[:, :DN_DV], x[:, DN_DV:], qk) for x, qk in zip(uw, qks)]


def _gdn_chunk_state(units, states):
    c = units[0][2].shape[0]
    r = [jnp.dot(jnp.concatenate([w, q_dec], axis=0), s.astype(BF16), preferred_element_type=F32)
         for (w, q_dec, _, _, _, _), s in zip(units, states)]
    v_new = [(u - r_h[:c]).astype(BF16) for (_, _, u, _, _, _), r_h in zip(units, r)]
    o = [r_h[c:] + jnp.dot(qk, v, preferred_element_type=F32)
         for (_, _, _, qk, _, _), r_h, v in zip(units, r, v_new)]
    s_new = [s * g_last + lax.dot_general(k_dec, v, (((0,), (0,)), ((), ())), preferred_element_type=F32)
             for (_, _, _, _, k_dec, g_last), s, v in zip(units, states, v_new)]
    return o, s_new


def _gate_norm(o, og, norm_g):
    o = o * lax.rsqrt(jnp.mean(o * o, axis=-1, keepdims=True) + RMS_EPS) * norm_g
    return o * _silu(og)


def _proj_layout(qkv_dim, dn_width, sc_width):
    off = {}
    pos = 0
    for name, width in (("qkv", qkv_dim), ("og", dn_width), ("sc_b", sc_width), ("sc_c", sc_width),
                        ("sc_h", sc_width), ("ab", AB_PAD)):
        off[name] = (pos, pos + width)
        pos += width
    return off


def _cols(p, off, name):
    return p[:, off[name][0]:off[name][1]]


def _mix_out(x, o_dn, o_sc, gate, w_out_ref, g, b, alpha):
    dn_w = o_dn.shape[1]
    mix = (jnp.dot(o_dn.astype(BF16), w_out_ref[0:dn_w, :], preferred_element_type=F32)
           + jnp.dot(o_sc.astype(BF16), w_out_ref[dn_w:, :], preferred_element_type=F32))
    return _post(x, mix, gate, g, b, alpha)


def _mixer_prompt_kernel(x_ref, mod_ref, w_in_ref, w_out_ref, cw_ref, mw_ref, alog_ref, dtb_ref, ng_ref,
                         lng_ref, lnb_ref,
                         y_ref, s_out_ref, cq_out_ref, cm_out_ref,
                         s_scr, xq_scr, z_scr, gc_scr, gct_scr, gl_scr, ops_scr, u_scr, w_scr, qk_scr, o_scr,
                         *, alpha, tt, off, n_conv, n_mconv, chunks_per_iter):
    t_idx = pl.program_id(1)
    nchunk = tt // CHUNK
    hk = DN_HEADS * DN_DK
    halo = SUBLANES
    sub = 1

    @pl.when(t_idx == 0)
    def _():
        s_scr[...] = jnp.zeros_like(s_scr)
        xq_scr[0:halo, :] = jnp.zeros((halo, xq_scr.shape[1]), F32)
        z_scr[0:halo, :] = jnp.zeros((halo, z_scr.shape[1]), F32)

    x = x_ref[...]
    shift = mod_ref[0, 3 * sub + 0:3 * sub + 1, :]
    scale = mod_ref[0, 3 * sub + 1:3 * sub + 2, :]
    gate = mod_ref[0, 3 * sub + 2:3 * sub + 3, :]
    u = (x * (1.0 + scale) + shift).astype(BF16)
    p = jnp.dot(u, w_in_ref[...], preferred_element_type=F32)

    xq_scr[halo:halo + tt, :] = _cols(p, off, "qkv")
    acc = None
    for j in range(n_conv):
        start = halo - (n_conv - 1) + j
        term = xq_scr[start:start + tt, :] * cw_ref[j:j + 1, :]
        acc = term if acc is None else acc + term
    act = _silu(acc)
    tail = xq_scr[halo + tt - (n_conv - 1):halo + tt, :]
    cq_out_ref[0] = tail
    xq_scr[halo - (n_conv - 1):halo, :] = tail

    ab = _cols(p, off, "ab")
    beta_all = jax.nn.sigmoid(ab)
    g = -jnp.exp(alog_ref[...]) * _softplus(ab + dtb_ref[...])
    rows = lax.broadcasted_iota(jnp.int32, (CHUNK, CHUNK), 0)
    cols = lax.broadcasted_iota(jnp.int32, (CHUNK, CHUNK), 1)
    tril = rows >= cols
    strict = rows > cols
    tril_f = tril.astype(F32)
    gc_parts, gl_parts = [], []
    for c in range(nchunk):
        gc_c = _mm_f32(tril_f, g[c * CHUNK:(c + 1) * CHUNK, :])
        gc_parts.append(gc_c)
        gl_parts.append(jnp.broadcast_to(gc_c[CHUNK - 1:CHUNK, :], (CHUNK, AB_PAD)))
    gc = jnp.concatenate(gc_parts, axis=0)
    gl_rows = jnp.concatenate(gl_parts, axis=0)
    gc_scr[...] = gc
    gc_t = gc.T
    for c in range(nchunk):
        gct_scr[c] = gc_t[0:SUBLANES, c * CHUNK:(c + 1) * CHUNK]

    eg_all = jnp.exp(gc)
    ekd_all = jnp.exp(gl_rows - gc)
    egl_all = jnp.exp(gl_rows)
    for h in range(DN_HEADS):
        lanes = slice(h * DN_DK, (h + 1) * DN_DK)
        q = _l2norm(act[:, h * DN_DK:(h + 1) * DN_DK]) * (DN_DK ** -0.5)
        k = _l2norm(act[:, hk + h * DN_DK:hk + (h + 1) * DN_DK])
        v = act[:, 2 * hk + h * DN_DV:2 * hk + (h + 1) * DN_DV]
        beta = beta_all[:, DN_HEADS + h:DN_HEADS + h + 1]
        eg = eg_all[:, h:h + 1]
        kb = k * beta
        ops_scr[0, :, lanes] = kb.astype(BF16)
        ops_scr[1, :, lanes] = q.astype(BF16)
        ops_scr[2, :, lanes] = k.astype(BF16)
        ops_scr[3, :, lanes] = (v * beta).astype(BF16)
        ops_scr[4, :, lanes] = (kb * eg).astype(BF16)
        ops_scr[5, :, lanes] = (q * eg).astype(BF16)
        ops_scr[6, :, lanes] = (k * ekd_all[:, h:h + 1]).astype(BF16)
        gl_scr[:, lanes] = jnp.broadcast_to(egl_all[:, h:h + 1], (tt, DN_DV))

    head_lanes = [slice(h * DN_DK, (h + 1) * DN_DK) for h in range(DN_HEADS)]

    def local_body(it, carry):
        where = []
        units = []
        for ci in range(chunks_per_iter):
            c = it * chunks_per_iter + ci
            r0 = pl.multiple_of(c * CHUNK, CHUNK)
            for h, lanes in enumerate(head_lanes):
                where.append((c, r0, h, lanes))
                units.append(tuple(ops_scr[i, pl.ds(r0, CHUNK), lanes] for i in range(5))
                             + (gc_scr[pl.ds(r0, CHUNK), h:h + 1], gct_scr[c, h:h + 1, :]))
        for (c, r0, h, lanes), (u_c, w_c, qk_c) in zip(where, _gdn_chunks_local(units, tril, strict)):
            u_scr[pl.ds(r0, CHUNK), lanes] = u_c
            w_scr[pl.ds(r0, CHUNK), lanes] = w_c.astype(BF16)
            qk_scr[c * DN_HEADS + h] = qk_c.astype(BF16)
        return carry

    lax.fori_loop(0, nchunk // chunks_per_iter, local_body, 0)

    def state_body(c, carry):
        r0 = pl.multiple_of(c * CHUNK, CHUNK)
        units = [(w_scr[pl.ds(r0, CHUNK), lanes], ops_scr[5, pl.ds(r0, CHUNK), lanes],
                  u_scr[pl.ds(r0, CHUNK), lanes], qk_scr[c * DN_HEADS + h],
                  ops_scr[6, pl.ds(r0, CHUNK), lanes], gl_scr[pl.ds(r0, 1), lanes])
                 for h, lanes in enumerate(head_lanes)]
        o_c, s_new = _gdn_chunk_state(units, [s_scr[h] for h in range(DN_HEADS)])
        for h, lanes in enumerate(head_lanes):
            s_scr[h] = s_new[h]
            o_scr[pl.ds(r0, CHUNK), lanes] = o_c[h]
        return carry

    lax.fori_loop(0, nchunk, state_body, 0)
    s_out_ref[0] = s_scr[...]

    og = _cols(p, off, "og")
    o_dn = jnp.concatenate(
        [_gate_norm(o_scr[:, h * DN_DV:(h + 1) * DN_DV], og[:, h * DN_DV:(h + 1) * DN_DV], ng_ref[...])
         for h in range(DN_HEADS)], axis=1)

    z_scr[halo:halo + tt, :] = _cols(p, off, "sc_c") * _cols(p, off, "sc_h")
    zc = None
    for j in range(n_mconv):
        start = halo - (n_mconv - 1) + j
        term = z_scr[start:start + tt, :] * mw_ref[j:j + 1, :]
        zc = term if zc is None else zc + term
    ztail = z_scr[halo + tt - (n_mconv - 1):halo + tt, :]
    cm_out_ref[0] = ztail
    z_scr[halo - (n_mconv - 1):halo, :] = ztail
    o_sc = _cols(p, off, "sc_b") * zc

    y_ref[...] = _mix_out(x, o_dn, o_sc, gate, w_out_ref,
                          lng_ref[sub:sub + 1, :], lnb_ref[sub:sub + 1, :], alpha)


def _mixer_prompt(x, mod, w_in, w_out, conv_w, mconv_w, alog, dtb, norm_g, lng, lnb, *, alpha, tt, off):
    bsz, t, d = x.shape
    assert t % tt == 0 and tt % LANES == 0
    nt = t // tt
    proj = w_in.shape[1]
    qkv_dim = off["qkv"][1] - off["qkv"][0]
    sc_w = off["sc_c"][1] - off["sc_c"][0]
    dn_w = DN_HEADS * DN_DV
    n_conv, n_mconv = conv_w.shape[0], mconv_w.shape[0]
    nchunk = tt // CHUNK
    chunks_per_iter = 2 if nchunk % 2 == 0 else 1
    x2 = x.reshape(bsz * t, d)
    kern = functools.partial(_mixer_prompt_kernel, alpha=alpha, tt=tt, off=off,
                             n_conv=n_conv, n_mconv=n_mconv, chunks_per_iter=chunks_per_iter)
    y, s_out, cq, cm = pl.pallas_call(
        kern,
        grid=(bsz, nt),
        in_specs=[pl.BlockSpec((tt, d), lambda b, i: (b * nt + i, 0)),
                  pl.BlockSpec((1, 3 * N_SUB, d), lambda b, i: (b, 0, 0)),
                  _const_spec((d, proj)), _const_spec(w_out.shape),
                  _const_spec(conv_w.shape), _const_spec(mconv_w.shape),
                  _const_spec(alog.shape), _const_spec(dtb.shape), _const_spec(norm_g.shape),
                  _const_spec(lng.shape), _const_spec(lnb.shape)],
        out_specs=[pl.BlockSpec((tt, d), lambda b, i: (b * nt + i, 0)),
                   pl.BlockSpec((1, DN_HEADS, DN_DK, DN_DV), lambda b, i: (b, 0, 0, 0)),
                   pl.BlockSpec((1, n_conv - 1, qkv_dim), lambda b, i: (b, 0, 0)),
                   pl.BlockSpec((1, n_mconv - 1, sc_w), lambda b, i: (b, 0, 0))],
        out_shape=[jax.ShapeDtypeStruct((bsz * t, d), F32),
                   jax.ShapeDtypeStruct((bsz, DN_HEADS, DN_DK, DN_DV), F32),
                   jax.ShapeDtypeStruct((bsz, n_conv - 1, qkv_dim), F32),
                   jax.ShapeDtypeStruct((bsz, n_mconv - 1, sc_w), F32)],
        scratch_shapes=[pltpu.VMEM((DN_HEADS, DN_DK, DN_DV), F32),
                        pltpu.VMEM((tt + SUBLANES, qkv_dim), F32),
                        pltpu.VMEM((tt + SUBLANES, sc_w), F32),
                        pltpu.VMEM((tt, AB_PAD), F32),
                        pltpu.VMEM((nchunk, SUBLANES, CHUNK), F32),
                        pltpu.VMEM((tt, dn_w), F32),
                        pltpu.VMEM((7, tt, dn_w), BF16),
                        pltpu.VMEM((tt, dn_w), F32),
                        pltpu.VMEM((tt, dn_w), BF16),
                        pltpu.VMEM((nchunk * DN_HEADS, CHUNK, CHUNK), BF16),
                        pltpu.VMEM((tt, dn_w), F32)],
        compiler_params=pltpu.CompilerParams(dimension_semantics=("arbitrary", "arbitrary"),
                                             vmem_limit_bytes=V7X_VMEM_LIMIT),
        name="mixer_prompt",
    )(x2, mod, w_in, w_out, conv_w, mconv_w, alog, dtb, norm_g, lng, lnb)
    return y.reshape(bsz, t, d), s_out, cq, cm


_PACK = SUBLANES


def _sample_phase_a(x_ref, mod_ref, w_in_ref, cw_ref, mw_ref, alog_ref, dtb_ref, cq_in_ref, cm_in_ref,
                    cq_out_ref, cm_out_ref, og_scr, osc_scr, lhs_scr, u_scr, kd_scr, gl_scr, qk_scr,
                    *, t_len, d, off, n_conv, n_mconv):
    bsz = x_ref.shape[0]
    hk = DN_HEADS * DN_DK
    qkv_dim = off["qkv"][1] - off["qkv"][0]
    sc_w = off["sc_c"][1] - off["sc_c"][0]

    x = _stack_time(x_ref, t_len, d)
    u = (x * (1.0 + _mod_rows(mod_ref, 1, t_len, d)) + _mod_rows(mod_ref, 0, t_len, d)).astype(BF16)
    p = jnp.dot(u, w_in_ref[...], preferred_element_type=F32)

    def rows(arr, t):
        return arr[t * bsz:(t + 1) * bsz]

    pq = _cols(p, off, "qkv")
    seq = ([cq_in_ref[:, j * qkv_dim:(j + 1) * qkv_dim] for j in range(n_conv - 1)]
           + [rows(pq, t) for t in range(t_len)])
    act = []
    for t in range(t_len):
        acc = None
        for j in range(n_conv):
            term = seq[t + j] * cw_ref[j:j + 1, :]
            acc = term if acc is None else acc + term
        act.append(_silu(acc))
    for j in range(n_conv - 1):
        cq_out_ref[:, j * qkv_dim:(j + 1) * qkv_dim] = seq[t_len + j]

    z = _cols(p, off, "sc_c") * _cols(p, off, "sc_h")
    zseq = ([cm_in_ref[:, j * sc_w:(j + 1) * sc_w] for j in range(n_mconv - 1)]
            + [rows(z, t) for t in range(t_len)])
    zc = []
    for t in range(t_len):
        acc = None
        for j in range(n_mconv):
            term = zseq[t + j] * mw_ref[j:j + 1, :]
            acc = term if acc is None else acc + term
        zc.append(acc)
    for j in range(n_mconv - 1):
        cm_out_ref[:, j * sc_w:(j + 1) * sc_w] = zseq[t_len + j]
    osc_scr[...] = _cols(p, off, "sc_b") * jnp.concatenate(zc, axis=0)
    og_scr[...] = _cols(p, off, "og")

    ab = _cols(p, off, "ab")
    beta_all = jax.nn.sigmoid(ab)
    g_all = -jnp.exp(alog_ref[...]) * _softplus(ab + dtb_ref[...])
    zero = jnp.zeros((bsz, LANES), F32)

    for h in range(DN_HEADS):
        q = [_l2norm(a[:, h * DN_DK:(h + 1) * DN_DK]) * (DN_DK ** -0.5) for a in act]
        k = [_l2norm(a[:, hk + h * DN_DK:hk + (h + 1) * DN_DK]) for a in act]
        v = [a[:, 2 * hk + h * DN_DV:2 * hk + (h + 1) * DN_DV] for a in act]
        beta = [rows(beta_all, t)[:, DN_HEADS + h:DN_HEADS + h + 1] for t in range(t_len)]
        g = [rows(g_all, t)[:, h:h + 1] for t in range(t_len)]
        gc = [g[0]]
        for t in range(1, t_len):
            gc.append(gc[-1] + g[t])
        kb = [k[t] * beta[t] for t in range(t_len)]
        vb = [v[t] * beta[t] for t in range(t_len)]
        eg = [jnp.exp(gc[t]) for t in range(t_len)]
        kbe = [kb[t] * eg[t] for t in range(t_len)]
        lm = [[None] * t_len for _ in range(t_len)]
        n_qk = 0
        for i in range(t_len):
            for j in range(i + 1):
                dec = jnp.exp(gc[i] - gc[j])
                qk_scr[h, n_qk] = jnp.broadcast_to(_rowsum(q[i] * k[j]) * dec, (bsz, LANES))
                n_qk += 1
                if j < i:
                    lm[i][j] = _rowsum(kb[i] * k[j]) * dec
        tm = [[None] * t_len for _ in range(t_len)]
        for i in range(t_len):
            for j in range(i):
                acc = lm[i][j]
                for m in range(j + 1, i):
                    acc = acc + lm[i][m] * tm[m][j]
                tm[i][j] = -acc
        for i in range(t_len):
            ui, wi = vb[i], kbe[i]
            for j in range(i):
                ui = ui + tm[i][j] * vb[j]
                wi = wi + tm[i][j] * kbe[j]
            lhs_scr[h, pl.ds(i, bsz, stride=_PACK), :] = wi
            lhs_scr[h, pl.ds(t_len + i, bsz, stride=_PACK), :] = q[i] * eg[i]
            u_scr[h, pl.ds(i, bsz, stride=_PACK), :] = ui
            u_scr[h, pl.ds(t_len + i, bsz, stride=_PACK), :] = zero
            kd_scr[h, pl.ds(i, bsz, stride=_PACK), :] = k[i] * jnp.exp(gc[t_len - 1] - gc[i])
            kd_scr[h, pl.ds(t_len + i, bsz, stride=_PACK), :] = zero
        gl_scr[h] = jnp.broadcast_to(jnp.exp(gc[t_len - 1]), (bsz, LANES))


def _sample_phase_c(x_ref, mod_ref, w_out_ref, ng_ref, lng_ref, lnb_ref, y_ref,
                    og_scr, osc_scr, res_scr, vn_scr, qk_scr, *, alpha, t_len, d):
    bsz = x_ref.shape[0]
    sub = 1
    x = _stack_time(x_ref, t_len, d)
    gate = _mod_rows(mod_ref, 2, t_len, d)
    og = og_scr[...]
    per_t = []
    for i in range(t_len):
        heads = []
        for h in range(DN_HEADS):
            oi = res_scr[h, pl.ds(t_len + i, bsz, stride=_PACK), :]
            base = i * (i + 1) // 2
            for j in range(i + 1):
                oi = oi + qk_scr[h, base + j] * vn_scr[h, pl.ds(j, bsz, stride=_PACK), :]
            heads.append(_gate_norm(oi, og[i * bsz:(i + 1) * bsz, h * DN_DV:(h + 1) * DN_DV], ng_ref[...]))
        per_t.append(jnp.concatenate(heads, axis=1))
    o_dn = jnp.concatenate(per_t, axis=0)
    y = _mix_out(x, o_dn, osc_scr[...], gate, w_out_ref,
                 lng_ref[sub:sub + 1, :], lnb_ref[sub:sub + 1, :], alpha)
    for t in range(t_len):
        y_ref[:, t * d:(t + 1) * d] = y[t * bsz:(t + 1) * bsz]


def _mixer_sample_kernel(x_ref, mod_ref, w_in_ref, w_out_ref, cw_ref, mw_ref, alog_ref, dtb_ref, ng_ref,
                         lng_ref, lnb_ref, cq_in_ref, cm_in_ref, s_in_ref,
                         y_ref, cq_out_ref, cm_out_ref, s_out_ref,
                         og_scr, osc_scr, lhs_scr, u_scr, kd_scr, gl_scr, qk_scr, res_scr, vn_scr,
                         *, alpha, t_len, d, off, n_conv, n_mconv, bt):
    step = pl.program_id(0)
    assert 2 * t_len == _PACK, "packed buffers hold [T rows | T rows] per sequence"

    @pl.when(step == 0)
    def _():
        _sample_phase_a(x_ref, mod_ref, w_in_ref, cw_ref, mw_ref, alog_ref, dtb_ref, cq_in_ref, cm_in_ref,
                        cq_out_ref, cm_out_ref, og_scr, osc_scr, lhs_scr, u_scr, kd_scr, gl_scr, qk_scr,
                        t_len=t_len, d=d, off=off, n_conv=n_conv, n_mconv=n_mconv)

    top = (lax.broadcasted_iota(jnp.int32, (_PACK, LANES), 0) < t_len).astype(F32)

    def seq_body(bl, carry):
        b = step * bt + bl
        r0 = pl.multiple_of(b * _PACK, _PACK)
        for h in range(DN_HEADS):
            s = s_in_ref[bl, h]
            r = _mm(lhs_scr[h, pl.ds(r0, _PACK), :], s)
            res_scr[h, pl.ds(r0, _PACK), :] = r
            vn = (u_scr[h, pl.ds(r0, _PACK), :] - r) * top
            vn_scr[h, pl.ds(r0, _PACK), :] = vn
            s_out_ref[bl, h] = (s * gl_scr[h, pl.ds(b, 1), :]
                                + _mm_at(kd_scr[h, pl.ds(r0, _PACK), :], vn))
        return carry

    lax.fori_loop(0, bt, seq_body, 0)

    @pl.when(step == pl.num_programs(0) - 1)
    def _():
        _sample_phase_c(x_ref, mod_ref, w_out_ref, ng_ref, lng_ref, lnb_ref, y_ref,
                        og_scr, osc_scr, res_scr, vn_scr, qk_scr, alpha=alpha, t_len=t_len, d=d)


def _mixer_sample(x, mod, w_in, w_out, conv_w, mconv_w, alog, dtb, norm_g, lng, lnb, s_in, cq_in, cm_in,
                  *, alpha, off, bt):
    bsz, t, d = x.shape
    assert bsz % bt == 0
    qkv_dim = off["qkv"][1] - off["qkv"][0]
    sc_w = off["sc_c"][1] - off["sc_c"][0]
    dn_w = DN_HEADS * DN_DV
    n_conv, n_mconv = conv_w.shape[0], mconv_w.shape[0]
    n_qk = t * (t + 1) // 2
    x2 = x.reshape(bsz, t * d)
    cq2 = cq_in.reshape(bsz, (n_conv - 1) * qkv_dim)
    cm2 = cm_in.reshape(bsz, (n_mconv - 1) * sc_w)
    kern = functools.partial(_mixer_sample_kernel, alpha=alpha, t_len=t, d=d, off=off,
                             n_conv=n_conv, n_mconv=n_mconv, bt=bt)
    consts = (w_in, w_out, conv_w, mconv_w, alog, dtb, norm_g, lng, lnb, cq2, cm2)
    state_spec = pl.BlockSpec((bt, DN_HEADS, DN_DK, DN_DV), lambda i: (i, 0, 0, 0))
    packed = pltpu.VMEM((DN_HEADS, _PACK * bsz, LANES), F32)
    y, cq, cm, s_out = pl.pallas_call(
        kern,
        grid=(bsz // bt,),
        in_specs=([_const_spec(x2.shape), _sub_mod_spec(bsz, d, 1)]
                  + [_const_spec(a.shape) for a in consts] + [state_spec]),
        out_specs=[pl.BlockSpec(x2.shape, lambda i: (0, 0)),
                   pl.BlockSpec(cq2.shape, lambda i: (0, 0)),
                   pl.BlockSpec(cm2.shape, lambda i: (0, 0)),
                   state_spec],
        out_shape=[jax.ShapeDtypeStruct(x2.shape, F32),
                   jax.ShapeDtypeStruct(cq2.shape, F32),
                   jax.ShapeDtypeStruct(cm2.shape, F32),
                   jax.ShapeDtypeStruct(s_in.shape, F32)],
        scratch_shapes=[pltpu.VMEM((t * bsz, dn_w), F32),
                        pltpu.VMEM((t * bsz, sc_w), F32),
                        packed,
                        packed,
                        packed,
                        pltpu.VMEM((DN_HEADS, bsz, LANES), F32),
                        pltpu.VMEM((DN_HEADS, n_qk, bsz, LANES), F32),
                        packed,
                        packed],
        compiler_params=pltpu.CompilerParams(dimension_semantics=("arbitrary",),
                                             vmem_limit_bytes=V7X_VMEM_LIMIT),
        name="mixer_sample",
    )(x2, mod, *consts, s_in)
    return (y.reshape(bsz, t, d), s_out, cq.reshape(bsz, n_conv - 1, qkv_dim),
            cm.reshape(bsz, n_mconv - 1, sc_w))


def _pack_w_in(w_in, qkv_dim, dn_width, sc_width):
    c0 = qkv_dim
    a = w_in[:, c0:c0 + DN_HEADS]
    b = w_in[:, c0 + DN_HEADS:c0 + 2 * DN_HEADS]
    c1 = c0 + 2 * DN_HEADS
    og = w_in[:, c1:c1 + dn_width]
    sc_b = w_in[:, c1 + dn_width:c1 + dn_width + sc_width]
    sc_c = w_in[:, c1 + dn_width + sc_width:c1 + dn_width + 2 * sc_width]
    sc_h = w_in[:, c1 + dn_width + 2 * sc_width:c1 + dn_width + 3 * sc_width]
    pad = jnp.zeros((w_in.shape[0], AB_PAD - 2 * DN_HEADS), w_in.dtype)
    return jnp.concatenate([w_in[:, :qkv_dim], og, sc_b, sc_c, sc_h, a, b, pad], axis=1).astype(BF16)


def _pad_lanes(v):
    return jnp.zeros((1, AB_PAD), F32).at[0, :v.shape[0]].set(v)


def kernel(x_prompt, x_sample, state_ssm, state_conv_qkv, state_conv_mix, c_prompt, c_sample, w_ada, b_ada, ln_g, ln_b, ffn1_wg, ffn1_wu, ffn1_wd, ffn2_wg, ffn2_wu, ffn2_wd, w_in, conv_qkv_w, a_log, dt_bias, dn_norm_g, conv_mix_w, w_out):
    depth = w_ada.shape[0]
    alpha = (2 * depth) ** 0.25
    bp, tp, d = x_prompt.shape
    qkv_dim = conv_qkv_w.shape[-1]
    sc_width = conv_mix_w.shape[-1]
    dn_width = DN_HEADS * DN_DV
    off = _proj_layout(qkv_dim, dn_width, sc_width)
    tm = min(512, tp)
    tt = min(512, tp)
    bt = min(8, x_sample.shape[0])

    hp, hs = x_prompt, x_sample
    outs = [[] for _ in range(6)]
    for l in range(depth):
        c_all = jnp.concatenate([c_prompt, c_sample], axis=0)
        mod = _adaln(c_all, w_ada[l], b_ada[l].reshape(1, -1), tn=d)
        mod_p = mod[:bp].reshape(bp, 3 * N_SUB, d)
        mod_s = mod[bp:]
        lng, lnb = ln_g[l], ln_b[l]
        w1 = (ffn1_wg[l].astype(BF16), ffn1_wu[l].astype(BF16), ffn1_wd[l].astype(BF16))
        w2 = (ffn2_wg[l].astype(BF16), ffn2_wu[l].astype(BF16), ffn2_wd[l].astype(BF16))
        w_in_p = _pack_w_in(w_in[l], qkv_dim, dn_width, sc_width)
        w_out_b = w_out[l].astype(BF16)
        alog, dtb = _pad_lanes(a_log[l]), _pad_lanes(dt_bias[l])
        norm_g = dn_norm_g[l].reshape(1, -1)
        mix_args = (w_in_p, w_out_b, conv_qkv_w[l], conv_mix_w[l], alog, dtb, norm_g, lng, lnb)

        hp = _ffn_prompt(hp, mod_p, *w1, lng, lnb, sub=0, alpha=alpha, tm=tm)
        hp, a1, a2, a3 = _mixer_prompt(hp, mod_p, *mix_args, alpha=alpha, tt=tt, off=off)
        hp = _ffn_prompt(hp, mod_p, *w2, lng, lnb, sub=2, alpha=alpha, tm=tm)

        hs = _ffn_sample(hs, mod_s, *w1, lng, lnb, sub=0, alpha=alpha)
        hs, b1, b2, b3 = _mixer_sample(hs, mod_s, *mix_args, state_ssm[l], state_conv_qkv[l],
                                       state_conv_mix[l], alpha=alpha, off=off, bt=bt)
        hs = _ffn_sample(hs, mod_s, *w2, lng, lnb, sub=2, alpha=alpha)
        for lst, val in zip(outs, (a1, a2, a3, b1, b2, b3)):
            lst.append(val)
    return (hp, hs) + tuple(jnp.stack(o) for o in outs)
```

```python
import functools

import jax
import jax.numpy as jnp
from jax import lax
from jax.experimental import pallas as pl
from jax.experimental.pallas import tpu as pltpu

F32 = jnp.float32
BF16 = jnp.bfloat16

LN_EPS = 1e-5
RMS_EPS = 1e-6
N_SUB = 3
DN_HEADS = 4
DN_DK = 128
DN_DV = 128
CHUNK = 64
LANES = 128
SUBLANES = 8
AB_PAD = LANES

V7X_VMEM_LIMIT = 60000 * 1024


def _mm(a, b):
    return jnp.dot(a.astype(BF16), b.astype(BF16), preferred_element_type=F32)


def _mm_bt(a, b):
    return lax.dot_general(a.astype(BF16), b.astype(BF16), (((1,), (1,)), ((), ())),
                           preferred_element_type=F32)


def _mm_at(a, b):
    return lax.dot_general(a.astype(BF16), b.astype(BF16), (((0,), (0,)), ((), ())),
                           preferred_element_type=F32)


def _mm_f32(a, b):
    return jnp.dot(a, b, preferred_element_type=F32, precision=lax.Precision.HIGHEST)


def _silu(x):
    return x * jax.nn.sigmoid(x)


def _softplus(x):
    return jnp.maximum(x, 0.0) + jnp.log1p(jnp.exp(-jnp.abs(x)))


def _layer_norm(y, g, b):
    mu = jnp.mean(y, axis=-1, keepdims=True)
    yc = y - mu
    var = jnp.mean(yc * yc, axis=-1, keepdims=True)
    return yc * lax.rsqrt(var + LN_EPS) * g + b


def _post(x, delta, gate, g, b, alpha):
    return _layer_norm(alpha * x + gate * delta, g, b)


def _rowsum(x):
    return jnp.sum(x, axis=-1, keepdims=True)


def _l2norm(x):
    return x * lax.rsqrt(_rowsum(x * x) + RMS_EPS)


def _const_spec(shape):
    nd = len(shape)
    return pl.BlockSpec(shape, lambda *_: (0,) * nd, pipeline_mode=pl.Buffered(1))


def _adaln_kernel(c_ref, w_ref, b_ref, o_ref):
    s = _silu(c_ref[...])
    o_ref[...] = _mm(s, w_ref[...]) + b_ref[...]


def _adaln(c, w, b, tn=1024):
    m, d = c.shape
    n = w.shape[1]
    assert n % tn == 0
    return pl.pallas_call(
        _adaln_kernel,
        grid=(n // tn,),
        in_specs=[_const_spec((m, d)),
                  pl.BlockSpec((d, tn), lambda j: (0, j)),
                  pl.BlockSpec((1, tn), lambda j: (0, j))],
        out_specs=pl.BlockSpec((m, tn), lambda j: (0, j)),
        out_shape=jax.ShapeDtypeStruct((m, n), F32),
        compiler_params=pltpu.CompilerParams(dimension_semantics=("arbitrary",)),
        name="adaln",
    )(c, w, b)


def _ffn_rows(x, shift, scale, gate, wg_ref, wu_ref, wd_ref, g, b, alpha):
    u = (x * (1.0 + scale) + shift).astype(BF16)
    hg = jnp.dot(u, wg_ref[...], preferred_element_type=F32)
    hu = jnp.dot(u, wu_ref[...], preferred_element_type=F32)
    h = (_silu(hg) * hu).astype(BF16)
    d = jnp.dot(h, wd_ref[...], preferred_element_type=F32)
    return _post(x, 0.5 * d, gate, g, b, alpha)


def _ffn_prompt_kernel(x_ref, mod_ref, wg_ref, wu_ref, wd_ref, lng_ref, lnb_ref, o_ref, *, sub, alpha):
    shift = mod_ref[0, 3 * sub + 0:3 * sub + 1, :]
    scale = mod_ref[0, 3 * sub + 1:3 * sub + 2, :]
    gate = mod_ref[0, 3 * sub + 2:3 * sub + 3, :]
    o_ref[...] = _ffn_rows(x_ref[...], shift, scale, gate, wg_ref, wu_ref, wd_ref,
                           lng_ref[sub:sub + 1, :], lnb_ref[sub:sub + 1, :], alpha)


def _ffn_prompt(x, mod, wg, wu, wd, lng, lnb, *, sub, alpha, tm):
    bsz, t, d = x.shape
    f = wg.shape[1]
    assert t % tm == 0
    per_seq = t // tm
    x2 = x.reshape(bsz * t, d)
    out = pl.pallas_call(
        functools.partial(_ffn_prompt_kernel, sub=sub, alpha=alpha),
        grid=(bsz * per_seq,),
        in_specs=[pl.BlockSpec((tm, d), lambda i: (i, 0)),
                  pl.BlockSpec((1, 3 * N_SUB, d), lambda i: (i // per_seq, 0, 0)),
                  _const_spec((d, f)), _const_spec((d, f)), _const_spec((f, d)),
                  _const_spec(lng.shape), _const_spec(lnb.shape)],
        out_specs=pl.BlockSpec((tm, d), lambda i: (i, 0)),
        out_shape=jax.ShapeDtypeStruct((bsz * t, d), F32),
        compiler_params=pltpu.CompilerParams(dimension_semantics=("arbitrary",),
                                             vmem_limit_bytes=V7X_VMEM_LIMIT),
        name=f"ffn_prompt{sub}",
    )(x2, mod, wg, wu, wd, lng, lnb)
    return out.reshape(bsz, t, d)


def _stack_time(ref, t_len, d):
    return jnp.concatenate([ref[:, t * d:(t + 1) * d] for t in range(t_len)], axis=0)


def _mod_rows(mod_ref, idx, t_len, d):
    m = mod_ref[:, idx * d:(idx + 1) * d]
    return jnp.concatenate([m] * t_len, axis=0)


def _ffn_sample_kernel(x_ref, mod_ref, wg_ref, wu_ref, wd_ref, lng_ref, lnb_ref, o_ref, *,
                       sub, alpha, t_len, d):
    bsz = x_ref.shape[0]
    x = _stack_time(x_ref, t_len, d)
    y = _ffn_rows(x, _mod_rows(mod_ref, 0, t_len, d), _mod_rows(mod_ref, 1, t_len, d),
                  _mod_rows(mod_ref, 2, t_len, d), wg_ref, wu_ref, wd_ref,
                  lng_ref[sub:sub + 1, :], lnb_ref[sub:sub + 1, :], alpha)
    for t in range(t_len):
        o_ref[:, t * d:(t + 1) * d] = y[t * bsz:(t + 1) * bsz]


def _sub_mod_spec(bsz, d, sub):
    return pl.BlockSpec((bsz, 3 * d), lambda *_: (0, sub), pipeline_mode=pl.Buffered(1))


def _ffn_sample(x, mod, wg, wu, wd, lng, lnb, *, sub, alpha):
    bsz, t, d = x.shape
    f = wg.shape[1]
    x2 = x.reshape(bsz, t * d)
    out = pl.pallas_call(
        functools.partial(_ffn_sample_kernel, sub=sub, alpha=alpha, t_len=t, d=d),
        grid=(1,),
        in_specs=[_const_spec((bsz, t * d)), _sub_mod_spec(bsz, d, sub),
                  _const_spec((d, f)), _const_spec((d, f)), _const_spec((f, d)),
                  _const_spec(lng.shape), _const_spec(lnb.shape)],
        out_specs=pl.BlockSpec((bsz, t * d), lambda i: (0, 0)),
        out_shape=jax.ShapeDtypeStruct((bsz, t * d), F32),
        compiler_params=pltpu.CompilerParams(dimension_semantics=("arbitrary",),
                                             vmem_limit_bytes=V7X_VMEM_LIMIT),
        name=f"ffn_sample{sub}",
    )(x2, mod, wg, wu, wd, lng, lnb)
    return out.reshape(bsz, t, d)


def _tri_inverse(lmats, c):
    ls = [l.astype(BF16) for l in lmats]
    ms = [jnp.dot(l, l, preferred_element_type=F32) for l in ls]
    qs = ms
    power = 2
    while 2 * power < c:
        mb = [m.astype(BF16) for m in ms]
        ms = [jnp.dot(m, m, preferred_element_type=F32) for m in mb]
        qm = [jnp.dot(q.astype(BF16), m.astype(BF16), preferred_element_type=F32) for q, m in zip(qs, ms)]
        qs = [q + m + x for q, m, x in zip(qs, ms, qm)]
        power *= 2
    rows = lax.broadcasted_iota(jnp.int32, (c, c), 0)
    cols = lax.broadcasted_iota(jnp.int32, (c, c), 1)
    eye = (rows == cols).astype(F32)
    lq = [jnp.dot(l, q.astype(BF16), preferred_element_type=F32) for l, q in zip(ls, qs)]
    return [eye - l + q - x for l, q, x in zip(lmats, qs, lq)]


def _gdn_chunks_local(units, tril, strict):
    c = units[0][1].shape[0]
    a = [lax.dot_general(jnp.concatenate([kb, q], axis=0), k, (((1,), (1,)), ((), ())),
                         preferred_element_type=F32)
         for kb, q, k, _, _, _, _ in units]
    lmats, qks = [], []
    for a_u, (_, _, _, _, _, gcol, grow) in zip(a, units):
        decay = jnp.where(tril, jnp.exp(jnp.where(tril, gcol - grow, 0.0)), 0.0)
        lmats.append(jnp.where(strict, a_u[:c] * decay, 0.0))
        qks.append(jnp.where(tril, a_u[c:] * decay, 0.0))
    tms = _tri_inverse(lmats, c)
    uw = [jnp.dot(tm.astype(BF16), jnp.concatenate([vb, kbe], axis=1), preferred_element_type=F32)
          for tm, (_, _, _, vb, kbe, _, _) in zip(tms, units)]
    return [(x[:, :DN_DV], x[:, DN_DV:], qk) for x, qk in zip(uw, qks)]


def _gdn_chunk_state(units, states):
    c = units[0][2].shape[0]
    r = [jnp.dot(jnp.concatenate([w, q_dec], axis=0), s.astype(BF16), preferred_element_type=F32)
         for (w, q_dec, _, _, _, _), s in zip(units, states)]
    v_new = [(u - r_h[:c]).astype(BF16) for (_, _, u, _, _, _), r_h in zip(units, r)]
    o = [r_h[c:] + jnp.dot(qk, v, preferred_element_type=F32)
         for (_, _, _, qk, _, _), r_h, v in zip(units, r, v_new)]
    s_new = [s * g_last + lax.dot_general(k_dec, v, (((0,), (0,)), ((), ())), preferred_element_type=F32)
             for (_, _, _, _, k_dec, g_last), s, v in zip(units, states, v_new)]
    return o, s_new


def _gate_norm(o, og, norm_g):
    o = o * lax.rsqrt(jnp.mean(o * o, axis=-1, keepdims=True) + RMS_EPS) * norm_g
    return o * _silu(og)


def _proj_layout(qkv_dim, dn_width, sc_width):
    off = {}
    pos = 0
    for name, width in (("qkv", qkv_dim), ("og", dn_width), ("sc_b", sc_width), ("sc_c", sc_width),
                        ("sc_h", sc_width), ("ab", AB_PAD)):
        off[name] = (pos, pos + width)
        pos += width
    return off


def _cols(p, off, name):
    return p[:, off[name][0]:off[name][1]]


def _mix_out(x, o_dn, o_sc, gate, w_out_ref, g, b, alpha):
    dn_w = o_dn.shape[1]
    mix = (jnp.dot(o_dn.astype(BF16), w_out_ref[0:dn_w, :], preferred_element_type=F32)
           + jnp.dot(o_sc.astype(BF16), w_out_ref[dn_w:, :], preferred_element_type=F32))
    return _post(x, mix, gate, g, b, alpha)


def _mixer_prompt_kernel(x_ref, mod_ref, w_in_ref, w_out_ref, cw_ref, mw_ref, alog_ref, dtb_ref, ng_ref,
                         lng_ref, lnb_ref,
                         y_ref, s_out_ref, cq_out_ref, cm_out_ref,
                         s_scr, cq_scr, cm_scr, gct_scr, gl_scr, ops_scr, u_scr, w_scr, qk_scr, o_scr,
                         *, alpha, nseq, tt, off, n_conv, n_mconv):
    t_idx = pl.program_id(1)
    rows_all = nseq * tt
    nchunk = rows_all // CHUNK
    per_seq = tt // CHUNK
    hk = DN_HEADS * DN_DK
    sub = 1

    @pl.when(t_idx == 0)
    def _():
        s_scr[...] = jnp.zeros_like(s_scr)
        cq_scr[...] = jnp.zeros_like(cq_scr)
        cm_scr[...] = jnp.zeros_like(cm_scr)

    xs = [x_ref[s] for s in range(nseq)]
    gates = [mod_ref[s, 3 * sub + 2:3 * sub + 3, :] for s in range(nseq)]
    u = jnp.concatenate(
        [(xs[s] * (1.0 + mod_ref[s, 3 * sub + 1:3 * sub + 2, :]) + mod_ref[s, 3 * sub + 0:3 * sub + 1, :])
         .astype(BF16) for s in range(nseq)], axis=0)
    p = jnp.dot(u, w_in_ref[...], preferred_element_type=F32)

    def causal_conv(new_rows, hist_scr, s, w_ref, taps):
        ext = jnp.concatenate([hist_scr[s], new_rows], axis=0)
        acc = new_rows * w_ref[taps - 1:taps, :]
        for j in range(taps - 1):
            acc = acc + pltpu.roll(ext, taps - 1 - j, axis=0)[SUBLANES:] * w_ref[j:j + 1, :]
        hist_scr[s] = new_rows[tt - SUBLANES:]
        return acc

    pq = _cols(p, off, "qkv")
    acts = []
    for s in range(nseq):
        pq_s = pq[s * tt:(s + 1) * tt]
        acts.append(_silu(causal_conv(pq_s, cq_scr, s, cw_ref, n_conv)))
        cq_out_ref[s] = pq_s[tt - (n_conv - 1):]
    act = jnp.concatenate(acts, axis=0)

    ab = _cols(p, off, "ab")
    beta_all = jax.nn.sigmoid(ab)
    g = -jnp.exp(alog_ref[...]) * _softplus(ab + dtb_ref[...])
    rows = lax.broadcasted_iota(jnp.int32, (CHUNK, CHUNK), 0)
    cols = lax.broadcasted_iota(jnp.int32, (CHUNK, CHUNK), 1)
    tril = rows >= cols
    strict = rows > cols
    tril_f = tril.astype(F32)
    gc_parts, gl_parts = [], []
    for c in range(nchunk):
        gc_c = _mm_f32(tril_f, g[c * CHUNK:(c + 1) * CHUNK, :])
        gc_parts.append(gc_c)
        gl_parts.append(jnp.broadcast_to(gc_c[CHUNK - 1:CHUNK, :], (CHUNK, AB_PAD)))
    gc = jnp.concatenate(gc_parts, axis=0)
    gl_rows = jnp.concatenate(gl_parts, axis=0)
    gc_t = gc.T
    for c in range(nchunk):
        gct_scr[c] = gc_t[0:SUBLANES, c * CHUNK:(c + 1) * CHUNK]

    eg_all = jnp.exp(gc)
    ekd_all = jnp.exp(gl_rows - gc)
    egl_all = jnp.exp(gl_rows)
    for h in range(DN_HEADS):
        lanes = slice(h * DN_DK, (h + 1) * DN_DK)
        q = _l2norm(act[:, h * DN_DK:(h + 1) * DN_DK]) * (DN_DK ** -0.5)
        k = _l2norm(act[:, hk + h * DN_DK:hk + (h + 1) * DN_DK])
        v = act[:, 2 * hk + h * DN_DV:2 * hk + (h + 1) * DN_DV]
        beta = beta_all[:, DN_HEADS + h:DN_HEADS + h + 1]
        eg = eg_all[:, h:h + 1]
        kb = k * beta
        ops_scr[0, :, lanes] = kb.astype(BF16)
        ops_scr[1, :, lanes] = q.astype(BF16)
        ops_scr[2, :, lanes] = k.astype(BF16)
        ops_scr[3, :, lanes] = (v * beta).astype(BF16)
        ops_scr[4, :, lanes] = (kb * eg).astype(BF16)
        ops_scr[5, :, lanes] = (q * eg).astype(BF16)
        ops_scr[6, :, lanes] = (k * ekd_all[:, h:h + 1]).astype(BF16)
        gl_scr[:, lanes] = jnp.broadcast_to(egl_all[:, h:h + 1], (rows_all, DN_DV))

    head_lanes = [slice(h * DN_DK, (h + 1) * DN_DK) for h in range(DN_HEADS)]

    where = [(c, h, lanes) for c in range(nchunk) for h, lanes in enumerate(head_lanes)]
    units = [tuple(ops_scr[i, c * CHUNK:(c + 1) * CHUNK, lanes] for i in range(5))
             + (gc[c * CHUNK:(c + 1) * CHUNK, h:h + 1], gct_scr[c, h:h + 1, :])
             for c, h, lanes in where]
    for (c, h, lanes), (u_c, w_c, qk_c) in zip(where, _gdn_chunks_local(units, tril, strict)):
        u_scr[c * CHUNK:(c + 1) * CHUNK, lanes] = u_c
        w_scr[c * CHUNK:(c + 1) * CHUNK, lanes] = w_c.astype(BF16)
        qk_scr[c * DN_HEADS + h] = qk_c.astype(BF16)

    for j in range(per_seq):
        chains = [(s, s * per_seq + j, h, lanes) for s in range(nseq) for h, lanes in enumerate(head_lanes)]
        units = [(w_scr[c * CHUNK:(c + 1) * CHUNK, lanes], ops_scr[5, c * CHUNK:(c + 1) * CHUNK, lanes],
                  u_scr[c * CHUNK:(c + 1) * CHUNK, lanes], qk_scr[c * DN_HEADS + h],
                  ops_scr[6, c * CHUNK:(c + 1) * CHUNK, lanes], gl_scr[c * CHUNK:c * CHUNK + 1, lanes])
                 for s, c, h, lanes in chains]
        o_c, s_new = _gdn_chunk_state(units, [s_scr[s * DN_HEADS + h] for s, _, h, _ in chains])
        for (s, c, h, lanes), o_u, s_u in zip(chains, o_c, s_new):
            s_scr[s * DN_HEADS + h] = s_u
            o_scr[c * CHUNK:(c + 1) * CHUNK, lanes] = o_u
    for s in range(nseq):
        s_out_ref[s] = s_scr[s * DN_HEADS:(s + 1) * DN_HEADS]

    og = _cols(p, off, "og")
    o_dn = jnp.concatenate(
        [_gate_norm(o_scr[:, h * DN_DV:(h + 1) * DN_DV], og[:, h * DN_DV:(h + 1) * DN_DV], ng_ref[...])
         for h in range(DN_HEADS)], axis=1)

    z = _cols(p, off, "sc_c") * _cols(p, off, "sc_h")
    zcs = []
    for s in range(nseq):
        z_s = z[s * tt:(s + 1) * tt]
        zcs.append(causal_conv(z_s, cm_scr, s, mw_ref, n_mconv))
        cm_out_ref[s] = z_s[tt - (n_mconv - 1):]
    o_sc = _cols(p, off, "sc_b") * jnp.concatenate(zcs, axis=0)

    dn_w = o_dn.shape[1]
    mix = (jnp.dot(o_dn.astype(BF16), w_out_ref[0:dn_w, :], preferred_element_type=F32)
           + jnp.dot(o_sc.astype(BF16), w_out_ref[dn_w:, :], preferred_element_type=F32))
    for s in range(nseq):
        y_ref[s] = _post(xs[s], mix[s * tt:(s + 1) * tt], gates[s],
                         lng_ref[sub:sub + 1, :], lnb_ref[sub:sub + 1, :], alpha)


def _mixer_prompt(x, mod, w_in, w_out, conv_w, mconv_w, alog, dtb, norm_g, lng, lnb, *, alpha, nseq, tt, off):
    bsz, t, d = x.shape
    assert t % tt == 0 and tt % LANES == 0 and bsz % nseq == 0
    nt = t // tt
    rows = nseq * tt
    proj = w_in.shape[1]
    qkv_dim = off["qkv"][1] - off["qkv"][0]
    sc_w = off["sc_c"][1] - off["sc_c"][0]
    dn_w = DN_HEADS * DN_DV
    n_conv, n_mconv = conv_w.shape[0], mconv_w.shape[0]
    assert max(n_conv, n_mconv) - 1 <= SUBLANES <= tt
    nchunk = rows // CHUNK
    kern = functools.partial(_mixer_prompt_kernel, alpha=alpha, nseq=nseq, tt=tt, off=off,
                             n_conv=n_conv, n_mconv=n_mconv)
    y, s_out, cq, cm = pl.pallas_call(
        kern,
        grid=(bsz // nseq, nt),
        in_specs=[pl.BlockSpec((nseq, tt, d), lambda b, i: (b, i, 0)),
                  pl.BlockSpec((nseq, 3 * N_SUB, d), lambda b, i: (b, 0, 0)),
                  _const_spec((d, proj)), _const_spec(w_out.shape),
                  _const_spec(conv_w.shape), _const_spec(mconv_w.shape),
                  _const_spec(alog.shape), _const_spec(dtb.shape), _const_spec(norm_g.shape),
                  _const_spec(lng.shape), _const_spec(lnb.shape)],
        out_specs=[pl.BlockSpec((nseq, tt, d), lambda b, i: (b, i, 0)),
                   pl.BlockSpec((nseq, DN_HEADS, DN_DK, DN_DV), lambda b, i: (b, 0, 0, 0)),
                   pl.BlockSpec((nseq, n_conv - 1, qkv_dim), lambda b, i: (b, 0, 0)),
                   pl.BlockSpec((nseq, n_mconv - 1, sc_w), lambda b, i: (b, 0, 0))],
        out_shape=[jax.ShapeDtypeStruct((bsz, t, d), F32),
                   jax.ShapeDtypeStruct((bsz, DN_HEADS, DN_DK, DN_DV), F32),
                   jax.ShapeDtypeStruct((bsz, n_conv - 1, qkv_dim), F32),
                   jax.ShapeDtypeStruct((bsz, n_mconv - 1, sc_w), F32)],
        scratch_shapes=[pltpu.VMEM((nseq * DN_HEADS, DN_DK, DN_DV), F32),
                        pltpu.VMEM((nseq, SUBLANES, qkv_dim), F32),
                        pltpu.VMEM((nseq, SUBLANES, sc_w), F32),
                        pltpu.VMEM((nchunk, SUBLANES, CHUNK), F32),
                        pltpu.VMEM((rows, dn_w), F32),
                        pltpu.VMEM((7, rows, dn_w), BF16),
                        pltpu.VMEM((rows, dn_w), F32),
                        pltpu.VMEM((rows, dn_w), BF16),
                        pltpu.VMEM((nchunk * DN_HEADS, CHUNK, CHUNK), BF16),
                        pltpu.VMEM((rows, dn_w), F32)],
        compiler_params=pltpu.CompilerParams(dimension_semantics=("arbitrary", "arbitrary"),
                                             vmem_limit_bytes=V7X_VMEM_LIMIT),
        name="mixer_prompt",
    )(x, mod, w_in, w_out, conv_w, mconv_w, alog, dtb, norm_g, lng, lnb)
    return y, s_out, cq, cm


_PACK = SUBLANES


def _sample_phase_a(x_ref, mod_ref, w_in_ref, cw_ref, mw_ref, alog_ref, dtb_ref, cq_in_ref, cm_in_ref,
                    cq_out_ref, cm_out_ref, og_scr, osc_scr, lhs_scr, u_scr, kd_scr, gl_scr, qk_scr,
                    *, t_len, d, off, n_conv, n_mconv):
    bsz = x_ref.shape[0]
    hk = DN_HEADS * DN_DK
    qkv_dim = off["qkv"][1] - off["qkv"][0]
    sc_w = off["sc_c"][1] - off["sc_c"][0]

    x = _stack_time(x_ref, t_len, d)
    u = (x * (1.0 + _mod_rows(mod_ref, 1, t_len, d)) + _mod_rows(mod_ref, 0, t_len, d)).astype(BF16)
    p = jnp.dot(u, w_in_ref[...], preferred_element_type=F32)

    def rows(arr, t):
        return arr[t * bsz:(t + 1) * bsz]

    pq = _cols(p, off, "qkv")
    seq = ([cq_in_ref[:, j * qkv_dim:(j + 1) * qkv_dim] for j in range(n_conv - 1)]
           + [rows(pq, t) for t in range(t_len)])
    act = []
    for t in range(t_len):
        acc = None
        for j in range(n_conv):
            term = seq[t + j] * cw_ref[j:j + 1, :]
            acc = term if acc is None else acc + term
        act.append(_silu(acc))
    for j in range(n_conv - 1):
        cq_out_ref[:, j * qkv_dim:(j + 1) * qkv_dim] = seq[t_len + j]

    z = _cols(p, off, "sc_c") * _cols(p, off, "sc_h")
    zseq = ([cm_in_ref[:, j * sc_w:(j + 1) * sc_w] for j in range(n_mconv - 1)]
            + [rows(z, t) for t in range(t_len)])
    zc = []
    for t in range(t_len):
        acc = None
        for j in range(n_mconv):
            term = zseq[t + j] * mw_ref[j:j + 1, :]
            acc = term if acc is None else acc + term
        zc.append(acc)
    for j in range(n_mconv - 1):
        cm_out_ref[:, j * sc_w:(j + 1) * sc_w] = zseq[t_len + j]
    osc_scr[...] = _cols(p, off, "sc_b") * jnp.concatenate(zc, axis=0)
    og_scr[...] = _cols(p, off, "og")

    ab = _cols(p, off, "ab")
    beta_all = jax.nn.sigmoid(ab)
    g_all = -jnp.exp(alog_ref[...]) * _softplus(ab + dtb_ref[...])
    zero = jnp.zeros((bsz, LANES), F32)

    for h in range(DN_HEADS):
        q = [_l2norm(a[:, h * DN_DK:(h + 1) * DN_DK]) * (DN_DK ** -0.5) for a in act]
        k = [_l2norm(a[:, hk + h * DN_DK:hk + (h + 1) * DN_DK]) for a in act]
        v = [a[:, 2 * hk + h * DN_DV:2 * hk + (h + 1) * DN_DV] for a in act]
        beta = [rows(beta_all, t)[:, DN_HEADS + h:DN_HEADS + h + 1] for t in range(t_len)]
        g = [rows(g_all, t)[:, h:h + 1] for t in range(t_len)]
        gc = [g[0]]
        for t in range(1, t_len):
            gc.append(gc[-1] + g[t])
        kb = [k[t] * beta[t] for t in range(t_len)]
        vb = [v[t] * beta[t] for t in range(t_len)]
        eg = [jnp.exp(gc[t]) for t in range(t_len)]
        kbe = [kb[t] * eg[t] for t in range(t_len)]
        lm = [[None] * t_len for _ in range(t_len)]
        n_qk = 0
        for i in range(t_len):
            for j in range(i + 1):
                dec = jnp.exp(gc[i] - gc[j])
                qk_scr[h, n_qk] = jnp.broadcast_to(_rowsum(q[i] * k[j]) * dec, (bsz, LANES))
                n_qk += 1
                if j < i:
                    lm[i][j] = _rowsum(kb[i] * k[j]) * dec
        tm = [[None] * t_len for _ in range(t_len)]
        for i in range(t_len):
            for j in range(i):
                acc = lm[i][j]
                for m in range(j + 1, i):
                    acc = acc + lm[i][m] * tm[m][j]
                tm[i][j] = -acc
        for i in range(t_len):
            ui, wi = vb[i], kbe[i]
            for j in range(i):
                ui = ui + tm[i][j] * vb[j]
                wi = wi + tm[i][j] * kbe[j]
            lhs_scr[h, pl.ds(i, bsz, stride=_PACK), :] = wi
            lhs_scr[h, pl.ds(t_len + i, bsz, stride=_PACK), :] = q[i] * eg[i]
            u_scr[h, pl.ds(i, bsz, stride=_PACK), :] = ui
            u_scr[h, pl.ds(t_len + i, bsz, stride=_PACK), :] = zero
            kd_scr[h, pl.ds(i, bsz, stride=_PACK), :] = k[i] * jnp.exp(gc[t_len - 1] - gc[i])
            kd_scr[h, pl.ds(t_len + i, bsz, stride=_PACK), :] = zero
        gl_scr[h] = jnp.broadcast_to(jnp.exp(gc[t_len - 1]), (bsz, LANES))


def _sample_phase_c(x_ref, mod_ref, w_out_ref, ng_ref, lng_ref, lnb_ref, y_ref,
                    og_scr, osc_scr, res_scr, vn_scr, qk_scr, *, alpha, t_len, d):
    bsz = x_ref.shape[0]
    sub = 1
    x = _stack_time(x_ref, t_len, d)
    gate = _mod_rows(mod_ref, 2, t_len, d)
    og = og_scr[...]
    per_t = []
    for i in range(t_len):
        heads = []
        for h in range(DN_HEADS):
            oi = res_scr[h, pl.ds(t_len + i, bsz, stride=_PACK), :]
            base = i * (i + 1) // 2
            for j in range(i + 1):
                oi = oi + qk_scr[h, base + j] * vn_scr[h, pl.ds(j, bsz, stride=_PACK), :]
            heads.append(_gate_norm(oi, og[i * bsz:(i + 1) * bsz, h * DN_DV:(h + 1) * DN_DV], ng_ref[...]))
        per_t.append(jnp.concatenate(heads, axis=1))
    o_dn = jnp.concatenate(per_t, axis=0)
    y = _mix_out(x, o_dn, osc_scr[...], gate, w_out_ref,
                 lng_ref[sub:sub + 1, :], lnb_ref[sub:sub + 1, :], alpha)
    for t in range(t_len):
        y_ref[:, t * d:(t + 1) * d] = y[t * bsz:(t + 1) * bsz]


def _mixer_sample_kernel(x_ref, mod_ref, w_in_ref, w_out_ref, cw_ref, mw_ref, alog_ref, dtb_ref, ng_ref,
                         lng_ref, lnb_ref, cq_in_ref, cm_in_ref, s_in_ref,
                         y_ref, cq_out_ref, cm_out_ref, s_out_ref,
                         og_scr, osc_scr, lhs_scr, u_scr, kd_scr, gl_scr, qk_scr, res_scr, vn_scr,
                         *, alpha, t_len, d, off, n_conv, n_mconv, bt):
    step = pl.program_id(0)
    assert 2 * t_len == _PACK, "packed buffers hold [T rows | T rows] per sequence"

    @pl.when(step == 0)
    def _():
        _sample_phase_a(x_ref, mod_ref, w_in_ref, cw_ref, mw_ref, alog_ref, dtb_ref, cq_in_ref, cm_in_ref,
                        cq_out_ref, cm_out_ref, og_scr, osc_scr, lhs_scr, u_scr, kd_scr, gl_scr, qk_scr,
                        t_len=t_len, d=d, off=off, n_conv=n_conv, n_mconv=n_mconv)

    top = (lax.broadcasted_iota(jnp.int32, (_PACK, LANES), 0) < t_len).astype(F32)

    def seq_body(bl, carry):
        b = step * bt + bl
        r0 = pl.multiple_of(b * _PACK, _PACK)
        for h in range(DN_HEADS):
            s = s_in_ref[bl, h]
            r = _mm(lhs_scr[h, pl.ds(r0, _PACK), :], s)
            res_scr[h, pl.ds(r0, _PACK), :] = r
            vn = (u_scr[h, pl.ds(r0, _PACK), :] - r) * top
            vn_scr[h, pl.ds(r0, _PACK), :] = vn
            s_out_ref[bl, h] = (s * gl_scr[h, pl.ds(b, 1), :]
                                + _mm_at(kd_scr[h, pl.ds(r0, _PACK), :], vn))
        return carry

    lax.fori_loop(0, bt, seq_body, 0)

    @pl.when(step == pl.num_programs(0) - 1)
    def _():
        _sample_phase_c(x_ref, mod_ref, w_out_ref, ng_ref, lng_ref, lnb_ref, y_ref,
                        og_scr, osc_scr, res_scr, vn_scr, qk_scr, alpha=alpha, t_len=t_len, d=d)


def _mixer_sample(x, mod, w_in, w_out, conv_w, mconv_w, alog, dtb, norm_g, lng, lnb, s_in, cq_in, cm_in,
                  *, alpha, off, bt):
    bsz, t, d = x.shape
    assert bsz % bt == 0
    qkv_dim = off["qkv"][1] - off["qkv"][0]
    sc_w = off["sc_c"][1] - off["sc_c"][0]
    dn_w = DN_HEADS * DN_DV
    n_conv, n_mconv = conv_w.shape[0], mconv_w.shape[0]
    n_qk = t * (t + 1) // 2
    x2 = x.reshape(bsz, t * d)
    cq2 = cq_in.reshape(bsz, (n_conv - 1) * qkv_dim)
    cm2 = cm_in.reshape(bsz, (n_mconv - 1) * sc_w)
    kern = functools.partial(_mixer_sample_kernel, alpha=alpha, t_len=t, d=d, off=off,
                             n_conv=n_conv, n_mconv=n_mconv, bt=bt)
    consts = (w_in, w_out, conv_w, mconv_w, alog, dtb, norm_g, lng, lnb, cq2, cm2)
    state_spec = pl.BlockSpec((bt, DN_HEADS, DN_DK, DN_DV), lambda i: (i, 0, 0, 0))
    packed = pltpu.VMEM((DN_HEADS, _PACK * bsz, LANES), F32)
    y, cq, cm, s_out = pl.pallas_call(
        kern,
        grid=(bsz // bt,),
        in_specs=([_const_spec(x2.shape), _sub_mod_spec(bsz, d, 1)]
                  + [_const_spec(a.shape) for a in consts] + [state_spec]),
        out_specs=[pl.BlockSpec(x2.shape, lambda i: (0, 0)),
                   pl.BlockSpec(cq2.shape, lambda i: (0, 0)),
                   pl.BlockSpec(cm2.shape, lambda i: (0, 0)),
                   state_spec],
        out_shape=[jax.ShapeDtypeStruct(x2.shape, F32),
                   jax.ShapeDtypeStruct(cq2.shape, F32),
                   jax.ShapeDtypeStruct(cm2.shape, F32),
                   jax.ShapeDtypeStruct(s_in.shape, F32)],
        scratch_shapes=[pltpu.VMEM((t * bsz, dn_w), F32),
                        pltpu.VMEM((t * bsz, sc_w), F32),
                        packed,
                        packed,
                        packed,
                        pltpu.VMEM((DN_HEADS, bsz, LANES), F32),
                        pltpu.VMEM((DN_HEADS, n_qk, bsz, LANES), F32),
                        packed,
                        packed],
        compiler_params=pltpu.CompilerParams(dimension_semantics=("arbitrary",),
                                             vmem_limit_bytes=V7X_VMEM_LIMIT),
        name="mixer_sample",
    )(x2, mod, *consts, s_in)
    return (y.reshape(bsz, t, d), s_out, cq.reshape(bsz, n_conv - 1, qkv_dim),
            cm.reshape(bsz, n_mconv - 1, sc_w))


def _pack_w_in(w_in, qkv_dim, dn_width, sc_width):
    c0 = qkv_dim
    a = w_in[:, c0:c0 + DN_HEADS]
    b = w_in[:, c0 + DN_HEADS:c0 + 2 * DN_HEADS]
    c1 = c0 + 2 * DN_HEADS
    og = w_in[:, c1:c1 + dn_width]
    sc_b = w_in[:, c1 + dn_width:c1 + dn_width + sc_width]
    sc_c = w_in[:, c1 + dn_width + sc_width:c1 + dn_width + 2 * sc_width]
    sc_h = w_in[:, c1 + dn_width + 2 * sc_width:c1 + dn_width + 3 * sc_width]
    pad = jnp.zeros((w_in.shape[0], AB_PAD - 2 * DN_HEADS), w_in.dtype)
    return jnp.concatenate([w_in[:, :qkv_dim], og, sc_b, sc_c, sc_h, a, b, pad], axis=1).astype(BF16)


def _pad_lanes(v):
    return jnp.zeros((1, AB_PAD), F32).at[0, :v.shape[0]].set(v)


def kernel(x_prompt, x_sample, state_ssm, state_conv_qkv, state_conv_mix, c_prompt, c_sample, w_ada, b_ada, ln_g, ln_b, ffn1_wg, ffn1_wu, ffn1_wd, ffn2_wg, ffn2_wu, ffn2_wd, w_in, conv_qkv_w, a_log, dt_bias, dn_norm_g, conv_mix_w, w_out):
    depth = w_ada.shape[0]
    alpha = (2 * depth) ** 0.25
    bp, tp, d = x_prompt.shape
    qkv_dim = conv_qkv_w.shape[-1]
    sc_width = conv_mix_w.shape[-1]
    dn_width = DN_HEADS * DN_DV
    off = _proj_layout(qkv_dim, dn_width, sc_width)
    tm = min(512, tp)
    nseq = 2 if bp % 2 == 0 else 1
    tt = min(512 // nseq, tp)
    bt = min(8, x_sample.shape[0])

    hp, hs = x_prompt, x_sample
    outs = [[] for _ in range(6)]
    for l in range(depth):
        c_all = jnp.concatenate([c_prompt, c_sample], axis=0)
        mod = _adaln(c_all, w_ada[l], b_ada[l].reshape(1, -1), tn=d)
        mod_p = mod[:bp].reshape(bp, 3 * N_SUB, d)
        mod_s = mod[bp:]
        lng, lnb = ln_g[l], ln_b[l]
        w1 = (ffn1_wg[l].astype(BF16), ffn1_wu[l].astype(BF16), ffn1_wd[l].astype(BF16))
        w2 = (ffn2_wg[l].astype(BF16), ffn2_wu[l].astype(BF16), ffn2_wd[l].astype(BF16))
        w_in_p = _pack_w_in(w_in[l], qkv_dim, dn_width, sc_width)
        w_out_b = w_out[l].astype(BF16)
        alog, dtb = _pad_lanes(a_log[l]), _pad_lanes(dt_bias[l])
        norm_g = dn_norm_g[l].reshape(1, -1)
        mix_args = (w_in_p, w_out_b, conv_qkv_w[l], conv_mix_w[l], alog, dtb, norm_g, lng, lnb)

        hp = _ffn_prompt(hp, mod_p, *w1, lng, lnb, sub=0, alpha=alpha, tm=tm)
        hp, a1, a2, a3 = _mixer_prompt(hp, mod_p, *mix_args, alpha=alpha, nseq=nseq, tt=tt, off=off)
        hp = _ffn_prompt(hp, mod_p, *w2, lng, lnb, sub=2, alpha=alpha, tm=tm)

        hs = _ffn_sample(hs, mod_s, *w1, lng, lnb, sub=0, alpha=alpha)
        hs, b1, b2, b3 = _mixer_sample(hs, mod_s, *mix_args, state_ssm[l], state_conv_qkv[l],
                                       state_conv_mix[l], alpha=alpha, off=off, bt=bt)
        hs = _ffn_sample(hs, mod_s, *w2, lng, lnb, sub=2, alpha=alpha)
        for lst, val in zip(outs, (a1, a2, a3, b1, b2, b3)):
            lst.append(val)
    return (hp, hs) + tuple(jnp.stack(o) for o in outs)
```

```python
import functools

import jax
import jax.numpy as jnp
from jax import lax
from jax.experimental import pallas as pl
from jax.experimental.pallas import tpu as pltpu

F32 = jnp.float32
BF16 = jnp.bfloat16

LN_EPS = 1e-5
RMS_EPS = 1e-6
N_SUB = 3
DN_HEADS = 4
DN_DK = 128
DN_DV = 128
CHUNK = 64
LANES = 128
SUBLANES = 8
AB_PAD = LANES

V7X_VMEM_LIMIT = 60000 * 1024


def _mm(a, b):
    return jnp.dot(a.astype(BF16), b.astype(BF16), preferred_element_type=F32)


def _mm_bt(a, b):
    return lax.dot_general(a.astype(BF16), b.astype(BF16), (((1,), (1,)), ((), ())),
                           preferred_element_type=F32)


def _mm_at(a, b):
    return lax.dot_general(a.astype(BF16), b.astype(BF16), (((0,), (0,)), ((), ())),
                           preferred_element_type=F32)


def _mm_f32(a, b):
    return jnp.dot(a, b, preferred_element_type=F32, precision=lax.Precision.HIGHEST)


def _silu(x):
    return x * jax.nn.sigmoid(x)


def _softplus(x):
    return jnp.maximum(x, 0.0) + jnp.log1p(jnp.exp(-jnp.abs(x)))


def _layer_norm(y, g, b):
    mu = jnp.mean(y, axis=-1, keepdims=True)
    yc = y - mu
    var = jnp.mean(yc * yc, axis=-1, keepdims=True)
    return yc * lax.rsqrt(var + LN_EPS) * g + b


def _post(x, delta, gate, g, b, alpha):
    return _layer_norm(alpha * x + gate * delta, g, b)


def _rowsum(x):
    return jnp.sum(x, axis=-1, keepdims=True)


def _l2norm(x):
    return x * lax.rsqrt(_rowsum(x * x) + RMS_EPS)


def _const_spec(shape):
    nd = len(shape)
    return pl.BlockSpec(shape, lambda *_: (0,) * nd, pipeline_mode=pl.Buffered(1))


def _adaln_kernel(c_ref, w_ref, b_ref, o_ref):
    s = _silu(c_ref[...])
    o_ref[...] = _mm(s, w_ref[...]) + b_ref[...]


def _adaln(c, w, b, tn=1024):
    m, d = c.shape
    n = w.shape[1]
    assert n % tn == 0
    return pl.pallas_call(
        _adaln_kernel,
        grid=(n // tn,),
        in_specs=[_const_spec((m, d)),
                  pl.BlockSpec((d, tn), lambda j: (0, j)),
                  pl.BlockSpec((1, tn), lambda j: (0, j))],
        out_specs=pl.BlockSpec((m, tn), lambda j: (0, j)),
        out_shape=jax.ShapeDtypeStruct((m, n), F32),
        compiler_params=pltpu.CompilerParams(dimension_semantics=("arbitrary",)),
        name="adaln",
    )(c, w, b)


def _ffn_rows(x, shift, scale, gate, wg_ref, wu_ref, wd_ref, g, b, alpha):
    u = (x * (1.0 + scale) + shift).astype(BF16)
    hg = jnp.dot(u, wg_ref[...], preferred_element_type=F32)
    hu = jnp.dot(u, wu_ref[...], preferred_element_type=F32)
    h = (_silu(hg) * hu).astype(BF16)
    d = jnp.dot(h, wd_ref[...], preferred_element_type=F32)
    return _post(x, 0.5 * d, gate, g, b, alpha)


def _ffn_prompt_kernel(x_ref, mod_ref, wg_ref, wu_ref, wd_ref, lng_ref, lnb_ref, o_ref, *, sub, alpha, parts):
    shift = mod_ref[0, 3 * sub + 0:3 * sub + 1, :]
    scale = mod_ref[0, 3 * sub + 1:3 * sub + 2, :]
    gate = mod_ref[0, 3 * sub + 2:3 * sub + 3, :]
    rows = x_ref.shape[0] // parts
    for part in range(parts):
        sl = slice(part * rows, (part + 1) * rows)
        o_ref[sl, :] = _ffn_rows(x_ref[sl, :], shift, scale, gate, wg_ref, wu_ref, wd_ref,
                                 lng_ref[sub:sub + 1, :], lnb_ref[sub:sub + 1, :], alpha)


def _ffn_prompt(x, mod, wg, wu, wd, lng, lnb, *, sub, alpha, tm, parts):
    bsz, t, d = x.shape
    f = wg.shape[1]
    assert t % tm == 0 and tm % (parts * SUBLANES) == 0
    per_seq = t // tm
    x2 = x.reshape(bsz * t, d)
    out = pl.pallas_call(
        functools.partial(_ffn_prompt_kernel, sub=sub, alpha=alpha, parts=parts),
        grid=(bsz * per_seq,),
        in_specs=[pl.BlockSpec((tm, d), lambda i: (i, 0)),
                  pl.BlockSpec((1, 3 * N_SUB, d), lambda i: (i // per_seq, 0, 0)),
                  _const_spec((d, f)), _const_spec((d, f)), _const_spec((f, d)),
                  _const_spec(lng.shape), _const_spec(lnb.shape)],
        out_specs=pl.BlockSpec((tm, d), lambda i: (i, 0)),
        out_shape=jax.ShapeDtypeStruct((bsz * t, d), F32),
        compiler_params=pltpu.CompilerParams(dimension_semantics=("arbitrary",),
                                             vmem_limit_bytes=V7X_VMEM_LIMIT),
        name=f"ffn_prompt{sub}",
    )(x2, mod, wg, wu, wd, lng, lnb)
    return out.reshape(bsz, t, d)


def _stack_time(ref, t_len, d):
    return jnp.concatenate([ref[:, t * d:(t + 1) * d] for t in range(t_len)], axis=0)


def _mod_rows(mod_ref, idx, t_len, d):
    m = mod_ref[:, idx * d:(idx + 1) * d]
    return jnp.concatenate([m] * t_len, axis=0)


def _ffn_sample_kernel(x_ref, mod_ref, wg_ref, wu_ref, wd_ref, lng_ref, lnb_ref, o_ref, *,
                       sub, alpha, t_len, d):
    bsz = x_ref.shape[0]
    x = _stack_time(x_ref, t_len, d)
    y = _ffn_rows(x, _mod_rows(mod_ref, 0, t_len, d), _mod_rows(mod_ref, 1, t_len, d),
                  _mod_rows(mod_ref, 2, t_len, d), wg_ref, wu_ref, wd_ref,
                  lng_ref[sub:sub + 1, :], lnb_ref[sub:sub + 1, :], alpha)
    for t in range(t_len):
        o_ref[:, t * d:(t + 1) * d] = y[t * bsz:(t + 1) * bsz]


def _sub_mod_spec(bsz, d, sub):
    return pl.BlockSpec((bsz, 3 * d), lambda *_: (0, sub), pipeline_mode=pl.Buffered(1))


def _ffn_sample(x, mod, wg, wu, wd, lng, lnb, *, sub, alpha):
    bsz, t, d = x.shape
    f = wg.shape[1]
    x2 = x.reshape(bsz, t * d)
    out = pl.pallas_call(
        functools.partial(_ffn_sample_kernel, sub=sub, alpha=alpha, t_len=t, d=d),
        grid=(1,),
        in_specs=[_const_spec((bsz, t * d)), _sub_mod_spec(bsz, d, sub),
                  _const_spec((d, f)), _const_spec((d, f)), _const_spec((f, d)),
                  _const_spec(lng.shape), _const_spec(lnb.shape)],
        out_specs=pl.BlockSpec((bsz, t * d), lambda i: (0, 0)),
        out_shape=jax.ShapeDtypeStruct((bsz, t * d), F32),
        compiler_params=pltpu.CompilerParams(dimension_semantics=("arbitrary",),
                                             vmem_limit_bytes=V7X_VMEM_LIMIT),
        name=f"ffn_sample{sub}",
    )(x2, mod, wg, wu, wd, lng, lnb)
    return out.reshape(bsz, t, d)


def _tri_inverse(lmats, c):
    ls = [l.astype(BF16) for l in lmats]
    ms = [jnp.dot(l, l, preferred_element_type=F32) for l in ls]
    qs = ms
    power = 2
    while 2 * power < c:
        mb = [m.astype(BF16) for m in ms]
        ms = [jnp.dot(m, m, preferred_element_type=F32) for m in mb]
        qm = [jnp.dot(q.astype(BF16), m.astype(BF16), preferred_element_type=F32) for q, m in zip(qs, ms)]
        qs = [q + m + x for q, m, x in zip(qs, ms, qm)]
        power *= 2
    rows = lax.broadcasted_iota(jnp.int32, (c, c), 0)
    cols = lax.broadcasted_iota(jnp.int32, (c, c), 1)
    eye = (rows == cols).astype(F32)
    lq = [jnp.dot(l, q.astype(BF16), preferred_element_type=F32) for l, q in zip(ls, qs)]
    return [eye - l + q - x for l, q, x in zip(lmats, qs, lq)]


def _gdn_chunks_local(units, tril, strict):
    c = units[0][1].shape[0]
    a = [lax.dot_general(jnp.concatenate([kb, q], axis=0), k, (((1,), (1,)), ((), ())),
                         preferred_element_type=F32)
         for kb, q, k, _, _, _, _ in units]
    lmats, qks = [], []
    for a_u, (_, _, _, _, _, gcol, grow) in zip(a, units):
        decay = jnp.where(tril, jnp.exp(jnp.where(tril, gcol - grow, 0.0)), 0.0)
        lmats.append(jnp.where(strict, a_u[:c] * decay, 0.0))
        qks.append(jnp.where(tril, a_u[c:] * decay, 0.0))
    tms = _tri_inverse(lmats, c)
    uw = [jnp.dot(tm.astype(BF16), jnp.concatenate([vb, kbe], axis=1), preferred_element_type=F32)
          for tm, (_, _, _, vb, kbe, _, _) in zip(tms, units)]
    return [(x[:, :DN_DV], x[:, DN_DV:], qk) for x, qk in zip(uw, qks)]


def _gdn_chunk_state(units, states):
    c = units[0][2].shape[0]
    r = [jnp.dot(jnp.concatenate([w, q_dec], axis=0), s.astype(BF16), preferred_element_type=F32)
         for (w, q_dec, _, _, _, _), s in zip(units, states)]
    v_new = [(u - r_h[:c]).astype(BF16) for (_, _, u, _, _, _), r_h in zip(units, r)]
    o = [r_h[c:] + jnp.dot(qk, v, preferred_element_type=F32)
         for (_, _, _, qk, _, _), r_h, v in zip(units, r, v_new)]
    s_new = [s * g_last + lax.dot_general(k_dec, v, (((0,), (0,)), ((), ())), preferred_element_type=F32)
             for (_, _, _, _, k_dec, g_last), s, v in zip(units, states, v_new)]
    return o, s_new


def _gate_norm(o, og, norm_g):
    o = o * lax.rsqrt(jnp.mean(o * o, axis=-1, keepdims=True) + RMS_EPS) * norm_g
    return o * _silu(og)


def _proj_layout(dn_width, sc_width):
    off = {}
    pos = 0
    for name, width in (("og", dn_width), ("sc_b", sc_width), ("sc_c", sc_width), ("sc_h", sc_width)):
        off[name] = (pos, pos + width)
        pos += width
    return off


def _project(u, wq_ref, wr_ref, wab_ref, off):
    p = {"qkv": jnp.dot(u, wq_ref[...], preferred_element_type=F32),
         "ab": jnp.dot(u, wab_ref[...], preferred_element_type=F32)}
    rest = jnp.dot(u, wr_ref[...], preferred_element_type=F32)
    for name, (lo, hi) in off.items():
        p[name] = rest[:, lo:hi]
    return p


def _cols(p, off, name):
    del off
    return p[name]


def _mix_out(x, o_dn, o_sc, gate, w_out_ref, g, b, alpha):
    dn_w = o_dn.shape[1]
    mix = (jnp.dot(o_dn.astype(BF16), w_out_ref[0:dn_w, :], preferred_element_type=F32)
           + jnp.dot(o_sc.astype(BF16), w_out_ref[dn_w:, :], preferred_element_type=F32))
    return _post(x, mix, gate, g, b, alpha)


def _mixer_prompt_kernel(x_ref, mod_ref, wq_ref, wr_ref, wab_ref, w_out_ref, cw_ref, mw_ref, alog_ref, dtb_ref, ng_ref,
                         lng_ref, lnb_ref,
                         y_ref, s_out_ref, cq_out_ref, cm_out_ref,
                         s_scr, cq_scr, cm_scr, gct_scr, gl_scr, ops_scr, u_scr, w_scr, qk_scr, o_scr,
                         *, alpha, nseq, tt, off, n_conv, n_mconv):
    t_idx = pl.program_id(1)
    rows_all = nseq * tt
    nchunk = rows_all // CHUNK
    per_seq = tt // CHUNK
    hk = DN_HEADS * DN_DK
    sub = 1

    @pl.when(t_idx == 0)
    def _():
        s_scr[...] = jnp.zeros_like(s_scr)
        cq_scr[...] = jnp.zeros_like(cq_scr)
        cm_scr[...] = jnp.zeros_like(cm_scr)

    xs = [x_ref[s] for s in range(nseq)]
    gates = [mod_ref[s, 3 * sub + 2:3 * sub + 3, :] for s in range(nseq)]
    u = jnp.concatenate(
        [(xs[s] * (1.0 + mod_ref[s, 3 * sub + 1:3 * sub + 2, :]) + mod_ref[s, 3 * sub + 0:3 * sub + 1, :])
         .astype(BF16) for s in range(nseq)], axis=0)
    p = _project(u, wq_ref, wr_ref, wab_ref, off)

    def causal_conv(new_rows, hist_scr, s, w_ref, taps):
        ext = jnp.concatenate([hist_scr[s], new_rows], axis=0)
        acc = new_rows * w_ref[taps - 1:taps, :]
        for j in range(taps - 1):
            acc = acc + pltpu.roll(ext, taps - 1 - j, axis=0)[SUBLANES:] * w_ref[j:j + 1, :]
        hist_scr[s] = new_rows[tt - SUBLANES:]
        return acc

    pq = _cols(p, off, "qkv")
    acts = []
    for s in range(nseq):
        pq_s = pq[s * tt:(s + 1) * tt]
        acts.append(_silu(causal_conv(pq_s, cq_scr, s, cw_ref, n_conv)))
        cq_out_ref[s] = pq_s[tt - (n_conv - 1):]
    act = jnp.concatenate(acts, axis=0)

    ab = _cols(p, off, "ab")
    beta_all = jax.nn.sigmoid(ab)
    g = -jnp.exp(alog_ref[...]) * _softplus(ab + dtb_ref[...])
    rows = lax.broadcasted_iota(jnp.int32, (CHUNK, CHUNK), 0)
    cols = lax.broadcasted_iota(jnp.int32, (CHUNK, CHUNK), 1)
    tril = rows >= cols
    strict = rows > cols
    tril_f = tril.astype(F32)
    gc_parts, gl_parts = [], []
    for c in range(nchunk):
        gc_c = _mm_f32(tril_f, g[c * CHUNK:(c + 1) * CHUNK, :])
        gc_parts.append(gc_c)
        gl_parts.append(jnp.broadcast_to(gc_c[CHUNK - 1:CHUNK, :], (CHUNK, AB_PAD)))
    gc = jnp.concatenate(gc_parts, axis=0)
    gl_rows = jnp.concatenate(gl_parts, axis=0)
    gc_t = gc.T
    for c in range(nchunk):
        gct_scr[c] = gc_t[0:SUBLANES, c * CHUNK:(c + 1) * CHUNK]

    eg_all = jnp.exp(gc)
    ekd_all = jnp.exp(gl_rows - gc)
    egl_all = jnp.exp(gl_rows)
    for h in range(DN_HEADS):
        lanes = slice(h * DN_DK, (h + 1) * DN_DK)
        q = _l2norm(act[:, h * DN_DK:(h + 1) * DN_DK]) * (DN_DK ** -0.5)
        k = _l2norm(act[:, hk + h * DN_DK:hk + (h + 1) * DN_DK])
        v = act[:, 2 * hk + h * DN_DV:2 * hk + (h + 1) * DN_DV]
        beta = beta_all[:, DN_HEADS + h:DN_HEADS + h + 1]
        eg = eg_all[:, h:h + 1]
        kb = k * beta
        ops_scr[0, :, lanes] = kb.astype(BF16)
        ops_scr[1, :, lanes] = q.astype(BF16)
        ops_scr[2, :, lanes] = k.astype(BF16)
        ops_scr[3, :, lanes] = (v * beta).astype(BF16)
        ops_scr[4, :, lanes] = (kb * eg).astype(BF16)
        ops_scr[5, :, lanes] = (q * eg).astype(BF16)
        ops_scr[6, :, lanes] = (k * ekd_all[:, h:h + 1]).astype(BF16)
        gl_scr[:, lanes] = jnp.broadcast_to(egl_all[:, h:h + 1], (rows_all, DN_DV))

    head_lanes = [slice(h * DN_DK, (h + 1) * DN_DK) for h in range(DN_HEADS)]

    where = [(c, h, lanes) for c in range(nchunk) for h, lanes in enumerate(head_lanes)]
    units = [tuple(ops_scr[i, c * CHUNK:(c + 1) * CHUNK, lanes] for i in range(5))
             + (gc[c * CHUNK:(c + 1) * CHUNK, h:h + 1], gct_scr[c, h:h + 1, :])
             for c, h, lanes in where]
    for (c, h, lanes), (u_c, w_c, qk_c) in zip(where, _gdn_chunks_local(units, tril, strict)):
        u_scr[c * CHUNK:(c + 1) * CHUNK, lanes] = u_c
        w_scr[c * CHUNK:(c + 1) * CHUNK, lanes] = w_c.astype(BF16)
        qk_scr[c * DN_HEADS + h] = qk_c.astype(BF16)

    for j in range(per_seq):
        chains = [(s, s * per_seq + j, h, lanes) for s in range(nseq) for h, lanes in enumerate(head_lanes)]
        units = [(w_scr[c * CHUNK:(c + 1) * CHUNK, lanes], ops_scr[5, c * CHUNK:(c + 1) * CHUNK, lanes],
                  u_scr[c * CHUNK:(c + 1) * CHUNK, lanes], qk_scr[c * DN_HEADS + h],
                  ops_scr[6, c * CHUNK:(c + 1) * CHUNK, lanes], gl_scr[c * CHUNK:c * CHUNK + 1, lanes])
                 for s, c, h, lanes in chains]
        o_c, s_new = _gdn_chunk_state(units, [s_scr[s * DN_HEADS + h] for s, _, h, _ in chains])
        for (s, c, h, lanes), o_u, s_u in zip(chains, o_c, s_new):
            s_scr[s * DN_HEADS + h] = s_u
            o_scr[c * CHUNK:(c + 1) * CHUNK, lanes] = o_u
    for s in range(nseq):
        s_out_ref[s] = s_scr[s * DN_HEADS:(s + 1) * DN_HEADS]

    og = _cols(p, off, "og")
    o_dn = jnp.concatenate(
        [_gate_norm(o_scr[:, h * DN_DV:(h + 1) * DN_DV], og[:, h * DN_DV:(h + 1) * DN_DV], ng_ref[...])
         for h in range(DN_HEADS)], axis=1)

    z = _cols(p, off, "sc_c") * _cols(p, off, "sc_h")
    zcs = []
    for s in range(nseq):
        z_s = z[s * tt:(s + 1) * tt]
        zcs.append(causal_conv(z_s, cm_scr, s, mw_ref, n_mconv))
        cm_out_ref[s] = z_s[tt - (n_mconv - 1):]
    o_sc = _cols(p, off, "sc_b") * jnp.concatenate(zcs, axis=0)

    dn_w = o_dn.shape[1]
    mix = (jnp.dot(o_dn.astype(BF16), w_out_ref[0:dn_w, :], preferred_element_type=F32)
           + jnp.dot(o_sc.astype(BF16), w_out_ref[dn_w:, :], preferred_element_type=F32))
    for s in range(nseq):
        y_ref[s] = _post(xs[s], mix[s * tt:(s + 1) * tt], gates[s],
                         lng_ref[sub:sub + 1, :], lnb_ref[sub:sub + 1, :], alpha)


def _mixer_prompt(x, mod, w_in, w_out, conv_w, mconv_w, alog, dtb, norm_g, lng, lnb, *, alpha, nseq, tt, off):
    bsz, t, d = x.shape
    assert t % tt == 0 and tt % LANES == 0 and bsz % nseq == 0
    nt = t // tt
    rows = nseq * tt
    qkv_dim = conv_w.shape[-1]
    sc_w = off["sc_c"][1] - off["sc_c"][0]
    dn_w = DN_HEADS * DN_DV
    n_conv, n_mconv = conv_w.shape[0], mconv_w.shape[0]
    assert max(n_conv, n_mconv) - 1 <= SUBLANES <= tt
    nchunk = rows // CHUNK
    kern = functools.partial(_mixer_prompt_kernel, alpha=alpha, nseq=nseq, tt=tt, off=off,
                             n_conv=n_conv, n_mconv=n_mconv)
    y, s_out, cq, cm = pl.pallas_call(
        kern,
        grid=(bsz // nseq, nt),
        in_specs=[pl.BlockSpec((nseq, tt, d), lambda b, i: (b, i, 0)),
                  pl.BlockSpec((nseq, 3 * N_SUB, d), lambda b, i: (b, 0, 0)),
                  *[_const_spec(w.shape) for w in w_in], _const_spec(w_out.shape),
                  _const_spec(conv_w.shape), _const_spec(mconv_w.shape),
                  _const_spec(alog.shape), _const_spec(dtb.shape), _const_spec(norm_g.shape),
                  _const_spec(lng.shape), _const_spec(lnb.shape)],
        out_specs=[pl.BlockSpec((nseq, tt, d), lambda b, i: (b, i, 0)),
                   pl.BlockSpec((nseq, DN_HEADS, DN_DK, DN_DV), lambda b, i: (b, 0, 0, 0)),
                   pl.BlockSpec((nseq, n_conv - 1, qkv_dim), lambda b, i: (b, 0, 0)),
                   pl.BlockSpec((nseq, n_mconv - 1, sc_w), lambda b, i: (b, 0, 0))],
        out_shape=[jax.ShapeDtypeStruct((bsz, t, d), F32),
                   jax.ShapeDtypeStruct((bsz, DN_HEADS, DN_DK, DN_DV), F32),
                   jax.ShapeDtypeStruct((bsz, n_conv - 1, qkv_dim), F32),
                   jax.ShapeDtypeStruct((bsz, n_mconv - 1, sc_w), F32)],
        scratch_shapes=[pltpu.VMEM((nseq * DN_HEADS, DN_DK, DN_DV), F32),
                        pltpu.VMEM((nseq, SUBLANES, qkv_dim), F32),
                        pltpu.VMEM((nseq, SUBLANES, sc_w), F32),
                        pltpu.VMEM((nchunk, SUBLANES, CHUNK), F32),
                        pltpu.VMEM((rows, dn_w), F32),
                        pltpu.VMEM((7, rows, dn_w), BF16),
                        pltpu.VMEM((rows, dn_w), F32),
                        pltpu.VMEM((rows, dn_w), BF16),
                        pltpu.VMEM((nchunk * DN_HEADS, CHUNK, CHUNK), BF16),
                        pltpu.VMEM((rows, dn_w), F32)],
        compiler_params=pltpu.CompilerParams(dimension_semantics=("arbitrary", "arbitrary"),
                                             vmem_limit_bytes=V7X_VMEM_LIMIT),
        name="mixer_prompt",
    )(x, mod, *w_in, w_out, conv_w, mconv_w, alog, dtb, norm_g, lng, lnb)
    return y, s_out, cq, cm


_PACK = SUBLANES


def _sample_phase_a(x_ref, mod_ref, wq_ref, wr_ref, wab_ref, cw_ref, mw_ref, alog_ref, dtb_ref, cq_in_ref, cm_in_ref,
                    cq_out_ref, cm_out_ref, og_scr, osc_scr, lhs_scr, u_scr, kd_scr, gl_scr, qk_scr,
                    *, t_len, d, off, n_conv, n_mconv):
    bsz = x_ref.shape[0]
    hk = DN_HEADS * DN_DK
    qkv_dim = cw_ref.shape[-1]
    sc_w = off["sc_c"][1] - off["sc_c"][0]

    x = _stack_time(x_ref, t_len, d)
    u = (x * (1.0 + _mod_rows(mod_ref, 1, t_len, d)) + _mod_rows(mod_ref, 0, t_len, d)).astype(BF16)
    p = _project(u, wq_ref, wr_ref, wab_ref, off)

    def rows(arr, t):
        return arr[t * bsz:(t + 1) * bsz]

    pq = _cols(p, off, "qkv")
    seq = ([cq_in_ref[:, j * qkv_dim:(j + 1) * qkv_dim] for j in range(n_conv - 1)]
           + [rows(pq, t) for t in range(t_len)])
    act = []
    for t in range(t_len):
        acc = None
        for j in range(n_conv):
            term = seq[t + j] * cw_ref[j:j + 1, :]
            acc = term if acc is None else acc + term
        act.append(_silu(acc))
    for j in range(n_conv - 1):
        cq_out_ref[:, j * qkv_dim:(j + 1) * qkv_dim] = seq[t_len + j]

    z = _cols(p, off, "sc_c") * _cols(p, off, "sc_h")
    zseq = ([cm_in_ref[:, j * sc_w:(j + 1) * sc_w] for j in range(n_mconv - 1)]
            + [rows(z, t) for t in range(t_len)])
    zc = []
    for t in range(t_len):
        acc = None
        for j in range(n_mconv):
            term = zseq[t + j] * mw_ref[j:j + 1, :]
            acc = term if acc is None else acc + term
        zc.append(acc)
    for j in range(n_mconv - 1):
        cm_out_ref[:, j * sc_w:(j + 1) * sc_w] = zseq[t_len + j]
    osc_scr[...] = _cols(p, off, "sc_b") * jnp.concatenate(zc, axis=0)
    og_scr[...] = _cols(p, off, "og")

    ab = _cols(p, off, "ab")
    beta_all = jax.nn.sigmoid(ab)
    g_all = -jnp.exp(alog_ref[...]) * _softplus(ab + dtb_ref[...])
    zero = jnp.zeros((bsz, LANES), F32)

    for h in range(DN_HEADS):
        q = [_l2norm(a[:, h * DN_DK:(h + 1) * DN_DK]) * (DN_DK ** -0.5) for a in act]
        k = [_l2norm(a[:, hk + h * DN_DK:hk + (h + 1) * DN_DK]) for a in act]
        v = [a[:, 2 * hk + h * DN_DV:2 * hk + (h + 1) * DN_DV] for a in act]
        beta = [rows(beta_all, t)[:, DN_HEADS + h:DN_HEADS + h + 1] for t in range(t_len)]
        g = [rows(g_all, t)[:, h:h + 1] for t in range(t_len)]
        gc = [g[0]]
        for t in range(1, t_len):
            gc.append(gc[-1] + g[t])
        kb = [k[t] * beta[t] for t in range(t_len)]
        vb = [v[t] * beta[t] for t in range(t_len)]
        eg = [jnp.exp(gc[t]) for t in range(t_len)]
        kbe = [kb[t] * eg[t] for t in range(t_len)]
        lm = [[None] * t_len for _ in range(t_len)]
        n_qk = 0
        for i in range(t_len):
            for j in range(i + 1):
                dec = jnp.exp(gc[i] - gc[j])
                qk_scr[h, n_qk] = jnp.broadcast_to(_rowsum(q[i] * k[j]) * dec, (bsz, LANES))
                n_qk += 1
                if j < i:
                    lm[i][j] = _rowsum(kb[i] * k[j]) * dec
        tm = [[None] * t_len for _ in range(t_len)]
        for i in range(t_len):
            for j in range(i):
                acc = lm[i][j]
                for m in range(j + 1, i):
                    acc = acc + lm[i][m] * tm[m][j]
                tm[i][j] = -acc
        for i in range(t_len):
            ui, wi = vb[i], kbe[i]
            for j in range(i):
                ui = ui + tm[i][j] * vb[j]
                wi = wi + tm[i][j] * kbe[j]
            lhs_scr[h, pl.ds(i, bsz, stride=_PACK), :] = wi
            lhs_scr[h, pl.ds(t_len + i, bsz, stride=_PACK), :] = q[i] * eg[i]
            u_scr[h, pl.ds(i, bsz, stride=_PACK), :] = ui
            u_scr[h, pl.ds(t_len + i, bsz, stride=_PACK), :] = zero
            kd_scr[h, pl.ds(i, bsz, stride=_PACK), :] = k[i] * jnp.exp(gc[t_len - 1] - gc[i])
            kd_scr[h, pl.ds(t_len + i, bsz, stride=_PACK), :] = zero
        gl_scr[h] = jnp.broadcast_to(jnp.exp(gc[t_len - 1]), (bsz, LANES))


def _sample_phase_c(x_ref, mod_ref, w_out_ref, ng_ref, lng_ref, lnb_ref, y_ref,
                    og_scr, osc_scr, res_scr, vn_scr, qk_scr, *, alpha, t_len, d):
    bsz = x_ref.shape[0]
    sub = 1
    x = _stack_time(x_ref, t_len, d)
    gate = _mod_rows(mod_ref, 2, t_len, d)
    og = og_scr[...]
    per_t = []
    for i in range(t_len):
        heads = []
        for h in range(DN_HEADS):
            oi = res_scr[h, pl.ds(t_len + i, bsz, stride=_PACK), :]
            base = i * (i + 1) // 2
            for j in range(i + 1):
                oi = oi + qk_scr[h, base + j] * vn_scr[h, pl.ds(j, bsz, stride=_PACK), :]
            heads.append(_gate_norm(oi, og[i * bsz:(i + 1) * bsz, h * DN_DV:(h + 1) * DN_DV], ng_ref[...]))
        per_t.append(jnp.concatenate(heads, axis=1))
    o_dn = jnp.concatenate(per_t, axis=0)
    y = _mix_out(x, o_dn, osc_scr[...], gate, w_out_ref,
                 lng_ref[sub:sub + 1, :], lnb_ref[sub:sub + 1, :], alpha)
    for t in range(t_len):
        y_ref[:, t * d:(t + 1) * d] = y[t * bsz:(t + 1) * bsz]


def _mixer_sample_kernel(x_ref, mod_ref, wq_ref, wr_ref, wab_ref, w_out_ref, cw_ref, mw_ref, alog_ref, dtb_ref, ng_ref,
                         lng_ref, lnb_ref, cq_in_ref, cm_in_ref, s_in_ref,
                         y_ref, cq_out_ref, cm_out_ref, s_out_ref,
                         og_scr, osc_scr, lhs_scr, u_scr, kd_scr, gl_scr, qk_scr, res_scr, vn_scr,
                         *, alpha, t_len, d, off, n_conv, n_mconv, bt):
    step = pl.program_id(0)
    assert 2 * t_len == _PACK, "packed buffers hold [T rows | T rows] per sequence"

    @pl.when(step == 0)
    def _():
        _sample_phase_a(x_ref, mod_ref, wq_ref, wr_ref, wab_ref, cw_ref, mw_ref, alog_ref, dtb_ref, cq_in_ref, cm_in_ref,
                        cq_out_ref, cm_out_ref, og_scr, osc_scr, lhs_scr, u_scr, kd_scr, gl_scr, qk_scr,
                        t_len=t_len, d=d, off=off, n_conv=n_conv, n_mconv=n_mconv)

    top = (lax.broadcasted_iota(jnp.int32, (_PACK, LANES), 0) < t_len).astype(F32)

    chains = [(bl, h) for bl in range(bt) for h in range(DN_HEADS)]
    seq0 = step * bt
    row0 = [pl.multiple_of((seq0 + bl) * _PACK, _PACK) for bl in range(bt)]
    states = [s_in_ref[bl, h] for bl, h in chains]
    r = [_mm(lhs_scr[h, pl.ds(row0[bl], _PACK), :], s) for (bl, h), s in zip(chains, states)]
    vns = []
    for (bl, h), r_u in zip(chains, r):
        res_scr[h, pl.ds(row0[bl], _PACK), :] = r_u
        vn = (u_scr[h, pl.ds(row0[bl], _PACK), :] - r_u) * top
        vn_scr[h, pl.ds(row0[bl], _PACK), :] = vn
        vns.append(vn)
    upd = [_mm_at(kd_scr[h, pl.ds(row0[bl], _PACK), :], vn) for (bl, h), vn in zip(chains, vns)]
    for (bl, h), s, s_add in zip(chains, states, upd):
        s_out_ref[bl, h] = s * gl_scr[h, pl.ds(seq0 + bl, 1), :] + s_add

    @pl.when(step == pl.num_programs(0) - 1)
    def _():
        _sample_phase_c(x_ref, mod_ref, w_out_ref, ng_ref, lng_ref, lnb_ref, y_ref,
                        og_scr, osc_scr, res_scr, vn_scr, qk_scr, alpha=alpha, t_len=t_len, d=d)


def _mixer_sample(x, mod, w_in, w_out, conv_w, mconv_w, alog, dtb, norm_g, lng, lnb, s_in, cq_in, cm_in,
                  *, alpha, off, bt):
    bsz, t, d = x.shape
    assert bsz % bt == 0
    qkv_dim = conv_w.shape[-1]
    sc_w = off["sc_c"][1] - off["sc_c"][0]
    dn_w = DN_HEADS * DN_DV
    n_conv, n_mconv = conv_w.shape[0], mconv_w.shape[0]
    n_qk = t * (t + 1) // 2
    x2 = x.reshape(bsz, t * d)
    cq2 = cq_in.reshape(bsz, (n_conv - 1) * qkv_dim)
    cm2 = cm_in.reshape(bsz, (n_mconv - 1) * sc_w)
    kern = functools.partial(_mixer_sample_kernel, alpha=alpha, t_len=t, d=d, off=off,
                             n_conv=n_conv, n_mconv=n_mconv, bt=bt)
    consts = (*w_in, w_out, conv_w, mconv_w, alog, dtb, norm_g, lng, lnb, cq2, cm2)
    state_spec = pl.BlockSpec((bt, DN_HEADS, DN_DK, DN_DV), lambda i: (i, 0, 0, 0))
    packed = pltpu.VMEM((DN_HEADS, _PACK * bsz, LANES), F32)
    y, cq, cm, s_out = pl.pallas_call(
        kern,
        grid=(bsz // bt,),
        in_specs=([_const_spec(x2.shape), _sub_mod_spec(bsz, d, 1)]
                  + [_const_spec(a.shape) for a in consts] + [state_spec]),
        out_specs=[pl.BlockSpec(x2.shape, lambda i: (0, 0)),
                   pl.BlockSpec(cq2.shape, lambda i: (0, 0)),
                   pl.BlockSpec(cm2.shape, lambda i: (0, 0)),
                   state_spec],
        out_shape=[jax.ShapeDtypeStruct(x2.shape, F32),
                   jax.ShapeDtypeStruct(cq2.shape, F32),
                   jax.ShapeDtypeStruct(cm2.shape, F32),
                   jax.ShapeDtypeStruct(s_in.shape, F32)],
        scratch_shapes=[pltpu.VMEM((t * bsz, dn_w), F32),
                        pltpu.VMEM((t * bsz, sc_w), F32),
                        packed,
                        packed,
                        packed,
                        pltpu.VMEM((DN_HEADS, bsz, LANES), F32),
                        pltpu.VMEM((DN_HEADS, n_qk, bsz, LANES), F32),
                        packed,
                        packed],
        compiler_params=pltpu.CompilerParams(dimension_semantics=("arbitrary",),
                                             vmem_limit_bytes=V7X_VMEM_LIMIT),
        name="mixer_sample",
    )(x2, mod, *consts, s_in)
    return (y.reshape(bsz, t, d), s_out, cq.reshape(bsz, n_conv - 1, qkv_dim),
            cm.reshape(bsz, n_mconv - 1, sc_w))


def _split_w_in(w_in, qkv_dim):
    c1 = qkv_dim + 2 * DN_HEADS
    wab = jnp.pad(w_in[:, qkv_dim:c1], ((0, 0), (0, AB_PAD - 2 * DN_HEADS)))
    return w_in[:, :qkv_dim].astype(BF16), w_in[:, c1:].astype(BF16), wab.astype(BF16)


def _pad_lanes(v):
    return jnp.zeros((1, AB_PAD), F32).at[0, :v.shape[0]].set(v)


def kernel(x_prompt, x_sample, state_ssm, state_conv_qkv, state_conv_mix, c_prompt, c_sample, w_ada, b_ada, ln_g, ln_b, ffn1_wg, ffn1_wu, ffn1_wd, ffn2_wg, ffn2_wu, ffn2_wd, w_in, conv_qkv_w, a_log, dt_bias, dn_norm_g, conv_mix_w, w_out):
    depth = w_ada.shape[0]
    alpha = (2 * depth) ** 0.25
    bp, tp, d = x_prompt.shape
    qkv_dim = conv_qkv_w.shape[-1]
    sc_width = conv_mix_w.shape[-1]
    dn_width = DN_HEADS * DN_DV
    off = _proj_layout(dn_width, sc_width)
    tm = min(1024, tp)
    ffn_parts = 2 if tm % (2 * SUBLANES) == 0 else 1
    nseq = 2 if bp % 2 == 0 else 1
    tt = min(512 // nseq, tp)
    bt = min(8, x_sample.shape[0])

    hp, hs = x_prompt, x_sample
    outs = [[] for _ in range(6)]
    for l in range(depth):
        c_all = jnp.concatenate([c_prompt, c_sample], axis=0)
        mod = _adaln(c_all, w_ada[l], b_ada[l].reshape(1, -1), tn=d)
        mod_p = mod[:bp].reshape(bp, 3 * N_SUB, d)
        mod_s = mod[bp:]
        lng, lnb = ln_g[l], ln_b[l]
        w1 = (ffn1_wg[l].astype(BF16), ffn1_wu[l].astype(BF16), ffn1_wd[l].astype(BF16))
        w2 = (ffn2_wg[l].astype(BF16), ffn2_wu[l].astype(BF16), ffn2_wd[l].astype(BF16))
        w_in_p = _split_w_in(w_in[l], qkv_dim)
        w_out_b = w_out[l].astype(BF16)
        alog, dtb = _pad_lanes(a_log[l]), _pad_lanes(dt_bias[l])
        norm_g = dn_norm_g[l].reshape(1, -1)
        mix_args = (w_in_p, w_out_b, conv_qkv_w[l], conv_mix_w[l], alog, dtb, norm_g, lng, lnb)

        hp = _ffn_prompt(hp, mod_p, *w1, lng, lnb, sub=0, alpha=alpha, tm=tm, parts=ffn_parts)
        hp, a1, a2, a3 = _mixer_prompt(hp, mod_p, *mix_args, alpha=alpha, nseq=nseq, tt=tt, off=off)
        hp = _ffn_prompt(hp, mod_p, *w2, lng, lnb, sub=2, alpha=alpha, tm=tm, parts=ffn_parts)

        hs = _ffn_sample(hs, mod_s, *w1, lng, lnb, sub=0, alpha=alpha)
        hs, b1, b2, b3 = _mixer_sample(hs, mod_s, *mix_args, state_ssm[l], state_conv_qkv[l],
                                       state_conv_mix[l], alpha=alpha, off=off, bt=bt)
        hs = _ffn_sample(hs, mod_s, *w2, lng, lnb, sub=2, alpha=alpha)
        for lst, val in zip(outs, (a1, a2, a3, b1, b2, b3)):
            lst.append(val)
    return (hp, hs) + tuple(jnp.stack(o) for o in outs)
```

```python
import functools

import jax
import jax.numpy as jnp
from jax import lax
from jax.experimental import pallas as pl
from jax.experimental.pallas import tpu as pltpu

F32 = jnp.float32
BF16 = jnp.bfloat16

LN_EPS = 1e-5
RMS_EPS = 1e-6
N_SUB = 3
DN_HEADS = 4
DN_DK = 128
DN_DV = 128
CHUNK = 64
LANES = 128
SUBLANES = 8
AB_PAD = LANES

V7X_VMEM_LIMIT = 60000 * 1024


def _mm(a, b):
    return jnp.dot(a.astype(BF16), b.astype(BF16), preferred_element_type=F32)


def _mm_bt(a, b):
    return lax.dot_general(a.astype(BF16), b.astype(BF16), (((1,), (1,)), ((), ())),
                           preferred_element_type=F32)


def _mm_at(a, b):
    return lax.dot_general(a.astype(BF16), b.astype(BF16), (((0,), (0,)), ((), ())),
                           preferred_element_type=F32)


def _mm_f32(a, b):
    return jnp.dot(a, b, preferred_element_type=F32, precision=lax.Precision.HIGHEST)


def _silu(x):
    return x * jax.nn.sigmoid(x)


def _softplus(x):
    return jnp.maximum(x, 0.0) + jnp.log1p(jnp.exp(-jnp.abs(x)))


def _layer_norm(y, g, b):
    mu = jnp.mean(y, axis=-1, keepdims=True)
    yc = y - mu
    var = jnp.mean(yc * yc, axis=-1, keepdims=True)
    return yc * lax.rsqrt(var + LN_EPS) * g + b


def _post(x, delta, gate, g, b, alpha):
    return _layer_norm(alpha * x + gate * delta, g, b)


def _rowsum(x):
    return jnp.sum(x, axis=-1, keepdims=True)


def _l2norm(x):
    return x * lax.rsqrt(_rowsum(x * x) + RMS_EPS)


def _const_spec(shape):
    nd = len(shape)
    return pl.BlockSpec(shape, lambda *_: (0,) * nd, pipeline_mode=pl.Buffered(1))


def _adaln_kernel(c_ref, w_ref, b_ref, o_ref):
    s = _silu(c_ref[...])
    o_ref[...] = _mm(s, w_ref[...]) + b_ref[...]


def _adaln(c, w, b, tn=1024):
    m, d = c.shape
    n = w.shape[1]
    assert n % tn == 0
    return pl.pallas_call(
        _adaln_kernel,
        grid=(n // tn,),
        in_specs=[_const_spec((m, d)),
                  pl.BlockSpec((d, tn), lambda j: (0, j)),
                  pl.BlockSpec((1, tn), lambda j: (0, j))],
        out_specs=pl.BlockSpec((m, tn), lambda j: (0, j)),
        out_shape=jax.ShapeDtypeStruct((m, n), F32),
        compiler_params=pltpu.CompilerParams(dimension_semantics=("arbitrary",)),
        name="adaln",
    )(c, w, b)


def _ffn_rows(x, shift, scale, gate, wg_ref, wu_ref, wd_ref, g, b, alpha):
    u = (x * (1.0 + scale) + shift).astype(BF16)
    hg = jnp.dot(u, wg_ref[...], preferred_element_type=F32)
    hu = jnp.dot(u, wu_ref[...], preferred_element_type=F32)
    h = (_silu(hg) * hu).astype(BF16)
    d = jnp.dot(h, wd_ref[...], preferred_element_type=F32)
    return _post(x, 0.5 * d, gate, g, b, alpha)


def _ffn_prompt_kernel(x_ref, mod_ref, wg_ref, wu_ref, wd_ref, lng_ref, lnb_ref, o_ref, *, sub, alpha, parts):
    shift = mod_ref[0, 3 * sub + 0:3 * sub + 1, :]
    scale = mod_ref[0, 3 * sub + 1:3 * sub + 2, :]
    gate = mod_ref[0, 3 * sub + 2:3 * sub + 3, :]
    rows = x_ref.shape[0] // parts
    for part in range(parts):
        sl = slice(part * rows, (part + 1) * rows)
        o_ref[sl, :] = _ffn_rows(x_ref[sl, :], shift, scale, gate, wg_ref, wu_ref, wd_ref,
                                 lng_ref[sub:sub + 1, :], lnb_ref[sub:sub + 1, :], alpha)


def _ffn_prompt(x, mod, wg, wu, wd, lng, lnb, *, sub, alpha, tm, parts):
    bsz, t, d = x.shape
    f = wg.shape[1]
    assert t % tm == 0 and tm % (parts * SUBLANES) == 0
    per_seq = t // tm
    x2 = x.reshape(bsz * t, d)
    out = pl.pallas_call(
        functools.partial(_ffn_prompt_kernel, sub=sub, alpha=alpha, parts=parts),
        grid=(bsz * per_seq,),
        in_specs=[pl.BlockSpec((tm, d), lambda i: (i, 0)),
                  pl.BlockSpec((1, 3 * N_SUB, d), lambda i: (i // per_seq, 0, 0)),
                  _const_spec((d, f)), _const_spec((d, f)), _const_spec((f, d)),
                  _const_spec(lng.shape), _const_spec(lnb.shape)],
        out_specs=pl.BlockSpec((tm, d), lambda i: (i, 0)),
        out_shape=jax.ShapeDtypeStruct((bsz * t, d), F32),
        compiler_params=pltpu.CompilerParams(dimension_semantics=("arbitrary",),
                                             vmem_limit_bytes=V7X_VMEM_LIMIT),
        name=f"ffn_prompt{sub}",
    )(x2, mod, wg, wu, wd, lng, lnb)
    return out.reshape(bsz, t, d)


def _stack_time(ref, t_len, d):
    return jnp.concatenate([ref[:, t * d:(t + 1) * d] for t in range(t_len)], axis=0)


def _mod_rows(mod_ref, idx, t_len, d):
    m = mod_ref[:, idx * d:(idx + 1) * d]
    return jnp.concatenate([m] * t_len, axis=0)


def _ffn_sample_kernel(x_ref, mod_ref, wg_ref, wu_ref, wd_ref, lng_ref, lnb_ref, o_ref, *,
                       sub, alpha, t_len, d):
    bsz = x_ref.shape[0]
    x = _stack_time(x_ref, t_len, d)
    y = _ffn_rows(x, _mod_rows(mod_ref, 0, t_len, d), _mod_rows(mod_ref, 1, t_len, d),
                  _mod_rows(mod_ref, 2, t_len, d), wg_ref, wu_ref, wd_ref,
                  lng_ref[sub:sub + 1, :], lnb_ref[sub:sub + 1, :], alpha)
    for t in range(t_len):
        o_ref[:, t * d:(t + 1) * d] = y[t * bsz:(t + 1) * bsz]


def _sub_mod_spec(bsz, d, sub):
    return pl.BlockSpec((bsz, 3 * d), lambda *_: (0, sub), pipeline_mode=pl.Buffered(1))


def _ffn_sample(x, mod, wg, wu, wd, lng, lnb, *, sub, alpha):
    bsz, t, d = x.shape
    f = wg.shape[1]
    x2 = x.reshape(bsz, t * d)
    out = pl.pallas_call(
        functools.partial(_ffn_sample_kernel, sub=sub, alpha=alpha, t_len=t, d=d),
        grid=(1,),
        in_specs=[_const_spec((bsz, t * d)), _sub_mod_spec(bsz, d, sub),
                  _const_spec((d, f)), _const_spec((d, f)), _const_spec((f, d)),
                  _const_spec(lng.shape), _const_spec(lnb.shape)],
        out_specs=pl.BlockSpec((bsz, t * d), lambda i: (0, 0)),
        out_shape=jax.ShapeDtypeStruct((bsz, t * d), F32),
        compiler_params=pltpu.CompilerParams(dimension_semantics=("arbitrary",),
                                             vmem_limit_bytes=V7X_VMEM_LIMIT),
        name=f"ffn_sample{sub}",
    )(x2, mod, wg, wu, wd, lng, lnb)
    return out.reshape(bsz, t, d)


def _tri_inverse(lmats, c):
    ls = [l.astype(BF16) for l in lmats]
    ms = [jnp.dot(l, l, preferred_element_type=F32) for l in ls]
    qs = ms
    power = 2
    while 2 * power < c:
        mb = [m.astype(BF16) for m in ms]
        ms = [jnp.dot(m, m, preferred_element_type=F32) for m in mb]
        qm = [jnp.dot(q.astype(BF16), m.astype(BF16), preferred_element_type=F32) for q, m in zip(qs, ms)]
        qs = [q + m + x for q, m, x in zip(qs, ms, qm)]
        power *= 2
    rows = lax.broadcasted_iota(jnp.int32, (c, c), 0)
    cols = lax.broadcasted_iota(jnp.int32, (c, c), 1)
    eye = (rows == cols).astype(F32)
    lq = [jnp.dot(l, q.astype(BF16), preferred_element_type=F32) for l, q in zip(ls, qs)]
    return [eye - l + q - x for l, q, x in zip(lmats, qs, lq)]


def _gdn_chunks_local(units, tril, strict):
    c = units[0][1].shape[0]
    a = [lax.dot_general(jnp.concatenate([kb, q], axis=0), k, (((1,), (1,)), ((), ())),
                         preferred_element_type=F32)
         for kb, q, k, _, _, _, _ in units]
    lmats, qks = [], []
    for a_u, (_, _, _, _, _, gcol, grow) in zip(a, units):
        decay = jnp.where(tril, jnp.exp(jnp.where(tril, gcol - grow, 0.0)), 0.0)
        lmats.append(jnp.where(strict, a_u[:c] * decay, 0.0))
        qks.append(jnp.where(tril, a_u[c:] * decay, 0.0))
    tms = _tri_inverse(lmats, c)
    uw = [jnp.dot(tm.astype(BF16), jnp.concatenate([vb, kbe], axis=1), preferred_element_type=F32)
          for tm, (_, _, _, vb, kbe, _, _) in zip(tms, units)]
    return [(x[:, :DN_DV], x[:, DN_DV:], qk) for x, qk in zip(uw, qks)]


def _gdn_chunk_state(units, states):
    c = units[0][2].shape[0]
    r = [jnp.dot(jnp.concatenate([w, q_dec], axis=0), s.astype(BF16), preferred_element_type=F32)
         for (w, q_dec, _, _, _, _), s in zip(units, states)]
    v_new = [(u - r_h[:c]).astype(BF16) for (_, _, u, _, _, _), r_h in zip(units, r)]
    o = [r_h[c:] + jnp.dot(qk, v, preferred_element_type=F32)
         for (_, _, _, qk, _, _), r_h, v in zip(units, r, v_new)]
    s_new = [s * g_last + lax.dot_general(k_dec, v, (((0,), (0,)), ((), ())), preferred_element_type=F32)
             for (_, _, _, _, k_dec, g_last), s, v in zip(units, states, v_new)]
    return o, s_new


def _tri_inverse_steps(lmats, c):
    ls = [l.astype(BF16) for l in lmats]
    ms = [jnp.dot(l, l, preferred_element_type=F32) for l in ls]
    yield
    qs = ms
    power = 2
    while 2 * power < c:
        mb = [m.astype(BF16) for m in ms]
        ms = [jnp.dot(m, m, preferred_element_type=F32) for m in mb]
        yield
        qm = [jnp.dot(q.astype(BF16), m.astype(BF16), preferred_element_type=F32) for q, m in zip(qs, ms)]
        yield
        qs = [q + m + x for q, m, x in zip(qs, ms, qm)]
        power *= 2
    rows = lax.broadcasted_iota(jnp.int32, (c, c), 0)
    cols = lax.broadcasted_iota(jnp.int32, (c, c), 1)
    eye = (rows == cols).astype(F32)
    lq = [jnp.dot(l, q.astype(BF16), preferred_element_type=F32) for l, q in zip(ls, qs)]
    yield
    return [eye - l + q - x for l, q, x in zip(lmats, qs, lq)]


def _gdn_local_steps(units, tril, strict):
    c = units[0][1].shape[0]
    a = [lax.dot_general(jnp.concatenate([kb, q], axis=0), k, (((1,), (1,)), ((), ())),
                         preferred_element_type=F32)
         for kb, q, k, _, _, _, _ in units]
    yield
    lmats, qks = [], []
    for a_u, (_, _, _, _, _, gcol, grow) in zip(a, units):
        decay = jnp.where(tril, jnp.exp(jnp.where(tril, gcol - grow, 0.0)), 0.0)
        lmats.append(jnp.where(strict, a_u[:c] * decay, 0.0))
        qks.append(jnp.where(tril, a_u[c:] * decay, 0.0))
    tms = yield from _tri_inverse_steps(lmats, c)
    uw = [jnp.dot(tm.astype(BF16), jnp.concatenate([vb, kbe], axis=1), preferred_element_type=F32)
          for tm, (_, _, _, vb, kbe, _, _) in zip(tms, units)]
    yield
    return [(x[:, :DN_DV], x[:, DN_DV:], qk) for x, qk in zip(uw, qks)]


def _run_interleaved(*gens):
    live = list(gens)
    while live:
        for g in list(live):
            try:
                next(g)
            except StopIteration:
                live.remove(g)


def _gate_norm(o, og, norm_g):
    o = o * lax.rsqrt(jnp.mean(o * o, axis=-1, keepdims=True) + RMS_EPS) * norm_g
    return o * _silu(og)


def _proj_layout(dn_width, sc_width):
    off = {}
    pos = 0
    for name, width in (("og", dn_width), ("sc_b", sc_width), ("sc_c", sc_width), ("sc_h", sc_width)):
        off[name] = (pos, pos + width)
        pos += width
    return off


def _project(u, wq_ref, wr_ref, wab_ref, off):
    p = {"qkv": jnp.dot(u, wq_ref[...], preferred_element_type=F32),
         "ab": jnp.dot(u, wab_ref[...], preferred_element_type=F32)}
    rest = jnp.dot(u, wr_ref[...], preferred_element_type=F32)
    for name, (lo, hi) in off.items():
        p[name] = rest[:, lo:hi]
    return p


def _cols(p, off, name):
    del off
    return p[name]


def _mix_out(x, o_dn, o_sc, gate, w_out_ref, g, b, alpha):
    dn_w = o_dn.shape[1]
    mix = (jnp.dot(o_dn.astype(BF16), w_out_ref[0:dn_w, :], preferred_element_type=F32)
           + jnp.dot(o_sc.astype(BF16), w_out_ref[dn_w:, :], preferred_element_type=F32))
    return _post(x, mix, gate, g, b, alpha)


def _mixer_prompt_kernel(x_ref, mod_ref, wq_ref, wr_ref, wab_ref, w_out_ref, cw_ref, mw_ref, alog_ref, dtb_ref, ng_ref,
                         lng_ref, lnb_ref,
                         y_ref, s_out_ref, cq_out_ref, cm_out_ref,
                         s_scr, cq_scr, cm_scr, gct_scr, gl_scr, ops_scr, u_scr, w_scr, qk_scr, o_scr,
                         *, alpha, nseq, tt, off, n_conv, n_mconv):
    t_idx = pl.program_id(1)
    rows_all = nseq * tt
    nchunk = rows_all // CHUNK
    per_seq = tt // CHUNK
    hk = DN_HEADS * DN_DK
    sub = 1

    @pl.when(t_idx == 0)
    def _():
        s_scr[...] = jnp.zeros_like(s_scr)
        cq_scr[...] = jnp.zeros_like(cq_scr)
        cm_scr[...] = jnp.zeros_like(cm_scr)

    xs = [x_ref[s] for s in range(nseq)]
    gates = [mod_ref[s, 3 * sub + 2:3 * sub + 3, :] for s in range(nseq)]
    u = jnp.concatenate(
        [(xs[s] * (1.0 + mod_ref[s, 3 * sub + 1:3 * sub + 2, :]) + mod_ref[s, 3 * sub + 0:3 * sub + 1, :])
         .astype(BF16) for s in range(nseq)], axis=0)
    p = _project(u, wq_ref, wr_ref, wab_ref, off)

    def causal_conv(new_rows, hist_scr, s, w_ref, taps):
        ext = jnp.concatenate([hist_scr[s], new_rows], axis=0)
        acc = new_rows * w_ref[taps - 1:taps, :]
        for j in range(taps - 1):
            acc = acc + pltpu.roll(ext, taps - 1 - j, axis=0)[SUBLANES:] * w_ref[j:j + 1, :]
        hist_scr[s] = new_rows[tt - SUBLANES:]
        return acc

    pq = _cols(p, off, "qkv")
    acts = []
    for s in range(nseq):
        pq_s = pq[s * tt:(s + 1) * tt]
        acts.append(_silu(causal_conv(pq_s, cq_scr, s, cw_ref, n_conv)))
        cq_out_ref[s] = pq_s[tt - (n_conv - 1):]
    act = jnp.concatenate(acts, axis=0)

    ab = _cols(p, off, "ab")
    beta_all = jax.nn.sigmoid(ab)
    g = -jnp.exp(alog_ref[...]) * _softplus(ab + dtb_ref[...])
    rows = lax.broadcasted_iota(jnp.int32, (CHUNK, CHUNK), 0)
    cols = lax.broadcasted_iota(jnp.int32, (CHUNK, CHUNK), 1)
    tril = rows >= cols
    strict = rows > cols
    tril_f = tril.astype(F32)
    gc_parts, gl_parts = [], []
    for c in range(nchunk):
        gc_c = _mm_f32(tril_f, g[c * CHUNK:(c + 1) * CHUNK, :])
        gc_parts.append(gc_c)
        gl_parts.append(jnp.broadcast_to(gc_c[CHUNK - 1:CHUNK, :], (CHUNK, AB_PAD)))
    gc = jnp.concatenate(gc_parts, axis=0)
    gl_rows = jnp.concatenate(gl_parts, axis=0)
    gc_t = gc.T
    for c in range(nchunk):
        gct_scr[c] = gc_t[0:SUBLANES, c * CHUNK:(c + 1) * CHUNK]

    eg_all = jnp.exp(gc)
    ekd_all = jnp.exp(gl_rows - gc)
    egl_all = jnp.exp(gl_rows)
    for h in range(DN_HEADS):
        lanes = slice(h * DN_DK, (h + 1) * DN_DK)
        q = _l2norm(act[:, h * DN_DK:(h + 1) * DN_DK]) * (DN_DK ** -0.5)
        k = _l2norm(act[:, hk + h * DN_DK:hk + (h + 1) * DN_DK])
        v = act[:, 2 * hk + h * DN_DV:2 * hk + (h + 1) * DN_DV]
        beta = beta_all[:, DN_HEADS + h:DN_HEADS + h + 1]
        eg = eg_all[:, h:h + 1]
        kb = k * beta
        ops_scr[0, :, lanes] = kb.astype(BF16)
        ops_scr[1, :, lanes] = q.astype(BF16)
        ops_scr[2, :, lanes] = k.astype(BF16)
        ops_scr[3, :, lanes] = (v * beta).astype(BF16)
        ops_scr[4, :, lanes] = (kb * eg).astype(BF16)
        ops_scr[5, :, lanes] = (q * eg).astype(BF16)
        ops_scr[6, :, lanes] = (k * ekd_all[:, h:h + 1]).astype(BF16)
        gl_scr[:, lanes] = jnp.broadcast_to(egl_all[:, h:h + 1], (rows_all, DN_DV))

    head_lanes = [slice(h * DN_DK, (h + 1) * DN_DK) for h in range(DN_HEADS)]

    where = [(c, h, lanes) for c in range(nchunk) for h, lanes in enumerate(head_lanes)]
    units = [tuple(ops_scr[i, c * CHUNK:(c + 1) * CHUNK, lanes] for i in range(5))
             + (gc[c * CHUNK:(c + 1) * CHUNK, h:h + 1], gct_scr[c, h:h + 1, :])
             for c, h, lanes in where]
    for (c, h, lanes), (u_c, w_c, qk_c) in zip(where, _gdn_chunks_local(units, tril, strict)):
        u_scr[c * CHUNK:(c + 1) * CHUNK, lanes] = u_c
        w_scr[c * CHUNK:(c + 1) * CHUNK, lanes] = w_c.astype(BF16)
        qk_scr[c * DN_HEADS + h] = qk_c.astype(BF16)

    for j in range(per_seq):
        chains = [(s, s * per_seq + j, h, lanes) for s in range(nseq) for h, lanes in enumerate(head_lanes)]
        units = [(w_scr[c * CHUNK:(c + 1) * CHUNK, lanes], ops_scr[5, c * CHUNK:(c + 1) * CHUNK, lanes],
                  u_scr[c * CHUNK:(c + 1) * CHUNK, lanes], qk_scr[c * DN_HEADS + h],
                  ops_scr[6, c * CHUNK:(c + 1) * CHUNK, lanes], gl_scr[c * CHUNK:c * CHUNK + 1, lanes])
                 for s, c, h, lanes in chains]
        o_c, s_new = _gdn_chunk_state(units, [s_scr[s * DN_HEADS + h] for s, _, h, _ in chains])
        for (s, c, h, lanes), o_u, s_u in zip(chains, o_c, s_new):
            s_scr[s * DN_HEADS + h] = s_u
            o_scr[c * CHUNK:(c + 1) * CHUNK, lanes] = o_u
    for s in range(nseq):
        s_out_ref[s] = s_scr[s * DN_HEADS:(s + 1) * DN_HEADS]

    og = _cols(p, off, "og")
    o_dn = jnp.concatenate(
        [_gate_norm(o_scr[:, h * DN_DV:(h + 1) * DN_DV], og[:, h * DN_DV:(h + 1) * DN_DV], ng_ref[...])
         for h in range(DN_HEADS)], axis=1)

    z = _cols(p, off, "sc_c") * _cols(p, off, "sc_h")
    zcs = []
    for s in range(nseq):
        z_s = z[s * tt:(s + 1) * tt]
        zcs.append(causal_conv(z_s, cm_scr, s, mw_ref, n_mconv))
        cm_out_ref[s] = z_s[tt - (n_mconv - 1):]
    o_sc = _cols(p, off, "sc_b") * jnp.concatenate(zcs, axis=0)

    dn_w = o_dn.shape[1]
    mix = (jnp.dot(o_dn.astype(BF16), w_out_ref[0:dn_w, :], preferred_element_type=F32)
           + jnp.dot(o_sc.astype(BF16), w_out_ref[dn_w:, :], preferred_element_type=F32))
    for s in range(nseq):
        y_ref[s] = _post(xs[s], mix[s * tt:(s + 1) * tt], gates[s],
                         lng_ref[sub:sub + 1, :], lnb_ref[sub:sub + 1, :], alpha)


def _mixer_prompt(x, mod, w_in, w_out, conv_w, mconv_w, alog, dtb, norm_g, lng, lnb, *, alpha, nseq, tt, off):
    bsz, t, d = x.shape
    assert t % tt == 0 and tt % LANES == 0 and bsz % nseq == 0
    nt = t // tt
    rows = nseq * tt
    qkv_dim = conv_w.shape[-1]
    sc_w = off["sc_c"][1] - off["sc_c"][0]
    dn_w = DN_HEADS * DN_DV
    n_conv, n_mconv = conv_w.shape[0], mconv_w.shape[0]
    assert max(n_conv, n_mconv) - 1 <= SUBLANES <= tt
    nchunk = rows // CHUNK
    kern = functools.partial(_mixer_prompt_kernel, alpha=alpha, nseq=nseq, tt=tt, off=off,
                             n_conv=n_conv, n_mconv=n_mconv)
    y, s_out, cq, cm = pl.pallas_call(
        kern,
        grid=(bsz // nseq, nt),
        in_specs=[pl.BlockSpec((nseq, tt, d), lambda b, i: (b, i, 0)),
                  pl.BlockSpec((nseq, 3 * N_SUB, d), lambda b, i: (b, 0, 0)),
                  *[_const_spec(w.shape) for w in w_in], _const_spec(w_out.shape),
                  _const_spec(conv_w.shape), _const_spec(mconv_w.shape),
                  _const_spec(alog.shape), _const_spec(dtb.shape), _const_spec(norm_g.shape),
                  _const_spec(lng.shape), _const_spec(lnb.shape)],
        out_specs=[pl.BlockSpec((nseq, tt, d), lambda b, i: (b, i, 0)),
                   pl.BlockSpec((nseq, DN_HEADS, DN_DK, DN_DV), lambda b, i: (b, 0, 0, 0)),
                   pl.BlockSpec((nseq, n_conv - 1, qkv_dim), lambda b, i: (b, 0, 0)),
                   pl.BlockSpec((nseq, n_mconv - 1, sc_w), lambda b, i: (b, 0, 0))],
        out_shape=[jax.ShapeDtypeStruct((bsz, t, d), F32),
                   jax.ShapeDtypeStruct((bsz, DN_HEADS, DN_DK, DN_DV), F32),
                   jax.ShapeDtypeStruct((bsz, n_conv - 1, qkv_dim), F32),
                   jax.ShapeDtypeStruct((bsz, n_mconv - 1, sc_w), F32)],
        scratch_shapes=[pltpu.VMEM((nseq * DN_HEADS, DN_DK, DN_DV), F32),
                        pltpu.VMEM((nseq, SUBLANES, qkv_dim), F32),
                        pltpu.VMEM((nseq, SUBLANES, sc_w), F32),
                        pltpu.VMEM((nchunk, SUBLANES, CHUNK), F32),
                        pltpu.VMEM((rows, dn_w), F32),
                        pltpu.VMEM((7, rows, dn_w), BF16),
                        pltpu.VMEM((rows, dn_w), F32),
                        pltpu.VMEM((rows, dn_w), BF16),
                        pltpu.VMEM((nchunk * DN_HEADS, CHUNK, CHUNK), BF16),
                        pltpu.VMEM((rows, dn_w), F32)],
        compiler_params=pltpu.CompilerParams(dimension_semantics=("arbitrary", "arbitrary"),
                                             vmem_limit_bytes=V7X_VMEM_LIMIT),
        name="mixer_prompt",
    )(x, mod, *w_in, w_out, conv_w, mconv_w, alog, dtb, norm_g, lng, lnb)
    return y, s_out, cq, cm


def _mixer_prompt_pipe_kernel(xf_ref, modf_ref, xb_ref, modb_ref, wq_ref, wr_ref, wab_ref, w_out_ref, cw_ref,
                              mw_ref, alog_ref, dtb_ref, ng_ref, lng_ref, lnb_ref,
                              y_ref, s_out_ref, cq_out_ref, cm_out_ref,
                              s_scr, cq_scr, cm_scr, gct_scr,
                              u_scr, w_scr, qk_scr, qd_scr, kd_scr, gl_scr, og_scr, osc_scr,
                              *, alpha, nseq, tt, tiles_per_group, n_tiles, off, n_conv, n_mconv):
    k = pl.program_id(0)
    rows_all = nseq * tt
    nchunk = rows_all // CHUNK
    per_seq = tt // CHUNK
    hk = DN_HEADS * DN_DK
    sub = 1
    head_lanes = [slice(h * DN_DK, (h + 1) * DN_DK) for h in range(DN_HEADS)]
    t_front = lax.rem(jnp.minimum(k, n_tiles - 1), tiles_per_group)
    t_back = lax.rem(jnp.maximum(k - 1, 0), tiles_per_group)
    keep_front = jnp.where(t_front == 0, 0.0, 1.0).astype(F32)
    keep_back = jnp.where(t_back == 0, 0.0, 1.0).astype(F32)

    def causal_conv(new_rows, hist_scr, s, w_ref, taps):
        ext = jnp.concatenate([hist_scr[s] * keep_front, new_rows], axis=0)
        acc = new_rows * w_ref[taps - 1:taps, :]
        for j in range(taps - 1):
            acc = acc + pltpu.roll(ext, taps - 1 - j, axis=0)[SUBLANES:] * w_ref[j:j + 1, :]
        hist_scr[s] = new_rows[tt - SUBLANES:]
        return acc

    def front():
        u = jnp.concatenate(
            [(xf_ref[s] * (1.0 + modf_ref[s, 3 * sub + 1:3 * sub + 2, :])
              + modf_ref[s, 3 * sub + 0:3 * sub + 1, :]).astype(BF16) for s in range(nseq)], axis=0)
        pq = jnp.dot(u, wq_ref[...], preferred_element_type=F32)
        yield
        rest = jnp.dot(u, wr_ref[...], preferred_element_type=F32)
        yield
        ab = jnp.dot(u, wab_ref[...], preferred_element_type=F32)
        p = {name: rest[:, lo:hi] for name, (lo, hi) in off.items()}

        beta_all = jax.nn.sigmoid(ab)
        g = -jnp.exp(alog_ref[...]) * _softplus(ab + dtb_ref[...])
        rows = lax.broadcasted_iota(jnp.int32, (CHUNK, CHUNK), 0)
        cols = lax.broadcasted_iota(jnp.int32, (CHUNK, CHUNK), 1)
        tril = rows >= cols
        strict = rows > cols
        tril_f = tril.astype(F32)
        gc_parts, gl_parts = [], []
        for c in range(nchunk):
            gc_c = _mm_f32(tril_f, g[c * CHUNK:(c + 1) * CHUNK, :])
            gc_parts.append(gc_c)
            gl_parts.append(jnp.broadcast_to(gc_c[CHUNK - 1:CHUNK, :], (CHUNK, AB_PAD)))
        yield
        gc = jnp.concatenate(gc_parts, axis=0)
        gl_rows = jnp.concatenate(gl_parts, axis=0)
        gc_t = gc.T
        for c in range(nchunk):
            gct_scr[c] = gc_t[0:SUBLANES, c * CHUNK:(c + 1) * CHUNK]

        acts = []
        for s in range(nseq):
            pq_s = pq[s * tt:(s + 1) * tt]
            acts.append(_silu(causal_conv(pq_s, cq_scr, s, cw_ref, n_conv)))
            cq_out_ref[s] = pq_s[tt - (n_conv - 1):]
        act = jnp.concatenate(acts, axis=0)

        eg_all = jnp.exp(gc)
        ekd_all = jnp.exp(gl_rows - gc)
        egl_all = jnp.exp(gl_rows)
        per_head = []
        for h, lanes in enumerate(head_lanes):
            q = _l2norm(act[:, h * DN_DK:(h + 1) * DN_DK]) * (DN_DK ** -0.5)
            kk = _l2norm(act[:, hk + h * DN_DK:hk + (h + 1) * DN_DK])
            v = act[:, 2 * hk + h * DN_DV:2 * hk + (h + 1) * DN_DV]
            beta = beta_all[:, DN_HEADS + h:DN_HEADS + h + 1]
            eg = eg_all[:, h:h + 1]
            kb = kk * beta
            per_head.append((kb.astype(BF16), q.astype(BF16), kk.astype(BF16), (v * beta).astype(BF16),
                             (kb * eg).astype(BF16)))
            qd_scr[:, lanes] = (q * eg).astype(BF16)
            kd_scr[:, lanes] = (kk * ekd_all[:, h:h + 1]).astype(BF16)
            for c in range(nchunk):
                gl_scr[c * SUBLANES:(c + 1) * SUBLANES, lanes] = jnp.broadcast_to(
                    egl_all[c * CHUNK:c * CHUNK + SUBLANES, h:h + 1], (SUBLANES, DN_DV))

        z = p["sc_c"] * p["sc_h"]
        zcs = []
        for s in range(nseq):
            z_s = z[s * tt:(s + 1) * tt]
            zcs.append(causal_conv(z_s, cm_scr, s, mw_ref, n_mconv))
            cm_out_ref[s] = z_s[tt - (n_mconv - 1):]
        osc_scr[...] = p["sc_b"] * jnp.concatenate(zcs, axis=0)
        og_scr[...] = p["og"]

        where = [(c, h, lanes) for c in range(nchunk) for h, lanes in enumerate(head_lanes)]
        units = [tuple(arr[c * CHUNK:(c + 1) * CHUNK] for arr in per_head[h])
                 + (gc[c * CHUNK:(c + 1) * CHUNK, h:h + 1], gct_scr[c, h:h + 1, :])
                 for c, h, lanes in where]
        local = yield from _gdn_local_steps(units, tril, strict)
        for (c, h, lanes), (u_c, w_c, qk_c) in zip(where, local):
            u_scr[c * CHUNK:(c + 1) * CHUNK, lanes] = u_c
            w_scr[c * CHUNK:(c + 1) * CHUNK, lanes] = w_c.astype(BF16)
            qk_scr[c * DN_HEADS + h] = qk_c.astype(BF16)

    def back():
        u_all, w_all, qd_all, kd_all = u_scr[...], w_scr[...], qd_scr[...], kd_scr[...]
        gl_all, og, o_sc = gl_scr[...], og_scr[...], osc_scr[...]
        qks = [qk_scr[i] for i in range(nchunk * DN_HEADS)]
        xs = [xb_ref[s] for s in range(nseq)]
        gates = [modb_ref[s, 3 * sub + 2:3 * sub + 3, :] for s in range(nseq)]
        states = [s_scr[i] * keep_back for i in range(nseq * DN_HEADS)]
        o_rows = [[None] * DN_HEADS for _ in range(nchunk)]
        for j in range(per_seq):
            chains = [(s, s * per_seq + j, h, lanes) for s in range(nseq) for h, lanes in enumerate(head_lanes)]
            rws = [slice(c * CHUNK, (c + 1) * CHUNK) for _, c, _, _ in chains]
            r = [jnp.dot(jnp.concatenate([w_all[rw, lanes], qd_all[rw, lanes]], axis=0),
                         states[s * DN_HEADS + h].astype(BF16), preferred_element_type=F32)
                 for (s, c, h, lanes), rw in zip(chains, rws)]
            yield
            v_new = [(u_all[rw, lanes] - r_u[:CHUNK]).astype(BF16)
                     for (s, c, h, lanes), rw, r_u in zip(chains, rws, r)]
            for (s, c, h, lanes), r_u, v_u in zip(chains, r, v_new):
                o_rows[c][h] = r_u[CHUNK:] + jnp.dot(qks[c * DN_HEADS + h], v_u, preferred_element_type=F32)
            for (s, c, h, lanes), rw, v_u in zip(chains, rws, v_new):
                i = s * DN_HEADS + h
                states[i] = (states[i] * gl_all[c * SUBLANES:c * SUBLANES + 1, lanes]
                             + lax.dot_general(kd_all[rw, lanes], v_u, (((0,), (0,)), ((), ())),
                                               preferred_element_type=F32))
            yield
        for i, st in enumerate(states):
            s_scr[i] = st
        for s in range(nseq):
            for h in range(DN_HEADS):
                s_out_ref[s, h] = states[s * DN_HEADS + h]

        o_dn = jnp.concatenate(
            [_gate_norm(jnp.concatenate([o_rows[c][h] for c in range(nchunk)], axis=0),
                        og[:, lanes], ng_ref[...]) for h, lanes in enumerate(head_lanes)], axis=1)
        dn_w = o_dn.shape[1]
        mix = (jnp.dot(o_dn.astype(BF16), w_out_ref[0:dn_w, :], preferred_element_type=F32)
               + jnp.dot(o_sc.astype(BF16), w_out_ref[dn_w:, :], preferred_element_type=F32))
        yield
        for s in range(nseq):
            y_ref[s] = _post(xs[s], mix[s * tt:(s + 1) * tt], gates[s],
                             lng_ref[sub:sub + 1, :], lnb_ref[sub:sub + 1, :], alpha)

    @pl.when(k == 0)
    def _():
        s_scr[...] = jnp.zeros_like(s_scr)
        cq_scr[...] = jnp.zeros_like(cq_scr)
        cm_scr[...] = jnp.zeros_like(cm_scr)
        _run_interleaved(front())

    @pl.when(jnp.logical_and(k > 0, k < n_tiles))
    def _():
        _run_interleaved(back(), front())

    @pl.when(k == n_tiles)
    def _():
        _run_interleaved(back())


def _mixer_prompt_pipe(x, mod, w_in, w_out, conv_w, mconv_w, alog, dtb, norm_g, lng, lnb, *, alpha, nseq, tt, off):
    bsz, t, d = x.shape
    assert t % tt == 0 and tt % LANES == 0 and bsz % nseq == 0
    nt = t // tt
    n_tiles = (bsz // nseq) * nt
    rows = nseq * tt
    qkv_dim = conv_w.shape[-1]
    sc_w = off["sc_c"][1] - off["sc_c"][0]
    dn_w = DN_HEADS * DN_DV
    n_conv, n_mconv = conv_w.shape[0], mconv_w.shape[0]
    assert max(n_conv, n_mconv) - 1 <= SUBLANES <= tt
    nchunk = rows // CHUNK
    kern = functools.partial(_mixer_prompt_pipe_kernel, alpha=alpha, nseq=nseq, tt=tt, tiles_per_group=nt,
                             n_tiles=n_tiles, off=off, n_conv=n_conv, n_mconv=n_mconv)

    def front_tile(k):
        return jnp.minimum(k, n_tiles - 1)

    def back_tile(k):
        return jnp.maximum(k - 1, 0)

    y, s_out, cq, cm = pl.pallas_call(
        kern,
        grid=(n_tiles + 1,),
        in_specs=[pl.BlockSpec((nseq, tt, d), lambda k: (front_tile(k) // nt, front_tile(k) % nt, 0)),
                  pl.BlockSpec((nseq, 3 * N_SUB, d), lambda k: (front_tile(k) // nt, 0, 0)),
                  pl.BlockSpec((nseq, tt, d), lambda k: (back_tile(k) // nt, back_tile(k) % nt, 0)),
                  pl.BlockSpec((nseq, 3 * N_SUB, d), lambda k: (back_tile(k) // nt, 0, 0)),
                  *[_const_spec(w.shape) for w in w_in], _const_spec(w_out.shape),
                  _const_spec(conv_w.shape), _const_spec(mconv_w.shape),
                  _const_spec(alog.shape), _const_spec(dtb.shape), _const_spec(norm_g.shape),
                  _const_spec(lng.shape), _const_spec(lnb.shape)],
        out_specs=[pl.BlockSpec((nseq, tt, d), lambda k: (back_tile(k) // nt, back_tile(k) % nt, 0)),
                   pl.BlockSpec((nseq, DN_HEADS, DN_DK, DN_DV), lambda k: (back_tile(k) // nt, 0, 0, 0)),
                   pl.BlockSpec((nseq, n_conv - 1, qkv_dim), lambda k: (front_tile(k) // nt, 0, 0)),
                   pl.BlockSpec((nseq, n_mconv - 1, sc_w), lambda k: (front_tile(k) // nt, 0, 0))],
        out_shape=[jax.ShapeDtypeStruct((bsz, t, d), F32),
                   jax.ShapeDtypeStruct((bsz, DN_HEADS, DN_DK, DN_DV), F32),
                   jax.ShapeDtypeStruct((bsz, n_conv - 1, qkv_dim), F32),
                   jax.ShapeDtypeStruct((bsz, n_mconv - 1, sc_w), F32)],
        scratch_shapes=[pltpu.VMEM((nseq * DN_HEADS, DN_DK, DN_DV), F32),
                        pltpu.VMEM((nseq, SUBLANES, qkv_dim), F32),
                        pltpu.VMEM((nseq, SUBLANES, sc_w), F32),
                        pltpu.VMEM((nchunk, SUBLANES, CHUNK), F32),
                        pltpu.VMEM((rows, dn_w), F32),
                        pltpu.VMEM((rows, dn_w), BF16),
                        pltpu.VMEM((nchunk * DN_HEADS, CHUNK, CHUNK), BF16),
                        pltpu.VMEM((rows, dn_w), BF16),
                        pltpu.VMEM((rows, dn_w), BF16),
                        pltpu.VMEM((nchunk * SUBLANES, dn_w), F32),
                        pltpu.VMEM((rows, dn_w), F32),
                        pltpu.VMEM((rows, sc_w), F32)],
        compiler_params=pltpu.CompilerParams(dimension_semantics=("arbitrary",),
                                             vmem_limit_bytes=V7X_VMEM_LIMIT),
        name="mixer_prompt",
    )(x, mod, x, mod, *w_in, w_out, conv_w, mconv_w, alog, dtb, norm_g, lng, lnb)
    return y, s_out, cq, cm


_PACK = SUBLANES


def _sample_phase_a(x_ref, mod_ref, wq_ref, wr_ref, wab_ref, cw_ref, mw_ref, alog_ref, dtb_ref, cq_in_ref, cm_in_ref,
                    cq_out_ref, cm_out_ref, og_scr, osc_scr, lhs_scr, u_scr, kd_scr, gl_scr, qk_scr,
                    *, t_len, d, off, n_conv, n_mconv):
    bsz = x_ref.shape[0]
    hk = DN_HEADS * DN_DK
    qkv_dim = cw_ref.shape[-1]
    sc_w = off["sc_c"][1] - off["sc_c"][0]

    x = _stack_time(x_ref, t_len, d)
    u = (x * (1.0 + _mod_rows(mod_ref, 1, t_len, d)) + _mod_rows(mod_ref, 0, t_len, d)).astype(BF16)
    p = _project(u, wq_ref, wr_ref, wab_ref, off)

    def rows(arr, t):
        return arr[t * bsz:(t + 1) * bsz]

    pq = _cols(p, off, "qkv")
    seq = ([cq_in_ref[:, j * qkv_dim:(j + 1) * qkv_dim] for j in range(n_conv - 1)]
           + [rows(pq, t) for t in range(t_len)])
    act = []
    for t in range(t_len):
        acc = None
        for j in range(n_conv):
            term = seq[t + j] * cw_ref[j:j + 1, :]
            acc = term if acc is None else acc + term
        act.append(_silu(acc))
    for j in range(n_conv - 1):
        cq_out_ref[:, j * qkv_dim:(j + 1) * qkv_dim] = seq[t_len + j]

    z = _cols(p, off, "sc_c") * _cols(p, off, "sc_h")
    zseq = ([cm_in_ref[:, j * sc_w:(j + 1) * sc_w] for j in range(n_mconv - 1)]
            + [rows(z, t) for t in range(t_len)])
    zc = []
    for t in range(t_len):
        acc = None
        for j in range(n_mconv):
            term = zseq[t + j] * mw_ref[j:j + 1, :]
            acc = term if acc is None else acc + term
        zc.append(acc)
    for j in range(n_mconv - 1):
        cm_out_ref[:, j * sc_w:(j + 1) * sc_w] = zseq[t_len + j]
    osc_scr[...] = _cols(p, off, "sc_b") * jnp.concatenate(zc, axis=0)
    og_scr[...] = _cols(p, off, "og")

    ab = _cols(p, off, "ab")
    beta_all = jax.nn.sigmoid(ab)
    g_all = -jnp.exp(alog_ref[...]) * _softplus(ab + dtb_ref[...])
    zero = jnp.zeros((bsz, LANES), F32)

    for h in range(DN_HEADS):
        q = [_l2norm(a[:, h * DN_DK:(h + 1) * DN_DK]) * (DN_DK ** -0.5) for a in act]
        k = [_l2norm(a[:, hk + h * DN_DK:hk + (h + 1) * DN_DK]) for a in act]
        v = [a[:, 2 * hk + h * DN_DV:2 * hk + (h + 1) * DN_DV] for a in act]
        beta = [rows(beta_all, t)[:, DN_HEADS + h:DN_HEADS + h + 1] for t in range(t_len)]
        g = [rows(g_all, t)[:, h:h + 1] for t in range(t_len)]
        gc = [g[0]]
        for t in range(1, t_len):
            gc.append(gc[-1] + g[t])
        kb = [k[t] * beta[t] for t in range(t_len)]
        vb = [v[t] * beta[t] for t in range(t_len)]
        eg = [jnp.exp(gc[t]) for t in range(t_len)]
        kbe = [kb[t] * eg[t] for t in range(t_len)]
        lm = [[None] * t_len for _ in range(t_len)]
        n_qk = 0
        for i in range(t_len):
            for j in range(i + 1):
                dec = jnp.exp(gc[i] - gc[j])
                qk_scr[h, n_qk] = jnp.broadcast_to(_rowsum(q[i] * k[j]) * dec, (bsz, LANES))
                n_qk += 1
                if j < i:
                    lm[i][j] = _rowsum(kb[i] * k[j]) * dec
        tm = [[None] * t_len for _ in range(t_len)]
        for i in range(t_len):
            for j in range(i):
                acc = lm[i][j]
                for m in range(j + 1, i):
                    acc = acc + lm[i][m] * tm[m][j]
                tm[i][j] = -acc
        for i in range(t_len):
            ui, wi = vb[i], kbe[i]
            for j in range(i):
                ui = ui + tm[i][j] * vb[j]
                wi = wi + tm[i][j] * kbe[j]
            lhs_scr[h, pl.ds(i, bsz, stride=_PACK), :] = wi
            lhs_scr[h, pl.ds(t_len + i, bsz, stride=_PACK), :] = q[i] * eg[i]
            u_scr[h, pl.ds(i, bsz, stride=_PACK), :] = ui
            u_scr[h, pl.ds(t_len + i, bsz, stride=_PACK), :] = zero
            kd_scr[h, pl.ds(i, bsz, stride=_PACK), :] = k[i] * jnp.exp(gc[t_len - 1] - gc[i])
            kd_scr[h, pl.ds(t_len + i, bsz, stride=_PACK), :] = zero
        gl_scr[h] = jnp.broadcast_to(jnp.exp(gc[t_len - 1]), (bsz, LANES))


def _sample_phase_c(x_ref, mod_ref, w_out_ref, ng_ref, lng_ref, lnb_ref, y_ref,
                    og_scr, osc_scr, res_scr, vn_scr, qk_scr, *, alpha, t_len, d):
    bsz = x_ref.shape[0]
    sub = 1
    x = _stack_time(x_ref, t_len, d)
    gate = _mod_rows(mod_ref, 2, t_len, d)
    og = og_scr[...]
    per_t = []
    for i in range(t_len):
        heads = []
        for h in range(DN_HEADS):
            oi = res_scr[h, pl.ds(t_len + i, bsz, stride=_PACK), :]
            base = i * (i + 1) // 2
            for j in range(i + 1):
                oi = oi + qk_scr[h, base + j] * vn_scr[h, pl.ds(j, bsz, stride=_PACK), :]
            heads.append(_gate_norm(oi, og[i * bsz:(i + 1) * bsz, h * DN_DV:(h + 1) * DN_DV], ng_ref[...]))
        per_t.append(jnp.concatenate(heads, axis=1))
    o_dn = jnp.concatenate(per_t, axis=0)
    y = _mix_out(x, o_dn, osc_scr[...], gate, w_out_ref,
                 lng_ref[sub:sub + 1, :], lnb_ref[sub:sub + 1, :], alpha)
    for t in range(t_len):
        y_ref[:, t * d:(t + 1) * d] = y[t * bsz:(t + 1) * bsz]


def _mixer_sample_kernel(x_ref, mod_ref, wq_ref, wr_ref, wab_ref, w_out_ref, cw_ref, mw_ref, alog_ref, dtb_ref, ng_ref,
                         lng_ref, lnb_ref, cq_in_ref, cm_in_ref, s_in_ref,
                         y_ref, cq_out_ref, cm_out_ref, s_out_ref,
                         og_scr, osc_scr, lhs_scr, u_scr, kd_scr, gl_scr, qk_scr, res_scr, vn_scr,
                         *, alpha, t_len, d, off, n_conv, n_mconv, bt):
    step = pl.program_id(0)
    assert 2 * t_len == _PACK, "packed buffers hold [T rows | T rows] per sequence"

    @pl.when(step == 0)
    def _():
        _sample_phase_a(x_ref, mod_ref, wq_ref, wr_ref, wab_ref, cw_ref, mw_ref, alog_ref, dtb_ref, cq_in_ref, cm_in_ref,
                        cq_out_ref, cm_out_ref, og_scr, osc_scr, lhs_scr, u_scr, kd_scr, gl_scr, qk_scr,
                        t_len=t_len, d=d, off=off, n_conv=n_conv, n_mconv=n_mconv)

    top = (lax.broadcasted_iota(jnp.int32, (_PACK, LANES), 0) < t_len).astype(F32)

    chains = [(bl, h) for bl in range(bt) for h in range(DN_HEADS)]
    seq0 = step * bt
    row0 = [pl.multiple_of((seq0 + bl) * _PACK, _PACK) for bl in range(bt)]
    states = [s_in_ref[bl, h] for bl, h in chains]
    r = [_mm(lhs_scr[h, pl.ds(row0[bl], _PACK), :], s) for (bl, h), s in zip(chains, states)]
    vns = []
    for (bl, h), r_u in zip(chains, r):
        res_scr[h, pl.ds(row0[bl], _PACK), :] = r_u
        vn = (u_scr[h, pl.ds(row0[bl], _PACK), :] - r_u) * top
        vn_scr[h, pl.ds(row0[bl], _PACK), :] = vn
        vns.append(vn)
    upd = [_mm_at(kd_scr[h, pl.ds(row0[bl], _PACK), :], vn) for (bl, h), vn in zip(chains, vns)]
    for (bl, h), s, s_add in zip(chains, states, upd):
        s_out_ref[bl, h] = s * gl_scr[h, pl.ds(seq0 + bl, 1), :] + s_add

    @pl.when(step == pl.num_programs(0) - 1)
    def _():
        _sample_phase_c(x_ref, mod_ref, w_out_ref, ng_ref, lng_ref, lnb_ref, y_ref,
                        og_scr, osc_scr, res_scr, vn_scr, qk_scr, alpha=alpha, t_len=t_len, d=d)


def _mixer_sample(x, mod, w_in, w_out, conv_w, mconv_w, alog, dtb, norm_g, lng, lnb, s_in, cq_in, cm_in,
                  *, alpha, off, bt):
    bsz, t, d = x.shape
    assert bsz % bt == 0
    qkv_dim = conv_w.shape[-1]
    sc_w = off["sc_c"][1] - off["sc_c"][0]
    dn_w = DN_HEADS * DN_DV
    n_conv, n_mconv = conv_w.shape[0], mconv_w.shape[0]
    n_qk = t * (t + 1) // 2
    x2 = x.reshape(bsz, t * d)
    cq2 = cq_in.reshape(bsz, (n_conv - 1) * qkv_dim)
    cm2 = cm_in.reshape(bsz, (n_mconv - 1) * sc_w)
    kern = functools.partial(_mixer_sample_kernel, alpha=alpha, t_len=t, d=d, off=off,
                             n_conv=n_conv, n_mconv=n_mconv, bt=bt)
    consts = (*w_in, w_out, conv_w, mconv_w, alog, dtb, norm_g, lng, lnb, cq2, cm2)
    state_spec = pl.BlockSpec((bt, DN_HEADS, DN_DK, DN_DV), lambda i: (i, 0, 0, 0))
    packed = pltpu.VMEM((DN_HEADS, _PACK * bsz, LANES), F32)
    y, cq, cm, s_out = pl.pallas_call(
        kern,
        grid=(bsz // bt,),
        in_specs=([_const_spec(x2.shape), _sub_mod_spec(bsz, d, 1)]
                  + [_const_spec(a.shape) for a in consts] + [state_spec]),
        out_specs=[pl.BlockSpec(x2.shape, lambda i: (0, 0)),
                   pl.BlockSpec(cq2.shape, lambda i: (0, 0)),
                   pl.BlockSpec(cm2.shape, lambda i: (0, 0)),
                   state_spec],
        out_shape=[jax.ShapeDtypeStruct(x2.shape, F32),
                   jax.ShapeDtypeStruct(cq2.shape, F32),
                   jax.ShapeDtypeStruct(cm2.shape, F32),
                   jax.ShapeDtypeStruct(s_in.shape, F32)],
        scratch_shapes=[pltpu.VMEM((t * bsz, dn_w), F32),
                        pltpu.VMEM((t * bsz, sc_w), F32),
                        packed,
                        packed,
                        packed,
                        pltpu.VMEM((DN_HEADS, bsz, LANES), F32),
                        pltpu.VMEM((DN_HEADS, n_qk, bsz, LANES), F32),
                        packed,
                        packed],
        compiler_params=pltpu.CompilerParams(dimension_semantics=("arbitrary",),
                                             vmem_limit_bytes=V7X_VMEM_LIMIT),
        name="mixer_sample",
    )(x2, mod, *consts, s_in)
    return (y.reshape(bsz, t, d), s_out, cq.reshape(bsz, n_conv - 1, qkv_dim),
            cm.reshape(bsz, n_mconv - 1, sc_w))


def _split_w_in(w_in, qkv_dim):
    c1 = qkv_dim + 2 * DN_HEADS
    wab = jnp.pad(w_in[:, qkv_dim:c1], ((0, 0), (0, AB_PAD - 2 * DN_HEADS)))
    return w_in[:, :qkv_dim].astype(BF16), w_in[:, c1:].astype(BF16), wab.astype(BF16)


def _pad_lanes(v):
    return jnp.zeros((1, AB_PAD), F32).at[0, :v.shape[0]].set(v)


def kernel(x_prompt, x_sample, state_ssm, state_conv_qkv, state_conv_mix, c_prompt, c_sample, w_ada, b_ada, ln_g, ln_b, ffn1_wg, ffn1_wu, ffn1_wd, ffn2_wg, ffn2_wu, ffn2_wd, w_in, conv_qkv_w, a_log, dt_bias, dn_norm_g, conv_mix_w, w_out):
    depth = w_ada.shape[0]
    alpha = (2 * depth) ** 0.25
    bp, tp, d = x_prompt.shape
    qkv_dim = conv_qkv_w.shape[-1]
    sc_width = conv_mix_w.shape[-1]
    dn_width = DN_HEADS * DN_DV
    off = _proj_layout(dn_width, sc_width)
    tm = min(1024, tp)
    ffn_parts = 2 if tm % (2 * SUBLANES) == 0 else 1
    nseq = 2 if bp % 2 == 0 else 1
    tt = min(512 // nseq, tp)
    bt = min(8, x_sample.shape[0])

    hp, hs = x_prompt, x_sample
    outs = [[] for _ in range(6)]
    for l in range(depth):
        c_all = jnp.concatenate([c_prompt, c_sample], axis=0)
        mod = _adaln(c_all, w_ada[l], b_ada[l].reshape(1, -1), tn=d)
        mod_p = mod[:bp].reshape(bp, 3 * N_SUB, d)
        mod_s = mod[bp:]
        lng, lnb = ln_g[l], ln_b[l]
        w1 = (ffn1_wg[l].astype(BF16), ffn1_wu[l].astype(BF16), ffn1_wd[l].astype(BF16))
        w2 = (ffn2_wg[l].astype(BF16), ffn2_wu[l].astype(BF16), ffn2_wd[l].astype(BF16))
        w_in_p = _split_w_in(w_in[l], qkv_dim)
        w_out_b = w_out[l].astype(BF16)
        alog, dtb = _pad_lanes(a_log[l]), _pad_lanes(dt_bias[l])
        norm_g = dn_norm_g[l].reshape(1, -1)
        mix_args = (w_in_p, w_out_b, conv_qkv_w[l], conv_mix_w[l], alog, dtb, norm_g, lng, lnb)

        hp = _ffn_prompt(hp, mod_p, *w1, lng, lnb, sub=0, alpha=alpha, tm=tm, parts=ffn_parts)
        hp, a1, a2, a3 = _mixer_prompt_pipe(hp, mod_p, *mix_args, alpha=alpha, nseq=nseq, tt=tt, off=off)
        hp = _ffn_prompt(hp, mod_p, *w2, lng, lnb, sub=2, alpha=alpha, tm=tm, parts=ffn_parts)

        hs = _ffn_sample(hs, mod_s, *w1, lng, lnb, sub=0, alpha=alpha)
        hs, b1, b2, b3 = _mixer_sample(hs, mod_s, *mix_args, state_ssm[l], state_conv_qkv[l],
                                       state_conv_mix[l], alpha=alpha, off=off, bt=bt)
        hs = _ffn_sample(hs, mod_s, *w2, lng, lnb, sub=2, alpha=alpha)
        for lst, val in zip(outs, (a1, a2, a3, b1, b2, b3)):
            lst.append(val)
    return (hp, hs) + tuple(jnp.stack(o) for o in outs)
```

```python
import functools

import jax
import jax.numpy as jnp
from jax import lax
from jax.experimental import pallas as pl
from jax.experimental.pallas import tpu as pltpu

F32 = jnp.float32
BF16 = jnp.bfloat16

LN_EPS = 1e-5
RMS_EPS = 1e-6
N_SUB = 3
DN_HEADS = 4
DN_DK = 128
DN_DV = 128
CHUNK = 64
LANES = 128
SUBLANES = 8
AB_PAD = LANES

V7X_VMEM_LIMIT = 60000 * 1024


def _mm(a, b):
    return jnp.dot(a.astype(BF16), b.astype(BF16), preferred_element_type=F32)


def _mm_bt(a, b):
    return lax.dot_general(a.astype(BF16), b.astype(BF16), (((1,), (1,)), ((), ())),
                           preferred_element_type=F32)


def _mm_at(a, b):
    return lax.dot_general(a.astype(BF16), b.astype(BF16), (((0,), (0,)), ((), ())),
                           preferred_element_type=F32)


def _mm_f32(a, b):
    return jnp.dot(a, b, preferred_element_type=F32, precision=lax.Precision.HIGHEST)


def _silu(x):
    return x * jax.nn.sigmoid(x)


def _softplus(x):
    return jnp.maximum(x, 0.0) + jnp.log1p(jnp.exp(-jnp.abs(x)))


def _layer_norm(y, g, b):
    mu = jnp.mean(y, axis=-1, keepdims=True)
    yc = y - mu
    var = jnp.mean(yc * yc, axis=-1, keepdims=True)
    return yc * lax.rsqrt(var + LN_EPS) * g + b


def _post(x, delta, gate, g, b, alpha):
    return _layer_norm(alpha * x + gate * delta, g, b)


def _rowsum(x):
    return jnp.sum(x, axis=-1, keepdims=True)


def _l2norm(x):
    return x * lax.rsqrt(_rowsum(x * x) + RMS_EPS)


def _const_spec(shape):
    nd = len(shape)
    return pl.BlockSpec(shape, lambda *_: (0,) * nd, pipeline_mode=pl.Buffered(1))


def _adaln_kernel(c_ref, w_ref, b_ref, o_ref):
    s = _silu(c_ref[...])
    o_ref[...] = _mm(s, w_ref[...]) + b_ref[...]


def _adaln(c, w, b, tn=1024):
    m, d = c.shape
    n = w.shape[1]
    assert n % tn == 0
    return pl.pallas_call(
        _adaln_kernel,
        grid=(n // tn,),
        in_specs=[_const_spec((m, d)),
                  pl.BlockSpec((d, tn), lambda j: (0, j)),
                  pl.BlockSpec((1, tn), lambda j: (0, j))],
        out_specs=pl.BlockSpec((m, tn), lambda j: (0, j)),
        out_shape=jax.ShapeDtypeStruct((m, n), F32),
        compiler_params=pltpu.CompilerParams(dimension_semantics=("arbitrary",)),
        name="adaln",
    )(c, w, b)


def _ffn_rows(x, shift, scale, gate, wg_ref, wu_ref, wd_ref, g, b, alpha):
    u = (x * (1.0 + scale) + shift).astype(BF16)
    hg = jnp.dot(u, wg_ref[...], preferred_element_type=F32)
    hu = jnp.dot(u, wu_ref[...], preferred_element_type=F32)
    h = (_silu(hg) * hu).astype(BF16)
    d = jnp.dot(h, wd_ref[...], preferred_element_type=F32)
    return _post(x, 0.5 * d, gate, g, b, alpha)


def _ffn_prompt_kernel(x_ref, mod_ref, wg_ref, wu_ref, wd_ref, lng_ref, lnb_ref, o_ref, *, sub, alpha, parts):
    shift = mod_ref[0, 3 * sub + 0:3 * sub + 1, :]
    scale = mod_ref[0, 3 * sub + 1:3 * sub + 2, :]
    gate = mod_ref[0, 3 * sub + 2:3 * sub + 3, :]
    rows = x_ref.shape[0] // parts
    for part in range(parts):
        sl = slice(part * rows, (part + 1) * rows)
        o_ref[sl, :] = _ffn_rows(x_ref[sl, :], shift, scale, gate, wg_ref, wu_ref, wd_ref,
                                 lng_ref[sub:sub + 1, :], lnb_ref[sub:sub + 1, :], alpha)


def _ffn_prompt(x, mod, wg, wu, wd, lng, lnb, *, sub, alpha, tm, parts):
    bsz, t, d = x.shape
    f = wg.shape[1]
    assert t % tm == 0 and tm % (parts * SUBLANES) == 0
    per_seq = t // tm
    x2 = x.reshape(bsz * t, d)
    out = pl.pallas_call(
        functools.partial(_ffn_prompt_kernel, sub=sub, alpha=alpha, parts=parts),
        grid=(bsz * per_seq,),
        in_specs=[pl.BlockSpec((tm, d), lambda i: (i, 0)),
                  pl.BlockSpec((1, 3 * N_SUB, d), lambda i: (i // per_seq, 0, 0)),
                  _const_spec((d, f)), _const_spec((d, f)), _const_spec((f, d)),
                  _const_spec(lng.shape), _const_spec(lnb.shape)],
        out_specs=pl.BlockSpec((tm, d), lambda i: (i, 0)),
        out_shape=jax.ShapeDtypeStruct((bsz * t, d), F32),
        compiler_params=pltpu.CompilerParams(dimension_semantics=("arbitrary",),
                                             vmem_limit_bytes=V7X_VMEM_LIMIT),
        name=f"ffn_prompt{sub}",
    )(x2, mod, wg, wu, wd, lng, lnb)
    return out.reshape(bsz, t, d)


def _stack_time(ref, t_len, d):
    return jnp.concatenate([ref[:, t * d:(t + 1) * d] for t in range(t_len)], axis=0)


def _mod_rows(mod_ref, idx, t_len, d):
    m = mod_ref[:, idx * d:(idx + 1) * d]
    return jnp.concatenate([m] * t_len, axis=0)


def _ffn_sample_kernel(x_ref, mod_ref, wg_ref, wu_ref, wd_ref, lng_ref, lnb_ref, o_ref, *,
                       sub, alpha, t_len, d):
    bsz = x_ref.shape[0]
    x = _stack_time(x_ref, t_len, d)
    y = _ffn_rows(x, _mod_rows(mod_ref, 0, t_len, d), _mod_rows(mod_ref, 1, t_len, d),
                  _mod_rows(mod_ref, 2, t_len, d), wg_ref, wu_ref, wd_ref,
                  lng_ref[sub:sub + 1, :], lnb_ref[sub:sub + 1, :], alpha)
    for t in range(t_len):
        o_ref[:, t * d:(t + 1) * d] = y[t * bsz:(t + 1) * bsz]


def _sub_mod_spec(bsz, d, sub):
    return pl.BlockSpec((bsz, 3 * d), lambda *_: (0, sub), pipeline_mode=pl.Buffered(1))


def _cast_weight(w_hbm, w_scr, stage, sem, chunk_rows):
    n = w_hbm.shape[0] // chunk_rows
    copies = [pltpu.make_async_copy(w_hbm.at[pl.ds(i * chunk_rows, chunk_rows), :], stage.at[i % 2],
                                    sem.at[i % 2]) for i in range(n)]
    copies[0].start()
    for i in range(n):
        if i + 1 < n:
            copies[i + 1].start()
        copies[i].wait()
        w_scr[i * chunk_rows:(i + 1) * chunk_rows, :] = stage[i % 2].astype(BF16)


def _ffn_kernel(xp_ref, modp_ref, xs_ref, mods_ref, wg_hbm, wu_hbm, wd_hbm, lng_ref, lnb_ref,
                yp_ref, ys_ref, wg_scr, wu_scr, wd_scr, stage_in, stage_out, sem,
                *, sub, alpha, n_prompt, t_len, d, in_rows, out_rows):
    k = pl.program_id(0)
    g, b = lng_ref[sub:sub + 1, :], lnb_ref[sub:sub + 1, :]

    @pl.when(k == 0)
    def _():
        _cast_weight(wg_hbm, wg_scr, stage_in, sem, in_rows)
        _cast_weight(wu_hbm, wu_scr, stage_in, sem, in_rows)
        _cast_weight(wd_hbm, wd_scr, stage_out, sem, out_rows)

    @pl.when(k < n_prompt)
    def _():
        shift = modp_ref[0, 3 * sub + 0:3 * sub + 1, :]
        scale = modp_ref[0, 3 * sub + 1:3 * sub + 2, :]
        gate = modp_ref[0, 3 * sub + 2:3 * sub + 3, :]
        yp_ref[...] = _ffn_rows(xp_ref[...], shift, scale, gate, wg_scr, wu_scr, wd_scr, g, b, alpha)

    @pl.when(k == n_prompt)
    def _():
        bsz = xs_ref.shape[0]
        x = _stack_time(xs_ref, t_len, d)
        y = _ffn_rows(x, _mod_rows(mods_ref, 0, t_len, d), _mod_rows(mods_ref, 1, t_len, d),
                      _mod_rows(mods_ref, 2, t_len, d), wg_scr, wu_scr, wd_scr, g, b, alpha)
        for t in range(t_len):
            ys_ref[:, t * d:(t + 1) * d] = y[t * bsz:(t + 1) * bsz]


def _ffn(xp, xs, mod_p, mod_s, wg, wu, wd, lng, lnb, *, sub, alpha, tm):
    bp, tp, d = xp.shape
    bs, ts, _ = xs.shape
    f = wg.shape[1]
    assert tp % tm == 0
    per_seq = tp // tm
    n_prompt = bp * per_seq
    in_rows = max(r for r in (16, 32, 64, 128) if d % r == 0)
    out_rows = max(r for r in (16, 32, 64, 128, 176, 352) if f % r == 0)
    last = n_prompt - 1
    yp, ys = pl.pallas_call(
        functools.partial(_ffn_kernel, sub=sub, alpha=alpha, n_prompt=n_prompt, t_len=ts, d=d,
                          in_rows=in_rows, out_rows=out_rows),
        grid=(n_prompt + 1,),
        in_specs=[pl.BlockSpec((tm, d), lambda k: (jnp.minimum(k, last), 0)),
                  pl.BlockSpec((1, 3 * N_SUB, d), lambda k: (jnp.minimum(k, last) // per_seq, 0, 0)),
                  _const_spec((bs, ts * d)), _sub_mod_spec(bs, d, sub),
                  pl.BlockSpec(memory_space=pl.ANY), pl.BlockSpec(memory_space=pl.ANY),
                  pl.BlockSpec(memory_space=pl.ANY),
                  _const_spec(lng.shape), _const_spec(lnb.shape)],
        out_specs=[pl.BlockSpec((tm, d), lambda k: (jnp.minimum(k, last), 0)),
                   pl.BlockSpec((bs, ts * d), lambda k: (0, 0))],
        out_shape=[jax.ShapeDtypeStruct((bp * tp, d), F32),
                   jax.ShapeDtypeStruct((bs, ts * d), F32)],
        scratch_shapes=[pltpu.VMEM((d, f), BF16), pltpu.VMEM((d, f), BF16), pltpu.VMEM((f, d), BF16),
                        pltpu.VMEM((2, in_rows, f), F32), pltpu.VMEM((2, out_rows, d), F32),
                        pltpu.SemaphoreType.DMA((2,))],
        compiler_params=pltpu.CompilerParams(dimension_semantics=("arbitrary",),
                                             vmem_limit_bytes=V7X_VMEM_LIMIT),
        name=f"ffn{sub}",
    )(xp.reshape(bp * tp, d), mod_p, xs.reshape(bs, ts * d), mod_s, wg, wu, wd, lng, lnb)
    return yp.reshape(bp, tp, d), ys.reshape(bs, ts, d)


def _ffn_sample(x, mod, wg, wu, wd, lng, lnb, *, sub, alpha):
    bsz, t, d = x.shape
    f = wg.shape[1]
    x2 = x.reshape(bsz, t * d)
    out = pl.pallas_call(
        functools.partial(_ffn_sample_kernel, sub=sub, alpha=alpha, t_len=t, d=d),
        grid=(1,),
        in_specs=[_const_spec((bsz, t * d)), _sub_mod_spec(bsz, d, sub),
                  _const_spec((d, f)), _const_spec((d, f)), _const_spec((f, d)),
                  _const_spec(lng.shape), _const_spec(lnb.shape)],
        out_specs=pl.BlockSpec((bsz, t * d), lambda i: (0, 0)),
        out_shape=jax.ShapeDtypeStruct((bsz, t * d), F32),
        compiler_params=pltpu.CompilerParams(dimension_semantics=("arbitrary",),
                                             vmem_limit_bytes=V7X_VMEM_LIMIT),
        name=f"ffn_sample{sub}",
    )(x2, mod, wg, wu, wd, lng, lnb)
    return out.reshape(bsz, t, d)


def _tri_inverse(lmats, c):
    ls = [l.astype(BF16) for l in lmats]
    ms = [jnp.dot(l, l, preferred_element_type=F32) for l in ls]
    qs = ms
    power = 2
    while 2 * power < c:
        mb = [m.astype(BF16) for m in ms]
        ms = [jnp.dot(m, m, preferred_element_type=F32) for m in mb]
        qm = [jnp.dot(q.astype(BF16), m.astype(BF16), preferred_element_type=F32) for q, m in zip(qs, ms)]
        qs = [q + m + x for q, m, x in zip(qs, ms, qm)]
        power *= 2
    rows = lax.broadcasted_iota(jnp.int32, (c, c), 0)
    cols = lax.broadcasted_iota(jnp.int32, (c, c), 1)
    eye = (rows == cols).astype(F32)
    lq = [jnp.dot(l, q.astype(BF16), preferred_element_type=F32) for l, q in zip(ls, qs)]
    return [eye - l + q - x for l, q, x in zip(lmats, qs, lq)]


def _gdn_chunks_local(units, tril, strict):
    c = units[0][1].shape[0]
    a = [lax.dot_general(jnp.concatenate([kb, q], axis=0), k, (((1,), (1,)), ((), ())),
                         preferred_element_type=F32)
         for kb, q, k, _, _, _, _ in units]
    lmats, qks = [], []
    for a_u, (_, _, _, _, _, gcol, grow) in zip(a, units):
        decay = jnp.where(tril, jnp.exp(jnp.where(tril, gcol - grow, 0.0)), 0.0)
        lmats.append(jnp.where(strict, a_u[:c] * decay, 0.0))
        qks.append(jnp.where(tril, a_u[c:] * decay, 0.0))
    tms = _tri_inverse(lmats, c)
    uw = [jnp.dot(tm.astype(BF16), jnp.concatenate([vb, kbe], axis=1), preferred_element_type=F32)
          for tm, (_, _, _, vb, kbe, _, _) in zip(tms, units)]
    return [(x[:, :DN_DV], x[:, DN_DV:], qk) for x, qk in zip(uw, qks)]


def _gdn_chunk_state(units, states):
    c = units[0][2].shape[0]
    r = [jnp.dot(jnp.concatenate([w, q_dec], axis=0), s.astype(BF16), preferred_element_type=F32)
         for (w, q_dec, _, _, _, _), s in zip(units, states)]
    v_new = [(u - r_h[:c]).astype(BF16) for (_, _, u, _, _, _), r_h in zip(units, r)]
    o = [r_h[c:] + jnp.dot(qk, v, preferred_element_type=F32)
         for (_, _, _, qk, _, _), r_h, v in zip(units, r, v_new)]
    s_new = [s * g_last + lax.dot_general(k_dec, v, (((0,), (0,)), ((), ())), preferred_element_type=F32)
             for (_, _, _, _, k_dec, g_last), s, v in zip(units, states, v_new)]
    return o, s_new


def _tri_inverse_steps(lmats, c):
    ls = [l.astype(BF16) for l in lmats]
    ms = [jnp.dot(l, l, preferred_element_type=F32) for l in ls]
    yield
    qs = ms
    power = 2
    while 2 * power < c:
        mb = [m.astype(BF16) for m in ms]
        ms = [jnp.dot(m, m, preferred_element_type=F32) for m in mb]
        yield
        qm = [jnp.dot(q.astype(BF16), m.astype(BF16), preferred_element_type=F32) for q, m in zip(qs, ms)]
        yield
        qs = [q + m + x for q, m, x in zip(qs, ms, qm)]
        power *= 2
    rows = lax.broadcasted_iota(jnp.int32, (c, c), 0)
    cols = lax.broadcasted_iota(jnp.int32, (c, c), 1)
    eye = (rows == cols).astype(F32)
    lq = [jnp.dot(l, q.astype(BF16), preferred_element_type=F32) for l, q in zip(ls, qs)]
    yield
    return [eye - l + q - x for l, q, x in zip(lmats, qs, lq)]


def _gdn_local_steps(units, tril, strict):
    c = units[0][1].shape[0]
    a = [lax.dot_general(jnp.concatenate([kb, q], axis=0), k, (((1,), (1,)), ((), ())),
                         preferred_element_type=F32)
         for kb, q, k, _, _, _, _ in units]
    yield
    lmats, qks = [], []
    for a_u, (_, _, _, _, _, gcol, grow) in zip(a, units):
        decay = jnp.where(tril, jnp.exp(jnp.where(tril, gcol - grow, 0.0)), 0.0)
        lmats.append(jnp.where(strict, a_u[:c] * decay, 0.0))
        qks.append(jnp.where(tril, a_u[c:] * decay, 0.0))
    tms = yield from _tri_inverse_steps(lmats, c)
    uw = [jnp.dot(tm.astype(BF16), jnp.concatenate([vb, kbe], axis=1), preferred_element_type=F32)
          for tm, (_, _, _, vb, kbe, _, _) in zip(tms, units)]
    yield
    return [(x[:, :DN_DV], x[:, DN_DV:], qk) for x, qk in zip(uw, qks)]


def _run_interleaved(*gens):
    live = list(gens)
    while live:
        for g in list(live):
            try:
                next(g)
            except StopIteration:
                live.remove(g)


def _gate_norm(o, og, norm_g):
    o = o * lax.rsqrt(jnp.mean(o * o, axis=-1, keepdims=True) + RMS_EPS) * norm_g
    return o * _silu(og)


def _proj_layout(dn_width, sc_width):
    off = {}
    pos = 0
    for name, width in (("og", dn_width), ("sc_b", sc_width), ("sc_c", sc_width), ("sc_h", sc_width)):
        off[name] = (pos, pos + width)
        pos += width
    return off


def _project(u, wq_ref, wr_ref, wab_ref, off):
    p = {"qkv": jnp.dot(u, wq_ref[...], preferred_element_type=F32),
         "ab": jnp.dot(u, wab_ref[...], preferred_element_type=F32)}
    rest = jnp.dot(u, wr_ref[...], preferred_element_type=F32)
    for name, (lo, hi) in off.items():
        p[name] = rest[:, lo:hi]
    return p


def _cols(p, off, name):
    del off
    return p[name]


def _mix_out(x, o_dn, o_sc, gate, w_out_ref, g, b, alpha):
    dn_w = o_dn.shape[1]
    mix = (jnp.dot(o_dn.astype(BF16), w_out_ref[0:dn_w, :], preferred_element_type=F32)
           + jnp.dot(o_sc.astype(BF16), w_out_ref[dn_w:, :], preferred_element_type=F32))
    return _post(x, mix, gate, g, b, alpha)


def _mixer_prompt_kernel(x_ref, mod_ref, wq_ref, wr_ref, wab_ref, w_out_ref, cw_ref, mw_ref, alog_ref, dtb_ref, ng_ref,
                         lng_ref, lnb_ref,
                         y_ref, s_out_ref, cq_out_ref, cm_out_ref,
                         s_scr, cq_scr, cm_scr, gct_scr, gl_scr, ops_scr, u_scr, w_scr, qk_scr, o_scr,
                         *, alpha, nseq, tt, off, n_conv, n_mconv):
    t_idx = pl.program_id(1)
    rows_all = nseq * tt
    nchunk = rows_all // CHUNK
    per_seq = tt // CHUNK
    hk = DN_HEADS * DN_DK
    sub = 1

    @pl.when(t_idx == 0)
    def _():
        s_scr[...] = jnp.zeros_like(s_scr)
        cq_scr[...] = jnp.zeros_like(cq_scr)
        cm_scr[...] = jnp.zeros_like(cm_scr)

    xs = [x_ref[s] for s in range(nseq)]
    gates = [mod_ref[s, 3 * sub + 2:3 * sub + 3, :] for s in range(nseq)]
    u = jnp.concatenate(
        [(xs[s] * (1.0 + mod_ref[s, 3 * sub + 1:3 * sub + 2, :]) + mod_ref[s, 3 * sub + 0:3 * sub + 1, :])
         .astype(BF16) for s in range(nseq)], axis=0)
    p = _project(u, wq_ref, wr_ref, wab_ref, off)

    def causal_conv(new_rows, hist_scr, s, w_ref, taps):
        ext = jnp.concatenate([hist_scr[s], new_rows], axis=0)
        acc = new_rows * w_ref[taps - 1:taps, :]
        for j in range(taps - 1):
            acc = acc + pltpu.roll(ext, taps - 1 - j, axis=0)[SUBLANES:] * w_ref[j:j + 1, :]
        hist_scr[s] = new_rows[tt - SUBLANES:]
        return acc

    pq = _cols(p, off, "qkv")
    acts = []
    for s in range(nseq):
        pq_s = pq[s * tt:(s + 1) * tt]
        acts.append(_silu(causal_conv(pq_s, cq_scr, s, cw_ref, n_conv)))
        cq_out_ref[s] = pq_s[tt - (n_conv - 1):]
    act = jnp.concatenate(acts, axis=0)

    ab = _cols(p, off, "ab")
    beta_all = jax.nn.sigmoid(ab)
    g = -jnp.exp(alog_ref[...]) * _softplus(ab + dtb_ref[...])
    rows = lax.broadcasted_iota(jnp.int32, (CHUNK, CHUNK), 0)
    cols = lax.broadcasted_iota(jnp.int32, (CHUNK, CHUNK), 1)
    tril = rows >= cols
    strict = rows > cols
    tril_f = tril.astype(F32)
    gc_parts, gl_parts = [], []
    for c in range(nchunk):
        gc_c = _mm_f32(tril_f, g[c * CHUNK:(c + 1) * CHUNK, :])
        gc_parts.append(gc_c)
        gl_parts.append(jnp.broadcast_to(gc_c[CHUNK - 1:CHUNK, :], (CHUNK, AB_PAD)))
    gc = jnp.concatenate(gc_parts, axis=0)
    gl_rows = jnp.concatenate(gl_parts, axis=0)
    gc_t = gc.T
    for c in range(nchunk):
        gct_scr[c] = gc_t[0:SUBLANES, c * CHUNK:(c + 1) * CHUNK]

    eg_all = jnp.exp(gc)
    ekd_all = jnp.exp(gl_rows - gc)
    egl_all = jnp.exp(gl_rows)
    for h in range(DN_HEADS):
        lanes = slice(h * DN_DK, (h + 1) * DN_DK)
        q = _l2norm(act[:, h * DN_DK:(h + 1) * DN_DK]) * (DN_DK ** -0.5)
        k = _l2norm(act[:, hk + h * DN_DK:hk + (h + 1) * DN_DK])
        v = act[:, 2 * hk + h * DN_DV:2 * hk + (h + 1) * DN_DV]
        beta = beta_all[:, DN_HEADS + h:DN_HEADS + h + 1]
        eg = eg_all[:, h:h + 1]
        kb = k * beta
        ops_scr[0, :, lanes] = kb.astype(BF16)
        ops_scr[1, :, lanes] = q.astype(BF16)
        ops_scr[2, :, lanes] = k.astype(BF16)
        ops_scr[3, :, lanes] = (v * beta).astype(BF16)
        ops_scr[4, :, lanes] = (kb * eg).astype(BF16)
        ops_scr[5, :, lanes] = (q * eg).astype(BF16)
        ops_scr[6, :, lanes] = (k * ekd_all[:, h:h + 1]).astype(BF16)
        gl_scr[:, lanes] = jnp.broadcast_to(egl_all[:, h:h + 1], (rows_all, DN_DV))

    head_lanes = [slice(h * DN_DK, (h + 1) * DN_DK) for h in range(DN_HEADS)]

    where = [(c, h, lanes) for c in range(nchunk) for h, lanes in enumerate(head_lanes)]
    units = [tuple(ops_scr[i, c * CHUNK:(c + 1) * CHUNK, lanes] for i in range(5))
             + (gc[c * CHUNK:(c + 1) * CHUNK, h:h + 1], gct_scr[c, h:h + 1, :])
             for c, h, lanes in where]
    for (c, h, lanes), (u_c, w_c, qk_c) in zip(where, _gdn_chunks_local(units, tril, strict)):
        u_scr[c * CHUNK:(c + 1) * CHUNK, lanes] = u_c
        w_scr[c * CHUNK:(c + 1) * CHUNK, lanes] = w_c.astype(BF16)
        qk_scr[c * DN_HEADS + h] = qk_c.astype(BF16)

    for j in range(per_seq):
        chains = [(s, s * per_seq + j, h, lanes) for s in range(nseq) for h, lanes in enumerate(head_lanes)]
        units = [(w_scr[c * CHUNK:(c + 1) * CHUNK, lanes], ops_scr[5, c * CHUNK:(c + 1) * CHUNK, lanes],
                  u_scr[c * CHUNK:(c + 1) * CHUNK, lanes], qk_scr[c * DN_HEADS + h],
                  ops_scr[6, c * CHUNK:(c + 1) * CHUNK, lanes], gl_scr[c * CHUNK:c * CHUNK + 1, lanes])
                 for s, c, h, lanes in chains]
        o_c, s_new = _gdn_chunk_state(units, [s_scr[s * DN_HEADS + h] for s, _, h, _ in chains])
        for (s, c, h, lanes), o_u, s_u in zip(chains, o_c, s_new):
            s_scr[s * DN_HEADS + h] = s_u
            o_scr[c * CHUNK:(c + 1) * CHUNK, lanes] = o_u
    for s in range(nseq):
        s_out_ref[s] = s_scr[s * DN_HEADS:(s + 1) * DN_HEADS]

    og = _cols(p, off, "og")
    o_dn = jnp.concatenate(
        [_gate_norm(o_scr[:, h * DN_DV:(h + 1) * DN_DV], og[:, h * DN_DV:(h + 1) * DN_DV], ng_ref[...])
         for h in range(DN_HEADS)], axis=1)

    z = _cols(p, off, "sc_c") * _cols(p, off, "sc_h")
    zcs = []
    for s in range(nseq):
        z_s = z[s * tt:(s + 1) * tt]
        zcs.append(causal_conv(z_s, cm_scr, s, mw_ref, n_mconv))
        cm_out_ref[s] = z_s[tt - (n_mconv - 1):]
    o_sc = _cols(p, off, "sc_b") * jnp.concatenate(zcs, axis=0)

    dn_w = o_dn.shape[1]
    mix = (jnp.dot(o_dn.astype(BF16), w_out_ref[0:dn_w, :], preferred_element_type=F32)
           + jnp.dot(o_sc.astype(BF16), w_out_ref[dn_w:, :], preferred_element_type=F32))
    for s in range(nseq):
        y_ref[s] = _post(xs[s], mix[s * tt:(s + 1) * tt], gates[s],
                         lng_ref[sub:sub + 1, :], lnb_ref[sub:sub + 1, :], alpha)


def _mixer_prompt(x, mod, w_in, w_out, conv_w, mconv_w, alog, dtb, norm_g, lng, lnb, *, alpha, nseq, tt, off):
    bsz, t, d = x.shape
    assert t % tt == 0 and tt % LANES == 0 and bsz % nseq == 0
    nt = t // tt
    rows = nseq * tt
    qkv_dim = conv_w.shape[-1]
    sc_w = off["sc_c"][1] - off["sc_c"][0]
    dn_w = DN_HEADS * DN_DV
    n_conv, n_mconv = conv_w.shape[0], mconv_w.shape[0]
    assert max(n_conv, n_mconv) - 1 <= SUBLANES <= tt
    nchunk = rows // CHUNK
    kern = functools.partial(_mixer_prompt_kernel, alpha=alpha, nseq=nseq, tt=tt, off=off,
                             n_conv=n_conv, n_mconv=n_mconv)
    y, s_out, cq, cm = pl.pallas_call(
        kern,
        grid=(bsz // nseq, nt),
        in_specs=[pl.BlockSpec((nseq, tt, d), lambda b, i: (b, i, 0)),
                  pl.BlockSpec((nseq, 3 * N_SUB, d), lambda b, i: (b, 0, 0)),
                  *[_const_spec(w.shape) for w in w_in], _const_spec(w_out.shape),
                  _const_spec(conv_w.shape), _const_spec(mconv_w.shape),
                  _const_spec(alog.shape), _const_spec(dtb.shape), _const_spec(norm_g.shape),
                  _const_spec(lng.shape), _const_spec(lnb.shape)],
        out_specs=[pl.BlockSpec((nseq, tt, d), lambda b, i: (b, i, 0)),
                   pl.BlockSpec((nseq, DN_HEADS, DN_DK, DN_DV), lambda b, i: (b, 0, 0, 0)),
                   pl.BlockSpec((nseq, n_conv - 1, qkv_dim), lambda b, i: (b, 0, 0)),
                   pl.BlockSpec((nseq, n_mconv - 1, sc_w), lambda b, i: (b, 0, 0))],
        out_shape=[jax.ShapeDtypeStruct((bsz, t, d), F32),
                   jax.ShapeDtypeStruct((bsz, DN_HEADS, DN_DK, DN_DV), F32),
                   jax.ShapeDtypeStruct((bsz, n_conv - 1, qkv_dim), F32),
                   jax.ShapeDtypeStruct((bsz, n_mconv - 1, sc_w), F32)],
        scratch_shapes=[pltpu.VMEM((nseq * DN_HEADS, DN_DK, DN_DV), F32),
                        pltpu.VMEM((nseq, SUBLANES, qkv_dim), F32),
                        pltpu.VMEM((nseq, SUBLANES, sc_w), F32),
                        pltpu.VMEM((nchunk, SUBLANES, CHUNK), F32),
                        pltpu.VMEM((rows, dn_w), F32),
                        pltpu.VMEM((7, rows, dn_w), BF16),
                        pltpu.VMEM((rows, dn_w), F32),
                        pltpu.VMEM((rows, dn_w), BF16),
                        pltpu.VMEM((nchunk * DN_HEADS, CHUNK, CHUNK), BF16),
                        pltpu.VMEM((rows, dn_w), F32)],
        compiler_params=pltpu.CompilerParams(dimension_semantics=("arbitrary", "arbitrary"),
                                             vmem_limit_bytes=V7X_VMEM_LIMIT),
        name="mixer_prompt",
    )(x, mod, *w_in, w_out, conv_w, mconv_w, alog, dtb, norm_g, lng, lnb)
    return y, s_out, cq, cm


def _mixer_prompt_pipe_kernel(xf_ref, modf_ref, xb_ref, modb_ref, wq_ref, wr_ref, wab_ref, w_out_ref, cw_ref,
                              mw_ref, alog_ref, dtb_ref, ng_ref, lng_ref, lnb_ref,
                              y_ref, s_out_ref, cq_out_ref, cm_out_ref,
                              s_scr, cq_scr, cm_scr, gct_scr,
                              u_scr, w_scr, qk_scr, qd_scr, kd_scr, gl_scr, og_scr, osc_scr,
                              *, alpha, nseq, tt, tiles_per_group, n_tiles, off, n_conv, n_mconv):
    k = pl.program_id(0)
    rows_all = nseq * tt
    nchunk = rows_all // CHUNK
    per_seq = tt // CHUNK
    hk = DN_HEADS * DN_DK
    sub = 1
    head_lanes = [slice(h * DN_DK, (h + 1) * DN_DK) for h in range(DN_HEADS)]
    t_front = lax.rem(jnp.minimum(k, n_tiles - 1), tiles_per_group)
    t_back = lax.rem(jnp.maximum(k - 1, 0), tiles_per_group)
    keep_front = jnp.where(t_front == 0, 0.0, 1.0).astype(F32)
    keep_back = jnp.where(t_back == 0, 0.0, 1.0).astype(F32)

    def causal_conv(new_rows, hist_scr, s, w_ref, taps):
        ext = jnp.concatenate([hist_scr[s] * keep_front, new_rows], axis=0)
        acc = new_rows * w_ref[taps - 1:taps, :]
        for j in range(taps - 1):
            acc = acc + pltpu.roll(ext, taps - 1 - j, axis=0)[SUBLANES:] * w_ref[j:j + 1, :]
        hist_scr[s] = new_rows[tt - SUBLANES:]
        return acc

    def front():
        u = jnp.concatenate(
            [(xf_ref[s] * (1.0 + modf_ref[s, 3 * sub + 1:3 * sub + 2, :])
              + modf_ref[s, 3 * sub + 0:3 * sub + 1, :]).astype(BF16) for s in range(nseq)], axis=0)
        pq = jnp.dot(u, wq_ref[...], preferred_element_type=F32)
        yield
        rest = jnp.dot(u, wr_ref[...], preferred_element_type=F32)
        yield
        ab = jnp.dot(u, wab_ref[...], preferred_element_type=F32)
        p = {name: rest[:, lo:hi] for name, (lo, hi) in off.items()}

        beta_all = jax.nn.sigmoid(ab)
        g = -jnp.exp(alog_ref[...]) * _softplus(ab + dtb_ref[...])
        rows = lax.broadcasted_iota(jnp.int32, (CHUNK, CHUNK), 0)
        cols = lax.broadcasted_iota(jnp.int32, (CHUNK, CHUNK), 1)
        tril = rows >= cols
        strict = rows > cols
        tril_f = tril.astype(F32)
        gc_parts, gl_parts = [], []
        for c in range(nchunk):
            gc_c = _mm_f32(tril_f, g[c * CHUNK:(c + 1) * CHUNK, :])
            gc_parts.append(gc_c)
            gl_parts.append(jnp.broadcast_to(gc_c[CHUNK - 1:CHUNK, :], (CHUNK, AB_PAD)))
        yield
        gc = jnp.concatenate(gc_parts, axis=0)
        gl_rows = jnp.concatenate(gl_parts, axis=0)
        gc_t = gc.T
        for c in range(nchunk):
            gct_scr[c] = gc_t[0:SUBLANES, c * CHUNK:(c + 1) * CHUNK]

        acts = []
        for s in range(nseq):
            pq_s = pq[s * tt:(s + 1) * tt]
            acts.append(_silu(causal_conv(pq_s, cq_scr, s, cw_ref, n_conv)))
            cq_out_ref[s] = pq_s[tt - (n_conv - 1):]
        act = jnp.concatenate(acts, axis=0)

        eg_all = jnp.exp(gc)
        ekd_all = jnp.exp(gl_rows - gc)
        egl_all = jnp.exp(gl_rows)
        per_head = []
        for h, lanes in enumerate(head_lanes):
            q = _l2norm(act[:, h * DN_DK:(h + 1) * DN_DK]) * (DN_DK ** -0.5)
            kk = _l2norm(act[:, hk + h * DN_DK:hk + (h + 1) * DN_DK])
            v = act[:, 2 * hk + h * DN_DV:2 * hk + (h + 1) * DN_DV]
            beta = beta_all[:, DN_HEADS + h:DN_HEADS + h + 1]
            eg = eg_all[:, h:h + 1]
            kb = kk * beta
            per_head.append((kb.astype(BF16), q.astype(BF16), kk.astype(BF16), (v * beta).astype(BF16),
                             (kb * eg).astype(BF16)))
            qd_scr[:, lanes] = (q * eg).astype(BF16)
            kd_scr[:, lanes] = (kk * ekd_all[:, h:h + 1]).astype(BF16)
            for c in range(nchunk):
                gl_scr[c * SUBLANES:(c + 1) * SUBLANES, lanes] = jnp.broadcast_to(
                    egl_all[c * CHUNK:c * CHUNK + SUBLANES, h:h + 1], (SUBLANES, DN_DV))

        z = p["sc_c"] * p["sc_h"]
        zcs = []
        for s in range(nseq):
            z_s = z[s * tt:(s + 1) * tt]
            zcs.append(causal_conv(z_s, cm_scr, s, mw_ref, n_mconv))
            cm_out_ref[s] = z_s[tt - (n_mconv - 1):]
        osc_scr[...] = p["sc_b"] * jnp.concatenate(zcs, axis=0)
        og_scr[...] = p["og"]

        where = [(c, h, lanes) for c in range(nchunk) for h, lanes in enumerate(head_lanes)]
        units = [tuple(arr[c * CHUNK:(c + 1) * CHUNK] for arr in per_head[h])
                 + (gc[c * CHUNK:(c + 1) * CHUNK, h:h + 1], gct_scr[c, h:h + 1, :])
                 for c, h, lanes in where]
        local = yield from _gdn_local_steps(units, tril, strict)
        for (c, h, lanes), (u_c, w_c, qk_c) in zip(where, local):
            u_scr[c * CHUNK:(c + 1) * CHUNK, lanes] = u_c
            w_scr[c * CHUNK:(c + 1) * CHUNK, lanes] = w_c.astype(BF16)
            qk_scr[c * DN_HEADS + h] = qk_c.astype(BF16)

    def back():
        u_all, w_all, qd_all, kd_all = u_scr[...], w_scr[...], qd_scr[...], kd_scr[...]
        gl_all, og, o_sc = gl_scr[...], og_scr[...], osc_scr[...]
        qks = [qk_scr[i] for i in range(nchunk * DN_HEADS)]
        xs = [xb_ref[s] for s in range(nseq)]
        gates = [modb_ref[s, 3 * sub + 2:3 * sub + 3, :] for s in range(nseq)]
        states = [s_scr[i] * keep_back for i in range(nseq * DN_HEADS)]
        o_rows = [[None] * DN_HEADS for _ in range(nchunk)]
        for j in range(per_seq):
            chains = [(s, s * per_seq + j, h, lanes) for s in range(nseq) for h, lanes in enumerate(head_lanes)]
            rws = [slice(c * CHUNK, (c + 1) * CHUNK) for _, c, _, _ in chains]
            r = [jnp.dot(jnp.concatenate([w_all[rw, lanes], qd_all[rw, lanes]], axis=0),
                         states[s * DN_HEADS + h].astype(BF16), preferred_element_type=F32)
                 for (s, c, h, lanes), rw in zip(chains, rws)]
            yield
            v_new = [(u_all[rw, lanes] - r_u[:CHUNK]).astype(BF16)
                     for (s, c, h, lanes), rw, r_u in zip(chains, rws, r)]
            for (s, c, h, lanes), r_u, v_u in zip(chains, r, v_new):
                o_rows[c][h] = r_u[CHUNK:] + jnp.dot(qks[c * DN_HEADS + h], v_u, preferred_element_type=F32)
            for (s, c, h, lanes), rw, v_u in zip(chains, rws, v_new):
                i = s * DN_HEADS + h
                states[i] = (states[i] * gl_all[c * SUBLANES:c * SUBLANES + 1, lanes]
                             + lax.dot_general(kd_all[rw, lanes], v_u, (((0,), (0,)), ((), ())),
                                               preferred_element_type=F32))
            yield
        for i, st in enumerate(states):
            s_scr[i] = st
        for s in range(nseq):
            for h in range(DN_HEADS):
                s_out_ref[s, h] = states[s * DN_HEADS + h]

        o_dn = jnp.concatenate(
            [_gate_norm(jnp.concatenate([o_rows[c][h] for c in range(nchunk)], axis=0),
                        og[:, lanes], ng_ref[...]) for h, lanes in enumerate(head_lanes)], axis=1)
        dn_w = o_dn.shape[1]
        mix = (jnp.dot(o_dn.astype(BF16), w_out_ref[0:dn_w, :], preferred_element_type=F32)
               + jnp.dot(o_sc.astype(BF16), w_out_ref[dn_w:, :], preferred_element_type=F32))
        yield
        for s in range(nseq):
            y_ref[s] = _post(xs[s], mix[s * tt:(s + 1) * tt], gates[s],
                             lng_ref[sub:sub + 1, :], lnb_ref[sub:sub + 1, :], alpha)

    @pl.when(k == 0)
    def _():
        s_scr[...] = jnp.zeros_like(s_scr)
        cq_scr[...] = jnp.zeros_like(cq_scr)
        cm_scr[...] = jnp.zeros_like(cm_scr)
        _run_interleaved(front())

    @pl.when(jnp.logical_and(k > 0, k < n_tiles))
    def _():
        _run_interleaved(back(), front())

    @pl.when(k == n_tiles)
    def _():
        _run_interleaved(back())


def _mixer_prompt_pipe(x, mod, w_in, w_out, conv_w, mconv_w, alog, dtb, norm_g, lng, lnb, *, alpha, nseq, tt, off):
    bsz, t, d = x.shape
    assert t % tt == 0 and tt % LANES == 0 and bsz % nseq == 0
    nt = t // tt
    n_tiles = (bsz // nseq) * nt
    rows = nseq * tt
    qkv_dim = conv_w.shape[-1]
    sc_w = off["sc_c"][1] - off["sc_c"][0]
    dn_w = DN_HEADS * DN_DV
    n_conv, n_mconv = conv_w.shape[0], mconv_w.shape[0]
    assert max(n_conv, n_mconv) - 1 <= SUBLANES <= tt
    nchunk = rows // CHUNK
    kern = functools.partial(_mixer_prompt_pipe_kernel, alpha=alpha, nseq=nseq, tt=tt, tiles_per_group=nt,
                             n_tiles=n_tiles, off=off, n_conv=n_conv, n_mconv=n_mconv)

    def front_tile(k):
        return jnp.minimum(k, n_tiles - 1)

    def back_tile(k):
        return jnp.maximum(k - 1, 0)

    y, s_out, cq, cm = pl.pallas_call(
        kern,
        grid=(n_tiles + 1,),
        in_specs=[pl.BlockSpec((nseq, tt, d), lambda k: (front_tile(k) // nt, front_tile(k) % nt, 0)),
                  pl.BlockSpec((nseq, 3 * N_SUB, d), lambda k: (front_tile(k) // nt, 0, 0)),
                  pl.BlockSpec((nseq, tt, d), lambda k: (back_tile(k) // nt, back_tile(k) % nt, 0)),
                  pl.BlockSpec((nseq, 3 * N_SUB, d), lambda k: (back_tile(k) // nt, 0, 0)),
                  *[_const_spec(w.shape) for w in w_in], _const_spec(w_out.shape),
                  _const_spec(conv_w.shape), _const_spec(mconv_w.shape),
                  _const_spec(alog.shape), _const_spec(dtb.shape), _const_spec(norm_g.shape),
                  _const_spec(lng.shape), _const_spec(lnb.shape)],
        out_specs=[pl.BlockSpec((nseq, tt, d), lambda k: (back_tile(k) // nt, back_tile(k) % nt, 0)),
                   pl.BlockSpec((nseq, DN_HEADS, DN_DK, DN_DV), lambda k: (back_tile(k) // nt, 0, 0, 0)),
                   pl.BlockSpec((nseq, n_conv - 1, qkv_dim), lambda k: (front_tile(k) // nt, 0, 0)),
                   pl.BlockSpec((nseq, n_mconv - 1, sc_w), lambda k: (front_tile(k) // nt, 0, 0))],
        out_shape=[jax.ShapeDtypeStruct((bsz, t, d), F32),
                   jax.ShapeDtypeStruct((bsz, DN_HEADS, DN_DK, DN_DV), F32),
                   jax.ShapeDtypeStruct((bsz, n_conv - 1, qkv_dim), F32),
                   jax.ShapeDtypeStruct((bsz, n_mconv - 1, sc_w), F32)],
        scratch_shapes=[pltpu.VMEM((nseq * DN_HEADS, DN_DK, DN_DV), F32),
                        pltpu.VMEM((nseq, SUBLANES, qkv_dim), F32),
                        pltpu.VMEM((nseq, SUBLANES, sc_w), F32),
                        pltpu.VMEM((nchunk, SUBLANES, CHUNK), F32),
                        pltpu.VMEM((rows, dn_w), F32),
                        pltpu.VMEM((rows, dn_w), BF16),
                        pltpu.VMEM((nchunk * DN_HEADS, CHUNK, CHUNK), BF16),
                        pltpu.VMEM((rows, dn_w), BF16),
                        pltpu.VMEM((rows, dn_w), BF16),
                        pltpu.VMEM((nchunk * SUBLANES, dn_w), F32),
                        pltpu.VMEM((rows, dn_w), F32),
                        pltpu.VMEM((rows, sc_w), F32)],
        compiler_params=pltpu.CompilerParams(dimension_semantics=("arbitrary",),
                                             vmem_limit_bytes=V7X_VMEM_LIMIT),
        name="mixer_prompt",
    )(x, mod, x, mod, *w_in, w_out, conv_w, mconv_w, alog, dtb, norm_g, lng, lnb)
    return y, s_out, cq, cm


_PACK = SUBLANES


def _sample_phase_a(x_ref, mod_ref, wq_ref, wr_ref, wab_ref, cw_ref, mw_ref, alog_ref, dtb_ref, cq_in_ref, cm_in_ref,
                    cq_out_ref, cm_out_ref, og_scr, osc_scr, lhs_scr, u_scr, kd_scr, gl_scr, qk_scr,
                    *, t_len, d, off, n_conv, n_mconv):
    bsz = x_ref.shape[0]
    hk = DN_HEADS * DN_DK
    qkv_dim = cw_ref.shape[-1]
    sc_w = off["sc_c"][1] - off["sc_c"][0]

    x = _stack_time(x_ref, t_len, d)
    u = (x * (1.0 + _mod_rows(mod_ref, 1, t_len, d)) + _mod_rows(mod_ref, 0, t_len, d)).astype(BF16)
    p = _project(u, wq_ref, wr_ref, wab_ref, off)

    def rows(arr, t):
        return arr[t * bsz:(t + 1) * bsz]

    pq = _cols(p, off, "qkv")
    seq = ([cq_in_ref[:, j * qkv_dim:(j + 1) * qkv_dim] for j in range(n_conv - 1)]
           + [rows(pq, t) for t in range(t_len)])
    act = []
    for t in range(t_len):
        acc = None
        for j in range(n_conv):
            term = seq[t + j] * cw_ref[j:j + 1, :]
            acc = term if acc is None else acc + term
        act.append(_silu(acc))
    for j in range(n_conv - 1):
        cq_out_ref[:, j * qkv_dim:(j + 1) * qkv_dim] = seq[t_len + j]

    z = _cols(p, off, "sc_c") * _cols(p, off, "sc_h")
    zseq = ([cm_in_ref[:, j * sc_w:(j + 1) * sc_w] for j in range(n_mconv - 1)]
            + [rows(z, t) for t in range(t_len)])
    zc = []
    for t in range(t_len):
        acc = None
        for j in range(n_mconv):
            term = zseq[t + j] * mw_ref[j:j + 1, :]
            acc = term if acc is None else acc + term
        zc.append(acc)
    for j in range(n_mconv - 1):
        cm_out_ref[:, j * sc_w:(j + 1) * sc_w] = zseq[t_len + j]
    osc_scr[...] = _cols(p, off, "sc_b") * jnp.concatenate(zc, axis=0)
    og_scr[...] = _cols(p, off, "og")

    ab = _cols(p, off, "ab")
    beta_all = jax.nn.sigmoid(ab)
    g_all = -jnp.exp(alog_ref[...]) * _softplus(ab + dtb_ref[...])
    zero = jnp.zeros((bsz, LANES), F32)

    for h in range(DN_HEADS):
        q = [_l2norm(a[:, h * DN_DK:(h + 1) * DN_DK]) * (DN_DK ** -0.5) for a in act]
        k = [_l2norm(a[:, hk + h * DN_DK:hk + (h + 1) * DN_DK]) for a in act]
        v = [a[:, 2 * hk + h * DN_DV:2 * hk + (h + 1) * DN_DV] for a in act]
        beta = [rows(beta_all, t)[:, DN_HEADS + h:DN_HEADS + h + 1] for t in range(t_len)]
        g = [rows(g_all, t)[:, h:h + 1] for t in range(t_len)]
        gc = [g[0]]
        for t in range(1, t_len):
            gc.append(gc[-1] + g[t])
        kb = [k[t] * beta[t] for t in range(t_len)]
        vb = [v[t] * beta[t] for t in range(t_len)]
        eg = [jnp.exp(gc[t]) for t in range(t_len)]
        kbe = [kb[t] * eg[t] for t in range(t_len)]
        lm = [[None] * t_len for _ in range(t_len)]
        n_qk = 0
        for i in range(t_len):
            for j in range(i + 1):
                dec = jnp.exp(gc[i] - gc[j])
                qk_scr[h, n_qk] = jnp.broadcast_to(_rowsum(q[i] * k[j]) * dec, (bsz, LANES))
                n_qk += 1
                if j < i:
                    lm[i][j] = _rowsum(kb[i] * k[j]) * dec
        tm = [[None] * t_len for _ in range(t_len)]
        for i in range(t_len):
            for j in range(i):
                acc = lm[i][j]
                for m in range(j + 1, i):
                    acc = acc + lm[i][m] * tm[m][j]
                tm[i][j] = -acc
        for i in range(t_len):
            ui, wi = vb[i], kbe[i]
            for j in range(i):
                ui = ui + tm[i][j] * vb[j]
                wi = wi + tm[i][j] * kbe[j]
            lhs_scr[h, pl.ds(i, bsz, stride=_PACK), :] = wi
            lhs_scr[h, pl.ds(t_len + i, bsz, stride=_PACK), :] = q[i] * eg[i]
            u_scr[h, pl.ds(i, bsz, stride=_PACK), :] = ui
            u_scr[h, pl.ds(t_len + i, bsz, stride=_PACK), :] = zero
            kd_scr[h, pl.ds(i, bsz, stride=_PACK), :] = k[i] * jnp.exp(gc[t_len - 1] - gc[i])
            kd_scr[h, pl.ds(t_len + i, bsz, stride=_PACK), :] = zero
        gl_scr[h] = jnp.broadcast_to(jnp.exp(gc[t_len - 1]), (bsz, LANES))


def _sample_phase_c(x_ref, mod_ref, w_out_ref, ng_ref, lng_ref, lnb_ref, y_ref,
                    og_scr, osc_scr, res_scr, vn_scr, qk_scr, *, alpha, t_len, d):
    bsz = x_ref.shape[0]
    sub = 1
    x = _stack_time(x_ref, t_len, d)
    gate = _mod_rows(mod_ref, 2, t_len, d)
    og = og_scr[...]
    per_t = []
    for i in range(t_len):
        heads = []
        for h in range(DN_HEADS):
            oi = res_scr[h, pl.ds(t_len + i, bsz, stride=_PACK), :]
            base = i * (i + 1) // 2
            for j in range(i + 1):
                oi = oi + qk_scr[h, base + j] * vn_scr[h, pl.ds(j, bsz, stride=_PACK), :]
            heads.append(_gate_norm(oi, og[i * bsz:(i + 1) * bsz, h * DN_DV:(h + 1) * DN_DV], ng_ref[...]))
        per_t.append(jnp.concatenate(heads, axis=1))
    o_dn = jnp.concatenate(per_t, axis=0)
    y = _mix_out(x, o_dn, osc_scr[...], gate, w_out_ref,
                 lng_ref[sub:sub + 1, :], lnb_ref[sub:sub + 1, :], alpha)
    for t in range(t_len):
        y_ref[:, t * d:(t + 1) * d] = y[t * bsz:(t + 1) * bsz]


def _mixer_sample_kernel(x_ref, mod_ref, wq_ref, wr_ref, wab_ref, w_out_ref, cw_ref, mw_ref, alog_ref, dtb_ref, ng_ref,
                         lng_ref, lnb_ref, cq_in_ref, cm_in_ref, s_in_ref,
                         y_ref, cq_out_ref, cm_out_ref, s_out_ref,
                         og_scr, osc_scr, lhs_scr, u_scr, kd_scr, gl_scr, qk_scr, res_scr, vn_scr,
                         *, alpha, t_len, d, off, n_conv, n_mconv, bt):
    step = pl.program_id(0)
    assert 2 * t_len == _PACK, "packed buffers hold [T rows | T rows] per sequence"

    @pl.when(step == 0)
    def _():
        _sample_phase_a(x_ref, mod_ref, wq_ref, wr_ref, wab_ref, cw_ref, mw_ref, alog_ref, dtb_ref, cq_in_ref, cm_in_ref,
                        cq_out_ref, cm_out_ref, og_scr, osc_scr, lhs_scr, u_scr, kd_scr, gl_scr, qk_scr,
                        t_len=t_len, d=d, off=off, n_conv=n_conv, n_mconv=n_mconv)

    top = (lax.broadcasted_iota(jnp.int32, (_PACK, LANES), 0) < t_len).astype(F32)

    chains = [(bl, h) for bl in range(bt) for h in range(DN_HEADS)]
    seq0 = step * bt
    row0 = [pl.multiple_of((seq0 + bl) * _PACK, _PACK) for bl in range(bt)]
    states = [s_in_ref[bl, h] for bl, h in chains]
    r = [_mm(lhs_scr[h, pl.ds(row0[bl], _PACK), :], s) for (bl, h), s in zip(chains, states)]
    vns = []
    for (bl, h), r_u in zip(chains, r):
        res_scr[h, pl.ds(row0[bl], _PACK), :] = r_u
        vn = (u_scr[h, pl.ds(row0[bl], _PACK), :] - r_u) * top
        vn_scr[h, pl.ds(row0[bl], _PACK), :] = vn
        vns.append(vn)
    upd = [_mm_at(kd_scr[h, pl.ds(row0[bl], _PACK), :], vn) for (bl, h), vn in zip(chains, vns)]
    for (bl, h), s, s_add in zip(chains, states, upd):
        s_out_ref[bl, h] = s * gl_scr[h, pl.ds(seq0 + bl, 1), :] + s_add

    @pl.when(step == pl.num_programs(0) - 1)
    def _():
        _sample_phase_c(x_ref, mod_ref, w_out_ref, ng_ref, lng_ref, lnb_ref, y_ref,
                        og_scr, osc_scr, res_scr, vn_scr, qk_scr, alpha=alpha, t_len=t_len, d=d)


def _mixer_sample(x, mod, w_in, w_out, conv_w, mconv_w, alog, dtb, norm_g, lng, lnb, s_in, cq_in, cm_in,
                  *, alpha, off, bt):
    bsz, t, d = x.shape
    assert bsz % bt == 0
    qkv_dim = conv_w.shape[-1]
    sc_w = off["sc_c"][1] - off["sc_c"][0]
    dn_w = DN_HEADS * DN_DV
    n_conv, n_mconv = conv_w.shape[0], mconv_w.shape[0]
    n_qk = t * (t + 1) // 2
    x2 = x.reshape(bsz, t * d)
    cq2 = cq_in.reshape(bsz, (n_conv - 1) * qkv_dim)
    cm2 = cm_in.reshape(bsz, (n_mconv - 1) * sc_w)
    kern = functools.partial(_mixer_sample_kernel, alpha=alpha, t_len=t, d=d, off=off,
                             n_conv=n_conv, n_mconv=n_mconv, bt=bt)
    consts = (*w_in, w_out, conv_w, mconv_w, alog, dtb, norm_g, lng, lnb, cq2, cm2)
    state_spec = pl.BlockSpec((bt, DN_HEADS, DN_DK, DN_DV), lambda i: (i, 0, 0, 0))
    packed = pltpu.VMEM((DN_HEADS, _PACK * bsz, LANES), F32)
    y, cq, cm, s_out = pl.pallas_call(
        kern,
        grid=(bsz // bt,),
        in_specs=([_const_spec(x2.shape), _sub_mod_spec(bsz, d, 1)]
                  + [_const_spec(a.shape) for a in consts] + [state_spec]),
        out_specs=[pl.BlockSpec(x2.shape, lambda i: (0, 0)),
                   pl.BlockSpec(cq2.shape, lambda i: (0, 0)),
                   pl.BlockSpec(cm2.shape, lambda i: (0, 0)),
                   state_spec],
        out_shape=[jax.ShapeDtypeStruct(x2.shape, F32),
                   jax.ShapeDtypeStruct(cq2.shape, F32),
                   jax.ShapeDtypeStruct(cm2.shape, F32),
                   jax.ShapeDtypeStruct(s_in.shape, F32)],
        scratch_shapes=[pltpu.VMEM((t * bsz, dn_w), F32),
                        pltpu.VMEM((t * bsz, sc_w), F32),
                        packed,
                        packed,
                        packed,
                        pltpu.VMEM((DN_HEADS, bsz, LANES), F32),
                        pltpu.VMEM((DN_HEADS, n_qk, bsz, LANES), F32),
                        packed,
                        packed],
        compiler_params=pltpu.CompilerParams(dimension_semantics=("arbitrary",),
                                             vmem_limit_bytes=V7X_VMEM_LIMIT),
        name="mixer_sample",
    )(x2, mod, *consts, s_in)
    return (y.reshape(bsz, t, d), s_out, cq.reshape(bsz, n_conv - 1, qkv_dim),
            cm.reshape(bsz, n_mconv - 1, sc_w))


def _split_w_in(w_in, qkv_dim):
    c1 = qkv_dim + 2 * DN_HEADS
    wab = jnp.pad(w_in[:, qkv_dim:c1], ((0, 0), (0, AB_PAD - 2 * DN_HEADS)))
    return w_in[:, :qkv_dim].astype(BF16), w_in[:, c1:].astype(BF16), wab.astype(BF16)


def _pad_lanes(v):
    return jnp.zeros((1, AB_PAD), F32).at[0, :v.shape[0]].set(v)


def kernel(x_prompt, x_sample, state_ssm, state_conv_qkv, state_conv_mix, c_prompt, c_sample, w_ada, b_ada, ln_g, ln_b, ffn1_wg, ffn1_wu, ffn1_wd, ffn2_wg, ffn2_wu, ffn2_wd, w_in, conv_qkv_w, a_log, dt_bias, dn_norm_g, conv_mix_w, w_out):
    depth = w_ada.shape[0]
    alpha = (2 * depth) ** 0.25
    bp, tp, d = x_prompt.shape
    qkv_dim = conv_qkv_w.shape[-1]
    sc_width = conv_mix_w.shape[-1]
    dn_width = DN_HEADS * DN_DV
    off = _proj_layout(dn_width, sc_width)
    tm = min(512, tp)
    nseq = 2 if bp % 2 == 0 else 1
    tt = min(512 // nseq, tp)
    bt = min(8, x_sample.shape[0])

    hp, hs = x_prompt, x_sample
    outs = [[] for _ in range(6)]
    for l in range(depth):
        c_all = jnp.concatenate([c_prompt, c_sample], axis=0)
        mod = _adaln(c_all, w_ada[l], b_ada[l].reshape(1, -1), tn=d)
        mod_p = mod[:bp].reshape(bp, 3 * N_SUB, d)
        mod_s = mod[bp:]
        lng, lnb = ln_g[l], ln_b[l]
        w1 = (ffn1_wg[l], ffn1_wu[l], ffn1_wd[l])
        w2 = (ffn2_wg[l], ffn2_wu[l], ffn2_wd[l])
        w_in_p = _split_w_in(w_in[l], qkv_dim)
        w_out_b = w_out[l].astype(BF16)
        alog, dtb = _pad_lanes(a_log[l]), _pad_lanes(dt_bias[l])
        norm_g = dn_norm_g[l].reshape(1, -1)
        mix_args = (w_in_p, w_out_b, conv_qkv_w[l], conv_mix_w[l], alog, dtb, norm_g, lng, lnb)

        hp, hs = _ffn(hp, hs, mod_p, mod_s, *w1, lng, lnb, sub=0, alpha=alpha, tm=tm)
        hp, a1, a2, a3 = _mixer_prompt_pipe(hp, mod_p, *mix_args, alpha=alpha, nseq=nseq, tt=tt, off=off)
        hs, b1, b2, b3 = _mixer_sample(hs, mod_s, *mix_args, state_ssm[l], state_conv_qkv[l],
                                       state_conv_mix[l], alpha=alpha, off=off, bt=bt)
        hp, hs = _ffn(hp, hs, mod_p, mod_s, *w2, lng, lnb, sub=2, alpha=alpha, tm=tm)
        for lst, val in zip(outs, (a1, a2, a3, b1, b2, b3)):
            lst.append(val)
    return (hp, hs) + tuple(jnp.stack(o) for o in outs)
```

```python
import functools

import jax
import jax.numpy as jnp
from jax import lax
from jax.experimental import pallas as pl
from jax.experimental.pallas import tpu as pltpu

F32 = jnp.float32
BF16 = jnp.bfloat16

LN_EPS = 1e-5
RMS_EPS = 1e-6
N_SUB = 3
DN_HEADS = 4
DN_DK = 128
DN_DV = 128
CHUNK = 64
LANES = 128
SUBLANES = 8
AB_PAD = LANES

V7X_VMEM_LIMIT = 60000 * 1024
CAST_BUFFERS = 4


def _mm(a, b):
    return jnp.dot(a.astype(BF16), b.astype(BF16), preferred_element_type=F32)


def _mm_bt(a, b):
    return lax.dot_general(a.astype(BF16), b.astype(BF16), (((1,), (1,)), ((), ())),
                           preferred_element_type=F32)


def _mm_at(a, b):
    return lax.dot_general(a.astype(BF16), b.astype(BF16), (((0,), (0,)), ((), ())),
                           preferred_element_type=F32)


def _mm_f32(a, b):
    return jnp.dot(a, b, preferred_element_type=F32, precision=lax.Precision.HIGHEST)


def _silu(x):
    return x * jax.nn.sigmoid(x)


def _softplus(x):
    return jnp.maximum(x, 0.0) + jnp.log1p(jnp.exp(-jnp.abs(x)))


def _layer_norm(y, g, b):
    mu = jnp.mean(y, axis=-1, keepdims=True)
    yc = y - mu
    var = jnp.mean(yc * yc, axis=-1, keepdims=True)
    return yc * lax.rsqrt(var + LN_EPS) * g + b


def _post(x, delta, gate, g, b, alpha):
    return _layer_norm(alpha * x + gate * delta, g, b)


def _rowsum(x):
    return jnp.sum(x, axis=-1, keepdims=True)


def _l2norm(x):
    return x * lax.rsqrt(_rowsum(x * x) + RMS_EPS)


def _const_spec(shape):
    nd = len(shape)
    return pl.BlockSpec(shape, lambda *_: (0,) * nd, pipeline_mode=pl.Buffered(1))


def _adaln_kernel(cp_ref, cs_ref, w_ref, b_ref, op_ref, os_ref):
    mp = cp_ref.shape[0]
    c = jnp.concatenate([cp_ref[...], cs_ref[...]], axis=0)
    mod = _mm(_silu(c), w_ref[...]) + b_ref[...]
    op_ref[...] = mod[:mp]
    os_ref[...] = mod[mp:]


def _adaln(c_p, c_s, w, b, tn):
    (mp, d), ms = c_p.shape, c_s.shape[0]
    n = w.shape[1]
    assert n % tn == 0
    return pl.pallas_call(
        _adaln_kernel,
        grid=(n // tn,),
        in_specs=[_const_spec((mp, d)), _const_spec((ms, d)),
                  pl.BlockSpec((d, tn), lambda j: (0, j)),
                  pl.BlockSpec((1, tn), lambda j: (0, j))],
        out_specs=[pl.BlockSpec((mp, tn), lambda j: (0, j)), pl.BlockSpec((ms, tn), lambda j: (0, j))],
        out_shape=[jax.ShapeDtypeStruct((mp, n), F32), jax.ShapeDtypeStruct((ms, n), F32)],
        compiler_params=pltpu.CompilerParams(dimension_semantics=("arbitrary",),
                                             vmem_limit_bytes=V7X_VMEM_LIMIT),
        name="adaln",
    )(c_p, c_s, w, b)


def _ffn_rows(x, shift, scale, gate, wg_ref, wu_ref, wd_ref, g, b, alpha):
    u = (x * (1.0 + scale) + shift).astype(BF16)
    hg = jnp.dot(u, wg_ref[...], preferred_element_type=F32)
    hu = jnp.dot(u, wu_ref[...], preferred_element_type=F32)
    h = (_silu(hg) * hu).astype(BF16)
    d = jnp.dot(h, wd_ref[...], preferred_element_type=F32)
    return _post(x, 0.5 * d, gate, g, b, alpha)


def _ffn_prompt_kernel(x_ref, mod_ref, wg_ref, wu_ref, wd_ref, lng_ref, lnb_ref, o_ref, *, sub, alpha, parts):
    shift = mod_ref[0, 3 * sub + 0:3 * sub + 1, :]
    scale = mod_ref[0, 3 * sub + 1:3 * sub + 2, :]
    gate = mod_ref[0, 3 * sub + 2:3 * sub + 3, :]
    rows = x_ref.shape[0] // parts
    for part in range(parts):
        sl = slice(part * rows, (part + 1) * rows)
        o_ref[sl, :] = _ffn_rows(x_ref[sl, :], shift, scale, gate, wg_ref, wu_ref, wd_ref,
                                 lng_ref[sub:sub + 1, :], lnb_ref[sub:sub + 1, :], alpha)


def _ffn_prompt(x, mod, wg, wu, wd, lng, lnb, *, sub, alpha, tm, parts):
    bsz, t, d = x.shape
    f = wg.shape[1]
    assert t % tm == 0 and tm % (parts * SUBLANES) == 0
    per_seq = t // tm
    x2 = x.reshape(bsz * t, d)
    out = pl.pallas_call(
        functools.partial(_ffn_prompt_kernel, sub=sub, alpha=alpha, parts=parts),
        grid=(bsz * per_seq,),
        in_specs=[pl.BlockSpec((tm, d), lambda i: (i, 0)),
                  pl.BlockSpec((1, 3 * N_SUB, d), lambda i: (i // per_seq, 0, 0)),
                  _const_spec((d, f)), _const_spec((d, f)), _const_spec((f, d)),
                  _const_spec(lng.shape), _const_spec(lnb.shape)],
        out_specs=pl.BlockSpec((tm, d), lambda i: (i, 0)),
        out_shape=jax.ShapeDtypeStruct((bsz * t, d), F32),
        compiler_params=pltpu.CompilerParams(dimension_semantics=("arbitrary",),
                                             vmem_limit_bytes=V7X_VMEM_LIMIT),
        name=f"ffn_prompt{sub}",
    )(x2, mod, wg, wu, wd, lng, lnb)
    return out.reshape(bsz, t, d)


def _stack_time(ref, t_len, d):
    return jnp.concatenate([ref[:, t * d:(t + 1) * d] for t in range(t_len)], axis=0)


def _mod_rows(mod_ref, idx, t_len, d):
    m = mod_ref[:, idx * d:(idx + 1) * d]
    return jnp.concatenate([m] * t_len, axis=0)


def _ffn_sample_kernel(x_ref, mod_ref, wg_ref, wu_ref, wd_ref, lng_ref, lnb_ref, o_ref, *,
                       sub, alpha, t_len, d):
    bsz = x_ref.shape[0]
    x = _stack_time(x_ref, t_len, d)
    y = _ffn_rows(x, _mod_rows(mod_ref, 0, t_len, d), _mod_rows(mod_ref, 1, t_len, d),
                  _mod_rows(mod_ref, 2, t_len, d), wg_ref, wu_ref, wd_ref,
                  lng_ref[sub:sub + 1, :], lnb_ref[sub:sub + 1, :], alpha)
    for t in range(t_len):
        o_ref[:, t * d:(t + 1) * d] = y[t * bsz:(t + 1) * bsz]


def _sub_mod_spec(bsz, d, sub):
    return pl.BlockSpec((bsz, 3 * d), lambda *_: (0, sub), pipeline_mode=pl.Buffered(1))


def _cast_weights(jobs):
    chunks = []
    used = {}
    for w_hbm, w_scr, stage, sem in jobs:
        nbuf, rows = stage.shape[0], stage.shape[1]
        for i in range(w_hbm.shape[0] // rows):
            slot = used.get(id(stage), 0) % nbuf
            used[id(stage)] = used.get(id(stage), 0) + 1
            copy = pltpu.make_async_copy(w_hbm.at[pl.ds(i * rows, rows), :], stage.at[slot], sem.at[slot])
            chunks.append((copy, w_scr, stage, slot, i * rows, rows))
    ahead = min(job[2].shape[0] for job in jobs) - 1
    for copy, *_ in chunks[:ahead]:
        copy.start()
    for j, (copy, w_scr, stage, slot, row0, rows) in enumerate(chunks):
        if j + ahead < len(chunks):
            chunks[j + ahead][0].start()
        copy.wait()
        w_scr[row0:row0 + rows, :] = stage[slot].astype(BF16)


def _ffn_kernel(xp_ref, modp_ref, xs_ref, mods_ref, wg_hbm, wu_hbm, wd_hbm, lng_ref, lnb_ref,
                yp_ref, ys_ref, wg_scr, wu_scr, wd_scr, stage_in, stage_out, sem_in, sem_out,
                *, sub, alpha, n_prompt, t_len, d):
    k = pl.program_id(0)
    g, b = lng_ref[sub:sub + 1, :], lnb_ref[sub:sub + 1, :]

    @pl.when(k == 0)
    def _():
        _cast_weights([(wg_hbm, wg_scr, stage_in, sem_in), (wu_hbm, wu_scr, stage_in, sem_in),
                       (wd_hbm, wd_scr, stage_out, sem_out)])

    @pl.when(k < n_prompt)
    def _():
        shift = modp_ref[0, 3 * sub + 0:3 * sub + 1, :]
        scale = modp_ref[0, 3 * sub + 1:3 * sub + 2, :]
        gate = modp_ref[0, 3 * sub + 2:3 * sub + 3, :]
        yp_ref[...] = _ffn_rows(xp_ref[...], shift, scale, gate, wg_scr, wu_scr, wd_scr, g, b, alpha)

    @pl.when(k == n_prompt)
    def _():
        bsz = xs_ref.shape[0]
        x = _stack_time(xs_ref, t_len, d)
        y = _ffn_rows(x, _mod_rows(mods_ref, 0, t_len, d), _mod_rows(mods_ref, 1, t_len, d),
                      _mod_rows(mods_ref, 2, t_len, d), wg_scr, wu_scr, wd_scr, g, b, alpha)
        for t in range(t_len):
            ys_ref[:, t * d:(t + 1) * d] = y[t * bsz:(t + 1) * bsz]


def _ffn(xp, xs, mod_p, mod_s, wg, wu, wd, lng, lnb, *, sub, alpha, tm):
    bp, tp, d = xp.shape
    bs, ts, _ = xs.shape
    f = wg.shape[1]
    assert tp % tm == 0
    per_seq = tp // tm
    n_prompt = bp * per_seq
    in_rows = max(r for r in (16, 32, 64, 128) if d % r == 0)
    out_rows = max(r for r in (16, 32, 64, 128, 176, 352) if f % r == 0)
    last = n_prompt - 1
    yp, ys = pl.pallas_call(
        functools.partial(_ffn_kernel, sub=sub, alpha=alpha, n_prompt=n_prompt, t_len=ts, d=d),
        grid=(n_prompt + 1,),
        in_specs=[pl.BlockSpec((tm, d), lambda k: (jnp.minimum(k, last), 0)),
                  pl.BlockSpec((1, 3 * N_SUB, d), lambda k: (jnp.minimum(k, last) // per_seq, 0, 0)),
                  _const_spec((bs, ts * d)), _sub_mod_spec(bs, d, sub),
                  pl.BlockSpec(memory_space=pl.ANY), pl.BlockSpec(memory_space=pl.ANY),
                  pl.BlockSpec(memory_space=pl.ANY),
                  _const_spec(lng.shape), _const_spec(lnb.shape)],
        out_specs=[pl.BlockSpec((tm, d), lambda k: (jnp.minimum(k, last), 0)),
                   pl.BlockSpec((bs, ts * d), lambda k: (0, 0))],
        out_shape=[jax.ShapeDtypeStruct((bp * tp, d), F32),
                   jax.ShapeDtypeStruct((bs, ts * d), F32)],
        scratch_shapes=[pltpu.VMEM((d, f), BF16), pltpu.VMEM((d, f), BF16), pltpu.VMEM((f, d), BF16),
                        pltpu.VMEM((CAST_BUFFERS, in_rows, f), F32), pltpu.VMEM((CAST_BUFFERS, out_rows, d), F32),
                        pltpu.SemaphoreType.DMA((CAST_BUFFERS,)), pltpu.SemaphoreType.DMA((CAST_BUFFERS,))],
        compiler_params=pltpu.CompilerParams(dimension_semantics=("arbitrary",),
                                             vmem_limit_bytes=V7X_VMEM_LIMIT),
        name=f"ffn{sub}",
    )(xp.reshape(bp * tp, d), mod_p, xs.reshape(bs, ts * d), mod_s, wg, wu, wd, lng, lnb)
    return yp.reshape(bp, tp, d), ys.reshape(bs, ts, d)


def _ffn_sample(x, mod, wg, wu, wd, lng, lnb, *, sub, alpha):
    bsz, t, d = x.shape
    f = wg.shape[1]
    x2 = x.reshape(bsz, t * d)
    out = pl.pallas_call(
        functools.partial(_ffn_sample_kernel, sub=sub, alpha=alpha, t_len=t, d=d),
        grid=(1,),
        in_specs=[_const_spec((bsz, t * d)), _sub_mod_spec(bsz, d, sub),
                  _const_spec((d, f)), _const_spec((d, f)), _const_spec((f, d)),
                  _const_spec(lng.shape), _const_spec(lnb.shape)],
        out_specs=pl.BlockSpec((bsz, t * d), lambda i: (0, 0)),
        out_shape=jax.ShapeDtypeStruct((bsz, t * d), F32),
        compiler_params=pltpu.CompilerParams(dimension_semantics=("arbitrary",),
                                             vmem_limit_bytes=V7X_VMEM_LIMIT),
        name=f"ffn_sample{sub}",
    )(x2, mod, wg, wu, wd, lng, lnb)
    return out.reshape(bsz, t, d)


def _tri_inverse(lmats, c):
    ls = [l.astype(BF16) for l in lmats]
    ms = [jnp.dot(l, l, preferred_element_type=F32) for l in ls]
    qs = ms
    power = 2
    while 2 * power < c:
        mb = [m.astype(BF16) for m in ms]
        ms = [jnp.dot(m, m, preferred_element_type=F32) for m in mb]
        qm = [jnp.dot(q.astype(BF16), m.astype(BF16), preferred_element_type=F32) for q, m in zip(qs, ms)]
        qs = [q + m + x for q, m, x in zip(qs, ms, qm)]
        power *= 2
    rows = lax.broadcasted_iota(jnp.int32, (c, c), 0)
    cols = lax.broadcasted_iota(jnp.int32, (c, c), 1)
    eye = (rows == cols).astype(F32)
    lq = [jnp.dot(l, q.astype(BF16), preferred_element_type=F32) for l, q in zip(ls, qs)]
    return [eye - l + q - x for l, q, x in zip(lmats, qs, lq)]


def _gdn_chunks_local(units, tril, strict):
    c = units[0][1].shape[0]
    a = [lax.dot_general(jnp.concatenate([kb, q], axis=0), k, (((1,), (1,)), ((), ())),
                         preferred_element_type=F32)
         for kb, q, k, _, _, _, _ in units]
    lmats, qks = [], []
    for a_u, (_, _, _, _, _, gcol, grow) in zip(a, units):
        decay = jnp.where(tril, jnp.exp(jnp.where(tril, gcol - grow, 0.0)), 0.0)
        lmats.append(jnp.where(strict, a_u[:c] * decay, 0.0))
        qks.append(jnp.where(tril, a_u[c:] * decay, 0.0))
    tms = _tri_inverse(lmats, c)
    uw = [jnp.dot(tm.astype(BF16), jnp.concatenate([vb, kbe], axis=1), preferred_element_type=F32)
          for tm, (_, _, _, vb, kbe, _, _) in zip(tms, units)]
    return [(x[:, :DN_DV], x[:, DN_DV:], qk) for x, qk in zip(uw, qks)]


def _gdn_chunk_state(units, states):
    c = units[0][2].shape[0]
    r = [jnp.dot(jnp.concatenate([w, q_dec], axis=0), s.astype(BF16), preferred_element_type=F32)
         for (w, q_dec, _, _, _, _), s in zip(units, states)]
    v_new = [(u - r_h[:c]).astype(BF16) for (_, _, u, _, _, _), r_h in zip(units, r)]
    o = [r_h[c:] + jnp.dot(qk, v, preferred_element_type=F32)
         for (_, _, _, qk, _, _), r_h, v in zip(units, r, v_new)]
    s_new = [s * g_last + lax.dot_general(k_dec, v, (((0,), (0,)), ((), ())), preferred_element_type=F32)
             for (_, _, _, _, k_dec, g_last), s, v in zip(units, states, v_new)]
    return o, s_new


def _tri_inverse_steps(lmats, c):
    ls = [l.astype(BF16) for l in lmats]
    ms = [jnp.dot(l, l, preferred_element_type=F32) for l in ls]
    yield
    qs = ms
    power = 2
    while 2 * power < c:
        mb = [m.astype(BF16) for m in ms]
        ms = [jnp.dot(m, m, preferred_element_type=F32) for m in mb]
        yield
        qm = [jnp.dot(q.astype(BF16), m.astype(BF16), preferred_element_type=F32) for q, m in zip(qs, ms)]
        yield
        qs = [q + m + x for q, m, x in zip(qs, ms, qm)]
        power *= 2
    rows = lax.broadcasted_iota(jnp.int32, (c, c), 0)
    cols = lax.broadcasted_iota(jnp.int32, (c, c), 1)
    eye = (rows == cols).astype(F32)
    lq = [jnp.dot(l, q.astype(BF16), preferred_element_type=F32) for l, q in zip(ls, qs)]
    yield
    return [eye - l + q - x for l, q, x in zip(lmats, qs, lq)]


def _gdn_local_steps(units, tril, strict):
    c = units[0][1].shape[0]
    a = [lax.dot_general(jnp.concatenate([kb, q], axis=0), k, (((1,), (1,)), ((), ())),
                         preferred_element_type=F32)
         for kb, q, k, _, _, _, _ in units]
    yield
    lmats, qks = [], []
    for a_u, (_, _, _, _, _, gcol, grow) in zip(a, units):
        decay = jnp.where(tril, jnp.exp(jnp.where(tril, gcol - grow, 0.0)), 0.0)
        lmats.append(jnp.where(strict, a_u[:c] * decay, 0.0))
        qks.append(jnp.where(tril, a_u[c:] * decay, 0.0))
    tms = yield from _tri_inverse_steps(lmats, c)
    uw = [jnp.dot(tm.astype(BF16), jnp.concatenate([vb, kbe], axis=1), preferred_element_type=F32)
          for tm, (_, _, _, vb, kbe, _, _) in zip(tms, units)]
    yield
    return [(x[:, :DN_DV], x[:, DN_DV:], qk) for x, qk in zip(uw, qks)]


def _run_interleaved(*gens):
    live = list(gens)
    while live:
        for g in list(live):
            try:
                next(g)
            except StopIteration:
                live.remove(g)


def _gate_norm(o, og, norm_g):
    o = o * lax.rsqrt(jnp.mean(o * o, axis=-1, keepdims=True) + RMS_EPS) * norm_g
    return o * _silu(og)


def _proj_layout(dn_width, sc_width):
    off = {}
    pos = 0
    for name, width in (("og", dn_width), ("sc_b", sc_width), ("sc_c", sc_width), ("sc_h", sc_width)):
        off[name] = (pos, pos + width)
        pos += width
    return off


def _project(u, wq_ref, wr_ref, wab_ref, off):
    p = {"qkv": jnp.dot(u, wq_ref[...], preferred_element_type=F32),
         "ab": jnp.dot(u, wab_ref[...], preferred_element_type=F32)}
    rest = jnp.dot(u, wr_ref[...], preferred_element_type=F32)
    for name, (lo, hi) in off.items():
        p[name] = rest[:, lo:hi]
    return p


def _cols(p, off, name):
    del off
    return p[name]


def _mix_out(x, o_dn, o_sc, gate, w_out_ref, g, b, alpha):
    dn_w = o_dn.shape[1]
    mix = (jnp.dot(o_dn.astype(BF16), w_out_ref[0:dn_w, :], preferred_element_type=F32)
           + jnp.dot(o_sc.astype(BF16), w_out_ref[dn_w:, :], preferred_element_type=F32))
    return _post(x, mix, gate, g, b, alpha)


def _mixer_prompt_kernel(x_ref, mod_ref, wq_ref, wr_ref, wab_ref, w_out_ref, cw_ref, mw_ref, alog_ref, dtb_ref, ng_ref,
                         lng_ref, lnb_ref,
                         y_ref, s_out_ref, cq_out_ref, cm_out_ref,
                         s_scr, cq_scr, cm_scr, gct_scr, gl_scr, ops_scr, u_scr, w_scr, qk_scr, o_scr,
                         *, alpha, nseq, tt, off, n_conv, n_mconv):
    t_idx = pl.program_id(1)
    rows_all = nseq * tt
    nchunk = rows_all // CHUNK
    per_seq = tt // CHUNK
    hk = DN_HEADS * DN_DK
    sub = 1

    @pl.when(t_idx == 0)
    def _():
        s_scr[...] = jnp.zeros_like(s_scr)
        cq_scr[...] = jnp.zeros_like(cq_scr)
        cm_scr[...] = jnp.zeros_like(cm_scr)

    xs = [x_ref[s] for s in range(nseq)]
    gates = [mod_ref[s, 3 * sub + 2:3 * sub + 3, :] for s in range(nseq)]
    u = jnp.concatenate(
        [(xs[s] * (1.0 + mod_ref[s, 3 * sub + 1:3 * sub + 2, :]) + mod_ref[s, 3 * sub + 0:3 * sub + 1, :])
         .astype(BF16) for s in range(nseq)], axis=0)
    p = _project(u, wq_ref, wr_ref, wab_ref, off)

    def causal_conv(new_rows, hist_scr, s, w_ref, taps):
        ext = jnp.concatenate([hist_scr[s], new_rows], axis=0)
        acc = new_rows * w_ref[taps - 1:taps, :]
        for j in range(taps - 1):
            acc = acc + pltpu.roll(ext, taps - 1 - j, axis=0)[SUBLANES:] * w_ref[j:j + 1, :]
        hist_scr[s] = new_rows[tt - SUBLANES:]
        return acc

    pq = _cols(p, off, "qkv")
    acts = []
    for s in range(nseq):
        pq_s = pq[s * tt:(s + 1) * tt]
        acts.append(_silu(causal_conv(pq_s, cq_scr, s, cw_ref, n_conv)))
        cq_out_ref[s] = pq_s[tt - (n_conv - 1):]
    act = jnp.concatenate(acts, axis=0)

    ab = _cols(p, off, "ab")
    beta_all = jax.nn.sigmoid(ab)
    g = -jnp.exp(alog_ref[...]) * _softplus(ab + dtb_ref[...])
    rows = lax.broadcasted_iota(jnp.int32, (CHUNK, CHUNK), 0)
    cols = lax.broadcasted_iota(jnp.int32, (CHUNK, CHUNK), 1)
    tril = rows >= cols
    strict = rows > cols
    tril_f = tril.astype(F32)
    gc_parts, gl_parts = [], []
    for c in range(nchunk):
        gc_c = _mm_f32(tril_f, g[c * CHUNK:(c + 1) * CHUNK, :])
        gc_parts.append(gc_c)
        gl_parts.append(jnp.broadcast_to(gc_c[CHUNK - 1:CHUNK, :], (CHUNK, AB_PAD)))
    gc = jnp.concatenate(gc_parts, axis=0)
    gl_rows = jnp.concatenate(gl_parts, axis=0)
    gc_t = gc.T
    for c in range(nchunk):
        gct_scr[c] = gc_t[0:SUBLANES, c * CHUNK:(c + 1) * CHUNK]

    eg_all = jnp.exp(gc)
    ekd_all = jnp.exp(gl_rows - gc)
    egl_all = jnp.exp(gl_rows)
    for h in range(DN_HEADS):
        lanes = slice(h * DN_DK, (h + 1) * DN_DK)
        q = _l2norm(act[:, h * DN_DK:(h + 1) * DN_DK]) * (DN_DK ** -0.5)
        k = _l2norm(act[:, hk + h * DN_DK:hk + (h + 1) * DN_DK])
        v = act[:, 2 * hk + h * DN_DV:2 * hk + (h + 1) * DN_DV]
        beta = beta_all[:, DN_HEADS + h:DN_HEADS + h + 1]
        eg = eg_all[:, h:h + 1]
        kb = k * beta
        ops_scr[0, :, lanes] = kb.astype(BF16)
        ops_scr[1, :, lanes] = q.astype(BF16)
        ops_scr[2, :, lanes] = k.astype(BF16)
        ops_scr[3, :, lanes] = (v * beta).astype(BF16)
        ops_scr[4, :, lanes] = (kb * eg).astype(BF16)
        ops_scr[5, :, lanes] = (q * eg).astype(BF16)
        ops_scr[6, :, lanes] = (k * ekd_all[:, h:h + 1]).astype(BF16)
        gl_scr[:, lanes] = jnp.broadcast_to(egl_all[:, h:h + 1], (rows_all, DN_DV))

    head_lanes = [slice(h * DN_DK, (h + 1) * DN_DK) for h in range(DN_HEADS)]

    where = [(c, h, lanes) for c in range(nchunk) for h, lanes in enumerate(head_lanes)]
    units = [tuple(ops_scr[i, c * CHUNK:(c + 1) * CHUNK, lanes] for i in range(5))
             + (gc[c * CHUNK:(c + 1) * CHUNK, h:h + 1], gct_scr[c, h:h + 1, :])
             for c, h, lanes in where]
    for (c, h, lanes), (u_c, w_c, qk_c) in zip(where, _gdn_chunks_local(units, tril, strict)):
        u_scr[c * CHUNK:(c + 1) * CHUNK, lanes] = u_c
        w_scr[c * CHUNK:(c + 1) * CHUNK, lanes] = w_c.astype(BF16)
        qk_scr[c * DN_HEADS + h] = qk_c.astype(BF16)

    for j in range(per_seq):
        chains = [(s, s * per_seq + j, h, lanes) for s in range(nseq) for h, lanes in enumerate(head_lanes)]
        units = [(w_scr[c * CHUNK:(c + 1) * CHUNK, lanes], ops_scr[5, c * CHUNK:(c + 1) * CHUNK, lanes],
                  u_scr[c * CHUNK:(c + 1) * CHUNK, lanes], qk_scr[c * DN_HEADS + h],
                  ops_scr[6, c * CHUNK:(c + 1) * CHUNK, lanes], gl_scr[c * CHUNK:c * CHUNK + 1, lanes])
                 for s, c, h, lanes in chains]
        o_c, s_new = _gdn_chunk_state(units, [s_scr[s * DN_HEADS + h] for s, _, h, _ in chains])
        for (s, c, h, lanes), o_u, s_u in zip(chains, o_c, s_new):
            s_scr[s * DN_HEADS + h] = s_u
            o_scr[c * CHUNK:(c + 1) * CHUNK, lanes] = o_u
    for s in range(nseq):
        s_out_ref[s] = s_scr[s * DN_HEADS:(s + 1) * DN_HEADS]

    og = _cols(p, off, "og")
    o_dn = jnp.concatenate(
        [_gate_norm(o_scr[:, h * DN_DV:(h + 1) * DN_DV], og[:, h * DN_DV:(h + 1) * DN_DV], ng_ref[...])
         for h in range(DN_HEADS)], axis=1)

    z = _cols(p, off, "sc_c") * _cols(p, off, "sc_h")
    zcs = []
    for s in range(nseq):
        z_s = z[s * tt:(s + 1) * tt]
        zcs.append(causal_conv(z_s, cm_scr, s, mw_ref, n_mconv))
        cm_out_ref[s] = z_s[tt - (n_mconv - 1):]
    o_sc = _cols(p, off, "sc_b") * jnp.concatenate(zcs, axis=0)

    dn_w = o_dn.shape[1]
    mix = (jnp.dot(o_dn.astype(BF16), w_out_ref[0:dn_w, :], preferred_element_type=F32)
           + jnp.dot(o_sc.astype(BF16), w_out_ref[dn_w:, :], preferred_element_type=F32))
    for s in range(nseq):
        y_ref[s] = _post(xs[s], mix[s * tt:(s + 1) * tt], gates[s],
                         lng_ref[sub:sub + 1, :], lnb_ref[sub:sub + 1, :], alpha)


def _mixer_prompt(x, mod, w_in, w_out, conv_w, mconv_w, alog, dtb, norm_g, lng, lnb, *, alpha, nseq, tt, off):
    bsz, t, d = x.shape
    assert t % tt == 0 and tt % LANES == 0 and bsz % nseq == 0
    nt = t // tt
    rows = nseq * tt
    qkv_dim = conv_w.shape[-1]
    sc_w = off["sc_c"][1] - off["sc_c"][0]
    dn_w = DN_HEADS * DN_DV
    n_conv, n_mconv = conv_w.shape[0], mconv_w.shape[0]
    assert max(n_conv, n_mconv) - 1 <= SUBLANES <= tt
    nchunk = rows // CHUNK
    kern = functools.partial(_mixer_prompt_kernel, alpha=alpha, nseq=nseq, tt=tt, off=off,
                             n_conv=n_conv, n_mconv=n_mconv)
    y, s_out, cq, cm = pl.pallas_call(
        kern,
        grid=(bsz // nseq, nt),
        in_specs=[pl.BlockSpec((nseq, tt, d), lambda b, i: (b, i, 0)),
                  pl.BlockSpec((nseq, 3 * N_SUB, d), lambda b, i: (b, 0, 0)),
                  *[_const_spec(w.shape) for w in w_in], _const_spec(w_out.shape),
                  _const_spec(conv_w.shape), _const_spec(mconv_w.shape),
                  _const_spec(alog.shape), _const_spec(dtb.shape), _const_spec(norm_g.shape),
                  _const_spec(lng.shape), _const_spec(lnb.shape)],
        out_specs=[pl.BlockSpec((nseq, tt, d), lambda b, i: (b, i, 0)),
                   pl.BlockSpec((nseq, DN_HEADS, DN_DK, DN_DV), lambda b, i: (b, 0, 0, 0)),
                   pl.BlockSpec((nseq, n_conv - 1, qkv_dim), lambda b, i: (b, 0, 0)),
                   pl.BlockSpec((nseq, n_mconv - 1, sc_w), lambda b, i: (b, 0, 0))],
        out_shape=[jax.ShapeDtypeStruct((bsz, t, d), F32),
                   jax.ShapeDtypeStruct((bsz, DN_HEADS, DN_DK, DN_DV), F32),
                   jax.ShapeDtypeStruct((bsz, n_conv - 1, qkv_dim), F32),
                   jax.ShapeDtypeStruct((bsz, n_mconv - 1, sc_w), F32)],
        scratch_shapes=[pltpu.VMEM((nseq * DN_HEADS, DN_DK, DN_DV), F32),
                        pltpu.VMEM((nseq, SUBLANES, qkv_dim), F32),
                        pltpu.VMEM((nseq, SUBLANES, sc_w), F32),
                        pltpu.VMEM((nchunk, SUBLANES, CHUNK), F32),
                        pltpu.VMEM((rows, dn_w), F32),
                        pltpu.VMEM((7, rows, dn_w), BF16),
                        pltpu.VMEM((rows, dn_w), F32),
                        pltpu.VMEM((rows, dn_w), BF16),
                        pltpu.VMEM((nchunk * DN_HEADS, CHUNK, CHUNK), BF16),
                        pltpu.VMEM((rows, dn_w), F32)],
        compiler_params=pltpu.CompilerParams(dimension_semantics=("arbitrary", "arbitrary"),
                                             vmem_limit_bytes=V7X_VMEM_LIMIT),
        name="mixer_prompt",
    )(x, mod, *w_in, w_out, conv_w, mconv_w, alog, dtb, norm_g, lng, lnb)
    return y, s_out, cq, cm


def _mixer_prompt_pipe_kernel(xf_ref, modf_ref, xb_ref, modb_ref, wq_ref, wr_ref, wab_ref, w_out_ref, cw_ref,
                              mw_ref, alog_ref, dtb_ref, ng_ref, lng_ref, lnb_ref,
                              y_ref, s_out_ref, cq_out_ref, cm_out_ref,
                              s_scr, cq_scr, cm_scr, gct_scr,
                              u_scr, w_scr, qk_scr, qd_scr, kd_scr, gl_scr, og_scr, osc_scr,
                              *, alpha, nseq, tt, tiles_per_group, n_tiles, off, n_conv, n_mconv):
    k = pl.program_id(0)
    rows_all = nseq * tt
    nchunk = rows_all // CHUNK
    per_seq = tt // CHUNK
    hk = DN_HEADS * DN_DK
    sub = 1
    head_lanes = [slice(h * DN_DK, (h + 1) * DN_DK) for h in range(DN_HEADS)]
    t_front = lax.rem(jnp.minimum(k, n_tiles - 1), tiles_per_group)
    t_back = lax.rem(jnp.maximum(k - 1, 0), tiles_per_group)
    keep_front = jnp.where(t_front == 0, 0.0, 1.0).astype(F32)
    keep_back = jnp.where(t_back == 0, 0.0, 1.0).astype(F32)

    def causal_conv(new_rows, hist_scr, s, w_ref, taps):
        ext = jnp.concatenate([hist_scr[s] * keep_front, new_rows], axis=0)
        acc = new_rows * w_ref[taps - 1:taps, :]
        for j in range(taps - 1):
            acc = acc + pltpu.roll(ext, taps - 1 - j, axis=0)[SUBLANES:] * w_ref[j:j + 1, :]
        hist_scr[s] = new_rows[tt - SUBLANES:]
        return acc

    def front():
        u = jnp.concatenate(
            [(xf_ref[s] * (1.0 + modf_ref[s, 3 * sub + 1:3 * sub + 2, :])
              + modf_ref[s, 3 * sub + 0:3 * sub + 1, :]).astype(BF16) for s in range(nseq)], axis=0)
        pq = jnp.dot(u, wq_ref[...], preferred_element_type=F32)
        yield
        rest = jnp.dot(u, wr_ref[...], preferred_element_type=F32)
        yield
        ab = jnp.dot(u, wab_ref[...], preferred_element_type=F32)
        p = {name: rest[:, lo:hi] for name, (lo, hi) in off.items()}

        beta_all = jax.nn.sigmoid(ab)
        g = -jnp.exp(alog_ref[...]) * _softplus(ab + dtb_ref[...])
        rows = lax.broadcasted_iota(jnp.int32, (CHUNK, CHUNK), 0)
        cols = lax.broadcasted_iota(jnp.int32, (CHUNK, CHUNK), 1)
        tril = rows >= cols
        strict = rows > cols
        tril_f = tril.astype(F32)
        gc_parts, gl_parts = [], []
        for c in range(nchunk):
            gc_c = _mm_f32(tril_f, g[c * CHUNK:(c + 1) * CHUNK, :])
            gc_parts.append(gc_c)
            gl_parts.append(jnp.broadcast_to(gc_c[CHUNK - 1:CHUNK, :], (CHUNK, AB_PAD)))
        yield
        gc = jnp.concatenate(gc_parts, axis=0)
        gl_rows = jnp.concatenate(gl_parts, axis=0)
        gc_t = gc.T
        for c in range(nchunk):
            gct_scr[c] = gc_t[0:SUBLANES, c * CHUNK:(c + 1) * CHUNK]

        acts = []
        for s in range(nseq):
            pq_s = pq[s * tt:(s + 1) * tt]
            acts.append(_silu(causal_conv(pq_s, cq_scr, s, cw_ref, n_conv)))
            cq_out_ref[s] = pq_s[tt - (n_conv - 1):]
        act = jnp.concatenate(acts, axis=0)

        eg_all = jnp.exp(gc)
        ekd_all = jnp.exp(gl_rows - gc)
        egl_all = jnp.exp(gl_rows)
        per_head = []
        for h, lanes in enumerate(head_lanes):
            q = _l2norm(act[:, h * DN_DK:(h + 1) * DN_DK]) * (DN_DK ** -0.5)
            kk = _l2norm(act[:, hk + h * DN_DK:hk + (h + 1) * DN_DK])
            v = act[:, 2 * hk + h * DN_DV:2 * hk + (h + 1) * DN_DV]
            beta = beta_all[:, DN_HEADS + h:DN_HEADS + h + 1]
            eg = eg_all[:, h:h + 1]
            kb = kk * beta
            per_head.append((kb.astype(BF16), q.astype(BF16), kk.astype(BF16), (v * beta).astype(BF16),
                             (kb * eg).astype(BF16)))
            qd_scr[:, lanes] = (q * eg).astype(BF16)
            kd_scr[:, lanes] = (kk * ekd_all[:, h:h + 1]).astype(BF16)
            for c in range(nchunk):
                gl_scr[c * SUBLANES:(c + 1) * SUBLANES, lanes] = jnp.broadcast_to(
                    egl_all[c * CHUNK:c * CHUNK + SUBLANES, h:h + 1], (SUBLANES, DN_DV))

        z = p["sc_c"] * p["sc_h"]
        zcs = []
        for s in range(nseq):
            z_s = z[s * tt:(s + 1) * tt]
            zcs.append(causal_conv(z_s, cm_scr, s, mw_ref, n_mconv))
            cm_out_ref[s] = z_s[tt - (n_mconv - 1):]
        osc_scr[...] = p["sc_b"] * jnp.concatenate(zcs, axis=0)
        og_scr[...] = p["og"]

        where = [(c, h, lanes) for c in range(nchunk) for h, lanes in enumerate(head_lanes)]
        units = [tuple(arr[c * CHUNK:(c + 1) * CHUNK] for arr in per_head[h])
                 + (gc[c * CHUNK:(c + 1) * CHUNK, h:h + 1], gct_scr[c, h:h + 1, :])
                 for c, h, lanes in where]
        local = yield from _gdn_local_steps(units, tril, strict)
        for (c, h, lanes), (u_c, w_c, qk_c) in zip(where, local):
            u_scr[c * CHUNK:(c + 1) * CHUNK, lanes] = u_c
            w_scr[c * CHUNK:(c + 1) * CHUNK, lanes] = w_c.astype(BF16)
            qk_scr[c * DN_HEADS + h] = qk_c.astype(BF16)

    def back():
        u_all, w_all, qd_all, kd_all = u_scr[...], w_scr[...], qd_scr[...], kd_scr[...]
        gl_all, og, o_sc = gl_scr[...], og_scr[...], osc_scr[...]
        qks = [qk_scr[i] for i in range(nchunk * DN_HEADS)]
        xs = [xb_ref[s] for s in range(nseq)]
        gates = [modb_ref[s, 3 * sub + 2:3 * sub + 3, :] for s in range(nseq)]
        states = [s_scr[i] * keep_back for i in range(nseq * DN_HEADS)]
        o_rows = [[None] * DN_HEADS for _ in range(nchunk)]
        for j in range(per_seq):
            chains = [(s, s * per_seq + j, h, lanes) for s in range(nseq) for h, lanes in enumerate(head_lanes)]
            rws = [slice(c * CHUNK, (c + 1) * CHUNK) for _, c, _, _ in chains]
            r = [jnp.dot(jnp.concatenate([w_all[rw, lanes], qd_all[rw, lanes]], axis=0),
                         states[s * DN_HEADS + h].astype(BF16), preferred_element_type=F32)
                 for (s, c, h, lanes), rw in zip(chains, rws)]
            yield
            v_new = [(u_all[rw, lanes] - r_u[:CHUNK]).astype(BF16)
                     for (s, c, h, lanes), rw, r_u in zip(chains, rws, r)]
            for (s, c, h, lanes), r_u, v_u in zip(chains, r, v_new):
                o_rows[c][h] = r_u[CHUNK:] + jnp.dot(qks[c * DN_HEADS + h], v_u, preferred_element_type=F32)
            for (s, c, h, lanes), rw, v_u in zip(chains, rws, v_new):
                i = s * DN_HEADS + h
                states[i] = (states[i] * gl_all[c * SUBLANES:c * SUBLANES + 1, lanes]
                             + lax.dot_general(kd_all[rw, lanes], v_u, (((0,), (0,)), ((), ())),
                                               preferred_element_type=F32))
            yield
        for i, st in enumerate(states):
            s_scr[i] = st
        for s in range(nseq):
            for h in range(DN_HEADS):
                s_out_ref[s, h] = states[s * DN_HEADS + h]

        o_dn = jnp.concatenate(
            [_gate_norm(jnp.concatenate([o_rows[c][h] for c in range(nchunk)], axis=0),
                        og[:, lanes], ng_ref[...]) for h, lanes in enumerate(head_lanes)], axis=1)
        dn_w = o_dn.shape[1]
        mix = (jnp.dot(o_dn.astype(BF16), w_out_ref[0:dn_w, :], preferred_element_type=F32)
               + jnp.dot(o_sc.astype(BF16), w_out_ref[dn_w:, :], preferred_element_type=F32))
        yield
        for s in range(nseq):
            y_ref[s] = _post(xs[s], mix[s * tt:(s + 1) * tt], gates[s],
                             lng_ref[sub:sub + 1, :], lnb_ref[sub:sub + 1, :], alpha)

    @pl.when(k == 0)
    def _():
        s_scr[...] = jnp.zeros_like(s_scr)
        cq_scr[...] = jnp.zeros_like(cq_scr)
        cm_scr[...] = jnp.zeros_like(cm_scr)
        _run_interleaved(front())

    @pl.when(jnp.logical_and(k > 0, k < n_tiles))
    def _():
        _run_interleaved(back(), front())

    @pl.when(k == n_tiles)
    def _():
        _run_interleaved(back())


def _mixer_prompt_pipe(x, mod, w_in, w_out, conv_w, mconv_w, alog, dtb, norm_g, lng, lnb, *, alpha, nseq, tt, off):
    bsz, t, d = x.shape
    assert t % tt == 0 and tt % LANES == 0 and bsz % nseq == 0
    nt = t // tt
    n_tiles = (bsz // nseq) * nt
    rows = nseq * tt
    qkv_dim = conv_w.shape[-1]
    sc_w = off["sc_c"][1] - off["sc_c"][0]
    dn_w = DN_HEADS * DN_DV
    n_conv, n_mconv = conv_w.shape[0], mconv_w.shape[0]
    assert max(n_conv, n_mconv) - 1 <= SUBLANES <= tt
    nchunk = rows // CHUNK
    kern = functools.partial(_mixer_prompt_pipe_kernel, alpha=alpha, nseq=nseq, tt=tt, tiles_per_group=nt,
                             n_tiles=n_tiles, off=off, n_conv=n_conv, n_mconv=n_mconv)

    def front_tile(k):
        return jnp.minimum(k, n_tiles - 1)

    def back_tile(k):
        return jnp.maximum(k - 1, 0)

    y, s_out, cq, cm = pl.pallas_call(
        kern,
        grid=(n_tiles + 1,),
        in_specs=[pl.BlockSpec((nseq, tt, d), lambda k: (front_tile(k) // nt, front_tile(k) % nt, 0)),
                  pl.BlockSpec((nseq, 3 * N_SUB, d), lambda k: (front_tile(k) // nt, 0, 0)),
                  pl.BlockSpec((nseq, tt, d), lambda k: (back_tile(k) // nt, back_tile(k) % nt, 0)),
                  pl.BlockSpec((nseq, 3 * N_SUB, d), lambda k: (back_tile(k) // nt, 0, 0)),
                  *[_const_spec(w.shape) for w in w_in], _const_spec(w_out.shape),
                  _const_spec(conv_w.shape), _const_spec(mconv_w.shape),
                  _const_spec(alog.shape), _const_spec(dtb.shape), _const_spec(norm_g.shape),
                  _const_spec(lng.shape), _const_spec(lnb.shape)],
        out_specs=[pl.BlockSpec((nseq, tt, d), lambda k: (back_tile(k) // nt, back_tile(k) % nt, 0)),
                   pl.BlockSpec((nseq, DN_HEADS, DN_DK, DN_DV), lambda k: (back_tile(k) // nt, 0, 0, 0)),
                   pl.BlockSpec((nseq, n_conv - 1, qkv_dim), lambda k: (front_tile(k) // nt, 0, 0)),
                   pl.BlockSpec((nseq, n_mconv - 1, sc_w), lambda k: (front_tile(k) // nt, 0, 0))],
        out_shape=[jax.ShapeDtypeStruct((bsz, t, d), F32),
                   jax.ShapeDtypeStruct((bsz, DN_HEADS, DN_DK, DN_DV), F32),
                   jax.ShapeDtypeStruct((bsz, n_conv - 1, qkv_dim), F32),
                   jax.ShapeDtypeStruct((bsz, n_mconv - 1, sc_w), F32)],
        scratch_shapes=[pltpu.VMEM((nseq * DN_HEADS, DN_DK, DN_DV), F32),
                        pltpu.VMEM((nseq, SUBLANES, qkv_dim), F32),
                        pltpu.VMEM((nseq, SUBLANES, sc_w), F32),
                        pltpu.VMEM((nchunk, SUBLANES, CHUNK), F32),
                        pltpu.VMEM((rows, dn_w), F32),
                        pltpu.VMEM((rows, dn_w), BF16),
                        pltpu.VMEM((nchunk * DN_HEADS, CHUNK, CHUNK), BF16),
                        pltpu.VMEM((rows, dn_w), BF16),
                        pltpu.VMEM((rows, dn_w), BF16),
                        pltpu.VMEM((nchunk * SUBLANES, dn_w), F32),
                        pltpu.VMEM((rows, dn_w), F32),
                        pltpu.VMEM((rows, sc_w), F32)],
        compiler_params=pltpu.CompilerParams(dimension_semantics=("arbitrary",),
                                             vmem_limit_bytes=V7X_VMEM_LIMIT),
        name="mixer_prompt",
    )(x, mod, x, mod, *w_in, w_out, conv_w, mconv_w, alog, dtb, norm_g, lng, lnb)
    return y, s_out, cq, cm


_PACK = SUBLANES


def _sample_phase_a(x_ref, mod_ref, wq_ref, wr_ref, wab_ref, cw_ref, mw_ref, alog_ref, dtb_ref, cq_in_ref, cm_in_ref,
                    cq_out_ref, cm_out_ref, og_scr, osc_scr, lhs_scr, u_scr, kd_scr, gl_scr, qk_scr,
                    *, t_len, d, off, n_conv, n_mconv):
    bsz = x_ref.shape[0]
    hk = DN_HEADS * DN_DK
    qkv_dim = cw_ref.shape[-1]
    sc_w = off["sc_c"][1] - off["sc_c"][0]

    x = _stack_time(x_ref, t_len, d)
    u = (x * (1.0 + _mod_rows(mod_ref, 1, t_len, d)) + _mod_rows(mod_ref, 0, t_len, d)).astype(BF16)
    p = _project(u, wq_ref, wr_ref, wab_ref, off)

    def rows(arr, t):
        return arr[t * bsz:(t + 1) * bsz]

    pq = _cols(p, off, "qkv")
    seq = ([cq_in_ref[:, j * qkv_dim:(j + 1) * qkv_dim] for j in range(n_conv - 1)]
           + [rows(pq, t) for t in range(t_len)])
    act = []
    for t in range(t_len):
        acc = None
        for j in range(n_conv):
            term = seq[t + j] * cw_ref[j:j + 1, :]
            acc = term if acc is None else acc + term
        act.append(_silu(acc))
    for j in range(n_conv - 1):
        cq_out_ref[:, j * qkv_dim:(j + 1) * qkv_dim] = seq[t_len + j]

    z = _cols(p, off, "sc_c") * _cols(p, off, "sc_h")
    zseq = ([cm_in_ref[:, j * sc_w:(j + 1) * sc_w] for j in range(n_mconv - 1)]
            + [rows(z, t) for t in range(t_len)])
    zc = []
    for t in range(t_len):
        acc = None
        for j in range(n_mconv):
            term = zseq[t + j] * mw_ref[j:j + 1, :]
            acc = term if acc is None else acc + term
        zc.append(acc)
    for j in range(n_mconv - 1):
        cm_out_ref[:, j * sc_w:(j + 1) * sc_w] = zseq[t_len + j]
    osc_scr[...] = _cols(p, off, "sc_b") * jnp.concatenate(zc, axis=0)
    og_scr[...] = _cols(p, off, "og")

    ab = _cols(p, off, "ab")
    beta_all = jax.nn.sigmoid(ab)
    g_all = -jnp.exp(alog_ref[...]) * _softplus(ab + dtb_ref[...])
    zero = jnp.zeros((bsz, LANES), F32)

    for h in range(DN_HEADS):
        q = [_l2norm(a[:, h * DN_DK:(h + 1) * DN_DK]) * (DN_DK ** -0.5) for a in act]
        k = [_l2norm(a[:, hk + h * DN_DK:hk + (h + 1) * DN_DK]) for a in act]
        v = [a[:, 2 * hk + h * DN_DV:2 * hk + (h + 1) * DN_DV] for a in act]
        beta = [rows(beta_all, t)[:, DN_HEADS + h:DN_HEADS + h + 1] for t in range(t_len)]
        g = [rows(g_all, t)[:, h:h + 1] for t in range(t_len)]
        gc = [g[0]]
        for t in range(1, t_len):
            gc.append(gc[-1] + g[t])
        kb = [k[t] * beta[t] for t in range(t_len)]
        vb = [v[t] * beta[t] for t in range(t_len)]
        eg = [jnp.exp(gc[t]) for t in range(t_len)]
        kbe = [kb[t] * eg[t] for t in range(t_len)]
        lm = [[None] * t_len for _ in range(t_len)]
        n_qk = 0
        for i in range(t_len):
            for j in range(i + 1):
                dec = jnp.exp(gc[i] - gc[j])
                qk_scr[h, n_qk] = jnp.broadcast_to(_rowsum(q[i] * k[j]) * dec, (bsz, LANES))
                n_qk += 1
                if j < i:
                    lm[i][j] = _rowsum(kb[i] * k[j]) * dec
        tm = [[None] * t_len for _ in range(t_len)]
        for i in range(t_len):
            for j in range(i):
                acc = lm[i][j]
                for m in range(j + 1, i):
                    acc = acc + lm[i][m] * tm[m][j]
                tm[i][j] = -acc
        for i in range(t_len):
            ui, wi = vb[i], kbe[i]
            for j in range(i):
                ui = ui + tm[i][j] * vb[j]
                wi = wi + tm[i][j] * kbe[j]
            lhs_scr[h, pl.ds(i, bsz, stride=_PACK), :] = wi
            lhs_scr[h, pl.ds(t_len + i, bsz, stride=_PACK), :] = q[i] * eg[i]
            u_scr[h, pl.ds(i, bsz, stride=_PACK), :] = ui
            u_scr[h, pl.ds(t_len + i, bsz, stride=_PACK), :] = zero
            kd_scr[h, pl.ds(i, bsz, stride=_PACK), :] = k[i] * jnp.exp(gc[t_len - 1] - gc[i])
            kd_scr[h, pl.ds(t_len + i, bsz, stride=_PACK), :] = zero
        gl_scr[h] = jnp.broadcast_to(jnp.exp(gc[t_len - 1]), (bsz, LANES))


def _sample_phase_c(x_ref, mod_ref, w_out_ref, ng_ref, lng_ref, lnb_ref, y_ref,
                    og_scr, osc_scr, res_scr, vn_scr, qk_scr, *, alpha, t_len, d):
    bsz = x_ref.shape[0]
    sub = 1
    x = _stack_time(x_ref, t_len, d)
    gate = _mod_rows(mod_ref, 2, t_len, d)
    og = og_scr[...]
    per_t = []
    for i in range(t_len):
        heads = []
        for h in range(DN_HEADS):
            oi = res_scr[h, pl.ds(t_len + i, bsz, stride=_PACK), :]
            base = i * (i + 1) // 2
            for j in range(i + 1):
                oi = oi + qk_scr[h, base + j] * vn_scr[h, pl.ds(j, bsz, stride=_PACK), :]
            heads.append(_gate_norm(oi, og[i * bsz:(i + 1) * bsz, h * DN_DV:(h + 1) * DN_DV], ng_ref[...]))
        per_t.append(jnp.concatenate(heads, axis=1))
    o_dn = jnp.concatenate(per_t, axis=0)
    y = _mix_out(x, o_dn, osc_scr[...], gate, w_out_ref,
                 lng_ref[sub:sub + 1, :], lnb_ref[sub:sub + 1, :], alpha)
    for t in range(t_len):
        y_ref[:, t * d:(t + 1) * d] = y[t * bsz:(t + 1) * bsz]


def _mixer_sample_kernel(x_ref, mod_ref, wq_ref, wr_ref, wab_ref, w_out_ref, cw_ref, mw_ref, alog_ref, dtb_ref, ng_ref,
                         lng_ref, lnb_ref, cq_in_ref, cm_in_ref, s_in_ref,
                         y_ref, cq_out_ref, cm_out_ref, s_out_ref,
                         og_scr, osc_scr, lhs_scr, u_scr, kd_scr, gl_scr, qk_scr, res_scr, vn_scr,
                         *, alpha, t_len, d, off, n_conv, n_mconv, bt):
    step = pl.program_id(0)
    assert 2 * t_len == _PACK, "packed buffers hold [T rows | T rows] per sequence"

    @pl.when(step == 0)
    def _():
        _sample_phase_a(x_ref, mod_ref, wq_ref, wr_ref, wab_ref, cw_ref, mw_ref, alog_ref, dtb_ref, cq_in_ref, cm_in_ref,
                        cq_out_ref, cm_out_ref, og_scr, osc_scr, lhs_scr, u_scr, kd_scr, gl_scr, qk_scr,
                        t_len=t_len, d=d, off=off, n_conv=n_conv, n_mconv=n_mconv)

    top = (lax.broadcasted_iota(jnp.int32, (_PACK, LANES), 0) < t_len).astype(F32)

    chains = [(bl, h) for bl in range(bt) for h in range(DN_HEADS)]
    seq0 = step * bt
    row0 = [pl.multiple_of((seq0 + bl) * _PACK, _PACK) for bl in range(bt)]
    states = [s_in_ref[bl, h] for bl, h in chains]
    r = [_mm(lhs_scr[h, pl.ds(row0[bl], _PACK), :], s) for (bl, h), s in zip(chains, states)]
    vns = []
    for (bl, h), r_u in zip(chains, r):
        res_scr[h, pl.ds(row0[bl], _PACK), :] = r_u
        vn = (u_scr[h, pl.ds(row0[bl], _PACK), :] - r_u) * top
        vn_scr[h, pl.ds(row0[bl], _PACK), :] = vn
        vns.append(vn)
    upd = [_mm_at(kd_scr[h, pl.ds(row0[bl], _PACK), :], vn) for (bl, h), vn in zip(chains, vns)]
    for (bl, h), s, s_add in zip(chains, states, upd):
        s_out_ref[bl, h] = s * gl_scr[h, pl.ds(seq0 + bl, 1), :] + s_add

    @pl.when(step == pl.num_programs(0) - 1)
    def _():
        _sample_phase_c(x_ref, mod_ref, w_out_ref, ng_ref, lng_ref, lnb_ref, y_ref,
                        og_scr, osc_scr, res_scr, vn_scr, qk_scr, alpha=alpha, t_len=t_len, d=d)


def _mixer_sample(x, mod, w_in, w_out, conv_w, mconv_w, alog, dtb, norm_g, lng, lnb, s_in, cq_in, cm_in,
                  *, alpha, off, bt):
    bsz, t, d = x.shape
    assert bsz % bt == 0
    qkv_dim = conv_w.shape[-1]
    sc_w = off["sc_c"][1] - off["sc_c"][0]
    dn_w = DN_HEADS * DN_DV
    n_conv, n_mconv = conv_w.shape[0], mconv_w.shape[0]
    n_qk = t * (t + 1) // 2
    x2 = x.reshape(bsz, t * d)
    cq2 = cq_in.reshape(bsz, (n_conv - 1) * qkv_dim)
    cm2 = cm_in.reshape(bsz, (n_mconv - 1) * sc_w)
    kern = functools.partial(_mixer_sample_kernel, alpha=alpha, t_len=t, d=d, off=off,
                             n_conv=n_conv, n_mconv=n_mconv, bt=bt)
    consts = (*w_in, w_out, conv_w, mconv_w, alog, dtb, norm_g, lng, lnb, cq2, cm2)
    state_spec = pl.BlockSpec((bt, DN_HEADS, DN_DK, DN_DV), lambda i: (i, 0, 0, 0))
    packed = pltpu.VMEM((DN_HEADS, _PACK * bsz, LANES), F32)
    y, cq, cm, s_out = pl.pallas_call(
        kern,
        grid=(bsz // bt,),
        in_specs=([_const_spec(x2.shape), _sub_mod_spec(bsz, d, 1)]
                  + [_const_spec(a.shape) for a in consts] + [state_spec]),
        out_specs=[pl.BlockSpec(x2.shape, lambda i: (0, 0)),
                   pl.BlockSpec(cq2.shape, lambda i: (0, 0)),
                   pl.BlockSpec(cm2.shape, lambda i: (0, 0)),
                   state_spec],
        out_shape=[jax.ShapeDtypeStruct(x2.shape, F32),
                   jax.ShapeDtypeStruct(cq2.shape, F32),
                   jax.ShapeDtypeStruct(cm2.shape, F32),
                   jax.ShapeDtypeStruct(s_in.shape, F32)],
        scratch_shapes=[pltpu.VMEM((t * bsz, dn_w), F32),
                        pltpu.VMEM((t * bsz, sc_w), F32),
                        packed,
                        packed,
                        packed,
                        pltpu.VMEM((DN_HEADS, bsz, LANES), F32),
                        pltpu.VMEM((DN_HEADS, n_qk, bsz, LANES), F32),
                        packed,
                        packed],
        compiler_params=pltpu.CompilerParams(dimension_semantics=("arbitrary",),
                                             vmem_limit_bytes=V7X_VMEM_LIMIT),
        name="mixer_sample",
    )(x2, mod, *consts, s_in)
    return (y.reshape(bsz, t, d), s_out, cq.reshape(bsz, n_conv - 1, qkv_dim),
            cm.reshape(bsz, n_mconv - 1, sc_w))


def _split_w_in(w_in, qkv_dim):
    c1 = qkv_dim + 2 * DN_HEADS
    wab = jnp.pad(w_in[:, qkv_dim:c1], ((0, 0), (0, AB_PAD - 2 * DN_HEADS)))
    return w_in[:, :qkv_dim].astype(BF16), w_in[:, c1:].astype(BF16), wab.astype(BF16)


def _pad_lanes(v):
    return jnp.zeros((1, AB_PAD), F32).at[0, :v.shape[0]].set(v)


def kernel(x_prompt, x_sample, state_ssm, state_conv_qkv, state_conv_mix, c_prompt, c_sample, w_ada, b_ada, ln_g, ln_b, ffn1_wg, ffn1_wu, ffn1_wd, ffn2_wg, ffn2_wu, ffn2_wd, w_in, conv_qkv_w, a_log, dt_bias, dn_norm_g, conv_mix_w, w_out):
    depth = w_ada.shape[0]
    alpha = (2 * depth) ** 0.25
    bp, tp, d = x_prompt.shape
    qkv_dim = conv_qkv_w.shape[-1]
    sc_width = conv_mix_w.shape[-1]
    dn_width = DN_HEADS * DN_DV
    off = _proj_layout(dn_width, sc_width)
    tm = min(512, tp)
    nseq = 2 if bp % 2 == 0 else 1
    tt = min(512 // nseq, tp)
    bt = min(8, x_sample.shape[0])

    hp, hs = x_prompt, x_sample
    outs = [[] for _ in range(6)]
    for l in range(depth):
        mod_p, mod_s = _adaln(c_prompt, c_sample, w_ada[l], b_ada[l].reshape(1, -1), tn=3 * d)
        mod_p = mod_p.reshape(bp, 3 * N_SUB, d)
        lng, lnb = ln_g[l], ln_b[l]
        w1 = (ffn1_wg[l], ffn1_wu[l], ffn1_wd[l])
        w2 = (ffn2_wg[l], ffn2_wu[l], ffn2_wd[l])
        w_in_p = _split_w_in(w_in[l], qkv_dim)
        w_out_b = w_out[l].astype(BF16)
        alog, dtb = _pad_lanes(a_log[l]), _pad_lanes(dt_bias[l])
        norm_g = dn_norm_g[l].reshape(1, -1)
        mix_args = (w_in_p, w_out_b, conv_qkv_w[l], conv_mix_w[l], alog, dtb, norm_g, lng, lnb)

        hp, hs = _ffn(hp, hs, mod_p, mod_s, *w1, lng, lnb, sub=0, alpha=alpha, tm=tm)
        hp, a1, a2, a3 = _mixer_prompt_pipe(hp, mod_p, *mix_args, alpha=alpha, nseq=nseq, tt=tt, off=off)
        hs, b1, b2, b3 = _mixer_sample(hs, mod_s, *mix_args, state_ssm[l], state_conv_qkv[l],
                                       state_conv_mix[l], alpha=alpha, off=off, bt=bt)
        hp, hs = _ffn(hp, hs, mod_p, mod_s, *w2, lng, lnb, sub=2, alpha=alpha, tm=tm)
        for lst, val in zip(outs, (a1, a2, a3, b1, b2, b3)):
            lst.append(val)
    return (hp, hs) + tuple(jnp.stack(o) for o in outs)
```

```python
import functools

import jax
import jax.numpy as jnp
from jax import lax
from jax.experimental import pallas as pl
from jax.experimental.pallas import tpu as pltpu

F32 = jnp.float32
BF16 = jnp.bfloat16

LN_EPS = 1e-5
RMS_EPS = 1e-6
N_SUB = 3
DN_HEADS = 4
DN_DK = 128
DN_DV = 128
CHUNK = 64
LANES = 128
SUBLANES = 8
AB_PAD = LANES

V7X_VMEM_LIMIT = 60000 * 1024
CAST_BUFFERS = 4


def _mm(a, b):
    return jnp.dot(a.astype(BF16), b.astype(BF16), preferred_element_type=F32)


def _mm_bt(a, b):
    return lax.dot_general(a.astype(BF16), b.astype(BF16), (((1,), (1,)), ((), ())),
                           preferred_element_type=F32)


def _mm_at(a, b):
    return lax.dot_general(a.astype(BF16), b.astype(BF16), (((0,), (0,)), ((), ())),
                           preferred_element_type=F32)


def _mm_f32(a, b):
    return jnp.dot(a, b, preferred_element_type=F32, precision=lax.Precision.HIGHEST)


def _silu(x):
    return x * jax.nn.sigmoid(x)


def _softplus(x):
    return jnp.maximum(x, 0.0) + jnp.log1p(jnp.exp(-jnp.abs(x)))


def _layer_norm(y, g, b):
    mu = jnp.mean(y, axis=-1, keepdims=True)
    yc = y - mu
    var = jnp.mean(yc * yc, axis=-1, keepdims=True)
    return yc * lax.rsqrt(var + LN_EPS) * g + b


def _post(x, delta, gate, g, b, alpha):
    return _layer_norm(alpha * x + gate * delta, g, b)


def _rowsum(x):
    return jnp.sum(x, axis=-1, keepdims=True)


def _l2norm(x):
    return x * lax.rsqrt(_rowsum(x * x) + RMS_EPS)


def _const_spec(shape):
    nd = len(shape)
    return pl.BlockSpec(shape, lambda *_: (0,) * nd, pipeline_mode=pl.Buffered(1))


def _adaln_kernel(cp_ref, cs_ref, w_ref, b_ref, op_ref, os_ref):
    mp = cp_ref.shape[0]
    c = jnp.concatenate([cp_ref[...], cs_ref[...]], axis=0)
    mod = _mm(_silu(c), w_ref[...]) + b_ref[...]
    op_ref[...] = mod[:mp]
    os_ref[...] = mod[mp:]


def _adaln(c_p, c_s, w, b, tn):
    (mp, d), ms = c_p.shape, c_s.shape[0]
    n = w.shape[1]
    assert n % tn == 0
    return pl.pallas_call(
        _adaln_kernel,
        grid=(n // tn,),
        in_specs=[_const_spec((mp, d)), _const_spec((ms, d)),
                  pl.BlockSpec((d, tn), lambda j: (0, j)),
                  pl.BlockSpec((1, tn), lambda j: (0, j))],
        out_specs=[pl.BlockSpec((mp, tn), lambda j: (0, j)), pl.BlockSpec((ms, tn), lambda j: (0, j))],
        out_shape=[jax.ShapeDtypeStruct((mp, n), F32), jax.ShapeDtypeStruct((ms, n), F32)],
        compiler_params=pltpu.CompilerParams(dimension_semantics=("arbitrary",),
                                             vmem_limit_bytes=V7X_VMEM_LIMIT),
        name="adaln",
    )(c_p, c_s, w, b)


def _ffn_rows(x, shift, scale, gate, wg_ref, wu_ref, wd_ref, g, b, alpha):
    u = (x * (1.0 + scale) + shift).astype(BF16)
    hg = jnp.dot(u, wg_ref[...], preferred_element_type=F32)
    hu = jnp.dot(u, wu_ref[...], preferred_element_type=F32)
    h = (_silu(hg) * hu).astype(BF16)
    d = jnp.dot(h, wd_ref[...], preferred_element_type=F32)
    return _post(x, 0.5 * d, gate, g, b, alpha)


def _ffn_prompt_kernel(x_ref, mod_ref, wg_ref, wu_ref, wd_ref, lng_ref, lnb_ref, o_ref, *, sub, alpha, parts):
    shift = mod_ref[0, 3 * sub + 0:3 * sub + 1, :]
    scale = mod_ref[0, 3 * sub + 1:3 * sub + 2, :]
    gate = mod_ref[0, 3 * sub + 2:3 * sub + 3, :]
    rows = x_ref.shape[0] // parts
    for part in range(parts):
        sl = slice(part * rows, (part + 1) * rows)
        o_ref[sl, :] = _ffn_rows(x_ref[sl, :], shift, scale, gate, wg_ref, wu_ref, wd_ref,
                                 lng_ref[sub:sub + 1, :], lnb_ref[sub:sub + 1, :], alpha)


def _ffn_prompt(x, mod, wg, wu, wd, lng, lnb, *, sub, alpha, tm, parts):
    bsz, t, d = x.shape
    f = wg.shape[1]
    assert t % tm == 0 and tm % (parts * SUBLANES) == 0
    per_seq = t // tm
    x2 = x.reshape(bsz * t, d)
    out = pl.pallas_call(
        functools.partial(_ffn_prompt_kernel, sub=sub, alpha=alpha, parts=parts),
        grid=(bsz * per_seq,),
        in_specs=[pl.BlockSpec((tm, d), lambda i: (i, 0)),
                  pl.BlockSpec((1, 3 * N_SUB, d), lambda i: (i // per_seq, 0, 0)),
                  _const_spec((d, f)), _const_spec((d, f)), _const_spec((f, d)),
                  _const_spec(lng.shape), _const_spec(lnb.shape)],
        out_specs=pl.BlockSpec((tm, d), lambda i: (i, 0)),
        out_shape=jax.ShapeDtypeStruct((bsz * t, d), F32),
        compiler_params=pltpu.CompilerParams(dimension_semantics=("arbitrary",),
                                             vmem_limit_bytes=V7X_VMEM_LIMIT),
        name=f"ffn_prompt{sub}",
    )(x2, mod, wg, wu, wd, lng, lnb)
    return out.reshape(bsz, t, d)


def _stack_time(ref, t_len, d):
    return jnp.concatenate([ref[:, t * d:(t + 1) * d] for t in range(t_len)], axis=0)


def _mod_rows(mod_ref, idx, t_len, d):
    m = mod_ref[:, idx * d:(idx + 1) * d]
    return jnp.concatenate([m] * t_len, axis=0)


def _ffn_sample_kernel(x_ref, mod_ref, wg_ref, wu_ref, wd_ref, lng_ref, lnb_ref, o_ref, *,
                       sub, alpha, t_len, d):
    bsz = x_ref.shape[0]
    x = _stack_time(x_ref, t_len, d)
    y = _ffn_rows(x, _mod_rows(mod_ref, 0, t_len, d), _mod_rows(mod_ref, 1, t_len, d),
                  _mod_rows(mod_ref, 2, t_len, d), wg_ref, wu_ref, wd_ref,
                  lng_ref[sub:sub + 1, :], lnb_ref[sub:sub + 1, :], alpha)
    for t in range(t_len):
        o_ref[:, t * d:(t + 1) * d] = y[t * bsz:(t + 1) * bsz]


def _sub_mod_spec(bsz, d, sub):
    return pl.BlockSpec((bsz, 3 * d), lambda *_: (0, sub), pipeline_mode=pl.Buffered(1))


def _cast_weights(jobs):
    chunks = []
    used = {}
    for w_hbm, w_scr, stage, sem in jobs:
        nbuf, rows = stage.shape[0], stage.shape[1]
        for i in range(w_hbm.shape[0] // rows):
            slot = used.get(id(stage), 0) % nbuf
            used[id(stage)] = used.get(id(stage), 0) + 1
            copy = pltpu.make_async_copy(w_hbm.at[pl.ds(i * rows, rows), :], stage.at[slot], sem.at[slot])
            chunks.append((copy, w_scr, stage, slot, i * rows, rows))
    ahead = min(job[2].shape[0] for job in jobs) - 1
    for copy, *_ in chunks[:ahead]:
        copy.start()
    for j, (copy, w_scr, stage, slot, row0, rows) in enumerate(chunks):
        if j + ahead < len(chunks):
            chunks[j + ahead][0].start()
        copy.wait()
        w_scr[row0:row0 + rows, :] = stage[slot].astype(BF16)


def _swiglu(x, shift, scale, wg_ref, wu_ref, wd_ref):
    u = (x * (1.0 + scale) + shift).astype(BF16)
    hg = jnp.dot(u, wg_ref[...], preferred_element_type=F32)
    hu = jnp.dot(u, wu_ref[...], preferred_element_type=F32)
    h = (_silu(hg) * hu).astype(BF16)
    return jnp.dot(h, wd_ref[...], preferred_element_type=F32)


def _ffn_kernel(xp_ref, modp_ref, xq_ref, modq_ref, xs_ref, mods_ref, wg_hbm, wu_hbm, wd_hbm, lng_ref, lnb_ref,
                yp_ref, ys_ref, wg_scr, wu_scr, wd_scr, d_scr, stage_in, stage_out, sem_in, sem_out,
                *, sub, alpha, n_prompt, t_len, d):
    k = pl.program_id(0)
    g, b = lng_ref[sub:sub + 1, :], lnb_ref[sub:sub + 1, :]

    def matmuls():
        return _swiglu(xp_ref[...], modp_ref[0, 3 * sub + 0:3 * sub + 1, :], modp_ref[0, 3 * sub + 1:3 * sub + 2, :],
                       wg_scr, wu_scr, wd_scr)

    def finish(d_prev):
        yp_ref[...] = _post(xq_ref[...], 0.5 * d_prev, modq_ref[0, 3 * sub + 2:3 * sub + 3, :], g, b, alpha)

    @pl.when(k == 0)
    def _():
        _cast_weights([(wg_hbm, wg_scr, stage_in, sem_in), (wu_hbm, wu_scr, stage_in, sem_in),
                       (wd_hbm, wd_scr, stage_out, sem_out)])
        d_scr[...] = matmuls()

    @pl.when(jnp.logical_and(k > 0, k < n_prompt))
    def _():
        d_prev = d_scr[...]
        d_new = matmuls()
        finish(d_prev)
        d_scr[...] = d_new

    @pl.when(k == n_prompt)
    def _():
        finish(d_scr[...])
        bsz = xs_ref.shape[0]
        x = _stack_time(xs_ref, t_len, d)
        y = _ffn_rows(x, _mod_rows(mods_ref, 0, t_len, d), _mod_rows(mods_ref, 1, t_len, d),
                      _mod_rows(mods_ref, 2, t_len, d), wg_scr, wu_scr, wd_scr, g, b, alpha)
        for t in range(t_len):
            ys_ref[:, t * d:(t + 1) * d] = y[t * bsz:(t + 1) * bsz]


def _ffn(xp, xs, mod_p, mod_s, wg, wu, wd, lng, lnb, *, sub, alpha, tm):
    bp, tp, d = xp.shape
    bs, ts, _ = xs.shape
    f = wg.shape[1]
    assert tp % tm == 0
    per_seq = tp // tm
    n_prompt = bp * per_seq
    in_rows = max(r for r in (16, 32, 64, 128) if d % r == 0)
    out_rows = max(r for r in (16, 32, 64, 128, 176, 352) if f % r == 0)
    last = n_prompt - 1
    yp, ys = pl.pallas_call(
        functools.partial(_ffn_kernel, sub=sub, alpha=alpha, n_prompt=n_prompt, t_len=ts, d=d),
        grid=(n_prompt + 1,),
        in_specs=[pl.BlockSpec((tm, d), lambda k: (jnp.minimum(k, last), 0)),
                  pl.BlockSpec((1, 3 * N_SUB, d), lambda k: (jnp.minimum(k, last) // per_seq, 0, 0)),
                  pl.BlockSpec((tm, d), lambda k: (jnp.maximum(k - 1, 0), 0)),
                  pl.BlockSpec((1, 3 * N_SUB, d), lambda k: (jnp.maximum(k - 1, 0) // per_seq, 0, 0)),
                  _const_spec((bs, ts * d)), _sub_mod_spec(bs, d, sub),
                  pl.BlockSpec(memory_space=pl.ANY), pl.BlockSpec(memory_space=pl.ANY),
                  pl.BlockSpec(memory_space=pl.ANY),
                  _const_spec(lng.shape), _const_spec(lnb.shape)],
        out_specs=[pl.BlockSpec((tm, d), lambda k: (jnp.maximum(k - 1, 0), 0)),
                   pl.BlockSpec((bs, ts * d), lambda k: (0, 0))],
        out_shape=[jax.ShapeDtypeStruct((bp * tp, d), F32),
                   jax.ShapeDtypeStruct((bs, ts * d), F32)],
        scratch_shapes=[pltpu.VMEM((d, f), BF16), pltpu.VMEM((d, f), BF16), pltpu.VMEM((f, d), BF16),
                        pltpu.VMEM((tm, d), F32),
                        pltpu.VMEM((CAST_BUFFERS, in_rows, f), F32), pltpu.VMEM((CAST_BUFFERS, out_rows, d), F32),
                        pltpu.SemaphoreType.DMA((CAST_BUFFERS,)), pltpu.SemaphoreType.DMA((CAST_BUFFERS,))],
        compiler_params=pltpu.CompilerParams(dimension_semantics=("arbitrary",),
                                             vmem_limit_bytes=V7X_VMEM_LIMIT),
        name=f"ffn{sub}",
    )(xp.reshape(bp * tp, d), mod_p, xp.reshape(bp * tp, d), mod_p, xs.reshape(bs, ts * d), mod_s,
      wg, wu, wd, lng, lnb)
    return yp.reshape(bp, tp, d), ys.reshape(bs, ts, d)


def _ffn_sample(x, mod, wg, wu, wd, lng, lnb, *, sub, alpha):
    bsz, t, d = x.shape
    f = wg.shape[1]
    x2 = x.reshape(bsz, t * d)
    out = pl.pallas_call(
        functools.partial(_ffn_sample_kernel, sub=sub, alpha=alpha, t_len=t, d=d),
        grid=(1,),
        in_specs=[_const_spec((bsz, t * d)), _sub_mod_spec(bsz, d, sub),
                  _const_spec((d, f)), _const_spec((d, f)), _const_spec((f, d)),
                  _const_spec(lng.shape), _const_spec(lnb.shape)],
        out_specs=pl.BlockSpec((bsz, t * d), lambda i: (0, 0)),
        out_shape=jax.ShapeDtypeStruct((bsz, t * d), F32),
        compiler_params=pltpu.CompilerParams(dimension_semantics=("arbitrary",),
                                             vmem_limit_bytes=V7X_VMEM_LIMIT),
        name=f"ffn_sample{sub}",
    )(x2, mod, wg, wu, wd, lng, lnb)
    return out.reshape(bsz, t, d)


def _tri_inverse(lmats, c):
    ls = [l.astype(BF16) for l in lmats]
    ms = [jnp.dot(l, l, preferred_element_type=F32) for l in ls]
    qs = ms
    power = 2
    while 2 * power < c:
        mb = [m.astype(BF16) for m in ms]
        ms = [jnp.dot(m, m, preferred_element_type=F32) for m in mb]
        qm = [jnp.dot(q.astype(BF16), m.astype(BF16), preferred_element_type=F32) for q, m in zip(qs, ms)]
        qs = [q + m + x for q, m, x in zip(qs, ms, qm)]
        power *= 2
    rows = lax.broadcasted_iota(jnp.int32, (c, c), 0)
    cols = lax.broadcasted_iota(jnp.int32, (c, c), 1)
    eye = (rows == cols).astype(F32)
    lq = [jnp.dot(l, q.astype(BF16), preferred_element_type=F32) for l, q in zip(ls, qs)]
    return [eye - l + q - x for l, q, x in zip(lmats, qs, lq)]


def _gdn_chunks_local(units, tril, strict):
    c = units[0][1].shape[0]
    a = [lax.dot_general(jnp.concatenate([kb, q], axis=0), k, (((1,), (1,)), ((), ())),
                         preferred_element_type=F32)
         for kb, q, k, _, _, _, _ in units]
    lmats, qks = [], []
    for a_u, (_, _, _, _, _, gcol, grow) in zip(a, units):
        decay = jnp.where(tril, jnp.exp(jnp.where(tril, gcol - grow, 0.0)), 0.0)
        lmats.append(jnp.where(strict, a_u[:c] * decay, 0.0))
        qks.append(jnp.where(tril, a_u[c:] * decay, 0.0))
    tms = _tri_inverse(lmats, c)
    uw = [jnp.dot(tm.astype(BF16), jnp.concatenate([vb, kbe], axis=1), preferred_element_type=F32)
          for tm, (_, _, _, vb, kbe, _, _) in zip(tms, units)]
    return [(x[:, :DN_DV], x[:, DN_DV:], qk) for x, qk in zip(uw, qks)]


def _gdn_chunk_state(units, states):
    c = units[0][2].shape[0]
    r = [jnp.dot(jnp.concatenate([w, q_dec], axis=0), s.astype(BF16), preferred_element_type=F32)
         for (w, q_dec, _, _, _, _), s in zip(units, states)]
    v_new = [(u - r_h[:c]).astype(BF16) for (_, _, u, _, _, _), r_h in zip(units, r)]
    o = [r_h[c:] + jnp.dot(qk, v, preferred_element_type=F32)
         for (_, _, _, qk, _, _), r_h, v in zip(units, r, v_new)]
    s_new = [s * g_last + lax.dot_general(k_dec, v, (((0,), (0,)), ((), ())), preferred_element_type=F32)
             for (_, _, _, _, k_dec, g_last), s, v in zip(units, states, v_new)]
    return o, s_new


def _tri_inverse_steps(lmats, c):
    ls = [l.astype(BF16) for l in lmats]
    ms = [jnp.dot(l, l, preferred_element_type=F32) for l in ls]
    yield
    qs = ms
    power = 2
    while 2 * power < c:
        mb = [m.astype(BF16) for m in ms]
        ms = [jnp.dot(m, m, preferred_element_type=F32) for m in mb]
        yield
        qm = [jnp.dot(q.astype(BF16), m.astype(BF16), preferred_element_type=F32) for q, m in zip(qs, ms)]
        yield
        qs = [q + m + x for q, m, x in zip(qs, ms, qm)]
        power *= 2
    rows = lax.broadcasted_iota(jnp.int32, (c, c), 0)
    cols = lax.broadcasted_iota(jnp.int32, (c, c), 1)
    eye = (rows == cols).astype(F32)
    lq = [jnp.dot(l, q.astype(BF16), preferred_element_type=F32) for l, q in zip(ls, qs)]
    yield
    return [eye - l + q - x for l, q, x in zip(lmats, qs, lq)]


def _gdn_local_steps(units, tril, strict):
    c = units[0][1].shape[0]
    a = [lax.dot_general(jnp.concatenate([kb, q], axis=0), k, (((1,), (1,)), ((), ())),
                         preferred_element_type=F32)
         for kb, q, k, _, _, _, _ in units]
    yield
    lmats, qks = [], []
    for a_u, (_, _, _, _, _, gcol, grow) in zip(a, units):
        decay = jnp.where(tril, jnp.exp(jnp.where(tril, gcol - grow, 0.0)), 0.0)
        lmats.append(jnp.where(strict, a_u[:c] * decay, 0.0))
        qks.append(jnp.where(tril, a_u[c:] * decay, 0.0))
    tms = yield from _tri_inverse_steps(lmats, c)
    uw = [jnp.dot(tm.astype(BF16), jnp.concatenate([vb, kbe], axis=1), preferred_element_type=F32)
          for tm, (_, _, _, vb, kbe, _, _) in zip(tms, units)]
    yield
    return [(x[:, :DN_DV], x[:, DN_DV:], qk) for x, qk in zip(uw, qks)]


def _run_interleaved(*gens):
    live = list(gens)
    while live:
        for g in list(live):
            try:
                next(g)
            except StopIteration:
                live.remove(g)


def _gate_norm(o, og, norm_g):
    o = o * lax.rsqrt(jnp.mean(o * o, axis=-1, keepdims=True) + RMS_EPS) * norm_g
    return o * _silu(og)


def _proj_layout(dn_width, sc_width):
    off = {}
    pos = 0
    for name, width in (("og", dn_width), ("sc_b", sc_width), ("sc_c", sc_width), ("sc_h", sc_width)):
        off[name] = (pos, pos + width)
        pos += width
    return off


def _project(u, wq_ref, wr_ref, wab_ref, off):
    p = {"qkv": jnp.dot(u, wq_ref[...], preferred_element_type=F32),
         "ab": jnp.dot(u, wab_ref[...], preferred_element_type=F32)}
    rest = jnp.dot(u, wr_ref[...], preferred_element_type=F32)
    for name, (lo, hi) in off.items():
        p[name] = rest[:, lo:hi]
    return p


def _cols(p, off, name):
    del off
    return p[name]


def _mix_out(x, o_dn, o_sc, gate, w_out_ref, g, b, alpha):
    dn_w = o_dn.shape[1]
    mix = (jnp.dot(o_dn.astype(BF16), w_out_ref[0:dn_w, :], preferred_element_type=F32)
           + jnp.dot(o_sc.astype(BF16), w_out_ref[dn_w:, :], preferred_element_type=F32))
    return _post(x, mix, gate, g, b, alpha)


def _mixer_prompt_kernel(x_ref, mod_ref, wq_ref, wr_ref, wab_ref, w_out_ref, cw_ref, mw_ref, alog_ref, dtb_ref, ng_ref,
                         lng_ref, lnb_ref,
                         y_ref, s_out_ref, cq_out_ref, cm_out_ref,
                         s_scr, cq_scr, cm_scr, gct_scr, gl_scr, ops_scr, u_scr, w_scr, qk_scr, o_scr,
                         *, alpha, nseq, tt, off, n_conv, n_mconv):
    t_idx = pl.program_id(1)
    rows_all = nseq * tt
    nchunk = rows_all // CHUNK
    per_seq = tt // CHUNK
    hk = DN_HEADS * DN_DK
    sub = 1

    @pl.when(t_idx == 0)
    def _():
        s_scr[...] = jnp.zeros_like(s_scr)
        cq_scr[...] = jnp.zeros_like(cq_scr)
        cm_scr[...] = jnp.zeros_like(cm_scr)

    xs = [x_ref[s] for s in range(nseq)]
    gates = [mod_ref[s, 3 * sub + 2:3 * sub + 3, :] for s in range(nseq)]
    u = jnp.concatenate(
        [(xs[s] * (1.0 + mod_ref[s, 3 * sub + 1:3 * sub + 2, :]) + mod_ref[s, 3 * sub + 0:3 * sub + 1, :])
         .astype(BF16) for s in range(nseq)], axis=0)
    p = _project(u, wq_ref, wr_ref, wab_ref, off)

    def causal_conv(new_rows, hist_scr, s, w_ref, taps):
        ext = jnp.concatenate([hist_scr[s], new_rows], axis=0)
        acc = new_rows * w_ref[taps - 1:taps, :]
        for j in range(taps - 1):
            acc = acc + pltpu.roll(ext, taps - 1 - j, axis=0)[SUBLANES:] * w_ref[j:j + 1, :]
        hist_scr[s] = new_rows[tt - SUBLANES:]
        return acc

    pq = _cols(p, off, "qkv")
    acts = []
    for s in range(nseq):
        pq_s = pq[s * tt:(s + 1) * tt]
        acts.append(_silu(causal_conv(pq_s, cq_scr, s, cw_ref, n_conv)))
        cq_out_ref[s] = pq_s[tt - (n_conv - 1):]
    act = jnp.concatenate(acts, axis=0)

    ab = _cols(p, off, "ab")
    beta_all = jax.nn.sigmoid(ab)
    g = -jnp.exp(alog_ref[...]) * _softplus(ab + dtb_ref[...])
    rows = lax.broadcasted_iota(jnp.int32, (CHUNK, CHUNK), 0)
    cols = lax.broadcasted_iota(jnp.int32, (CHUNK, CHUNK), 1)
    tril = rows >= cols
    strict = rows > cols
    tril_f = tril.astype(F32)
    gc_parts, gl_parts = [], []
    for c in range(nchunk):
        gc_c = _mm_f32(tril_f, g[c * CHUNK:(c + 1) * CHUNK, :])
        gc_parts.append(gc_c)
        gl_parts.append(jnp.broadcast_to(gc_c[CHUNK - 1:CHUNK, :], (CHUNK, AB_PAD)))
    gc = jnp.concatenate(gc_parts, axis=0)
    gl_rows = jnp.concatenate(gl_parts, axis=0)
    gc_t = gc.T
    for c in range(nchunk):
        gct_scr[c] = gc_t[0:SUBLANES, c * CHUNK:(c + 1) * CHUNK]

    eg_all = jnp.exp(gc)
    ekd_all = jnp.exp(gl_rows - gc)
    egl_all = jnp.exp(gl_rows)
    for h in range(DN_HEADS):
        lanes = slice(h * DN_DK, (h + 1) * DN_DK)
        q = _l2norm(act[:, h * DN_DK:(h + 1) * DN_DK]) * (DN_DK ** -0.5)
        k = _l2norm(act[:, hk + h * DN_DK:hk + (h + 1) * DN_DK])
        v = act[:, 2 * hk + h * DN_DV:2 * hk + (h + 1) * DN_DV]
        beta = beta_all[:, DN_HEADS + h:DN_HEADS + h + 1]
        eg = eg_all[:, h:h + 1]
        kb = k * beta
        ops_scr[0, :, lanes] = kb.astype(BF16)
        ops_scr[1, :, lanes] = q.astype(BF16)
        ops_scr[2, :, lanes] = k.astype(BF16)
        ops_scr[3, :, lanes] = (v * beta).astype(BF16)
        ops_scr[4, :, lanes] = (kb * eg).astype(BF16)
        ops_scr[5, :, lanes] = (q * eg).astype(BF16)
        ops_scr[6, :, lanes] = (k * ekd_all[:, h:h + 1]).astype(BF16)
        gl_scr[:, lanes] = jnp.broadcast_to(egl_all[:, h:h + 1], (rows_all, DN_DV))

    head_lanes = [slice(h * DN_DK, (h + 1) * DN_DK) for h in range(DN_HEADS)]

    where = [(c, h, lanes) for c in range(nchunk) for h, lanes in enumerate(head_lanes)]
    units = [tuple(ops_scr[i, c * CHUNK:(c + 1) * CHUNK, lanes] for i in range(5))
             + (gc[c * CHUNK:(c + 1) * CHUNK, h:h + 1], gct_scr[c, h:h + 1, :])
             for c, h, lanes in where]
    for (c, h, lanes), (u_c, w_c, qk_c) in zip(where, _gdn_chunks_local(units, tril, strict)):
        u_scr[c * CHUNK:(c + 1) * CHUNK, lanes] = u_c
        w_scr[c * CHUNK:(c + 1) * CHUNK, lanes] = w_c.astype(BF16)
        qk_scr[c * DN_HEADS + h] = qk_c.astype(BF16)

    for j in range(per_seq):
        chains = [(s, s * per_seq + j, h, lanes) for s in range(nseq) for h, lanes in enumerate(head_lanes)]
        units = [(w_scr[c * CHUNK:(c + 1) * CHUNK, lanes], ops_scr[5, c * CHUNK:(c + 1) * CHUNK, lanes],
                  u_scr[c * CHUNK:(c + 1) * CHUNK, lanes], qk_scr[c * DN_HEADS + h],
                  ops_scr[6, c * CHUNK:(c + 1) * CHUNK, lanes], gl_scr[c * CHUNK:c * CHUNK + 1, lanes])
                 for s, c, h, lanes in chains]
        o_c, s_new = _gdn_chunk_state(units, [s_scr[s * DN_HEADS + h] for s, _, h, _ in chains])
        for (s, c, h, lanes), o_u, s_u in zip(chains, o_c, s_new):
            s_scr[s * DN_HEADS + h] = s_u
            o_scr[c * CHUNK:(c + 1) * CHUNK, lanes] = o_u
    for s in range(nseq):
        s_out_ref[s] = s_scr[s * DN_HEADS:(s + 1) * DN_HEADS]

    og = _cols(p, off, "og")
    o_dn = jnp.concatenate(
        [_gate_norm(o_scr[:, h * DN_DV:(h + 1) * DN_DV], og[:, h * DN_DV:(h + 1) * DN_DV], ng_ref[...])
         for h in range(DN_HEADS)], axis=1)

    z = _cols(p, off, "sc_c") * _cols(p, off, "sc_h")
    zcs = []
    for s in range(nseq):
        z_s = z[s * tt:(s + 1) * tt]
        zcs.append(causal_conv(z_s, cm_scr, s, mw_ref, n_mconv))
        cm_out_ref[s] = z_s[tt - (n_mconv - 1):]
    o_sc = _cols(p, off, "sc_b") * jnp.concatenate(zcs, axis=0)

    dn_w = o_dn.shape[1]
    mix = (jnp.dot(o_dn.astype(BF16), w_out_ref[0:dn_w, :], preferred_element_type=F32)
           + jnp.dot(o_sc.astype(BF16), w_out_ref[dn_w:, :], preferred_element_type=F32))
    for s in range(nseq):
        y_ref[s] = _post(xs[s], mix[s * tt:(s + 1) * tt], gates[s],
                         lng_ref[sub:sub + 1, :], lnb_ref[sub:sub + 1, :], alpha)


def _mixer_prompt(x, mod, w_in, w_out, conv_w, mconv_w, alog, dtb, norm_g, lng, lnb, *, alpha, nseq, tt, off):
    bsz, t, d = x.shape
    assert t % tt == 0 and tt % LANES == 0 and bsz % nseq == 0
    nt = t // tt
    rows = nseq * tt
    qkv_dim = conv_w.shape[-1]
    sc_w = off["sc_c"][1] - off["sc_c"][0]
    dn_w = DN_HEADS * DN_DV
    n_conv, n_mconv = conv_w.shape[0], mconv_w.shape[0]
    assert max(n_conv, n_mconv) - 1 <= SUBLANES <= tt
    nchunk = rows // CHUNK
    kern = functools.partial(_mixer_prompt_kernel, alpha=alpha, nseq=nseq, tt=tt, off=off,
                             n_conv=n_conv, n_mconv=n_mconv)
    y, s_out, cq, cm = pl.pallas_call(
        kern,
        grid=(bsz // nseq, nt),
        in_specs=[pl.BlockSpec((nseq, tt, d), lambda b, i: (b, i, 0)),
                  pl.BlockSpec((nseq, 3 * N_SUB, d), lambda b, i: (b, 0, 0)),
                  *[_const_spec(w.shape) for w in w_in], _const_spec(w_out.shape),
                  _const_spec(conv_w.shape), _const_spec(mconv_w.shape),
                  _const_spec(alog.shape), _const_spec(dtb.shape), _const_spec(norm_g.shape),
                  _const_spec(lng.shape), _const_spec(lnb.shape)],
        out_specs=[pl.BlockSpec((nseq, tt, d), lambda b, i: (b, i, 0)),
                   pl.BlockSpec((nseq, DN_HEADS, DN_DK, DN_DV), lambda b, i: (b, 0, 0, 0)),
                   pl.BlockSpec((nseq, n_conv - 1, qkv_dim), lambda b, i: (b, 0, 0)),
                   pl.BlockSpec((nseq, n_mconv - 1, sc_w), lambda b, i: (b, 0, 0))],
        out_shape=[jax.ShapeDtypeStruct((bsz, t, d), F32),
                   jax.ShapeDtypeStruct((bsz, DN_HEADS, DN_DK, DN_DV), F32),
                   jax.ShapeDtypeStruct((bsz, n_conv - 1, qkv_dim), F32),
                   jax.ShapeDtypeStruct((bsz, n_mconv - 1, sc_w), F32)],
        scratch_shapes=[pltpu.VMEM((nseq * DN_HEADS, DN_DK, DN_DV), F32),
                        pltpu.VMEM((nseq, SUBLANES, qkv_dim), F32),
                        pltpu.VMEM((nseq, SUBLANES, sc_w), F32),
                        pltpu.VMEM((nchunk, SUBLANES, CHUNK), F32),
                        pltpu.VMEM((rows, dn_w), F32),
                        pltpu.VMEM((7, rows, dn_w), BF16),
                        pltpu.VMEM((rows, dn_w), F32),
                        pltpu.VMEM((rows, dn_w), BF16),
                        pltpu.VMEM((nchunk * DN_HEADS, CHUNK, CHUNK), BF16),
                        pltpu.VMEM((rows, dn_w), F32)],
        compiler_params=pltpu.CompilerParams(dimension_semantics=("arbitrary", "arbitrary"),
                                             vmem_limit_bytes=V7X_VMEM_LIMIT),
        name="mixer_prompt",
    )(x, mod, *w_in, w_out, conv_w, mconv_w, alog, dtb, norm_g, lng, lnb)
    return y, s_out, cq, cm


def _mixer_prompt_pipe_kernel(xf_ref, modf_ref, xb_ref, modb_ref, wq_ref, wr_ref, wab_ref, w_out_ref, cw_ref,
                              mw_ref, alog_ref, dtb_ref, ng_ref, lng_ref, lnb_ref,
                              y_ref, s_out_ref, cq_out_ref, cm_out_ref,
                              s_scr, cq_scr, cm_scr, gct_scr,
                              u_scr, w_scr, qk_scr, qd_scr, kd_scr, gl_scr, og_scr, osc_scr,
                              *, alpha, nseq, tt, tiles_per_group, n_tiles, off, n_conv, n_mconv):
    k = pl.program_id(0)
    rows_all = nseq * tt
    nchunk = rows_all // CHUNK
    per_seq = tt // CHUNK
    hk = DN_HEADS * DN_DK
    sub = 1
    head_lanes = [slice(h * DN_DK, (h + 1) * DN_DK) for h in range(DN_HEADS)]
    t_front = lax.rem(jnp.minimum(k, n_tiles - 1), tiles_per_group)
    t_back = lax.rem(jnp.maximum(k - 1, 0), tiles_per_group)
    keep_front = jnp.where(t_front == 0, 0.0, 1.0).astype(F32)
    keep_back = jnp.where(t_back == 0, 0.0, 1.0).astype(F32)

    def causal_conv(new_rows, hist_scr, s, w_ref, taps):
        ext = jnp.concatenate([hist_scr[s] * keep_front, new_rows], axis=0)
        acc = new_rows * w_ref[taps - 1:taps, :]
        for j in range(taps - 1):
            acc = acc + pltpu.roll(ext, taps - 1 - j, axis=0)[SUBLANES:] * w_ref[j:j + 1, :]
        hist_scr[s] = new_rows[tt - SUBLANES:]
        return acc

    def front():
        u = jnp.concatenate(
            [(xf_ref[s] * (1.0 + modf_ref[s, 3 * sub + 1:3 * sub + 2, :])
              + modf_ref[s, 3 * sub + 0:3 * sub + 1, :]).astype(BF16) for s in range(nseq)], axis=0)
        pq = jnp.dot(u, wq_ref[...], preferred_element_type=F32)
        yield
        rest = jnp.dot(u, wr_ref[...], preferred_element_type=F32)
        yield
        ab = jnp.dot(u, wab_ref[...], preferred_element_type=F32)
        p = {name: rest[:, lo:hi] for name, (lo, hi) in off.items()}

        beta_all = jax.nn.sigmoid(ab)
        g = -jnp.exp(alog_ref[...]) * _softplus(ab + dtb_ref[...])
        rows = lax.broadcasted_iota(jnp.int32, (CHUNK, CHUNK), 0)
        cols = lax.broadcasted_iota(jnp.int32, (CHUNK, CHUNK), 1)
        tril = rows >= cols
        strict = rows > cols
        tril_f = tril.astype(F32)
        gc_parts, gl_parts = [], []
        for c in range(nchunk):
            gc_c = _mm_f32(tril_f, g[c * CHUNK:(c + 1) * CHUNK, :])
            gc_parts.append(gc_c)
            gl_parts.append(jnp.broadcast_to(gc_c[CHUNK - 1:CHUNK, :], (CHUNK, AB_PAD)))
        yield
        gc = jnp.concatenate(gc_parts, axis=0)
        gl_rows = jnp.concatenate(gl_parts, axis=0)
        gc_t = gc.T
        for c in range(nchunk):
            gct_scr[c] = gc_t[0:SUBLANES, c * CHUNK:(c + 1) * CHUNK]

        acts = []
        for s in range(nseq):
            pq_s = pq[s * tt:(s + 1) * tt]
            acts.append(_silu(causal_conv(pq_s, cq_scr, s, cw_ref, n_conv)))
            cq_out_ref[s] = pq_s[tt - (n_conv - 1):]
        act = jnp.concatenate(acts, axis=0)

        eg_all = jnp.exp(gc)
        ekd_all = jnp.exp(gl_rows - gc)
        egl_all = jnp.exp(gl_rows)
        per_head = []
        for h, lanes in enumerate(head_lanes):
            q = _l2norm(act[:, h * DN_DK:(h + 1) * DN_DK]) * (DN_DK ** -0.5)
            kk = _l2norm(act[:, hk + h * DN_DK:hk + (h + 1) * DN_DK])
            v = act[:, 2 * hk + h * DN_DV:2 * hk + (h + 1) * DN_DV]
            beta = beta_all[:, DN_HEADS + h:DN_HEADS + h + 1]
            eg = eg_all[:, h:h + 1]
            kb = kk * beta
            per_head.append((kb.astype(BF16), q.astype(BF16), kk.astype(BF16), (v * beta).astype(BF16),
                             (kb * eg).astype(BF16)))
            qd_scr[:, lanes] = (q * eg).astype(BF16)
            kd_scr[:, lanes] = (kk * ekd_all[:, h:h + 1]).astype(BF16)
            for c in range(nchunk):
                gl_scr[c * SUBLANES:(c + 1) * SUBLANES, lanes] = jnp.broadcast_to(
                    egl_all[c * CHUNK:c * CHUNK + SUBLANES, h:h + 1], (SUBLANES, DN_DV))

        z = p["sc_c"] * p["sc_h"]
        zcs = []
        for s in range(nseq):
            z_s = z[s * tt:(s + 1) * tt]
            zcs.append(causal_conv(z_s, cm_scr, s, mw_ref, n_mconv))
            cm_out_ref[s] = z_s[tt - (n_mconv - 1):]
        osc_scr[...] = p["sc_b"] * jnp.concatenate(zcs, axis=0)
        og_scr[...] = p["og"]

        where = [(c, h, lanes) for c in range(nchunk) for h, lanes in enumerate(head_lanes)]
        units = [tuple(arr[c * CHUNK:(c + 1) * CHUNK] for arr in per_head[h])
                 + (gc[c * CHUNK:(c + 1) * CHUNK, h:h + 1], gct_scr[c, h:h + 1, :])
                 for c, h, lanes in where]
        local = yield from _gdn_local_steps(units, tril, strict)
        for (c, h, lanes), (u_c, w_c, qk_c) in zip(where, local):
            u_scr[c * CHUNK:(c + 1) * CHUNK, lanes] = u_c
            w_scr[c * CHUNK:(c + 1) * CHUNK, lanes] = w_c.astype(BF16)
            qk_scr[c * DN_HEADS + h] = qk_c.astype(BF16)

    def back():
        u_all, w_all, qd_all, kd_all = u_scr[...], w_scr[...], qd_scr[...], kd_scr[...]
        gl_all, og, o_sc = gl_scr[...], og_scr[...], osc_scr[...]
        qks = [qk_scr[i] for i in range(nchunk * DN_HEADS)]
        xs = [xb_ref[s] for s in range(nseq)]
        gates = [modb_ref[s, 3 * sub + 2:3 * sub + 3, :] for s in range(nseq)]
        states = [s_scr[i] * keep_back for i in range(nseq * DN_HEADS)]
        o_rows = [[None] * DN_HEADS for _ in range(nchunk)]
        for j in range(per_seq):
            chains = [(s, s * per_seq + j, h, lanes) for s in range(nseq) for h, lanes in enumerate(head_lanes)]
            rws = [slice(c * CHUNK, (c + 1) * CHUNK) for _, c, _, _ in chains]
            r = [jnp.dot(jnp.concatenate([w_all[rw, lanes], qd_all[rw, lanes]], axis=0),
                         states[s * DN_HEADS + h].astype(BF16), preferred_element_type=F32)
                 for (s, c, h, lanes), rw in zip(chains, rws)]
            yield
            v_new = [(u_all[rw, lanes] - r_u[:CHUNK]).astype(BF16)
                     for (s, c, h, lanes), rw, r_u in zip(chains, rws, r)]
            for (s, c, h, lanes), r_u, v_u in zip(chains, r, v_new):
                o_rows[c][h] = r_u[CHUNK:] + jnp.dot(qks[c * DN_HEADS + h], v_u, preferred_element_type=F32)
            for (s, c, h, lanes), rw, v_u in zip(chains, rws, v_new):
                i = s * DN_HEADS + h
                states[i] = (states[i] * gl_all[c * SUBLANES:c * SUBLANES + 1, lanes]
                             + lax.dot_general(kd_all[rw, lanes], v_u, (((0,), (0,)), ((), ())),
                                               preferred_element_type=F32))
            yield
        for i, st in enumerate(states):
            s_scr[i] = st
        for s in range(nseq):
            for h in range(DN_HEADS):
                s_out_ref[s, h] = states[s * DN_HEADS + h]

        o_dn = jnp.concatenate(
            [_gate_norm(jnp.concatenate([o_rows[c][h] for c in range(nchunk)], axis=0),
                        og[:, lanes], ng_ref[...]) for h, lanes in enumerate(head_lanes)], axis=1)
        dn_w = o_dn.shape[1]
        mix = (jnp.dot(o_dn.astype(BF16), w_out_ref[0:dn_w, :], preferred_element_type=F32)
               + jnp.dot(o_sc.astype(BF16), w_out_ref[dn_w:, :], preferred_element_type=F32))
        yield
        for s in range(nseq):
            y_ref[s] = _post(xs[s], mix[s * tt:(s + 1) * tt], gates[s],
                             lng_ref[sub:sub + 1, :], lnb_ref[sub:sub + 1, :], alpha)

    @pl.when(k == 0)
    def _():
        s_scr[...] = jnp.zeros_like(s_scr)
        cq_scr[...] = jnp.zeros_like(cq_scr)
        cm_scr[...] = jnp.zeros_like(cm_scr)
        _run_interleaved(front())

    @pl.when(jnp.logical_and(k > 0, k < n_tiles))
    def _():
        _run_interleaved(back(), front())

    @pl.when(k == n_tiles)
    def _():
        _run_interleaved(back())


def _mixer_prompt_pipe(x, mod, w_in, w_out, conv_w, mconv_w, alog, dtb, norm_g, lng, lnb, *, alpha, nseq, tt, off):
    bsz, t, d = x.shape
    assert t % tt == 0 and tt % LANES == 0 and bsz % nseq == 0
    nt = t // tt
    n_tiles = (bsz // nseq) * nt
    rows = nseq * tt
    qkv_dim = conv_w.shape[-1]
    sc_w = off["sc_c"][1] - off["sc_c"][0]
    dn_w = DN_HEADS * DN_DV
    n_conv, n_mconv = conv_w.shape[0], mconv_w.shape[0]
    assert max(n_conv, n_mconv) - 1 <= SUBLANES <= tt
    nchunk = rows // CHUNK
    kern = functools.partial(_mixer_prompt_pipe_kernel, alpha=alpha, nseq=nseq, tt=tt, tiles_per_group=nt,
                             n_tiles=n_tiles, off=off, n_conv=n_conv, n_mconv=n_mconv)

    def front_tile(k):
        return jnp.minimum(k, n_tiles - 1)

    def back_tile(k):
        return jnp.maximum(k - 1, 0)

    y, s_out, cq, cm = pl.pallas_call(
        kern,
        grid=(n_tiles + 1,),
        in_specs=[pl.BlockSpec((nseq, tt, d), lambda k: (front_tile(k) // nt, front_tile(k) % nt, 0)),
                  pl.BlockSpec((nseq, 3 * N_SUB, d), lambda k: (front_tile(k) // nt, 0, 0)),
                  pl.BlockSpec((nseq, tt, d), lambda k: (back_tile(k) // nt, back_tile(k) % nt, 0)),
                  pl.BlockSpec((nseq, 3 * N_SUB, d), lambda k: (back_tile(k) // nt, 0, 0)),
                  *[_const_spec(w.shape) for w in w_in], _const_spec(w_out.shape),
                  _const_spec(conv_w.shape), _const_spec(mconv_w.shape),
                  _const_spec(alog.shape), _const_spec(dtb.shape), _const_spec(norm_g.shape),
                  _const_spec(lng.shape), _const_spec(lnb.shape)],
        out_specs=[pl.BlockSpec((nseq, tt, d), lambda k: (back_tile(k) // nt, back_tile(k) % nt, 0)),
                   pl.BlockSpec((nseq, DN_HEADS, DN_DK, DN_DV), lambda k: (back_tile(k) // nt, 0, 0, 0)),
                   pl.BlockSpec((nseq, n_conv - 1, qkv_dim), lambda k: (front_tile(k) // nt, 0, 0)),
                   pl.BlockSpec((nseq, n_mconv - 1, sc_w), lambda k: (front_tile(k) // nt, 0, 0))],
        out_shape=[jax.ShapeDtypeStruct((bsz, t, d), F32),
                   jax.ShapeDtypeStruct((bsz, DN_HEADS, DN_DK, DN_DV), F32),
                   jax.ShapeDtypeStruct((bsz, n_conv - 1, qkv_dim), F32),
                   jax.ShapeDtypeStruct((bsz, n_mconv - 1, sc_w), F32)],
        scratch_shapes=[pltpu.VMEM((nseq * DN_HEADS, DN_DK, DN_DV), F32),
                        pltpu.VMEM((nseq, SUBLANES, qkv_dim), F32),
                        pltpu.VMEM((nseq, SUBLANES, sc_w), F32),
                        pltpu.VMEM((nchunk, SUBLANES, CHUNK), F32),
                        pltpu.VMEM((rows, dn_w), F32),
                        pltpu.VMEM((rows, dn_w), BF16),
                        pltpu.VMEM((nchunk * DN_HEADS, CHUNK, CHUNK), BF16),
                        pltpu.VMEM((rows, dn_w), BF16),
                        pltpu.VMEM((rows, dn_w), BF16),
                        pltpu.VMEM((nchunk * SUBLANES, dn_w), F32),
                        pltpu.VMEM((rows, dn_w), F32),
                        pltpu.VMEM((rows, sc_w), F32)],
        compiler_params=pltpu.CompilerParams(dimension_semantics=("arbitrary",),
                                             vmem_limit_bytes=V7X_VMEM_LIMIT),
        name="mixer_prompt",
    )(x, mod, x, mod, *w_in, w_out, conv_w, mconv_w, alog, dtb, norm_g, lng, lnb)
    return y, s_out, cq, cm


_PACK = SUBLANES


def _sample_phase_a(x_ref, mod_ref, wq_ref, wr_ref, wab_ref, cw_ref, mw_ref, alog_ref, dtb_ref, cq_in_ref, cm_in_ref,
                    cq_out_ref, cm_out_ref, og_scr, osc_scr, lhs_scr, u_scr, kd_scr, gl_scr, qk_scr,
                    *, t_len, d, off, n_conv, n_mconv):
    bsz = x_ref.shape[0]
    hk = DN_HEADS * DN_DK
    qkv_dim = cw_ref.shape[-1]
    sc_w = off["sc_c"][1] - off["sc_c"][0]

    x = _stack_time(x_ref, t_len, d)
    u = (x * (1.0 + _mod_rows(mod_ref, 1, t_len, d)) + _mod_rows(mod_ref, 0, t_len, d)).astype(BF16)
    p = _project(u, wq_ref, wr_ref, wab_ref, off)

    def rows(arr, t):
        return arr[t * bsz:(t + 1) * bsz]

    pq = _cols(p, off, "qkv")
    seq = ([cq_in_ref[:, j * qkv_dim:(j + 1) * qkv_dim] for j in range(n_conv - 1)]
           + [rows(pq, t) for t in range(t_len)])
    act = []
    for t in range(t_len):
        acc = None
        for j in range(n_conv):
            term = seq[t + j] * cw_ref[j:j + 1, :]
            acc = term if acc is None else acc + term
        act.append(_silu(acc))
    for j in range(n_conv - 1):
        cq_out_ref[:, j * qkv_dim:(j + 1) * qkv_dim] = seq[t_len + j]

    z = _cols(p, off, "sc_c") * _cols(p, off, "sc_h")
    zseq = ([cm_in_ref[:, j * sc_w:(j + 1) * sc_w] for j in range(n_mconv - 1)]
            + [rows(z, t) for t in range(t_len)])
    zc = []
    for t in range(t_len):
        acc = None
        for j in range(n_mconv):
            term = zseq[t + j] * mw_ref[j:j + 1, :]
            acc = term if acc is None else acc + term
        zc.append(acc)
    for j in range(n_mconv - 1):
        cm_out_ref[:, j * sc_w:(j + 1) * sc_w] = zseq[t_len + j]
    osc_scr[...] = _cols(p, off, "sc_b") * jnp.concatenate(zc, axis=0)
    og_scr[...] = _cols(p, off, "og")

    ab = _cols(p, off, "ab")
    beta_all = jax.nn.sigmoid(ab)
    g_all = -jnp.exp(alog_ref[...]) * _softplus(ab + dtb_ref[...])
    zero = jnp.zeros((bsz, LANES), F32)

    for h in range(DN_HEADS):
        q = [_l2norm(a[:, h * DN_DK:(h + 1) * DN_DK]) * (DN_DK ** -0.5) for a in act]
        k = [_l2norm(a[:, hk + h * DN_DK:hk + (h + 1) * DN_DK]) for a in act]
        v = [a[:, 2 * hk + h * DN_DV:2 * hk + (h + 1) * DN_DV] for a in act]
        beta = [rows(beta_all, t)[:, DN_HEADS + h:DN_HEADS + h + 1] for t in range(t_len)]
        g = [rows(g_all, t)[:, h:h + 1] for t in range(t_len)]
        gc = [g[0]]
        for t in range(1, t_len):
            gc.append(gc[-1] + g[t])
        kb = [k[t] * beta[t] for t in range(t_len)]
        vb = [v[t] * beta[t] for t in range(t_len)]
        eg = [jnp.exp(gc[t]) for t in range(t_len)]
        kbe = [kb[t] * eg[t] for t in range(t_len)]
        lm = [[None] * t_len for _ in range(t_len)]
        n_qk = 0
        for i in range(t_len):
            for j in range(i + 1):
                dec = jnp.exp(gc[i] - gc[j])
                qk_scr[h, n_qk] = jnp.broadcast_to(_rowsum(q[i] * k[j]) * dec, (bsz, LANES))
                n_qk += 1
                if j < i:
                    lm[i][j] = _rowsum(kb[i] * k[j]) * dec
        tm = [[None] * t_len for _ in range(t_len)]
        for i in range(t_len):
            for j in range(i):
                acc = lm[i][j]
                for m in range(j + 1, i):
                    acc = acc + lm[i][m] * tm[m][j]
                tm[i][j] = -acc
        for i in range(t_len):
            ui, wi = vb[i], kbe[i]
            for j in range(i):
                ui = ui + tm[i][j] * vb[j]
                wi = wi + tm[i][j] * kbe[j]
            lhs_scr[h, pl.ds(i, bsz, stride=_PACK), :] = wi
            lhs_scr[h, pl.ds(t_len + i, bsz, stride=_PACK), :] = q[i] * eg[i]
            u_scr[h, pl.ds(i, bsz, stride=_PACK), :] = ui
            u_scr[h, pl.ds(t_len + i, bsz, stride=_PACK), :] = zero
            kd_scr[h, pl.ds(i, bsz, stride=_PACK), :] = k[i] * jnp.exp(gc[t_len - 1] - gc[i])
            kd_scr[h, pl.ds(t_len + i, bsz, stride=_PACK), :] = zero
        gl_scr[h] = jnp.broadcast_to(jnp.exp(gc[t_len - 1]), (bsz, LANES))


def _sample_phase_c(x_ref, mod_ref, w_out_ref, ng_ref, lng_ref, lnb_ref, y_ref,
                    og_scr, osc_scr, res_scr, vn_scr, qk_scr, *, alpha, t_len, d):
    bsz = x_ref.shape[0]
    sub = 1
    x = _stack_time(x_ref, t_len, d)
    gate = _mod_rows(mod_ref, 2, t_len, d)
    og = og_scr[...]
    per_t = []
    for i in range(t_len):
        heads = []
        for h in range(DN_HEADS):
            oi = res_scr[h, pl.ds(t_len + i, bsz, stride=_PACK), :]
            base = i * (i + 1) // 2
            for j in range(i + 1):
                oi = oi + qk_scr[h, base + j] * vn_scr[h, pl.ds(j, bsz, stride=_PACK), :]
            heads.append(_gate_norm(oi, og[i * bsz:(i + 1) * bsz, h * DN_DV:(h + 1) * DN_DV], ng_ref[...]))
        per_t.append(jnp.concatenate(heads, axis=1))
    o_dn = jnp.concatenate(per_t, axis=0)
    y = _mix_out(x, o_dn, osc_scr[...], gate, w_out_ref,
                 lng_ref[sub:sub + 1, :], lnb_ref[sub:sub + 1, :], alpha)
    for t in range(t_len):
        y_ref[:, t * d:(t + 1) * d] = y[t * bsz:(t + 1) * bsz]


def _mixer_sample_kernel(x_ref, mod_ref, wq_ref, wr_ref, wab_ref, w_out_ref, cw_ref, mw_ref, alog_ref, dtb_ref, ng_ref,
                         lng_ref, lnb_ref, cq_in_ref, cm_in_ref, s_in_ref,
                         y_ref, cq_out_ref, cm_out_ref, s_out_ref,
                         og_scr, osc_scr, lhs_scr, u_scr, kd_scr, gl_scr, qk_scr, res_scr, vn_scr,
                         *, alpha, t_len, d, off, n_conv, n_mconv, bt):
    step = pl.program_id(0)
    assert 2 * t_len == _PACK, "packed buffers hold [T rows | T rows] per sequence"

    @pl.when(step == 0)
    def _():
        _sample_phase_a(x_ref, mod_ref, wq_ref, wr_ref, wab_ref, cw_ref, mw_ref, alog_ref, dtb_ref, cq_in_ref, cm_in_ref,
                        cq_out_ref, cm_out_ref, og_scr, osc_scr, lhs_scr, u_scr, kd_scr, gl_scr, qk_scr,
                        t_len=t_len, d=d, off=off, n_conv=n_conv, n_mconv=n_mconv)

    top = (lax.broadcasted_iota(jnp.int32, (_PACK, LANES), 0) < t_len).astype(F32)

    chains = [(bl, h) for bl in range(bt) for h in range(DN_HEADS)]
    seq0 = step * bt
    row0 = [pl.multiple_of((seq0 + bl) * _PACK, _PACK) for bl in range(bt)]
    states = [s_in_ref[bl, h] for bl, h in chains]
    r = [_mm(lhs_scr[h, pl.ds(row0[bl], _PACK), :], s) for (bl, h), s in zip(chains, states)]
    vns = []
    for (bl, h), r_u in zip(chains, r):
        res_scr[h, pl.ds(row0[bl], _PACK), :] = r_u
        vn = (u_scr[h, pl.ds(row0[bl], _PACK), :] - r_u) * top
        vn_scr[h, pl.ds(row0[bl], _PACK), :] = vn
        vns.append(vn)
    upd = [_mm_at(kd_scr[h, pl.ds(row0[bl], _PACK), :], vn) for (bl, h), vn in zip(chains, vns)]
    for (bl, h), s, s_add in zip(chains, states, upd):
        s_out_ref[bl, h] = s * gl_scr[h, pl.ds(seq0 + bl, 1), :] + s_add

    @pl.when(step == pl.num_programs(0) - 1)
    def _():
        _sample_phase_c(x_ref, mod_ref, w_out_ref, ng_ref, lng_ref, lnb_ref, y_ref,
                        og_scr, osc_scr, res_scr, vn_scr, qk_scr, alpha=alpha, t_len=t_len, d=d)


def _mixer_sample(x, mod, w_in, w_out, conv_w, mconv_w, alog, dtb, norm_g, lng, lnb, s_in, cq_in, cm_in,
                  *, alpha, off, bt):
    bsz, t, d = x.shape
    assert bsz % bt == 0
    qkv_dim = conv_w.shape[-1]
    sc_w = off["sc_c"][1] - off["sc_c"][0]
    dn_w = DN_HEADS * DN_DV
    n_conv, n_mconv = conv_w.shape[0], mconv_w.shape[0]
    n_qk = t * (t + 1) // 2
    x2 = x.reshape(bsz, t * d)
    cq2 = cq_in.reshape(bsz, (n_conv - 1) * qkv_dim)
    cm2 = cm_in.reshape(bsz, (n_mconv - 1) * sc_w)
    kern = functools.partial(_mixer_sample_kernel, alpha=alpha, t_len=t, d=d, off=off,
                             n_conv=n_conv, n_mconv=n_mconv, bt=bt)
    consts = (*w_in, w_out, conv_w, mconv_w, alog, dtb, norm_g, lng, lnb, cq2, cm2)
    state_spec = pl.BlockSpec((bt, DN_HEADS, DN_DK, DN_DV), lambda i: (i, 0, 0, 0))
    packed = pltpu.VMEM((DN_HEADS, _PACK * bsz, LANES), F32)
    y, cq, cm, s_out = pl.pallas_call(
        kern,
        grid=(bsz // bt,),
        in_specs=([_const_spec(x2.shape), _sub_mod_spec(bsz, d, 1)]
                  + [_const_spec(a.shape) for a in consts] + [state_spec]),
        out_specs=[pl.BlockSpec(x2.shape, lambda i: (0, 0)),
                   pl.BlockSpec(cq2.shape, lambda i: (0, 0)),
                   pl.BlockSpec(cm2.shape, lambda i: (0, 0)),
                   state_spec],
        out_shape=[jax.ShapeDtypeStruct(x2.shape, F32),
                   jax.ShapeDtypeStruct(cq2.shape, F32),
                   jax.ShapeDtypeStruct(cm2.shape, F32),
                   jax.ShapeDtypeStruct(s_in.shape, F32)],
        scratch_shapes=[pltpu.VMEM((t * bsz, dn_w), F32),
                        pltpu.VMEM((t * bsz, sc_w), F32),
                        packed,
                        packed,
                        packed,
                        pltpu.VMEM((DN_HEADS, bsz, LANES), F32),
                        pltpu.VMEM((DN_HEADS, n_qk, bsz, LANES), F32),
                        packed,
                        packed],
        compiler_params=pltpu.CompilerParams(dimension_semantics=("arbitrary",),
                                             vmem_limit_bytes=V7X_VMEM_LIMIT),
        name="mixer_sample",
    )(x2, mod, *consts, s_in)
    return (y.reshape(bsz, t, d), s_out, cq.reshape(bsz, n_conv - 1, qkv_dim),
            cm.reshape(bsz, n_mconv - 1, sc_w))


def _split_w_in(w_in, qkv_dim):
    c1 = qkv_dim + 2 * DN_HEADS
    wab = jnp.pad(w_in[:, qkv_dim:c1], ((0, 0), (0, AB_PAD - 2 * DN_HEADS)))
    return w_in[:, :qkv_dim].astype(BF16), w_in[:, c1:].astype(BF16), wab.astype(BF16)


def _pad_lanes(v):
    return jnp.zeros((1, AB_PAD), F32).at[0, :v.shape[0]].set(v)


def kernel(x_prompt, x_sample, state_ssm, state_conv_qkv, state_conv_mix, c_prompt, c_sample, w_ada, b_ada, ln_g, ln_b, ffn1_wg, ffn1_wu, ffn1_wd, ffn2_wg, ffn2_wu, ffn2_wd, w_in, conv_qkv_w, a_log, dt_bias, dn_norm_g, conv_mix_w, w_out):
    depth = w_ada.shape[0]
    alpha = (2 * depth) ** 0.25
    bp, tp, d = x_prompt.shape
    qkv_dim = conv_qkv_w.shape[-1]
    sc_width = conv_mix_w.shape[-1]
    dn_width = DN_HEADS * DN_DV
    off = _proj_layout(dn_width, sc_width)
    tm = min(512, tp)
    nseq = 2 if bp % 2 == 0 else 1
    tt = min(512 // nseq, tp)
    bt = min(8, x_sample.shape[0])

    hp, hs = x_prompt, x_sample
    outs = [[] for _ in range(6)]
    for l in range(depth):
        mod_p, mod_s = _adaln(c_prompt, c_sample, w_ada[l], b_ada[l].reshape(1, -1), tn=3 * d)
        mod_p = mod_p.reshape(bp, 3 * N_SUB, d)
        lng, lnb = ln_g[l], ln_b[l]
        w1 = (ffn1_wg[l], ffn1_wu[l], ffn1_wd[l])
        w2 = (ffn2_wg[l], ffn2_wu[l], ffn2_wd[l])
        w_in_p = _split_w_in(w_in[l], qkv_dim)
        w_out_b = w_out[l].astype(BF16)
        alog, dtb = _pad_lanes(a_log[l]), _pad_lanes(dt_bias[l])
        norm_g = dn_norm_g[l].reshape(1, -1)
        mix_args = (w_in_p, w_out_b, conv_qkv_w[l], conv_mix_w[l], alog, dtb, norm_g, lng, lnb)

        hp, hs = _ffn(hp, hs, mod_p, mod_s, *w1, lng, lnb, sub=0, alpha=alpha, tm=tm)
        hp, a1, a2, a3 = _mixer_prompt_pipe(hp, mod_p, *mix_args, alpha=alpha, nseq=nseq, tt=tt, off=off)
        hs, b1, b2, b3 = _mixer_sample(hs, mod_s, *mix_args, state_ssm[l], state_conv_qkv[l],
                                       state_conv_mix[l], alpha=alpha, off=off, bt=bt)
        hp, hs = _ffn(hp, hs, mod_p, mod_s, *w2, lng, lnb, sub=2, alpha=alpha, tm=tm)
        for lst, val in zip(outs, (a1, a2, a3, b1, b2, b3)):
            lst.append(val)
    return (hp, hs) + tuple(jnp.stack(o) for o in outs)
```

```python
import functools

import jax
import jax.numpy as jnp
from jax import lax
from jax.experimental import pallas as pl
from jax.experimental.pallas import tpu as pltpu

F32 = jnp.float32
BF16 = jnp.bfloat16

LN_EPS = 1e-5
RMS_EPS = 1e-6
N_SUB = 3
DN_HEADS = 4
DN_DK = 128
DN_DV = 128
CHUNK = 64
LANES = 128
SUBLANES = 8
AB_PAD = LANES

V7X_VMEM_LIMIT = 60000 * 1024
CAST_BUFFERS = 4


def _mm(a, b):
    return jnp.dot(a.astype(BF16), b.astype(BF16), preferred_element_type=F32)


def _mm_bt(a, b):
    return lax.dot_general(a.astype(BF16), b.astype(BF16), (((1,), (1,)), ((), ())),
                           preferred_element_type=F32)


def _mm_at(a, b):
    return lax.dot_general(a.astype(BF16), b.astype(BF16), (((0,), (0,)), ((), ())),
                           preferred_element_type=F32)


def _mm_f32(a, b):
    return jnp.dot(a, b, preferred_element_type=F32, precision=lax.Precision.HIGHEST)


def _silu(x):
    return x * jax.nn.sigmoid(x)


def _softplus(x):
    return jnp.maximum(x, 0.0) + jnp.log1p(jnp.exp(-jnp.abs(x)))


def _layer_norm(y, g, b):
    mu = jnp.mean(y, axis=-1, keepdims=True)
    yc = y - mu
    var = jnp.mean(yc * yc, axis=-1, keepdims=True)
    return yc * lax.rsqrt(var + LN_EPS) * g + b


def _post(x, delta, gate, g, b, alpha):
    return _layer_norm(alpha * x + gate * delta, g, b)


def _rowsum(x):
    return jnp.sum(x, axis=-1, keepdims=True)


def _l2norm(x):
    return x * lax.rsqrt(_rowsum(x * x) + RMS_EPS)


def _const_spec(shape):
    nd = len(shape)
    return pl.BlockSpec(shape, lambda *_: (0,) * nd, pipeline_mode=pl.Buffered(1))


def _adaln_kernel(cp_ref, cs_ref, w_ref, b_ref, op_ref, os_ref):
    mp = cp_ref.shape[0]
    c = jnp.concatenate([cp_ref[...], cs_ref[...]], axis=0)
    mod = _mm(_silu(c), w_ref[...]) + b_ref[...]
    op_ref[...] = mod[:mp]
    os_ref[...] = mod[mp:]


def _adaln(c_p, c_s, w, b, tn):
    (mp, d), ms = c_p.shape, c_s.shape[0]
    n = w.shape[1]
    assert n % tn == 0
    return pl.pallas_call(
        _adaln_kernel,
        grid=(n // tn,),
        in_specs=[_const_spec((mp, d)), _const_spec((ms, d)),
                  pl.BlockSpec((d, tn), lambda j: (0, j)),
                  pl.BlockSpec((1, tn), lambda j: (0, j))],
        out_specs=[pl.BlockSpec((mp, tn), lambda j: (0, j)), pl.BlockSpec((ms, tn), lambda j: (0, j))],
        out_shape=[jax.ShapeDtypeStruct((mp, n), F32), jax.ShapeDtypeStruct((ms, n), F32)],
        compiler_params=pltpu.CompilerParams(dimension_semantics=("arbitrary",),
                                             vmem_limit_bytes=V7X_VMEM_LIMIT),
        name="adaln",
    )(c_p, c_s, w, b)


def _ffn_rows(x, shift, scale, gate, wg_ref, wu_ref, wd_ref, g, b, alpha):
    u = (x * (1.0 + scale) + shift).astype(BF16)
    hg = jnp.dot(u, wg_ref[...], preferred_element_type=F32)
    hu = jnp.dot(u, wu_ref[...], preferred_element_type=F32)
    h = (_silu(hg) * hu).astype(BF16)
    d = jnp.dot(h, wd_ref[...], preferred_element_type=F32)
    return _post(x, 0.5 * d, gate, g, b, alpha)


def _ffn_prompt_kernel(x_ref, mod_ref, wg_ref, wu_ref, wd_ref, lng_ref, lnb_ref, o_ref, *, sub, alpha, parts):
    shift = mod_ref[0, 3 * sub + 0:3 * sub + 1, :]
    scale = mod_ref[0, 3 * sub + 1:3 * sub + 2, :]
    gate = mod_ref[0, 3 * sub + 2:3 * sub + 3, :]
    rows = x_ref.shape[0] // parts
    for part in range(parts):
        sl = slice(part * rows, (part + 1) * rows)
        o_ref[sl, :] = _ffn_rows(x_ref[sl, :], shift, scale, gate, wg_ref, wu_ref, wd_ref,
                                 lng_ref[sub:sub + 1, :], lnb_ref[sub:sub + 1, :], alpha)


def _ffn_prompt(x, mod, wg, wu, wd, lng, lnb, *, sub, alpha, tm, parts):
    bsz, t, d = x.shape
    f = wg.shape[1]
    assert t % tm == 0 and tm % (parts * SUBLANES) == 0
    per_seq = t // tm
    x2 = x.reshape(bsz * t, d)
    out = pl.pallas_call(
        functools.partial(_ffn_prompt_kernel, sub=sub, alpha=alpha, parts=parts),
        grid=(bsz * per_seq,),
        in_specs=[pl.BlockSpec((tm, d), lambda i: (i, 0)),
                  pl.BlockSpec((1, 3 * N_SUB, d), lambda i: (i // per_seq, 0, 0)),
                  _const_spec((d, f)), _const_spec((d, f)), _const_spec((f, d)),
                  _const_spec(lng.shape), _const_spec(lnb.shape)],
        out_specs=pl.BlockSpec((tm, d), lambda i: (i, 0)),
        out_shape=jax.ShapeDtypeStruct((bsz * t, d), F32),
        compiler_params=pltpu.CompilerParams(dimension_semantics=("arbitrary",),
                                             vmem_limit_bytes=V7X_VMEM_LIMIT),
        name=f"ffn_prompt{sub}",
    )(x2, mod, wg, wu, wd, lng, lnb)
    return out.reshape(bsz, t, d)


def _stack_time(ref, t_len, d):
    return jnp.concatenate([ref[:, t * d:(t + 1) * d] for t in range(t_len)], axis=0)


def _mod_rows(mod_ref, idx, t_len, d):
    m = mod_ref[:, idx * d:(idx + 1) * d]
    return jnp.concatenate([m] * t_len, axis=0)


def _ffn_sample_kernel(x_ref, mod_ref, wg_ref, wu_ref, wd_ref, lng_ref, lnb_ref, o_ref, *,
                       sub, alpha, t_len, d):
    bsz = x_ref.shape[0]
    x = _stack_time(x_ref, t_len, d)
    y = _ffn_rows(x, _mod_rows(mod_ref, 0, t_len, d), _mod_rows(mod_ref, 1, t_len, d),
                  _mod_rows(mod_ref, 2, t_len, d), wg_ref, wu_ref, wd_ref,
                  lng_ref[sub:sub + 1, :], lnb_ref[sub:sub + 1, :], alpha)
    for t in range(t_len):
        o_ref[:, t * d:(t + 1) * d] = y[t * bsz:(t + 1) * bsz]


def _sub_mod_spec(bsz, d, sub):
    return pl.BlockSpec((bsz, 3 * d), lambda *_: (0, sub), pipeline_mode=pl.Buffered(1))


def _cast_weights(jobs):
    chunks = []
    used = {}
    for w_hbm, w_scr, stage, sem in jobs:
        nbuf, rows = stage.shape[0], stage.shape[1]
        for i in range(w_hbm.shape[0] // rows):
            slot = used.get(id(stage), 0) % nbuf
            used[id(stage)] = used.get(id(stage), 0) + 1
            copy = pltpu.make_async_copy(w_hbm.at[pl.ds(i * rows, rows), :], stage.at[slot], sem.at[slot])
            chunks.append((copy, w_scr, stage, slot, i * rows, rows))
    ahead = min(job[2].shape[0] for job in jobs) - 1
    for copy, *_ in chunks[:ahead]:
        copy.start()
    for j, (copy, w_scr, stage, slot, row0, rows) in enumerate(chunks):
        if j + ahead < len(chunks):
            chunks[j + ahead][0].start()
        copy.wait()
        w_scr[row0:row0 + rows, :] = stage[slot].astype(BF16)


def _swiglu(x, shift, scale, wg_ref, wu_ref, wd_ref):
    u = (x * (1.0 + scale) + shift).astype(BF16)
    hg = jnp.dot(u, wg_ref[...], preferred_element_type=F32)
    hu = jnp.dot(u, wu_ref[...], preferred_element_type=F32)
    h = (_silu(hg) * hu).astype(BF16)
    return jnp.dot(h, wd_ref[...], preferred_element_type=F32)


def _ffn_kernel(xp_ref, modp_ref, xq_ref, modq_ref, xs_ref, mods_ref, wg_hbm, wu_hbm, wd_hbm, lng_ref, lnb_ref,
                yp_ref, ys_ref, wg_scr, wu_scr, wd_scr, d_scr, stage_in, stage_out, sem_in, sem_out,
                *, sub, alpha, n_prompt, t_len, d):
    k = pl.program_id(0)
    g, b = lng_ref[sub:sub + 1, :], lnb_ref[sub:sub + 1, :]

    def matmuls():
        return _swiglu(xp_ref[...], modp_ref[0, 3 * sub + 0:3 * sub + 1, :], modp_ref[0, 3 * sub + 1:3 * sub + 2, :],
                       wg_scr, wu_scr, wd_scr)

    def finish(d_prev):
        yp_ref[...] = _post(xq_ref[...], 0.5 * d_prev, modq_ref[0, 3 * sub + 2:3 * sub + 3, :], g, b, alpha)

    @pl.when(k == 0)
    def _():
        _cast_weights([(wg_hbm, wg_scr, stage_in, sem_in), (wu_hbm, wu_scr, stage_in, sem_in),
                       (wd_hbm, wd_scr, stage_out, sem_out)])
        d_scr[...] = matmuls()

    @pl.when(jnp.logical_and(k > 0, k < n_prompt))
    def _():
        d_prev = d_scr[...]
        d_new = matmuls()
        finish(d_prev)
        d_scr[...] = d_new

    @pl.when(k == n_prompt)
    def _():
        finish(d_scr[...])
        bsz = xs_ref.shape[0]
        x = _stack_time(xs_ref, t_len, d)
        y = _ffn_rows(x, _mod_rows(mods_ref, 0, t_len, d), _mod_rows(mods_ref, 1, t_len, d),
                      _mod_rows(mods_ref, 2, t_len, d), wg_scr, wu_scr, wd_scr, g, b, alpha)
        for t in range(t_len):
            ys_ref[:, t * d:(t + 1) * d] = y[t * bsz:(t + 1) * bsz]


def _ffn(xp, xs, mod_p, mod_s, wg, wu, wd, lng, lnb, *, sub, alpha, tm):
    bp, tp, d = xp.shape
    bs, ts, _ = xs.shape
    f = wg.shape[1]
    assert tp % tm == 0
    per_seq = tp // tm
    n_prompt = bp * per_seq
    in_rows = max(r for r in (16, 32, 64, 128) if d % r == 0)
    out_rows = max(r for r in (16, 32, 64, 128, 176, 352) if f % r == 0)
    last = n_prompt - 1
    yp, ys = pl.pallas_call(
        functools.partial(_ffn_kernel, sub=sub, alpha=alpha, n_prompt=n_prompt, t_len=ts, d=d),
        grid=(n_prompt + 1,),
        in_specs=[pl.BlockSpec((tm, d), lambda k: (jnp.minimum(k, last), 0)),
                  pl.BlockSpec((1, 3 * N_SUB, d), lambda k: (jnp.minimum(k, last) // per_seq, 0, 0)),
                  pl.BlockSpec((tm, d), lambda k: (jnp.maximum(k - 1, 0), 0)),
                  pl.BlockSpec((1, 3 * N_SUB, d), lambda k: (jnp.maximum(k - 1, 0) // per_seq, 0, 0)),
                  _const_spec((bs, ts * d)), _sub_mod_spec(bs, d, sub),
                  pl.BlockSpec(memory_space=pl.ANY), pl.BlockSpec(memory_space=pl.ANY),
                  pl.BlockSpec(memory_space=pl.ANY),
                  _const_spec(lng.shape), _const_spec(lnb.shape)],
        out_specs=[pl.BlockSpec((tm, d), lambda k: (jnp.maximum(k - 1, 0), 0)),
                   pl.BlockSpec((bs, ts * d), lambda k: (0, 0))],
        out_shape=[jax.ShapeDtypeStruct((bp * tp, d), F32),
                   jax.ShapeDtypeStruct((bs, ts * d), F32)],
        scratch_shapes=[pltpu.VMEM((d, f), BF16), pltpu.VMEM((d, f), BF16), pltpu.VMEM((f, d), BF16),
                        pltpu.VMEM((tm, d), F32),
                        pltpu.VMEM((CAST_BUFFERS, in_rows, f), F32), pltpu.VMEM((CAST_BUFFERS, out_rows, d), F32),
                        pltpu.SemaphoreType.DMA((CAST_BUFFERS,)), pltpu.SemaphoreType.DMA((CAST_BUFFERS,))],
        compiler_params=pltpu.CompilerParams(dimension_semantics=("arbitrary",),
                                             vmem_limit_bytes=V7X_VMEM_LIMIT),
        name=f"ffn{sub}",
    )(xp.reshape(bp * tp, d), mod_p, xp.reshape(bp * tp, d), mod_p, xs.reshape(bs, ts * d), mod_s,
      wg, wu, wd, lng, lnb)
    return yp.reshape(bp, tp, d), ys.reshape(bs, ts, d)


def _ffn_sample(x, mod, wg, wu, wd, lng, lnb, *, sub, alpha):
    bsz, t, d = x.shape
    f = wg.shape[1]
    x2 = x.reshape(bsz, t * d)
    out = pl.pallas_call(
        functools.partial(_ffn_sample_kernel, sub=sub, alpha=alpha, t_len=t, d=d),
        grid=(1,),
        in_specs=[_const_spec((bsz, t * d)), _sub_mod_spec(bsz, d, sub),
                  _const_spec((d, f)), _const_spec((d, f)), _const_spec((f, d)),
                  _const_spec(lng.shape), _const_spec(lnb.shape)],
        out_specs=pl.BlockSpec((bsz, t * d), lambda i: (0, 0)),
        out_shape=jax.ShapeDtypeStruct((bsz, t * d), F32),
        compiler_params=pltpu.CompilerParams(dimension_semantics=("arbitrary",),
                                             vmem_limit_bytes=V7X_VMEM_LIMIT),
        name=f"ffn_sample{sub}",
    )(x2, mod, wg, wu, wd, lng, lnb)
    return out.reshape(bsz, t, d)


def _tri_inverse(lmats, c):
    ls = [l.astype(BF16) for l in lmats]
    ms = [jnp.dot(l, l, preferred_element_type=F32) for l in ls]
    qs = ms
    power = 2
    while 2 * power < c:
        mb = [m.astype(BF16) for m in ms]
        ms = [jnp.dot(m, m, preferred_element_type=F32) for m in mb]
        qm = [jnp.dot(q.astype(BF16), m.astype(BF16), preferred_element_type=F32) for q, m in zip(qs, ms)]
        qs = [q + m + x for q, m, x in zip(qs, ms, qm)]
        power *= 2
    rows = lax.broadcasted_iota(jnp.int32, (c, c), 0)
    cols = lax.broadcasted_iota(jnp.int32, (c, c), 1)
    eye = (rows == cols).astype(F32)
    lq = [jnp.dot(l, q.astype(BF16), preferred_element_type=F32) for l, q in zip(ls, qs)]
    return [eye - l + q - x for l, q, x in zip(lmats, qs, lq)]


def _gdn_chunks_local(units, tril, strict):
    c = units[0][1].shape[0]
    a = [lax.dot_general(jnp.concatenate([kb, q], axis=0), k, (((1,), (1,)), ((), ())),
                         preferred_element_type=F32)
         for kb, q, k, _, _, _, _ in units]
    lmats, qks = [], []
    for a_u, (_, _, _, _, _, gcol, grow) in zip(a, units):
        decay = jnp.where(tril, jnp.exp(jnp.where(tril, gcol - grow, 0.0)), 0.0)
        lmats.append(jnp.where(strict, a_u[:c] * decay, 0.0))
        qks.append(jnp.where(tril, a_u[c:] * decay, 0.0))
    tms = _tri_inverse(lmats, c)
    uw = [jnp.dot(tm.astype(BF16), jnp.concatenate([vb, kbe], axis=1), preferred_element_type=F32)
          for tm, (_, _, _, vb, kbe, _, _) in zip(tms, units)]
    return [(x[:, :DN_DV], x[:, DN_DV:], qk) for x, qk in zip(uw, qks)]


def _gdn_chunk_state(units, states):
    c = units[0][2].shape[0]
    r = [jnp.dot(jnp.concatenate([w, q_dec], axis=0), s.astype(BF16), preferred_element_type=F32)
         for (w, q_dec, _, _, _, _), s in zip(units, states)]
    v_new = [(u - r_h[:c]).astype(BF16) for (_, _, u, _, _, _), r_h in zip(units, r)]
    o = [r_h[c:] + jnp.dot(qk, v, preferred_element_type=F32)
         for (_, _, _, qk, _, _), r_h, v in zip(units, r, v_new)]
    s_new = [s * g_last + lax.dot_general(k_dec, v, (((0,), (0,)), ((), ())), preferred_element_type=F32)
             for (_, _, _, _, k_dec, g_last), s, v in zip(units, states, v_new)]
    return o, s_new


def _tri_inverse_steps(lmats, c):
    ls = [l.astype(BF16) for l in lmats]
    ms = [jnp.dot(l, l, preferred_element_type=F32) for l in ls]
    yield
    qs = ms
    power = 2
    while 2 * power < c:
        mb = [m.astype(BF16) for m in ms]
        ms = [jnp.dot(m, m, preferred_element_type=F32) for m in mb]
        yield
        qm = [jnp.dot(q.astype(BF16), m.astype(BF16), preferred_element_type=F32) for q, m in zip(qs, ms)]
        yield
        qs = [q + m + x for q, m, x in zip(qs, ms, qm)]
        power *= 2
    rows = lax.broadcasted_iota(jnp.int32, (c, c), 0)
    cols = lax.broadcasted_iota(jnp.int32, (c, c), 1)
    eye = (rows == cols).astype(F32)
    lq = [jnp.dot(l, q.astype(BF16), preferred_element_type=F32) for l, q in zip(ls, qs)]
    yield
    return [eye - l + q - x for l, q, x in zip(lmats, qs, lq)]


def _gdn_local_steps(units, tril, strict):
    c = units[0][1].shape[0]
    a = [lax.dot_general(jnp.concatenate([kb, q], axis=0), k, (((1,), (1,)), ((), ())),
                         preferred_element_type=F32)
         for kb, q, k, _, _, _, _ in units]
    yield
    lmats, qks = [], []
    for a_u, (_, _, _, _, _, gcol, grow) in zip(a, units):
        decay = jnp.where(tril, jnp.exp(jnp.where(tril, gcol - grow, 0.0)), 0.0)
        lmats.append(jnp.where(strict, a_u[:c] * decay, 0.0))
        qks.append(jnp.where(tril, a_u[c:] * decay, 0.0))
    tms = yield from _tri_inverse_steps(lmats, c)
    uw = [jnp.dot(tm.astype(BF16), jnp.concatenate([vb, kbe], axis=1), preferred_element_type=F32)
          for tm, (_, _, _, vb, kbe, _, _) in zip(tms, units)]
    yield
    return [(x[:, :DN_DV], x[:, DN_DV:], qk) for x, qk in zip(uw, qks)]


def _alternate(primary, secondary):
    result = None
    done = False
    while not done:
        try:
            next(primary)
            yield
        except StopIteration as stop:
            result, done = stop.value, True
        if secondary is not None:
            try:
                next(secondary)
                yield
            except StopIteration:
                secondary = None
    if secondary is not None:
        yield from secondary
    return result


def _run_interleaved(*gens):
    live = list(gens)
    while live:
        for g in list(live):
            try:
                next(g)
            except StopIteration:
                live.remove(g)


def _gate_norm(o, og, norm_g):
    o = o * lax.rsqrt(jnp.mean(o * o, axis=-1, keepdims=True) + RMS_EPS) * norm_g
    return o * _silu(og)


def _proj_layout(dn_width, sc_width):
    off = {}
    pos = 0
    for name, width in (("og", dn_width), ("sc_b", sc_width), ("sc_c", sc_width), ("sc_h", sc_width)):
        off[name] = (pos, pos + width)
        pos += width
    return off


def _project(u, wq_ref, wr_ref, wab_ref, off):
    p = {"qkv": jnp.dot(u, wq_ref[...], preferred_element_type=F32),
         "ab": jnp.dot(u, wab_ref[...], preferred_element_type=F32)}
    rest = jnp.dot(u, wr_ref[...], preferred_element_type=F32)
    for name, (lo, hi) in off.items():
        p[name] = rest[:, lo:hi]
    return p


def _cols(p, off, name):
    del off
    return p[name]


def _mix_out(x, o_dn, o_sc, gate, w_out_ref, g, b, alpha):
    dn_w = o_dn.shape[1]
    mix = (jnp.dot(o_dn.astype(BF16), w_out_ref[0:dn_w, :], preferred_element_type=F32)
           + jnp.dot(o_sc.astype(BF16), w_out_ref[dn_w:, :], preferred_element_type=F32))
    return _post(x, mix, gate, g, b, alpha)


def _mixer_prompt_kernel(x_ref, mod_ref, wq_ref, wr_ref, wab_ref, w_out_ref, cw_ref, mw_ref, alog_ref, dtb_ref, ng_ref,
                         lng_ref, lnb_ref,
                         y_ref, s_out_ref, cq_out_ref, cm_out_ref,
                         s_scr, cq_scr, cm_scr, gct_scr, gl_scr, ops_scr, u_scr, w_scr, qk_scr, o_scr,
                         *, alpha, nseq, tt, off, n_conv, n_mconv):
    t_idx = pl.program_id(1)
    rows_all = nseq * tt
    nchunk = rows_all // CHUNK
    per_seq = tt // CHUNK
    hk = DN_HEADS * DN_DK
    sub = 1

    @pl.when(t_idx == 0)
    def _():
        s_scr[...] = jnp.zeros_like(s_scr)
        cq_scr[...] = jnp.zeros_like(cq_scr)
        cm_scr[...] = jnp.zeros_like(cm_scr)

    xs = [x_ref[s] for s in range(nseq)]
    gates = [mod_ref[s, 3 * sub + 2:3 * sub + 3, :] for s in range(nseq)]
    u = jnp.concatenate(
        [(xs[s] * (1.0 + mod_ref[s, 3 * sub + 1:3 * sub + 2, :]) + mod_ref[s, 3 * sub + 0:3 * sub + 1, :])
         .astype(BF16) for s in range(nseq)], axis=0)
    p = _project(u, wq_ref, wr_ref, wab_ref, off)

    def causal_conv(new_rows, hist_scr, s, w_ref, taps):
        ext = jnp.concatenate([hist_scr[s], new_rows], axis=0)
        acc = new_rows * w_ref[taps - 1:taps, :]
        for j in range(taps - 1):
            acc = acc + pltpu.roll(ext, taps - 1 - j, axis=0)[SUBLANES:] * w_ref[j:j + 1, :]
        hist_scr[s] = new_rows[tt - SUBLANES:]
        return acc

    pq = _cols(p, off, "qkv")
    acts = []
    for s in range(nseq):
        pq_s = pq[s * tt:(s + 1) * tt]
        acts.append(_silu(causal_conv(pq_s, cq_scr, s, cw_ref, n_conv)))
        cq_out_ref[s] = pq_s[tt - (n_conv - 1):]
    act = jnp.concatenate(acts, axis=0)

    ab = _cols(p, off, "ab")
    beta_all = jax.nn.sigmoid(ab)
    g = -jnp.exp(alog_ref[...]) * _softplus(ab + dtb_ref[...])
    rows = lax.broadcasted_iota(jnp.int32, (CHUNK, CHUNK), 0)
    cols = lax.broadcasted_iota(jnp.int32, (CHUNK, CHUNK), 1)
    tril = rows >= cols
    strict = rows > cols
    tril_f = tril.astype(F32)
    gc_parts, gl_parts = [], []
    for c in range(nchunk):
        gc_c = _mm_f32(tril_f, g[c * CHUNK:(c + 1) * CHUNK, :])
        gc_parts.append(gc_c)
        gl_parts.append(jnp.broadcast_to(gc_c[CHUNK - 1:CHUNK, :], (CHUNK, AB_PAD)))
    gc = jnp.concatenate(gc_parts, axis=0)
    gl_rows = jnp.concatenate(gl_parts, axis=0)
    gc_t = gc.T
    for c in range(nchunk):
        gct_scr[c] = gc_t[0:SUBLANES, c * CHUNK:(c + 1) * CHUNK]

    eg_all = jnp.exp(gc)
    ekd_all = jnp.exp(gl_rows - gc)
    egl_all = jnp.exp(gl_rows)
    for h in range(DN_HEADS):
        lanes = slice(h * DN_DK, (h + 1) * DN_DK)
        q = _l2norm(act[:, h * DN_DK:(h + 1) * DN_DK]) * (DN_DK ** -0.5)
        k = _l2norm(act[:, hk + h * DN_DK:hk + (h + 1) * DN_DK])
        v = act[:, 2 * hk + h * DN_DV:2 * hk + (h + 1) * DN_DV]
        beta = beta_all[:, DN_HEADS + h:DN_HEADS + h + 1]
        eg = eg_all[:, h:h + 1]
        kb = k * beta
        ops_scr[0, :, lanes] = kb.astype(BF16)
        ops_scr[1, :, lanes] = q.astype(BF16)
        ops_scr[2, :, lanes] = k.astype(BF16)
        ops_scr[3, :, lanes] = (v * beta).astype(BF16)
        ops_scr[4, :, lanes] = (kb * eg).astype(BF16)
        ops_scr[5, :, lanes] = (q * eg).astype(BF16)
        ops_scr[6, :, lanes] = (k * ekd_all[:, h:h + 1]).astype(BF16)
        gl_scr[:, lanes] = jnp.broadcast_to(egl_all[:, h:h + 1], (rows_all, DN_DV))

    head_lanes = [slice(h * DN_DK, (h + 1) * DN_DK) for h in range(DN_HEADS)]

    where = [(c, h, lanes) for c in range(nchunk) for h, lanes in enumerate(head_lanes)]
    units = [tuple(ops_scr[i, c * CHUNK:(c + 1) * CHUNK, lanes] for i in range(5))
             + (gc[c * CHUNK:(c + 1) * CHUNK, h:h + 1], gct_scr[c, h:h + 1, :])
             for c, h, lanes in where]
    for (c, h, lanes), (u_c, w_c, qk_c) in zip(where, _gdn_chunks_local(units, tril, strict)):
        u_scr[c * CHUNK:(c + 1) * CHUNK, lanes] = u_c
        w_scr[c * CHUNK:(c + 1) * CHUNK, lanes] = w_c.astype(BF16)
        qk_scr[c * DN_HEADS + h] = qk_c.astype(BF16)

    for j in range(per_seq):
        chains = [(s, s * per_seq + j, h, lanes) for s in range(nseq) for h, lanes in enumerate(head_lanes)]
        units = [(w_scr[c * CHUNK:(c + 1) * CHUNK, lanes], ops_scr[5, c * CHUNK:(c + 1) * CHUNK, lanes],
                  u_scr[c * CHUNK:(c + 1) * CHUNK, lanes], qk_scr[c * DN_HEADS + h],
                  ops_scr[6, c * CHUNK:(c + 1) * CHUNK, lanes], gl_scr[c * CHUNK:c * CHUNK + 1, lanes])
                 for s, c, h, lanes in chains]
        o_c, s_new = _gdn_chunk_state(units, [s_scr[s * DN_HEADS + h] for s, _, h, _ in chains])
        for (s, c, h, lanes), o_u, s_u in zip(chains, o_c, s_new):
            s_scr[s * DN_HEADS + h] = s_u
            o_scr[c * CHUNK:(c + 1) * CHUNK, lanes] = o_u
    for s in range(nseq):
        s_out_ref[s] = s_scr[s * DN_HEADS:(s + 1) * DN_HEADS]

    og = _cols(p, off, "og")
    o_dn = jnp.concatenate(
        [_gate_norm(o_scr[:, h * DN_DV:(h + 1) * DN_DV], og[:, h * DN_DV:(h + 1) * DN_DV], ng_ref[...])
         for h in range(DN_HEADS)], axis=1)

    z = _cols(p, off, "sc_c") * _cols(p, off, "sc_h")
    zcs = []
    for s in range(nseq):
        z_s = z[s * tt:(s + 1) * tt]
        zcs.append(causal_conv(z_s, cm_scr, s, mw_ref, n_mconv))
        cm_out_ref[s] = z_s[tt - (n_mconv - 1):]
    o_sc = _cols(p, off, "sc_b") * jnp.concatenate(zcs, axis=0)

    dn_w = o_dn.shape[1]
    mix = (jnp.dot(o_dn.astype(BF16), w_out_ref[0:dn_w, :], preferred_element_type=F32)
           + jnp.dot(o_sc.astype(BF16), w_out_ref[dn_w:, :], preferred_element_type=F32))
    for s in range(nseq):
        y_ref[s] = _post(xs[s], mix[s * tt:(s + 1) * tt], gates[s],
                         lng_ref[sub:sub + 1, :], lnb_ref[sub:sub + 1, :], alpha)


def _mixer_prompt(x, mod, w_in, w_out, conv_w, mconv_w, alog, dtb, norm_g, lng, lnb, *, alpha, nseq, tt, off):
    bsz, t, d = x.shape
    assert t % tt == 0 and tt % LANES == 0 and bsz % nseq == 0
    nt = t // tt
    rows = nseq * tt
    qkv_dim = conv_w.shape[-1]
    sc_w = off["sc_c"][1] - off["sc_c"][0]
    dn_w = DN_HEADS * DN_DV
    n_conv, n_mconv = conv_w.shape[0], mconv_w.shape[0]
    assert max(n_conv, n_mconv) - 1 <= SUBLANES <= tt
    nchunk = rows // CHUNK
    kern = functools.partial(_mixer_prompt_kernel, alpha=alpha, nseq=nseq, tt=tt, off=off,
                             n_conv=n_conv, n_mconv=n_mconv)
    y, s_out, cq, cm = pl.pallas_call(
        kern,
        grid=(bsz // nseq, nt),
        in_specs=[pl.BlockSpec((nseq, tt, d), lambda b, i: (b, i, 0)),
                  pl.BlockSpec((nseq, 3 * N_SUB, d), lambda b, i: (b, 0, 0)),
                  *[_const_spec(w.shape) for w in w_in], _const_spec(w_out.shape),
                  _const_spec(conv_w.shape), _const_spec(mconv_w.shape),
                  _const_spec(alog.shape), _const_spec(dtb.shape), _const_spec(norm_g.shape),
                  _const_spec(lng.shape), _const_spec(lnb.shape)],
        out_specs=[pl.BlockSpec((nseq, tt, d), lambda b, i: (b, i, 0)),
                   pl.BlockSpec((nseq, DN_HEADS, DN_DK, DN_DV), lambda b, i: (b, 0, 0, 0)),
                   pl.BlockSpec((nseq, n_conv - 1, qkv_dim), lambda b, i: (b, 0, 0)),
                   pl.BlockSpec((nseq, n_mconv - 1, sc_w), lambda b, i: (b, 0, 0))],
        out_shape=[jax.ShapeDtypeStruct((bsz, t, d), F32),
                   jax.ShapeDtypeStruct((bsz, DN_HEADS, DN_DK, DN_DV), F32),
                   jax.ShapeDtypeStruct((bsz, n_conv - 1, qkv_dim), F32),
                   jax.ShapeDtypeStruct((bsz, n_mconv - 1, sc_w), F32)],
        scratch_shapes=[pltpu.VMEM((nseq * DN_HEADS, DN_DK, DN_DV), F32),
                        pltpu.VMEM((nseq, SUBLANES, qkv_dim), F32),
                        pltpu.VMEM((nseq, SUBLANES, sc_w), F32),
                        pltpu.VMEM((nchunk, SUBLANES, CHUNK), F32),
                        pltpu.VMEM((rows, dn_w), F32),
                        pltpu.VMEM((7, rows, dn_w), BF16),
                        pltpu.VMEM((rows, dn_w), F32),
                        pltpu.VMEM((rows, dn_w), BF16),
                        pltpu.VMEM((nchunk * DN_HEADS, CHUNK, CHUNK), BF16),
                        pltpu.VMEM((rows, dn_w), F32)],
        compiler_params=pltpu.CompilerParams(dimension_semantics=("arbitrary", "arbitrary"),
                                             vmem_limit_bytes=V7X_VMEM_LIMIT),
        name="mixer_prompt",
    )(x, mod, *w_in, w_out, conv_w, mconv_w, alog, dtb, norm_g, lng, lnb)
    return y, s_out, cq, cm


def _mixer_prompt_pipe_kernel(xf_ref, modf_ref, xb_ref, modb_ref, wq_ref, wr_ref, wab_ref, w_out_ref, cw_ref,
                              mw_ref, alog_ref, dtb_ref, ng_ref, lng_ref, lnb_ref,
                              y_ref, s_out_ref, cq_out_ref, cm_out_ref,
                              s_scr, cq_scr, cm_scr, gct_scr,
                              u_scr, w_scr, qk_scr, qd_scr, kd_scr, gl_scr, og_scr, osc_scr,
                              *, alpha, nseq, tt, tiles_per_group, n_tiles, off, n_conv, n_mconv):
    k = pl.program_id(0)
    rows_all = nseq * tt
    nchunk = rows_all // CHUNK
    per_seq = tt // CHUNK
    hk = DN_HEADS * DN_DK
    sub = 1
    head_lanes = [slice(h * DN_DK, (h + 1) * DN_DK) for h in range(DN_HEADS)]
    t_front = lax.rem(jnp.minimum(k, n_tiles - 1), tiles_per_group)
    t_back = lax.rem(jnp.maximum(k - 1, 0), tiles_per_group)
    keep_front = jnp.where(t_front == 0, 0.0, 1.0).astype(F32)
    keep_back = jnp.where(t_back == 0, 0.0, 1.0).astype(F32)

    def causal_conv(new_rows, hist_scr, s, w_ref, taps):
        ext = jnp.concatenate([hist_scr[s] * keep_front, new_rows], axis=0)
        acc = new_rows * w_ref[taps - 1:taps, :]
        for j in range(taps - 1):
            acc = acc + pltpu.roll(ext, taps - 1 - j, axis=0)[SUBLANES:] * w_ref[j:j + 1, :]
        hist_scr[s] = new_rows[tt - SUBLANES:]
        return acc

    def front():
        u = jnp.concatenate(
            [(xf_ref[s] * (1.0 + modf_ref[s, 3 * sub + 1:3 * sub + 2, :])
              + modf_ref[s, 3 * sub + 0:3 * sub + 1, :]).astype(BF16) for s in range(nseq)], axis=0)
        ab = jnp.dot(u, wab_ref[...], preferred_element_type=F32)
        pq = jnp.dot(u, wq_ref[...], preferred_element_type=F32)
        yield

        beta_all = jax.nn.sigmoid(ab)
        g = -jnp.exp(alog_ref[...]) * _softplus(ab + dtb_ref[...])
        rows = lax.broadcasted_iota(jnp.int32, (CHUNK, CHUNK), 0)
        cols = lax.broadcasted_iota(jnp.int32, (CHUNK, CHUNK), 1)
        tril = rows >= cols
        strict = rows > cols
        tril_f = tril.astype(F32)
        gc_parts, gl_parts = [], []
        for c in range(nchunk):
            gc_c = _mm_f32(tril_f, g[c * CHUNK:(c + 1) * CHUNK, :])
            gc_parts.append(gc_c)
            gl_parts.append(jnp.broadcast_to(gc_c[CHUNK - 1:CHUNK, :], (CHUNK, AB_PAD)))
        yield
        p = {}

        def project_rest(names):
            for name in names:
                lo, hi = off[name]
                p[name] = jnp.dot(u, wr_ref[:, lo:hi], preferred_element_type=F32)
                yield

        yield from project_rest(("sc_c", "sc_h"))
        gc = jnp.concatenate(gc_parts, axis=0)
        gl_rows = jnp.concatenate(gl_parts, axis=0)
        gc_t = gc.T
        for c in range(nchunk):
            gct_scr[c] = gc_t[0:SUBLANES, c * CHUNK:(c + 1) * CHUNK]

        acts = []
        for s in range(nseq):
            pq_s = pq[s * tt:(s + 1) * tt]
            acts.append(_silu(causal_conv(pq_s, cq_scr, s, cw_ref, n_conv)))
            cq_out_ref[s] = pq_s[tt - (n_conv - 1):]
        act = jnp.concatenate(acts, axis=0)

        eg_all = jnp.exp(gc)
        ekd_all = jnp.exp(gl_rows - gc)
        egl_all = jnp.exp(gl_rows)
        per_head = []
        for h, lanes in enumerate(head_lanes):
            q = _l2norm(act[:, h * DN_DK:(h + 1) * DN_DK]) * (DN_DK ** -0.5)
            kk = _l2norm(act[:, hk + h * DN_DK:hk + (h + 1) * DN_DK])
            v = act[:, 2 * hk + h * DN_DV:2 * hk + (h + 1) * DN_DV]
            beta = beta_all[:, DN_HEADS + h:DN_HEADS + h + 1]
            eg = eg_all[:, h:h + 1]
            kb = kk * beta
            per_head.append((kb.astype(BF16), q.astype(BF16), kk.astype(BF16), (v * beta).astype(BF16),
                             (kb * eg).astype(BF16)))
            qd_scr[:, lanes] = (q * eg).astype(BF16)
            kd_scr[:, lanes] = (kk * ekd_all[:, h:h + 1]).astype(BF16)
            for c in range(nchunk):
                gl_scr[c * SUBLANES:(c + 1) * SUBLANES, lanes] = jnp.broadcast_to(
                    egl_all[c * CHUNK:c * CHUNK + SUBLANES, h:h + 1], (SUBLANES, DN_DV))

        z = p["sc_c"] * p["sc_h"]
        zcs = []
        for s in range(nseq):
            z_s = z[s * tt:(s + 1) * tt]
            zcs.append(causal_conv(z_s, cm_scr, s, mw_ref, n_mconv))
            cm_out_ref[s] = z_s[tt - (n_mconv - 1):]

        where = [(c, h, lanes) for c in range(nchunk) for h, lanes in enumerate(head_lanes)]
        units = [tuple(arr[c * CHUNK:(c + 1) * CHUNK] for arr in per_head[h])
                 + (gc[c * CHUNK:(c + 1) * CHUNK, h:h + 1], gct_scr[c, h:h + 1, :])
                 for c, h, lanes in where]
        local = yield from _alternate(_gdn_local_steps(units, tril, strict), project_rest(("sc_b", "og")))
        osc_scr[...] = p["sc_b"] * jnp.concatenate(zcs, axis=0)
        og_scr[...] = p["og"]
        for (c, h, lanes), (u_c, w_c, qk_c) in zip(where, local):
            u_scr[c * CHUNK:(c + 1) * CHUNK, lanes] = u_c
            w_scr[c * CHUNK:(c + 1) * CHUNK, lanes] = w_c.astype(BF16)
            qk_scr[c * DN_HEADS + h] = qk_c.astype(BF16)

    def back():
        u_all, w_all, qd_all, kd_all = u_scr[...], w_scr[...], qd_scr[...], kd_scr[...]
        gl_all, og, o_sc = gl_scr[...], og_scr[...], osc_scr[...]
        qks = [qk_scr[i] for i in range(nchunk * DN_HEADS)]
        xs = [xb_ref[s] for s in range(nseq)]
        gates = [modb_ref[s, 3 * sub + 2:3 * sub + 3, :] for s in range(nseq)]
        states = [s_scr[i] * keep_back for i in range(nseq * DN_HEADS)]
        o_rows = [[None] * DN_HEADS for _ in range(nchunk)]
        for j in range(per_seq):
            chains = [(s, s * per_seq + j, h, lanes) for s in range(nseq) for h, lanes in enumerate(head_lanes)]
            rws = [slice(c * CHUNK, (c + 1) * CHUNK) for _, c, _, _ in chains]
            r = [jnp.dot(jnp.concatenate([w_all[rw, lanes], qd_all[rw, lanes]], axis=0),
                         states[s * DN_HEADS + h].astype(BF16), preferred_element_type=F32)
                 for (s, c, h, lanes), rw in zip(chains, rws)]
            yield
            v_new = [(u_all[rw, lanes] - r_u[:CHUNK]).astype(BF16)
                     for (s, c, h, lanes), rw, r_u in zip(chains, rws, r)]
            for (s, c, h, lanes), r_u, v_u in zip(chains, r, v_new):
                o_rows[c][h] = r_u[CHUNK:] + jnp.dot(qks[c * DN_HEADS + h], v_u, preferred_element_type=F32)
            for (s, c, h, lanes), rw, v_u in zip(chains, rws, v_new):
                i = s * DN_HEADS + h
                states[i] = (states[i] * gl_all[c * SUBLANES:c * SUBLANES + 1, lanes]
                             + lax.dot_general(kd_all[rw, lanes], v_u, (((0,), (0,)), ((), ())),
                                               preferred_element_type=F32))
            yield
        for i, st in enumerate(states):
            s_scr[i] = st
        for s in range(nseq):
            for h in range(DN_HEADS):
                s_out_ref[s, h] = states[s * DN_HEADS + h]

        o_dn = jnp.concatenate(
            [_gate_norm(jnp.concatenate([o_rows[c][h] for c in range(nchunk)], axis=0),
                        og[:, lanes], ng_ref[...]) for h, lanes in enumerate(head_lanes)], axis=1)
        dn_w = o_dn.shape[1]
        mix = (jnp.dot(o_dn.astype(BF16), w_out_ref[0:dn_w, :], preferred_element_type=F32)
               + jnp.dot(o_sc.astype(BF16), w_out_ref[dn_w:, :], preferred_element_type=F32))
        yield
        for s in range(nseq):
            y_ref[s] = _post(xs[s], mix[s * tt:(s + 1) * tt], gates[s],
                             lng_ref[sub:sub + 1, :], lnb_ref[sub:sub + 1, :], alpha)

    @pl.when(k == 0)
    def _():
        s_scr[...] = jnp.zeros_like(s_scr)
        cq_scr[...] = jnp.zeros_like(cq_scr)
        cm_scr[...] = jnp.zeros_like(cm_scr)
        _run_interleaved(front())

    @pl.when(jnp.logical_and(k > 0, k < n_tiles))
    def _():
        _run_interleaved(back(), front())

    @pl.when(k == n_tiles)
    def _():
        _run_interleaved(back())


def _mixer_prompt_pipe(x, mod, w_in, w_out, conv_w, mconv_w, alog, dtb, norm_g, lng, lnb, *, alpha, nseq, tt, off):
    bsz, t, d = x.shape
    assert t % tt == 0 and tt % LANES == 0 and bsz % nseq == 0
    nt = t // tt
    n_tiles = (bsz // nseq) * nt
    rows = nseq * tt
    qkv_dim = conv_w.shape[-1]
    sc_w = off["sc_c"][1] - off["sc_c"][0]
    dn_w = DN_HEADS * DN_DV
    n_conv, n_mconv = conv_w.shape[0], mconv_w.shape[0]
    assert max(n_conv, n_mconv) - 1 <= SUBLANES <= tt
    nchunk = rows // CHUNK
    kern = functools.partial(_mixer_prompt_pipe_kernel, alpha=alpha, nseq=nseq, tt=tt, tiles_per_group=nt,
                             n_tiles=n_tiles, off=off, n_conv=n_conv, n_mconv=n_mconv)

    def front_tile(k):
        return jnp.minimum(k, n_tiles - 1)

    def back_tile(k):
        return jnp.maximum(k - 1, 0)

    y, s_out, cq, cm = pl.pallas_call(
        kern,
        grid=(n_tiles + 1,),
        in_specs=[pl.BlockSpec((nseq, tt, d), lambda k: (front_tile(k) // nt, front_tile(k) % nt, 0)),
                  pl.BlockSpec((nseq, 3 * N_SUB, d), lambda k: (front_tile(k) // nt, 0, 0)),
                  pl.BlockSpec((nseq, tt, d), lambda k: (back_tile(k) // nt, back_tile(k) % nt, 0)),
                  pl.BlockSpec((nseq, 3 * N_SUB, d), lambda k: (back_tile(k) // nt, 0, 0)),
                  *[_const_spec(w.shape) for w in w_in], _const_spec(w_out.shape),
                  _const_spec(conv_w.shape), _const_spec(mconv_w.shape),
                  _const_spec(alog.shape), _const_spec(dtb.shape), _const_spec(norm_g.shape),
                  _const_spec(lng.shape), _const_spec(lnb.shape)],
        out_specs=[pl.BlockSpec((nseq, tt, d), lambda k: (back_tile(k) // nt, back_tile(k) % nt, 0)),
                   pl.BlockSpec((nseq, DN_HEADS, DN_DK, DN_DV), lambda k: (back_tile(k) // nt, 0, 0, 0)),
                   pl.BlockSpec((nseq, n_conv - 1, qkv_dim), lambda k: (front_tile(k) // nt, 0, 0)),
                   pl.BlockSpec((nseq, n_mconv - 1, sc_w), lambda k: (front_tile(k) // nt, 0, 0))],
        out_shape=[jax.ShapeDtypeStruct((bsz, t, d), F32),
                   jax.ShapeDtypeStruct((bsz, DN_HEADS, DN_DK, DN_DV), F32),
                   jax.ShapeDtypeStruct((bsz, n_conv - 1, qkv_dim), F32),
                   jax.ShapeDtypeStruct((bsz, n_mconv - 1, sc_w), F32)],
        scratch_shapes=[pltpu.VMEM((nseq * DN_HEADS, DN_DK, DN_DV), F32),
                        pltpu.VMEM((nseq, SUBLANES, qkv_dim), F32),
                        pltpu.VMEM((nseq, SUBLANES, sc_w), F32),
                        pltpu.VMEM((nchunk, SUBLANES, CHUNK), F32),
                        pltpu.VMEM((rows, dn_w), F32),
                        pltpu.VMEM((rows, dn_w), BF16),
                        pltpu.VMEM((nchunk * DN_HEADS, CHUNK, CHUNK), BF16),
                        pltpu.VMEM((rows, dn_w), BF16),
                        pltpu.VMEM((rows, dn_w), BF16),
                        pltpu.VMEM((nchunk * SUBLANES, dn_w), F32),
                        pltpu.VMEM((rows, dn_w), F32),
                        pltpu.VMEM((rows, sc_w), F32)],
        compiler_params=pltpu.CompilerParams(dimension_semantics=("arbitrary",),
                                             vmem_limit_bytes=V7X_VMEM_LIMIT),
        name="mixer_prompt",
    )(x, mod, x, mod, *w_in, w_out, conv_w, mconv_w, alog, dtb, norm_g, lng, lnb)
    return y, s_out, cq, cm


_PACK = SUBLANES


def _sample_phase_a(x_ref, mod_ref, wq_ref, wr_ref, wab_ref, cw_ref, mw_ref, alog_ref, dtb_ref, cq_in_ref, cm_in_ref,
                    cq_out_ref, cm_out_ref, og_scr, osc_scr, lhs_scr, u_scr, kd_scr, gl_scr, qk_scr,
                    *, t_len, d, off, n_conv, n_mconv):
    bsz = x_ref.shape[0]
    hk = DN_HEADS * DN_DK
    qkv_dim = cw_ref.shape[-1]
    sc_w = off["sc_c"][1] - off["sc_c"][0]

    x = _stack_time(x_ref, t_len, d)
    u = (x * (1.0 + _mod_rows(mod_ref, 1, t_len, d)) + _mod_rows(mod_ref, 0, t_len, d)).astype(BF16)
    p = _project(u, wq_ref, wr_ref, wab_ref, off)

    def rows(arr, t):
        return arr[t * bsz:(t + 1) * bsz]

    pq = _cols(p, off, "qkv")
    seq = ([cq_in_ref[:, j * qkv_dim:(j + 1) * qkv_dim] for j in range(n_conv - 1)]
           + [rows(pq, t) for t in range(t_len)])
    act = []
    for t in range(t_len):
        acc = None
        for j in range(n_conv):
            term = seq[t + j] * cw_ref[j:j + 1, :]
            acc = term if acc is None else acc + term
        act.append(_silu(acc))
    for j in range(n_conv - 1):
        cq_out_ref[:, j * qkv_dim:(j + 1) * qkv_dim] = seq[t_len + j]

    z = _cols(p, off, "sc_c") * _cols(p, off, "sc_h")
    zseq = ([cm_in_ref[:, j * sc_w:(j + 1) * sc_w] for j in range(n_mconv - 1)]
            + [rows(z, t) for t in range(t_len)])
    zc = []
    for t in range(t_len):
        acc = None
        for j in range(n_mconv):
            term = zseq[t + j] * mw_ref[j:j + 1, :]
            acc = term if acc is None else acc + term
        zc.append(acc)
    for j in range(n_mconv - 1):
        cm_out_ref[:, j * sc_w:(j + 1) * sc_w] = zseq[t_len + j]
    osc_scr[...] = _cols(p, off, "sc_b") * jnp.concatenate(zc, axis=0)
    og_scr[...] = _cols(p, off, "og")

    ab = _cols(p, off, "ab")
    beta_all = jax.nn.sigmoid(ab)
    g_all = -jnp.exp(alog_ref[...]) * _softplus(ab + dtb_ref[...])
    zero = jnp.zeros((bsz, LANES), F32)

    for h in range(DN_HEADS):
        q = [_l2norm(a[:, h * DN_DK:(h + 1) * DN_DK]) * (DN_DK ** -0.5) for a in act]
        k = [_l2norm(a[:, hk + h * DN_DK:hk + (h + 1) * DN_DK]) for a in act]
        v = [a[:, 2 * hk + h * DN_DV:2 * hk + (h + 1) * DN_DV] for a in act]
        beta = [rows(beta_all, t)[:, DN_HEADS + h:DN_HEADS + h + 1] for t in range(t_len)]
        g = [rows(g_all, t)[:, h:h + 1] for t in range(t_len)]
        gc = [g[0]]
        for t in range(1, t_len):
            gc.append(gc[-1] + g[t])
        kb = [k[t] * beta[t] for t in range(t_len)]
        vb = [v[t] * beta[t] for t in range(t_len)]
        eg = [jnp.exp(gc[t]) for t in range(t_len)]
        kbe = [kb[t] * eg[t] for t in range(t_len)]
        lm = [[None] * t_len for _ in range(t_len)]
        n_qk = 0
        for i in range(t_len):
            for j in range(i + 1):
                dec = jnp.exp(gc[i] - gc[j])
                qk_scr[h, n_qk] = jnp.broadcast_to(_rowsum(q[i] * k[j]) * dec, (bsz, LANES))
                n_qk += 1
                if j < i:
                    lm[i][j] = _rowsum(kb[i] * k[j]) * dec
        tm = [[None] * t_len for _ in range(t_len)]
        for i in range(t_len):
            for j in range(i):
                acc = lm[i][j]
                for m in range(j + 1, i):
                    acc = acc + lm[i][m] * tm[m][j]
                tm[i][j] = -acc
        for i in range(t_len):
            ui, wi = vb[i], kbe[i]
            for j in range(i):
                ui = ui + tm[i][j] * vb[j]
                wi = wi + tm[i][j] * kbe[j]
            lhs_scr[h, pl.ds(i, bsz, stride=_PACK), :] = wi
            lhs_scr[h, pl.ds(t_len + i, bsz, stride=_PACK), :] = q[i] * eg[i]
            u_scr[h, pl.ds(i, bsz, stride=_PACK), :] = ui
            u_scr[h, pl.ds(t_len + i, bsz, stride=_PACK), :] = zero
            kd_scr[h, pl.ds(i, bsz, stride=_PACK), :] = k[i] * jnp.exp(gc[t_len - 1] - gc[i])
            kd_scr[h, pl.ds(t_len + i, bsz, stride=_PACK), :] = zero
        gl_scr[h] = jnp.broadcast_to(jnp.exp(gc[t_len - 1]), (bsz, LANES))


def _sample_phase_c(x_ref, mod_ref, w_out_ref, ng_ref, lng_ref, lnb_ref, y_ref,
                    og_scr, osc_scr, res_scr, vn_scr, qk_scr, *, alpha, t_len, d):
    bsz = x_ref.shape[0]
    sub = 1
    x = _stack_time(x_ref, t_len, d)
    gate = _mod_rows(mod_ref, 2, t_len, d)
    og = og_scr[...]
    per_t = []
    for i in range(t_len):
        heads = []
        for h in range(DN_HEADS):
            oi = res_scr[h, pl.ds(t_len + i, bsz, stride=_PACK), :]
            base = i * (i + 1) // 2
            for j in range(i + 1):
                oi = oi + qk_scr[h, base + j] * vn_scr[h, pl.ds(j, bsz, stride=_PACK), :]
            heads.append(_gate_norm(oi, og[i * bsz:(i + 1) * bsz, h * DN_DV:(h + 1) * DN_DV], ng_ref[...]))
        per_t.append(jnp.concatenate(heads, axis=1))
    o_dn = jnp.concatenate(per_t, axis=0)
    y = _mix_out(x, o_dn, osc_scr[...], gate, w_out_ref,
                 lng_ref[sub:sub + 1, :], lnb_ref[sub:sub + 1, :], alpha)
    for t in range(t_len):
        y_ref[:, t * d:(t + 1) * d] = y[t * bsz:(t + 1) * bsz]


def _mixer_sample_kernel(x_ref, mod_ref, wq_ref, wr_ref, wab_ref, w_out_ref, cw_ref, mw_ref, alog_ref, dtb_ref, ng_ref,
                         lng_ref, lnb_ref, cq_in_ref, cm_in_ref, s_in_ref,
                         y_ref, cq_out_ref, cm_out_ref, s_out_ref,
                         og_scr, osc_scr, lhs_scr, u_scr, kd_scr, gl_scr, qk_scr, res_scr, vn_scr,
                         *, alpha, t_len, d, off, n_conv, n_mconv, bt):
    step = pl.program_id(0)
    assert 2 * t_len == _PACK, "packed buffers hold [T rows | T rows] per sequence"

    @pl.when(step == 0)
    def _():
        _sample_phase_a(x_ref, mod_ref, wq_ref, wr_ref, wab_ref, cw_ref, mw_ref, alog_ref, dtb_ref, cq_in_ref, cm_in_ref,
                        cq_out_ref, cm_out_ref, og_scr, osc_scr, lhs_scr, u_scr, kd_scr, gl_scr, qk_scr,
                        t_len=t_len, d=d, off=off, n_conv=n_conv, n_mconv=n_mconv)

    top = (lax.broadcasted_iota(jnp.int32, (_PACK, LANES), 0) < t_len).astype(F32)

    chains = [(bl, h) for bl in range(bt) for h in range(DN_HEADS)]
    seq0 = step * bt
    row0 = [pl.multiple_of((seq0 + bl) * _PACK, _PACK) for bl in range(bt)]
    states = [s_in_ref[bl, h] for bl, h in chains]
    r = [_mm(lhs_scr[h, pl.ds(row0[bl], _PACK), :], s) for (bl, h), s in zip(chains, states)]
    vns = []
    for (bl, h), r_u in zip(chains, r):
        res_scr[h, pl.ds(row0[bl], _PACK), :] = r_u
        vn = (u_scr[h, pl.ds(row0[bl], _PACK), :] - r_u) * top
        vn_scr[h, pl.ds(row0[bl], _PACK), :] = vn
        vns.append(vn)
    upd = [_mm_at(kd_scr[h, pl.ds(row0[bl], _PACK), :], vn) for (bl, h), vn in zip(chains, vns)]
    for (bl, h), s, s_add in zip(chains, states, upd):
        s_out_ref[bl, h] = s * gl_scr[h, pl.ds(seq0 + bl, 1), :] + s_add

    @pl.when(step == pl.num_programs(0) - 1)
    def _():
        _sample_phase_c(x_ref, mod_ref, w_out_ref, ng_ref, lng_ref, lnb_ref, y_ref,
                        og_scr, osc_scr, res_scr, vn_scr, qk_scr, alpha=alpha, t_len=t_len, d=d)


def _mixer_sample(x, mod, w_in, w_out, conv_w, mconv_w, alog, dtb, norm_g, lng, lnb, s_in, cq_in, cm_in,
                  *, alpha, off, bt):
    bsz, t, d = x.shape
    assert bsz % bt == 0
    qkv_dim = conv_w.shape[-1]
    sc_w = off["sc_c"][1] - off["sc_c"][0]
    dn_w = DN_HEADS * DN_DV
    n_conv, n_mconv = conv_w.shape[0], mconv_w.shape[0]
    n_qk = t * (t + 1) // 2
    x2 = x.reshape(bsz, t * d)
    cq2 = cq_in.reshape(bsz, (n_conv - 1) * qkv_dim)
    cm2 = cm_in.reshape(bsz, (n_mconv - 1) * sc_w)
    kern = functools.partial(_mixer_sample_kernel, alpha=alpha, t_len=t, d=d, off=off,
                             n_conv=n_conv, n_mconv=n_mconv, bt=bt)
    consts = (*w_in, w_out, conv_w, mconv_w, alog, dtb, norm_g, lng, lnb, cq2, cm2)
    state_spec = pl.BlockSpec((bt, DN_HEADS, DN_DK, DN_DV), lambda i: (i, 0, 0, 0))
    packed = pltpu.VMEM((DN_HEADS, _PACK * bsz, LANES), F32)
    y, cq, cm, s_out = pl.pallas_call(
        kern,
        grid=(bsz // bt,),
        in_specs=([_const_spec(x2.shape), _sub_mod_spec(bsz, d, 1)]
                  + [_const_spec(a.shape) for a in consts] + [state_spec]),
        out_specs=[pl.BlockSpec(x2.shape, lambda i: (0, 0)),
                   pl.BlockSpec(cq2.shape, lambda i: (0, 0)),
                   pl.BlockSpec(cm2.shape, lambda i: (0, 0)),
                   state_spec],
        out_shape=[jax.ShapeDtypeStruct(x2.shape, F32),
                   jax.ShapeDtypeStruct(cq2.shape, F32),
                   jax.ShapeDtypeStruct(cm2.shape, F32),
                   jax.ShapeDtypeStruct(s_in.shape, F32)],
        scratch_shapes=[pltpu.VMEM((t * bsz, dn_w), F32),
                        pltpu.VMEM((t * bsz, sc_w), F32),
                        packed,
                        packed,
                        packed,
                        pltpu.VMEM((DN_HEADS, bsz, LANES), F32),
                        pltpu.VMEM((DN_HEADS, n_qk, bsz, LANES), F32),
                        packed,
                        packed],
        compiler_params=pltpu.CompilerParams(dimension_semantics=("arbitrary",),
                                             vmem_limit_bytes=V7X_VMEM_LIMIT),
        name="mixer_sample",
    )(x2, mod, *consts, s_in)
    return (y.reshape(bsz, t, d), s_out, cq.reshape(bsz, n_conv - 1, qkv_dim),
            cm.reshape(bsz, n_mconv - 1, sc_w))


def _split_w_in(w_in, qkv_dim):
    c1 = qkv_dim + 2 * DN_HEADS
    wab = jnp.pad(w_in[:, qkv_dim:c1], ((0, 0), (0, AB_PAD - 2 * DN_HEADS)))
    return w_in[:, :qkv_dim].astype(BF16), w_in[:, c1:].astype(BF16), wab.astype(BF16)


def _pad_lanes(v):
    return jnp.zeros((1, AB_PAD), F32).at[0, :v.shape[0]].set(v)


def kernel(x_prompt, x_sample, state_ssm, state_conv_qkv, state_conv_mix, c_prompt, c_sample, w_ada, b_ada, ln_g, ln_b, ffn1_wg, ffn1_wu, ffn1_wd, ffn2_wg, ffn2_wu, ffn2_wd, w_in, conv_qkv_w, a_log, dt_bias, dn_norm_g, conv_mix_w, w_out):
    depth = w_ada.shape[0]
    alpha = (2 * depth) ** 0.25
    bp, tp, d = x_prompt.shape
    qkv_dim = conv_qkv_w.shape[-1]
    sc_width = conv_mix_w.shape[-1]
    dn_width = DN_HEADS * DN_DV
    off = _proj_layout(dn_width, sc_width)
    tm = min(512, tp)
    nseq = 2 if bp % 2 == 0 else 1
    tt = min(512 // nseq, tp)
    bt = min(8, x_sample.shape[0])

    hp, hs = x_prompt, x_sample
    outs = [[] for _ in range(6)]
    for l in range(depth):
        mod_p, mod_s = _adaln(c_prompt, c_sample, w_ada[l], b_ada[l].reshape(1, -1), tn=3 * d)
        mod_p = mod_p.reshape(bp, 3 * N_SUB, d)
        lng, lnb = ln_g[l], ln_b[l]
        w1 = (ffn1_wg[l], ffn1_wu[l], ffn1_wd[l])
        w2 = (ffn2_wg[l], ffn2_wu[l], ffn2_wd[l])
        w_in_p = _split_w_in(w_in[l], qkv_dim)
        w_out_b = w_out[l].astype(BF16)
        alog, dtb = _pad_lanes(a_log[l]), _pad_lanes(dt_bias[l])
        norm_g = dn_norm_g[l].reshape(1, -1)
        mix_args = (w_in_p, w_out_b, conv_qkv_w[l], conv_mix_w[l], alog, dtb, norm_g, lng, lnb)

        hp, hs = _ffn(hp, hs, mod_p, mod_s, *w1, lng, lnb, sub=0, alpha=alpha, tm=tm)
        hp, a1, a2, a3 = _mixer_prompt_pipe(hp, mod_p, *mix_args, alpha=alpha, nseq=nseq, tt=tt, off=off)
        hs, b1, b2, b3 = _mixer_sample(hs, mod_s, *mix_args, state_ssm[l], state_conv_qkv[l],
                                       state_conv_mix[l], alpha=alpha, off=off, bt=bt)
        hp, hs = _ffn(hp, hs, mod_p, mod_s, *w2, lng, lnb, sub=2, alpha=alpha, tm=tm)
        for lst, val in zip(outs, (a1, a2, a3, b1, b2, b3)):
            lst.append(val)
    return (hp, hs) + tuple(jnp.stack(o) for o in outs)
```

```python
import functools

import jax
import jax.numpy as jnp
from jax import lax
from jax.experimental import pallas as pl
from jax.experimental.pallas import tpu as pltpu

F32 = jnp.float32
BF16 = jnp.bfloat16

LN_EPS = 1e-5
RMS_EPS = 1e-6
N_SUB = 3
DN_HEADS = 4
DN_DK = 128
DN_DV = 128
CHUNK = 64
LANES = 128
SUBLANES = 8
AB_PAD = LANES

V7X_VMEM_LIMIT = 60000 * 1024
CAST_BUFFERS = 4


def _mm(a, b):
    return jnp.dot(a.astype(BF16), b.astype(BF16), preferred_element_type=F32)


def _mm_bt(a, b):
    return lax.dot_general(a.astype(BF16), b.astype(BF16), (((1,), (1,)), ((), ())),
                           preferred_element_type=F32)


def _mm_at(a, b):
    return lax.dot_general(a.astype(BF16), b.astype(BF16), (((0,), (0,)), ((), ())),
                           preferred_element_type=F32)


def _mm_f32(a, b):
    return jnp.dot(a, b, preferred_element_type=F32, precision=lax.Precision.HIGHEST)


def _silu(x):
    return x * jax.nn.sigmoid(x)


def _softplus(x):
    return jnp.maximum(x, 0.0) + jnp.log1p(jnp.exp(-jnp.abs(x)))


def _layer_norm(y, g, b):
    mu = jnp.mean(y, axis=-1, keepdims=True)
    yc = y - mu
    var = jnp.mean(yc * yc, axis=-1, keepdims=True)
    return yc * lax.rsqrt(var + LN_EPS) * g + b


def _post(x, delta, gate, g, b, alpha):
    return _layer_norm(alpha * x + gate * delta, g, b)


def _rowsum(x):
    return jnp.sum(x, axis=-1, keepdims=True)


def _l2norm(x):
    return x * lax.rsqrt(_rowsum(x * x) + RMS_EPS)


def _const_spec(shape):
    nd = len(shape)
    return pl.BlockSpec(shape, lambda *_: (0,) * nd, pipeline_mode=pl.Buffered(1))


def _adaln_kernel(cp_ref, cs_ref, w_ref, b_ref, win_ref, op_ref, os_ref, wq_ref, wr_ref, wab_ref, *, qkv_dim):
    mp = cp_ref.shape[0]
    c = jnp.concatenate([cp_ref[...], cs_ref[...]], axis=0)
    mod = _mm(_silu(c), w_ref[...]) + b_ref[...]
    op_ref[...] = mod[:mp]
    os_ref[...] = mod[mp:]

    w_in = win_ref[...]
    gates = 2 * DN_HEADS
    wq_ref[...] = w_in[:, :qkv_dim].astype(BF16)
    wr_ref[...] = w_in[:, qkv_dim + gates:].astype(BF16)
    ab_tile = w_in[:, qkv_dim:qkv_dim + AB_PAD]
    lane = lax.broadcasted_iota(jnp.int32, ab_tile.shape, 1)
    wab_ref[...] = jnp.where(lane < gates, ab_tile, 0.0).astype(BF16)


def _adaln(c_p, c_s, w, b, w_in, qkv_dim):
    (mp, d), ms = c_p.shape, c_s.shape[0]
    n = w.shape[1]
    proj = w_in.shape[1]
    rest = proj - qkv_dim - 2 * DN_HEADS
    assert qkv_dim % LANES == 0 and rest % LANES == 0 and proj >= qkv_dim + AB_PAD
    steps = max(g for g in (1, 2, 4) if n % (g * LANES) == 0 and d % (g * 2 * SUBLANES) == 0)
    tn, rows = n // steps, d // steps
    return pl.pallas_call(
        functools.partial(_adaln_kernel, qkv_dim=qkv_dim),
        grid=(steps,),
        in_specs=[_const_spec((mp, d)), _const_spec((ms, d)),
                  pl.BlockSpec((d, tn), lambda j: (0, j)),
                  pl.BlockSpec((1, tn), lambda j: (0, j)),
                  pl.BlockSpec((rows, proj), lambda j: (j, 0))],
        out_specs=[pl.BlockSpec((mp, tn), lambda j: (0, j)), pl.BlockSpec((ms, tn), lambda j: (0, j)),
                   pl.BlockSpec((rows, qkv_dim), lambda j: (j, 0)),
                   pl.BlockSpec((rows, rest), lambda j: (j, 0)),
                   pl.BlockSpec((rows, AB_PAD), lambda j: (j, 0))],
        out_shape=[jax.ShapeDtypeStruct((mp, n), F32), jax.ShapeDtypeStruct((ms, n), F32),
                   jax.ShapeDtypeStruct((d, qkv_dim), BF16), jax.ShapeDtypeStruct((d, rest), BF16),
                   jax.ShapeDtypeStruct((d, AB_PAD), BF16)],
        compiler_params=pltpu.CompilerParams(dimension_semantics=("arbitrary",),
                                             vmem_limit_bytes=V7X_VMEM_LIMIT),
        name="adaln",
    )(c_p, c_s, w, b, w_in)


def _ffn_rows(x, shift, scale, gate, wg_ref, wu_ref, wd_ref, g, b, alpha):
    u = (x * (1.0 + scale) + shift).astype(BF16)
    hg = jnp.dot(u, wg_ref[...], preferred_element_type=F32)
    hu = jnp.dot(u, wu_ref[...], preferred_element_type=F32)
    h = (_silu(hg) * hu).astype(BF16)
    d = jnp.dot(h, wd_ref[...], preferred_element_type=F32)
    return _post(x, 0.5 * d, gate, g, b, alpha)


def _ffn_prompt_kernel(x_ref, mod_ref, wg_ref, wu_ref, wd_ref, lng_ref, lnb_ref, o_ref, *, sub, alpha, parts):
    shift = mod_ref[0, 3 * sub + 0:3 * sub + 1, :]
    scale = mod_ref[0, 3 * sub + 1:3 * sub + 2, :]
    gate = mod_ref[0, 3 * sub + 2:3 * sub + 3, :]
    rows = x_ref.shape[0] // parts
    for part in range(parts):
        sl = slice(part * rows, (part + 1) * rows)
        o_ref[sl, :] = _ffn_rows(x_ref[sl, :], shift, scale, gate, wg_ref, wu_ref, wd_ref,
                                 lng_ref[sub:sub + 1, :], lnb_ref[sub:sub + 1, :], alpha)


def _ffn_prompt(x, mod, wg, wu, wd, lng, lnb, *, sub, alpha, tm, parts):
    bsz, t, d = x.shape
    f = wg.shape[1]
    assert t % tm == 0 and tm % (parts * SUBLANES) == 0
    per_seq = t // tm
    x2 = x.reshape(bsz * t, d)
    out = pl.pallas_call(
        functools.partial(_ffn_prompt_kernel, sub=sub, alpha=alpha, parts=parts),
        grid=(bsz * per_seq,),
        in_specs=[pl.BlockSpec((tm, d), lambda i: (i, 0)),
                  pl.BlockSpec((1, 3 * N_SUB, d), lambda i: (i // per_seq, 0, 0)),
                  _const_spec((d, f)), _const_spec((d, f)), _const_spec((f, d)),
                  _const_spec(lng.shape), _const_spec(lnb.shape)],
        out_specs=pl.BlockSpec((tm, d), lambda i: (i, 0)),
        out_shape=jax.ShapeDtypeStruct((bsz * t, d), F32),
        compiler_params=pltpu.CompilerParams(dimension_semantics=("arbitrary",),
                                             vmem_limit_bytes=V7X_VMEM_LIMIT),
        name=f"ffn_prompt{sub}",
    )(x2, mod, wg, wu, wd, lng, lnb)
    return out.reshape(bsz, t, d)


def _stack_time(ref, t_len, d):
    return jnp.concatenate([ref[:, t * d:(t + 1) * d] for t in range(t_len)], axis=0)


def _mod_rows(mod_ref, idx, t_len, d):
    m = mod_ref[:, idx * d:(idx + 1) * d]
    return jnp.concatenate([m] * t_len, axis=0)


def _ffn_sample_kernel(x_ref, mod_ref, wg_ref, wu_ref, wd_ref, lng_ref, lnb_ref, o_ref, *,
                       sub, alpha, t_len, d):
    bsz = x_ref.shape[0]
    x = _stack_time(x_ref, t_len, d)
    y = _ffn_rows(x, _mod_rows(mod_ref, 0, t_len, d), _mod_rows(mod_ref, 1, t_len, d),
                  _mod_rows(mod_ref, 2, t_len, d), wg_ref, wu_ref, wd_ref,
                  lng_ref[sub:sub + 1, :], lnb_ref[sub:sub + 1, :], alpha)
    for t in range(t_len):
        o_ref[:, t * d:(t + 1) * d] = y[t * bsz:(t + 1) * bsz]


def _sub_mod_spec(bsz, d, sub):
    return pl.BlockSpec((bsz, 3 * d), lambda *_: (0, sub), pipeline_mode=pl.Buffered(1))


def _cast_weights(jobs):
    chunks = []
    used = {}
    for w_hbm, w_scr, stage, sem in jobs:
        nbuf, rows = stage.shape[0], stage.shape[1]
        for i in range(w_hbm.shape[0] // rows):
            slot = used.get(id(stage), 0) % nbuf
            used[id(stage)] = used.get(id(stage), 0) + 1
            copy = pltpu.make_async_copy(w_hbm.at[pl.ds(i * rows, rows), :], stage.at[slot], sem.at[slot])
            chunks.append((copy, w_scr, stage, slot, i * rows, rows))
    ahead = min(job[2].shape[0] for job in jobs) - 1
    for copy, *_ in chunks[:ahead]:
        copy.start()
    for j, (copy, w_scr, stage, slot, row0, rows) in enumerate(chunks):
        if j + ahead < len(chunks):
            chunks[j + ahead][0].start()
        copy.wait()
        w_scr[row0:row0 + rows, :] = stage[slot].astype(BF16)


def _swiglu(x, shift, scale, wg_ref, wu_ref, wd_ref):
    u = (x * (1.0 + scale) + shift).astype(BF16)
    hg = jnp.dot(u, wg_ref[...], preferred_element_type=F32)
    hu = jnp.dot(u, wu_ref[...], preferred_element_type=F32)
    h = (_silu(hg) * hu).astype(BF16)
    return jnp.dot(h, wd_ref[...], preferred_element_type=F32)


def _ffn_kernel(xp_ref, modp_ref, xq_ref, modq_ref, xs_ref, mods_ref, wg_hbm, wu_hbm, wd_hbm, lng_ref, lnb_ref,
                yp_ref, ys_ref, wg_scr, wu_scr, wd_scr, d_scr, stage_in, stage_out, sem_in, sem_out,
                *, sub, alpha, n_prompt, t_len, d):
    k = pl.program_id(0)
    g, b = lng_ref[sub:sub + 1, :], lnb_ref[sub:sub + 1, :]

    def matmuls():
        return _swiglu(xp_ref[...], modp_ref[0, 3 * sub + 0:3 * sub + 1, :], modp_ref[0, 3 * sub + 1:3 * sub + 2, :],
                       wg_scr, wu_scr, wd_scr)

    def finish(d_prev):
        yp_ref[...] = _post(xq_ref[...], 0.5 * d_prev, modq_ref[0, 3 * sub + 2:3 * sub + 3, :], g, b, alpha)

    @pl.when(k == 0)
    def _():
        _cast_weights([(wg_hbm, wg_scr, stage_in, sem_in), (wu_hbm, wu_scr, stage_in, sem_in),
                       (wd_hbm, wd_scr, stage_out, sem_out)])
        d_scr[...] = matmuls()

    @pl.when(jnp.logical_and(k > 0, k < n_prompt))
    def _():
        d_prev = d_scr[...]
        d_new = matmuls()
        finish(d_prev)
        d_scr[...] = d_new

    @pl.when(k == n_prompt)
    def _():
        finish(d_scr[...])
        bsz = xs_ref.shape[0]
        x = _stack_time(xs_ref, t_len, d)
        y = _ffn_rows(x, _mod_rows(mods_ref, 0, t_len, d), _mod_rows(mods_ref, 1, t_len, d),
                      _mod_rows(mods_ref, 2, t_len, d), wg_scr, wu_scr, wd_scr, g, b, alpha)
        for t in range(t_len):
            ys_ref[:, t * d:(t + 1) * d] = y[t * bsz:(t + 1) * bsz]


def _ffn(xp, xs, mod_p, mod_s, wg, wu, wd, lng, lnb, *, sub, alpha, tm):
    bp, tp, d = xp.shape
    bs, ts, _ = xs.shape
    f = wg.shape[1]
    assert tp % tm == 0
    per_seq = tp // tm
    n_prompt = bp * per_seq
    in_rows = max(r for r in (16, 32, 64, 128) if d % r == 0)
    out_rows = max(r for r in (16, 32, 64, 128, 176, 352) if f % r == 0)
    last = n_prompt - 1
    yp, ys = pl.pallas_call(
        functools.partial(_ffn_kernel, sub=sub, alpha=alpha, n_prompt=n_prompt, t_len=ts, d=d),
        grid=(n_prompt + 1,),
        in_specs=[pl.BlockSpec((tm, d), lambda k: (jnp.minimum(k, last), 0)),
                  pl.BlockSpec((1, 3 * N_SUB, d), lambda k: (jnp.minimum(k, last) // per_seq, 0, 0)),
                  pl.BlockSpec((tm, d), lambda k: (jnp.maximum(k - 1, 0), 0)),
                  pl.BlockSpec((1, 3 * N_SUB, d), lambda k: (jnp.maximum(k - 1, 0) // per_seq, 0, 0)),
                  _const_spec((bs, ts * d)), _sub_mod_spec(bs, d, sub),
                  pl.BlockSpec(memory_space=pl.ANY), pl.BlockSpec(memory_space=pl.ANY),
                  pl.BlockSpec(memory_space=pl.ANY),
                  _const_spec(lng.shape), _const_spec(lnb.shape)],
        out_specs=[pl.BlockSpec((tm, d), lambda k: (jnp.maximum(k - 1, 0), 0)),
                   pl.BlockSpec((bs, ts * d), lambda k: (0, 0))],
        out_shape=[jax.ShapeDtypeStruct((bp * tp, d), F32),
                   jax.ShapeDtypeStruct((bs, ts * d), F32)],
        scratch_shapes=[pltpu.VMEM((d, f), BF16), pltpu.VMEM((d, f), BF16), pltpu.VMEM((f, d), BF16),
                        pltpu.VMEM((tm, d), F32),
                        pltpu.VMEM((CAST_BUFFERS, in_rows, f), F32), pltpu.VMEM((CAST_BUFFERS, out_rows, d), F32),
                        pltpu.SemaphoreType.DMA((CAST_BUFFERS,)), pltpu.SemaphoreType.DMA((CAST_BUFFERS,))],
        compiler_params=pltpu.CompilerParams(dimension_semantics=("arbitrary",),
                                             vmem_limit_bytes=V7X_VMEM_LIMIT),
        name=f"ffn{sub}",
    )(xp.reshape(bp * tp, d), mod_p, xp.reshape(bp * tp, d), mod_p, xs.reshape(bs, ts * d), mod_s,
      wg, wu, wd, lng, lnb)
    return yp.reshape(bp, tp, d), ys.reshape(bs, ts, d)


def _ffn_sample(x, mod, wg, wu, wd, lng, lnb, *, sub, alpha):
    bsz, t, d = x.shape
    f = wg.shape[1]
    x2 = x.reshape(bsz, t * d)
    out = pl.pallas_call(
        functools.partial(_ffn_sample_kernel, sub=sub, alpha=alpha, t_len=t, d=d),
        grid=(1,),
        in_specs=[_const_spec((bsz, t * d)), _sub_mod_spec(bsz, d, sub),
                  _const_spec((d, f)), _const_spec((d, f)), _const_spec((f, d)),
                  _const_spec(lng.shape), _const_spec(lnb.shape)],
        out_specs=pl.BlockSpec((bsz, t * d), lambda i: (0, 0)),
        out_shape=jax.ShapeDtypeStruct((bsz, t * d), F32),
        compiler_params=pltpu.CompilerParams(dimension_semantics=("arbitrary",),
                                             vmem_limit_bytes=V7X_VMEM_LIMIT),
        name=f"ffn_sample{sub}",
    )(x2, mod, wg, wu, wd, lng, lnb)
    return out.reshape(bsz, t, d)


def _tri_inverse(lmats, c):
    ls = [l.astype(BF16) for l in lmats]
    ms = [jnp.dot(l, l, preferred_element_type=F32) for l in ls]
    qs = ms
    power = 2
    while 2 * power < c:
        mb = [m.astype(BF16) for m in ms]
        ms = [jnp.dot(m, m, preferred_element_type=F32) for m in mb]
        qm = [jnp.dot(q.astype(BF16), m.astype(BF16), preferred_element_type=F32) for q, m in zip(qs, ms)]
        qs = [q + m + x for q, m, x in zip(qs, ms, qm)]
        power *= 2
    rows = lax.broadcasted_iota(jnp.int32, (c, c), 0)
    cols = lax.broadcasted_iota(jnp.int32, (c, c), 1)
    eye = (rows == cols).astype(F32)
    lq = [jnp.dot(l, q.astype(BF16), preferred_element_type=F32) for l, q in zip(ls, qs)]
    return [eye - l + q - x for l, q, x in zip(lmats, qs, lq)]


def _gdn_chunks_local(units, tril, strict):
    c = units[0][1].shape[0]
    a = [lax.dot_general(jnp.concatenate([kb, q], axis=0), k, (((1,), (1,)), ((), ())),
                         preferred_element_type=F32)
         for kb, q, k, _, _, _, _ in units]
    lmats, qks = [], []
    for a_u, (_, _, _, _, _, gcol, grow) in zip(a, units):
        decay = jnp.where(tril, jnp.exp(jnp.where(tril, gcol - grow, 0.0)), 0.0)
        lmats.append(jnp.where(strict, a_u[:c] * decay, 0.0))
        qks.append(jnp.where(tril, a_u[c:] * decay, 0.0))
    tms = _tri_inverse(lmats, c)
    uw = [jnp.dot(tm.astype(BF16), jnp.concatenate([vb, kbe], axis=1), preferred_element_type=F32)
          for tm, (_, _, _, vb, kbe, _, _) in zip(tms, units)]
    return [(x[:, :DN_DV], x[:, DN_DV:], qk) for x, qk in zip(uw, qks)]


def _gdn_chunk_state(units, states):
    c = units[0][2].shape[0]
    r = [jnp.dot(jnp.concatenate([w, q_dec], axis=0), s.astype(BF16), preferred_element_type=F32)
         for (w, q_dec, _, _, _, _), s in zip(units, states)]
    v_new = [(u - r_h[:c]).astype(BF16) for (_, _, u, _, _, _), r_h in zip(units, r)]
    o = [r_h[c:] + jnp.dot(qk, v, preferred_element_type=F32)
         for (_, _, _, qk, _, _), r_h, v in zip(units, r, v_new)]
    s_new = [s * g_last + lax.dot_general(k_dec, v, (((0,), (0,)), ((), ())), preferred_element_type=F32)
             for (_, _, _, _, k_dec, g_last), s, v in zip(units, states, v_new)]
    return o, s_new


def _tri_inverse_steps(lmats, c):
    ls = [l.astype(BF16) for l in lmats]
    ms = [jnp.dot(l, l, preferred_element_type=F32) for l in ls]
    yield
    qs = ms
    power = 2
    while 2 * power < c:
        mb = [m.astype(BF16) for m in ms]
        ms = [jnp.dot(m, m, preferred_element_type=F32) for m in mb]
        yield
        qm = [jnp.dot(q.astype(BF16), m.astype(BF16), preferred_element_type=F32) for q, m in zip(qs, ms)]
        yield
        qs = [q + m + x for q, m, x in zip(qs, ms, qm)]
        power *= 2
    rows = lax.broadcasted_iota(jnp.int32, (c, c), 0)
    cols = lax.broadcasted_iota(jnp.int32, (c, c), 1)
    eye = (rows == cols).astype(F32)
    lq = [jnp.dot(l, q.astype(BF16), preferred_element_type=F32) for l, q in zip(ls, qs)]
    yield
    return [eye - l + q - x for l, q, x in zip(lmats, qs, lq)]


def _gdn_local_steps(units, tril, strict):
    c = units[0][1].shape[0]
    a = [lax.dot_general(jnp.concatenate([kb, q], axis=0), k, (((1,), (1,)), ((), ())),
                         preferred_element_type=F32)
         for kb, q, k, _, _, _, _ in units]
    yield
    lmats, qks = [], []
    for a_u, (_, _, _, _, _, gcol, grow) in zip(a, units):
        decay = jnp.where(tril, jnp.exp(jnp.where(tril, gcol - grow, 0.0)), 0.0)
        lmats.append(jnp.where(strict, a_u[:c] * decay, 0.0))
        qks.append(jnp.where(tril, a_u[c:] * decay, 0.0))
    tms = yield from _tri_inverse_steps(lmats, c)
    uw = [jnp.dot(tm.astype(BF16), jnp.concatenate([vb, kbe], axis=1), preferred_element_type=F32)
          for tm, (_, _, _, vb, kbe, _, _) in zip(tms, units)]
    yield
    return [(x[:, :DN_DV], x[:, DN_DV:], qk) for x, qk in zip(uw, qks)]


def _alternate(primary, secondary):
    result = None
    done = False
    while not done:
        try:
            next(primary)
            yield
        except StopIteration as stop:
            result, done = stop.value, True
        if secondary is not None:
            try:
                next(secondary)
                yield
            except StopIteration:
                secondary = None
    if secondary is not None:
        yield from secondary
    return result


def _run_interleaved(*gens):
    live = list(gens)
    while live:
        for g in list(live):
            try:
                next(g)
            except StopIteration:
                live.remove(g)


def _gate_norm(o, og, norm_g):
    o = o * lax.rsqrt(jnp.mean(o * o, axis=-1, keepdims=True) + RMS_EPS) * norm_g
    return o * _silu(og)


def _proj_layout(dn_width, sc_width):
    off = {}
    pos = 0
    for name, width in (("og", dn_width), ("sc_b", sc_width), ("sc_c", sc_width), ("sc_h", sc_width)):
        off[name] = (pos, pos + width)
        pos += width
    return off


def _project(u, wq_ref, wr_ref, wab_ref, off):
    p = {"qkv": jnp.dot(u, wq_ref[...], preferred_element_type=F32),
         "ab": jnp.dot(u, wab_ref[...], preferred_element_type=F32)}
    rest = jnp.dot(u, wr_ref[...], preferred_element_type=F32)
    for name, (lo, hi) in off.items():
        p[name] = rest[:, lo:hi]
    return p


def _cols(p, off, name):
    del off
    return p[name]


def _mix_out(x, o_dn, o_sc, gate, w_out_ref, g, b, alpha):
    dn_w = o_dn.shape[1]
    mix = (jnp.dot(o_dn.astype(BF16), w_out_ref[0:dn_w, :], preferred_element_type=F32)
           + jnp.dot(o_sc.astype(BF16), w_out_ref[dn_w:, :], preferred_element_type=F32))
    return _post(x, mix, gate, g, b, alpha)


def _mixer_prompt_kernel(x_ref, mod_ref, wq_ref, wr_ref, wab_ref, w_out_ref, cw_ref, mw_ref, alog_ref, dtb_ref, ng_ref,
                         lng_ref, lnb_ref,
                         y_ref, s_out_ref, cq_out_ref, cm_out_ref,
                         s_scr, cq_scr, cm_scr, gct_scr, gl_scr, ops_scr, u_scr, w_scr, qk_scr, o_scr,
                         *, alpha, nseq, tt, off, n_conv, n_mconv):
    t_idx = pl.program_id(1)
    rows_all = nseq * tt
    nchunk = rows_all // CHUNK
    per_seq = tt // CHUNK
    hk = DN_HEADS * DN_DK
    sub = 1

    @pl.when(t_idx == 0)
    def _():
        s_scr[...] = jnp.zeros_like(s_scr)
        cq_scr[...] = jnp.zeros_like(cq_scr)
        cm_scr[...] = jnp.zeros_like(cm_scr)

    xs = [x_ref[s] for s in range(nseq)]
    gates = [mod_ref[s, 3 * sub + 2:3 * sub + 3, :] for s in range(nseq)]
    u = jnp.concatenate(
        [(xs[s] * (1.0 + mod_ref[s, 3 * sub + 1:3 * sub + 2, :]) + mod_ref[s, 3 * sub + 0:3 * sub + 1, :])
         .astype(BF16) for s in range(nseq)], axis=0)
    p = _project(u, wq_ref, wr_ref, wab_ref, off)

    def causal_conv(new_rows, hist_scr, s, w_ref, taps):
        ext = jnp.concatenate([hist_scr[s], new_rows], axis=0)
        acc = new_rows * w_ref[taps - 1:taps, :]
        for j in range(taps - 1):
            acc = acc + pltpu.roll(ext, taps - 1 - j, axis=0)[SUBLANES:] * w_ref[j:j + 1, :]
        hist_scr[s] = new_rows[tt - SUBLANES:]
        return acc

    pq = _cols(p, off, "qkv")
    acts = []
    for s in range(nseq):
        pq_s = pq[s * tt:(s + 1) * tt]
        acts.append(_silu(causal_conv(pq_s, cq_scr, s, cw_ref, n_conv)))
        cq_out_ref[s] = pq_s[tt - (n_conv - 1):]
    act = jnp.concatenate(acts, axis=0)

    ab = _cols(p, off, "ab")
    beta_all = jax.nn.sigmoid(ab)
    g = -jnp.exp(alog_ref[...]) * _softplus(ab + dtb_ref[...])
    rows = lax.broadcasted_iota(jnp.int32, (CHUNK, CHUNK), 0)
    cols = lax.broadcasted_iota(jnp.int32, (CHUNK, CHUNK), 1)
    tril = rows >= cols
    strict = rows > cols
    tril_f = tril.astype(F32)
    gc_parts, gl_parts = [], []
    for c in range(nchunk):
        gc_c = _mm_f32(tril_f, g[c * CHUNK:(c + 1) * CHUNK, :])
        gc_parts.append(gc_c)
        gl_parts.append(jnp.broadcast_to(gc_c[CHUNK - 1:CHUNK, :], (CHUNK, AB_PAD)))
    gc = jnp.concatenate(gc_parts, axis=0)
    gl_rows = jnp.concatenate(gl_parts, axis=0)
    gc_t = gc.T
    for c in range(nchunk):
        gct_scr[c] = gc_t[0:SUBLANES, c * CHUNK:(c + 1) * CHUNK]

    eg_all = jnp.exp(gc)
    ekd_all = jnp.exp(gl_rows - gc)
    egl_all = jnp.exp(gl_rows)
    for h in range(DN_HEADS):
        lanes = slice(h * DN_DK, (h + 1) * DN_DK)
        q = _l2norm(act[:, h * DN_DK:(h + 1) * DN_DK]) * (DN_DK ** -0.5)
        k = _l2norm(act[:, hk + h * DN_DK:hk + (h + 1) * DN_DK])
        v = act[:, 2 * hk + h * DN_DV:2 * hk + (h + 1) * DN_DV]
        beta = beta_all[:, DN_HEADS + h:DN_HEADS + h + 1]
        eg = eg_all[:, h:h + 1]
        kb = k * beta
        ops_scr[0, :, lanes] = kb.astype(BF16)
        ops_scr[1, :, lanes] = q.astype(BF16)
        ops_scr[2, :, lanes] = k.astype(BF16)
        ops_scr[3, :, lanes] = (v * beta).astype(BF16)
        ops_scr[4, :, lanes] = (kb * eg).astype(BF16)
        ops_scr[5, :, lanes] = (q * eg).astype(BF16)
        ops_scr[6, :, lanes] = (k * ekd_all[:, h:h + 1]).astype(BF16)
        gl_scr[:, lanes] = jnp.broadcast_to(egl_all[:, h:h + 1], (rows_all, DN_DV))

    head_lanes = [slice(h * DN_DK, (h + 1) * DN_DK) for h in range(DN_HEADS)]

    where = [(c, h, lanes) for c in range(nchunk) for h, lanes in enumerate(head_lanes)]
    units = [tuple(ops_scr[i, c * CHUNK:(c + 1) * CHUNK, lanes] for i in range(5))
             + (gc[c * CHUNK:(c + 1) * CHUNK, h:h + 1], gct_scr[c, h:h + 1, :])
             for c, h, lanes in where]
    for (c, h, lanes), (u_c, w_c, qk_c) in zip(where, _gdn_chunks_local(units, tril, strict)):
        u_scr[c * CHUNK:(c + 1) * CHUNK, lanes] = u_c
        w_scr[c * CHUNK:(c + 1) * CHUNK, lanes] = w_c.astype(BF16)
        qk_scr[c * DN_HEADS + h] = qk_c.astype(BF16)

    for j in range(per_seq):
        chains = [(s, s * per_seq + j, h, lanes) for s in range(nseq) for h, lanes in enumerate(head_lanes)]
        units = [(w_scr[c * CHUNK:(c + 1) * CHUNK, lanes], ops_scr[5, c * CHUNK:(c + 1) * CHUNK, lanes],
                  u_scr[c * CHUNK:(c + 1) * CHUNK, lanes], qk_scr[c * DN_HEADS + h],
                  ops_scr[6, c * CHUNK:(c + 1) * CHUNK, lanes], gl_scr[c * CHUNK:c * CHUNK + 1, lanes])
                 for s, c, h, lanes in chains]
        o_c, s_new = _gdn_chunk_state(units, [s_scr[s * DN_HEADS + h] for s, _, h, _ in chains])
        for (s, c, h, lanes), o_u, s_u in zip(chains, o_c, s_new):
            s_scr[s * DN_HEADS + h] = s_u
            o_scr[c * CHUNK:(c + 1) * CHUNK, lanes] = o_u
    for s in range(nseq):
        s_out_ref[s] = s_scr[s * DN_HEADS:(s + 1) * DN_HEADS]

    og = _cols(p, off, "og")
    o_dn = jnp.concatenate(
        [_gate_norm(o_scr[:, h * DN_DV:(h + 1) * DN_DV], og[:, h * DN_DV:(h + 1) * DN_DV], ng_ref[...])
         for h in range(DN_HEADS)], axis=1)

    z = _cols(p, off, "sc_c") * _cols(p, off, "sc_h")
    zcs = []
    for s in range(nseq):
        z_s = z[s * tt:(s + 1) * tt]
        zcs.append(causal_conv(z_s, cm_scr, s, mw_ref, n_mconv))
        cm_out_ref[s] = z_s[tt - (n_mconv - 1):]
    o_sc = _cols(p, off, "sc_b") * jnp.concatenate(zcs, axis=0)

    dn_w = o_dn.shape[1]
    mix = (jnp.dot(o_dn.astype(BF16), w_out_ref[0:dn_w, :], preferred_element_type=F32)
           + jnp.dot(o_sc.astype(BF16), w_out_ref[dn_w:, :], preferred_element_type=F32))
    for s in range(nseq):
        y_ref[s] = _post(xs[s], mix[s * tt:(s + 1) * tt], gates[s],
                         lng_ref[sub:sub + 1, :], lnb_ref[sub:sub + 1, :], alpha)


def _mixer_prompt(x, mod, w_in, w_out, conv_w, mconv_w, alog, dtb, norm_g, lng, lnb, *, alpha, nseq, tt, off):
    bsz, t, d = x.shape
    assert t % tt == 0 and tt % LANES == 0 and bsz % nseq == 0
    nt = t // tt
    rows = nseq * tt
    qkv_dim = conv_w.shape[-1]
    sc_w = off["sc_c"][1] - off["sc_c"][0]
    dn_w = DN_HEADS * DN_DV
    n_conv, n_mconv = conv_w.shape[0], mconv_w.shape[0]
    assert max(n_conv, n_mconv) - 1 <= SUBLANES <= tt
    nchunk = rows // CHUNK
    kern = functools.partial(_mixer_prompt_kernel, alpha=alpha, nseq=nseq, tt=tt, off=off,
                             n_conv=n_conv, n_mconv=n_mconv)
    y, s_out, cq, cm = pl.pallas_call(
        kern,
        grid=(bsz // nseq, nt),
        in_specs=[pl.BlockSpec((nseq, tt, d), lambda b, i: (b, i, 0)),
                  pl.BlockSpec((nseq, 3 * N_SUB, d), lambda b, i: (b, 0, 0)),
                  *[_const_spec(w.shape) for w in w_in], _const_spec(w_out.shape),
                  _const_spec(conv_w.shape), _const_spec(mconv_w.shape),
                  _const_spec(alog.shape), _const_spec(dtb.shape), _const_spec(norm_g.shape),
                  _const_spec(lng.shape), _const_spec(lnb.shape)],
        out_specs=[pl.BlockSpec((nseq, tt, d), lambda b, i: (b, i, 0)),
                   pl.BlockSpec((nseq, DN_HEADS, DN_DK, DN_DV), lambda b, i: (b, 0, 0, 0)),
                   pl.BlockSpec((nseq, n_conv - 1, qkv_dim), lambda b, i: (b, 0, 0)),
                   pl.BlockSpec((nseq, n_mconv - 1, sc_w), lambda b, i: (b, 0, 0))],
        out_shape=[jax.ShapeDtypeStruct((bsz, t, d), F32),
                   jax.ShapeDtypeStruct((bsz, DN_HEADS, DN_DK, DN_DV), F32),
                   jax.ShapeDtypeStruct((bsz, n_conv - 1, qkv_dim), F32),
                   jax.ShapeDtypeStruct((bsz, n_mconv - 1, sc_w), F32)],
        scratch_shapes=[pltpu.VMEM((nseq * DN_HEADS, DN_DK, DN_DV), F32),
                        pltpu.VMEM((nseq, SUBLANES, qkv_dim), F32),
                        pltpu.VMEM((nseq, SUBLANES, sc_w), F32),
                        pltpu.VMEM((nchunk, SUBLANES, CHUNK), F32),
                        pltpu.VMEM((rows, dn_w), F32),
                        pltpu.VMEM((7, rows, dn_w), BF16),
                        pltpu.VMEM((rows, dn_w), F32),
                        pltpu.VMEM((rows, dn_w), BF16),
                        pltpu.VMEM((nchunk * DN_HEADS, CHUNK, CHUNK), BF16),
                        pltpu.VMEM((rows, dn_w), F32)],
        compiler_params=pltpu.CompilerParams(dimension_semantics=("arbitrary", "arbitrary"),
                                             vmem_limit_bytes=V7X_VMEM_LIMIT),
        name="mixer_prompt",
    )(x, mod, *w_in, w_out, conv_w, mconv_w, alog, dtb, norm_g, lng, lnb)
    return y, s_out, cq, cm


def _mixer_prompt_pipe_kernel(xf_ref, modf_ref, xb_ref, modb_ref, wq_ref, wr_ref, wab_ref, w_out_ref, cw_ref,
                              mw_ref, alog_ref, dtb_ref, ng_ref, lng_ref, lnb_ref,
                              y_ref, s_out_ref, cq_out_ref, cm_out_ref,
                              s_scr, cq_scr, cm_scr, gct_scr,
                              u_scr, w_scr, qk_scr, qd_scr, kd_scr, gl_scr, og_scr, osc_scr,
                              *, alpha, nseq, tt, tiles_per_group, n_tiles, off, n_conv, n_mconv):
    k = pl.program_id(0)
    rows_all = nseq * tt
    nchunk = rows_all // CHUNK
    per_seq = tt // CHUNK
    hk = DN_HEADS * DN_DK
    sub = 1
    head_lanes = [slice(h * DN_DK, (h + 1) * DN_DK) for h in range(DN_HEADS)]
    t_front = lax.rem(jnp.minimum(k, n_tiles - 1), tiles_per_group)
    t_back = lax.rem(jnp.maximum(k - 1, 0), tiles_per_group)
    keep_front = jnp.where(t_front == 0, 0.0, 1.0).astype(F32)
    keep_back = jnp.where(t_back == 0, 0.0, 1.0).astype(F32)

    def causal_conv(new_rows, hist_scr, s, w_ref, taps):
        ext = jnp.concatenate([hist_scr[s] * keep_front, new_rows], axis=0)
        acc = new_rows * w_ref[taps - 1:taps, :]
        for j in range(taps - 1):
            acc = acc + pltpu.roll(ext, taps - 1 - j, axis=0)[SUBLANES:] * w_ref[j:j + 1, :]
        hist_scr[s] = new_rows[tt - SUBLANES:]
        return acc

    def front():
        u = jnp.concatenate(
            [(xf_ref[s] * (1.0 + modf_ref[s, 3 * sub + 1:3 * sub + 2, :])
              + modf_ref[s, 3 * sub + 0:3 * sub + 1, :]).astype(BF16) for s in range(nseq)], axis=0)
        ab = jnp.dot(u, wab_ref[...], preferred_element_type=F32)
        pq = jnp.dot(u, wq_ref[...], preferred_element_type=F32)
        yield

        beta_all = jax.nn.sigmoid(ab)
        g = -jnp.exp(alog_ref[...]) * _softplus(ab + dtb_ref[...])
        rows = lax.broadcasted_iota(jnp.int32, (CHUNK, CHUNK), 0)
        cols = lax.broadcasted_iota(jnp.int32, (CHUNK, CHUNK), 1)
        tril = rows >= cols
        strict = rows > cols
        tril_f = tril.astype(F32)
        gc_parts, gl_parts = [], []
        for c in range(nchunk):
            gc_c = _mm_f32(tril_f, g[c * CHUNK:(c + 1) * CHUNK, :])
            gc_parts.append(gc_c)
            gl_parts.append(jnp.broadcast_to(gc_c[CHUNK - 1:CHUNK, :], (CHUNK, AB_PAD)))
        yield
        p = {}

        def project_rest(names):
            for name in names:
                lo, hi = off[name]
                p[name] = jnp.dot(u, wr_ref[:, lo:hi], preferred_element_type=F32)
                yield

        yield from project_rest(("sc_c", "sc_h"))
        gc = jnp.concatenate(gc_parts, axis=0)
        gl_rows = jnp.concatenate(gl_parts, axis=0)
        gc_t = gc.T
        for c in range(nchunk):
            gct_scr[c] = gc_t[0:SUBLANES, c * CHUNK:(c + 1) * CHUNK]

        acts = []
        for s in range(nseq):
            pq_s = pq[s * tt:(s + 1) * tt]
            acts.append(_silu(causal_conv(pq_s, cq_scr, s, cw_ref, n_conv)))
            cq_out_ref[s] = pq_s[tt - (n_conv - 1):]
        act = jnp.concatenate(acts, axis=0)

        eg_all = jnp.exp(gc)
        ekd_all = jnp.exp(gl_rows - gc)
        egl_all = jnp.exp(gl_rows)
        per_head = []
        for h, lanes in enumerate(head_lanes):
            q = _l2norm(act[:, h * DN_DK:(h + 1) * DN_DK]) * (DN_DK ** -0.5)
            kk = _l2norm(act[:, hk + h * DN_DK:hk + (h + 1) * DN_DK])
            v = act[:, 2 * hk + h * DN_DV:2 * hk + (h + 1) * DN_DV]
            beta = beta_all[:, DN_HEADS + h:DN_HEADS + h + 1]
            eg = eg_all[:, h:h + 1]
            kb = kk * beta
            per_head.append((kb.astype(BF16), q.astype(BF16), kk.astype(BF16), (v * beta).astype(BF16),
                             (kb * eg).astype(BF16)))
            qd_scr[:, lanes] = (q * eg).astype(BF16)
            kd_scr[:, lanes] = (kk * ekd_all[:, h:h + 1]).astype(BF16)
            for c in range(nchunk):
                gl_scr[c * SUBLANES:(c + 1) * SUBLANES, lanes] = jnp.broadcast_to(
                    egl_all[c * CHUNK:c * CHUNK + SUBLANES, h:h + 1], (SUBLANES, DN_DV))

        z = p["sc_c"] * p["sc_h"]
        zcs = []
        for s in range(nseq):
            z_s = z[s * tt:(s + 1) * tt]
            zcs.append(causal_conv(z_s, cm_scr, s, mw_ref, n_mconv))
            cm_out_ref[s] = z_s[tt - (n_mconv - 1):]

        where = [(c, h, lanes) for c in range(nchunk) for h, lanes in enumerate(head_lanes)]
        units = [tuple(arr[c * CHUNK:(c + 1) * CHUNK] for arr in per_head[h])
                 + (gc[c * CHUNK:(c + 1) * CHUNK, h:h + 1], gct_scr[c, h:h + 1, :])
                 for c, h, lanes in where]
        local = yield from _alternate(_gdn_local_steps(units, tril, strict), project_rest(("sc_b", "og")))
        osc_scr[...] = p["sc_b"] * jnp.concatenate(zcs, axis=0)
        og_scr[...] = p["og"]
        for (c, h, lanes), (u_c, w_c, qk_c) in zip(where, local):
            u_scr[c * CHUNK:(c + 1) * CHUNK, lanes] = u_c
            w_scr[c * CHUNK:(c + 1) * CHUNK, lanes] = w_c.astype(BF16)
            qk_scr[c * DN_HEADS + h] = qk_c.astype(BF16)

    def back():
        u_all, w_all, qd_all, kd_all = u_scr[...], w_scr[...], qd_scr[...], kd_scr[...]
        gl_all, og, o_sc = gl_scr[...], og_scr[...], osc_scr[...]
        qks = [qk_scr[i] for i in range(nchunk * DN_HEADS)]
        xs = [xb_ref[s] for s in range(nseq)]
        gates = [modb_ref[s, 3 * sub + 2:3 * sub + 3, :] for s in range(nseq)]
        states = [s_scr[i] * keep_back for i in range(nseq * DN_HEADS)]
        o_rows = [[None] * DN_HEADS for _ in range(nchunk)]
        for j in range(per_seq):
            chains = [(s, s * per_seq + j, h, lanes) for s in range(nseq) for h, lanes in enumerate(head_lanes)]
            rws = [slice(c * CHUNK, (c + 1) * CHUNK) for _, c, _, _ in chains]
            r = [jnp.dot(jnp.concatenate([w_all[rw, lanes], qd_all[rw, lanes]], axis=0),
                         states[s * DN_HEADS + h].astype(BF16), preferred_element_type=F32)
                 for (s, c, h, lanes), rw in zip(chains, rws)]
            yield
            v_new = [(u_all[rw, lanes] - r_u[:CHUNK]).astype(BF16)
                     for (s, c, h, lanes), rw, r_u in zip(chains, rws, r)]
            for (s, c, h, lanes), r_u, v_u in zip(chains, r, v_new):
                o_rows[c][h] = r_u[CHUNK:] + jnp.dot(qks[c * DN_HEADS + h], v_u, preferred_element_type=F32)
            for (s, c, h, lanes), rw, v_u in zip(chains, rws, v_new):
                i = s * DN_HEADS + h
                states[i] = (states[i] * gl_all[c * SUBLANES:c * SUBLANES + 1, lanes]
                             + lax.dot_general(kd_all[rw, lanes], v_u, (((0,), (0,)), ((), ())),
                                               preferred_element_type=F32))
            yield
        for i, st in enumerate(states):
            s_scr[i] = st
        for s in range(nseq):
            for h in range(DN_HEADS):
                s_out_ref[s, h] = states[s * DN_HEADS + h]

        o_dn = jnp.concatenate(
            [_gate_norm(jnp.concatenate([o_rows[c][h] for c in range(nchunk)], axis=0),
                        og[:, lanes], ng_ref[...]) for h, lanes in enumerate(head_lanes)], axis=1)
        dn_w = o_dn.shape[1]
        mix = (jnp.dot(o_dn.astype(BF16), w_out_ref[0:dn_w, :], preferred_element_type=F32)
               + jnp.dot(o_sc.astype(BF16), w_out_ref[dn_w:, :], preferred_element_type=F32))
        yield
        for s in range(nseq):
            y_ref[s] = _post(xs[s], mix[s * tt:(s + 1) * tt], gates[s],
                             lng_ref[sub:sub + 1, :], lnb_ref[sub:sub + 1, :], alpha)

    @pl.when(k == 0)
    def _():
        s_scr[...] = jnp.zeros_like(s_scr)
        cq_scr[...] = jnp.zeros_like(cq_scr)
        cm_scr[...] = jnp.zeros_like(cm_scr)
        _run_interleaved(front())

    @pl.when(jnp.logical_and(k > 0, k < n_tiles))
    def _():
        _run_interleaved(back(), front())

    @pl.when(k == n_tiles)
    def _():
        _run_interleaved(back())


def _mixer_prompt_pipe(x, mod, w_in, w_out, conv_w, mconv_w, alog, dtb, norm_g, lng, lnb, *, alpha, nseq, tt, off):
    bsz, t, d = x.shape
    assert t % tt == 0 and tt % LANES == 0 and bsz % nseq == 0
    nt = t // tt
    n_tiles = (bsz // nseq) * nt
    rows = nseq * tt
    qkv_dim = conv_w.shape[-1]
    sc_w = off["sc_c"][1] - off["sc_c"][0]
    dn_w = DN_HEADS * DN_DV
    n_conv, n_mconv = conv_w.shape[0], mconv_w.shape[0]
    assert max(n_conv, n_mconv) - 1 <= SUBLANES <= tt
    nchunk = rows // CHUNK
    kern = functools.partial(_mixer_prompt_pipe_kernel, alpha=alpha, nseq=nseq, tt=tt, tiles_per_group=nt,
                             n_tiles=n_tiles, off=off, n_conv=n_conv, n_mconv=n_mconv)

    def front_tile(k):
        return jnp.minimum(k, n_tiles - 1)

    def back_tile(k):
        return jnp.maximum(k - 1, 0)

    y, s_out, cq, cm = pl.pallas_call(
        kern,
        grid=(n_tiles + 1,),
        in_specs=[pl.BlockSpec((nseq, tt, d), lambda k: (front_tile(k) // nt, front_tile(k) % nt, 0)),
                  pl.BlockSpec((nseq, 3 * N_SUB, d), lambda k: (front_tile(k) // nt, 0, 0)),
                  pl.BlockSpec((nseq, tt, d), lambda k: (back_tile(k) // nt, back_tile(k) % nt, 0)),
                  pl.BlockSpec((nseq, 3 * N_SUB, d), lambda k: (back_tile(k) // nt, 0, 0)),
                  *[_const_spec(w.shape) for w in w_in], _const_spec(w_out.shape),
                  _const_spec(conv_w.shape), _const_spec(mconv_w.shape),
                  _const_spec(alog.shape), _const_spec(dtb.shape), _const_spec(norm_g.shape),
                  _const_spec(lng.shape), _const_spec(lnb.shape)],
        out_specs=[pl.BlockSpec((nseq, tt, d), lambda k: (back_tile(k) // nt, back_tile(k) % nt, 0)),
                   pl.BlockSpec((nseq, DN_HEADS, DN_DK, DN_DV), lambda k: (back_tile(k) // nt, 0, 0, 0)),
                   pl.BlockSpec((nseq, n_conv - 1, qkv_dim), lambda k: (front_tile(k) // nt, 0, 0)),
                   pl.BlockSpec((nseq, n_mconv - 1, sc_w), lambda k: (front_tile(k) // nt, 0, 0))],
        out_shape=[jax.ShapeDtypeStruct((bsz, t, d), F32),
                   jax.ShapeDtypeStruct((bsz, DN_HEADS, DN_DK, DN_DV), F32),
                   jax.ShapeDtypeStruct((bsz, n_conv - 1, qkv_dim), F32),
                   jax.ShapeDtypeStruct((bsz, n_mconv - 1, sc_w), F32)],
        scratch_shapes=[pltpu.VMEM((nseq * DN_HEADS, DN_DK, DN_DV), F32),
                        pltpu.VMEM((nseq, SUBLANES, qkv_dim), F32),
                        pltpu.VMEM((nseq, SUBLANES, sc_w), F32),
                        pltpu.VMEM((nchunk, SUBLANES, CHUNK), F32),
                        pltpu.VMEM((rows, dn_w), F32),
                        pltpu.VMEM((rows, dn_w), BF16),
                        pltpu.VMEM((nchunk * DN_HEADS, CHUNK, CHUNK), BF16),
                        pltpu.VMEM((rows, dn_w), BF16),
                        pltpu.VMEM((rows, dn_w), BF16),
                        pltpu.VMEM((nchunk * SUBLANES, dn_w), F32),
                        pltpu.VMEM((rows, dn_w), F32),
                        pltpu.VMEM((rows, sc_w), F32)],
        compiler_params=pltpu.CompilerParams(dimension_semantics=("arbitrary",),
                                             vmem_limit_bytes=V7X_VMEM_LIMIT),
        name="mixer_prompt",
    )(x, mod, x, mod, *w_in, w_out, conv_w, mconv_w, alog, dtb, norm_g, lng, lnb)
    return y, s_out, cq, cm


_PACK = SUBLANES


def _sample_phase_a(x_ref, mod_ref, wq_ref, wr_ref, wab_ref, cw_ref, mw_ref, alog_ref, dtb_ref, cq_in_ref, cm_in_ref,
                    cq_out_ref, cm_out_ref, og_scr, osc_scr, lhs_scr, u_scr, kd_scr, gl_scr, qk_scr,
                    *, t_len, d, off, n_conv, n_mconv):
    bsz = x_ref.shape[0]
    hk = DN_HEADS * DN_DK
    qkv_dim = cw_ref.shape[-1]
    sc_w = off["sc_c"][1] - off["sc_c"][0]

    x = _stack_time(x_ref, t_len, d)
    u = (x * (1.0 + _mod_rows(mod_ref, 1, t_len, d)) + _mod_rows(mod_ref, 0, t_len, d)).astype(BF16)
    p = _project(u, wq_ref, wr_ref, wab_ref, off)

    def rows(arr, t):
        return arr[t * bsz:(t + 1) * bsz]

    pq = _cols(p, off, "qkv")
    seq = ([cq_in_ref[:, j * qkv_dim:(j + 1) * qkv_dim] for j in range(n_conv - 1)]
           + [rows(pq, t) for t in range(t_len)])
    act = []
    for t in range(t_len):
        acc = None
        for j in range(n_conv):
            term = seq[t + j] * cw_ref[j:j + 1, :]
            acc = term if acc is None else acc + term
        act.append(_silu(acc))
    for j in range(n_conv - 1):
        cq_out_ref[:, j * qkv_dim:(j + 1) * qkv_dim] = seq[t_len + j]

    z = _cols(p, off, "sc_c") * _cols(p, off, "sc_h")
    zseq = ([cm_in_ref[:, j * sc_w:(j + 1) * sc_w] for j in range(n_mconv - 1)]
            + [rows(z, t) for t in range(t_len)])
    zc = []
    for t in range(t_len):
        acc = None
        for j in range(n_mconv):
            term = zseq[t + j] * mw_ref[j:j + 1, :]
            acc = term if acc is None else acc + term
        zc.append(acc)
    for j in range(n_mconv - 1):
        cm_out_ref[:, j * sc_w:(j + 1) * sc_w] = zseq[t_len + j]
    osc_scr[...] = _cols(p, off, "sc_b") * jnp.concatenate(zc, axis=0)
    og_scr[...] = _cols(p, off, "og")

    ab = _cols(p, off, "ab")
    beta_all = jax.nn.sigmoid(ab)
    g_all = -jnp.exp(alog_ref[...]) * _softplus(ab + dtb_ref[...])
    zero = jnp.zeros((bsz, LANES), F32)

    for h in range(DN_HEADS):
        q = [_l2norm(a[:, h * DN_DK:(h + 1) * DN_DK]) * (DN_DK ** -0.5) for a in act]
        k = [_l2norm(a[:, hk + h * DN_DK:hk + (h + 1) * DN_DK]) for a in act]
        v = [a[:, 2 * hk + h * DN_DV:2 * hk + (h + 1) * DN_DV] for a in act]
        beta = [rows(beta_all, t)[:, DN_HEADS + h:DN_HEADS + h + 1] for t in range(t_len)]
        g = [rows(g_all, t)[:, h:h + 1] for t in range(t_len)]
        gc = [g[0]]
        for t in range(1, t_len):
            gc.append(gc[-1] + g[t])
        kb = [k[t] * beta[t] for t in range(t_len)]
        vb = [v[t] * beta[t] for t in range(t_len)]
        eg = [jnp.exp(gc[t]) for t in range(t_len)]
        kbe = [kb[t] * eg[t] for t in range(t_len)]
        lm = [[None] * t_len for _ in range(t_len)]
        n_qk = 0
        for i in range(t_len):
            for j in range(i + 1):
                dec = jnp.exp(gc[i] - gc[j])
                qk_scr[h, n_qk] = jnp.broadcast_to(_rowsum(q[i] * k[j]) * dec, (bsz, LANES))
                n_qk += 1
                if j < i:
                    lm[i][j] = _rowsum(kb[i] * k[j]) * dec
        tm = [[None] * t_len for _ in range(t_len)]
        for i in range(t_len):
            for j in range(i):
                acc = lm[i][j]
                for m in range(j + 1, i):
                    acc = acc + lm[i][m] * tm[m][j]
                tm[i][j] = -acc
        for i in range(t_len):
            ui, wi = vb[i], kbe[i]
            for j in range(i):
                ui = ui + tm[i][j] * vb[j]
                wi = wi + tm[i][j] * kbe[j]
            lhs_scr[h, pl.ds(i, bsz, stride=_PACK), :] = wi
            lhs_scr[h, pl.ds(t_len + i, bsz, stride=_PACK), :] = q[i] * eg[i]
            u_scr[h, pl.ds(i, bsz, stride=_PACK), :] = ui
            u_scr[h, pl.ds(t_len + i, bsz, stride=_PACK), :] = zero
            kd_scr[h, pl.ds(i, bsz, stride=_PACK), :] = k[i] * jnp.exp(gc[t_len - 1] - gc[i])
            kd_scr[h, pl.ds(t_len + i, bsz, stride=_PACK), :] = zero
        gl_scr[h] = jnp.broadcast_to(jnp.exp(gc[t_len - 1]), (bsz, LANES))


def _sample_phase_c(x_ref, mod_ref, w_out_ref, ng_ref, lng_ref, lnb_ref, y_ref,
                    og_scr, osc_scr, res_scr, vn_scr, qk_scr, *, alpha, t_len, d):
    bsz = x_ref.shape[0]
    sub = 1
    x = _stack_time(x_ref, t_len, d)
    gate = _mod_rows(mod_ref, 2, t_len, d)
    og = og_scr[...]
    per_t = []
    for i in range(t_len):
        heads = []
        for h in range(DN_HEADS):
            oi = res_scr[h, pl.ds(t_len + i, bsz, stride=_PACK), :]
            base = i * (i + 1) // 2
            for j in range(i + 1):
                oi = oi + qk_scr[h, base + j] * vn_scr[h, pl.ds(j, bsz, stride=_PACK), :]
            heads.append(_gate_norm(oi, og[i * bsz:(i + 1) * bsz, h * DN_DV:(h + 1) * DN_DV], ng_ref[...]))
        per_t.append(jnp.concatenate(heads, axis=1))
    o_dn = jnp.concatenate(per_t, axis=0)
    y = _mix_out(x, o_dn, osc_scr[...], gate, w_out_ref,
                 lng_ref[sub:sub + 1, :], lnb_ref[sub:sub + 1, :], alpha)
    for t in range(t_len):
        y_ref[:, t * d:(t + 1) * d] = y[t * bsz:(t + 1) * bsz]


def _mixer_sample_kernel(x_ref, mod_ref, wq_ref, wr_ref, wab_ref, w_out_ref, cw_ref, mw_ref, alog_ref, dtb_ref, ng_ref,
                         lng_ref, lnb_ref, cq_in_ref, cm_in_ref, s_in_ref,
                         y_ref, cq_out_ref, cm_out_ref, s_out_ref,
                         og_scr, osc_scr, lhs_scr, u_scr, kd_scr, gl_scr, qk_scr, res_scr, vn_scr,
                         *, alpha, t_len, d, off, n_conv, n_mconv, bt):
    step = pl.program_id(0)
    assert 2 * t_len == _PACK, "packed buffers hold [T rows | T rows] per sequence"

    @pl.when(step == 0)
    def _():
        _sample_phase_a(x_ref, mod_ref, wq_ref, wr_ref, wab_ref, cw_ref, mw_ref, alog_ref, dtb_ref, cq_in_ref, cm_in_ref,
                        cq_out_ref, cm_out_ref, og_scr, osc_scr, lhs_scr, u_scr, kd_scr, gl_scr, qk_scr,
                        t_len=t_len, d=d, off=off, n_conv=n_conv, n_mconv=n_mconv)

    top = (lax.broadcasted_iota(jnp.int32, (_PACK, LANES), 0) < t_len).astype(F32)

    chains = [(bl, h) for bl in range(bt) for h in range(DN_HEADS)]
    seq0 = step * bt
    row0 = [pl.multiple_of((seq0 + bl) * _PACK, _PACK) for bl in range(bt)]
    states = [s_in_ref[bl, h] for bl, h in chains]
    r = [_mm(lhs_scr[h, pl.ds(row0[bl], _PACK), :], s) for (bl, h), s in zip(chains, states)]
    vns = []
    for (bl, h), r_u in zip(chains, r):
        res_scr[h, pl.ds(row0[bl], _PACK), :] = r_u
        vn = (u_scr[h, pl.ds(row0[bl], _PACK), :] - r_u) * top
        vn_scr[h, pl.ds(row0[bl], _PACK), :] = vn
        vns.append(vn)
    upd = [_mm_at(kd_scr[h, pl.ds(row0[bl], _PACK), :], vn) for (bl, h), vn in zip(chains, vns)]
    for (bl, h), s, s_add in zip(chains, states, upd):
        s_out_ref[bl, h] = s * gl_scr[h, pl.ds(seq0 + bl, 1), :] + s_add

    @pl.when(step == pl.num_programs(0) - 1)
    def _():
        _sample_phase_c(x_ref, mod_ref, w_out_ref, ng_ref, lng_ref, lnb_ref, y_ref,
                        og_scr, osc_scr, res_scr, vn_scr, qk_scr, alpha=alpha, t_len=t_len, d=d)


def _mixer_sample(x, mod, w_in, w_out, conv_w, mconv_w, alog, dtb, norm_g, lng, lnb, s_in, cq_in, cm_in,
                  *, alpha, off, bt):
    bsz, t, d = x.shape
    assert bsz % bt == 0
    qkv_dim = conv_w.shape[-1]
    sc_w = off["sc_c"][1] - off["sc_c"][0]
    dn_w = DN_HEADS * DN_DV
    n_conv, n_mconv = conv_w.shape[0], mconv_w.shape[0]
    n_qk = t * (t + 1) // 2
    x2 = x.reshape(bsz, t * d)
    cq2 = cq_in.reshape(bsz, (n_conv - 1) * qkv_dim)
    cm2 = cm_in.reshape(bsz, (n_mconv - 1) * sc_w)
    kern = functools.partial(_mixer_sample_kernel, alpha=alpha, t_len=t, d=d, off=off,
                             n_conv=n_conv, n_mconv=n_mconv, bt=bt)
    consts = (*w_in, w_out, conv_w, mconv_w, alog, dtb, norm_g, lng, lnb, cq2, cm2)
    state_spec = pl.BlockSpec((bt, DN_HEADS, DN_DK, DN_DV), lambda i: (i, 0, 0, 0))
    packed = pltpu.VMEM((DN_HEADS, _PACK * bsz, LANES), F32)
    y, cq, cm, s_out = pl.pallas_call(
        kern,
        grid=(bsz // bt,),
        in_specs=([_const_spec(x2.shape), _sub_mod_spec(bsz, d, 1)]
                  + [_const_spec(a.shape) for a in consts] + [state_spec]),
        out_specs=[pl.BlockSpec(x2.shape, lambda i: (0, 0)),
                   pl.BlockSpec(cq2.shape, lambda i: (0, 0)),
                   pl.BlockSpec(cm2.shape, lambda i: (0, 0)),
                   state_spec],
        out_shape=[jax.ShapeDtypeStruct(x2.shape, F32),
                   jax.ShapeDtypeStruct(cq2.shape, F32),
                   jax.ShapeDtypeStruct(cm2.shape, F32),
                   jax.ShapeDtypeStruct(s_in.shape, F32)],
        scratch_shapes=[pltpu.VMEM((t * bsz, dn_w), F32),
                        pltpu.VMEM((t * bsz, sc_w), F32),
                        packed,
                        packed,
                        packed,
                        pltpu.VMEM((DN_HEADS, bsz, LANES), F32),
                        pltpu.VMEM((DN_HEADS, n_qk, bsz, LANES), F32),
                        packed,
                        packed],
        compiler_params=pltpu.CompilerParams(dimension_semantics=("arbitrary",),
                                             vmem_limit_bytes=V7X_VMEM_LIMIT),
        name="mixer_sample",
    )(x2, mod, *consts, s_in)
    return (y.reshape(bsz, t, d), s_out, cq.reshape(bsz, n_conv - 1, qkv_dim),
            cm.reshape(bsz, n_mconv - 1, sc_w))


def _split_w_in(w_in, qkv_dim):
    c1 = qkv_dim + 2 * DN_HEADS
    wab = jnp.pad(w_in[:, qkv_dim:c1], ((0, 0), (0, AB_PAD - 2 * DN_HEADS)))
    return w_in[:, :qkv_dim].astype(BF16), w_in[:, c1:].astype(BF16), wab.astype(BF16)


def _pad_lanes(v):
    return jnp.zeros((1, AB_PAD), F32).at[0, :v.shape[0]].set(v)


def kernel(x_prompt, x_sample, state_ssm, state_conv_qkv, state_conv_mix, c_prompt, c_sample, w_ada, b_ada, ln_g, ln_b, ffn1_wg, ffn1_wu, ffn1_wd, ffn2_wg, ffn2_wu, ffn2_wd, w_in, conv_qkv_w, a_log, dt_bias, dn_norm_g, conv_mix_w, w_out):
    depth = w_ada.shape[0]
    alpha = (2 * depth) ** 0.25
    bp, tp, d = x_prompt.shape
    qkv_dim = conv_qkv_w.shape[-1]
    sc_width = conv_mix_w.shape[-1]
    dn_width = DN_HEADS * DN_DV
    off = _proj_layout(dn_width, sc_width)
    tm = min(512, tp)
    nseq = 2 if bp % 2 == 0 else 1
    tt = min(512 // nseq, tp)
    bt = min(8, x_sample.shape[0])

    hp, hs = x_prompt, x_sample
    outs = [[] for _ in range(6)]
    for l in range(depth):
        mod_p, mod_s, *w_in_p = _adaln(c_prompt, c_sample, w_ada[l], b_ada[l].reshape(1, -1), w_in[l], qkv_dim)
        mod_p = mod_p.reshape(bp, 3 * N_SUB, d)
        lng, lnb = ln_g[l], ln_b[l]
        w1 = (ffn1_wg[l], ffn1_wu[l], ffn1_wd[l])
        w2 = (ffn2_wg[l], ffn2_wu[l], ffn2_wd[l])
        w_out_b = w_out[l].astype(BF16)
        alog, dtb = _pad_lanes(a_log[l]), _pad_lanes(dt_bias[l])
        norm_g = dn_norm_g[l].reshape(1, -1)
        mix_args = (w_in_p, w_out_b, conv_qkv_w[l], conv_mix_w[l], alog, dtb, norm_g, lng, lnb)

        hp, hs = _ffn(hp, hs, mod_p, mod_s, *w1, lng, lnb, sub=0, alpha=alpha, tm=tm)
        hp, a1, a2, a3 = _mixer_prompt_pipe(hp, mod_p, *mix_args, alpha=alpha, nseq=nseq, tt=tt, off=off)
        hs, b1, b2, b3 = _mixer_sample(hs, mod_s, *mix_args, state_ssm[l], state_conv_qkv[l],
                                       state_conv_mix[l], alpha=alpha, off=off, bt=bt)
        hp, hs = _ffn(hp, hs, mod_p, mod_s, *w2, lng, lnb, sub=2, alpha=alpha, tm=tm)
        for lst, val in zip(outs, (a1, a2, a3, b1, b2, b3)):
            lst.append(val)
    return (hp, hs) + tuple(jnp.stack(o) for o in outs)
```

```python
import functools

import jax
import jax.numpy as jnp
from jax import lax
from jax.experimental import pallas as pl
from jax.experimental.pallas import tpu as pltpu

F32 = jnp.float32
BF16 = jnp.bfloat16

LN_EPS = 1e-5
RMS_EPS = 1e-6
N_SUB = 3
DN_HEADS = 4
DN_DK = 128
DN_DV = 128
CHUNK = 64
LANES = 128
SUBLANES = 8
AB_PAD = LANES

V7X_VMEM_LIMIT = 60000 * 1024
CAST_BUFFERS = 4


def _mm(a, b):
    return jnp.dot(a.astype(BF16), b.astype(BF16), preferred_element_type=F32)


def _mm_bt(a, b):
    return lax.dot_general(a.astype(BF16), b.astype(BF16), (((1,), (1,)), ((), ())),
                           preferred_element_type=F32)


def _mm_at(a, b):
    return lax.dot_general(a.astype(BF16), b.astype(BF16), (((0,), (0,)), ((), ())),
                           preferred_element_type=F32)


def _mm_f32(a, b):
    return jnp.dot(a, b, preferred_element_type=F32, precision=lax.Precision.HIGHEST)


def _silu(x):
    return x * jax.nn.sigmoid(x)


def _softplus(x):
    return jnp.maximum(x, 0.0) + jnp.log1p(jnp.exp(-jnp.abs(x)))


def _layer_norm(y, g, b):
    mu = jnp.mean(y, axis=-1, keepdims=True)
    yc = y - mu
    var = jnp.mean(yc * yc, axis=-1, keepdims=True)
    return yc * lax.rsqrt(var + LN_EPS) * g + b


def _post(x, delta, gate, g, b, alpha):
    return _layer_norm(alpha * x + gate * delta, g, b)


def _rowsum(x):
    return jnp.sum(x, axis=-1, keepdims=True)


def _l2norm(x):
    return x * lax.rsqrt(_rowsum(x * x) + RMS_EPS)


def _const_spec(shape):
    nd = len(shape)
    return pl.BlockSpec(shape, lambda *_: (0,) * nd, pipeline_mode=pl.Buffered(1))


def _adaln_kernel(cp_ref, cs_ref, w_ref, b_ref, win_ref, op_ref, os_ref, wq_ref, wr_ref, wab_ref, *, qkv_dim):
    mp = cp_ref.shape[0]
    c = jnp.concatenate([cp_ref[...], cs_ref[...]], axis=0)
    mod = _mm(_silu(c), w_ref[...]) + b_ref[...]
    op_ref[...] = mod[:mp]
    os_ref[...] = mod[mp:]

    gates = 2 * DN_HEADS
    wq_ref[...] = win_ref[0:qkv_dim, :].T.astype(BF16)
    wr_ref[...] = win_ref[qkv_dim + gates:, :].T.astype(BF16)
    ab_rows = jnp.concatenate([win_ref[qkv_dim:qkv_dim + gates, :],
                               jnp.zeros((AB_PAD - gates, win_ref.shape[1]), F32)], axis=0)
    wab_ref[...] = ab_rows.T.astype(BF16)


def _adaln(c_p, c_s, w, b, w_in_t, qkv_dim):
    (mp, d), ms = c_p.shape, c_s.shape[0]
    n = w.shape[1]
    proj = w_in_t.shape[0]
    rest = proj - qkv_dim - 2 * DN_HEADS
    assert qkv_dim % LANES == 0 and rest % LANES == 0 and (2 * DN_HEADS) % SUBLANES == 0
    steps = max(g for g in (1, 2, 4) if n % (g * LANES) == 0 and d % (g * LANES) == 0)
    tn, rows = n // steps, d // steps
    return pl.pallas_call(
        functools.partial(_adaln_kernel, qkv_dim=qkv_dim),
        grid=(steps,),
        in_specs=[_const_spec((mp, d)), _const_spec((ms, d)),
                  pl.BlockSpec((d, tn), lambda j: (0, j)),
                  pl.BlockSpec((1, tn), lambda j: (0, j)),
                  pl.BlockSpec((proj, rows), lambda j: (0, j))],
        out_specs=[pl.BlockSpec((mp, tn), lambda j: (0, j)), pl.BlockSpec((ms, tn), lambda j: (0, j)),
                   pl.BlockSpec((rows, qkv_dim), lambda j: (j, 0)),
                   pl.BlockSpec((rows, rest), lambda j: (j, 0)),
                   pl.BlockSpec((rows, AB_PAD), lambda j: (j, 0))],
        out_shape=[jax.ShapeDtypeStruct((mp, n), F32), jax.ShapeDtypeStruct((ms, n), F32),
                   jax.ShapeDtypeStruct((d, qkv_dim), BF16), jax.ShapeDtypeStruct((d, rest), BF16),
                   jax.ShapeDtypeStruct((d, AB_PAD), BF16)],
        compiler_params=pltpu.CompilerParams(dimension_semantics=("arbitrary",),
                                             vmem_limit_bytes=V7X_VMEM_LIMIT),
        name="adaln",
    )(c_p, c_s, w, b, w_in_t)


def _ffn_rows(x, shift, scale, gate, wg_ref, wu_ref, wd_ref, g, b, alpha):
    u = (x * (1.0 + scale) + shift).astype(BF16)
    hg = jnp.dot(u, wg_ref[...], preferred_element_type=F32)
    hu = jnp.dot(u, wu_ref[...], preferred_element_type=F32)
    h = (_silu(hg) * hu).astype(BF16)
    d = jnp.dot(h, wd_ref[...], preferred_element_type=F32)
    return _post(x, 0.5 * d, gate, g, b, alpha)


def _ffn_prompt_kernel(x_ref, mod_ref, wg_ref, wu_ref, wd_ref, lng_ref, lnb_ref, o_ref, *, sub, alpha, parts):
    shift = mod_ref[0, 3 * sub + 0:3 * sub + 1, :]
    scale = mod_ref[0, 3 * sub + 1:3 * sub + 2, :]
    gate = mod_ref[0, 3 * sub + 2:3 * sub + 3, :]
    rows = x_ref.shape[0] // parts
    for part in range(parts):
        sl = slice(part * rows, (part + 1) * rows)
        o_ref[sl, :] = _ffn_rows(x_ref[sl, :], shift, scale, gate, wg_ref, wu_ref, wd_ref,
                                 lng_ref[sub:sub + 1, :], lnb_ref[sub:sub + 1, :], alpha)


def _ffn_prompt(x, mod, wg, wu, wd, lng, lnb, *, sub, alpha, tm, parts):
    bsz, t, d = x.shape
    f = wg.shape[1]
    assert t % tm == 0 and tm % (parts * SUBLANES) == 0
    per_seq = t // tm
    x2 = x.reshape(bsz * t, d)
    out = pl.pallas_call(
        functools.partial(_ffn_prompt_kernel, sub=sub, alpha=alpha, parts=parts),
        grid=(bsz * per_seq,),
        in_specs=[pl.BlockSpec((tm, d), lambda i: (i, 0)),
                  pl.BlockSpec((1, 3 * N_SUB, d), lambda i: (i // per_seq, 0, 0)),
                  _const_spec((d, f)), _const_spec((d, f)), _const_spec((f, d)),
                  _const_spec(lng.shape), _const_spec(lnb.shape)],
        out_specs=pl.BlockSpec((tm, d), lambda i: (i, 0)),
        out_shape=jax.ShapeDtypeStruct((bsz * t, d), F32),
        compiler_params=pltpu.CompilerParams(dimension_semantics=("arbitrary",),
                                             vmem_limit_bytes=V7X_VMEM_LIMIT),
        name=f"ffn_prompt{sub}",
    )(x2, mod, wg, wu, wd, lng, lnb)
    return out.reshape(bsz, t, d)


def _stack_time(ref, t_len, d):
    return jnp.concatenate([ref[:, t * d:(t + 1) * d] for t in range(t_len)], axis=0)


def _mod_rows(mod_ref, idx, t_len, d):
    m = mod_ref[:, idx * d:(idx + 1) * d]
    return jnp.concatenate([m] * t_len, axis=0)


def _ffn_sample_kernel(x_ref, mod_ref, wg_ref, wu_ref, wd_ref, lng_ref, lnb_ref, o_ref, *,
                       sub, alpha, t_len, d):
    bsz = x_ref.shape[0]
    x = _stack_time(x_ref, t_len, d)
    y = _ffn_rows(x, _mod_rows(mod_ref, 0, t_len, d), _mod_rows(mod_ref, 1, t_len, d),
                  _mod_rows(mod_ref, 2, t_len, d), wg_ref, wu_ref, wd_ref,
                  lng_ref[sub:sub + 1, :], lnb_ref[sub:sub + 1, :], alpha)
    for t in range(t_len):
        o_ref[:, t * d:(t + 1) * d] = y[t * bsz:(t + 1) * bsz]


def _sub_mod_spec(bsz, d, sub):
    return pl.BlockSpec((bsz, 3 * d), lambda *_: (0, sub), pipeline_mode=pl.Buffered(1))


def _cast_weights(jobs):
    chunks = []
    used = {}
    for w_hbm, w_scr, stage, sem in jobs:
        nbuf, rows = stage.shape[0], stage.shape[1]
        for i in range(w_hbm.shape[0] // rows):
            slot = used.get(id(stage), 0) % nbuf
            used[id(stage)] = used.get(id(stage), 0) + 1
            copy = pltpu.make_async_copy(w_hbm.at[pl.ds(i * rows, rows), :], stage.at[slot], sem.at[slot])
            chunks.append((copy, w_scr, stage, slot, i * rows, rows))
    ahead = min(job[2].shape[0] for job in jobs) - 1
    for copy, *_ in chunks[:ahead]:
        copy.start()
    for j, (copy, w_scr, stage, slot, row0, rows) in enumerate(chunks):
        if j + ahead < len(chunks):
            chunks[j + ahead][0].start()
        copy.wait()
        w_scr[row0:row0 + rows, :] = stage[slot].astype(BF16)


def _swiglu(x, shift, scale, wg_ref, wu_ref, wd_ref):
    u = (x * (1.0 + scale) + shift).astype(BF16)
    hg = jnp.dot(u, wg_ref[...], preferred_element_type=F32)
    hu = jnp.dot(u, wu_ref[...], preferred_element_type=F32)
    h = (_silu(hg) * hu).astype(BF16)
    return jnp.dot(h, wd_ref[...], preferred_element_type=F32)


def _ffn_kernel(xp_ref, modp_ref, xq_ref, modq_ref, xs_ref, mods_ref, wg_hbm, wu_hbm, wd_hbm, lng_ref, lnb_ref,
                yp_ref, ys_ref, wg_scr, wu_scr, wd_scr, d_scr, stage_in, stage_out, sem_in, sem_out,
                *, sub, alpha, n_prompt, t_len, d):
    k = pl.program_id(0)
    g, b = lng_ref[sub:sub + 1, :], lnb_ref[sub:sub + 1, :]

    def matmuls():
        return _swiglu(xp_ref[...], modp_ref[0, 3 * sub + 0:3 * sub + 1, :], modp_ref[0, 3 * sub + 1:3 * sub + 2, :],
                       wg_scr, wu_scr, wd_scr)

    def finish(d_prev):
        yp_ref[...] = _post(xq_ref[...], 0.5 * d_prev, modq_ref[0, 3 * sub + 2:3 * sub + 3, :], g, b, alpha)

    @pl.when(k == 0)
    def _():
        _cast_weights([(wg_hbm, wg_scr, stage_in, sem_in), (wu_hbm, wu_scr, stage_in, sem_in),
                       (wd_hbm, wd_scr, stage_out, sem_out)])
        d_scr[...] = matmuls()

    @pl.when(jnp.logical_and(k > 0, k < n_prompt))
    def _():
        d_prev = d_scr[...]
        d_new = matmuls()
        finish(d_prev)
        d_scr[...] = d_new

    @pl.when(k == n_prompt)
    def _():
        finish(d_scr[...])
        bsz = xs_ref.shape[0]
        x = _stack_time(xs_ref, t_len, d)
        y = _ffn_rows(x, _mod_rows(mods_ref, 0, t_len, d), _mod_rows(mods_ref, 1, t_len, d),
                      _mod_rows(mods_ref, 2, t_len, d), wg_scr, wu_scr, wd_scr, g, b, alpha)
        for t in range(t_len):
            ys_ref[:, t * d:(t + 1) * d] = y[t * bsz:(t + 1) * bsz]


def _ffn(xp, xs, mod_p, mod_s, wg, wu, wd, lng, lnb, *, sub, alpha, tm):
    bp, tp, d = xp.shape
    bs, ts, _ = xs.shape
    f = wg.shape[1]
    assert tp % tm == 0
    per_seq = tp // tm
    n_prompt = bp * per_seq
    in_rows = max(r for r in (16, 32, 64, 128) if d % r == 0)
    out_rows = max(r for r in (16, 32, 64, 128, 176, 352) if f % r == 0)
    last = n_prompt - 1
    yp, ys = pl.pallas_call(
        functools.partial(_ffn_kernel, sub=sub, alpha=alpha, n_prompt=n_prompt, t_len=ts, d=d),
        grid=(n_prompt + 1,),
        in_specs=[pl.BlockSpec((tm, d), lambda k: (jnp.minimum(k, last), 0)),
                  pl.BlockSpec((1, 3 * N_SUB, d), lambda k: (jnp.minimum(k, last) // per_seq, 0, 0)),
                  pl.BlockSpec((tm, d), lambda k: (jnp.maximum(k - 1, 0), 0)),
                  pl.BlockSpec((1, 3 * N_SUB, d), lambda k: (jnp.maximum(k - 1, 0) // per_seq, 0, 0)),
                  _const_spec((bs, ts * d)), _sub_mod_spec(bs, d, sub),
                  pl.BlockSpec(memory_space=pl.ANY), pl.BlockSpec(memory_space=pl.ANY),
                  pl.BlockSpec(memory_space=pl.ANY),
                  _const_spec(lng.shape), _const_spec(lnb.shape)],
        out_specs=[pl.BlockSpec((tm, d), lambda k: (jnp.maximum(k - 1, 0), 0)),
                   pl.BlockSpec((bs, ts * d), lambda k: (0, 0))],
        out_shape=[jax.ShapeDtypeStruct((bp * tp, d), F32),
                   jax.ShapeDtypeStruct((bs, ts * d), F32)],
        scratch_shapes=[pltpu.VMEM((d, f), BF16), pltpu.VMEM((d, f), BF16), pltpu.VMEM((f, d), BF16),
                        pltpu.VMEM((tm, d), F32),
                        pltpu.VMEM((CAST_BUFFERS, in_rows, f), F32), pltpu.VMEM((CAST_BUFFERS, out_rows, d), F32),
                        pltpu.SemaphoreType.DMA((CAST_BUFFERS,)), pltpu.SemaphoreType.DMA((CAST_BUFFERS,))],
        compiler_params=pltpu.CompilerParams(dimension_semantics=("arbitrary",),
                                             vmem_limit_bytes=V7X_VMEM_LIMIT),
        name=f"ffn{sub}",
    )(xp.reshape(bp * tp, d), mod_p, xp.reshape(bp * tp, d), mod_p, xs.reshape(bs, ts * d), mod_s,
      wg, wu, wd, lng, lnb)
    return yp.reshape(bp, tp, d), ys.reshape(bs, ts, d)


def _ffn_sample(x, mod, wg, wu, wd, lng, lnb, *, sub, alpha):
    bsz, t, d = x.shape
    f = wg.shape[1]
    x2 = x.reshape(bsz, t * d)
    out = pl.pallas_call(
        functools.partial(_ffn_sample_kernel, sub=sub, alpha=alpha, t_len=t, d=d),
        grid=(1,),
        in_specs=[_const_spec((bsz, t * d)), _sub_mod_spec(bsz, d, sub),
                  _const_spec((d, f)), _const_spec((d, f)), _const_spec((f, d)),
                  _const_spec(lng.shape), _const_spec(lnb.shape)],
        out_specs=pl.BlockSpec((bsz, t * d), lambda i: (0, 0)),
        out_shape=jax.ShapeDtypeStruct((bsz, t * d), F32),
        compiler_params=pltpu.CompilerParams(dimension_semantics=("arbitrary",),
                                             vmem_limit_bytes=V7X_VMEM_LIMIT),
        name=f"ffn_sample{sub}",
    )(x2, mod, wg, wu, wd, lng, lnb)
    return out.reshape(bsz, t, d)


def _tri_inverse(lmats, c):
    ls = [l.astype(BF16) for l in lmats]
    ms = [jnp.dot(l, l, preferred_element_type=F32) for l in ls]
    qs = ms
    power = 2
    while 2 * power < c:
        mb = [m.astype(BF16) for m in ms]
        ms = [jnp.dot(m, m, preferred_element_type=F32) for m in mb]
        qm = [jnp.dot(q.astype(BF16), m.astype(BF16), preferred_element_type=F32) for q, m in zip(qs, ms)]
        qs = [q + m + x for q, m, x in zip(qs, ms, qm)]
        power *= 2
    rows = lax.broadcasted_iota(jnp.int32, (c, c), 0)
    cols = lax.broadcasted_iota(jnp.int32, (c, c), 1)
    eye = (rows == cols).astype(F32)
    lq = [jnp.dot(l, q.astype(BF16), preferred_element_type=F32) for l, q in zip(ls, qs)]
    return [eye - l + q - x for l, q, x in zip(lmats, qs, lq)]


def _gdn_chunks_local(units, tril, strict):
    c = units[0][1].shape[0]
    a = [lax.dot_general(jnp.concatenate([kb, q], axis=0), k, (((1,), (1,)), ((), ())),
                         preferred_element_type=F32)
         for kb, q, k, _, _, _, _ in units]
    lmats, qks = [], []
    for a_u, (_, _, _, _, _, gcol, grow) in zip(a, units):
        decay = jnp.where(tril, jnp.exp(jnp.where(tril, gcol - grow, 0.0)), 0.0)
        lmats.append(jnp.where(strict, a_u[:c] * decay, 0.0))
        qks.append(jnp.where(tril, a_u[c:] * decay, 0.0))
    tms = _tri_inverse(lmats, c)
    uw = [jnp.dot(tm.astype(BF16), jnp.concatenate([vb, kbe], axis=1), preferred_element_type=F32)
          for tm, (_, _, _, vb, kbe, _, _) in zip(tms, units)]
    return [(x[:, :DN_DV], x[:, DN_DV:], qk) for x, qk in zip(uw, qks)]


def _gdn_chunk_state(units, states):
    c = units[0][2].shape[0]
    r = [jnp.dot(jnp.concatenate([w, q_dec], axis=0), s.astype(BF16), preferred_element_type=F32)
         for (w, q_dec, _, _, _, _), s in zip(units, states)]
    v_new = [(u - r_h[:c]).astype(BF16) for (_, _, u, _, _, _), r_h in zip(units, r)]
    o = [r_h[c:] + jnp.dot(qk, v, preferred_element_type=F32)
         for (_, _, _, qk, _, _), r_h, v in zip(units, r, v_new)]
    s_new = [s * g_last + lax.dot_general(k_dec, v, (((0,), (0,)), ((), ())), preferred_element_type=F32)
             for (_, _, _, _, k_dec, g_last), s, v in zip(units, states, v_new)]
    return o, s_new


def _tri_inverse_steps(lmats, c):
    ls = [l.astype(BF16) for l in lmats]
    ms = [jnp.dot(l, l, preferred_element_type=F32) for l in ls]
    yield
    qs = ms
    power = 2
    while 2 * power < c:
        mb = [m.astype(BF16) for m in ms]
        ms = [jnp.dot(m, m, preferred_element_type=F32) for m in mb]
        yield
        qm = [jnp.dot(q.astype(BF16), m.astype(BF16), preferred_element_type=F32) for q, m in zip(qs, ms)]
        yield
        qs = [q + m + x for q, m, x in zip(qs, ms, qm)]
        power *= 2
    rows = lax.broadcasted_iota(jnp.int32, (c, c), 0)
    cols = lax.broadcasted_iota(jnp.int32, (c, c), 1)
    eye = (rows == cols).astype(F32)
    lq = [jnp.dot(l, q.astype(BF16), preferred_element_type=F32) for l, q in zip(ls, qs)]
    yield
    return [eye - l + q - x for l, q, x in zip(lmats, qs, lq)]


def _gdn_local_steps(units, tril, strict):
    c = units[0][1].shape[0]
    a = [lax.dot_general(jnp.concatenate([kb, q], axis=0), k, (((1,), (1,)), ((), ())),
                         preferred_element_type=F32)
         for kb, q, k, _, _, _, _ in units]
    yield
    lmats, qks = [], []
    for a_u, (_, _, _, _, _, gcol, grow) in zip(a, units):
        decay = jnp.where(tril, jnp.exp(jnp.where(tril, gcol - grow, 0.0)), 0.0)
        lmats.append(jnp.where(strict, a_u[:c] * decay, 0.0))
        qks.append(jnp.where(tril, a_u[c:] * decay, 0.0))
    tms = yield from _tri_inverse_steps(lmats, c)
    uw = [jnp.dot(tm.astype(BF16), jnp.concatenate([vb, kbe], axis=1), preferred_element_type=F32)
          for tm, (_, _, _, vb, kbe, _, _) in zip(tms, units)]
    yield
    return [(x[:, :DN_DV], x[:, DN_DV:], qk) for x, qk in zip(uw, qks)]


def _alternate(primary, secondary):
    result = None
    done = False
    while not done:
        try:
            next(primary)
            yield
        except StopIteration as stop:
            result, done = stop.value, True
        if secondary is not None:
            try:
                next(secondary)
                yield
            except StopIteration:
                secondary = None
    if secondary is not None:
        yield from secondary
    return result


def _run_interleaved(*gens):
    live = list(gens)
    while live:
        for g in list(live):
            try:
                next(g)
            except StopIteration:
                live.remove(g)


def _gate_norm(o, og, norm_g):
    o = o * lax.rsqrt(jnp.mean(o * o, axis=-1, keepdims=True) + RMS_EPS) * norm_g
    return o * _silu(og)


def _proj_layout(dn_width, sc_width):
    off = {}
    pos = 0
    for name, width in (("og", dn_width), ("sc_b", sc_width), ("sc_c", sc_width), ("sc_h", sc_width)):
        off[name] = (pos, pos + width)
        pos += width
    return off


def _project(u, wq_ref, wr_ref, wab_ref, off):
    p = {"qkv": jnp.dot(u, wq_ref[...], preferred_element_type=F32),
         "ab": jnp.dot(u, wab_ref[...], preferred_element_type=F32)}
    rest = jnp.dot(u, wr_ref[...], preferred_element_type=F32)
    for name, (lo, hi) in off.items():
        p[name] = rest[:, lo:hi]
    return p


def _cols(p, off, name):
    del off
    return p[name]


def _mix_out(x, o_dn, o_sc, gate, w_out_ref, g, b, alpha):
    dn_w = o_dn.shape[1]
    mix = (jnp.dot(o_dn.astype(BF16), w_out_ref[0:dn_w, :], preferred_element_type=F32)
           + jnp.dot(o_sc.astype(BF16), w_out_ref[dn_w:, :], preferred_element_type=F32))
    return _post(x, mix, gate, g, b, alpha)


def _mixer_prompt_kernel(x_ref, mod_ref, wq_ref, wr_ref, wab_ref, w_out_ref, cw_ref, mw_ref, alog_ref, dtb_ref, ng_ref,
                         lng_ref, lnb_ref,
                         y_ref, s_out_ref, cq_out_ref, cm_out_ref,
                         s_scr, cq_scr, cm_scr, gct_scr, gl_scr, ops_scr, u_scr, w_scr, qk_scr, o_scr,
                         *, alpha, nseq, tt, off, n_conv, n_mconv):
    t_idx = pl.program_id(1)
    rows_all = nseq * tt
    nchunk = rows_all // CHUNK
    per_seq = tt // CHUNK
    hk = DN_HEADS * DN_DK
    sub = 1

    @pl.when(t_idx == 0)
    def _():
        s_scr[...] = jnp.zeros_like(s_scr)
        cq_scr[...] = jnp.zeros_like(cq_scr)
        cm_scr[...] = jnp.zeros_like(cm_scr)

    xs = [x_ref[s] for s in range(nseq)]
    gates = [mod_ref[s, 3 * sub + 2:3 * sub + 3, :] for s in range(nseq)]
    u = jnp.concatenate(
        [(xs[s] * (1.0 + mod_ref[s, 3 * sub + 1:3 * sub + 2, :]) + mod_ref[s, 3 * sub + 0:3 * sub + 1, :])
         .astype(BF16) for s in range(nseq)], axis=0)
    p = _project(u, wq_ref, wr_ref, wab_ref, off)

    def causal_conv(new_rows, hist_scr, s, w_ref, taps):
        ext = jnp.concatenate([hist_scr[s], new_rows], axis=0)
        acc = new_rows * w_ref[taps - 1:taps, :]
        for j in range(taps - 1):
            acc = acc + pltpu.roll(ext, taps - 1 - j, axis=0)[SUBLANES:] * w_ref[j:j + 1, :]
        hist_scr[s] = new_rows[tt - SUBLANES:]
        return acc

    pq = _cols(p, off, "qkv")
    acts = []
    for s in range(nseq):
        pq_s = pq[s * tt:(s + 1) * tt]
        acts.append(_silu(causal_conv(pq_s, cq_scr, s, cw_ref, n_conv)))
        cq_out_ref[s] = pq_s[tt - (n_conv - 1):]
    act = jnp.concatenate(acts, axis=0)

    ab = _cols(p, off, "ab")
    beta_all = jax.nn.sigmoid(ab)
    g = -jnp.exp(alog_ref[...]) * _softplus(ab + dtb_ref[...])
    rows = lax.broadcasted_iota(jnp.int32, (CHUNK, CHUNK), 0)
    cols = lax.broadcasted_iota(jnp.int32, (CHUNK, CHUNK), 1)
    tril = rows >= cols
    strict = rows > cols
    tril_f = tril.astype(F32)
    gc_parts, gl_parts = [], []
    for c in range(nchunk):
        gc_c = _mm_f32(tril_f, g[c * CHUNK:(c + 1) * CHUNK, :])
        gc_parts.append(gc_c)
        gl_parts.append(jnp.broadcast_to(gc_c[CHUNK - 1:CHUNK, :], (CHUNK, AB_PAD)))
    gc = jnp.concatenate(gc_parts, axis=0)
    gl_rows = jnp.concatenate(gl_parts, axis=0)
    gc_t = gc.T
    for c in range(nchunk):
        gct_scr[c] = gc_t[0:SUBLANES, c * CHUNK:(c + 1) * CHUNK]

    eg_all = jnp.exp(gc)
    ekd_all = jnp.exp(gl_rows - gc)
    egl_all = jnp.exp(gl_rows)
    for h in range(DN_HEADS):
        lanes = slice(h * DN_DK, (h + 1) * DN_DK)
        q = _l2norm(act[:, h * DN_DK:(h + 1) * DN_DK]) * (DN_DK ** -0.5)
        k = _l2norm(act[:, hk + h * DN_DK:hk + (h + 1) * DN_DK])
        v = act[:, 2 * hk + h * DN_DV:2 * hk + (h + 1) * DN_DV]
        beta = beta_all[:, DN_HEADS + h:DN_HEADS + h + 1]
        eg = eg_all[:, h:h + 1]
        kb = k * beta
        ops_scr[0, :, lanes] = kb.astype(BF16)
        ops_scr[1, :, lanes] = q.astype(BF16)
        ops_scr[2, :, lanes] = k.astype(BF16)
        ops_scr[3, :, lanes] = (v * beta).astype(BF16)
        ops_scr[4, :, lanes] = (kb * eg).astype(BF16)
        ops_scr[5, :, lanes] = (q * eg).astype(BF16)
        ops_scr[6, :, lanes] = (k * ekd_all[:, h:h + 1]).astype(BF16)
        gl_scr[:, lanes] = jnp.broadcast_to(egl_all[:, h:h + 1], (rows_all, DN_DV))

    head_lanes = [slice(h * DN_DK, (h + 1) * DN_DK) for h in range(DN_HEADS)]

    where = [(c, h, lanes) for c in range(nchunk) for h, lanes in enumerate(head_lanes)]
    units = [tuple(ops_scr[i, c * CHUNK:(c + 1) * CHUNK, lanes] for i in range(5))
             + (gc[c * CHUNK:(c + 1) * CHUNK, h:h + 1], gct_scr[c, h:h + 1, :])
             for c, h, lanes in where]
    for (c, h, lanes), (u_c, w_c, qk_c) in zip(where, _gdn_chunks_local(units, tril, strict)):
        u_scr[c * CHUNK:(c + 1) * CHUNK, lanes] = u_c
        w_scr[c * CHUNK:(c + 1) * CHUNK, lanes] = w_c.astype(BF16)
        qk_scr[c * DN_HEADS + h] = qk_c.astype(BF16)

    for j in range(per_seq):
        chains = [(s, s * per_seq + j, h, lanes) for s in range(nseq) for h, lanes in enumerate(head_lanes)]
        units = [(w_scr[c * CHUNK:(c + 1) * CHUNK, lanes], ops_scr[5, c * CHUNK:(c + 1) * CHUNK, lanes],
                  u_scr[c * CHUNK:(c + 1) * CHUNK, lanes], qk_scr[c * DN_HEADS + h],
                  ops_scr[6, c * CHUNK:(c + 1) * CHUNK, lanes], gl_scr[c * CHUNK:c * CHUNK + 1, lanes])
                 for s, c, h, lanes in chains]
        o_c, s_new = _gdn_chunk_state(units, [s_scr[s * DN_HEADS + h] for s, _, h, _ in chains])
        for (s, c, h, lanes), o_u, s_u in zip(chains, o_c, s_new):
            s_scr[s * DN_HEADS + h] = s_u
            o_scr[c * CHUNK:(c + 1) * CHUNK, lanes] = o_u
    for s in range(nseq):
        s_out_ref[s] = s_scr[s * DN_HEADS:(s + 1) * DN_HEADS]

    og = _cols(p, off, "og")
    o_dn = jnp.concatenate(
        [_gate_norm(o_scr[:, h * DN_DV:(h + 1) * DN_DV], og[:, h * DN_DV:(h + 1) * DN_DV], ng_ref[...])
         for h in range(DN_HEADS)], axis=1)

    z = _cols(p, off, "sc_c") * _cols(p, off, "sc_h")
    zcs = []
    for s in range(nseq):
        z_s = z[s * tt:(s + 1) * tt]
        zcs.append(causal_conv(z_s, cm_scr, s, mw_ref, n_mconv))
        cm_out_ref[s] = z_s[tt - (n_mconv - 1):]
    o_sc = _cols(p, off, "sc_b") * jnp.concatenate(zcs, axis=0)

    dn_w = o_dn.shape[1]
    mix = (jnp.dot(o_dn.astype(BF16), w_out_ref[0:dn_w, :], preferred_element_type=F32)
           + jnp.dot(o_sc.astype(BF16), w_out_ref[dn_w:, :], preferred_element_type=F32))
    for s in range(nseq):
        y_ref[s] = _post(xs[s], mix[s * tt:(s + 1) * tt], gates[s],
                         lng_ref[sub:sub + 1, :], lnb_ref[sub:sub + 1, :], alpha)


def _mixer_prompt(x, mod, w_in, w_out, conv_w, mconv_w, alog, dtb, norm_g, lng, lnb, *, alpha, nseq, tt, off):
    bsz, t, d = x.shape
    assert t % tt == 0 and tt % LANES == 0 and bsz % nseq == 0
    nt = t // tt
    rows = nseq * tt
    qkv_dim = conv_w.shape[-1]
    sc_w = off["sc_c"][1] - off["sc_c"][0]
    dn_w = DN_HEADS * DN_DV
    n_conv, n_mconv = conv_w.shape[0], mconv_w.shape[0]
    assert max(n_conv, n_mconv) - 1 <= SUBLANES <= tt
    nchunk = rows // CHUNK
    kern = functools.partial(_mixer_prompt_kernel, alpha=alpha, nseq=nseq, tt=tt, off=off,
                             n_conv=n_conv, n_mconv=n_mconv)
    y, s_out, cq, cm = pl.pallas_call(
        kern,
        grid=(bsz // nseq, nt),
        in_specs=[pl.BlockSpec((nseq, tt, d), lambda b, i: (b, i, 0)),
                  pl.BlockSpec((nseq, 3 * N_SUB, d), lambda b, i: (b, 0, 0)),
                  *[_const_spec(w.shape) for w in w_in], _const_spec(w_out.shape),
                  _const_spec(conv_w.shape), _const_spec(mconv_w.shape),
                  _const_spec(alog.shape), _const_spec(dtb.shape), _const_spec(norm_g.shape),
                  _const_spec(lng.shape), _const_spec(lnb.shape)],
        out_specs=[pl.BlockSpec((nseq, tt, d), lambda b, i: (b, i, 0)),
                   pl.BlockSpec((nseq, DN_HEADS, DN_DK, DN_DV), lambda b, i: (b, 0, 0, 0)),
                   pl.BlockSpec((nseq, n_conv - 1, qkv_dim), lambda b, i: (b, 0, 0)),
                   pl.BlockSpec((nseq, n_mconv - 1, sc_w), lambda b, i: (b, 0, 0))],
        out_shape=[jax.ShapeDtypeStruct((bsz, t, d), F32),
                   jax.ShapeDtypeStruct((bsz, DN_HEADS, DN_DK, DN_DV), F32),
                   jax.ShapeDtypeStruct((bsz, n_conv - 1, qkv_dim), F32),
                   jax.ShapeDtypeStruct((bsz, n_mconv - 1, sc_w), F32)],
        scratch_shapes=[pltpu.VMEM((nseq * DN_HEADS, DN_DK, DN_DV), F32),
                        pltpu.VMEM((nseq, SUBLANES, qkv_dim), F32),
                        pltpu.VMEM((nseq, SUBLANES, sc_w), F32),
                        pltpu.VMEM((nchunk, SUBLANES, CHUNK), F32),
                        pltpu.VMEM((rows, dn_w), F32),
                        pltpu.VMEM((7, rows, dn_w), BF16),
                        pltpu.VMEM((rows, dn_w), F32),
                        pltpu.VMEM((rows, dn_w), BF16),
                        pltpu.VMEM((nchunk * DN_HEADS, CHUNK, CHUNK), BF16),
                        pltpu.VMEM((rows, dn_w), F32)],
        compiler_params=pltpu.CompilerParams(dimension_semantics=("arbitrary", "arbitrary"),
                                             vmem_limit_bytes=V7X_VMEM_LIMIT),
        name="mixer_prompt",
    )(x, mod, *w_in, w_out, conv_w, mconv_w, alog, dtb, norm_g, lng, lnb)
    return y, s_out, cq, cm


def _mixer_prompt_pipe_kernel(xf_ref, modf_ref, xb_ref, modb_ref, wq_ref, wr_ref, wab_ref, w_out_ref, cw_ref,
                              mw_ref, alog_ref, dtb_ref, ng_ref, lng_ref, lnb_ref,
                              y_ref, s_out_ref, cq_out_ref, cm_out_ref,
                              s_scr, cq_scr, cm_scr, gct_scr,
                              u_scr, w_scr, qk_scr, qd_scr, kd_scr, gl_scr, og_scr, osc_scr,
                              *, alpha, nseq, tt, tiles_per_group, n_tiles, off, n_conv, n_mconv):
    k = pl.program_id(0)
    rows_all = nseq * tt
    nchunk = rows_all // CHUNK
    per_seq = tt // CHUNK
    hk = DN_HEADS * DN_DK
    sub = 1
    head_lanes = [slice(h * DN_DK, (h + 1) * DN_DK) for h in range(DN_HEADS)]
    t_front = lax.rem(jnp.minimum(k, n_tiles - 1), tiles_per_group)
    t_back = lax.rem(jnp.maximum(k - 1, 0), tiles_per_group)
    keep_front = jnp.where(t_front == 0, 0.0, 1.0).astype(F32)
    keep_back = jnp.where(t_back == 0, 0.0, 1.0).astype(F32)

    def causal_conv(new_rows, hist_scr, s, w_ref, taps):
        ext = jnp.concatenate([hist_scr[s] * keep_front, new_rows], axis=0)
        acc = new_rows * w_ref[taps - 1:taps, :]
        for j in range(taps - 1):
            acc = acc + pltpu.roll(ext, taps - 1 - j, axis=0)[SUBLANES:] * w_ref[j:j + 1, :]
        hist_scr[s] = new_rows[tt - SUBLANES:]
        return acc

    def front():
        u = jnp.concatenate(
            [(xf_ref[s] * (1.0 + modf_ref[s, 3 * sub + 1:3 * sub + 2, :])
              + modf_ref[s, 3 * sub + 0:3 * sub + 1, :]).astype(BF16) for s in range(nseq)], axis=0)
        ab = jnp.dot(u, wab_ref[...], preferred_element_type=F32)
        pq = jnp.dot(u, wq_ref[...], preferred_element_type=F32)
        yield

        beta_all = jax.nn.sigmoid(ab)
        g = -jnp.exp(alog_ref[...]) * _softplus(ab + dtb_ref[...])
        rows = lax.broadcasted_iota(jnp.int32, (CHUNK, CHUNK), 0)
        cols = lax.broadcasted_iota(jnp.int32, (CHUNK, CHUNK), 1)
        tril = rows >= cols
        strict = rows > cols
        tril_f = tril.astype(F32)
        gc_parts, gl_parts = [], []
        for c in range(nchunk):
            gc_c = _mm_f32(tril_f, g[c * CHUNK:(c + 1) * CHUNK, :])
            gc_parts.append(gc_c)
            gl_parts.append(jnp.broadcast_to(gc_c[CHUNK - 1:CHUNK, :], (CHUNK, AB_PAD)))
        yield
        p = {}

        def project_rest(names):
            for name in names:
                lo, hi = off[name]
                p[name] = jnp.dot(u, wr_ref[:, lo:hi], preferred_element_type=F32)
                yield

        yield from project_rest(("sc_c", "sc_h"))
        gc = jnp.concatenate(gc_parts, axis=0)
        gl_rows = jnp.concatenate(gl_parts, axis=0)
        gc_t = gc.T
        for c in range(nchunk):
            gct_scr[c] = gc_t[0:SUBLANES, c * CHUNK:(c + 1) * CHUNK]

        acts = []
        for s in range(nseq):
            pq_s = pq[s * tt:(s + 1) * tt]
            acts.append(_silu(causal_conv(pq_s, cq_scr, s, cw_ref, n_conv)))
            cq_out_ref[s] = pq_s[tt - (n_conv - 1):]
        act = jnp.concatenate(acts, axis=0)

        eg_all = jnp.exp(gc)
        ekd_all = jnp.exp(gl_rows - gc)
        egl_all = jnp.exp(gl_rows)
        per_head = []
        for h, lanes in enumerate(head_lanes):
            q = _l2norm(act[:, h * DN_DK:(h + 1) * DN_DK]) * (DN_DK ** -0.5)
            kk = _l2norm(act[:, hk + h * DN_DK:hk + (h + 1) * DN_DK])
            v = act[:, 2 * hk + h * DN_DV:2 * hk + (h + 1) * DN_DV]
            beta = beta_all[:, DN_HEADS + h:DN_HEADS + h + 1]
            eg = eg_all[:, h:h + 1]
            kb = kk * beta
            per_head.append((kb.astype(BF16), q.astype(BF16), kk.astype(BF16), (v * beta).astype(BF16),
                             (kb * eg).astype(BF16)))
            qd_scr[:, lanes] = (q * eg).astype(BF16)
            kd_scr[:, lanes] = (kk * ekd_all[:, h:h + 1]).astype(BF16)
            for c in range(nchunk):
                gl_scr[c * SUBLANES:(c + 1) * SUBLANES, lanes] = jnp.broadcast_to(
                    egl_all[c * CHUNK:c * CHUNK + SUBLANES, h:h + 1], (SUBLANES, DN_DV))

        z = p["sc_c"] * p["sc_h"]
        zcs = []
        for s in range(nseq):
            z_s = z[s * tt:(s + 1) * tt]
            zcs.append(causal_conv(z_s, cm_scr, s, mw_ref, n_mconv))
            cm_out_ref[s] = z_s[tt - (n_mconv - 1):]

        where = [(c, h, lanes) for c in range(nchunk) for h, lanes in enumerate(head_lanes)]
        units = [tuple(arr[c * CHUNK:(c + 1) * CHUNK] for arr in per_head[h])
                 + (gc[c * CHUNK:(c + 1) * CHUNK, h:h + 1], gct_scr[c, h:h + 1, :])
                 for c, h, lanes in where]
        local = yield from _alternate(_gdn_local_steps(units, tril, strict), project_rest(("sc_b", "og")))
        osc_scr[...] = p["sc_b"] * jnp.concatenate(zcs, axis=0)
        og_scr[...] = p["og"]
        for (c, h, lanes), (u_c, w_c, qk_c) in zip(where, local):
            u_scr[c * CHUNK:(c + 1) * CHUNK, lanes] = u_c
            w_scr[c * CHUNK:(c + 1) * CHUNK, lanes] = w_c.astype(BF16)
            qk_scr[c * DN_HEADS + h] = qk_c.astype(BF16)

    def back():
        u_all, w_all, qd_all, kd_all = u_scr[...], w_scr[...], qd_scr[...], kd_scr[...]
        gl_all, og, o_sc = gl_scr[...], og_scr[...], osc_scr[...]
        qks = [qk_scr[i] for i in range(nchunk * DN_HEADS)]
        xs = [xb_ref[s] for s in range(nseq)]
        gates = [modb_ref[s, 3 * sub + 2:3 * sub + 3, :] for s in range(nseq)]
        states = [s_scr[i] * keep_back for i in range(nseq * DN_HEADS)]
        o_rows = [[None] * DN_HEADS for _ in range(nchunk)]
        for j in range(per_seq):
            chains = [(s, s * per_seq + j, h, lanes) for s in range(nseq) for h, lanes in enumerate(head_lanes)]
            rws = [slice(c * CHUNK, (c + 1) * CHUNK) for _, c, _, _ in chains]
            r = [jnp.dot(jnp.concatenate([w_all[rw, lanes], qd_all[rw, lanes]], axis=0),
                         states[s * DN_HEADS + h].astype(BF16), preferred_element_type=F32)
                 for (s, c, h, lanes), rw in zip(chains, rws)]
            yield
            v_new = [(u_all[rw, lanes] - r_u[:CHUNK]).astype(BF16)
                     for (s, c, h, lanes), rw, r_u in zip(chains, rws, r)]
            for (s, c, h, lanes), r_u, v_u in zip(chains, r, v_new):
                o_rows[c][h] = r_u[CHUNK:] + jnp.dot(qks[c * DN_HEADS + h], v_u, preferred_element_type=F32)
            for (s, c, h, lanes), rw, v_u in zip(chains, rws, v_new):
                i = s * DN_HEADS + h
                states[i] = (states[i] * gl_all[c * SUBLANES:c * SUBLANES + 1, lanes]
                             + lax.dot_general(kd_all[rw, lanes], v_u, (((0,), (0,)), ((), ())),
                                               preferred_element_type=F32))
            yield
        for i, st in enumerate(states):
            s_scr[i] = st
        for s in range(nseq):
            for h in range(DN_HEADS):
                s_out_ref[s, h] = states[s * DN_HEADS + h]

        o_dn = jnp.concatenate(
            [_gate_norm(jnp.concatenate([o_rows[c][h] for c in range(nchunk)], axis=0),
                        og[:, lanes], ng_ref[...]) for h, lanes in enumerate(head_lanes)], axis=1)
        dn_w = o_dn.shape[1]
        mix = (jnp.dot(o_dn.astype(BF16), w_out_ref[0:dn_w, :], preferred_element_type=F32)
               + jnp.dot(o_sc.astype(BF16), w_out_ref[dn_w:, :], preferred_element_type=F32))
        yield
        for s in range(nseq):
            y_ref[s] = _post(xs[s], mix[s * tt:(s + 1) * tt], gates[s],
                             lng_ref[sub:sub + 1, :], lnb_ref[sub:sub + 1, :], alpha)

    @pl.when(k == 0)
    def _():
        s_scr[...] = jnp.zeros_like(s_scr)
        cq_scr[...] = jnp.zeros_like(cq_scr)
        cm_scr[...] = jnp.zeros_like(cm_scr)
        _run_interleaved(front())

    @pl.when(jnp.logical_and(k > 0, k < n_tiles))
    def _():
        _run_interleaved(back(), front())

    @pl.when(k == n_tiles)
    def _():
        _run_interleaved(back())


def _mixer_prompt_pipe(x, mod, w_in, w_out, conv_w, mconv_w, alog, dtb, norm_g, lng, lnb, *, alpha, nseq, tt, off):
    bsz, t, d = x.shape
    assert t % tt == 0 and tt % LANES == 0 and bsz % nseq == 0
    nt = t // tt
    n_tiles = (bsz // nseq) * nt
    rows = nseq * tt
    qkv_dim = conv_w.shape[-1]
    sc_w = off["sc_c"][1] - off["sc_c"][0]
    dn_w = DN_HEADS * DN_DV
    n_conv, n_mconv = conv_w.shape[0], mconv_w.shape[0]
    assert max(n_conv, n_mconv) - 1 <= SUBLANES <= tt
    nchunk = rows // CHUNK
    kern = functools.partial(_mixer_prompt_pipe_kernel, alpha=alpha, nseq=nseq, tt=tt, tiles_per_group=nt,
                             n_tiles=n_tiles, off=off, n_conv=n_conv, n_mconv=n_mconv)

    def front_tile(k):
        return jnp.minimum(k, n_tiles - 1)

    def back_tile(k):
        return jnp.maximum(k - 1, 0)

    y, s_out, cq, cm = pl.pallas_call(
        kern,
        grid=(n_tiles + 1,),
        in_specs=[pl.BlockSpec((nseq, tt, d), lambda k: (front_tile(k) // nt, front_tile(k) % nt, 0)),
                  pl.BlockSpec((nseq, 3 * N_SUB, d), lambda k: (front_tile(k) // nt, 0, 0)),
                  pl.BlockSpec((nseq, tt, d), lambda k: (back_tile(k) // nt, back_tile(k) % nt, 0)),
                  pl.BlockSpec((nseq, 3 * N_SUB, d), lambda k: (back_tile(k) // nt, 0, 0)),
                  *[_const_spec(w.shape) for w in w_in], _const_spec(w_out.shape),
                  _const_spec(conv_w.shape), _const_spec(mconv_w.shape),
                  _const_spec(alog.shape), _const_spec(dtb.shape), _const_spec(norm_g.shape),
                  _const_spec(lng.shape), _const_spec(lnb.shape)],
        out_specs=[pl.BlockSpec((nseq, tt, d), lambda k: (back_tile(k) // nt, back_tile(k) % nt, 0)),
                   pl.BlockSpec((nseq, DN_HEADS, DN_DK, DN_DV), lambda k: (back_tile(k) // nt, 0, 0, 0)),
                   pl.BlockSpec((nseq, n_conv - 1, qkv_dim), lambda k: (front_tile(k) // nt, 0, 0)),
                   pl.BlockSpec((nseq, n_mconv - 1, sc_w), lambda k: (front_tile(k) // nt, 0, 0))],
        out_shape=[jax.ShapeDtypeStruct((bsz, t, d), F32),
                   jax.ShapeDtypeStruct((bsz, DN_HEADS, DN_DK, DN_DV), F32),
                   jax.ShapeDtypeStruct((bsz, n_conv - 1, qkv_dim), F32),
                   jax.ShapeDtypeStruct((bsz, n_mconv - 1, sc_w), F32)],
        scratch_shapes=[pltpu.VMEM((nseq * DN_HEADS, DN_DK, DN_DV), F32),
                        pltpu.VMEM((nseq, SUBLANES, qkv_dim), F32),
                        pltpu.VMEM((nseq, SUBLANES, sc_w), F32),
                        pltpu.VMEM((nchunk, SUBLANES, CHUNK), F32),
                        pltpu.VMEM((rows, dn_w), F32),
                        pltpu.VMEM((rows, dn_w), BF16),
                        pltpu.VMEM((nchunk * DN_HEADS, CHUNK, CHUNK), BF16),
                        pltpu.VMEM((rows, dn_w), BF16),
                        pltpu.VMEM((rows, dn_w), BF16),
                        pltpu.VMEM((nchunk * SUBLANES, dn_w), F32),
                        pltpu.VMEM((rows, dn_w), F32),
                        pltpu.VMEM((rows, sc_w), F32)],
        compiler_params=pltpu.CompilerParams(dimension_semantics=("arbitrary",),
                                             vmem_limit_bytes=V7X_VMEM_LIMIT),
        name="mixer_prompt",
    )(x, mod, x, mod, *w_in, w_out, conv_w, mconv_w, alog, dtb, norm_g, lng, lnb)
    return y, s_out, cq, cm


_PACK = SUBLANES


def _sample_phase_a(x_ref, mod_ref, wq_ref, wr_ref, wab_ref, cw_ref, mw_ref, alog_ref, dtb_ref, cq_in_ref, cm_in_ref,
                    cq_out_ref, cm_out_ref, og_scr, osc_scr, lhs_scr, u_scr, kd_scr, gl_scr, qk_scr,
                    *, t_len, d, off, n_conv, n_mconv):
    bsz = x_ref.shape[0]
    hk = DN_HEADS * DN_DK
    qkv_dim = cw_ref.shape[-1]
    sc_w = off["sc_c"][1] - off["sc_c"][0]

    x = _stack_time(x_ref, t_len, d)
    u = (x * (1.0 + _mod_rows(mod_ref, 1, t_len, d)) + _mod_rows(mod_ref, 0, t_len, d)).astype(BF16)
    p = _project(u, wq_ref, wr_ref, wab_ref, off)

    def rows(arr, t):
        return arr[t * bsz:(t + 1) * bsz]

    pq = _cols(p, off, "qkv")
    seq = [cq_in_ref[j] for j in range(n_conv - 1)] + [rows(pq, t) for t in range(t_len)]
    act = []
    for t in range(t_len):
        acc = None
        for j in range(n_conv):
            term = seq[t + j] * cw_ref[j:j + 1, :]
            acc = term if acc is None else acc + term
        act.append(_silu(acc))
    for j in range(n_conv - 1):
        cq_out_ref[j] = seq[t_len + j]

    z = _cols(p, off, "sc_c") * _cols(p, off, "sc_h")
    zseq = ([cm_in_ref[:, j * sc_w:(j + 1) * sc_w] for j in range(n_mconv - 1)]
            + [rows(z, t) for t in range(t_len)])
    zc = []
    for t in range(t_len):
        acc = None
        for j in range(n_mconv):
            term = zseq[t + j] * mw_ref[j:j + 1, :]
            acc = term if acc is None else acc + term
        zc.append(acc)
    for j in range(n_mconv - 1):
        cm_out_ref[:, j * sc_w:(j + 1) * sc_w] = zseq[t_len + j]
    osc_scr[...] = _cols(p, off, "sc_b") * jnp.concatenate(zc, axis=0)
    og_scr[...] = _cols(p, off, "og")

    ab = _cols(p, off, "ab")
    beta_all = jax.nn.sigmoid(ab)
    g_all = -jnp.exp(alog_ref[...]) * _softplus(ab + dtb_ref[...])
    zero = jnp.zeros((bsz, LANES), F32)

    for h in range(DN_HEADS):
        q = [_l2norm(a[:, h * DN_DK:(h + 1) * DN_DK]) * (DN_DK ** -0.5) for a in act]
        k = [_l2norm(a[:, hk + h * DN_DK:hk + (h + 1) * DN_DK]) for a in act]
        v = [a[:, 2 * hk + h * DN_DV:2 * hk + (h + 1) * DN_DV] for a in act]
        beta = [rows(beta_all, t)[:, DN_HEADS + h:DN_HEADS + h + 1] for t in range(t_len)]
        g = [rows(g_all, t)[:, h:h + 1] for t in range(t_len)]
        gc = [g[0]]
        for t in range(1, t_len):
            gc.append(gc[-1] + g[t])
        kb = [k[t] * beta[t] for t in range(t_len)]
        vb = [v[t] * beta[t] for t in range(t_len)]
        eg = [jnp.exp(gc[t]) for t in range(t_len)]
        kbe = [kb[t] * eg[t] for t in range(t_len)]
        lm = [[None] * t_len for _ in range(t_len)]
        n_qk = 0
        for i in range(t_len):
            for j in range(i + 1):
                dec = jnp.exp(gc[i] - gc[j])
                qk_scr[h, n_qk] = jnp.broadcast_to(_rowsum(q[i] * k[j]) * dec, (bsz, LANES))
                n_qk += 1
                if j < i:
                    lm[i][j] = _rowsum(kb[i] * k[j]) * dec
        tm = [[None] * t_len for _ in range(t_len)]
        for i in range(t_len):
            for j in range(i):
                acc = lm[i][j]
                for m in range(j + 1, i):
                    acc = acc + lm[i][m] * tm[m][j]
                tm[i][j] = -acc
        for i in range(t_len):
            ui, wi = vb[i], kbe[i]
            for j in range(i):
                ui = ui + tm[i][j] * vb[j]
                wi = wi + tm[i][j] * kbe[j]
            lhs_scr[h, pl.ds(i, bsz, stride=_PACK), :] = wi
            lhs_scr[h, pl.ds(t_len + i, bsz, stride=_PACK), :] = q[i] * eg[i]
            u_scr[h, pl.ds(i, bsz, stride=_PACK), :] = ui
            u_scr[h, pl.ds(t_len + i, bsz, stride=_PACK), :] = zero
            kd_scr[h, pl.ds(i, bsz, stride=_PACK), :] = k[i] * jnp.exp(gc[t_len - 1] - gc[i])
            kd_scr[h, pl.ds(t_len + i, bsz, stride=_PACK), :] = zero
        gl_scr[h] = jnp.broadcast_to(jnp.exp(gc[t_len - 1]), (bsz, LANES))


def _sample_phase_c(x_ref, mod_ref, w_out_ref, ng_ref, lng_ref, lnb_ref, y_ref,
                    og_scr, osc_scr, res_scr, vn_scr, qk_scr, *, alpha, t_len, d):
    bsz = x_ref.shape[0]
    sub = 1
    x = _stack_time(x_ref, t_len, d)
    gate = _mod_rows(mod_ref, 2, t_len, d)
    og = og_scr[...]
    per_t = []
    for i in range(t_len):
        heads = []
        for h in range(DN_HEADS):
            oi = res_scr[h, pl.ds(t_len + i, bsz, stride=_PACK), :]
            base = i * (i + 1) // 2
            for j in range(i + 1):
                oi = oi + qk_scr[h, base + j] * vn_scr[h, pl.ds(j, bsz, stride=_PACK), :]
            heads.append(_gate_norm(oi, og[i * bsz:(i + 1) * bsz, h * DN_DV:(h + 1) * DN_DV], ng_ref[...]))
        per_t.append(jnp.concatenate(heads, axis=1))
    o_dn = jnp.concatenate(per_t, axis=0)
    y = _mix_out(x, o_dn, osc_scr[...], gate, w_out_ref,
                 lng_ref[sub:sub + 1, :], lnb_ref[sub:sub + 1, :], alpha)
    for t in range(t_len):
        y_ref[:, t * d:(t + 1) * d] = y[t * bsz:(t + 1) * bsz]


def _mixer_sample_kernel(x_ref, mod_ref, wq_ref, wr_ref, wab_ref, w_out_ref, cw_ref, mw_ref, alog_ref, dtb_ref, ng_ref,
                         lng_ref, lnb_ref, cq_in_ref, cm_in_ref, s_in_ref,
                         y_ref, cq_out_ref, cm_out_ref, s_out_ref,
                         og_scr, osc_scr, lhs_scr, u_scr, kd_scr, gl_scr, qk_scr, res_scr, vn_scr,
                         *, alpha, t_len, d, off, n_conv, n_mconv, bt):
    step = pl.program_id(0)
    assert 2 * t_len == _PACK, "packed buffers hold [T rows | T rows] per sequence"

    @pl.when(step == 0)
    def _():
        _sample_phase_a(x_ref, mod_ref, wq_ref, wr_ref, wab_ref, cw_ref, mw_ref, alog_ref, dtb_ref, cq_in_ref, cm_in_ref,
                        cq_out_ref, cm_out_ref, og_scr, osc_scr, lhs_scr, u_scr, kd_scr, gl_scr, qk_scr,
                        t_len=t_len, d=d, off=off, n_conv=n_conv, n_mconv=n_mconv)

    top = (lax.broadcasted_iota(jnp.int32, (_PACK, LANES), 0) < t_len).astype(F32)

    chains = [(bl, h) for bl in range(bt) for h in range(DN_HEADS)]
    seq0 = step * bt
    row0 = [pl.multiple_of((seq0 + bl) * _PACK, _PACK) for bl in range(bt)]
    states = [s_in_ref[bl, h] for bl, h in chains]
    r = [_mm(lhs_scr[h, pl.ds(row0[bl], _PACK), :], s) for (bl, h), s in zip(chains, states)]
    vns = []
    for (bl, h), r_u in zip(chains, r):
        res_scr[h, pl.ds(row0[bl], _PACK), :] = r_u
        vn = (u_scr[h, pl.ds(row0[bl], _PACK), :] - r_u) * top
        vn_scr[h, pl.ds(row0[bl], _PACK), :] = vn
        vns.append(vn)
    upd = [_mm_at(kd_scr[h, pl.ds(row0[bl], _PACK), :], vn) for (bl, h), vn in zip(chains, vns)]
    for (bl, h), s, s_add in zip(chains, states, upd):
        s_out_ref[bl, h] = s * gl_scr[h, pl.ds(seq0 + bl, 1), :] + s_add

    @pl.when(step == pl.num_programs(0) - 1)
    def _():
        _sample_phase_c(x_ref, mod_ref, w_out_ref, ng_ref, lng_ref, lnb_ref, y_ref,
                        og_scr, osc_scr, res_scr, vn_scr, qk_scr, alpha=alpha, t_len=t_len, d=d)


def _mixer_sample(x, mod, w_in, w_out, conv_w, mconv_w, alog, dtb, norm_g, lng, lnb, s_in, cq_in, cm_in,
                  *, alpha, off, bt):
    bsz, t, d = x.shape
    assert bsz % bt == 0
    qkv_dim = conv_w.shape[-1]
    sc_w = off["sc_c"][1] - off["sc_c"][0]
    dn_w = DN_HEADS * DN_DV
    n_conv, n_mconv = conv_w.shape[0], mconv_w.shape[0]
    n_qk = t * (t + 1) // 2
    x2 = x.reshape(bsz, t * d)
    cq2 = jnp.swapaxes(cq_in, 0, 1)
    cm2 = cm_in.reshape(bsz, (n_mconv - 1) * sc_w)
    kern = functools.partial(_mixer_sample_kernel, alpha=alpha, t_len=t, d=d, off=off,
                             n_conv=n_conv, n_mconv=n_mconv, bt=bt)
    consts = (*w_in, w_out, conv_w, mconv_w, alog, dtb, norm_g, lng, lnb, cq2, cm2)
    state_spec = pl.BlockSpec((bt, DN_HEADS, DN_DK, DN_DV), lambda i: (i, 0, 0, 0))
    packed = pltpu.VMEM((DN_HEADS, _PACK * bsz, LANES), F32)
    y, cq, cm, s_out = pl.pallas_call(
        kern,
        grid=(bsz // bt,),
        in_specs=([_const_spec(x2.shape), _sub_mod_spec(bsz, d, 1)]
                  + [_const_spec(a.shape) for a in consts] + [state_spec]),
        out_specs=[pl.BlockSpec(x2.shape, lambda i: (0, 0)),
                   pl.BlockSpec(cq2.shape, lambda i: (0, 0, 0)),
                   pl.BlockSpec(cm2.shape, lambda i: (0, 0)),
                   state_spec],
        out_shape=[jax.ShapeDtypeStruct(x2.shape, F32),
                   jax.ShapeDtypeStruct(cq2.shape, F32),
                   jax.ShapeDtypeStruct(cm2.shape, F32),
                   jax.ShapeDtypeStruct(s_in.shape, F32)],
        scratch_shapes=[pltpu.VMEM((t * bsz, dn_w), F32),
                        pltpu.VMEM((t * bsz, sc_w), F32),
                        packed,
                        packed,
                        packed,
                        pltpu.VMEM((DN_HEADS, bsz, LANES), F32),
                        pltpu.VMEM((DN_HEADS, n_qk, bsz, LANES), F32),
                        packed,
                        packed],
        compiler_params=pltpu.CompilerParams(dimension_semantics=("arbitrary",),
                                             vmem_limit_bytes=V7X_VMEM_LIMIT),
        name="mixer_sample",
    )(x2, mod, *consts, s_in)
    return y.reshape(bsz, t, d), s_out, jnp.swapaxes(cq, 0, 1), cm.reshape(bsz, n_mconv - 1, sc_w)


def _split_w_in(w_in, qkv_dim):
    c1 = qkv_dim + 2 * DN_HEADS
    wab = jnp.pad(w_in[:, qkv_dim:c1], ((0, 0), (0, AB_PAD - 2 * DN_HEADS)))
    return w_in[:, :qkv_dim].astype(BF16), w_in[:, c1:].astype(BF16), wab.astype(BF16)


def _pad_lanes(v):
    return jnp.zeros((1, AB_PAD), F32).at[0, :v.shape[0]].set(v)


def kernel(x_prompt, x_sample, state_ssm, state_conv_qkv, state_conv_mix, c_prompt, c_sample, w_ada, b_ada, ln_g, ln_b, ffn1_wg, ffn1_wu, ffn1_wd, ffn2_wg, ffn2_wu, ffn2_wd, w_in, conv_qkv_w, a_log, dt_bias, dn_norm_g, conv_mix_w, w_out):
    depth = w_ada.shape[0]
    alpha = (2 * depth) ** 0.25
    bp, tp, d = x_prompt.shape
    qkv_dim = conv_qkv_w.shape[-1]
    sc_width = conv_mix_w.shape[-1]
    dn_width = DN_HEADS * DN_DV
    off = _proj_layout(dn_width, sc_width)
    tm = min(512, tp)
    nseq = 2 if bp % 2 == 0 else 1
    tt = min(512 // nseq, tp)
    bt = min(8, x_sample.shape[0])

    hp, hs = x_prompt, x_sample
    outs = [[] for _ in range(6)]
    for l in range(depth):
        mod_p, mod_s, *w_in_p = _adaln(c_prompt, c_sample, w_ada[l], b_ada[l].reshape(1, -1),
                                       jnp.swapaxes(w_in[l], 0, 1), qkv_dim)
        mod_p = mod_p.reshape(bp, 3 * N_SUB, d)
        lng, lnb = ln_g[l], ln_b[l]
        w1 = (ffn1_wg[l], ffn1_wu[l], ffn1_wd[l])
        w2 = (ffn2_wg[l], ffn2_wu[l], ffn2_wd[l])
        w_out_b = w_out[l].astype(BF16)
        alog, dtb = _pad_lanes(a_log[l]), _pad_lanes(dt_bias[l])
        norm_g = dn_norm_g[l].reshape(1, -1)
        mix_args = (w_in_p, w_out_b, conv_qkv_w[l], conv_mix_w[l], alog, dtb, norm_g, lng, lnb)

        hp, hs = _ffn(hp, hs, mod_p, mod_s, *w1, lng, lnb, sub=0, alpha=alpha, tm=tm)
        hp, a1, a2, a3 = _mixer_prompt_pipe(hp, mod_p, *mix_args, alpha=alpha, nseq=nseq, tt=tt, off=off)
        hs, b1, b2, b3 = _mixer_sample(hs, mod_s, *mix_args, state_ssm[l], state_conv_qkv[l],
                                       state_conv_mix[l], alpha=alpha, off=off, bt=bt)
        hp, hs = _ffn(hp, hs, mod_p, mod_s, *w2, lng, lnb, sub=2, alpha=alpha, tm=tm)
        for lst, val in zip(outs, (a1, a2, a3, b1, b2, b3)):
            lst.append(val)
    return (hp, hs) + tuple(jnp.stack(o) for o in outs)
```

```python
import functools

import jax
import jax.numpy as jnp
from jax import lax
from jax.experimental import pallas as pl
from jax.experimental.pallas import tpu as pltpu

F32 = jnp.float32
BF16 = jnp.bfloat16

LN_EPS = 1e-5
RMS_EPS = 1e-6
N_SUB = 3
DN_HEADS = 4
DN_DK = 128
DN_DV = 128
CHUNK = 64
LANES = 128
SUBLANES = 8
AB_PAD = LANES

V7X_VMEM_LIMIT = 60000 * 1024
CAST_BUFFERS = 4


def _mm(a, b):
    return jnp.dot(a.astype(BF16), b.astype(BF16), preferred_element_type=F32)


def _mm_bt(a, b):
    return lax.dot_general(a.astype(BF16), b.astype(BF16), (((1,), (1,)), ((), ())),
                           preferred_element_type=F32)


def _mm_at(a, b):
    return lax.dot_general(a.astype(BF16), b.astype(BF16), (((0,), (0,)), ((), ())),
                           preferred_element_type=F32)


def _mm_f32(a, b):
    return jnp.dot(a, b, preferred_element_type=F32, precision=lax.Precision.HIGHEST)


def _silu(x):
    return x * jax.nn.sigmoid(x)


def _softplus(x):
    return jnp.maximum(x, 0.0) + jnp.log1p(jnp.exp(-jnp.abs(x)))


def _layer_norm(y, g, b):
    mu = jnp.mean(y, axis=-1, keepdims=True)
    yc = y - mu
    var = jnp.mean(yc * yc, axis=-1, keepdims=True)
    return yc * lax.rsqrt(var + LN_EPS) * g + b


def _post(x, delta, gate, g, b, alpha):
    return _layer_norm(alpha * x + gate * delta, g, b)


def _rowsum(x):
    return jnp.sum(x, axis=-1, keepdims=True)


def _l2norm(x):
    return x * lax.rsqrt(_rowsum(x * x) + RMS_EPS)


def _const_spec(shape):
    nd = len(shape)
    return pl.BlockSpec(shape, lambda *_: (0,) * nd, pipeline_mode=pl.Buffered(1))


def _adaln_kernel(cp_ref, cs_ref, w_ref, b_ref, win_ref, op_ref, os_ref, wq_ref, wr_ref, wab_ref, *, qkv_dim):
    mp = cp_ref.shape[0]
    c = jnp.concatenate([cp_ref[...], cs_ref[...]], axis=0)
    mod = _mm(_silu(c), w_ref[...]) + b_ref[...]
    op_ref[...] = mod[:mp]
    os_ref[...] = mod[mp:]

    gates = 2 * DN_HEADS
    wq_ref[...] = win_ref[0:qkv_dim, :].T.astype(BF16)
    wr_ref[...] = win_ref[qkv_dim + gates:, :].T.astype(BF16)
    ab_rows = jnp.concatenate([win_ref[qkv_dim:qkv_dim + gates, :],
                               jnp.zeros((AB_PAD - gates, win_ref.shape[1]), F32)], axis=0)
    wab_ref[...] = ab_rows.T.astype(BF16)


def _adaln(c_p, c_s, w, b, w_in_t, qkv_dim):
    (mp, d), ms = c_p.shape, c_s.shape[0]
    n = w.shape[1]
    proj = w_in_t.shape[0]
    rest = proj - qkv_dim - 2 * DN_HEADS
    assert qkv_dim % LANES == 0 and rest % LANES == 0 and (2 * DN_HEADS) % SUBLANES == 0
    steps = max(g for g in (1, 2, 4) if n % (g * LANES) == 0 and d % (g * LANES) == 0)
    tn, rows = n // steps, d // steps
    return pl.pallas_call(
        functools.partial(_adaln_kernel, qkv_dim=qkv_dim),
        grid=(steps,),
        in_specs=[_const_spec((mp, d)), _const_spec((ms, d)),
                  pl.BlockSpec((d, tn), lambda j: (0, j)),
                  pl.BlockSpec((1, tn), lambda j: (0, j)),
                  pl.BlockSpec((proj, rows), lambda j: (0, j))],
        out_specs=[pl.BlockSpec((mp, tn), lambda j: (0, j)), pl.BlockSpec((ms, tn), lambda j: (0, j)),
                   pl.BlockSpec((rows, qkv_dim), lambda j: (j, 0)),
                   pl.BlockSpec((rows, rest), lambda j: (j, 0)),
                   pl.BlockSpec((rows, AB_PAD), lambda j: (j, 0))],
        out_shape=[jax.ShapeDtypeStruct((mp, n), F32), jax.ShapeDtypeStruct((ms, n), F32),
                   jax.ShapeDtypeStruct((d, qkv_dim), BF16), jax.ShapeDtypeStruct((d, rest), BF16),
                   jax.ShapeDtypeStruct((d, AB_PAD), BF16)],
        compiler_params=pltpu.CompilerParams(dimension_semantics=("arbitrary",),
                                             vmem_limit_bytes=V7X_VMEM_LIMIT),
        name="adaln",
    )(c_p, c_s, w, b, w_in_t)


def _ffn_rows(x, shift, scale, gate, wg_ref, wu_ref, wd_ref, g, b, alpha):
    u = (x * (1.0 + scale) + shift).astype(BF16)
    hg = jnp.dot(u, wg_ref[...], preferred_element_type=F32)
    hu = jnp.dot(u, wu_ref[...], preferred_element_type=F32)
    h = (_silu(hg) * hu).astype(BF16)
    d = jnp.dot(h, wd_ref[...], preferred_element_type=F32)
    return _post(x, 0.5 * d, gate, g, b, alpha)


def _ffn_prompt_kernel(x_ref, mod_ref, wg_ref, wu_ref, wd_ref, lng_ref, lnb_ref, o_ref, *, sub, alpha, parts):
    shift = mod_ref[0, 3 * sub + 0:3 * sub + 1, :]
    scale = mod_ref[0, 3 * sub + 1:3 * sub + 2, :]
    gate = mod_ref[0, 3 * sub + 2:3 * sub + 3, :]
    rows = x_ref.shape[0] // parts
    for part in range(parts):
        sl = slice(part * rows, (part + 1) * rows)
        o_ref[sl, :] = _ffn_rows(x_ref[sl, :], shift, scale, gate, wg_ref, wu_ref, wd_ref,
                                 lng_ref[sub:sub + 1, :], lnb_ref[sub:sub + 1, :], alpha)


def _ffn_prompt(x, mod, wg, wu, wd, lng, lnb, *, sub, alpha, tm, parts):
    bsz, t, d = x.shape
    f = wg.shape[1]
    assert t % tm == 0 and tm % (parts * SUBLANES) == 0
    per_seq = t // tm
    x2 = x.reshape(bsz * t, d)
    out = pl.pallas_call(
        functools.partial(_ffn_prompt_kernel, sub=sub, alpha=alpha, parts=parts),
        grid=(bsz * per_seq,),
        in_specs=[pl.BlockSpec((tm, d), lambda i: (i, 0)),
                  pl.BlockSpec((1, 3 * N_SUB, d), lambda i: (i // per_seq, 0, 0)),
                  _const_spec((d, f)), _const_spec((d, f)), _const_spec((f, d)),
                  _const_spec(lng.shape), _const_spec(lnb.shape)],
        out_specs=pl.BlockSpec((tm, d), lambda i: (i, 0)),
        out_shape=jax.ShapeDtypeStruct((bsz * t, d), F32),
        compiler_params=pltpu.CompilerParams(dimension_semantics=("arbitrary",),
                                             vmem_limit_bytes=V7X_VMEM_LIMIT),
        name=f"ffn_prompt{sub}",
    )(x2, mod, wg, wu, wd, lng, lnb)
    return out.reshape(bsz, t, d)


def _stack_time(ref, t_len, d):
    return jnp.concatenate([ref[:, t * d:(t + 1) * d] for t in range(t_len)], axis=0)


def _mod_rows(mod_ref, idx, t_len, d):
    m = mod_ref[:, idx * d:(idx + 1) * d]
    return jnp.concatenate([m] * t_len, axis=0)


def _ffn_sample_kernel(x_ref, mod_ref, wg_ref, wu_ref, wd_ref, lng_ref, lnb_ref, o_ref, *,
                       sub, alpha, t_len, d):
    bsz = x_ref.shape[0]
    x = _stack_time(x_ref, t_len, d)
    y = _ffn_rows(x, _mod_rows(mod_ref, 0, t_len, d), _mod_rows(mod_ref, 1, t_len, d),
                  _mod_rows(mod_ref, 2, t_len, d), wg_ref, wu_ref, wd_ref,
                  lng_ref[sub:sub + 1, :], lnb_ref[sub:sub + 1, :], alpha)
    for t in range(t_len):
        o_ref[:, t * d:(t + 1) * d] = y[t * bsz:(t + 1) * bsz]


def _sub_mod_spec(bsz, d, sub):
    return pl.BlockSpec((bsz, 3 * d), lambda *_: (0, sub), pipeline_mode=pl.Buffered(1))


def _cast_weights(jobs):
    chunks = []
    used = {}
    for w_hbm, w_scr, stage, sem in jobs:
        nbuf, rows = stage.shape[0], stage.shape[1]
        for i in range(w_hbm.shape[0] // rows):
            slot = used.get(id(stage), 0) % nbuf
            used[id(stage)] = used.get(id(stage), 0) + 1
            copy = pltpu.make_async_copy(w_hbm.at[pl.ds(i * rows, rows), :], stage.at[slot], sem.at[slot])
            chunks.append((copy, w_scr, stage, slot, i * rows, rows))
    ahead = min(job[2].shape[0] for job in jobs) - 1
    for copy, *_ in chunks[:ahead]:
        copy.start()
    for j, (copy, w_scr, stage, slot, row0, rows) in enumerate(chunks):
        if j + ahead < len(chunks):
            chunks[j + ahead][0].start()
        copy.wait()
        w_scr[row0:row0 + rows, :] = stage[slot].astype(BF16)


def _swiglu(x, shift, scale, wg_ref, wu_ref, wd_ref):
    u = (x * (1.0 + scale) + shift).astype(BF16)
    hg = jnp.dot(u, wg_ref[...], preferred_element_type=F32)
    hu = jnp.dot(u, wu_ref[...], preferred_element_type=F32)
    h = (_silu(hg) * hu).astype(BF16)
    return jnp.dot(h, wd_ref[...], preferred_element_type=F32)


def _ffn_kernel(xp_ref, modp_ref, xq_ref, modq_ref, xs_ref, mods_ref, wg_hbm, wu_hbm, wd_hbm, lng_ref, lnb_ref,
                yp_ref, ys_ref, wg_scr, wu_scr, wd_scr, d_scr, stage_in, stage_out, sem_in, sem_out,
                *, sub, alpha, n_prompt, t_len, d):
    k = pl.program_id(0)
    g, b = lng_ref[sub:sub + 1, :], lnb_ref[sub:sub + 1, :]

    def matmuls():
        return _swiglu(xp_ref[...], modp_ref[0, 3 * sub + 0:3 * sub + 1, :], modp_ref[0, 3 * sub + 1:3 * sub + 2, :],
                       wg_scr, wu_scr, wd_scr)

    def finish(d_prev):
        yp_ref[...] = _post(xq_ref[...], 0.5 * d_prev, modq_ref[0, 3 * sub + 2:3 * sub + 3, :], g, b, alpha)

    @pl.when(k == 0)
    def _():
        _cast_weights([(wg_hbm, wg_scr, stage_in, sem_in), (wu_hbm, wu_scr, stage_in, sem_in),
                       (wd_hbm, wd_scr, stage_out, sem_out)])
        d_scr[...] = matmuls()

    @pl.when(jnp.logical_and(k > 0, k < n_prompt))
    def _():
        d_prev = d_scr[...]
        d_new = matmuls()
        finish(d_prev)
        d_scr[...] = d_new

    @pl.when(k == n_prompt)
    def _():
        finish(d_scr[...])
        bsz = xs_ref.shape[0]
        x = _stack_time(xs_ref, t_len, d)
        y = _ffn_rows(x, _mod_rows(mods_ref, 0, t_len, d), _mod_rows(mods_ref, 1, t_len, d),
                      _mod_rows(mods_ref, 2, t_len, d), wg_scr, wu_scr, wd_scr, g, b, alpha)
        for t in range(t_len):
            ys_ref[:, t * d:(t + 1) * d] = y[t * bsz:(t + 1) * bsz]


def _ffn(xp, xs, mod_p, mod_s, wg, wu, wd, lng, lnb, *, sub, alpha, tm):
    bp, tp, d = xp.shape
    bs, ts, _ = xs.shape
    f = wg.shape[1]
    assert tp % tm == 0
    per_seq = tp // tm
    n_prompt = bp * per_seq
    in_rows = max(r for r in (16, 32, 64, 128) if d % r == 0)
    out_rows = max(r for r in (16, 32, 64, 128, 176, 352) if f % r == 0)
    last = n_prompt - 1
    yp, ys = pl.pallas_call(
        functools.partial(_ffn_kernel, sub=sub, alpha=alpha, n_prompt=n_prompt, t_len=ts, d=d),
        grid=(n_prompt + 1,),
        in_specs=[pl.BlockSpec((tm, d), lambda k: (jnp.minimum(k, last), 0)),
                  pl.BlockSpec((1, 3 * N_SUB, d), lambda k: (jnp.minimum(k, last) // per_seq, 0, 0)),
                  pl.BlockSpec((tm, d), lambda k: (jnp.maximum(k - 1, 0), 0)),
                  pl.BlockSpec((1, 3 * N_SUB, d), lambda k: (jnp.maximum(k - 1, 0) // per_seq, 0, 0)),
                  _const_spec((bs, ts * d)), _sub_mod_spec(bs, d, sub),
                  pl.BlockSpec(memory_space=pl.ANY), pl.BlockSpec(memory_space=pl.ANY),
                  pl.BlockSpec(memory_space=pl.ANY),
                  _const_spec(lng.shape), _const_spec(lnb.shape)],
        out_specs=[pl.BlockSpec((tm, d), lambda k: (jnp.maximum(k - 1, 0), 0)),
                   pl.BlockSpec((bs, ts * d), lambda k: (0, 0))],
        out_shape=[jax.ShapeDtypeStruct((bp * tp, d), F32),
                   jax.ShapeDtypeStruct((bs, ts * d), F32)],
        scratch_shapes=[pltpu.VMEM((d, f), BF16), pltpu.VMEM((d, f), BF16), pltpu.VMEM((f, d), BF16),
                        pltpu.VMEM((tm, d), F32),
                        pltpu.VMEM((CAST_BUFFERS, in_rows, f), F32), pltpu.VMEM((CAST_BUFFERS, out_rows, d), F32),
                        pltpu.SemaphoreType.DMA((CAST_BUFFERS,)), pltpu.SemaphoreType.DMA((CAST_BUFFERS,))],
        compiler_params=pltpu.CompilerParams(dimension_semantics=("arbitrary",),
                                             vmem_limit_bytes=V7X_VMEM_LIMIT),
        name=f"ffn{sub}",
    )(xp.reshape(bp * tp, d), mod_p, xp.reshape(bp * tp, d), mod_p, xs.reshape(bs, ts * d), mod_s,
      wg, wu, wd, lng, lnb)
    return yp.reshape(bp, tp, d), ys.reshape(bs, ts, d)


def _ffn_sample(x, mod, wg, wu, wd, lng, lnb, *, sub, alpha):
    bsz, t, d = x.shape
    f = wg.shape[1]
    x2 = x.reshape(bsz, t * d)
    out = pl.pallas_call(
        functools.partial(_ffn_sample_kernel, sub=sub, alpha=alpha, t_len=t, d=d),
        grid=(1,),
        in_specs=[_const_spec((bsz, t * d)), _sub_mod_spec(bsz, d, sub),
                  _const_spec((d, f)), _const_spec((d, f)), _const_spec((f, d)),
                  _const_spec(lng.shape), _const_spec(lnb.shape)],
        out_specs=pl.BlockSpec((bsz, t * d), lambda i: (0, 0)),
        out_shape=jax.ShapeDtypeStruct((bsz, t * d), F32),
        compiler_params=pltpu.CompilerParams(dimension_semantics=("arbitrary",),
                                             vmem_limit_bytes=V7X_VMEM_LIMIT),
        name=f"ffn_sample{sub}",
    )(x2, mod, wg, wu, wd, lng, lnb)
    return out.reshape(bsz, t, d)


def _tri_inverse(lmats, c):
    ls = [l.astype(BF16) for l in lmats]
    ms = [jnp.dot(l, l, preferred_element_type=F32) for l in ls]
    qs = ms
    power = 2
    while 2 * power < c:
        mb = [m.astype(BF16) for m in ms]
        ms = [jnp.dot(m, m, preferred_element_type=F32) for m in mb]
        qm = [jnp.dot(q.astype(BF16), m.astype(BF16), preferred_element_type=F32) for q, m in zip(qs, ms)]
        qs = [q + m + x for q, m, x in zip(qs, ms, qm)]
        power *= 2
    rows = lax.broadcasted_iota(jnp.int32, (c, c), 0)
    cols = lax.broadcasted_iota(jnp.int32, (c, c), 1)
    eye = (rows == cols).astype(F32)
    lq = [jnp.dot(l, q.astype(BF16), preferred_element_type=F32) for l, q in zip(ls, qs)]
    return [eye - l + q - x for l, q, x in zip(lmats, qs, lq)]


def _gdn_chunks_local(units, tril, strict):
    c = units[0][1].shape[0]
    a = [lax.dot_general(jnp.concatenate([kb, q], axis=0), k, (((1,), (1,)), ((), ())),
                         preferred_element_type=F32)
         for kb, q, k, _, _, _, _ in units]
    lmats, qks = [], []
    for a_u, (_, _, _, _, _, gcol, grow) in zip(a, units):
        decay = jnp.where(tril, jnp.exp(jnp.where(tril, gcol - grow, 0.0)), 0.0)
        lmats.append(jnp.where(strict, a_u[:c] * decay, 0.0))
        qks.append(jnp.where(tril, a_u[c:] * decay, 0.0))
    tms = _tri_inverse(lmats, c)
    uw = [jnp.dot(tm.astype(BF16), jnp.concatenate([vb, kbe], axis=1), preferred_element_type=F32)
          for tm, (_, _, _, vb, kbe, _, _) in zip(tms, units)]
    return [(x[:, :DN_DV], x[:, DN_DV:], qk) for x, qk in zip(uw, qks)]


def _gdn_chunk_state(units, states):
    c = units[0][2].shape[0]
    r = [jnp.dot(jnp.concatenate([w, q_dec], axis=0), s.astype(BF16), preferred_element_type=F32)
         for (w, q_dec, _, _, _, _), s in zip(units, states)]
    v_new = [(u - r_h[:c]).astype(BF16) for (_, _, u, _, _, _), r_h in zip(units, r)]
    o = [r_h[c:] + jnp.dot(qk, v, preferred_element_type=F32)
         for (_, _, _, qk, _, _), r_h, v in zip(units, r, v_new)]
    s_new = [s * g_last + lax.dot_general(k_dec, v, (((0,), (0,)), ((), ())), preferred_element_type=F32)
             for (_, _, _, _, k_dec, g_last), s, v in zip(units, states, v_new)]
    return o, s_new


def _tri_inverse_steps(lmats, c):
    ls = [l.astype(BF16) for l in lmats]
    ms = [jnp.dot(l, l, preferred_element_type=F32) for l in ls]
    yield
    qs = ms
    power = 2
    while 2 * power < c:
        mb = [m.astype(BF16) for m in ms]
        ms = [jnp.dot(m, m, preferred_element_type=F32) for m in mb]
        yield
        qm = [jnp.dot(q.astype(BF16), m.astype(BF16), preferred_element_type=F32) for q, m in zip(qs, ms)]
        yield
        qs = [q + m + x for q, m, x in zip(qs, ms, qm)]
        power *= 2
    rows = lax.broadcasted_iota(jnp.int32, (c, c), 0)
    cols = lax.broadcasted_iota(jnp.int32, (c, c), 1)
    eye = (rows == cols).astype(F32)
    lq = [jnp.dot(l, q.astype(BF16), preferred_element_type=F32) for l, q in zip(ls, qs)]
    yield
    return [eye - l + q - x for l, q, x in zip(lmats, qs, lq)]


def _gdn_local_steps(units, tril, strict):
    c = units[0][1].shape[0]
    a = [lax.dot_general(jnp.concatenate([kb, q], axis=0), k, (((1,), (1,)), ((), ())),
                         preferred_element_type=F32)
         for kb, q, k, _, _, _, _ in units]
    yield
    lmats, qks = [], []
    for a_u, (_, _, _, _, _, gcol, grow) in zip(a, units):
        decay = jnp.where(tril, jnp.exp(jnp.where(tril, gcol - grow, 0.0)), 0.0)
        lmats.append(jnp.where(strict, a_u[:c] * decay, 0.0))
        qks.append(jnp.where(tril, a_u[c:] * decay, 0.0))
    tms = yield from _tri_inverse_steps(lmats, c)
    uw = [jnp.dot(tm.astype(BF16), jnp.concatenate([vb, kbe], axis=1), preferred_element_type=F32)
          for tm, (_, _, _, vb, kbe, _, _) in zip(tms, units)]
    yield
    return [(x[:, :DN_DV], x[:, DN_DV:], qk) for x, qk in zip(uw, qks)]


def _alternate(primary, secondary):
    result = None
    done = False
    while not done:
        try:
            next(primary)
            yield
        except StopIteration as stop:
            result, done = stop.value, True
        if secondary is not None:
            try:
                next(secondary)
                yield
            except StopIteration:
                secondary = None
    if secondary is not None:
        yield from secondary
    return result


def _run_interleaved(*gens):
    live = list(gens)
    while live:
        for g in list(live):
            try:
                next(g)
            except StopIteration:
                live.remove(g)


def _gate_norm(o, og, norm_g):
    o = o * lax.rsqrt(jnp.mean(o * o, axis=-1, keepdims=True) + RMS_EPS) * norm_g
    return o * _silu(og)


def _proj_layout(dn_width, sc_width):
    off = {}
    pos = 0
    for name, width in (("og", dn_width), ("sc_b", sc_width), ("sc_c", sc_width), ("sc_h", sc_width)):
        off[name] = (pos, pos + width)
        pos += width
    return off


def _project(u, wq_ref, wr_ref, wab_ref, off):
    p = {"qkv": jnp.dot(u, wq_ref[...], preferred_element_type=F32),
         "ab": jnp.dot(u, wab_ref[...], preferred_element_type=F32)}
    rest = jnp.dot(u, wr_ref[...], preferred_element_type=F32)
    for name, (lo, hi) in off.items():
        p[name] = rest[:, lo:hi]
    return p


def _cols(p, off, name):
    del off
    return p[name]


def _mix_out(x, o_dn, o_sc, gate, w_out_ref, g, b, alpha):
    dn_w = o_dn.shape[1]
    mix = (jnp.dot(o_dn.astype(BF16), w_out_ref[0:dn_w, :], preferred_element_type=F32)
           + jnp.dot(o_sc.astype(BF16), w_out_ref[dn_w:, :], preferred_element_type=F32))
    return _post(x, mix, gate, g, b, alpha)


def _mixer_prompt_kernel(x_ref, mod_ref, wq_ref, wr_ref, wab_ref, w_out_ref, cw_ref, mw_ref, alog_ref, dtb_ref, ng_ref,
                         lng_ref, lnb_ref,
                         y_ref, s_out_ref, cq_out_ref, cm_out_ref,
                         s_scr, cq_scr, cm_scr, gct_scr, gl_scr, ops_scr, u_scr, w_scr, qk_scr, o_scr,
                         *, alpha, nseq, tt, off, n_conv, n_mconv):
    t_idx = pl.program_id(1)
    rows_all = nseq * tt
    nchunk = rows_all // CHUNK
    per_seq = tt // CHUNK
    hk = DN_HEADS * DN_DK
    sub = 1

    @pl.when(t_idx == 0)
    def _():
        s_scr[...] = jnp.zeros_like(s_scr)
        cq_scr[...] = jnp.zeros_like(cq_scr)
        cm_scr[...] = jnp.zeros_like(cm_scr)

    xs = [x_ref[s] for s in range(nseq)]
    gates = [mod_ref[s, 3 * sub + 2:3 * sub + 3, :] for s in range(nseq)]
    u = jnp.concatenate(
        [(xs[s] * (1.0 + mod_ref[s, 3 * sub + 1:3 * sub + 2, :]) + mod_ref[s, 3 * sub + 0:3 * sub + 1, :])
         .astype(BF16) for s in range(nseq)], axis=0)
    p = _project(u, wq_ref, wr_ref, wab_ref, off)

    def causal_conv(new_rows, hist_scr, s, w_ref, taps):
        ext = jnp.concatenate([hist_scr[s], new_rows], axis=0)
        acc = new_rows * w_ref[taps - 1:taps, :]
        for j in range(taps - 1):
            acc = acc + pltpu.roll(ext, taps - 1 - j, axis=0)[SUBLANES:] * w_ref[j:j + 1, :]
        hist_scr[s] = new_rows[tt - SUBLANES:]
        return acc

    pq = _cols(p, off, "qkv")
    acts = []
    for s in range(nseq):
        pq_s = pq[s * tt:(s + 1) * tt]
        acts.append(_silu(causal_conv(pq_s, cq_scr, s, cw_ref, n_conv)))
        cq_out_ref[s] = pq_s[tt - (n_conv - 1):]
    act = jnp.concatenate(acts, axis=0)

    ab = _cols(p, off, "ab")
    beta_all = jax.nn.sigmoid(ab)
    g = -jnp.exp(alog_ref[...]) * _softplus(ab + dtb_ref[...])
    rows = lax.broadcasted_iota(jnp.int32, (CHUNK, CHUNK), 0)
    cols = lax.broadcasted_iota(jnp.int32, (CHUNK, CHUNK), 1)
    tril = rows >= cols
    strict = rows > cols
    tril_f = tril.astype(F32)
    gc_parts, gl_parts = [], []
    for c in range(nchunk):
        gc_c = _mm_f32(tril_f, g[c * CHUNK:(c + 1) * CHUNK, :])
        gc_parts.append(gc_c)
        gl_parts.append(jnp.broadcast_to(gc_c[CHUNK - 1:CHUNK, :], (CHUNK, AB_PAD)))
    gc = jnp.concatenate(gc_parts, axis=0)
    gl_rows = jnp.concatenate(gl_parts, axis=0)
    gc_t = gc.T
    for c in range(nchunk):
        gct_scr[c] = gc_t[0:SUBLANES, c * CHUNK:(c + 1) * CHUNK]

    eg_all = jnp.exp(gc)
    ekd_all = jnp.exp(gl_rows - gc)
    egl_all = jnp.exp(gl_rows)
    for h in range(DN_HEADS):
        lanes = slice(h * DN_DK, (h + 1) * DN_DK)
        q = _l2norm(act[:, h * DN_DK:(h + 1) * DN_DK]) * (DN_DK ** -0.5)
        k = _l2norm(act[:, hk + h * DN_DK:hk + (h + 1) * DN_DK])
        v = act[:, 2 * hk + h * DN_DV:2 * hk + (h + 1) * DN_DV]
        beta = beta_all[:, DN_HEADS + h:DN_HEADS + h + 1]
        eg = eg_all[:, h:h + 1]
        kb = k * beta
        ops_scr[0, :, lanes] = kb.astype(BF16)
        ops_scr[1, :, lanes] = q.astype(BF16)
        ops_scr[2, :, lanes] = k.astype(BF16)
        ops_scr[3, :, lanes] = (v * beta).astype(BF16)
        ops_scr[4, :, lanes] = (kb * eg).astype(BF16)
        ops_scr[5, :, lanes] = (q * eg).astype(BF16)
        ops_scr[6, :, lanes] = (k * ekd_all[:, h:h + 1]).astype(BF16)
        gl_scr[:, lanes] = jnp.broadcast_to(egl_all[:, h:h + 1], (rows_all, DN_DV))

    head_lanes = [slice(h * DN_DK, (h + 1) * DN_DK) for h in range(DN_HEADS)]

    where = [(c, h, lanes) for c in range(nchunk) for h, lanes in enumerate(head_lanes)]
    units = [tuple(ops_scr[i, c * CHUNK:(c + 1) * CHUNK, lanes] for i in range(5))
             + (gc[c * CHUNK:(c + 1) * CHUNK, h:h + 1], gct_scr[c, h:h + 1, :])
             for c, h, lanes in where]
    for (c, h, lanes), (u_c, w_c, qk_c) in zip(where, _gdn_chunks_local(units, tril, strict)):
        u_scr[c * CHUNK:(c + 1) * CHUNK, lanes] = u_c
        w_scr[c * CHUNK:(c + 1) * CHUNK, lanes] = w_c.astype(BF16)
        qk_scr[c * DN_HEADS + h] = qk_c.astype(BF16)

    for j in range(per_seq):
        chains = [(s, s * per_seq + j, h, lanes) for s in range(nseq) for h, lanes in enumerate(head_lanes)]
        units = [(w_scr[c * CHUNK:(c + 1) * CHUNK, lanes], ops_scr[5, c * CHUNK:(c + 1) * CHUNK, lanes],
                  u_scr[c * CHUNK:(c + 1) * CHUNK, lanes], qk_scr[c * DN_HEADS + h],
                  ops_scr[6, c * CHUNK:(c + 1) * CHUNK, lanes], gl_scr[c * CHUNK:c * CHUNK + 1, lanes])
                 for s, c, h, lanes in chains]
        o_c, s_new = _gdn_chunk_state(units, [s_scr[s * DN_HEADS + h] for s, _, h, _ in chains])
        for (s, c, h, lanes), o_u, s_u in zip(chains, o_c, s_new):
            s_scr[s * DN_HEADS + h] = s_u
            o_scr[c * CHUNK:(c + 1) * CHUNK, lanes] = o_u
    for s in range(nseq):
        s_out_ref[s] = s_scr[s * DN_HEADS:(s + 1) * DN_HEADS]

    og = _cols(p, off, "og")
    o_dn = jnp.concatenate(
        [_gate_norm(o_scr[:, h * DN_DV:(h + 1) * DN_DV], og[:, h * DN_DV:(h + 1) * DN_DV], ng_ref[...])
         for h in range(DN_HEADS)], axis=1)

    z = _cols(p, off, "sc_c") * _cols(p, off, "sc_h")
    zcs = []
    for s in range(nseq):
        z_s = z[s * tt:(s + 1) * tt]
        zcs.append(causal_conv(z_s, cm_scr, s, mw_ref, n_mconv))
        cm_out_ref[s] = z_s[tt - (n_mconv - 1):]
    o_sc = _cols(p, off, "sc_b") * jnp.concatenate(zcs, axis=0)

    dn_w = o_dn.shape[1]
    mix = (jnp.dot(o_dn.astype(BF16), w_out_ref[0:dn_w, :], preferred_element_type=F32)
           + jnp.dot(o_sc.astype(BF16), w_out_ref[dn_w:, :], preferred_element_type=F32))
    for s in range(nseq):
        y_ref[s] = _post(xs[s], mix[s * tt:(s + 1) * tt], gates[s],
                         lng_ref[sub:sub + 1, :], lnb_ref[sub:sub + 1, :], alpha)


def _mixer_prompt(x, mod, w_in, w_out, conv_w, mconv_w, alog, dtb, norm_g, lng, lnb, *, alpha, nseq, tt, off):
    bsz, t, d = x.shape
    assert t % tt == 0 and tt % LANES == 0 and bsz % nseq == 0
    nt = t // tt
    rows = nseq * tt
    qkv_dim = conv_w.shape[-1]
    sc_w = off["sc_c"][1] - off["sc_c"][0]
    dn_w = DN_HEADS * DN_DV
    n_conv, n_mconv = conv_w.shape[0], mconv_w.shape[0]
    assert max(n_conv, n_mconv) - 1 <= SUBLANES <= tt
    nchunk = rows // CHUNK
    kern = functools.partial(_mixer_prompt_kernel, alpha=alpha, nseq=nseq, tt=tt, off=off,
                             n_conv=n_conv, n_mconv=n_mconv)
    y, s_out, cq, cm = pl.pallas_call(
        kern,
        grid=(bsz // nseq, nt),
        in_specs=[pl.BlockSpec((nseq, tt, d), lambda b, i: (b, i, 0)),
                  pl.BlockSpec((nseq, 3 * N_SUB, d), lambda b, i: (b, 0, 0)),
                  *[_const_spec(w.shape) for w in w_in], _const_spec(w_out.shape),
                  _const_spec(conv_w.shape), _const_spec(mconv_w.shape),
                  _const_spec(alog.shape), _const_spec(dtb.shape), _const_spec(norm_g.shape),
                  _const_spec(lng.shape), _const_spec(lnb.shape)],
        out_specs=[pl.BlockSpec((nseq, tt, d), lambda b, i: (b, i, 0)),
                   pl.BlockSpec((nseq, DN_HEADS, DN_DK, DN_DV), lambda b, i: (b, 0, 0, 0)),
                   pl.BlockSpec((nseq, n_conv - 1, qkv_dim), lambda b, i: (b, 0, 0)),
                   pl.BlockSpec((nseq, n_mconv - 1, sc_w), lambda b, i: (b, 0, 0))],
        out_shape=[jax.ShapeDtypeStruct((bsz, t, d), F32),
                   jax.ShapeDtypeStruct((bsz, DN_HEADS, DN_DK, DN_DV), F32),
                   jax.ShapeDtypeStruct((bsz, n_conv - 1, qkv_dim), F32),
                   jax.ShapeDtypeStruct((bsz, n_mconv - 1, sc_w), F32)],
        scratch_shapes=[pltpu.VMEM((nseq * DN_HEADS, DN_DK, DN_DV), F32),
                        pltpu.VMEM((nseq, SUBLANES, qkv_dim), F32),
                        pltpu.VMEM((nseq, SUBLANES, sc_w), F32),
                        pltpu.VMEM((nchunk, SUBLANES, CHUNK), F32),
                        pltpu.VMEM((rows, dn_w), F32),
                        pltpu.VMEM((7, rows, dn_w), BF16),
                        pltpu.VMEM((rows, dn_w), F32),
                        pltpu.VMEM((rows, dn_w), BF16),
                        pltpu.VMEM((nchunk * DN_HEADS, CHUNK, CHUNK), BF16),
                        pltpu.VMEM((rows, dn_w), F32)],
        compiler_params=pltpu.CompilerParams(dimension_semantics=("arbitrary", "arbitrary"),
                                             vmem_limit_bytes=V7X_VMEM_LIMIT),
        name="mixer_prompt",
    )(x, mod, *w_in, w_out, conv_w, mconv_w, alog, dtb, norm_g, lng, lnb)
    return y, s_out, cq, cm


def _mixer_prompt_pipe_kernel(xf_ref, modf_ref, xb_ref, modb_ref, wq_ref, wr_ref, wab_ref, w_out_ref, cw_ref,
                              mw_ref, alog_ref, dtb_ref, ng_ref, lng_ref, lnb_ref,
                              y_ref, s_out_ref, cq_out_ref, cm_out_ref,
                              s_scr, cq_scr, cm_scr, gct_scr,
                              u_scr, w_scr, qk_scr, qd_scr, kd_scr, gl_scr, og_scr, osc_scr,
                              ops_scr, qd_new, kd_new, gl_new, og_new, osc_new, o_scr,
                              *, alpha, nseq, tt, tiles_per_group, n_tiles, off, n_conv, n_mconv):
    k = pl.program_id(0)
    rows_all = nseq * tt
    nchunk = rows_all // CHUNK
    per_seq = tt // CHUNK
    hk = DN_HEADS * DN_DK
    sub = 1
    head_lanes = [slice(h * DN_DK, (h + 1) * DN_DK) for h in range(DN_HEADS)]
    t_front = lax.rem(jnp.minimum(k, n_tiles - 1), tiles_per_group)
    t_back = lax.rem(jnp.maximum(k - 1, 0), tiles_per_group)
    keep_front = jnp.where(t_front == 0, 0.0, 1.0).astype(F32)
    keep_back = jnp.where(t_back == 0, 0.0, 1.0).astype(F32)

    def causal_conv(new_rows, hist_scr, s, w_ref, taps):
        ext = jnp.concatenate([hist_scr[s] * keep_front, new_rows], axis=0)
        acc = new_rows * w_ref[taps - 1:taps, :]
        for j in range(taps - 1):
            acc = acc + pltpu.roll(ext, taps - 1 - j, axis=0)[SUBLANES:] * w_ref[j:j + 1, :]
        hist_scr[s] = new_rows[tt - SUBLANES:]
        return acc

    def front():
        u = jnp.concatenate(
            [(xf_ref[s] * (1.0 + modf_ref[s, 3 * sub + 1:3 * sub + 2, :])
              + modf_ref[s, 3 * sub + 0:3 * sub + 1, :]).astype(BF16) for s in range(nseq)], axis=0)
        ab = jnp.dot(u, wab_ref[...], preferred_element_type=F32)
        pq = jnp.dot(u, wq_ref[...], preferred_element_type=F32)
        yield

        beta_all = jax.nn.sigmoid(ab)
        g = -jnp.exp(alog_ref[...]) * _softplus(ab + dtb_ref[...])
        rows = lax.broadcasted_iota(jnp.int32, (CHUNK, CHUNK), 0)
        cols = lax.broadcasted_iota(jnp.int32, (CHUNK, CHUNK), 1)
        tril = rows >= cols
        strict = rows > cols
        tril_f = tril.astype(F32)
        gc_parts, gl_parts = [], []
        for c in range(nchunk):
            gc_c = _mm_f32(tril_f, g[c * CHUNK:(c + 1) * CHUNK, :])
            gc_parts.append(gc_c)
            gl_parts.append(jnp.broadcast_to(gc_c[CHUNK - 1:CHUNK, :], (CHUNK, AB_PAD)))
        yield
        p = {}

        def project_rest(names):
            for name in names:
                lo, hi = off[name]
                p[name] = jnp.dot(u, wr_ref[:, lo:hi], preferred_element_type=F32)
                yield

        yield from project_rest(("sc_c", "sc_h"))
        gc = jnp.concatenate(gc_parts, axis=0)
        gl_rows = jnp.concatenate(gl_parts, axis=0)
        gc_t = gc.T
        for c in range(nchunk):
            gct_scr[c] = gc_t[0:SUBLANES, c * CHUNK:(c + 1) * CHUNK]

        acts = []
        for s in range(nseq):
            pq_s = pq[s * tt:(s + 1) * tt]
            acts.append(_silu(causal_conv(pq_s, cq_scr, s, cw_ref, n_conv)))
            cq_out_ref[s] = pq_s[tt - (n_conv - 1):]
        act = jnp.concatenate(acts, axis=0)

        eg_all = jnp.exp(gc)
        ekd_all = jnp.exp(gl_rows - gc)
        egl_all = jnp.exp(gl_rows)
        for h, lanes in enumerate(head_lanes):
            q = _l2norm(act[:, h * DN_DK:(h + 1) * DN_DK]) * (DN_DK ** -0.5)
            kk = _l2norm(act[:, hk + h * DN_DK:hk + (h + 1) * DN_DK])
            v = act[:, 2 * hk + h * DN_DV:2 * hk + (h + 1) * DN_DV]
            beta = beta_all[:, DN_HEADS + h:DN_HEADS + h + 1]
            eg = eg_all[:, h:h + 1]
            kb = kk * beta
            for i, operand in enumerate((kb, q, kk, v * beta, kb * eg)):
                ops_scr[i, :, lanes] = operand.astype(BF16)
            qd_new[:, lanes] = (q * eg).astype(BF16)
            kd_new[:, lanes] = (kk * ekd_all[:, h:h + 1]).astype(BF16)
            for c in range(nchunk):
                gl_new[c * SUBLANES:(c + 1) * SUBLANES, lanes] = jnp.broadcast_to(
                    egl_all[c * CHUNK:c * CHUNK + SUBLANES, h:h + 1], (SUBLANES, DN_DV))

        z = p["sc_c"] * p["sc_h"]
        zcs = []
        for s in range(nseq):
            z_s = z[s * tt:(s + 1) * tt]
            zcs.append(causal_conv(z_s, cm_scr, s, mw_ref, n_mconv))
            cm_out_ref[s] = z_s[tt - (n_mconv - 1):]

        where = [(c, h, lanes) for c in range(nchunk) for h, lanes in enumerate(head_lanes)]
        units = [tuple(ops_scr[i, c * CHUNK:(c + 1) * CHUNK, lanes] for i in range(5))
                 + (gc[c * CHUNK:(c + 1) * CHUNK, h:h + 1], gct_scr[c, h:h + 1, :])
                 for c, h, lanes in where]
        local = yield from _alternate(_gdn_local_steps(units, tril, strict), project_rest(("sc_b", "og")))
        osc_new[...] = p["sc_b"] * jnp.concatenate(zcs, axis=0)
        og_new[...] = p["og"]
        handed_over.extend(zip(where, local))

    handed_over = []

    def commit():
        for (c, h, lanes), (u_c, w_c, qk_c) in handed_over:
            u_scr[c * CHUNK:(c + 1) * CHUNK, lanes] = u_c
            w_scr[c * CHUNK:(c + 1) * CHUNK, lanes] = w_c.astype(BF16)
            qk_scr[c * DN_HEADS + h] = qk_c.astype(BF16)
        for cur, new in ((qd_scr, qd_new), (kd_scr, kd_new), (gl_scr, gl_new), (og_scr, og_new),
                         (osc_scr, osc_new)):
            cur[...] = new[...]
        handed_over.clear()

    def back():
        xs = [xb_ref[s] for s in range(nseq)]
        gates = [modb_ref[s, 3 * sub + 2:3 * sub + 3, :] for s in range(nseq)]
        for j in range(per_seq):
            chains = [(s, s * per_seq + j, h, lanes) for s in range(nseq) for h, lanes in enumerate(head_lanes)]
            rws = [slice(c * CHUNK, (c + 1) * CHUNK) for _, c, _, _ in chains]
            states = [s_scr[s * DN_HEADS + h] for s, _, h, _ in chains]
            if j == 0:
                states = [st * keep_back for st in states]
            r = [jnp.dot(jnp.concatenate([w_scr[rw, lanes], qd_scr[rw, lanes]], axis=0),
                         st.astype(BF16), preferred_element_type=F32)
                 for (s, c, h, lanes), rw, st in zip(chains, rws, states)]
            yield
            v_new = [(u_scr[rw, lanes] - r_u[:CHUNK]).astype(BF16)
                     for (s, c, h, lanes), rw, r_u in zip(chains, rws, r)]
            for (s, c, h, lanes), rw, r_u, v_u in zip(chains, rws, r, v_new):
                o_scr[rw, lanes] = r_u[CHUNK:] + jnp.dot(qk_scr[c * DN_HEADS + h], v_u,
                                                         preferred_element_type=F32)
            for (s, c, h, lanes), rw, st, v_u in zip(chains, rws, states, v_new):
                s_scr[s * DN_HEADS + h] = (
                    st * gl_scr[c * SUBLANES:c * SUBLANES + 1, lanes]
                    + lax.dot_general(kd_scr[rw, lanes], v_u, (((0,), (0,)), ((), ())),
                                      preferred_element_type=F32))
            yield
        for s in range(nseq):
            s_out_ref[s] = s_scr[s * DN_HEADS:(s + 1) * DN_HEADS]

        o_dn = jnp.concatenate(
            [_gate_norm(o_scr[:, lanes], og_scr[:, lanes], ng_ref[...]) for lanes in head_lanes], axis=1)
        dn_w = o_dn.shape[1]
        mix = (jnp.dot(o_dn.astype(BF16), w_out_ref[0:dn_w, :], preferred_element_type=F32)
               + jnp.dot(osc_scr[...].astype(BF16), w_out_ref[dn_w:, :], preferred_element_type=F32))
        yield
        for s in range(nseq):
            y_ref[s] = _post(xs[s], mix[s * tt:(s + 1) * tt], gates[s],
                             lng_ref[sub:sub + 1, :], lnb_ref[sub:sub + 1, :], alpha)

    @pl.when(k == 0)
    def _():
        s_scr[...] = jnp.zeros_like(s_scr)
        cq_scr[...] = jnp.zeros_like(cq_scr)
        cm_scr[...] = jnp.zeros_like(cm_scr)
        _run_interleaved(front())
        commit()

    @pl.when(jnp.logical_and(k > 0, k < n_tiles))
    def _():
        _run_interleaved(back(), front())
        commit()

    @pl.when(k == n_tiles)
    def _():
        _run_interleaved(back())


def _mixer_prompt_pipe(x, mod, w_in, w_out, conv_w, mconv_w, alog, dtb, norm_g, lng, lnb, *, alpha, nseq, tt, off):
    bsz, t, d = x.shape
    assert t % tt == 0 and tt % LANES == 0 and bsz % nseq == 0
    nt = t // tt
    n_tiles = (bsz // nseq) * nt
    rows = nseq * tt
    qkv_dim = conv_w.shape[-1]
    sc_w = off["sc_c"][1] - off["sc_c"][0]
    dn_w = DN_HEADS * DN_DV
    n_conv, n_mconv = conv_w.shape[0], mconv_w.shape[0]
    assert max(n_conv, n_mconv) - 1 <= SUBLANES <= tt
    nchunk = rows // CHUNK
    kern = functools.partial(_mixer_prompt_pipe_kernel, alpha=alpha, nseq=nseq, tt=tt, tiles_per_group=nt,
                             n_tiles=n_tiles, off=off, n_conv=n_conv, n_mconv=n_mconv)

    def front_tile(k):
        return jnp.minimum(k, n_tiles - 1)

    def back_tile(k):
        return jnp.maximum(k - 1, 0)

    y, s_out, cq, cm = pl.pallas_call(
        kern,
        grid=(n_tiles + 1,),
        in_specs=[pl.BlockSpec((nseq, tt, d), lambda k: (front_tile(k) // nt, front_tile(k) % nt, 0)),
                  pl.BlockSpec((nseq, 3 * N_SUB, d), lambda k: (front_tile(k) // nt, 0, 0)),
                  pl.BlockSpec((nseq, tt, d), lambda k: (back_tile(k) // nt, back_tile(k) % nt, 0)),
                  pl.BlockSpec((nseq, 3 * N_SUB, d), lambda k: (back_tile(k) // nt, 0, 0)),
                  *[_const_spec(w.shape) for w in w_in], _const_spec(w_out.shape),
                  _const_spec(conv_w.shape), _const_spec(mconv_w.shape),
                  _const_spec(alog.shape), _const_spec(dtb.shape), _const_spec(norm_g.shape),
                  _const_spec(lng.shape), _const_spec(lnb.shape)],
        out_specs=[pl.BlockSpec((nseq, tt, d), lambda k: (back_tile(k) // nt, back_tile(k) % nt, 0)),
                   pl.BlockSpec((nseq, DN_HEADS, DN_DK, DN_DV), lambda k: (back_tile(k) // nt, 0, 0, 0)),
                   pl.BlockSpec((nseq, n_conv - 1, qkv_dim), lambda k: (front_tile(k) // nt, 0, 0)),
                   pl.BlockSpec((nseq, n_mconv - 1, sc_w), lambda k: (front_tile(k) // nt, 0, 0))],
        out_shape=[jax.ShapeDtypeStruct((bsz, t, d), F32),
                   jax.ShapeDtypeStruct((bsz, DN_HEADS, DN_DK, DN_DV), F32),
                   jax.ShapeDtypeStruct((bsz, n_conv - 1, qkv_dim), F32),
                   jax.ShapeDtypeStruct((bsz, n_mconv - 1, sc_w), F32)],
        scratch_shapes=[pltpu.VMEM((nseq * DN_HEADS, DN_DK, DN_DV), F32),
                        pltpu.VMEM((nseq, SUBLANES, qkv_dim), F32),
                        pltpu.VMEM((nseq, SUBLANES, sc_w), F32),
                        pltpu.VMEM((nchunk, SUBLANES, CHUNK), F32),
                        pltpu.VMEM((rows, dn_w), F32),
                        pltpu.VMEM((rows, dn_w), BF16),
                        pltpu.VMEM((nchunk * DN_HEADS, CHUNK, CHUNK), BF16),
                        pltpu.VMEM((rows, dn_w), BF16),
                        pltpu.VMEM((rows, dn_w), BF16),
                        pltpu.VMEM((nchunk * SUBLANES, dn_w), F32),
                        pltpu.VMEM((rows, dn_w), F32),
                        pltpu.VMEM((rows, sc_w), F32),
                        pltpu.VMEM((5, rows, dn_w), BF16),
                        pltpu.VMEM((rows, dn_w), BF16), pltpu.VMEM((rows, dn_w), BF16),
                        pltpu.VMEM((nchunk * SUBLANES, dn_w), F32),
                        pltpu.VMEM((rows, dn_w), F32), pltpu.VMEM((rows, sc_w), F32),
                        pltpu.VMEM((rows, dn_w), F32)],
        compiler_params=pltpu.CompilerParams(dimension_semantics=("arbitrary",),
                                             vmem_limit_bytes=V7X_VMEM_LIMIT),
        name="mixer_prompt",
    )(x, mod, x, mod, *w_in, w_out, conv_w, mconv_w, alog, dtb, norm_g, lng, lnb)
    return y, s_out, cq, cm


_PACK = SUBLANES


def _sample_phase_a(x_ref, mod_ref, wq_ref, wr_ref, wab_ref, cw_ref, mw_ref, alog_ref, dtb_ref, cq_in_ref, cm_in_ref,
                    cq_out_ref, cm_out_ref, og_scr, osc_scr, lhs_scr, u_scr, kd_scr, gl_scr, qk_scr,
                    *, t_len, d, off, n_conv, n_mconv):
    bsz = x_ref.shape[0]
    hk = DN_HEADS * DN_DK
    qkv_dim = cw_ref.shape[-1]
    sc_w = off["sc_c"][1] - off["sc_c"][0]

    x = _stack_time(x_ref, t_len, d)
    u = (x * (1.0 + _mod_rows(mod_ref, 1, t_len, d)) + _mod_rows(mod_ref, 0, t_len, d)).astype(BF16)
    p = _project(u, wq_ref, wr_ref, wab_ref, off)

    def rows(arr, t):
        return arr[t * bsz:(t + 1) * bsz]

    pq = _cols(p, off, "qkv")
    seq = [cq_in_ref[j] for j in range(n_conv - 1)] + [rows(pq, t) for t in range(t_len)]
    act = []
    for t in range(t_len):
        acc = None
        for j in range(n_conv):
            term = seq[t + j] * cw_ref[j:j + 1, :]
            acc = term if acc is None else acc + term
        act.append(_silu(acc))
    for j in range(n_conv - 1):
        cq_out_ref[j] = seq[t_len + j]

    z = _cols(p, off, "sc_c") * _cols(p, off, "sc_h")
    zseq = ([cm_in_ref[:, j * sc_w:(j + 1) * sc_w] for j in range(n_mconv - 1)]
            + [rows(z, t) for t in range(t_len)])
    zc = []
    for t in range(t_len):
        acc = None
        for j in range(n_mconv):
            term = zseq[t + j] * mw_ref[j:j + 1, :]
            acc = term if acc is None else acc + term
        zc.append(acc)
    for j in range(n_mconv - 1):
        cm_out_ref[:, j * sc_w:(j + 1) * sc_w] = zseq[t_len + j]
    osc_scr[...] = _cols(p, off, "sc_b") * jnp.concatenate(zc, axis=0)
    og_scr[...] = _cols(p, off, "og")

    ab = _cols(p, off, "ab")
    beta_all = jax.nn.sigmoid(ab)
    g_all = -jnp.exp(alog_ref[...]) * _softplus(ab + dtb_ref[...])
    zero = jnp.zeros((bsz, LANES), F32)

    for h in range(DN_HEADS):
        q = [_l2norm(a[:, h * DN_DK:(h + 1) * DN_DK]) * (DN_DK ** -0.5) for a in act]
        k = [_l2norm(a[:, hk + h * DN_DK:hk + (h + 1) * DN_DK]) for a in act]
        v = [a[:, 2 * hk + h * DN_DV:2 * hk + (h + 1) * DN_DV] for a in act]
        beta = [rows(beta_all, t)[:, DN_HEADS + h:DN_HEADS + h + 1] for t in range(t_len)]
        g = [rows(g_all, t)[:, h:h + 1] for t in range(t_len)]
        gc = [g[0]]
        for t in range(1, t_len):
            gc.append(gc[-1] + g[t])
        kb = [k[t] * beta[t] for t in range(t_len)]
        vb = [v[t] * beta[t] for t in range(t_len)]
        eg = [jnp.exp(gc[t]) for t in range(t_len)]
        kbe = [kb[t] * eg[t] for t in range(t_len)]
        lm = [[None] * t_len for _ in range(t_len)]
        n_qk = 0
        for i in range(t_len):
            for j in range(i + 1):
                dec = jnp.exp(gc[i] - gc[j])
                qk_scr[h, n_qk] = jnp.broadcast_to(_rowsum(q[i] * k[j]) * dec, (bsz, LANES))
                n_qk += 1
                if j < i:
                    lm[i][j] = _rowsum(kb[i] * k[j]) * dec
        tm = [[None] * t_len for _ in range(t_len)]
        for i in range(t_len):
            for j in range(i):
                acc = lm[i][j]
                for m in range(j + 1, i):
                    acc = acc + lm[i][m] * tm[m][j]
                tm[i][j] = -acc
        for i in range(t_len):
            ui, wi = vb[i], kbe[i]
            for j in range(i):
                ui = ui + tm[i][j] * vb[j]
                wi = wi + tm[i][j] * kbe[j]
            lhs_scr[h, pl.ds(i, bsz, stride=_PACK), :] = wi
            lhs_scr[h, pl.ds(t_len + i, bsz, stride=_PACK), :] = q[i] * eg[i]
            u_scr[h, pl.ds(i, bsz, stride=_PACK), :] = ui
            u_scr[h, pl.ds(t_len + i, bsz, stride=_PACK), :] = zero
            kd_scr[h, pl.ds(i, bsz, stride=_PACK), :] = k[i] * jnp.exp(gc[t_len - 1] - gc[i])
            kd_scr[h, pl.ds(t_len + i, bsz, stride=_PACK), :] = zero
        gl_scr[h] = jnp.broadcast_to(jnp.exp(gc[t_len - 1]), (bsz, LANES))


def _sample_phase_c(x_ref, mod_ref, w_out_ref, ng_ref, lng_ref, lnb_ref, y_ref,
                    og_scr, osc_scr, res_scr, vn_scr, qk_scr, *, alpha, t_len, d):
    bsz = x_ref.shape[0]
    sub = 1
    x = _stack_time(x_ref, t_len, d)
    gate = _mod_rows(mod_ref, 2, t_len, d)
    og = og_scr[...]
    per_t = []
    for i in range(t_len):
        heads = []
        for h in range(DN_HEADS):
            oi = res_scr[h, pl.ds(t_len + i, bsz, stride=_PACK), :]
            base = i * (i + 1) // 2
            for j in range(i + 1):
                oi = oi + qk_scr[h, base + j] * vn_scr[h, pl.ds(j, bsz, stride=_PACK), :]
            heads.append(_gate_norm(oi, og[i * bsz:(i + 1) * bsz, h * DN_DV:(h + 1) * DN_DV], ng_ref[...]))
        per_t.append(jnp.concatenate(heads, axis=1))
    o_dn = jnp.concatenate(per_t, axis=0)
    y = _mix_out(x, o_dn, osc_scr[...], gate, w_out_ref,
                 lng_ref[sub:sub + 1, :], lnb_ref[sub:sub + 1, :], alpha)
    for t in range(t_len):
        y_ref[:, t * d:(t + 1) * d] = y[t * bsz:(t + 1) * bsz]


def _mixer_sample_kernel(x_ref, mod_ref, wq_ref, wr_ref, wab_ref, w_out_ref, cw_ref, mw_ref, alog_ref, dtb_ref, ng_ref,
                         lng_ref, lnb_ref, cq_in_ref, cm_in_ref, s_in_ref,
                         y_ref, cq_out_ref, cm_out_ref, s_out_ref,
                         og_scr, osc_scr, lhs_scr, u_scr, kd_scr, gl_scr, qk_scr, res_scr, vn_scr,
                         *, alpha, t_len, d, off, n_conv, n_mconv, bt):
    step = pl.program_id(0)
    assert 2 * t_len == _PACK, "packed buffers hold [T rows | T rows] per sequence"

    @pl.when(step == 0)
    def _():
        _sample_phase_a(x_ref, mod_ref, wq_ref, wr_ref, wab_ref, cw_ref, mw_ref, alog_ref, dtb_ref, cq_in_ref, cm_in_ref,
                        cq_out_ref, cm_out_ref, og_scr, osc_scr, lhs_scr, u_scr, kd_scr, gl_scr, qk_scr,
                        t_len=t_len, d=d, off=off, n_conv=n_conv, n_mconv=n_mconv)

    top = (lax.broadcasted_iota(jnp.int32, (_PACK, LANES), 0) < t_len).astype(F32)

    chains = [(bl, h) for bl in range(bt) for h in range(DN_HEADS)]
    seq0 = step * bt
    row0 = [pl.multiple_of((seq0 + bl) * _PACK, _PACK) for bl in range(bt)]
    states = [s_in_ref[bl, h] for bl, h in chains]
    r = [_mm(lhs_scr[h, pl.ds(row0[bl], _PACK), :], s) for (bl, h), s in zip(chains, states)]
    vns = []
    for (bl, h), r_u in zip(chains, r):
        res_scr[h, pl.ds(row0[bl], _PACK), :] = r_u
        vn = (u_scr[h, pl.ds(row0[bl], _PACK), :] - r_u) * top
        vn_scr[h, pl.ds(row0[bl], _PACK), :] = vn
        vns.append(vn)
    upd = [_mm_at(kd_scr[h, pl.ds(row0[bl], _PACK), :], vn) for (bl, h), vn in zip(chains, vns)]
    for (bl, h), s, s_add in zip(chains, states, upd):
        s_out_ref[bl, h] = s * gl_scr[h, pl.ds(seq0 + bl, 1), :] + s_add

    @pl.when(step == pl.num_programs(0) - 1)
    def _():
        _sample_phase_c(x_ref, mod_ref, w_out_ref, ng_ref, lng_ref, lnb_ref, y_ref,
                        og_scr, osc_scr, res_scr, vn_scr, qk_scr, alpha=alpha, t_len=t_len, d=d)


def _mixer_sample(x, mod, w_in, w_out, conv_w, mconv_w, alog, dtb, norm_g, lng, lnb, s_in, cq_in, cm_in,
                  *, alpha, off, bt):
    bsz, t, d = x.shape
    assert bsz % bt == 0
    qkv_dim = conv_w.shape[-1]
    sc_w = off["sc_c"][1] - off["sc_c"][0]
    dn_w = DN_HEADS * DN_DV
    n_conv, n_mconv = conv_w.shape[0], mconv_w.shape[0]
    n_qk = t * (t + 1) // 2
    x2 = x.reshape(bsz, t * d)
    cq2 = jnp.swapaxes(cq_in, 0, 1)
    cm2 = cm_in.reshape(bsz, (n_mconv - 1) * sc_w)
    kern = functools.partial(_mixer_sample_kernel, alpha=alpha, t_len=t, d=d, off=off,
                             n_conv=n_conv, n_mconv=n_mconv, bt=bt)
    consts = (*w_in, w_out, conv_w, mconv_w, alog, dtb, norm_g, lng, lnb, cq2, cm2)
    state_spec = pl.BlockSpec((bt, DN_HEADS, DN_DK, DN_DV), lambda i: (i, 0, 0, 0))
    packed = pltpu.VMEM((DN_HEADS, _PACK * bsz, LANES), F32)
    y, cq, cm, s_out = pl.pallas_call(
        kern,
        grid=(bsz // bt,),
        in_specs=([_const_spec(x2.shape), _sub_mod_spec(bsz, d, 1)]
                  + [_const_spec(a.shape) for a in consts] + [state_spec]),
        out_specs=[pl.BlockSpec(x2.shape, lambda i: (0, 0)),
                   pl.BlockSpec(cq2.shape, lambda i: (0, 0, 0)),
                   pl.BlockSpec(cm2.shape, lambda i: (0, 0)),
                   state_spec],
        out_shape=[jax.ShapeDtypeStruct(x2.shape, F32),
                   jax.ShapeDtypeStruct(cq2.shape, F32),
                   jax.ShapeDtypeStruct(cm2.shape, F32),
                   jax.ShapeDtypeStruct(s_in.shape, F32)],
        scratch_shapes=[pltpu.VMEM((t * bsz, dn_w), F32),
                        pltpu.VMEM((t * bsz, sc_w), F32),
                        packed,
                        packed,
                        packed,
                        pltpu.VMEM((DN_HEADS, bsz, LANES), F32),
                        pltpu.VMEM((DN_HEADS, n_qk, bsz, LANES), F32),
                        packed,
                        packed],
        compiler_params=pltpu.CompilerParams(dimension_semantics=("arbitrary",),
                                             vmem_limit_bytes=V7X_VMEM_LIMIT),
        name="mixer_sample",
    )(x2, mod, *consts, s_in)
    return y.reshape(bsz, t, d), s_out, jnp.swapaxes(cq, 0, 1), cm.reshape(bsz, n_mconv - 1, sc_w)


def _split_w_in(w_in, qkv_dim):
    c1 = qkv_dim + 2 * DN_HEADS
    wab = jnp.pad(w_in[:, qkv_dim:c1], ((0, 0), (0, AB_PAD - 2 * DN_HEADS)))
    return w_in[:, :qkv_dim].astype(BF16), w_in[:, c1:].astype(BF16), wab.astype(BF16)


def _pad_lanes(v):
    return jnp.zeros((1, AB_PAD), F32).at[0, :v.shape[0]].set(v)


def kernel(x_prompt, x_sample, state_ssm, state_conv_qkv, state_conv_mix, c_prompt, c_sample, w_ada, b_ada, ln_g, ln_b, ffn1_wg, ffn1_wu, ffn1_wd, ffn2_wg, ffn2_wu, ffn2_wd, w_in, conv_qkv_w, a_log, dt_bias, dn_norm_g, conv_mix_w, w_out):
    depth = w_ada.shape[0]
    alpha = (2 * depth) ** 0.25
    bp, tp, d = x_prompt.shape
    qkv_dim = conv_qkv_w.shape[-1]
    sc_width = conv_mix_w.shape[-1]
    dn_width = DN_HEADS * DN_DV
    off = _proj_layout(dn_width, sc_width)
    tm = min(512, tp)
    nseq = 2 if bp % 2 == 0 else 1
    tt = min(512 // nseq, tp)
    bt = min(8, x_sample.shape[0])

    hp, hs = x_prompt, x_sample
    outs = [[] for _ in range(6)]
    for l in range(depth):
        mod_p, mod_s, *w_in_p = _adaln(c_prompt, c_sample, w_ada[l], b_ada[l].reshape(1, -1),
                                       jnp.swapaxes(w_in[l], 0, 1), qkv_dim)
        mod_p = mod_p.reshape(bp, 3 * N_SUB, d)
        lng, lnb = ln_g[l], ln_b[l]
        w1 = (ffn1_wg[l], ffn1_wu[l], ffn1_wd[l])
        w2 = (ffn2_wg[l], ffn2_wu[l], ffn2_wd[l])
        w_out_b = w_out[l].astype(BF16)
        alog, dtb = _pad_lanes(a_log[l]), _pad_lanes(dt_bias[l])
        norm_g = dn_norm_g[l].reshape(1, -1)
        mix_args = (w_in_p, w_out_b, conv_qkv_w[l], conv_mix_w[l], alog, dtb, norm_g, lng, lnb)

        hp, hs = _ffn(hp, hs, mod_p, mod_s, *w1, lng, lnb, sub=0, alpha=alpha, tm=tm)
        hp, a1, a2, a3 = _mixer_prompt_pipe(hp, mod_p, *mix_args, alpha=alpha, nseq=nseq, tt=tt, off=off)
        hs, b1, b2, b3 = _mixer_sample(hs, mod_s, *mix_args, state_ssm[l], state_conv_qkv[l],
                                       state_conv_mix[l], alpha=alpha, off=off, bt=bt)
        hp, hs = _ffn(hp, hs, mod_p, mod_s, *w2, lng, lnb, sub=2, alpha=alpha, tm=tm)
        for lst, val in zip(outs, (a1, a2, a3, b1, b2, b3)):
            lst.append(val)
    return (hp, hs) + tuple(jnp.stack(o) for o in outs)
```

```python
import functools

import jax
import jax.numpy as jnp
from jax import lax
from jax.experimental import pallas as pl
from jax.experimental.pallas import tpu as pltpu

F32 = jnp.float32
BF16 = jnp.bfloat16

LN_EPS = 1e-5
RMS_EPS = 1e-6
N_SUB = 3
DN_HEADS = 4
DN_DK = 128
DN_DV = 128
CHUNK = 64
LANES = 128
SUBLANES = 8
AB_PAD = LANES

V7X_VMEM_LIMIT = 60000 * 1024
CAST_BUFFERS = 4


def _mm(a, b):
    return jnp.dot(a.astype(BF16), b.astype(BF16), preferred_element_type=F32)


def _mm_at(a, b):
    return lax.dot_general(a.astype(BF16), b.astype(BF16), (((0,), (0,)), ((), ())),
                           preferred_element_type=F32)


def _mm_f32(a, b):
    return jnp.dot(a, b, preferred_element_type=F32, precision=lax.Precision.HIGHEST)


def _silu(x):
    return x * jax.nn.sigmoid(x)


def _softplus(x):
    return jnp.maximum(x, 0.0) + jnp.log1p(jnp.exp(-jnp.abs(x)))


def _layer_norm(y, g, b):
    mu = jnp.mean(y, axis=-1, keepdims=True)
    yc = y - mu
    var = jnp.mean(yc * yc, axis=-1, keepdims=True)
    return yc * lax.rsqrt(var + LN_EPS) * g + b


def _post(x, delta, gate, g, b, alpha):
    return _layer_norm(alpha * x + gate * delta, g, b)


def _rowsum(x):
    return jnp.sum(x, axis=-1, keepdims=True)


def _l2norm(x):
    return x * lax.rsqrt(_rowsum(x * x) + RMS_EPS)


def _const_spec(shape):
    nd = len(shape)
    return pl.BlockSpec(shape, lambda *_: (0,) * nd, pipeline_mode=pl.Buffered(1))


def _adaln_kernel(cp_ref, cs_ref, w_ref, b_ref, win_ref, op_ref, os_ref, wq_ref, wr_ref, wab_ref, *, qkv_dim):
    mp = cp_ref.shape[0]
    c = jnp.concatenate([cp_ref[...], cs_ref[...]], axis=0)
    mod = _mm(_silu(c), w_ref[...]) + b_ref[...]
    op_ref[...] = mod[:mp]
    os_ref[...] = mod[mp:]

    gates = 2 * DN_HEADS
    wq_ref[...] = win_ref[0:qkv_dim, :].T.astype(BF16)
    wr_ref[...] = win_ref[qkv_dim + gates:, :].T.astype(BF16)
    ab_rows = jnp.concatenate([win_ref[qkv_dim:qkv_dim + gates, :],
                               jnp.zeros((AB_PAD - gates, win_ref.shape[1]), F32)], axis=0)
    wab_ref[...] = ab_rows.T.astype(BF16)


def _adaln(c_p, c_s, w, b, w_in_t, qkv_dim):
    (mp, d), ms = c_p.shape, c_s.shape[0]
    n = w.shape[1]
    proj = w_in_t.shape[0]
    rest = proj - qkv_dim - 2 * DN_HEADS
    assert qkv_dim % LANES == 0 and rest % LANES == 0 and (2 * DN_HEADS) % SUBLANES == 0
    steps = max(g for g in (1, 2, 4) if n % (g * LANES) == 0 and d % (g * LANES) == 0)
    tn, rows = n // steps, d // steps
    return pl.pallas_call(
        functools.partial(_adaln_kernel, qkv_dim=qkv_dim),
        grid=(steps,),
        in_specs=[_const_spec((mp, d)), _const_spec((ms, d)),
                  pl.BlockSpec((d, tn), lambda j: (0, j)),
                  pl.BlockSpec((1, tn), lambda j: (0, j)),
                  pl.BlockSpec((proj, rows), lambda j: (0, j))],
        out_specs=[pl.BlockSpec((mp, tn), lambda j: (0, j)), pl.BlockSpec((ms, tn), lambda j: (0, j)),
                   pl.BlockSpec((rows, qkv_dim), lambda j: (j, 0)),
                   pl.BlockSpec((rows, rest), lambda j: (j, 0)),
                   pl.BlockSpec((rows, AB_PAD), lambda j: (j, 0))],
        out_shape=[jax.ShapeDtypeStruct((mp, n), F32), jax.ShapeDtypeStruct((ms, n), F32),
                   jax.ShapeDtypeStruct((d, qkv_dim), BF16), jax.ShapeDtypeStruct((d, rest), BF16),
                   jax.ShapeDtypeStruct((d, AB_PAD), BF16)],
        compiler_params=pltpu.CompilerParams(dimension_semantics=("arbitrary",),
                                             vmem_limit_bytes=V7X_VMEM_LIMIT),
        name="adaln",
    )(c_p, c_s, w, b, w_in_t)


def _swiglu(x, shift, scale, wg_ref, wu_ref, wd_ref):
    u = (x * (1.0 + scale) + shift).astype(BF16)
    hg = jnp.dot(u, wg_ref[...], preferred_element_type=F32)
    hu = jnp.dot(u, wu_ref[...], preferred_element_type=F32)
    h = (_silu(hg) * hu).astype(BF16)
    return jnp.dot(h, wd_ref[...], preferred_element_type=F32)


def _stack_time(ref, t_len, d):
    return jnp.concatenate([ref[:, t * d:(t + 1) * d] for t in range(t_len)], axis=0)


def _mod_rows(mod_ref, idx, t_len, d):
    m = mod_ref[:, idx * d:(idx + 1) * d]
    return jnp.concatenate([m] * t_len, axis=0)


def _sub_mod_spec(bsz, d, sub):
    return pl.BlockSpec((bsz, 3 * d), lambda *_: (0, sub), pipeline_mode=pl.Buffered(1))


def _cast_weights(jobs):
    chunks = []
    used = {}
    for w_hbm, w_scr, stage, sem in jobs:
        nbuf, rows = stage.shape[0], stage.shape[1]
        for i in range(w_hbm.shape[0] // rows):
            slot = used.get(id(stage), 0) % nbuf
            used[id(stage)] = used.get(id(stage), 0) + 1
            copy = pltpu.make_async_copy(w_hbm.at[pl.ds(i * rows, rows), :], stage.at[slot], sem.at[slot])
            chunks.append((copy, w_scr, stage, slot, i * rows, rows))
    ahead = min(job[2].shape[0] for job in jobs) - 1
    for copy, *_ in chunks[:ahead]:
        copy.start()
    for j, (copy, w_scr, stage, slot, row0, rows) in enumerate(chunks):
        if j + ahead < len(chunks):
            chunks[j + ahead][0].start()
        copy.wait()
        w_scr[row0:row0 + rows, :] = stage[slot].astype(BF16)


def _ffn_kernel(xp_ref, modp_ref, xq_ref, modq_ref, xs_ref, mods_ref, wg_hbm, wu_hbm, wd_hbm, lng_ref, lnb_ref,
                yp_ref, ys_ref, wg_scr, wu_scr, wd_scr, d_scr, stage_in, stage_out, sem_in, sem_out,
                *, sub, alpha, n_prompt, t_len, d):
    k = pl.program_id(0)
    g, b = lng_ref[sub:sub + 1, :], lnb_ref[sub:sub + 1, :]

    def matmuls():
        return _swiglu(xp_ref[...], modp_ref[0, 3 * sub + 0:3 * sub + 1, :], modp_ref[0, 3 * sub + 1:3 * sub + 2, :],
                       wg_scr, wu_scr, wd_scr)

    def finish(d_prev):
        yp_ref[...] = _post(xq_ref[...], 0.5 * d_prev, modq_ref[0, 3 * sub + 2:3 * sub + 3, :], g, b, alpha)

    @pl.when(k == 0)
    def _():
        _cast_weights([(wg_hbm, wg_scr, stage_in, sem_in), (wu_hbm, wu_scr, stage_in, sem_in),
                       (wd_hbm, wd_scr, stage_out, sem_out)])
        d_scr[...] = matmuls()

    @pl.when(jnp.logical_and(k > 0, k < n_prompt))
    def _():
        d_prev = d_scr[...]
        d_new = matmuls()
        finish(d_prev)
        d_scr[...] = d_new

    @pl.when(k == n_prompt)
    def _():
        finish(d_scr[...])
        bsz = xs_ref.shape[0]
        x = _stack_time(xs_ref, t_len, d)
        delta = _swiglu(x, _mod_rows(mods_ref, 0, t_len, d), _mod_rows(mods_ref, 1, t_len, d),
                        wg_scr, wu_scr, wd_scr)
        y = _post(x, 0.5 * delta, _mod_rows(mods_ref, 2, t_len, d), g, b, alpha)
        for t in range(t_len):
            ys_ref[:, t * d:(t + 1) * d] = y[t * bsz:(t + 1) * bsz]


def _ffn(xp, xs, mod_p, mod_s, wg, wu, wd, lng, lnb, *, sub, alpha, tm):
    bp, tp, d = xp.shape
    bs, ts, _ = xs.shape
    f = wg.shape[1]
    assert tp % tm == 0
    per_seq = tp // tm
    n_prompt = bp * per_seq
    in_rows = max(r for r in (16, 32, 64, 128) if d % r == 0)
    out_rows = max(r for r in (16, 32, 64, 128, 176, 352) if f % r == 0)
    last = n_prompt - 1
    yp, ys = pl.pallas_call(
        functools.partial(_ffn_kernel, sub=sub, alpha=alpha, n_prompt=n_prompt, t_len=ts, d=d),
        grid=(n_prompt + 1,),
        in_specs=[pl.BlockSpec((tm, d), lambda k: (jnp.minimum(k, last), 0)),
                  pl.BlockSpec((1, 3 * N_SUB, d), lambda k: (jnp.minimum(k, last) // per_seq, 0, 0)),
                  pl.BlockSpec((tm, d), lambda k: (jnp.maximum(k - 1, 0), 0)),
                  pl.BlockSpec((1, 3 * N_SUB, d), lambda k: (jnp.maximum(k - 1, 0) // per_seq, 0, 0)),
                  _const_spec((bs, ts * d)), _sub_mod_spec(bs, d, sub),
                  pl.BlockSpec(memory_space=pl.ANY), pl.BlockSpec(memory_space=pl.ANY),
                  pl.BlockSpec(memory_space=pl.ANY),
                  _const_spec(lng.shape), _const_spec(lnb.shape)],
        out_specs=[pl.BlockSpec((tm, d), lambda k: (jnp.maximum(k - 1, 0), 0)),
                   pl.BlockSpec((bs, ts * d), lambda k: (0, 0))],
        out_shape=[jax.ShapeDtypeStruct((bp * tp, d), F32),
                   jax.ShapeDtypeStruct((bs, ts * d), F32)],
        scratch_shapes=[pltpu.VMEM((d, f), BF16), pltpu.VMEM((d, f), BF16), pltpu.VMEM((f, d), BF16),
                        pltpu.VMEM((tm, d), F32),
                        pltpu.VMEM((CAST_BUFFERS, in_rows, f), F32), pltpu.VMEM((CAST_BUFFERS, out_rows, d), F32),
                        pltpu.SemaphoreType.DMA((CAST_BUFFERS,)), pltpu.SemaphoreType.DMA((CAST_BUFFERS,))],
        compiler_params=pltpu.CompilerParams(dimension_semantics=("arbitrary",),
                                             vmem_limit_bytes=V7X_VMEM_LIMIT),
        name=f"ffn{sub}",
    )(xp.reshape(bp * tp, d), mod_p, xp.reshape(bp * tp, d), mod_p, xs.reshape(bs, ts * d), mod_s,
      wg, wu, wd, lng, lnb)
    return yp.reshape(bp, tp, d), ys.reshape(bs, ts, d)


def _tri_inverse_steps(lmats, c):
    ls = [l.astype(BF16) for l in lmats]
    ms = [jnp.dot(l, l, preferred_element_type=F32) for l in ls]
    yield
    qs = ms
    power = 2
    while 2 * power < c:
        mb = [m.astype(BF16) for m in ms]
        ms = [jnp.dot(m, m, preferred_element_type=F32) for m in mb]
        yield
        qm = [jnp.dot(q.astype(BF16), m.astype(BF16), preferred_element_type=F32) for q, m in zip(qs, ms)]
        yield
        qs = [q + m + x for q, m, x in zip(qs, ms, qm)]
        power *= 2
    rows = lax.broadcasted_iota(jnp.int32, (c, c), 0)
    cols = lax.broadcasted_iota(jnp.int32, (c, c), 1)
    eye = (rows == cols).astype(F32)
    lq = [jnp.dot(l, q.astype(BF16), preferred_element_type=F32) for l, q in zip(ls, qs)]
    yield
    return [eye - l + q - x for l, q, x in zip(lmats, qs, lq)]


def _gdn_local_steps(units, tril, strict):
    c = units[0][1].shape[0]
    a = [lax.dot_general(jnp.concatenate([kb, q], axis=0), k, (((1,), (1,)), ((), ())),
                         preferred_element_type=F32)
         for kb, q, k, _, _, _, _ in units]
    yield
    lmats, qks = [], []
    for a_u, (_, _, _, _, _, gcol, grow) in zip(a, units):
        decay = jnp.where(tril, jnp.exp(jnp.where(tril, gcol - grow, 0.0)), 0.0)
        lmats.append(jnp.where(strict, a_u[:c] * decay, 0.0))
        qks.append(jnp.where(tril, a_u[c:] * decay, 0.0))
    tms = yield from _tri_inverse_steps(lmats, c)
    uw = [jnp.dot(tm.astype(BF16), jnp.concatenate([vb, kbe], axis=1), preferred_element_type=F32)
          for tm, (_, _, _, vb, kbe, _, _) in zip(tms, units)]
    yield
    return [(x[:, :DN_DV], x[:, DN_DV:], qk) for x, qk in zip(uw, qks)]


def _alternate(primary, secondary):
    result = None
    done = False
    while not done:
        try:
            next(primary)
            yield
        except StopIteration as stop:
            result, done = stop.value, True
        if secondary is not None:
            try:
                next(secondary)
                yield
            except StopIteration:
                secondary = None
    if secondary is not None:
        yield from secondary
    return result


def _run_interleaved(*gens):
    live = list(gens)
    while live:
        for g in list(live):
            try:
                next(g)
            except StopIteration:
                live.remove(g)


def _gate_norm(o, og, norm_g):
    o = o * lax.rsqrt(jnp.mean(o * o, axis=-1, keepdims=True) + RMS_EPS) * norm_g
    return o * _silu(og)


def _proj_layout(dn_width, sc_width):
    off = {}
    pos = 0
    for name, width in (("og", dn_width), ("sc_b", sc_width), ("sc_c", sc_width), ("sc_h", sc_width)):
        off[name] = (pos, pos + width)
        pos += width
    return off


def _mixer_prompt_kernel(xf_ref, modf_ref, xb_ref, modb_ref, wq_ref, wr_ref, wab_ref, w_out_ref, cw_ref,
                         mw_ref, alog_ref, dtb_ref, ng_ref, lng_ref, lnb_ref,
                         y_ref, s_out_ref, cq_out_ref, cm_out_ref,
                         s_scr, cq_scr, cm_scr, gct_scr,
                         u_scr, w_scr, qk_scr, qd_scr, kd_scr, gl_scr, og_scr, osc_scr,
                         *, alpha, nseq, tt, tiles_per_group, n_tiles, off, n_conv, n_mconv):
    k = pl.program_id(0)
    rows_all = nseq * tt
    nchunk = rows_all // CHUNK
    per_seq = tt // CHUNK
    hk = DN_HEADS * DN_DK
    sub = 1
    head_lanes = [slice(h * DN_DK, (h + 1) * DN_DK) for h in range(DN_HEADS)]
    t_front = lax.rem(jnp.minimum(k, n_tiles - 1), tiles_per_group)
    t_back = lax.rem(jnp.maximum(k - 1, 0), tiles_per_group)
    keep_front = jnp.where(t_front == 0, 0.0, 1.0).astype(F32)
    keep_back = jnp.where(t_back == 0, 0.0, 1.0).astype(F32)

    def causal_conv(new_rows, hist_scr, s, w_ref, taps):
        ext = jnp.concatenate([hist_scr[s] * keep_front, new_rows], axis=0)
        acc = new_rows * w_ref[taps - 1:taps, :]
        for j in range(taps - 1):
            acc = acc + pltpu.roll(ext, taps - 1 - j, axis=0)[SUBLANES:] * w_ref[j:j + 1, :]
        hist_scr[s] = new_rows[tt - SUBLANES:]
        return acc

    def front():
        u = jnp.concatenate(
            [(xf_ref[s] * (1.0 + modf_ref[s, 3 * sub + 1:3 * sub + 2, :])
              + modf_ref[s, 3 * sub + 0:3 * sub + 1, :]).astype(BF16) for s in range(nseq)], axis=0)
        ab = jnp.dot(u, wab_ref[...], preferred_element_type=F32)
        pq = jnp.dot(u, wq_ref[...], preferred_element_type=F32)
        yield

        beta_all = jax.nn.sigmoid(ab)
        g = -jnp.exp(alog_ref[...]) * _softplus(ab + dtb_ref[...])
        rows = lax.broadcasted_iota(jnp.int32, (CHUNK, CHUNK), 0)
        cols = lax.broadcasted_iota(jnp.int32, (CHUNK, CHUNK), 1)
        tril = rows >= cols
        strict = rows > cols
        tril_f = tril.astype(F32)
        gc_parts, gl_parts = [], []
        for c in range(nchunk):
            gc_c = _mm_f32(tril_f, g[c * CHUNK:(c + 1) * CHUNK, :])
            gc_parts.append(gc_c)
            gl_parts.append(jnp.broadcast_to(gc_c[CHUNK - 1:CHUNK, :], (CHUNK, AB_PAD)))
        yield
        p = {}

        def project_rest(names):
            for name in names:
                lo, hi = off[name]
                p[name] = jnp.dot(u, wr_ref[:, lo:hi], preferred_element_type=F32)
                yield

        yield from project_rest(("sc_c", "sc_h"))
        gc = jnp.concatenate(gc_parts, axis=0)
        gl_rows = jnp.concatenate(gl_parts, axis=0)
        gc_t = gc.T
        for c in range(nchunk):
            gct_scr[c] = gc_t[0:SUBLANES, c * CHUNK:(c + 1) * CHUNK]

        acts = []
        for s in range(nseq):
            pq_s = pq[s * tt:(s + 1) * tt]
            acts.append(_silu(causal_conv(pq_s, cq_scr, s, cw_ref, n_conv)))
            cq_out_ref[s] = pq_s[tt - (n_conv - 1):]
        act = jnp.concatenate(acts, axis=0)

        eg_all = jnp.exp(gc)
        ekd_all = jnp.exp(gl_rows - gc)
        egl_all = jnp.exp(gl_rows)
        per_head = []
        for h, lanes in enumerate(head_lanes):
            q = _l2norm(act[:, h * DN_DK:(h + 1) * DN_DK]) * (DN_DK ** -0.5)
            kk = _l2norm(act[:, hk + h * DN_DK:hk + (h + 1) * DN_DK])
            v = act[:, 2 * hk + h * DN_DV:2 * hk + (h + 1) * DN_DV]
            beta = beta_all[:, DN_HEADS + h:DN_HEADS + h + 1]
            eg = eg_all[:, h:h + 1]
            kb = kk * beta
            per_head.append((kb.astype(BF16), q.astype(BF16), kk.astype(BF16), (v * beta).astype(BF16),
                             (kb * eg).astype(BF16)))
            qd_scr[:, lanes] = (q * eg).astype(BF16)
            kd_scr[:, lanes] = (kk * ekd_all[:, h:h + 1]).astype(BF16)
            for c in range(nchunk):
                gl_scr[c * SUBLANES:(c + 1) * SUBLANES, lanes] = jnp.broadcast_to(
                    egl_all[c * CHUNK:c * CHUNK + SUBLANES, h:h + 1], (SUBLANES, DN_DV))

        z = p["sc_c"] * p["sc_h"]
        zcs = []
        for s in range(nseq):
            z_s = z[s * tt:(s + 1) * tt]
            zcs.append(causal_conv(z_s, cm_scr, s, mw_ref, n_mconv))
            cm_out_ref[s] = z_s[tt - (n_mconv - 1):]

        where = [(c, h, lanes) for c in range(nchunk) for h, lanes in enumerate(head_lanes)]
        units = [tuple(arr[c * CHUNK:(c + 1) * CHUNK] for arr in per_head[h])
                 + (gc[c * CHUNK:(c + 1) * CHUNK, h:h + 1], gct_scr[c, h:h + 1, :])
                 for c, h, lanes in where]
        local = yield from _alternate(_gdn_local_steps(units, tril, strict), project_rest(("sc_b", "og")))
        osc_scr[...] = p["sc_b"] * jnp.concatenate(zcs, axis=0)
        og_scr[...] = p["og"]
        for (c, h, lanes), (u_c, w_c, qk_c) in zip(where, local):
            u_scr[c * CHUNK:(c + 1) * CHUNK, lanes] = u_c
            w_scr[c * CHUNK:(c + 1) * CHUNK, lanes] = w_c.astype(BF16)
            qk_scr[c * DN_HEADS + h] = qk_c.astype(BF16)

    def back():
        u_all, w_all, qd_all, kd_all = u_scr[...], w_scr[...], qd_scr[...], kd_scr[...]
        gl_all, og, o_sc = gl_scr[...], og_scr[...], osc_scr[...]
        qks = [qk_scr[i] for i in range(nchunk * DN_HEADS)]
        xs = [xb_ref[s] for s in range(nseq)]
        gates = [modb_ref[s, 3 * sub + 2:3 * sub + 3, :] for s in range(nseq)]
        states = [s_scr[i] * keep_back for i in range(nseq * DN_HEADS)]
        o_rows = [[None] * DN_HEADS for _ in range(nchunk)]
        for j in range(per_seq):
            chains = [(s, s * per_seq + j, h, lanes) for s in range(nseq) for h, lanes in enumerate(head_lanes)]
            rws = [slice(c * CHUNK, (c + 1) * CHUNK) for _, c, _, _ in chains]
            r = [jnp.dot(jnp.concatenate([w_all[rw, lanes], qd_all[rw, lanes]], axis=0),
                         states[s * DN_HEADS + h].astype(BF16), preferred_element_type=F32)
                 for (s, c, h, lanes), rw in zip(chains, rws)]
            yield
            v_new = [(u_all[rw, lanes] - r_u[:CHUNK]).astype(BF16)
                     for (s, c, h, lanes), rw, r_u in zip(chains, rws, r)]
            for (s, c, h, lanes), r_u, v_u in zip(chains, r, v_new):
                o_rows[c][h] = r_u[CHUNK:] + jnp.dot(qks[c * DN_HEADS + h], v_u, preferred_element_type=F32)
            for (s, c, h, lanes), rw, v_u in zip(chains, rws, v_new):
                i = s * DN_HEADS + h
                states[i] = (states[i] * gl_all[c * SUBLANES:c * SUBLANES + 1, lanes]
                             + lax.dot_general(kd_all[rw, lanes], v_u, (((0,), (0,)), ((), ())),
                                               preferred_element_type=F32))
            yield
        for i, st in enumerate(states):
            s_scr[i] = st
        for s in range(nseq):
            for h in range(DN_HEADS):
                s_out_ref[s, h] = states[s * DN_HEADS + h]

        o_dn = jnp.concatenate(
            [_gate_norm(jnp.concatenate([o_rows[c][h] for c in range(nchunk)], axis=0),
                        og[:, lanes], ng_ref[...]) for h, lanes in enumerate(head_lanes)], axis=1)
        dn_w = o_dn.shape[1]
        mix = (jnp.dot(o_dn.astype(BF16), w_out_ref[0:dn_w, :], preferred_element_type=F32)
               + jnp.dot(o_sc.astype(BF16), w_out_ref[dn_w:, :], preferred_element_type=F32))
        yield
        for s in range(nseq):
            y_ref[s] = _post(xs[s], mix[s * tt:(s + 1) * tt], gates[s],
                             lng_ref[sub:sub + 1, :], lnb_ref[sub:sub + 1, :], alpha)

    @pl.when(k == 0)
    def _():
        s_scr[...] = jnp.zeros_like(s_scr)
        cq_scr[...] = jnp.zeros_like(cq_scr)
        cm_scr[...] = jnp.zeros_like(cm_scr)
        _run_interleaved(front())

    @pl.when(jnp.logical_and(k > 0, k < n_tiles))
    def _():
        _run_interleaved(back(), front())

    @pl.when(k == n_tiles)
    def _():
        _run_interleaved(back())


def _mixer_prompt(x, mod, w_in, w_out, conv_w, mconv_w, alog, dtb, norm_g, lng, lnb, *, alpha, nseq, tt, off):
    bsz, t, d = x.shape
    assert t % tt == 0 and tt % LANES == 0 and bsz % nseq == 0
    nt = t // tt
    n_tiles = (bsz // nseq) * nt
    rows = nseq * tt
    qkv_dim = conv_w.shape[-1]
    sc_w = off["sc_c"][1] - off["sc_c"][0]
    dn_w = DN_HEADS * DN_DV
    n_conv, n_mconv = conv_w.shape[0], mconv_w.shape[0]
    assert max(n_conv, n_mconv) - 1 <= SUBLANES <= tt
    nchunk = rows // CHUNK
    kern = functools.partial(_mixer_prompt_kernel, alpha=alpha, nseq=nseq, tt=tt, tiles_per_group=nt,
                             n_tiles=n_tiles, off=off, n_conv=n_conv, n_mconv=n_mconv)

    def front_tile(k):
        return jnp.minimum(k, n_tiles - 1)

    def back_tile(k):
        return jnp.maximum(k - 1, 0)

    y, s_out, cq, cm = pl.pallas_call(
        kern,
        grid=(n_tiles + 1,),
        in_specs=[pl.BlockSpec((nseq, tt, d), lambda k: (front_tile(k) // nt, front_tile(k) % nt, 0)),
                  pl.BlockSpec((nseq, 3 * N_SUB, d), lambda k: (front_tile(k) // nt, 0, 0)),
                  pl.BlockSpec((nseq, tt, d), lambda k: (back_tile(k) // nt, back_tile(k) % nt, 0)),
                  pl.BlockSpec((nseq, 3 * N_SUB, d), lambda k: (back_tile(k) // nt, 0, 0)),
                  *[_const_spec(w.shape) for w in w_in], _const_spec(w_out.shape),
                  _const_spec(conv_w.shape), _const_spec(mconv_w.shape),
                  _const_spec(alog.shape), _const_spec(dtb.shape), _const_spec(norm_g.shape),
                  _const_spec(lng.shape), _const_spec(lnb.shape)],
        out_specs=[pl.BlockSpec((nseq, tt, d), lambda k: (back_tile(k) // nt, back_tile(k) % nt, 0)),
                   pl.BlockSpec((nseq, DN_HEADS, DN_DK, DN_DV), lambda k: (back_tile(k) // nt, 0, 0, 0)),
                   pl.BlockSpec((nseq, n_conv - 1, qkv_dim), lambda k: (front_tile(k) // nt, 0, 0)),
                   pl.BlockSpec((nseq, n_mconv - 1, sc_w), lambda k: (front_tile(k) // nt, 0, 0))],
        out_shape=[jax.ShapeDtypeStruct((bsz, t, d), F32),
                   jax.ShapeDtypeStruct((bsz, DN_HEADS, DN_DK, DN_DV), F32),
                   jax.ShapeDtypeStruct((bsz, n_conv - 1, qkv_dim), F32),
                   jax.ShapeDtypeStruct((bsz, n_mconv - 1, sc_w), F32)],
        scratch_shapes=[pltpu.VMEM((nseq * DN_HEADS, DN_DK, DN_DV), F32),
                        pltpu.VMEM((nseq, SUBLANES, qkv_dim), F32),
                        pltpu.VMEM((nseq, SUBLANES, sc_w), F32),
                        pltpu.VMEM((nchunk, SUBLANES, CHUNK), F32),
                        pltpu.VMEM((rows, dn_w), F32),
                        pltpu.VMEM((rows, dn_w), BF16),
                        pltpu.VMEM((nchunk * DN_HEADS, CHUNK, CHUNK), BF16),
                        pltpu.VMEM((rows, dn_w), BF16),
                        pltpu.VMEM((rows, dn_w), BF16),
                        pltpu.VMEM((nchunk * SUBLANES, dn_w), F32),
                        pltpu.VMEM((rows, dn_w), F32),
                        pltpu.VMEM((rows, sc_w), F32)],
        compiler_params=pltpu.CompilerParams(dimension_semantics=("arbitrary",),
                                             vmem_limit_bytes=V7X_VMEM_LIMIT),
        name="mixer_prompt",
    )(x, mod, x, mod, *w_in, w_out, conv_w, mconv_w, alog, dtb, norm_g, lng, lnb)
    return y, s_out, cq, cm


_PACK = SUBLANES


def _sample_phase_a(x_ref, mod_ref, wq_ref, wr_ref, wab_ref, cw_ref, mw_ref, alog_ref, dtb_ref, cq_in_ref, cm_in_ref,
                    cq_out_ref, cm_out_ref, og_scr, osc_scr, lhs_scr, u_scr, kd_scr, gl_scr, qk_scr,
                    *, t_len, d, off, n_conv, n_mconv):
    bsz = x_ref.shape[0]
    hk = DN_HEADS * DN_DK
    sc_w = off["sc_c"][1] - off["sc_c"][0]

    x = _stack_time(x_ref, t_len, d)
    u = (x * (1.0 + _mod_rows(mod_ref, 1, t_len, d)) + _mod_rows(mod_ref, 0, t_len, d)).astype(BF16)
    pq = jnp.dot(u, wq_ref[...], preferred_element_type=F32)
    ab = jnp.dot(u, wab_ref[...], preferred_element_type=F32)
    rest = jnp.dot(u, wr_ref[...], preferred_element_type=F32)
    p = {name: rest[:, lo:hi] for name, (lo, hi) in off.items()}

    def rows(arr, t):
        return arr[t * bsz:(t + 1) * bsz]

    seq = [cq_in_ref[j] for j in range(n_conv - 1)] + [rows(pq, t) for t in range(t_len)]
    act = []
    for t in range(t_len):
        acc = None
        for j in range(n_conv):
            term = seq[t + j] * cw_ref[j:j + 1, :]
            acc = term if acc is None else acc + term
        act.append(_silu(acc))
    for j in range(n_conv - 1):
        cq_out_ref[j] = seq[t_len + j]

    z = p["sc_c"] * p["sc_h"]
    zseq = ([cm_in_ref[:, j * sc_w:(j + 1) * sc_w] for j in range(n_mconv - 1)]
            + [rows(z, t) for t in range(t_len)])
    zc = []
    for t in range(t_len):
        acc = None
        for j in range(n_mconv):
            term = zseq[t + j] * mw_ref[j:j + 1, :]
            acc = term if acc is None else acc + term
        zc.append(acc)
    for j in range(n_mconv - 1):
        cm_out_ref[:, j * sc_w:(j + 1) * sc_w] = zseq[t_len + j]
    osc_scr[...] = p["sc_b"] * jnp.concatenate(zc, axis=0)
    og_scr[...] = p["og"]

    beta_all = jax.nn.sigmoid(ab)
    g_all = -jnp.exp(alog_ref[...]) * _softplus(ab + dtb_ref[...])
    zero = jnp.zeros((bsz, LANES), F32)

    for h in range(DN_HEADS):
        q = [_l2norm(a[:, h * DN_DK:(h + 1) * DN_DK]) * (DN_DK ** -0.5) for a in act]
        k = [_l2norm(a[:, hk + h * DN_DK:hk + (h + 1) * DN_DK]) for a in act]
        v = [a[:, 2 * hk + h * DN_DV:2 * hk + (h + 1) * DN_DV] for a in act]
        beta = [rows(beta_all, t)[:, DN_HEADS + h:DN_HEADS + h + 1] for t in range(t_len)]
        g = [rows(g_all, t)[:, h:h + 1] for t in range(t_len)]
        gc = [g[0]]
        for t in range(1, t_len):
            gc.append(gc[-1] + g[t])
        kb = [k[t] * beta[t] for t in range(t_len)]
        vb = [v[t] * beta[t] for t in range(t_len)]
        eg = [jnp.exp(gc[t]) for t in range(t_len)]
        kbe = [kb[t] * eg[t] for t in range(t_len)]
        lm = [[None] * t_len for _ in range(t_len)]
        n_qk = 0
        for i in range(t_len):
            for j in range(i + 1):
                dec = jnp.exp(gc[i] - gc[j])
                qk_scr[h, n_qk] = jnp.broadcast_to(_rowsum(q[i] * k[j]) * dec, (bsz, LANES))
                n_qk += 1
                if j < i:
                    lm[i][j] = _rowsum(kb[i] * k[j]) * dec
        tm = [[None] * t_len for _ in range(t_len)]
        for i in range(t_len):
            for j in range(i):
                acc = lm[i][j]
                for m in range(j + 1, i):
                    acc = acc + lm[i][m] * tm[m][j]
                tm[i][j] = -acc
        for i in range(t_len):
            ui, wi = vb[i], kbe[i]
            for j in range(i):
                ui = ui + tm[i][j] * vb[j]
                wi = wi + tm[i][j] * kbe[j]
            lhs_scr[h, pl.ds(i, bsz, stride=_PACK), :] = wi
            lhs_scr[h, pl.ds(t_len + i, bsz, stride=_PACK), :] = q[i] * eg[i]
            u_scr[h, pl.ds(i, bsz, stride=_PACK), :] = ui
            u_scr[h, pl.ds(t_len + i, bsz, stride=_PACK), :] = zero
            kd_scr[h, pl.ds(i, bsz, stride=_PACK), :] = k[i] * jnp.exp(gc[t_len - 1] - gc[i])
            kd_scr[h, pl.ds(t_len + i, bsz, stride=_PACK), :] = zero
        gl_scr[h] = jnp.broadcast_to(jnp.exp(gc[t_len - 1]), (bsz, LANES))


def _sample_phase_c(x_ref, mod_ref, w_out_ref, ng_ref, lng_ref, lnb_ref, y_ref,
                    og_scr, osc_scr, res_scr, vn_scr, qk_scr, *, alpha, t_len, d):
    bsz = x_ref.shape[0]
    sub = 1
    x = _stack_time(x_ref, t_len, d)
    gate = _mod_rows(mod_ref, 2, t_len, d)
    og = og_scr[...]
    per_t = []
    for i in range(t_len):
        heads = []
        for h in range(DN_HEADS):
            oi = res_scr[h, pl.ds(t_len + i, bsz, stride=_PACK), :]
            base = i * (i + 1) // 2
            for j in range(i + 1):
                oi = oi + qk_scr[h, base + j] * vn_scr[h, pl.ds(j, bsz, stride=_PACK), :]
            heads.append(_gate_norm(oi, og[i * bsz:(i + 1) * bsz, h * DN_DV:(h + 1) * DN_DV], ng_ref[...]))
        per_t.append(jnp.concatenate(heads, axis=1))
    o_dn = jnp.concatenate(per_t, axis=0)
    dn_w = o_dn.shape[1]
    mix = (jnp.dot(o_dn.astype(BF16), w_out_ref[0:dn_w, :], preferred_element_type=F32)
           + jnp.dot(osc_scr[...].astype(BF16), w_out_ref[dn_w:, :], preferred_element_type=F32))
    y = _post(x, mix, gate, lng_ref[sub:sub + 1, :], lnb_ref[sub:sub + 1, :], alpha)
    for t in range(t_len):
        y_ref[:, t * d:(t + 1) * d] = y[t * bsz:(t + 1) * bsz]


def _mixer_sample_kernel(x_ref, mod_ref, wq_ref, wr_ref, wab_ref, w_out_ref, cw_ref, mw_ref, alog_ref, dtb_ref, ng_ref,
                         lng_ref, lnb_ref, cq_in_ref, cm_in_ref, s_in_ref,
                         y_ref, cq_out_ref, cm_out_ref, s_out_ref,
                         og_scr, osc_scr, lhs_scr, u_scr, kd_scr, gl_scr, qk_scr, res_scr, vn_scr,
                         *, alpha, t_len, d, off, n_conv, n_mconv, bt):
    step = pl.program_id(0)
    assert 2 * t_len == _PACK, "packed buffers hold [T rows | T rows] per sequence"

    @pl.when(step == 0)
    def _():
        _sample_phase_a(x_ref, mod_ref, wq_ref, wr_ref, wab_ref, cw_ref, mw_ref, alog_ref, dtb_ref, cq_in_ref, cm_in_ref,
                        cq_out_ref, cm_out_ref, og_scr, osc_scr, lhs_scr, u_scr, kd_scr, gl_scr, qk_scr,
                        t_len=t_len, d=d, off=off, n_conv=n_conv, n_mconv=n_mconv)

    top = (lax.broadcasted_iota(jnp.int32, (_PACK, LANES), 0) < t_len).astype(F32)

    chains = [(bl, h) for bl in range(bt) for h in range(DN_HEADS)]
    seq0 = step * bt
    row0 = [pl.multiple_of((seq0 + bl) * _PACK, _PACK) for bl in range(bt)]
    states = [s_in_ref[bl, h] for bl, h in chains]
    r = [_mm(lhs_scr[h, pl.ds(row0[bl], _PACK), :], s) for (bl, h), s in zip(chains, states)]
    vns = []
    for (bl, h), r_u in zip(chains, r):
        res_scr[h, pl.ds(row0[bl], _PACK), :] = r_u
        vn = (u_scr[h, pl.ds(row0[bl], _PACK), :] - r_u) * top
        vn_scr[h, pl.ds(row0[bl], _PACK), :] = vn
        vns.append(vn)
    upd = [_mm_at(kd_scr[h, pl.ds(row0[bl], _PACK), :], vn) for (bl, h), vn in zip(chains, vns)]
    for (bl, h), s, s_add in zip(chains, states, upd):
        s_out_ref[bl, h] = s * gl_scr[h, pl.ds(seq0 + bl, 1), :] + s_add

    @pl.when(step == pl.num_programs(0) - 1)
    def _():
        _sample_phase_c(x_ref, mod_ref, w_out_ref, ng_ref, lng_ref, lnb_ref, y_ref,
                        og_scr, osc_scr, res_scr, vn_scr, qk_scr, alpha=alpha, t_len=t_len, d=d)


def _mixer_sample(x, mod, w_in, w_out, conv_w, mconv_w, alog, dtb, norm_g, lng, lnb, s_in, cq_in, cm_in,
                  *, alpha, off, bt):
    bsz, t, d = x.shape
    assert bsz % bt == 0
    sc_w = off["sc_c"][1] - off["sc_c"][0]
    dn_w = DN_HEADS * DN_DV
    n_conv, n_mconv = conv_w.shape[0], mconv_w.shape[0]
    n_qk = t * (t + 1) // 2
    x2 = x.reshape(bsz, t * d)
    cq2 = jnp.swapaxes(cq_in, 0, 1)
    cm2 = cm_in.reshape(bsz, (n_mconv - 1) * sc_w)
    kern = functools.partial(_mixer_sample_kernel, alpha=alpha, t_len=t, d=d, off=off,
                             n_conv=n_conv, n_mconv=n_mconv, bt=bt)
    consts = (*w_in, w_out, conv_w, mconv_w, alog, dtb, norm_g, lng, lnb, cq2, cm2)
    state_spec = pl.BlockSpec((bt, DN_HEADS, DN_DK, DN_DV), lambda i: (i, 0, 0, 0))
    packed = pltpu.VMEM((DN_HEADS, _PACK * bsz, LANES), F32)
    y, cq, cm, s_out = pl.pallas_call(
        kern,
        grid=(bsz // bt,),
        in_specs=([_const_spec(x2.shape), _sub_mod_spec(bsz, d, 1)]
                  + [_const_spec(a.shape) for a in consts] + [state_spec]),
        out_specs=[pl.BlockSpec(x2.shape, lambda i: (0, 0)),
                   pl.BlockSpec(cq2.shape, lambda i: (0, 0, 0)),
                   pl.BlockSpec(cm2.shape, lambda i: (0, 0)),
                   state_spec],
        out_shape=[jax.ShapeDtypeStruct(x2.shape, F32),
                   jax.ShapeDtypeStruct(cq2.shape, F32),
                   jax.ShapeDtypeStruct(cm2.shape, F32),
                   jax.ShapeDtypeStruct(s_in.shape, F32)],
        scratch_shapes=[pltpu.VMEM((t * bsz, dn_w), F32),
                        pltpu.VMEM((t * bsz, sc_w), F32),
                        packed,
                        packed,
                        packed,
                        pltpu.VMEM((DN_HEADS, bsz, LANES), F32),
                        pltpu.VMEM((DN_HEADS, n_qk, bsz, LANES), F32),
                        packed,
                        packed],
        compiler_params=pltpu.CompilerParams(dimension_semantics=("arbitrary",),
                                             vmem_limit_bytes=V7X_VMEM_LIMIT),
        name="mixer_sample",
    )(x2, mod, *consts, s_in)
    return y.reshape(bsz, t, d), s_out, jnp.swapaxes(cq, 0, 1), cm.reshape(bsz, n_mconv - 1, sc_w)


def _pad_lanes(v):
    return jnp.zeros((1, AB_PAD), F32).at[0, :v.shape[0]].set(v)


def kernel(x_prompt, x_sample, state_ssm, state_conv_qkv, state_conv_mix, c_prompt, c_sample, w_ada, b_ada, ln_g, ln_b, ffn1_wg, ffn1_wu, ffn1_wd, ffn2_wg, ffn2_wu, ffn2_wd, w_in, conv_qkv_w, a_log, dt_bias, dn_norm_g, conv_mix_w, w_out):
    depth = w_ada.shape[0]
    alpha = (2 * depth) ** 0.25
    bp, tp, d = x_prompt.shape
    qkv_dim = conv_qkv_w.shape[-1]
    sc_width = conv_mix_w.shape[-1]
    dn_width = DN_HEADS * DN_DV
    off = _proj_layout(dn_width, sc_width)
    tm = min(512, tp)
    nseq = 2 if bp % 2 == 0 else 1
    tt = min(512 // nseq, tp)
    bt = min(8, x_sample.shape[0])

    hp, hs = x_prompt, x_sample
    outs = [[] for _ in range(6)]
    for l in range(depth):
        mod_p, mod_s, *w_in_p = _adaln(c_prompt, c_sample, w_ada[l], b_ada[l].reshape(1, -1),
                                       jnp.swapaxes(w_in[l], 0, 1), qkv_dim)
        mod_p = mod_p.reshape(bp, 3 * N_SUB, d)
        lng, lnb = ln_g[l], ln_b[l]
        w1 = (ffn1_wg[l], ffn1_wu[l], ffn1_wd[l])
        w2 = (ffn2_wg[l], ffn2_wu[l], ffn2_wd[l])
        w_out_b = w_out[l].astype(BF16)
        alog, dtb = _pad_lanes(a_log[l]), _pad_lanes(dt_bias[l])
        norm_g = dn_norm_g[l].reshape(1, -1)
        mix_args = (w_in_p, w_out_b, conv_qkv_w[l], conv_mix_w[l], alog, dtb, norm_g, lng, lnb)

        hp, hs = _ffn(hp, hs, mod_p, mod_s, *w1, lng, lnb, sub=0, alpha=alpha, tm=tm)
        hp, a1, a2, a3 = _mixer_prompt(hp, mod_p, *mix_args, alpha=alpha, nseq=nseq, tt=tt, off=off)
        hs, b1, b2, b3 = _mixer_sample(hs, mod_s, *mix_args, state_ssm[l], state_conv_qkv[l],
                                       state_conv_mix[l], alpha=alpha, off=off, bt=bt)
        hp, hs = _ffn(hp, hs, mod_p, mod_s, *w2, lng, lnb, sub=2, alpha=alpha, tm=tm)
        for lst, val in zip(outs, (a1, a2, a3, b1, b2, b3)):
            lst.append(val)
    return (hp, hs) + tuple(jnp.stack(o) for o in outs)
```

```python
import functools

import jax
import jax.numpy as jnp
from jax import lax
from jax.experimental import pallas as pl
from jax.experimental.pallas import tpu as pltpu

F32 = jnp.float32
BF16 = jnp.bfloat16

LN_EPS = 1e-5
RMS_EPS = 1e-6
N_SUB = 3
DN_HEADS = 4
DN_DK = 128
DN_DV = 128
CHUNK = 64
LANES = 128
SUBLANES = 8
AB_PAD = LANES

V7X_VMEM_LIMIT = 60000 * 1024
CAST_BUFFERS = 3
FFN_CHUNK = 256


def _mm(a, b):
    return jnp.dot(a.astype(BF16), b.astype(BF16), preferred_element_type=F32)


def _mm_at(a, b):
    return lax.dot_general(a.astype(BF16), b.astype(BF16), (((0,), (0,)), ((), ())),
                           preferred_element_type=F32)


def _mm_f32(a, b):
    return jnp.dot(a, b, preferred_element_type=F32, precision=lax.Precision.HIGHEST)


def _silu(x):
    return x * jax.nn.sigmoid(x)


def _softplus(x):
    return jnp.maximum(x, 0.0) + jnp.log1p(jnp.exp(-jnp.abs(x)))


def _layer_norm(y, g, b):
    mu = jnp.mean(y, axis=-1, keepdims=True)
    yc = y - mu
    var = jnp.mean(yc * yc, axis=-1, keepdims=True)
    return yc * lax.rsqrt(var + LN_EPS) * g + b


def _post(x, delta, gate, g, b, alpha):
    return _layer_norm(alpha * x + gate * delta, g, b)


def _rowsum(x):
    return jnp.sum(x, axis=-1, keepdims=True)


def _l2norm(x):
    return x * lax.rsqrt(_rowsum(x * x) + RMS_EPS)


def _const_spec(shape):
    nd = len(shape)
    return pl.BlockSpec(shape, lambda *_: (0,) * nd, pipeline_mode=pl.Buffered(1))


def _adaln_kernel(cp_ref, cs_ref, w_ref, b_ref, win_ref, op_ref, os_ref, wq_ref, wr_ref, wab_ref, *, qkv_dim):
    mp = cp_ref.shape[0]
    c = jnp.concatenate([cp_ref[...], cs_ref[...]], axis=0)
    mod = _mm(_silu(c), w_ref[...]) + b_ref[...]
    op_ref[...] = mod[:mp]
    os_ref[...] = mod[mp:]

    gates = 2 * DN_HEADS
    wq_ref[...] = win_ref[0:qkv_dim, :].T.astype(BF16)
    wr_ref[...] = win_ref[qkv_dim + gates:, :].T.astype(BF16)
    ab_rows = jnp.concatenate([win_ref[qkv_dim:qkv_dim + gates, :],
                               jnp.zeros((AB_PAD - gates, win_ref.shape[1]), F32)], axis=0)
    wab_ref[...] = ab_rows.T.astype(BF16)


def _adaln(c_p, c_s, w, b, w_in_t, qkv_dim):
    (mp, d), ms = c_p.shape, c_s.shape[0]
    n = w.shape[1]
    proj = w_in_t.shape[0]
    rest = proj - qkv_dim - 2 * DN_HEADS
    assert qkv_dim % LANES == 0 and rest % LANES == 0 and (2 * DN_HEADS) % SUBLANES == 0
    steps = max(g for g in (1, 2, 4) if n % (g * LANES) == 0 and d % (g * LANES) == 0)
    tn, rows = n // steps, d // steps
    return pl.pallas_call(
        functools.partial(_adaln_kernel, qkv_dim=qkv_dim),
        grid=(steps,),
        in_specs=[_const_spec((mp, d)), _const_spec((ms, d)),
                  pl.BlockSpec((d, tn), lambda j: (0, j)),
                  pl.BlockSpec((1, tn), lambda j: (0, j)),
                  pl.BlockSpec((proj, rows), lambda j: (0, j))],
        out_specs=[pl.BlockSpec((mp, tn), lambda j: (0, j)), pl.BlockSpec((ms, tn), lambda j: (0, j)),
                   pl.BlockSpec((rows, qkv_dim), lambda j: (j, 0)),
                   pl.BlockSpec((rows, rest), lambda j: (j, 0)),
                   pl.BlockSpec((rows, AB_PAD), lambda j: (j, 0))],
        out_shape=[jax.ShapeDtypeStruct((mp, n), F32), jax.ShapeDtypeStruct((ms, n), F32),
                   jax.ShapeDtypeStruct((d, qkv_dim), BF16), jax.ShapeDtypeStruct((d, rest), BF16),
                   jax.ShapeDtypeStruct((d, AB_PAD), BF16)],
        compiler_params=pltpu.CompilerParams(dimension_semantics=("arbitrary",),
                                             vmem_limit_bytes=V7X_VMEM_LIMIT),
        name="adaln",
    )(c_p, c_s, w, b, w_in_t)


def _swiglu(x, shift, scale, wg_ref, wu_ref, wd_ref):
    u = (x * (1.0 + scale) + shift).astype(BF16)
    hg = jnp.dot(u, wg_ref[...], preferred_element_type=F32)
    hu = jnp.dot(u, wu_ref[...], preferred_element_type=F32)
    h = (_silu(hg) * hu).astype(BF16)
    return jnp.dot(h, wd_ref[...], preferred_element_type=F32)


def _stack_time(ref, t_len, d):
    return jnp.concatenate([ref[:, t * d:(t + 1) * d] for t in range(t_len)], axis=0)


def _mod_rows(mod_ref, idx, t_len, d):
    m = mod_ref[:, idx * d:(idx + 1) * d]
    return jnp.concatenate([m] * t_len, axis=0)


def _sub_mod_spec(bsz, d, sub):
    return pl.BlockSpec((bsz, 3 * d), lambda *_: (0, sub), pipeline_mode=pl.Buffered(1))


def _first_tile_swiglu(x, shift, scale, w_hbm, w_scr, stages, sems):
    wg_hbm, wu_hbm, wd_hbm = w_hbm
    wg_scr, wu_scr, wd_scr = w_scr
    nbuf = stages[0].shape[0]
    cw = stages[0].shape[2]
    n = wg_hbm.shape[1] // cw

    def copies(j):
        slot = j % nbuf
        cols = pl.ds(j * cw, cw)
        return (pltpu.make_async_copy(wg_hbm.at[:, cols], stages[0].at[slot], sems[0].at[slot]),
                pltpu.make_async_copy(wu_hbm.at[:, cols], stages[1].at[slot], sems[1].at[slot]),
                pltpu.make_async_copy(wd_hbm.at[cols, :], stages[2].at[slot], sems[2].at[slot]))

    queue = [copies(j) for j in range(n)]
    ahead = nbuf - 1
    for group in queue[:ahead]:
        for copy in group:
            copy.start()
    u = (x * (1.0 + scale) + shift).astype(BF16)
    acc = None
    for j, group in enumerate(queue):
        if j + ahead < n:
            for copy in queue[j + ahead]:
                copy.start()
        for copy in group:
            copy.wait()
        slot, cols = j % nbuf, slice(j * cw, (j + 1) * cw)
        wg_scr[:, cols] = stages[0][slot].astype(BF16)
        wu_scr[:, cols] = stages[1][slot].astype(BF16)
        wd_scr[cols, :] = stages[2][slot].astype(BF16)
        hg = jnp.dot(u, wg_scr[:, cols], preferred_element_type=F32)
        hu = jnp.dot(u, wu_scr[:, cols], preferred_element_type=F32)
        part = jnp.dot((_silu(hg) * hu).astype(BF16), wd_scr[cols, :], preferred_element_type=F32)
        acc = part if acc is None else acc + part
    return acc


def _ffn_kernel(xp_ref, modp_ref, xq_ref, modq_ref, xs_ref, mods_ref, wg_hbm, wu_hbm, wd_hbm, lng_ref, lnb_ref,
                yp_ref, ys_ref, wg_scr, wu_scr, wd_scr, d_scr, stage_g, stage_u, stage_d, sem_g, sem_u, sem_d,
                *, sub, alpha, n_prompt, t_len, d):
    k = pl.program_id(0)
    g, b = lng_ref[sub:sub + 1, :], lnb_ref[sub:sub + 1, :]

    def matmuls():
        return _swiglu(xp_ref[...], modp_ref[0, 3 * sub + 0:3 * sub + 1, :], modp_ref[0, 3 * sub + 1:3 * sub + 2, :],
                       wg_scr, wu_scr, wd_scr)

    def finish(d_prev):
        yp_ref[...] = _post(xq_ref[...], 0.5 * d_prev, modq_ref[0, 3 * sub + 2:3 * sub + 3, :], g, b, alpha)

    @pl.when(k == 0)
    def _():
        d_scr[...] = _first_tile_swiglu(
            xp_ref[...], modp_ref[0, 3 * sub + 0:3 * sub + 1, :], modp_ref[0, 3 * sub + 1:3 * sub + 2, :],
            (wg_hbm, wu_hbm, wd_hbm), (wg_scr, wu_scr, wd_scr), (stage_g, stage_u, stage_d),
            (sem_g, sem_u, sem_d))

    @pl.when(jnp.logical_and(k > 0, k < n_prompt))
    def _():
        d_prev = d_scr[...]
        d_new = matmuls()
        finish(d_prev)
        d_scr[...] = d_new

    @pl.when(k == n_prompt)
    def _():
        finish(d_scr[...])
        bsz = xs_ref.shape[0]
        x = _stack_time(xs_ref, t_len, d)
        delta = _swiglu(x, _mod_rows(mods_ref, 0, t_len, d), _mod_rows(mods_ref, 1, t_len, d),
                        wg_scr, wu_scr, wd_scr)
        y = _post(x, 0.5 * delta, _mod_rows(mods_ref, 2, t_len, d), g, b, alpha)
        for t in range(t_len):
            ys_ref[:, t * d:(t + 1) * d] = y[t * bsz:(t + 1) * bsz]


def _ffn(xp, xs, mod_p, mod_s, wg, wu, wd, lng, lnb, *, sub, alpha, tm):
    bp, tp, d = xp.shape
    bs, ts, _ = xs.shape
    f = wg.shape[1]
    assert tp % tm == 0
    per_seq = tp // tm
    n_prompt = bp * per_seq
    assert f % FFN_CHUNK == 0
    last = n_prompt - 1
    yp, ys = pl.pallas_call(
        functools.partial(_ffn_kernel, sub=sub, alpha=alpha, n_prompt=n_prompt, t_len=ts, d=d),
        grid=(n_prompt + 1,),
        in_specs=[pl.BlockSpec((tm, d), lambda k: (jnp.minimum(k, last), 0)),
                  pl.BlockSpec((1, 3 * N_SUB, d), lambda k: (jnp.minimum(k, last) // per_seq, 0, 0)),
                  pl.BlockSpec((tm, d), lambda k: (jnp.maximum(k - 1, 0), 0)),
                  pl.BlockSpec((1, 3 * N_SUB, d), lambda k: (jnp.maximum(k - 1, 0) // per_seq, 0, 0)),
                  _const_spec((bs, ts * d)), _sub_mod_spec(bs, d, sub),
                  pl.BlockSpec(memory_space=pl.ANY), pl.BlockSpec(memory_space=pl.ANY),
                  pl.BlockSpec(memory_space=pl.ANY),
                  _const_spec(lng.shape), _const_spec(lnb.shape)],
        out_specs=[pl.BlockSpec((tm, d), lambda k: (jnp.maximum(k - 1, 0), 0)),
                   pl.BlockSpec((bs, ts * d), lambda k: (0, 0))],
        out_shape=[jax.ShapeDtypeStruct((bp * tp, d), F32),
                   jax.ShapeDtypeStruct((bs, ts * d), F32)],
        scratch_shapes=[pltpu.VMEM((d, f), BF16), pltpu.VMEM((d, f), BF16), pltpu.VMEM((f, d), BF16),
                        pltpu.VMEM((tm, d), F32),
                        pltpu.VMEM((CAST_BUFFERS, d, FFN_CHUNK), F32), pltpu.VMEM((CAST_BUFFERS, d, FFN_CHUNK), F32),
                        pltpu.VMEM((CAST_BUFFERS, FFN_CHUNK, d), F32),
                        pltpu.SemaphoreType.DMA((CAST_BUFFERS,)), pltpu.SemaphoreType.DMA((CAST_BUFFERS,)),
                        pltpu.SemaphoreType.DMA((CAST_BUFFERS,))],
        compiler_params=pltpu.CompilerParams(dimension_semantics=("arbitrary",),
                                             vmem_limit_bytes=V7X_VMEM_LIMIT),
        name=f"ffn{sub}",
    )(xp.reshape(bp * tp, d), mod_p, xp.reshape(bp * tp, d), mod_p, xs.reshape(bs, ts * d), mod_s,
      wg, wu, wd, lng, lnb)
    return yp.reshape(bp, tp, d), ys.reshape(bs, ts, d)


def _tri_inverse_steps(lmats, c):
    ls = [l.astype(BF16) for l in lmats]
    ms = [jnp.dot(l, l, preferred_element_type=F32) for l in ls]
    yield
    qs = ms
    power = 2
    while 2 * power < c:
        mb = [m.astype(BF16) for m in ms]
        ms = [jnp.dot(m, m, preferred_element_type=F32) for m in mb]
        yield
        qm = [jnp.dot(q.astype(BF16), m.astype(BF16), preferred_element_type=F32) for q, m in zip(qs, ms)]
        yield
        qs = [q + m + x for q, m, x in zip(qs, ms, qm)]
        power *= 2
    rows = lax.broadcasted_iota(jnp.int32, (c, c), 0)
    cols = lax.broadcasted_iota(jnp.int32, (c, c), 1)
    eye = (rows == cols).astype(F32)
    lq = [jnp.dot(l, q.astype(BF16), preferred_element_type=F32) for l, q in zip(ls, qs)]
    yield
    return [eye - l + q - x for l, q, x in zip(lmats, qs, lq)]


def _gdn_local_steps(units, tril, strict):
    c = units[0][1].shape[0]
    a = [lax.dot_general(jnp.concatenate([kb, q], axis=0), k, (((1,), (1,)), ((), ())),
                         preferred_element_type=F32)
         for kb, q, k, _, _, _, _ in units]
    yield
    lmats, qks = [], []
    for a_u, (_, _, _, _, _, gcol, grow) in zip(a, units):
        decay = jnp.where(tril, jnp.exp(jnp.where(tril, gcol - grow, 0.0)), 0.0)
        lmats.append(jnp.where(strict, a_u[:c] * decay, 0.0))
        qks.append(jnp.where(tril, a_u[c:] * decay, 0.0))
    tms = yield from _tri_inverse_steps(lmats, c)
    uw = [jnp.dot(tm.astype(BF16), jnp.concatenate([vb, kbe], axis=1), preferred_element_type=F32)
          for tm, (_, _, _, vb, kbe, _, _) in zip(tms, units)]
    yield
    return [(x[:, :DN_DV], x[:, DN_DV:], qk) for x, qk in zip(uw, qks)]


def _alternate(primary, secondary):
    result = None
    done = False
    while not done:
        try:
            next(primary)
            yield
        except StopIteration as stop:
            result, done = stop.value, True
        if secondary is not None:
            try:
                next(secondary)
                yield
            except StopIteration:
                secondary = None
    if secondary is not None:
        yield from secondary
    return result


def _run_interleaved(*gens):
    live = list(gens)
    while live:
        for g in list(live):
            try:
                next(g)
            except StopIteration:
                live.remove(g)


def _gate_norm(o, og, norm_g):
    o = o * lax.rsqrt(jnp.mean(o * o, axis=-1, keepdims=True) + RMS_EPS) * norm_g
    return o * _silu(og)


def _proj_layout(dn_width, sc_width):
    off = {}
    pos = 0
    for name, width in (("og", dn_width), ("sc_b", sc_width), ("sc_c", sc_width), ("sc_h", sc_width)):
        off[name] = (pos, pos + width)
        pos += width
    return off


def _mixer_prompt_kernel(xf_ref, modf_ref, xb_ref, modb_ref, wq_ref, wr_ref, wab_ref, w_out_ref, cw_ref,
                         mw_ref, alog_ref, dtb_ref, ng_ref, lng_ref, lnb_ref,
                         y_ref, s_out_ref, cq_out_ref, cm_out_ref,
                         s_scr, cq_scr, cm_scr, gct_scr,
                         u_scr, w_scr, qk_scr, qd_scr, kd_scr, gl_scr, og_scr, osc_scr,
                         *, alpha, nseq, tt, tiles_per_group, n_tiles, off, n_conv, n_mconv):
    k = pl.program_id(0)
    rows_all = nseq * tt
    nchunk = rows_all // CHUNK
    per_seq = tt // CHUNK
    hk = DN_HEADS * DN_DK
    sub = 1
    head_lanes = [slice(h * DN_DK, (h + 1) * DN_DK) for h in range(DN_HEADS)]
    t_front = lax.rem(jnp.minimum(k, n_tiles - 1), tiles_per_group)
    t_back = lax.rem(jnp.maximum(k - 1, 0), tiles_per_group)
    keep_front = jnp.where(t_front == 0, 0.0, 1.0).astype(F32)
    keep_back = jnp.where(t_back == 0, 0.0, 1.0).astype(F32)

    def causal_conv(new_rows, hist_scr, s, w_ref, taps):
        ext = jnp.concatenate([hist_scr[s] * keep_front, new_rows], axis=0)
        acc = new_rows * w_ref[taps - 1:taps, :]
        for j in range(taps - 1):
            acc = acc + pltpu.roll(ext, taps - 1 - j, axis=0)[SUBLANES:] * w_ref[j:j + 1, :]
        hist_scr[s] = new_rows[tt - SUBLANES:]
        return acc

    def front():
        u = jnp.concatenate(
            [(xf_ref[s] * (1.0 + modf_ref[s, 3 * sub + 1:3 * sub + 2, :])
              + modf_ref[s, 3 * sub + 0:3 * sub + 1, :]).astype(BF16) for s in range(nseq)], axis=0)
        ab = jnp.dot(u, wab_ref[...], preferred_element_type=F32)
        pq = jnp.dot(u, wq_ref[...], preferred_element_type=F32)
        yield

        beta_all = jax.nn.sigmoid(ab)
        g = -jnp.exp(alog_ref[...]) * _softplus(ab + dtb_ref[...])
        rows = lax.broadcasted_iota(jnp.int32, (CHUNK, CHUNK), 0)
        cols = lax.broadcasted_iota(jnp.int32, (CHUNK, CHUNK), 1)
        tril = rows >= cols
        strict = rows > cols
        tril_f = tril.astype(F32)
        gc_parts, gl_parts = [], []
        for c in range(nchunk):
            gc_c = _mm_f32(tril_f, g[c * CHUNK:(c + 1) * CHUNK, :])
            gc_parts.append(gc_c)
            gl_parts.append(jnp.broadcast_to(gc_c[CHUNK - 1:CHUNK, :], (CHUNK, AB_PAD)))
        yield
        p = {}

        def project_rest(names):
            for name in names:
                lo, hi = off[name]
                p[name] = jnp.dot(u, wr_ref[:, lo:hi], preferred_element_type=F32)
                yield

        yield from project_rest(("sc_c", "sc_h"))
        gc = jnp.concatenate(gc_parts, axis=0)
        gl_rows = jnp.concatenate(gl_parts, axis=0)
        gc_t = gc.T
        for c in range(nchunk):
            gct_scr[c] = gc_t[0:SUBLANES, c * CHUNK:(c + 1) * CHUNK]

        acts = []
        for s in range(nseq):
            pq_s = pq[s * tt:(s + 1) * tt]
            acts.append(_silu(causal_conv(pq_s, cq_scr, s, cw_ref, n_conv)))
            cq_out_ref[s] = pq_s[tt - (n_conv - 1):]
        act = jnp.concatenate(acts, axis=0)

        eg_all = jnp.exp(gc)
        ekd_all = jnp.exp(gl_rows - gc)
        egl_all = jnp.exp(gl_rows)
        per_head = []
        for h, lanes in enumerate(head_lanes):
            q = _l2norm(act[:, h * DN_DK:(h + 1) * DN_DK]) * (DN_DK ** -0.5)
            kk = _l2norm(act[:, hk + h * DN_DK:hk + (h + 1) * DN_DK])
            v = act[:, 2 * hk + h * DN_DV:2 * hk + (h + 1) * DN_DV]
            beta = beta_all[:, DN_HEADS + h:DN_HEADS + h + 1]
            eg = eg_all[:, h:h + 1]
            kb = kk * beta
            per_head.append((kb.astype(BF16), q.astype(BF16), kk.astype(BF16), (v * beta).astype(BF16),
                             (kb * eg).astype(BF16)))
            qd_scr[:, lanes] = (q * eg).astype(BF16)
            kd_scr[:, lanes] = (kk * ekd_all[:, h:h + 1]).astype(BF16)
            for c in range(nchunk):
                gl_scr[c * SUBLANES:(c + 1) * SUBLANES, lanes] = jnp.broadcast_to(
                    egl_all[c * CHUNK:c * CHUNK + SUBLANES, h:h + 1], (SUBLANES, DN_DV))

        z = p["sc_c"] * p["sc_h"]
        zcs = []
        for s in range(nseq):
            z_s = z[s * tt:(s + 1) * tt]
            zcs.append(causal_conv(z_s, cm_scr, s, mw_ref, n_mconv))
            cm_out_ref[s] = z_s[tt - (n_mconv - 1):]

        where = [(c, h, lanes) for c in range(nchunk) for h, lanes in enumerate(head_lanes)]
        units = [tuple(arr[c * CHUNK:(c + 1) * CHUNK] for arr in per_head[h])
                 + (gc[c * CHUNK:(c + 1) * CHUNK, h:h + 1], gct_scr[c, h:h + 1, :])
                 for c, h, lanes in where]
        local = yield from _alternate(_gdn_local_steps(units, tril, strict), project_rest(("sc_b", "og")))
        osc_scr[...] = p["sc_b"] * jnp.concatenate(zcs, axis=0)
        og_scr[...] = p["og"]
        for (c, h, lanes), (u_c, w_c, qk_c) in zip(where, local):
            u_scr[c * CHUNK:(c + 1) * CHUNK, lanes] = u_c
            w_scr[c * CHUNK:(c + 1) * CHUNK, lanes] = w_c.astype(BF16)
            qk_scr[c * DN_HEADS + h] = qk_c.astype(BF16)

    def back():
        u_all, w_all, qd_all, kd_all = u_scr[...], w_scr[...], qd_scr[...], kd_scr[...]
        gl_all, og, o_sc = gl_scr[...], og_scr[...], osc_scr[...]
        qks = [qk_scr[i] for i in range(nchunk * DN_HEADS)]
        xs = [xb_ref[s] for s in range(nseq)]
        gates = [modb_ref[s, 3 * sub + 2:3 * sub + 3, :] for s in range(nseq)]
        states = [s_scr[i] * keep_back for i in range(nseq * DN_HEADS)]
        o_rows = [[None] * DN_HEADS for _ in range(nchunk)]
        for j in range(per_seq):
            chains = [(s, s * per_seq + j, h, lanes) for s in range(nseq) for h, lanes in enumerate(head_lanes)]
            rws = [slice(c * CHUNK, (c + 1) * CHUNK) for _, c, _, _ in chains]
            r = [jnp.dot(jnp.concatenate([w_all[rw, lanes], qd_all[rw, lanes]], axis=0),
                         states[s * DN_HEADS + h].astype(BF16), preferred_element_type=F32)
                 for (s, c, h, lanes), rw in zip(chains, rws)]
            yield
            v_new = [(u_all[rw, lanes] - r_u[:CHUNK]).astype(BF16)
                     for (s, c, h, lanes), rw, r_u in zip(chains, rws, r)]
            for (s, c, h, lanes), r_u, v_u in zip(chains, r, v_new):
                o_rows[c][h] = r_u[CHUNK:] + jnp.dot(qks[c * DN_HEADS + h], v_u, preferred_element_type=F32)
            for (s, c, h, lanes), rw, v_u in zip(chains, rws, v_new):
                i = s * DN_HEADS + h
                states[i] = (states[i] * gl_all[c * SUBLANES:c * SUBLANES + 1, lanes]
                             + lax.dot_general(kd_all[rw, lanes], v_u, (((0,), (0,)), ((), ())),
                                               preferred_element_type=F32))
            yield
        for i, st in enumerate(states):
            s_scr[i] = st
        for s in range(nseq):
            for h in range(DN_HEADS):
                s_out_ref[s, h] = states[s * DN_HEADS + h]

        o_dn = jnp.concatenate(
            [_gate_norm(jnp.concatenate([o_rows[c][h] for c in range(nchunk)], axis=0),
                        og[:, lanes], ng_ref[...]) for h, lanes in enumerate(head_lanes)], axis=1)
        dn_w = o_dn.shape[1]
        mix = (jnp.dot(o_dn.astype(BF16), w_out_ref[0:dn_w, :], preferred_element_type=F32)
               + jnp.dot(o_sc.astype(BF16), w_out_ref[dn_w:, :], preferred_element_type=F32))
        yield
        for s in range(nseq):
            y_ref[s] = _post(xs[s], mix[s * tt:(s + 1) * tt], gates[s],
                             lng_ref[sub:sub + 1, :], lnb_ref[sub:sub + 1, :], alpha)

    @pl.when(k == 0)
    def _():
        s_scr[...] = jnp.zeros_like(s_scr)
        cq_scr[...] = jnp.zeros_like(cq_scr)
        cm_scr[...] = jnp.zeros_like(cm_scr)
        _run_interleaved(front())

    @pl.when(jnp.logical_and(k > 0, k < n_tiles))
    def _():
        _run_interleaved(back(), front())

    @pl.when(k == n_tiles)
    def _():
        _run_interleaved(back())


def _mixer_prompt(x, mod, w_in, w_out, conv_w, mconv_w, alog, dtb, norm_g, lng, lnb, *, alpha, nseq, tt, off):
    bsz, t, d = x.shape
    assert t % tt == 0 and tt % LANES == 0 and bsz % nseq == 0
    nt = t // tt
    n_tiles = (bsz // nseq) * nt
    rows = nseq * tt
    qkv_dim = conv_w.shape[-1]
    sc_w = off["sc_c"][1] - off["sc_c"][0]
    dn_w = DN_HEADS * DN_DV
    n_conv, n_mconv = conv_w.shape[0], mconv_w.shape[0]
    assert max(n_conv, n_mconv) - 1 <= SUBLANES <= tt
    nchunk = rows // CHUNK
    kern = functools.partial(_mixer_prompt_kernel, alpha=alpha, nseq=nseq, tt=tt, tiles_per_group=nt,
                             n_tiles=n_tiles, off=off, n_conv=n_conv, n_mconv=n_mconv)

    def front_tile(k):
        return jnp.minimum(k, n_tiles - 1)

    def back_tile(k):
        return jnp.maximum(k - 1, 0)

    y, s_out, cq, cm = pl.pallas_call(
        kern,
        grid=(n_tiles + 1,),
        in_specs=[pl.BlockSpec((nseq, tt, d), lambda k: (front_tile(k) // nt, front_tile(k) % nt, 0)),
                  pl.BlockSpec((nseq, 3 * N_SUB, d), lambda k: (front_tile(k) // nt, 0, 0)),
                  pl.BlockSpec((nseq, tt, d), lambda k: (back_tile(k) // nt, back_tile(k) % nt, 0)),
                  pl.BlockSpec((nseq, 3 * N_SUB, d), lambda k: (back_tile(k) // nt, 0, 0)),
                  *[_const_spec(w.shape) for w in w_in], _const_spec(w_out.shape),
                  _const_spec(conv_w.shape), _const_spec(mconv_w.shape),
                  _const_spec(alog.shape), _const_spec(dtb.shape), _const_spec(norm_g.shape),
                  _const_spec(lng.shape), _const_spec(lnb.shape)],
        out_specs=[pl.BlockSpec((nseq, tt, d), lambda k: (back_tile(k) // nt, back_tile(k) % nt, 0)),
                   pl.BlockSpec((nseq, DN_HEADS, DN_DK, DN_DV), lambda k: (back_tile(k) // nt, 0, 0, 0)),
                   pl.BlockSpec((nseq, n_conv - 1, qkv_dim), lambda k: (front_tile(k) // nt, 0, 0)),
                   pl.BlockSpec((nseq, n_mconv - 1, sc_w), lambda k: (front_tile(k) // nt, 0, 0))],
        out_shape=[jax.ShapeDtypeStruct((bsz, t, d), F32),
                   jax.ShapeDtypeStruct((bsz, DN_HEADS, DN_DK, DN_DV), F32),
                   jax.ShapeDtypeStruct((bsz, n_conv - 1, qkv_dim), F32),
                   jax.ShapeDtypeStruct((bsz, n_mconv - 1, sc_w), F32)],
        scratch_shapes=[pltpu.VMEM((nseq * DN_HEADS, DN_DK, DN_DV), F32),
                        pltpu.VMEM((nseq, SUBLANES, qkv_dim), F32),
                        pltpu.VMEM((nseq, SUBLANES, sc_w), F32),
                        pltpu.VMEM((nchunk, SUBLANES, CHUNK), F32),
                        pltpu.VMEM((rows, dn_w), F32),
                        pltpu.VMEM((rows, dn_w), BF16),
                        pltpu.VMEM((nchunk * DN_HEADS, CHUNK, CHUNK), BF16),
                        pltpu.VMEM((rows, dn_w), BF16),
                        pltpu.VMEM((rows, dn_w), BF16),
                        pltpu.VMEM((nchunk * SUBLANES, dn_w), F32),
                        pltpu.VMEM((rows, dn_w), F32),
                        pltpu.VMEM((rows, sc_w), F32)],
        compiler_params=pltpu.CompilerParams(dimension_semantics=("arbitrary",),
                                             vmem_limit_bytes=V7X_VMEM_LIMIT),
        name="mixer_prompt",
    )(x, mod, x, mod, *w_in, w_out, conv_w, mconv_w, alog, dtb, norm_g, lng, lnb)
    return y, s_out, cq, cm


_PACK = SUBLANES


def _sample_phase_a(x_ref, mod_ref, wq_ref, wr_ref, wab_ref, cw_ref, mw_ref, alog_ref, dtb_ref, cq_in_ref, cm_in_ref,
                    cq_out_ref, cm_out_ref, og_scr, osc_scr, lhs_scr, u_scr, kd_scr, gl_scr, qk_scr,
                    *, t_len, d, off, n_conv, n_mconv):
    bsz = x_ref.shape[0]
    hk = DN_HEADS * DN_DK
    sc_w = off["sc_c"][1] - off["sc_c"][0]

    x = _stack_time(x_ref, t_len, d)
    u = (x * (1.0 + _mod_rows(mod_ref, 1, t_len, d)) + _mod_rows(mod_ref, 0, t_len, d)).astype(BF16)
    pq = jnp.dot(u, wq_ref[...], preferred_element_type=F32)
    ab = jnp.dot(u, wab_ref[...], preferred_element_type=F32)
    rest = jnp.dot(u, wr_ref[...], preferred_element_type=F32)
    p = {name: rest[:, lo:hi] for name, (lo, hi) in off.items()}

    def rows(arr, t):
        return arr[t * bsz:(t + 1) * bsz]

    seq = [cq_in_ref[j] for j in range(n_conv - 1)] + [rows(pq, t) for t in range(t_len)]
    act = []
    for t in range(t_len):
        acc = None
        for j in range(n_conv):
            term = seq[t + j] * cw_ref[j:j + 1, :]
            acc = term if acc is None else acc + term
        act.append(_silu(acc))
    for j in range(n_conv - 1):
        cq_out_ref[j] = seq[t_len + j]

    z = p["sc_c"] * p["sc_h"]
    zseq = ([cm_in_ref[:, j * sc_w:(j + 1) * sc_w] for j in range(n_mconv - 1)]
            + [rows(z, t) for t in range(t_len)])
    zc = []
    for t in range(t_len):
        acc = None
        for j in range(n_mconv):
            term = zseq[t + j] * mw_ref[j:j + 1, :]
            acc = term if acc is None else acc + term
        zc.append(acc)
    for j in range(n_mconv - 1):
        cm_out_ref[:, j * sc_w:(j + 1) * sc_w] = zseq[t_len + j]
    osc_scr[...] = p["sc_b"] * jnp.concatenate(zc, axis=0)
    og_scr[...] = p["og"]

    beta_all = jax.nn.sigmoid(ab)
    g_all = -jnp.exp(alog_ref[...]) * _softplus(ab + dtb_ref[...])
    zero = jnp.zeros((bsz, LANES), F32)

    for h in range(DN_HEADS):
        q = [_l2norm(a[:, h * DN_DK:(h + 1) * DN_DK]) * (DN_DK ** -0.5) for a in act]
        k = [_l2norm(a[:, hk + h * DN_DK:hk + (h + 1) * DN_DK]) for a in act]
        v = [a[:, 2 * hk + h * DN_DV:2 * hk + (h + 1) * DN_DV] for a in act]
        beta = [rows(beta_all, t)[:, DN_HEADS + h:DN_HEADS + h + 1] for t in range(t_len)]
        g = [rows(g_all, t)[:, h:h + 1] for t in range(t_len)]
        gc = [g[0]]
        for t in range(1, t_len):
            gc.append(gc[-1] + g[t])
        kb = [k[t] * beta[t] for t in range(t_len)]
        vb = [v[t] * beta[t] for t in range(t_len)]
        eg = [jnp.exp(gc[t]) for t in range(t_len)]
        kbe = [kb[t] * eg[t] for t in range(t_len)]
        lm = [[None] * t_len for _ in range(t_len)]
        n_qk = 0
        for i in range(t_len):
            for j in range(i + 1):
                dec = jnp.exp(gc[i] - gc[j])
                qk_scr[h, n_qk] = jnp.broadcast_to(_rowsum(q[i] * k[j]) * dec, (bsz, LANES))
                n_qk += 1
                if j < i:
                    lm[i][j] = _rowsum(kb[i] * k[j]) * dec
        tm = [[None] * t_len for _ in range(t_len)]
        for i in range(t_len):
            for j in range(i):
                acc = lm[i][j]
                for m in range(j + 1, i):
                    acc = acc + lm[i][m] * tm[m][j]
                tm[i][j] = -acc
        for i in range(t_len):
            ui, wi = vb[i], kbe[i]
            for j in range(i):
                ui = ui + tm[i][j] * vb[j]
                wi = wi + tm[i][j] * kbe[j]
            lhs_scr[h, pl.ds(i, bsz, stride=_PACK), :] = wi
            lhs_scr[h, pl.ds(t_len + i, bsz, stride=_PACK), :] = q[i] * eg[i]
            u_scr[h, pl.ds(i, bsz, stride=_PACK), :] = ui
            u_scr[h, pl.ds(t_len + i, bsz, stride=_PACK), :] = zero
            kd_scr[h, pl.ds(i, bsz, stride=_PACK), :] = k[i] * jnp.exp(gc[t_len - 1] - gc[i])
            kd_scr[h, pl.ds(t_len + i, bsz, stride=_PACK), :] = zero
        gl_scr[h] = jnp.broadcast_to(jnp.exp(gc[t_len - 1]), (bsz, LANES))


def _sample_phase_c(x_ref, mod_ref, w_out_ref, ng_ref, lng_ref, lnb_ref, y_ref,
                    og_scr, osc_scr, res_scr, vn_scr, qk_scr, *, alpha, t_len, d):
    bsz = x_ref.shape[0]
    sub = 1
    x = _stack_time(x_ref, t_len, d)
    gate = _mod_rows(mod_ref, 2, t_len, d)
    og = og_scr[...]
    per_t = []
    for i in range(t_len):
        heads = []
        for h in range(DN_HEADS):
            oi = res_scr[h, pl.ds(t_len + i, bsz, stride=_PACK), :]
            base = i * (i + 1) // 2
            for j in range(i + 1):
                oi = oi + qk_scr[h, base + j] * vn_scr[h, pl.ds(j, bsz, stride=_PACK), :]
            heads.append(_gate_norm(oi, og[i * bsz:(i + 1) * bsz, h * DN_DV:(h + 1) * DN_DV], ng_ref[...]))
        per_t.append(jnp.concatenate(heads, axis=1))
    o_dn = jnp.concatenate(per_t, axis=0)
    dn_w = o_dn.shape[1]
    mix = (jnp.dot(o_dn.astype(BF16), w_out_ref[0:dn_w, :], preferred_element_type=F32)
           + jnp.dot(osc_scr[...].astype(BF16), w_out_ref[dn_w:, :], preferred_element_type=F32))
    y = _post(x, mix, gate, lng_ref[sub:sub + 1, :], lnb_ref[sub:sub + 1, :], alpha)
    for t in range(t_len):
        y_ref[:, t * d:(t + 1) * d] = y[t * bsz:(t + 1) * bsz]


def _mixer_sample_kernel(x_ref, mod_ref, wq_ref, wr_ref, wab_ref, w_out_ref, cw_ref, mw_ref, alog_ref, dtb_ref, ng_ref,
                         lng_ref, lnb_ref, cq_in_ref, cm_in_ref, s_in_ref,
                         y_ref, cq_out_ref, cm_out_ref, s_out_ref,
                         og_scr, osc_scr, lhs_scr, u_scr, kd_scr, gl_scr, qk_scr, res_scr, vn_scr,
                         *, alpha, t_len, d, off, n_conv, n_mconv, bt):
    step = pl.program_id(0)
    assert 2 * t_len == _PACK, "packed buffers hold [T rows | T rows] per sequence"

    @pl.when(step == 0)
    def _():
        _sample_phase_a(x_ref, mod_ref, wq_ref, wr_ref, wab_ref, cw_ref, mw_ref, alog_ref, dtb_ref, cq_in_ref, cm_in_ref,
                        cq_out_ref, cm_out_ref, og_scr, osc_scr, lhs_scr, u_scr, kd_scr, gl_scr, qk_scr,
                        t_len=t_len, d=d, off=off, n_conv=n_conv, n_mconv=n_mconv)

    top = (lax.broadcasted_iota(jnp.int32, (_PACK, LANES), 0) < t_len).astype(F32)

    chains = [(bl, h) for bl in range(bt) for h in range(DN_HEADS)]
    seq0 = step * bt
    row0 = [pl.multiple_of((seq0 + bl) * _PACK, _PACK) for bl in range(bt)]
    states = [s_in_ref[bl, h] for bl, h in chains]
    r = [_mm(lhs_scr[h, pl.ds(row0[bl], _PACK), :], s) for (bl, h), s in zip(chains, states)]
    vns = []
    for (bl, h), r_u in zip(chains, r):
        res_scr[h, pl.ds(row0[bl], _PACK), :] = r_u
        vn = (u_scr[h, pl.ds(row0[bl], _PACK), :] - r_u) * top
        vn_scr[h, pl.ds(row0[bl], _PACK), :] = vn
        vns.append(vn)
    upd = [_mm_at(kd_scr[h, pl.ds(row0[bl], _PACK), :], vn) for (bl, h), vn in zip(chains, vns)]
    for (bl, h), s, s_add in zip(chains, states, upd):
        s_out_ref[bl, h] = s * gl_scr[h, pl.ds(seq0 + bl, 1), :] + s_add

    @pl.when(step == pl.num_programs(0) - 1)
    def _():
        _sample_phase_c(x_ref, mod_ref, w_out_ref, ng_ref, lng_ref, lnb_ref, y_ref,
                        og_scr, osc_scr, res_scr, vn_scr, qk_scr, alpha=alpha, t_len=t_len, d=d)


def _mixer_sample(x, mod, w_in, w_out, conv_w, mconv_w, alog, dtb, norm_g, lng, lnb, s_in, cq_in, cm_in,
                  *, alpha, off, bt):
    bsz, t, d = x.shape
    assert bsz % bt == 0
    sc_w = off["sc_c"][1] - off["sc_c"][0]
    dn_w = DN_HEADS * DN_DV
    n_conv, n_mconv = conv_w.shape[0], mconv_w.shape[0]
    n_qk = t * (t + 1) // 2
    x2 = x.reshape(bsz, t * d)
    cq2 = jnp.swapaxes(cq_in, 0, 1)
    cm2 = cm_in.reshape(bsz, (n_mconv - 1) * sc_w)
    kern = functools.partial(_mixer_sample_kernel, alpha=alpha, t_len=t, d=d, off=off,
                             n_conv=n_conv, n_mconv=n_mconv, bt=bt)
    consts = (*w_in, w_out, conv_w, mconv_w, alog, dtb, norm_g, lng, lnb, cq2, cm2)
    state_spec = pl.BlockSpec((bt, DN_HEADS, DN_DK, DN_DV), lambda i: (i, 0, 0, 0))
    packed = pltpu.VMEM((DN_HEADS, _PACK * bsz, LANES), F32)
    y, cq, cm, s_out = pl.pallas_call(
        kern,
        grid=(bsz // bt,),
        in_specs=([_const_spec(x2.shape), _sub_mod_spec(bsz, d, 1)]
                  + [_const_spec(a.shape) for a in consts] + [state_spec]),
        out_specs=[pl.BlockSpec(x2.shape, lambda i: (0, 0)),
                   pl.BlockSpec(cq2.shape, lambda i: (0, 0, 0)),
                   pl.BlockSpec(cm2.shape, lambda i: (0, 0)),
                   state_spec],
        out_shape=[jax.ShapeDtypeStruct(x2.shape, F32),
                   jax.ShapeDtypeStruct(cq2.shape, F32),
                   jax.ShapeDtypeStruct(cm2.shape, F32),
                   jax.ShapeDtypeStruct(s_in.shape, F32)],
        scratch_shapes=[pltpu.VMEM((t * bsz, dn_w), F32),
                        pltpu.VMEM((t * bsz, sc_w), F32),
                        packed,
                        packed,
                        packed,
                        pltpu.VMEM((DN_HEADS, bsz, LANES), F32),
                        pltpu.VMEM((DN_HEADS, n_qk, bsz, LANES), F32),
                        packed,
                        packed],
        compiler_params=pltpu.CompilerParams(dimension_semantics=("arbitrary",),
                                             vmem_limit_bytes=V7X_VMEM_LIMIT),
        name="mixer_sample",
    )(x2, mod, *consts, s_in)
    return y.reshape(bsz, t, d), s_out, jnp.swapaxes(cq, 0, 1), cm.reshape(bsz, n_mconv - 1, sc_w)


def _pad_lanes(v):
    return jnp.zeros((1, AB_PAD), F32).at[0, :v.shape[0]].set(v)


def kernel(x_prompt, x_sample, state_ssm, state_conv_qkv, state_conv_mix, c_prompt, c_sample, w_ada, b_ada, ln_g, ln_b, ffn1_wg, ffn1_wu, ffn1_wd, ffn2_wg, ffn2_wu, ffn2_wd, w_in, conv_qkv_w, a_log, dt_bias, dn_norm_g, conv_mix_w, w_out):
    depth = w_ada.shape[0]
    alpha = (2 * depth) ** 0.25
    bp, tp, d = x_prompt.shape
    qkv_dim = conv_qkv_w.shape[-1]
    sc_width = conv_mix_w.shape[-1]
    dn_width = DN_HEADS * DN_DV
    off = _proj_layout(dn_width, sc_width)
    tm = min(512, tp)
    nseq = 2 if bp % 2 == 0 else 1
    tt = min(512 // nseq, tp)
    bt = min(16, x_sample.shape[0])

    hp, hs = x_prompt, x_sample
    outs = [[] for _ in range(6)]
    for l in range(depth):
        mod_p, mod_s, *w_in_p = _adaln(c_prompt, c_sample, w_ada[l], b_ada[l].reshape(1, -1),
                                       jnp.swapaxes(w_in[l], 0, 1), qkv_dim)
        mod_p = mod_p.reshape(bp, 3 * N_SUB, d)
        lng, lnb = ln_g[l], ln_b[l]
        w1 = (ffn1_wg[l], ffn1_wu[l], ffn1_wd[l])
        w2 = (ffn2_wg[l], ffn2_wu[l], ffn2_wd[l])
        w_out_b = w_out[l].astype(BF16)
        alog, dtb = _pad_lanes(a_log[l]), _pad_lanes(dt_bias[l])
        norm_g = dn_norm_g[l].reshape(1, -1)
        mix_args = (w_in_p, w_out_b, conv_qkv_w[l], conv_mix_w[l], alog, dtb, norm_g, lng, lnb)

        hp, hs = _ffn(hp, hs, mod_p, mod_s, *w1, lng, lnb, sub=0, alpha=alpha, tm=tm)
        hp, a1, a2, a3 = _mixer_prompt(hp, mod_p, *mix_args, alpha=alpha, nseq=nseq, tt=tt, off=off)
        hs, b1, b2, b3 = _mixer_sample(hs, mod_s, *mix_args, state_ssm[l], state_conv_qkv[l],
                                       state_conv_mix[l], alpha=alpha, off=off, bt=bt)
        hp, hs = _ffn(hp, hs, mod_p, mod_s, *w2, lng, lnb, sub=2, alpha=alpha, tm=tm)
        for lst, val in zip(outs, (a1, a2, a3, b1, b2, b3)):
            lst.append(val)
    return (hp, hs) + tuple(jnp.stack(o) for o in outs)
```

```python
import functools

import jax
import jax.numpy as jnp
from jax import lax
from jax.experimental import pallas as pl
from jax.experimental.pallas import tpu as pltpu

F32 = jnp.float32
BF16 = jnp.bfloat16

LN_EPS = 1e-5
RMS_EPS = 1e-6
N_SUB = 3
DN_HEADS = 4
DN_DK = 128
DN_DV = 128
CHUNK = 64
LANES = 128
SUBLANES = 8
AB_PAD = LANES

V7X_VMEM_LIMIT = 60000 * 1024
CAST_BUFFERS = 3
FFN_CHUNK = 256


def _mm(a, b):
    return jnp.dot(a.astype(BF16), b.astype(BF16), preferred_element_type=F32)


def _mm_at(a, b):
    return lax.dot_general(a.astype(BF16), b.astype(BF16), (((0,), (0,)), ((), ())),
                           preferred_element_type=F32)


def _mm_f32(a, b):
    return jnp.dot(a, b, preferred_element_type=F32, precision=lax.Precision.HIGHEST)


def _silu(x):
    return x * jax.nn.sigmoid(x)


def _softplus(x):
    return jnp.maximum(x, 0.0) + jnp.log1p(jnp.exp(-jnp.abs(x)))


def _layer_norm(y, g, b):
    mu = jnp.mean(y, axis=-1, keepdims=True)
    yc = y - mu
    var = jnp.mean(yc * yc, axis=-1, keepdims=True)
    return yc * lax.rsqrt(var + LN_EPS) * g + b


def _post(x, delta, gate, g, b, alpha):
    return _layer_norm(alpha * x + gate * delta, g, b)


def _rowsum(x):
    return jnp.sum(x, axis=-1, keepdims=True)


def _l2norm(x):
    return x * lax.rsqrt(_rowsum(x * x) + RMS_EPS)


def _const_spec(shape):
    nd = len(shape)
    return pl.BlockSpec(shape, lambda *_: (0,) * nd, pipeline_mode=pl.Buffered(1))


def _adaln_kernel(cp_ref, cs_ref, w_ref, b_ref, win_ref, op_ref, os_ref, wq_ref, wr_ref, wab_ref, *, qkv_dim):
    mp = cp_ref.shape[0]
    c = jnp.concatenate([cp_ref[...], cs_ref[...]], axis=0)
    mod = _mm(_silu(c), w_ref[...]) + b_ref[...]
    op_ref[...] = mod[:mp]
    os_ref[...] = mod[mp:]

    gates = 2 * DN_HEADS
    wq_ref[...] = win_ref[0:qkv_dim, :].T.astype(BF16)
    wr_ref[...] = win_ref[qkv_dim + gates:, :].T.astype(BF16)
    ab_rows = jnp.concatenate([win_ref[qkv_dim:qkv_dim + gates, :],
                               jnp.zeros((AB_PAD - gates, win_ref.shape[1]), F32)], axis=0)
    wab_ref[...] = ab_rows.T.astype(BF16)


def _adaln(c_p, c_s, w, b, w_in_t, qkv_dim):
    (mp, d), ms = c_p.shape, c_s.shape[0]
    n = w.shape[1]
    proj = w_in_t.shape[0]
    rest = proj - qkv_dim - 2 * DN_HEADS
    assert qkv_dim % LANES == 0 and rest % LANES == 0 and (2 * DN_HEADS) % SUBLANES == 0
    steps = max(g for g in (1, 2, 4) if n % (g * LANES) == 0 and d % (g * LANES) == 0)
    tn, rows = n // steps, d // steps
    return pl.pallas_call(
        functools.partial(_adaln_kernel, qkv_dim=qkv_dim),
        grid=(steps,),
        in_specs=[_const_spec((mp, d)), _const_spec((ms, d)),
                  pl.BlockSpec((d, tn), lambda j: (0, j)),
                  pl.BlockSpec((1, tn), lambda j: (0, j)),
                  pl.BlockSpec((proj, rows), lambda j: (0, j))],
        out_specs=[pl.BlockSpec((mp, tn), lambda j: (0, j)), pl.BlockSpec((ms, tn), lambda j: (0, j)),
                   pl.BlockSpec((rows, qkv_dim), lambda j: (j, 0)),
                   pl.BlockSpec((rows, rest), lambda j: (j, 0)),
                   pl.BlockSpec((rows, AB_PAD), lambda j: (j, 0))],
        out_shape=[jax.ShapeDtypeStruct((mp, n), F32), jax.ShapeDtypeStruct((ms, n), F32),
                   jax.ShapeDtypeStruct((d, qkv_dim), BF16), jax.ShapeDtypeStruct((d, rest), BF16),
                   jax.ShapeDtypeStruct((d, AB_PAD), BF16)],
        compiler_params=pltpu.CompilerParams(dimension_semantics=("arbitrary",),
                                             vmem_limit_bytes=V7X_VMEM_LIMIT),
        name="adaln",
    )(c_p, c_s, w, b, w_in_t)


def _swiglu(x, shift, scale, wg_ref, wu_ref, wd_ref):
    u = (x * (1.0 + scale) + shift).astype(BF16)
    hg = jnp.dot(u, wg_ref[...], preferred_element_type=F32)
    hu = jnp.dot(u, wu_ref[...], preferred_element_type=F32)
    h = (_silu(hg) * hu).astype(BF16)
    return jnp.dot(h, wd_ref[...], preferred_element_type=F32)


def _stack_time(ref, t_len, d):
    if len(ref.shape) == 3:
        return jnp.concatenate([ref[:, t, :] for t in range(t_len)], axis=0)
    return jnp.concatenate([ref[:, t * d:(t + 1) * d] for t in range(t_len)], axis=0)


def _mod_rows(mod_ref, idx, t_len, d):
    m = mod_ref[:, idx * d:(idx + 1) * d]
    return jnp.concatenate([m] * t_len, axis=0)


def _sub_mod_spec(bsz, d, sub):
    return pl.BlockSpec((bsz, 3 * d), lambda *_: (0, sub), pipeline_mode=pl.Buffered(1))


def _first_tile_swiglu(x, shift, scale, w_hbm, w_scr, stages, sems):
    wg_hbm, wu_hbm, wd_hbm = w_hbm
    wg_scr, wu_scr, wd_scr = w_scr
    nbuf = stages[0].shape[0]
    cw = stages[0].shape[2]
    n = wg_hbm.shape[1] // cw

    def copies(j):
        slot = j % nbuf
        cols = pl.ds(j * cw, cw)
        return (pltpu.make_async_copy(wg_hbm.at[:, cols], stages[0].at[slot], sems[0].at[slot]),
                pltpu.make_async_copy(wu_hbm.at[:, cols], stages[1].at[slot], sems[1].at[slot]),
                pltpu.make_async_copy(wd_hbm.at[cols, :], stages[2].at[slot], sems[2].at[slot]))

    queue = [copies(j) for j in range(n)]
    ahead = nbuf - 1
    for group in queue[:ahead]:
        for copy in group:
            copy.start()
    u = (x * (1.0 + scale) + shift).astype(BF16)
    acc = None
    for j, group in enumerate(queue):
        if j + ahead < n:
            for copy in queue[j + ahead]:
                copy.start()
        for copy in group:
            copy.wait()
        slot, cols = j % nbuf, slice(j * cw, (j + 1) * cw)
        wg_scr[:, cols] = stages[0][slot].astype(BF16)
        wu_scr[:, cols] = stages[1][slot].astype(BF16)
        wd_scr[cols, :] = stages[2][slot].astype(BF16)
        hg = jnp.dot(u, wg_scr[:, cols], preferred_element_type=F32)
        hu = jnp.dot(u, wu_scr[:, cols], preferred_element_type=F32)
        part = jnp.dot((_silu(hg) * hu).astype(BF16), wd_scr[cols, :], preferred_element_type=F32)
        acc = part if acc is None else acc + part
    return acc


def _ffn_kernel(xp_ref, modp_ref, xq_ref, modq_ref, xs_ref, mods_ref, wg_hbm, wu_hbm, wd_hbm, lng_ref, lnb_ref,
                yp_ref, ys_ref, wg_scr, wu_scr, wd_scr, d_scr, stage_g, stage_u, stage_d, sem_g, sem_u, sem_d,
                *, sub, alpha, n_prompt, t_len, d):
    k = pl.program_id(0)
    g, b = lng_ref[sub:sub + 1, :], lnb_ref[sub:sub + 1, :]

    def matmuls():
        return _swiglu(xp_ref[...], modp_ref[0, 3 * sub + 0:3 * sub + 1, :], modp_ref[0, 3 * sub + 1:3 * sub + 2, :],
                       wg_scr, wu_scr, wd_scr)

    def finish(d_prev):
        yp_ref[...] = _post(xq_ref[...], 0.5 * d_prev, modq_ref[0, 3 * sub + 2:3 * sub + 3, :], g, b, alpha)

    @pl.when(k == 0)
    def _():
        d_scr[...] = _first_tile_swiglu(
            xp_ref[...], modp_ref[0, 3 * sub + 0:3 * sub + 1, :], modp_ref[0, 3 * sub + 1:3 * sub + 2, :],
            (wg_hbm, wu_hbm, wd_hbm), (wg_scr, wu_scr, wd_scr), (stage_g, stage_u, stage_d),
            (sem_g, sem_u, sem_d))

    @pl.when(jnp.logical_and(k > 0, k < n_prompt))
    def _():
        d_prev = d_scr[...]
        d_new = matmuls()
        finish(d_prev)
        d_scr[...] = d_new

    @pl.when(k == n_prompt)
    def _():
        finish(d_scr[...])
        bsz = xs_ref.shape[0]
        x = _stack_time(xs_ref, t_len, d)
        delta = _swiglu(x, _mod_rows(mods_ref, 0, t_len, d), _mod_rows(mods_ref, 1, t_len, d),
                        wg_scr, wu_scr, wd_scr)
        y = _post(x, 0.5 * delta, _mod_rows(mods_ref, 2, t_len, d), g, b, alpha)
        for t in range(t_len):
            if len(ys_ref.shape) == 3:
                ys_ref[:, t, :] = y[t * bsz:(t + 1) * bsz]
            else:
                ys_ref[:, t * d:(t + 1) * d] = y[t * bsz:(t + 1) * bsz]


def _ffn(xp, xs, mod_p, mod_s, wg, wu, wd, lng, lnb, *, sub, alpha, tm, ts, ys_3d):
    bp, tp, d = xp.shape
    bs = xs.shape[0]
    ys_shape = (bs, ts, d) if ys_3d else (bs, ts * d)
    f = wg.shape[1]
    assert tp % tm == 0
    per_seq = tp // tm
    n_prompt = bp * per_seq
    assert f % FFN_CHUNK == 0
    last = n_prompt - 1
    yp, ys = pl.pallas_call(
        functools.partial(_ffn_kernel, sub=sub, alpha=alpha, n_prompt=n_prompt, t_len=ts, d=d),
        grid=(n_prompt + 1,),
        in_specs=[pl.BlockSpec((tm, d), lambda k: (jnp.minimum(k, last), 0)),
                  pl.BlockSpec((1, 3 * N_SUB, d), lambda k: (jnp.minimum(k, last) // per_seq, 0, 0)),
                  pl.BlockSpec((tm, d), lambda k: (jnp.maximum(k - 1, 0), 0)),
                  pl.BlockSpec((1, 3 * N_SUB, d), lambda k: (jnp.maximum(k - 1, 0) // per_seq, 0, 0)),
                  _const_spec(xs.shape), _sub_mod_spec(bs, d, sub),
                  pl.BlockSpec(memory_space=pl.ANY), pl.BlockSpec(memory_space=pl.ANY),
                  pl.BlockSpec(memory_space=pl.ANY),
                  _const_spec(lng.shape), _const_spec(lnb.shape)],
        out_specs=[pl.BlockSpec((tm, d), lambda k: (jnp.maximum(k - 1, 0), 0)),
                   pl.BlockSpec(ys_shape, lambda k: (0,) * len(ys_shape))],
        out_shape=[jax.ShapeDtypeStruct((bp * tp, d), F32),
                   jax.ShapeDtypeStruct(ys_shape, F32)],
        scratch_shapes=[pltpu.VMEM((d, f), BF16), pltpu.VMEM((d, f), BF16), pltpu.VMEM((f, d), BF16),
                        pltpu.VMEM((tm, d), F32),
                        pltpu.VMEM((CAST_BUFFERS, d, FFN_CHUNK), F32), pltpu.VMEM((CAST_BUFFERS, d, FFN_CHUNK), F32),
                        pltpu.VMEM((CAST_BUFFERS, FFN_CHUNK, d), F32),
                        pltpu.SemaphoreType.DMA((CAST_BUFFERS,)), pltpu.SemaphoreType.DMA((CAST_BUFFERS,)),
                        pltpu.SemaphoreType.DMA((CAST_BUFFERS,))],
        compiler_params=pltpu.CompilerParams(dimension_semantics=("arbitrary",),
                                             vmem_limit_bytes=V7X_VMEM_LIMIT),
        name=f"ffn{sub}",
    )(xp.reshape(bp * tp, d), mod_p, xp.reshape(bp * tp, d), mod_p, xs, mod_s,
      wg, wu, wd, lng, lnb)
    return yp.reshape(bp, tp, d), ys


def _tri_inverse_steps(lmats, c):
    ls = [l.astype(BF16) for l in lmats]
    ms = [jnp.dot(l, l, preferred_element_type=F32) for l in ls]
    yield
    qs = ms
    power = 2
    while 2 * power < c:
        mb = [m.astype(BF16) for m in ms]
        ms = [jnp.dot(m, m, preferred_element_type=F32) for m in mb]
        yield
        qm = [jnp.dot(q.astype(BF16), m.astype(BF16), preferred_element_type=F32) for q, m in zip(qs, ms)]
        yield
        qs = [q + m + x for q, m, x in zip(qs, ms, qm)]
        power *= 2
    rows = lax.broadcasted_iota(jnp.int32, (c, c), 0)
    cols = lax.broadcasted_iota(jnp.int32, (c, c), 1)
    eye = (rows == cols).astype(F32)
    lq = [jnp.dot(l, q.astype(BF16), preferred_element_type=F32) for l, q in zip(ls, qs)]
    yield
    return [eye - l + q - x for l, q, x in zip(lmats, qs, lq)]


def _gdn_local_steps(units, tril, strict):
    c = units[0][1].shape[0]
    a = [lax.dot_general(jnp.concatenate([kb, q], axis=0), k, (((1,), (1,)), ((), ())),
                         preferred_element_type=F32)
         for kb, q, k, _, _, _, _ in units]
    yield
    lmats, qks = [], []
    for a_u, (_, _, _, _, _, gcol, grow) in zip(a, units):
        decay = jnp.where(tril, jnp.exp(jnp.where(tril, gcol - grow, 0.0)), 0.0)
        lmats.append(jnp.where(strict, a_u[:c] * decay, 0.0))
        qks.append(jnp.where(tril, a_u[c:] * decay, 0.0))
    tms = yield from _tri_inverse_steps(lmats, c)
    uw = [jnp.dot(tm.astype(BF16), jnp.concatenate([vb, kbe], axis=1), preferred_element_type=F32)
          for tm, (_, _, _, vb, kbe, _, _) in zip(tms, units)]
    yield
    return [(x[:, :DN_DV], x[:, DN_DV:], qk) for x, qk in zip(uw, qks)]


def _alternate(primary, secondary):
    result = None
    done = False
    while not done:
        try:
            next(primary)
            yield
        except StopIteration as stop:
            result, done = stop.value, True
        if secondary is not None:
            try:
                next(secondary)
                yield
            except StopIteration:
                secondary = None
    if secondary is not None:
        yield from secondary
    return result


def _run_interleaved(*gens):
    live = list(gens)
    while live:
        for g in list(live):
            try:
                next(g)
            except StopIteration:
                live.remove(g)


def _gate_norm(o, og, norm_g):
    o = o * lax.rsqrt(jnp.mean(o * o, axis=-1, keepdims=True) + RMS_EPS) * norm_g
    return o * _silu(og)


def _proj_layout(dn_width, sc_width):
    off = {}
    pos = 0
    for name, width in (("og", dn_width), ("sc_b", sc_width), ("sc_c", sc_width), ("sc_h", sc_width)):
        off[name] = (pos, pos + width)
        pos += width
    return off


def _mixer_prompt_kernel(xf_ref, modf_ref, xb_ref, modb_ref, wq_ref, wr_ref, wab_ref, w_out_ref, cw_ref,
                         mw_ref, alog_ref, dtb_ref, ng_ref, lng_ref, lnb_ref,
                         y_ref, s_out_ref, cq_out_ref, cm_out_ref,
                         s_scr, cq_scr, cm_scr, gct_scr,
                         u_scr, w_scr, qk_scr, qd_scr, kd_scr, gl_scr, og_scr, osc_scr,
                         *, alpha, nseq, tt, tiles_per_group, n_tiles, off, n_conv, n_mconv):
    k = pl.program_id(0)
    rows_all = nseq * tt
    nchunk = rows_all // CHUNK
    per_seq = tt // CHUNK
    hk = DN_HEADS * DN_DK
    sub = 1
    head_lanes = [slice(h * DN_DK, (h + 1) * DN_DK) for h in range(DN_HEADS)]
    t_front = lax.rem(jnp.minimum(k, n_tiles - 1), tiles_per_group)
    t_back = lax.rem(jnp.maximum(k - 1, 0), tiles_per_group)
    keep_front = jnp.where(t_front == 0, 0.0, 1.0).astype(F32)
    keep_back = jnp.where(t_back == 0, 0.0, 1.0).astype(F32)

    def causal_conv(new_rows, hist_scr, s, w_ref, taps):
        ext = jnp.concatenate([hist_scr[s] * keep_front, new_rows], axis=0)
        acc = new_rows * w_ref[taps - 1:taps, :]
        for j in range(taps - 1):
            acc = acc + pltpu.roll(ext, taps - 1 - j, axis=0)[SUBLANES:] * w_ref[j:j + 1, :]
        hist_scr[s] = new_rows[tt - SUBLANES:]
        return acc

    def front():
        u = jnp.concatenate(
            [(xf_ref[s] * (1.0 + modf_ref[s, 3 * sub + 1:3 * sub + 2, :])
              + modf_ref[s, 3 * sub + 0:3 * sub + 1, :]).astype(BF16) for s in range(nseq)], axis=0)
        ab = jnp.dot(u, wab_ref[...], preferred_element_type=F32)
        pq = jnp.dot(u, wq_ref[...], preferred_element_type=F32)
        yield

        beta_all = jax.nn.sigmoid(ab)
        g = -jnp.exp(alog_ref[...]) * _softplus(ab + dtb_ref[...])
        rows = lax.broadcasted_iota(jnp.int32, (CHUNK, CHUNK), 0)
        cols = lax.broadcasted_iota(jnp.int32, (CHUNK, CHUNK), 1)
        tril = rows >= cols
        strict = rows > cols
        tril_f = tril.astype(F32)
        gc_parts, gl_parts = [], []
        for c in range(nchunk):
            gc_c = _mm_f32(tril_f, g[c * CHUNK:(c + 1) * CHUNK, :])
            gc_parts.append(gc_c)
            gl_parts.append(jnp.broadcast_to(gc_c[CHUNK - 1:CHUNK, :], (CHUNK, AB_PAD)))
        yield
        p = {}

        def project_rest(names):
            for name in names:
                lo, hi = off[name]
                p[name] = jnp.dot(u, wr_ref[:, lo:hi], preferred_element_type=F32)
                yield

        yield from project_rest(("sc_c", "sc_h"))
        gc = jnp.concatenate(gc_parts, axis=0)
        gl_rows = jnp.concatenate(gl_parts, axis=0)
        gc_t = gc.T
        for c in range(nchunk):
            gct_scr[c] = gc_t[0:SUBLANES, c * CHUNK:(c + 1) * CHUNK]

        acts = []
        for s in range(nseq):
            pq_s = pq[s * tt:(s + 1) * tt]
            acts.append(_silu(causal_conv(pq_s, cq_scr, s, cw_ref, n_conv)))
            cq_out_ref[s] = pq_s[tt - (n_conv - 1):]
        act = jnp.concatenate(acts, axis=0)

        eg_all = jnp.exp(gc)
        ekd_all = jnp.exp(gl_rows - gc)
        egl_all = jnp.exp(gl_rows)
        per_head = []
        for h, lanes in enumerate(head_lanes):
            q = _l2norm(act[:, h * DN_DK:(h + 1) * DN_DK]) * (DN_DK ** -0.5)
            kk = _l2norm(act[:, hk + h * DN_DK:hk + (h + 1) * DN_DK])
            v = act[:, 2 * hk + h * DN_DV:2 * hk + (h + 1) * DN_DV]
            beta = beta_all[:, DN_HEADS + h:DN_HEADS + h + 1]
            eg = eg_all[:, h:h + 1]
            kb = kk * beta
            per_head.append((kb.astype(BF16), q.astype(BF16), kk.astype(BF16), (v * beta).astype(BF16),
                             (kb * eg).astype(BF16)))
            qd_scr[:, lanes] = (q * eg).astype(BF16)
            kd_scr[:, lanes] = (kk * ekd_all[:, h:h + 1]).astype(BF16)
            for c in range(nchunk):
                gl_scr[c * SUBLANES:(c + 1) * SUBLANES, lanes] = jnp.broadcast_to(
                    egl_all[c * CHUNK:c * CHUNK + SUBLANES, h:h + 1], (SUBLANES, DN_DV))

        z = p["sc_c"] * p["sc_h"]
        zcs = []
        for s in range(nseq):
            z_s = z[s * tt:(s + 1) * tt]
            zcs.append(causal_conv(z_s, cm_scr, s, mw_ref, n_mconv))
            cm_out_ref[s] = z_s[tt - (n_mconv - 1):]

        where = [(c, h, lanes) for c in range(nchunk) for h, lanes in enumerate(head_lanes)]
        units = [tuple(arr[c * CHUNK:(c + 1) * CHUNK] for arr in per_head[h])
                 + (gc[c * CHUNK:(c + 1) * CHUNK, h:h + 1], gct_scr[c, h:h + 1, :])
                 for c, h, lanes in where]
        local = yield from _alternate(_gdn_local_steps(units, tril, strict), project_rest(("sc_b", "og")))
        osc_scr[...] = p["sc_b"] * jnp.concatenate(zcs, axis=0)
        og_scr[...] = p["og"]
        for (c, h, lanes), (u_c, w_c, qk_c) in zip(where, local):
            u_scr[c * CHUNK:(c + 1) * CHUNK, lanes] = u_c
            w_scr[c * CHUNK:(c + 1) * CHUNK, lanes] = w_c.astype(BF16)
            qk_scr[c * DN_HEADS + h] = qk_c.astype(BF16)

    def back():
        u_all, w_all, qd_all, kd_all = u_scr[...], w_scr[...], qd_scr[...], kd_scr[...]
        gl_all, og, o_sc = gl_scr[...], og_scr[...], osc_scr[...]
        qks = [qk_scr[i] for i in range(nchunk * DN_HEADS)]
        xs = [xb_ref[s] for s in range(nseq)]
        gates = [modb_ref[s, 3 * sub + 2:3 * sub + 3, :] for s in range(nseq)]
        states = [s_scr[i] * keep_back for i in range(nseq * DN_HEADS)]
        o_rows = [[None] * DN_HEADS for _ in range(nchunk)]
        for j in range(per_seq):
            chains = [(s, s * per_seq + j, h, lanes) for s in range(nseq) for h, lanes in enumerate(head_lanes)]
            rws = [slice(c * CHUNK, (c + 1) * CHUNK) for _, c, _, _ in chains]
            r = [jnp.dot(jnp.concatenate([w_all[rw, lanes], qd_all[rw, lanes]], axis=0),
                         states[s * DN_HEADS + h].astype(BF16), preferred_element_type=F32)
                 for (s, c, h, lanes), rw in zip(chains, rws)]
            yield
            v_new = [(u_all[rw, lanes] - r_u[:CHUNK]).astype(BF16)
                     for (s, c, h, lanes), rw, r_u in zip(chains, rws, r)]
            for (s, c, h, lanes), r_u, v_u in zip(chains, r, v_new):
                o_rows[c][h] = r_u[CHUNK:] + jnp.dot(qks[c * DN_HEADS + h], v_u, preferred_element_type=F32)
            for (s, c, h, lanes), rw, v_u in zip(chains, rws, v_new):
                i = s * DN_HEADS + h
                states[i] = (states[i] * gl_all[c * SUBLANES:c * SUBLANES + 1, lanes]
                             + lax.dot_general(kd_all[rw, lanes], v_u, (((0,), (0,)), ((), ())),
                                               preferred_element_type=F32))
            yield
        for i, st in enumerate(states):
            s_scr[i] = st
        for s in range(nseq):
            for h in range(DN_HEADS):
                s_out_ref[s, h] = states[s * DN_HEADS + h]

        o_dn = jnp.concatenate(
            [_gate_norm(jnp.concatenate([o_rows[c][h] for c in range(nchunk)], axis=0),
                        og[:, lanes], ng_ref[...]) for h, lanes in enumerate(head_lanes)], axis=1)
        dn_w = o_dn.shape[1]
        mix = (jnp.dot(o_dn.astype(BF16), w_out_ref[0:dn_w, :], preferred_element_type=F32)
               + jnp.dot(o_sc.astype(BF16), w_out_ref[dn_w:, :], preferred_element_type=F32))
        yield
        for s in range(nseq):
            y_ref[s] = _post(xs[s], mix[s * tt:(s + 1) * tt], gates[s],
                             lng_ref[sub:sub + 1, :], lnb_ref[sub:sub + 1, :], alpha)

    @pl.when(k == 0)
    def _():
        s_scr[...] = jnp.zeros_like(s_scr)
        cq_scr[...] = jnp.zeros_like(cq_scr)
        cm_scr[...] = jnp.zeros_like(cm_scr)
        _run_interleaved(front())

    @pl.when(jnp.logical_and(k > 0, k < n_tiles))
    def _():
        _run_interleaved(back(), front())

    @pl.when(k == n_tiles)
    def _():
        _run_interleaved(back())


def _mixer_prompt(x, mod, w_in, w_out, conv_w, mconv_w, alog, dtb, norm_g, lng, lnb, *, alpha, nseq, tt, off):
    bsz, t, d = x.shape
    assert t % tt == 0 and tt % LANES == 0 and bsz % nseq == 0
    nt = t // tt
    n_tiles = (bsz // nseq) * nt
    rows = nseq * tt
    qkv_dim = conv_w.shape[-1]
    sc_w = off["sc_c"][1] - off["sc_c"][0]
    dn_w = DN_HEADS * DN_DV
    n_conv, n_mconv = conv_w.shape[0], mconv_w.shape[0]
    assert max(n_conv, n_mconv) - 1 <= SUBLANES <= tt
    nchunk = rows // CHUNK
    kern = functools.partial(_mixer_prompt_kernel, alpha=alpha, nseq=nseq, tt=tt, tiles_per_group=nt,
                             n_tiles=n_tiles, off=off, n_conv=n_conv, n_mconv=n_mconv)

    def front_tile(k):
        return jnp.minimum(k, n_tiles - 1)

    def back_tile(k):
        return jnp.maximum(k - 1, 0)

    y, s_out, cq, cm = pl.pallas_call(
        kern,
        grid=(n_tiles + 1,),
        in_specs=[pl.BlockSpec((nseq, tt, d), lambda k: (front_tile(k) // nt, front_tile(k) % nt, 0)),
                  pl.BlockSpec((nseq, 3 * N_SUB, d), lambda k: (front_tile(k) // nt, 0, 0)),
                  pl.BlockSpec((nseq, tt, d), lambda k: (back_tile(k) // nt, back_tile(k) % nt, 0)),
                  pl.BlockSpec((nseq, 3 * N_SUB, d), lambda k: (back_tile(k) // nt, 0, 0)),
                  *[_const_spec(w.shape) for w in w_in], _const_spec(w_out.shape),
                  _const_spec(conv_w.shape), _const_spec(mconv_w.shape),
                  _const_spec(alog.shape), _const_spec(dtb.shape), _const_spec(norm_g.shape),
                  _const_spec(lng.shape), _const_spec(lnb.shape)],
        out_specs=[pl.BlockSpec((nseq, tt, d), lambda k: (back_tile(k) // nt, back_tile(k) % nt, 0)),
                   pl.BlockSpec((nseq, DN_HEADS, DN_DK, DN_DV), lambda k: (back_tile(k) // nt, 0, 0, 0)),
                   pl.BlockSpec((nseq, n_conv - 1, qkv_dim), lambda k: (front_tile(k) // nt, 0, 0)),
                   pl.BlockSpec((nseq, n_mconv - 1, sc_w), lambda k: (front_tile(k) // nt, 0, 0))],
        out_shape=[jax.ShapeDtypeStruct((bsz, t, d), F32),
                   jax.ShapeDtypeStruct((bsz, DN_HEADS, DN_DK, DN_DV), F32),
                   jax.ShapeDtypeStruct((bsz, n_conv - 1, qkv_dim), F32),
                   jax.ShapeDtypeStruct((bsz, n_mconv - 1, sc_w), F32)],
        scratch_shapes=[pltpu.VMEM((nseq * DN_HEADS, DN_DK, DN_DV), F32),
                        pltpu.VMEM((nseq, SUBLANES, qkv_dim), F32),
                        pltpu.VMEM((nseq, SUBLANES, sc_w), F32),
                        pltpu.VMEM((nchunk, SUBLANES, CHUNK), F32),
                        pltpu.VMEM((rows, dn_w), F32),
                        pltpu.VMEM((rows, dn_w), BF16),
                        pltpu.VMEM((nchunk * DN_HEADS, CHUNK, CHUNK), BF16),
                        pltpu.VMEM((rows, dn_w), BF16),
                        pltpu.VMEM((rows, dn_w), BF16),
                        pltpu.VMEM((nchunk * SUBLANES, dn_w), F32),
                        pltpu.VMEM((rows, dn_w), F32),
                        pltpu.VMEM((rows, sc_w), F32)],
        compiler_params=pltpu.CompilerParams(dimension_semantics=("arbitrary",),
                                             vmem_limit_bytes=V7X_VMEM_LIMIT),
        name="mixer_prompt",
    )(x, mod, x, mod, *w_in, w_out, conv_w, mconv_w, alog, dtb, norm_g, lng, lnb)
    return y, s_out, cq, cm


_PACK = SUBLANES


def _sample_phase_a(x_ref, mod_ref, wq_ref, wr_ref, wab_ref, cw_ref, mw_ref, alog_ref, dtb_ref, cq_in_ref, cm_in_ref,
                    cq_out_ref, cm_out_ref, og_scr, osc_scr, lhs_scr, u_scr, kd_scr, gl_scr, qk_scr,
                    *, t_len, d, off, n_conv, n_mconv):
    bsz = x_ref.shape[0]
    hk = DN_HEADS * DN_DK
    sc_w = off["sc_c"][1] - off["sc_c"][0]

    x = _stack_time(x_ref, t_len, d)
    u = (x * (1.0 + _mod_rows(mod_ref, 1, t_len, d)) + _mod_rows(mod_ref, 0, t_len, d)).astype(BF16)
    pq = jnp.dot(u, wq_ref[...], preferred_element_type=F32)
    ab = jnp.dot(u, wab_ref[...], preferred_element_type=F32)
    rest = jnp.dot(u, wr_ref[...], preferred_element_type=F32)
    p = {name: rest[:, lo:hi] for name, (lo, hi) in off.items()}

    def rows(arr, t):
        return arr[t * bsz:(t + 1) * bsz]

    seq = [cq_in_ref[j] for j in range(n_conv - 1)] + [rows(pq, t) for t in range(t_len)]
    act = []
    for t in range(t_len):
        acc = None
        for j in range(n_conv):
            term = seq[t + j] * cw_ref[j:j + 1, :]
            acc = term if acc is None else acc + term
        act.append(_silu(acc))
    for j in range(n_conv - 1):
        cq_out_ref[j] = seq[t_len + j]

    z = p["sc_c"] * p["sc_h"]
    zseq = ([cm_in_ref[:, j * sc_w:(j + 1) * sc_w] for j in range(n_mconv - 1)]
            + [rows(z, t) for t in range(t_len)])
    zc = []
    for t in range(t_len):
        acc = None
        for j in range(n_mconv):
            term = zseq[t + j] * mw_ref[j:j + 1, :]
            acc = term if acc is None else acc + term
        zc.append(acc)
    for j in range(n_mconv - 1):
        cm_out_ref[:, j * sc_w:(j + 1) * sc_w] = zseq[t_len + j]
    osc_scr[...] = p["sc_b"] * jnp.concatenate(zc, axis=0)
    og_scr[...] = p["og"]

    beta_all = jax.nn.sigmoid(ab)
    g_all = -jnp.exp(alog_ref[...]) * _softplus(ab + dtb_ref[...])
    zero = jnp.zeros((bsz, LANES), F32)

    for h in range(DN_HEADS):
        q = [_l2norm(a[:, h * DN_DK:(h + 1) * DN_DK]) * (DN_DK ** -0.5) for a in act]
        k = [_l2norm(a[:, hk + h * DN_DK:hk + (h + 1) * DN_DK]) for a in act]
        v = [a[:, 2 * hk + h * DN_DV:2 * hk + (h + 1) * DN_DV] for a in act]
        beta = [rows(beta_all, t)[:, DN_HEADS + h:DN_HEADS + h + 1] for t in range(t_len)]
        g = [rows(g_all, t)[:, h:h + 1] for t in range(t_len)]
        gc = [g[0]]
        for t in range(1, t_len):
            gc.append(gc[-1] + g[t])
        kb = [k[t] * beta[t] for t in range(t_len)]
        vb = [v[t] * beta[t] for t in range(t_len)]
        eg = [jnp.exp(gc[t]) for t in range(t_len)]
        kbe = [kb[t] * eg[t] for t in range(t_len)]
        lm = [[None] * t_len for _ in range(t_len)]
        n_qk = 0
        for i in range(t_len):
            for j in range(i + 1):
                dec = jnp.exp(gc[i] - gc[j])
                qk_scr[h, n_qk] = jnp.broadcast_to(_rowsum(q[i] * k[j]) * dec, (bsz, LANES))
                n_qk += 1
                if j < i:
                    lm[i][j] = _rowsum(kb[i] * k[j]) * dec
        tm = [[None] * t_len for _ in range(t_len)]
        for i in range(t_len):
            for j in range(i):
                acc = lm[i][j]
                for m in range(j + 1, i):
                    acc = acc + lm[i][m] * tm[m][j]
                tm[i][j] = -acc
        for i in range(t_len):
            ui, wi = vb[i], kbe[i]
            for j in range(i):
                ui = ui + tm[i][j] * vb[j]
                wi = wi + tm[i][j] * kbe[j]
            lhs_scr[h, pl.ds(i, bsz, stride=_PACK), :] = wi
            lhs_scr[h, pl.ds(t_len + i, bsz, stride=_PACK), :] = q[i] * eg[i]
            u_scr[h, pl.ds(i, bsz, stride=_PACK), :] = ui
            u_scr[h, pl.ds(t_len + i, bsz, stride=_PACK), :] = zero
            kd_scr[h, pl.ds(i, bsz, stride=_PACK), :] = k[i] * jnp.exp(gc[t_len - 1] - gc[i])
            kd_scr[h, pl.ds(t_len + i, bsz, stride=_PACK), :] = zero
        gl_scr[h] = jnp.broadcast_to(jnp.exp(gc[t_len - 1]), (bsz, LANES))


def _sample_phase_c(x_ref, mod_ref, w_out_ref, ng_ref, lng_ref, lnb_ref, y_ref,
                    og_scr, osc_scr, res_scr, vn_scr, qk_scr, *, alpha, t_len, d):
    bsz = x_ref.shape[0]
    sub = 1
    x = _stack_time(x_ref, t_len, d)
    gate = _mod_rows(mod_ref, 2, t_len, d)
    og = og_scr[...]
    per_t = []
    for i in range(t_len):
        heads = []
        for h in range(DN_HEADS):
            oi = res_scr[h, pl.ds(t_len + i, bsz, stride=_PACK), :]
            base = i * (i + 1) // 2
            for j in range(i + 1):
                oi = oi + qk_scr[h, base + j] * vn_scr[h, pl.ds(j, bsz, stride=_PACK), :]
            heads.append(_gate_norm(oi, og[i * bsz:(i + 1) * bsz, h * DN_DV:(h + 1) * DN_DV], ng_ref[...]))
        per_t.append(jnp.concatenate(heads, axis=1))
    o_dn = jnp.concatenate(per_t, axis=0)
    dn_w = o_dn.shape[1]
    mix = (jnp.dot(o_dn.astype(BF16), w_out_ref[0:dn_w, :], preferred_element_type=F32)
           + jnp.dot(osc_scr[...].astype(BF16), w_out_ref[dn_w:, :], preferred_element_type=F32))
    y = _post(x, mix, gate, lng_ref[sub:sub + 1, :], lnb_ref[sub:sub + 1, :], alpha)
    for t in range(t_len):
        y_ref[:, t * d:(t + 1) * d] = y[t * bsz:(t + 1) * bsz]


def _mixer_sample_kernel(x_ref, mod_ref, wq_ref, wr_ref, wab_ref, w_out_ref, cw_ref, mw_ref, alog_ref, dtb_ref, ng_ref,
                         lng_ref, lnb_ref, cq_in_ref, cm_in_ref, s_in_ref,
                         y_ref, cq_out_ref, cm_out_ref, s_out_ref,
                         og_scr, osc_scr, lhs_scr, u_scr, kd_scr, gl_scr, qk_scr, res_scr, vn_scr,
                         *, alpha, t_len, d, off, n_conv, n_mconv, bt):
    step = pl.program_id(0)
    assert 2 * t_len == _PACK, "packed buffers hold [T rows | T rows] per sequence"

    @pl.when(step == 0)
    def _():
        _sample_phase_a(x_ref, mod_ref, wq_ref, wr_ref, wab_ref, cw_ref, mw_ref, alog_ref, dtb_ref, cq_in_ref, cm_in_ref,
                        cq_out_ref, cm_out_ref, og_scr, osc_scr, lhs_scr, u_scr, kd_scr, gl_scr, qk_scr,
                        t_len=t_len, d=d, off=off, n_conv=n_conv, n_mconv=n_mconv)

    top = (lax.broadcasted_iota(jnp.int32, (_PACK, LANES), 0) < t_len).astype(F32)

    chains = [(bl, h) for bl in range(bt) for h in range(DN_HEADS)]
    seq0 = step * bt
    row0 = [pl.multiple_of((seq0 + bl) * _PACK, _PACK) for bl in range(bt)]
    states = [s_in_ref[bl, h] for bl, h in chains]
    r = [_mm(lhs_scr[h, pl.ds(row0[bl], _PACK), :], s) for (bl, h), s in zip(chains, states)]
    vns = []
    for (bl, h), r_u in zip(chains, r):
        res_scr[h, pl.ds(row0[bl], _PACK), :] = r_u
        vn = (u_scr[h, pl.ds(row0[bl], _PACK), :] - r_u) * top
        vn_scr[h, pl.ds(row0[bl], _PACK), :] = vn
        vns.append(vn)
    upd = [_mm_at(kd_scr[h, pl.ds(row0[bl], _PACK), :], vn) for (bl, h), vn in zip(chains, vns)]
    for (bl, h), s, s_add in zip(chains, states, upd):
        s_out_ref[bl, h] = s * gl_scr[h, pl.ds(seq0 + bl, 1), :] + s_add

    @pl.when(step == pl.num_programs(0) - 1)
    def _():
        _sample_phase_c(x_ref, mod_ref, w_out_ref, ng_ref, lng_ref, lnb_ref, y_ref,
                        og_scr, osc_scr, res_scr, vn_scr, qk_scr, alpha=alpha, t_len=t_len, d=d)


def _mixer_sample(x, mod, w_in, w_out, conv_w, mconv_w, alog, dtb, norm_g, lng, lnb, s_in, cq_in, cm_in,
                  *, alpha, off, bt, t):
    bsz, d = x.shape[0], x.shape[1] // t
    assert bsz % bt == 0
    sc_w = off["sc_c"][1] - off["sc_c"][0]
    dn_w = DN_HEADS * DN_DV
    n_conv, n_mconv = conv_w.shape[0], mconv_w.shape[0]
    n_qk = t * (t + 1) // 2
    x2 = x
    cq2 = jnp.swapaxes(cq_in, 0, 1)
    cm2 = cm_in.reshape(bsz, (n_mconv - 1) * sc_w)
    kern = functools.partial(_mixer_sample_kernel, alpha=alpha, t_len=t, d=d, off=off,
                             n_conv=n_conv, n_mconv=n_mconv, bt=bt)
    consts = (*w_in, w_out, conv_w, mconv_w, alog, dtb, norm_g, lng, lnb, cq2, cm2)
    state_spec = pl.BlockSpec((bt, DN_HEADS, DN_DK, DN_DV), lambda i: (i, 0, 0, 0))
    packed = pltpu.VMEM((DN_HEADS, _PACK * bsz, LANES), F32)
    y, cq, cm, s_out = pl.pallas_call(
        kern,
        grid=(bsz // bt,),
        in_specs=([_const_spec(x2.shape), _sub_mod_spec(bsz, d, 1)]
                  + [_const_spec(a.shape) for a in consts] + [state_spec]),
        out_specs=[pl.BlockSpec(x2.shape, lambda i: (0, 0)),
                   pl.BlockSpec(cq2.shape, lambda i: (0, 0, 0)),
                   pl.BlockSpec(cm2.shape, lambda i: (0, 0)),
                   state_spec],
        out_shape=[jax.ShapeDtypeStruct(x2.shape, F32),
                   jax.ShapeDtypeStruct(cq2.shape, F32),
                   jax.ShapeDtypeStruct(cm2.shape, F32),
                   jax.ShapeDtypeStruct(s_in.shape, F32)],
        scratch_shapes=[pltpu.VMEM((t * bsz, dn_w), F32),
                        pltpu.VMEM((t * bsz, sc_w), F32),
                        packed,
                        packed,
                        packed,
                        pltpu.VMEM((DN_HEADS, bsz, LANES), F32),
                        pltpu.VMEM((DN_HEADS, n_qk, bsz, LANES), F32),
                        packed,
                        packed],
        compiler_params=pltpu.CompilerParams(dimension_semantics=("arbitrary",),
                                             vmem_limit_bytes=V7X_VMEM_LIMIT),
        name="mixer_sample",
    )(x2, mod, *consts, s_in)
    return y, s_out, jnp.swapaxes(cq, 0, 1), cm.reshape(bsz, n_mconv - 1, sc_w)


def _pad_lanes(v):
    return jnp.zeros((1, AB_PAD), F32).at[0, :v.shape[0]].set(v)


def kernel(x_prompt, x_sample, state_ssm, state_conv_qkv, state_conv_mix, c_prompt, c_sample, w_ada, b_ada, ln_g, ln_b, ffn1_wg, ffn1_wu, ffn1_wd, ffn2_wg, ffn2_wu, ffn2_wd, w_in, conv_qkv_w, a_log, dt_bias, dn_norm_g, conv_mix_w, w_out):
    depth = w_ada.shape[0]
    alpha = (2 * depth) ** 0.25
    bp, tp, d = x_prompt.shape
    qkv_dim = conv_qkv_w.shape[-1]
    sc_width = conv_mix_w.shape[-1]
    dn_width = DN_HEADS * DN_DV
    off = _proj_layout(dn_width, sc_width)
    tm = min(512, tp)
    nseq = 2 if bp % 2 == 0 else 1
    tt = min(512 // nseq, tp)
    bt = min(16, x_sample.shape[0])
    ts = x_sample.shape[1]

    hp, hs = x_prompt, x_sample
    outs = [[] for _ in range(6)]
    for l in range(depth):
        mod_p, mod_s, *w_in_p = _adaln(c_prompt, c_sample, w_ada[l], b_ada[l].reshape(1, -1),
                                       jnp.swapaxes(w_in[l], 0, 1), qkv_dim)
        mod_p = mod_p.reshape(bp, 3 * N_SUB, d)
        lng, lnb = ln_g[l], ln_b[l]
        w1 = (ffn1_wg[l], ffn1_wu[l], ffn1_wd[l])
        w2 = (ffn2_wg[l], ffn2_wu[l], ffn2_wd[l])
        w_out_b = w_out[l].astype(BF16)
        alog, dtb = _pad_lanes(a_log[l]), _pad_lanes(dt_bias[l])
        norm_g = dn_norm_g[l].reshape(1, -1)
        mix_args = (w_in_p, w_out_b, conv_qkv_w[l], conv_mix_w[l], alog, dtb, norm_g, lng, lnb)

        hp, hs = _ffn(hp, hs, mod_p, mod_s, *w1, lng, lnb, sub=0, alpha=alpha, tm=tm, ts=ts,
                      ys_3d=False)
        hp, a1, a2, a3 = _mixer_prompt(hp, mod_p, *mix_args, alpha=alpha, nseq=nseq, tt=tt, off=off)
        hs, b1, b2, b3 = _mixer_sample(hs, mod_s, *mix_args, state_ssm[l], state_conv_qkv[l],
                                       state_conv_mix[l], alpha=alpha, off=off, bt=bt, t=ts)
        hp, hs = _ffn(hp, hs, mod_p, mod_s, *w2, lng, lnb, sub=2, alpha=alpha, tm=tm, ts=ts,
                      ys_3d=True)
        for lst, val in zip(outs, (a1, a2, a3, b1, b2, b3)):
            lst.append(val)
    return (hp, hs) + tuple(jnp.stack(o) for o in outs)
```

```python
import functools

import jax
import jax.numpy as jnp
from jax import lax
from jax.experimental import pallas as pl
from jax.experimental.pallas import tpu as pltpu

F32 = jnp.float32
BF16 = jnp.bfloat16

LN_EPS = 1e-5
RMS_EPS = 1e-6
N_SUB = 3
DN_HEADS = 4
DN_DK = 128
DN_DV = 128
CHUNK = 64
LANES = 128
SUBLANES = 8
AB_PAD = LANES

V7X_VMEM_LIMIT = 60000 * 1024
CAST_BUFFERS = 3
FFN_CHUNK = 256


def _mm(a, b):
    return jnp.dot(a.astype(BF16), b.astype(BF16), preferred_element_type=F32)


def _mm_at(a, b):
    return lax.dot_general(a.astype(BF16), b.astype(BF16), (((0,), (0,)), ((), ())),
                           preferred_element_type=F32)


def _mm_f32(a, b):
    return jnp.dot(a, b, preferred_element_type=F32, precision=lax.Precision.HIGHEST)


def _silu(x):
    return x * jax.nn.sigmoid(x)


def _softplus(x):
    return jnp.maximum(x, 0.0) + jnp.log1p(jnp.exp(-jnp.abs(x)))


def _layer_norm(y, g, b):
    mu = jnp.mean(y, axis=-1, keepdims=True)
    yc = y - mu
    var = jnp.mean(yc * yc, axis=-1, keepdims=True)
    return yc * lax.rsqrt(var + LN_EPS) * g + b


def _post(x, delta, gate, g, b, alpha):
    return _layer_norm(alpha * x + gate * delta, g, b)


def _rowsum(x):
    return jnp.sum(x, axis=-1, keepdims=True)


def _l2norm(x):
    return x * lax.rsqrt(_rowsum(x * x) + RMS_EPS)


def _const_spec(shape):
    nd = len(shape)
    return pl.BlockSpec(shape, lambda *_: (0,) * nd, pipeline_mode=pl.Buffered(1))


def _adaln_kernel(cp_ref, cs_ref, w_ref, b_ref, win_ref, op_ref, os_ref, wq_ref, wr_ref, wab_ref, *, qkv_dim):
    mp = cp_ref.shape[0]
    c = jnp.concatenate([cp_ref[...], cs_ref[...]], axis=0)
    mod = _mm(_silu(c), w_ref[...]) + b_ref[...]
    op_ref[...] = mod[:mp]
    os_ref[...] = mod[mp:]

    gates = 2 * DN_HEADS
    wq_ref[...] = win_ref[0:qkv_dim, :].T.astype(BF16)
    wr_ref[...] = win_ref[qkv_dim + gates:, :].T.astype(BF16)
    ab_rows = jnp.concatenate([win_ref[qkv_dim:qkv_dim + gates, :],
                               jnp.zeros((AB_PAD - gates, win_ref.shape[1]), F32)], axis=0)
    wab_ref[...] = ab_rows.T.astype(BF16)


def _adaln(c_p, c_s, w, b, w_in_t, qkv_dim):
    (mp, d), ms = c_p.shape, c_s.shape[0]
    n = w.shape[1]
    proj = w_in_t.shape[0]
    rest = proj - qkv_dim - 2 * DN_HEADS
    assert qkv_dim % LANES == 0 and rest % LANES == 0 and (2 * DN_HEADS) % SUBLANES == 0
    steps = max(g for g in (1, 2, 4) if n % (g * LANES) == 0 and d % (g * LANES) == 0)
    tn, rows = n // steps, d // steps
    return pl.pallas_call(
        functools.partial(_adaln_kernel, qkv_dim=qkv_dim),
        grid=(steps,),
        in_specs=[_const_spec((mp, d)), _const_spec((ms, d)),
                  pl.BlockSpec((d, tn), lambda j: (0, j)),
                  pl.BlockSpec((1, tn), lambda j: (0, j)),
                  pl.BlockSpec((proj, rows), lambda j: (0, j))],
        out_specs=[pl.BlockSpec((mp, tn), lambda j: (0, j)), pl.BlockSpec((ms, tn), lambda j: (0, j)),
                   pl.BlockSpec((rows, qkv_dim), lambda j: (j, 0)),
                   pl.BlockSpec((rows, rest), lambda j: (j, 0)),
                   pl.BlockSpec((rows, AB_PAD), lambda j: (j, 0))],
        out_shape=[jax.ShapeDtypeStruct((mp, n), F32), jax.ShapeDtypeStruct((ms, n), F32),
                   jax.ShapeDtypeStruct((d, qkv_dim), BF16), jax.ShapeDtypeStruct((d, rest), BF16),
                   jax.ShapeDtypeStruct((d, AB_PAD), BF16)],
        compiler_params=pltpu.CompilerParams(dimension_semantics=("arbitrary",),
                                             vmem_limit_bytes=V7X_VMEM_LIMIT),
        name="adaln",
    )(c_p, c_s, w, b, w_in_t)


def _swiglu(x, shift, scale, wg_ref, wu_ref, wd_ref):
    u = (x * (1.0 + scale) + shift).astype(BF16)
    hg = jnp.dot(u, wg_ref[...], preferred_element_type=F32)
    hu = jnp.dot(u, wu_ref[...], preferred_element_type=F32)
    h = (_silu(hg) * hu).astype(BF16)
    return jnp.dot(h, wd_ref[...], preferred_element_type=F32)


def _stack_time(ref, t_len, d):
    if len(ref.shape) == 3:
        return jnp.concatenate([ref[:, t, :] for t in range(t_len)], axis=0)
    return jnp.concatenate([ref[:, t * d:(t + 1) * d] for t in range(t_len)], axis=0)


def _mod_rows(mod_ref, idx, t_len, d):
    m = mod_ref[:, idx * d:(idx + 1) * d]
    return jnp.concatenate([m] * t_len, axis=0)


def _sub_mod_spec(bsz, d, sub):
    return pl.BlockSpec((bsz, 3 * d), lambda *_: (0, sub), pipeline_mode=pl.Buffered(1))


def _first_tile_swiglu(x, shift, scale, w_hbm, w_scr, stages, sems):
    wg_hbm, wu_hbm, wd_hbm = w_hbm
    wg_scr, wu_scr, wd_scr = w_scr
    nbuf = stages[0].shape[0]
    cw = stages[0].shape[2]
    n = wg_hbm.shape[1] // cw

    def copies(j):
        slot = j % nbuf
        cols = pl.ds(j * cw, cw)
        return (pltpu.make_async_copy(wg_hbm.at[:, cols], stages[0].at[slot], sems[0].at[slot]),
                pltpu.make_async_copy(wu_hbm.at[:, cols], stages[1].at[slot], sems[1].at[slot]),
                pltpu.make_async_copy(wd_hbm.at[cols, :], stages[2].at[slot], sems[2].at[slot]))

    queue = [copies(j) for j in range(n)]
    ahead = nbuf - 1
    for group in queue[:ahead]:
        for copy in group:
            copy.start()
    u = (x * (1.0 + scale) + shift).astype(BF16)
    acc = None
    for j, group in enumerate(queue):
        if j + ahead < n:
            for copy in queue[j + ahead]:
                copy.start()
        for copy in group:
            copy.wait()
        slot, cols = j % nbuf, slice(j * cw, (j + 1) * cw)
        wg_scr[:, cols] = stages[0][slot].astype(BF16)
        wu_scr[:, cols] = stages[1][slot].astype(BF16)
        wd_scr[cols, :] = stages[2][slot].astype(BF16)
        hg = jnp.dot(u, wg_scr[:, cols], preferred_element_type=F32)
        hu = jnp.dot(u, wu_scr[:, cols], preferred_element_type=F32)
        part = jnp.dot((_silu(hg) * hu).astype(BF16), wd_scr[cols, :], preferred_element_type=F32)
        acc = part if acc is None else acc + part
    return acc


def _ffn_kernel(xp_ref, modp_ref, xq_ref, modq_ref, xs_ref, mods_ref, wg_hbm, wu_hbm, wd_hbm, lng_ref, lnb_ref,
                yp_ref, ys_ref, wg_scr, wu_scr, wd_scr, d_scr, stage_g, stage_u, stage_d, sem_g, sem_u, sem_d,
                *, sub, alpha, n_prompt, t_len, d):
    k = pl.program_id(0)
    g, b = lng_ref[sub:sub + 1, :], lnb_ref[sub:sub + 1, :]

    def matmuls():
        return _swiglu(xp_ref[...], modp_ref[0, 3 * sub + 0:3 * sub + 1, :], modp_ref[0, 3 * sub + 1:3 * sub + 2, :],
                       wg_scr, wu_scr, wd_scr)

    def finish(d_prev):
        yp_ref[...] = _post(xq_ref[...], 0.5 * d_prev, modq_ref[0, 3 * sub + 2:3 * sub + 3, :], g, b, alpha)

    @pl.when(k == 0)
    def _():
        d_scr[...] = _first_tile_swiglu(
            xp_ref[...], modp_ref[0, 3 * sub + 0:3 * sub + 1, :], modp_ref[0, 3 * sub + 1:3 * sub + 2, :],
            (wg_hbm, wu_hbm, wd_hbm), (wg_scr, wu_scr, wd_scr), (stage_g, stage_u, stage_d),
            (sem_g, sem_u, sem_d))

    @pl.when(jnp.logical_and(k > 0, k < n_prompt))
    def _():
        d_prev = d_scr[...]
        d_new = matmuls()
        finish(d_prev)
        d_scr[...] = d_new

    @pl.when(k == n_prompt)
    def _():
        finish(d_scr[...])
        bsz = xs_ref.shape[0]
        x = _stack_time(xs_ref, t_len, d)
        delta = _swiglu(x, _mod_rows(mods_ref, 0, t_len, d), _mod_rows(mods_ref, 1, t_len, d),
                        wg_scr, wu_scr, wd_scr)
        y = _post(x, 0.5 * delta, _mod_rows(mods_ref, 2, t_len, d), g, b, alpha)
        for t in range(t_len):
            if len(ys_ref.shape) == 3:
                ys_ref[:, t, :] = y[t * bsz:(t + 1) * bsz]
            else:
                ys_ref[:, t * d:(t + 1) * d] = y[t * bsz:(t + 1) * bsz]


def _ffn(xp, xs, mod_p, mod_s, wg, wu, wd, lng, lnb, *, sub, alpha, tm, ts, ys_3d):
    bp, tp, d = xp.shape
    bs = xs.shape[0]
    ys_shape = (bs, ts, d) if ys_3d else (bs, ts * d)
    f = wg.shape[1]
    assert tp % tm == 0
    per_seq = tp // tm
    n_prompt = bp * per_seq
    assert f % FFN_CHUNK == 0
    last = n_prompt - 1
    yp, ys = pl.pallas_call(
        functools.partial(_ffn_kernel, sub=sub, alpha=alpha, n_prompt=n_prompt, t_len=ts, d=d),
        grid=(n_prompt + 1,),
        in_specs=[pl.BlockSpec((tm, d), lambda k: (jnp.minimum(k, last), 0)),
                  pl.BlockSpec((1, 3 * N_SUB, d), lambda k: (jnp.minimum(k, last) // per_seq, 0, 0)),
                  pl.BlockSpec((tm, d), lambda k: (jnp.maximum(k - 1, 0), 0)),
                  pl.BlockSpec((1, 3 * N_SUB, d), lambda k: (jnp.maximum(k - 1, 0) // per_seq, 0, 0)),
                  _const_spec(xs.shape), _sub_mod_spec(bs, d, sub),
                  pl.BlockSpec(memory_space=pl.ANY), pl.BlockSpec(memory_space=pl.ANY),
                  pl.BlockSpec(memory_space=pl.ANY),
                  _const_spec(lng.shape), _const_spec(lnb.shape)],
        out_specs=[pl.BlockSpec((tm, d), lambda k: (jnp.maximum(k - 1, 0), 0)),
                   pl.BlockSpec(ys_shape, lambda k: (0,) * len(ys_shape))],
        out_shape=[jax.ShapeDtypeStruct((bp * tp, d), F32),
                   jax.ShapeDtypeStruct(ys_shape, F32)],
        scratch_shapes=[pltpu.VMEM((d, f), BF16), pltpu.VMEM((d, f), BF16), pltpu.VMEM((f, d), BF16),
                        pltpu.VMEM((tm, d), F32),
                        pltpu.VMEM((CAST_BUFFERS, d, FFN_CHUNK), F32), pltpu.VMEM((CAST_BUFFERS, d, FFN_CHUNK), F32),
                        pltpu.VMEM((CAST_BUFFERS, FFN_CHUNK, d), F32),
                        pltpu.SemaphoreType.DMA((CAST_BUFFERS,)), pltpu.SemaphoreType.DMA((CAST_BUFFERS,)),
                        pltpu.SemaphoreType.DMA((CAST_BUFFERS,))],
        compiler_params=pltpu.CompilerParams(dimension_semantics=("arbitrary",),
                                             vmem_limit_bytes=V7X_VMEM_LIMIT),
        name=f"ffn{sub}",
    )(xp.reshape(bp * tp, d), mod_p, xp.reshape(bp * tp, d), mod_p, xs, mod_s,
      wg, wu, wd, lng, lnb)
    return yp.reshape(bp, tp, d), ys


def _tri_inverse_steps(lmats, c):
    ls = [l.astype(BF16) for l in lmats]
    ms = [jnp.dot(l, l, preferred_element_type=F32) for l in ls]
    yield
    rows = lax.broadcasted_iota(jnp.int32, (c, c), 0)
    cols = lax.broadcasted_iota(jnp.int32, (c, c), 1)
    eye = (rows == cols).astype(F32)
    ps = [eye - l for l in lmats]
    power = 2
    while 2 * power < c:
        mp = [jnp.concatenate([m, p], axis=0).astype(BF16) for m, p in zip(ms, ps)]
        sq = [jnp.dot(x, x[:c], preferred_element_type=F32) for x in mp]
        yield
        ms = [x[:c] for x in sq]
        ps = [p + x[c:] for p, x in zip(ps, sq)]
        power *= 2
    pm = [jnp.dot(p.astype(BF16), m.astype(BF16), preferred_element_type=F32) for p, m in zip(ps, ms)]
    yield
    return [p + x for p, x in zip(ps, pm)]


def _gdn_local_steps(units, tril, strict):
    c = units[0][1].shape[0]
    a = [lax.dot_general(jnp.concatenate([kb, q], axis=0), k, (((1,), (1,)), ((), ())),
                         preferred_element_type=F32)
         for kb, q, k, _, _, _, _ in units]
    yield
    lmats, qks = [], []
    for a_u, (_, _, _, _, _, gcol, grow) in zip(a, units):
        decay = jnp.where(tril, jnp.exp(jnp.where(tril, gcol - grow, 0.0)), 0.0)
        lmats.append(jnp.where(strict, a_u[:c] * decay, 0.0))
        qks.append(jnp.where(tril, a_u[c:] * decay, 0.0))
    tms = yield from _tri_inverse_steps(lmats, c)
    uw = [jnp.dot(tm.astype(BF16), jnp.concatenate([vb, kbe], axis=1), preferred_element_type=F32)
          for tm, (_, _, _, vb, kbe, _, _) in zip(tms, units)]
    yield
    return [(x[:, :DN_DV], x[:, DN_DV:], qk) for x, qk in zip(uw, qks)]


def _alternate(primary, secondary):
    result = None
    done = False
    while not done:
        try:
            next(primary)
            yield
        except StopIteration as stop:
            result, done = stop.value, True
        if secondary is not None:
            try:
                next(secondary)
                yield
            except StopIteration:
                secondary = None
    if secondary is not None:
        yield from secondary
    return result


def _run_interleaved(*gens):
    live = list(gens)
    while live:
        for g in list(live):
            try:
                next(g)
            except StopIteration:
                live.remove(g)


def _gate_norm(o, og, norm_g):
    o = o * lax.rsqrt(jnp.mean(o * o, axis=-1, keepdims=True) + RMS_EPS) * norm_g
    return o * _silu(og)


def _proj_layout(dn_width, sc_width):
    off = {}
    pos = 0
    for name, width in (("og", dn_width), ("sc_b", sc_width), ("sc_c", sc_width), ("sc_h", sc_width)):
        off[name] = (pos, pos + width)
        pos += width
    return off


def _mixer_prompt_kernel(xf_ref, modf_ref, xb_ref, modb_ref, wq_ref, wr_ref, wab_ref, w_out_ref, cw_ref,
                         mw_ref, alog_ref, dtb_ref, ng_ref, lng_ref, lnb_ref,
                         y_ref, s_out_ref, cq_out_ref, cm_out_ref,
                         s_scr, cq_scr, cm_scr, gct_scr,
                         u_scr, w_scr, qk_scr, qd_scr, kd_scr, gl_scr, og_scr, osc_scr,
                         *, alpha, nseq, tt, tiles_per_group, n_tiles, off, n_conv, n_mconv):
    k = pl.program_id(0)
    rows_all = nseq * tt
    nchunk = rows_all // CHUNK
    per_seq = tt // CHUNK
    hk = DN_HEADS * DN_DK
    sub = 1
    head_lanes = [slice(h * DN_DK, (h + 1) * DN_DK) for h in range(DN_HEADS)]
    t_front = lax.rem(jnp.minimum(k, n_tiles - 1), tiles_per_group)
    t_back = lax.rem(jnp.maximum(k - 1, 0), tiles_per_group)
    keep_front = jnp.where(t_front == 0, 0.0, 1.0).astype(F32)
    keep_back = jnp.where(t_back == 0, 0.0, 1.0).astype(F32)

    def causal_conv(new_rows, hist_scr, s, w_ref, taps):
        ext = jnp.concatenate([hist_scr[s] * keep_front, new_rows], axis=0)
        acc = new_rows * w_ref[taps - 1:taps, :]
        for j in range(taps - 1):
            acc = acc + pltpu.roll(ext, taps - 1 - j, axis=0)[SUBLANES:] * w_ref[j:j + 1, :]
        hist_scr[s] = new_rows[tt - SUBLANES:]
        return acc

    def front():
        u = jnp.concatenate(
            [(xf_ref[s] * (1.0 + modf_ref[s, 3 * sub + 1:3 * sub + 2, :])
              + modf_ref[s, 3 * sub + 0:3 * sub + 1, :]).astype(BF16) for s in range(nseq)], axis=0)
        ab = jnp.dot(u, wab_ref[...], preferred_element_type=F32)
        pq = jnp.dot(u, wq_ref[...], preferred_element_type=F32)
        yield

        beta_all = jax.nn.sigmoid(ab)
        g = -jnp.exp(alog_ref[...]) * _softplus(ab + dtb_ref[...])
        rows = lax.broadcasted_iota(jnp.int32, (CHUNK, CHUNK), 0)
        cols = lax.broadcasted_iota(jnp.int32, (CHUNK, CHUNK), 1)
        tril = rows >= cols
        strict = rows > cols
        tril_f = tril.astype(F32)
        gc_parts, gl_parts = [], []
        for c in range(nchunk):
            gc_c = _mm_f32(tril_f, g[c * CHUNK:(c + 1) * CHUNK, :])
            gc_parts.append(gc_c)
            gl_parts.append(jnp.broadcast_to(gc_c[CHUNK - 1:CHUNK, :], (CHUNK, AB_PAD)))
        yield
        p = {}

        def project_rest(names):
            for name in names:
                lo, hi = off[name]
                p[name] = jnp.dot(u, wr_ref[:, lo:hi], preferred_element_type=F32)
                yield

        yield from project_rest(("sc_c", "sc_h"))
        gc = jnp.concatenate(gc_parts, axis=0)
        gl_rows = jnp.concatenate(gl_parts, axis=0)
        gc_t = gc.T
        for c in range(nchunk):
            gct_scr[c] = gc_t[0:SUBLANES, c * CHUNK:(c + 1) * CHUNK]

        acts = []
        for s in range(nseq):
            pq_s = pq[s * tt:(s + 1) * tt]
            acts.append(_silu(causal_conv(pq_s, cq_scr, s, cw_ref, n_conv)))
            cq_out_ref[s] = pq_s[tt - (n_conv - 1):]
        act = jnp.concatenate(acts, axis=0)

        eg_all = jnp.exp(gc)
        ekd_all = jnp.exp(gl_rows - gc)
        egl_all = jnp.exp(gl_rows)
        per_head = []
        for h, lanes in enumerate(head_lanes):
            q = _l2norm(act[:, h * DN_DK:(h + 1) * DN_DK]) * (DN_DK ** -0.5)
            kk = _l2norm(act[:, hk + h * DN_DK:hk + (h + 1) * DN_DK])
            v = act[:, 2 * hk + h * DN_DV:2 * hk + (h + 1) * DN_DV]
            beta = beta_all[:, DN_HEADS + h:DN_HEADS + h + 1]
            eg = eg_all[:, h:h + 1]
            kb = kk * beta
            per_head.append((kb.astype(BF16), q.astype(BF16), kk.astype(BF16), (v * beta).astype(BF16),
                             (kb * eg).astype(BF16)))
            qd_scr[:, lanes] = (q * eg).astype(BF16)
            kd_scr[:, lanes] = (kk * ekd_all[:, h:h + 1]).astype(BF16)
            for c in range(nchunk):
                gl_scr[c * SUBLANES:(c + 1) * SUBLANES, lanes] = jnp.broadcast_to(
                    egl_all[c * CHUNK:c * CHUNK + SUBLANES, h:h + 1], (SUBLANES, DN_DV))

        z = p["sc_c"] * p["sc_h"]
        zcs = []
        for s in range(nseq):
            z_s = z[s * tt:(s + 1) * tt]
            zcs.append(causal_conv(z_s, cm_scr, s, mw_ref, n_mconv))
            cm_out_ref[s] = z_s[tt - (n_mconv - 1):]

        where = [(c, h, lanes) for c in range(nchunk) for h, lanes in enumerate(head_lanes)]
        units = [tuple(arr[c * CHUNK:(c + 1) * CHUNK] for arr in per_head[h])
                 + (gc[c * CHUNK:(c + 1) * CHUNK, h:h + 1], gct_scr[c, h:h + 1, :])
                 for c, h, lanes in where]
        local = yield from _alternate(_gdn_local_steps(units, tril, strict), project_rest(("sc_b", "og")))
        osc_scr[...] = p["sc_b"] * jnp.concatenate(zcs, axis=0)
        og_scr[...] = p["og"]
        for (c, h, lanes), (u_c, w_c, qk_c) in zip(where, local):
            u_scr[c * CHUNK:(c + 1) * CHUNK, lanes] = u_c
            w_scr[c * CHUNK:(c + 1) * CHUNK, lanes] = w_c.astype(BF16)
            qk_scr[c * DN_HEADS + h] = qk_c.astype(BF16)

    def back():
        u_all, w_all, qd_all, kd_all = u_scr[...], w_scr[...], qd_scr[...], kd_scr[...]
        gl_all, og, o_sc = gl_scr[...], og_scr[...], osc_scr[...]
        qks = [qk_scr[i] for i in range(nchunk * DN_HEADS)]
        xs = [xb_ref[s] for s in range(nseq)]
        gates = [modb_ref[s, 3 * sub + 2:3 * sub + 3, :] for s in range(nseq)]
        states = [s_scr[i] * keep_back for i in range(nseq * DN_HEADS)]
        o_rows = [[None] * DN_HEADS for _ in range(nchunk)]
        for j in range(per_seq):
            chains = [(s, s * per_seq + j, h, lanes) for s in range(nseq) for h, lanes in enumerate(head_lanes)]
            rws = [slice(c * CHUNK, (c + 1) * CHUNK) for _, c, _, _ in chains]
            r = [jnp.dot(jnp.concatenate([w_all[rw, lanes], qd_all[rw, lanes]], axis=0),
                         states[s * DN_HEADS + h].astype(BF16), preferred_element_type=F32)
                 for (s, c, h, lanes), rw in zip(chains, rws)]
            yield
            v_new = [(u_all[rw, lanes] - r_u[:CHUNK]).astype(BF16)
                     for (s, c, h, lanes), rw, r_u in zip(chains, rws, r)]
            for (s, c, h, lanes), r_u, v_u in zip(chains, r, v_new):
                o_rows[c][h] = r_u[CHUNK:] + jnp.dot(qks[c * DN_HEADS + h], v_u, preferred_element_type=F32)
            for (s, c, h, lanes), rw, v_u in zip(chains, rws, v_new):
                i = s * DN_HEADS + h
                states[i] = (states[i] * gl_all[c * SUBLANES:c * SUBLANES + 1, lanes]
                             + lax.dot_general(kd_all[rw, lanes], v_u, (((0,), (0,)), ((), ())),
                                               preferred_element_type=F32))
            yield
        for i, st in enumerate(states):
            s_scr[i] = st
        for s in range(nseq):
            for h in range(DN_HEADS):
                s_out_ref[s, h] = states[s * DN_HEADS + h]

        o_dn = jnp.concatenate(
            [_gate_norm(jnp.concatenate([o_rows[c][h] for c in range(nchunk)], axis=0),
                        og[:, lanes], ng_ref[...]) for h, lanes in enumerate(head_lanes)], axis=1)
        dn_w = o_dn.shape[1]
        mix = (jnp.dot(o_dn.astype(BF16), w_out_ref[0:dn_w, :], preferred_element_type=F32)
               + jnp.dot(o_sc.astype(BF16), w_out_ref[dn_w:, :], preferred_element_type=F32))
        yield
        for s in range(nseq):
            y_ref[s] = _post(xs[s], mix[s * tt:(s + 1) * tt], gates[s],
                             lng_ref[sub:sub + 1, :], lnb_ref[sub:sub + 1, :], alpha)

    @pl.when(k == 0)
    def _():
        s_scr[...] = jnp.zeros_like(s_scr)
        cq_scr[...] = jnp.zeros_like(cq_scr)
        cm_scr[...] = jnp.zeros_like(cm_scr)
        _run_interleaved(front())

    @pl.when(jnp.logical_and(k > 0, k < n_tiles))
    def _():
        _run_interleaved(back(), front())

    @pl.when(k == n_tiles)
    def _():
        _run_interleaved(back())


def _mixer_prompt(x, mod, w_in, w_out, conv_w, mconv_w, alog, dtb, norm_g, lng, lnb, *, alpha, nseq, tt, off):
    bsz, t, d = x.shape
    assert t % tt == 0 and tt % LANES == 0 and bsz % nseq == 0
    nt = t // tt
    n_tiles = (bsz // nseq) * nt
    rows = nseq * tt
    qkv_dim = conv_w.shape[-1]
    sc_w = off["sc_c"][1] - off["sc_c"][0]
    dn_w = DN_HEADS * DN_DV
    n_conv, n_mconv = conv_w.shape[0], mconv_w.shape[0]
    assert max(n_conv, n_mconv) - 1 <= SUBLANES <= tt
    nchunk = rows // CHUNK
    kern = functools.partial(_mixer_prompt_kernel, alpha=alpha, nseq=nseq, tt=tt, tiles_per_group=nt,
                             n_tiles=n_tiles, off=off, n_conv=n_conv, n_mconv=n_mconv)

    def front_tile(k):
        return jnp.minimum(k, n_tiles - 1)

    def back_tile(k):
        return jnp.maximum(k - 1, 0)

    y, s_out, cq, cm = pl.pallas_call(
        kern,
        grid=(n_tiles + 1,),
        in_specs=[pl.BlockSpec((nseq, tt, d), lambda k: (front_tile(k) // nt, front_tile(k) % nt, 0)),
                  pl.BlockSpec((nseq, 3 * N_SUB, d), lambda k: (front_tile(k) // nt, 0, 0)),
                  pl.BlockSpec((nseq, tt, d), lambda k: (back_tile(k) // nt, back_tile(k) % nt, 0)),
                  pl.BlockSpec((nseq, 3 * N_SUB, d), lambda k: (back_tile(k) // nt, 0, 0)),
                  *[_const_spec(w.shape) for w in w_in], _const_spec(w_out.shape),
                  _const_spec(conv_w.shape), _const_spec(mconv_w.shape),
                  _const_spec(alog.shape), _const_spec(dtb.shape), _const_spec(norm_g.shape),
                  _const_spec(lng.shape), _const_spec(lnb.shape)],
        out_specs=[pl.BlockSpec((nseq, tt, d), lambda k: (back_tile(k) // nt, back_tile(k) % nt, 0)),
                   pl.BlockSpec((nseq, DN_HEADS, DN_DK, DN_DV), lambda k: (back_tile(k) // nt, 0, 0, 0)),
                   pl.BlockSpec((nseq, n_conv - 1, qkv_dim), lambda k: (front_tile(k) // nt, 0, 0)),
                   pl.BlockSpec((nseq, n_mconv - 1, sc_w), lambda k: (front_tile(k) // nt, 0, 0))],
        out_shape=[jax.ShapeDtypeStruct((bsz, t, d), F32),
                   jax.ShapeDtypeStruct((bsz, DN_HEADS, DN_DK, DN_DV), F32),
                   jax.ShapeDtypeStruct((bsz, n_conv - 1, qkv_dim), F32),
                   jax.ShapeDtypeStruct((bsz, n_mconv - 1, sc_w), F32)],
        scratch_shapes=[pltpu.VMEM((nseq * DN_HEADS, DN_DK, DN_DV), F32),
                        pltpu.VMEM((nseq, SUBLANES, qkv_dim), F32),
                        pltpu.VMEM((nseq, SUBLANES, sc_w), F32),
                        pltpu.VMEM((nchunk, SUBLANES, CHUNK), F32),
                        pltpu.VMEM((rows, dn_w), F32),
                        pltpu.VMEM((rows, dn_w), BF16),
                        pltpu.VMEM((nchunk * DN_HEADS, CHUNK, CHUNK), BF16),
                        pltpu.VMEM((rows, dn_w), BF16),
                        pltpu.VMEM((rows, dn_w), BF16),
                        pltpu.VMEM((nchunk * SUBLANES, dn_w), F32),
                        pltpu.VMEM((rows, dn_w), F32),
                        pltpu.VMEM((rows, sc_w), F32)],
        compiler_params=pltpu.CompilerParams(dimension_semantics=("arbitrary",),
                                             vmem_limit_bytes=V7X_VMEM_LIMIT),
        name="mixer_prompt",
    )(x, mod, x, mod, *w_in, w_out, conv_w, mconv_w, alog, dtb, norm_g, lng, lnb)
    return y, s_out, cq, cm


_PACK = SUBLANES


def _sample_phase_a(x_ref, mod_ref, wq_ref, wr_ref, wab_ref, cw_ref, mw_ref, alog_ref, dtb_ref, cq_in_ref, cm_in_ref,
                    cq_out_ref, cm_out_ref, og_scr, osc_scr, lhs_scr, u_scr, kd_scr, gl_scr, qk_scr,
                    *, t_len, d, off, n_conv, n_mconv):
    bsz = x_ref.shape[0]
    hk = DN_HEADS * DN_DK
    sc_w = off["sc_c"][1] - off["sc_c"][0]

    x = _stack_time(x_ref, t_len, d)
    u = (x * (1.0 + _mod_rows(mod_ref, 1, t_len, d)) + _mod_rows(mod_ref, 0, t_len, d)).astype(BF16)
    pq = jnp.dot(u, wq_ref[...], preferred_element_type=F32)
    ab = jnp.dot(u, wab_ref[...], preferred_element_type=F32)
    rest = jnp.dot(u, wr_ref[...], preferred_element_type=F32)
    p = {name: rest[:, lo:hi] for name, (lo, hi) in off.items()}

    def rows(arr, t):
        return arr[t * bsz:(t + 1) * bsz]

    seq = [cq_in_ref[j] for j in range(n_conv - 1)] + [rows(pq, t) for t in range(t_len)]
    act = []
    for t in range(t_len):
        acc = None
        for j in range(n_conv):
            term = seq[t + j] * cw_ref[j:j + 1, :]
            acc = term if acc is None else acc + term
        act.append(_silu(acc))
    for j in range(n_conv - 1):
        cq_out_ref[j] = seq[t_len + j]

    z = p["sc_c"] * p["sc_h"]
    zseq = ([cm_in_ref[:, j * sc_w:(j + 1) * sc_w] for j in range(n_mconv - 1)]
            + [rows(z, t) for t in range(t_len)])
    zc = []
    for t in range(t_len):
        acc = None
        for j in range(n_mconv):
            term = zseq[t + j] * mw_ref[j:j + 1, :]
            acc = term if acc is None else acc + term
        zc.append(acc)
    for j in range(n_mconv - 1):
        cm_out_ref[:, j * sc_w:(j + 1) * sc_w] = zseq[t_len + j]
    osc_scr[...] = p["sc_b"] * jnp.concatenate(zc, axis=0)
    og_scr[...] = p["og"]

    beta_all = jax.nn.sigmoid(ab)
    g_all = -jnp.exp(alog_ref[...]) * _softplus(ab + dtb_ref[...])
    zero = jnp.zeros((bsz, LANES), F32)

    for h in range(DN_HEADS):
        q = [_l2norm(a[:, h * DN_DK:(h + 1) * DN_DK]) * (DN_DK ** -0.5) for a in act]
        k = [_l2norm(a[:, hk + h * DN_DK:hk + (h + 1) * DN_DK]) for a in act]
        v = [a[:, 2 * hk + h * DN_DV:2 * hk + (h + 1) * DN_DV] for a in act]
        beta = [rows(beta_all, t)[:, DN_HEADS + h:DN_HEADS + h + 1] for t in range(t_len)]
        g = [rows(g_all, t)[:, h:h + 1] for t in range(t_len)]
        gc = [g[0]]
        for t in range(1, t_len):
            gc.append(gc[-1] + g[t])
        kb = [k[t] * beta[t] for t in range(t_len)]
        vb = [v[t] * beta[t] for t in range(t_len)]
        eg = [jnp.exp(gc[t]) for t in range(t_len)]
        kbe = [kb[t] * eg[t] for t in range(t_len)]
        lm = [[None] * t_len for _ in range(t_len)]
        n_qk = 0
        for i in range(t_len):
            for j in range(i + 1):
                dec = jnp.exp(gc[i] - gc[j])
                qk_scr[h, n_qk] = jnp.broadcast_to(_rowsum(q[i] * k[j]) * dec, (bsz, LANES))
                n_qk += 1
                if j < i:
                    lm[i][j] = _rowsum(kb[i] * k[j]) * dec
        tm = [[None] * t_len for _ in range(t_len)]
        for i in range(t_len):
            for j in range(i):
                acc = lm[i][j]
                for m in range(j + 1, i):
                    acc = acc + lm[i][m] * tm[m][j]
                tm[i][j] = -acc
        for i in range(t_len):
            ui, wi = vb[i], kbe[i]
            for j in range(i):
                ui = ui + tm[i][j] * vb[j]
                wi = wi + tm[i][j] * kbe[j]
            lhs_scr[h, pl.ds(i, bsz, stride=_PACK), :] = wi
            lhs_scr[h, pl.ds(t_len + i, bsz, stride=_PACK), :] = q[i] * eg[i]
            u_scr[h, pl.ds(i, bsz, stride=_PACK), :] = ui
            u_scr[h, pl.ds(t_len + i, bsz, stride=_PACK), :] = zero
            kd_scr[h, pl.ds(i, bsz, stride=_PACK), :] = k[i] * jnp.exp(gc[t_len - 1] - gc[i])
            kd_scr[h, pl.ds(t_len + i, bsz, stride=_PACK), :] = zero
        gl_scr[h] = jnp.broadcast_to(jnp.exp(gc[t_len - 1]), (bsz, LANES))


def _sample_phase_c(x_ref, mod_ref, w_out_ref, ng_ref, lng_ref, lnb_ref, y_ref,
                    og_scr, osc_scr, res_scr, vn_scr, qk_scr, *, alpha, t_len, d):
    bsz = x_ref.shape[0]
    sub = 1
    x = _stack_time(x_ref, t_len, d)
    gate = _mod_rows(mod_ref, 2, t_len, d)
    og = og_scr[...]
    per_t = []
    for i in range(t_len):
        heads = []
        for h in range(DN_HEADS):
            oi = res_scr[h, pl.ds(t_len + i, bsz, stride=_PACK), :]
            base = i * (i + 1) // 2
            for j in range(i + 1):
                oi = oi + qk_scr[h, base + j] * vn_scr[h, pl.ds(j, bsz, stride=_PACK), :]
            heads.append(_gate_norm(oi, og[i * bsz:(i + 1) * bsz, h * DN_DV:(h + 1) * DN_DV], ng_ref[...]))
        per_t.append(jnp.concatenate(heads, axis=1))
    o_dn = jnp.concatenate(per_t, axis=0)
    dn_w = o_dn.shape[1]
    mix = (jnp.dot(o_dn.astype(BF16), w_out_ref[0:dn_w, :], preferred_element_type=F32)
           + jnp.dot(osc_scr[...].astype(BF16), w_out_ref[dn_w:, :], preferred_element_type=F32))
    y = _post(x, mix, gate, lng_ref[sub:sub + 1, :], lnb_ref[sub:sub + 1, :], alpha)
    for t in range(t_len):
        y_ref[:, t * d:(t + 1) * d] = y[t * bsz:(t + 1) * bsz]


def _mixer_sample_kernel(x_ref, mod_ref, wq_ref, wr_ref, wab_ref, w_out_ref, cw_ref, mw_ref, alog_ref, dtb_ref, ng_ref,
                         lng_ref, lnb_ref, cq_in_ref, cm_in_ref, s_in_ref,
                         y_ref, cq_out_ref, cm_out_ref, s_out_ref,
                         og_scr, osc_scr, lhs_scr, u_scr, kd_scr, gl_scr, qk_scr, res_scr, vn_scr,
                         *, alpha, t_len, d, off, n_conv, n_mconv, bt):
    step = pl.program_id(0)
    assert 2 * t_len == _PACK, "packed buffers hold [T rows | T rows] per sequence"

    @pl.when(step == 0)
    def _():
        _sample_phase_a(x_ref, mod_ref, wq_ref, wr_ref, wab_ref, cw_ref, mw_ref, alog_ref, dtb_ref, cq_in_ref, cm_in_ref,
                        cq_out_ref, cm_out_ref, og_scr, osc_scr, lhs_scr, u_scr, kd_scr, gl_scr, qk_scr,
                        t_len=t_len, d=d, off=off, n_conv=n_conv, n_mconv=n_mconv)

    top = (lax.broadcasted_iota(jnp.int32, (_PACK, LANES), 0) < t_len).astype(F32)

    chains = [(bl, h) for bl in range(bt) for h in range(DN_HEADS)]
    seq0 = step * bt
    row0 = [pl.multiple_of((seq0 + bl) * _PACK, _PACK) for bl in range(bt)]
    states = [s_in_ref[bl, h] for bl, h in chains]
    r = [_mm(lhs_scr[h, pl.ds(row0[bl], _PACK), :], s) for (bl, h), s in zip(chains, states)]
    vns = []
    for (bl, h), r_u in zip(chains, r):
        res_scr[h, pl.ds(row0[bl], _PACK), :] = r_u
        vn = (u_scr[h, pl.ds(row0[bl], _PACK), :] - r_u) * top
        vn_scr[h, pl.ds(row0[bl], _PACK), :] = vn
        vns.append(vn)
    upd = [_mm_at(kd_scr[h, pl.ds(row0[bl], _PACK), :], vn) for (bl, h), vn in zip(chains, vns)]
    for (bl, h), s, s_add in zip(chains, states, upd):
        s_out_ref[bl, h] = s * gl_scr[h, pl.ds(seq0 + bl, 1), :] + s_add

    @pl.when(step == pl.num_programs(0) - 1)
    def _():
        _sample_phase_c(x_ref, mod_ref, w_out_ref, ng_ref, lng_ref, lnb_ref, y_ref,
                        og_scr, osc_scr, res_scr, vn_scr, qk_scr, alpha=alpha, t_len=t_len, d=d)


def _mixer_sample(x, mod, w_in, w_out, conv_w, mconv_w, alog, dtb, norm_g, lng, lnb, s_in, cq_in, cm_in,
                  *, alpha, off, bt, t):
    bsz, d = x.shape[0], x.shape[1] // t
    assert bsz % bt == 0
    sc_w = off["sc_c"][1] - off["sc_c"][0]
    dn_w = DN_HEADS * DN_DV
    n_conv, n_mconv = conv_w.shape[0], mconv_w.shape[0]
    n_qk = t * (t + 1) // 2
    x2 = x
    cq2 = jnp.swapaxes(cq_in, 0, 1)
    cm2 = cm_in.reshape(bsz, (n_mconv - 1) * sc_w)
    kern = functools.partial(_mixer_sample_kernel, alpha=alpha, t_len=t, d=d, off=off,
                             n_conv=n_conv, n_mconv=n_mconv, bt=bt)
    consts = (*w_in, w_out, conv_w, mconv_w, alog, dtb, norm_g, lng, lnb, cq2, cm2)
    state_spec = pl.BlockSpec((bt, DN_HEADS, DN_DK, DN_DV), lambda i: (i, 0, 0, 0))
    packed = pltpu.VMEM((DN_HEADS, _PACK * bsz, LANES), F32)
    y, cq, cm, s_out = pl.pallas_call(
        kern,
        grid=(bsz // bt,),
        in_specs=([_const_spec(x2.shape), _sub_mod_spec(bsz, d, 1)]
                  + [_const_spec(a.shape) for a in consts] + [state_spec]),
        out_specs=[pl.BlockSpec(x2.shape, lambda i: (0, 0)),
                   pl.BlockSpec(cq2.shape, lambda i: (0, 0, 0)),
                   pl.BlockSpec(cm2.shape, lambda i: (0, 0)),
                   state_spec],
        out_shape=[jax.ShapeDtypeStruct(x2.shape, F32),
                   jax.ShapeDtypeStruct(cq2.shape, F32),
                   jax.ShapeDtypeStruct(cm2.shape, F32),
                   jax.ShapeDtypeStruct(s_in.shape, F32)],
        scratch_shapes=[pltpu.VMEM((t * bsz, dn_w), F32),
                        pltpu.VMEM((t * bsz, sc_w), F32),
                        packed,
                        packed,
                        packed,
                        pltpu.VMEM((DN_HEADS, bsz, LANES), F32),
                        pltpu.VMEM((DN_HEADS, n_qk, bsz, LANES), F32),
                        packed,
                        packed],
        compiler_params=pltpu.CompilerParams(dimension_semantics=("arbitrary",),
                                             vmem_limit_bytes=V7X_VMEM_LIMIT),
        name="mixer_sample",
    )(x2, mod, *consts, s_in)
    return y, s_out, jnp.swapaxes(cq, 0, 1), cm.reshape(bsz, n_mconv - 1, sc_w)


def _pad_lanes(v):
    return jnp.zeros((1, AB_PAD), F32).at[0, :v.shape[0]].set(v)


def kernel(x_prompt, x_sample, state_ssm, state_conv_qkv, state_conv_mix, c_prompt, c_sample, w_ada, b_ada, ln_g, ln_b, ffn1_wg, ffn1_wu, ffn1_wd, ffn2_wg, ffn2_wu, ffn2_wd, w_in, conv_qkv_w, a_log, dt_bias, dn_norm_g, conv_mix_w, w_out):
    depth = w_ada.shape[0]
    alpha = (2 * depth) ** 0.25
    bp, tp, d = x_prompt.shape
    qkv_dim = conv_qkv_w.shape[-1]
    sc_width = conv_mix_w.shape[-1]
    dn_width = DN_HEADS * DN_DV
    off = _proj_layout(dn_width, sc_width)
    tm = min(512, tp)
    nseq = 2 if bp % 2 == 0 else 1
    tt = min(512 // nseq, tp)
    bt = min(16, x_sample.shape[0])
    ts = x_sample.shape[1]

    hp, hs = x_prompt, x_sample
    outs = [[] for _ in range(6)]
    for l in range(depth):
        mod_p, mod_s, *w_in_p = _adaln(c_prompt, c_sample, w_ada[l], b_ada[l].reshape(1, -1),
                                       jnp.swapaxes(w_in[l], 0, 1), qkv_dim)
        mod_p = mod_p.reshape(bp, 3 * N_SUB, d)
        lng, lnb = ln_g[l], ln_b[l]
        w1 = (ffn1_wg[l], ffn1_wu[l], ffn1_wd[l])
        w2 = (ffn2_wg[l], ffn2_wu[l], ffn2_wd[l])
        w_out_b = w_out[l].astype(BF16)
        alog, dtb = _pad_lanes(a_log[l]), _pad_lanes(dt_bias[l])
        norm_g = dn_norm_g[l].reshape(1, -1)
        mix_args = (w_in_p, w_out_b, conv_qkv_w[l], conv_mix_w[l], alog, dtb, norm_g, lng, lnb)

        hp, hs = _ffn(hp, hs, mod_p, mod_s, *w1, lng, lnb, sub=0, alpha=alpha, tm=tm, ts=ts,
                      ys_3d=False)
        hp, a1, a2, a3 = _mixer_prompt(hp, mod_p, *mix_args, alpha=alpha, nseq=nseq, tt=tt, off=off)
        hs, b1, b2, b3 = _mixer_sample(hs, mod_s, *mix_args, state_ssm[l], state_conv_qkv[l],
                                       state_conv_mix[l], alpha=alpha, off=off, bt=bt, t=ts)
        hp, hs = _ffn(hp, hs, mod_p, mod_s, *w2, lng, lnb, sub=2, alpha=alpha, tm=tm, ts=ts,
                      ys_3d=True)
        for lst, val in zip(outs, (a1, a2, a3, b1, b2, b3)):
            lst.append(val)
    return (hp, hs) + tuple(jnp.stack(o) for o in outs)
```

```python
import functools

import jax
import jax.numpy as jnp
from jax import lax
from jax.experimental import pallas as pl
from jax.experimental.pallas import tpu as pltpu

F32 = jnp.float32
BF16 = jnp.bfloat16

LN_EPS = 1e-5
RMS_EPS = 1e-6
N_SUB = 3
DN_HEADS = 4
DN_DK = 128
DN_DV = 128
CHUNK = 64
LANES = 128
SUBLANES = 8
AB_PAD = LANES

V7X_VMEM_LIMIT = 60000 * 1024
CAST_BUFFERS = 3
FFN_CHUNK = 256


def _mm(a, b):
    return jnp.dot(a.astype(BF16), b.astype(BF16), preferred_element_type=F32)


def _mm_at(a, b):
    return lax.dot_general(a.astype(BF16), b.astype(BF16), (((0,), (0,)), ((), ())),
                           preferred_element_type=F32)


def _mm_f32(a, b):
    return jnp.dot(a, b, preferred_element_type=F32, precision=lax.Precision.HIGHEST)


def _silu(x):
    return x * jax.nn.sigmoid(x)


def _softplus(x):
    return jnp.maximum(x, 0.0) + jnp.log1p(jnp.exp(-jnp.abs(x)))


def _layer_norm(y, g, b):
    mu = jnp.mean(y, axis=-1, keepdims=True)
    yc = y - mu
    var = jnp.mean(yc * yc, axis=-1, keepdims=True)
    return yc * lax.rsqrt(var + LN_EPS) * g + b


def _post(x, delta, gate, g, b, alpha):
    return _layer_norm(alpha * x + gate * delta, g, b)


def _rowsum(x):
    return jnp.sum(x, axis=-1, keepdims=True)


def _l2norm(x):
    return x * lax.rsqrt(_rowsum(x * x) + RMS_EPS)


def _const_spec(shape):
    nd = len(shape)
    return pl.BlockSpec(shape, lambda *_: (0,) * nd, pipeline_mode=pl.Buffered(1))


def _row(ref, j):
    return ref[j] if len(ref.shape) == 3 else ref[j:j + 1, :]


def _adaln_kernel(cp_ref, cs_ref, w_ref, b_ref, win_ref, op_ref, os_ref, wq_ref, wr_ref, wab_ref, *, qkv_dim):
    mp = cp_ref.shape[0]
    c = jnp.concatenate([cp_ref[...], cs_ref[...]], axis=0)
    mod = _mm(_silu(c), w_ref[...]) + b_ref[...]
    op_ref[...] = mod[:mp]
    os_ref[...] = mod[mp:]

    gates = 2 * DN_HEADS
    wq_ref[...] = win_ref[0:qkv_dim, :].T.astype(BF16)
    wr_ref[...] = win_ref[qkv_dim + gates:, :].T.astype(BF16)
    ab_rows = jnp.concatenate([win_ref[qkv_dim:qkv_dim + gates, :],
                               jnp.zeros((AB_PAD - gates, win_ref.shape[1]), F32)], axis=0)
    wab_ref[...] = ab_rows.T.astype(BF16)


def _adaln(c_p, c_s, w, b, w_in_t, qkv_dim):
    (mp, d), ms = c_p.shape, c_s.shape[0]
    n = w.shape[1]
    proj = w_in_t.shape[0]
    rest = proj - qkv_dim - 2 * DN_HEADS
    assert qkv_dim % LANES == 0 and rest % LANES == 0 and (2 * DN_HEADS) % SUBLANES == 0
    steps = max(g for g in (1, 2, 4) if n % (g * LANES) == 0 and d % (g * LANES) == 0)
    tn, rows = n // steps, d // steps
    return pl.pallas_call(
        functools.partial(_adaln_kernel, qkv_dim=qkv_dim),
        grid=(steps,),
        in_specs=[_const_spec((mp, d)), _const_spec((ms, d)),
                  pl.BlockSpec((d, tn), lambda j: (0, j)),
                  pl.BlockSpec((1, tn), lambda j: (0, j)),
                  pl.BlockSpec((proj, rows), lambda j: (0, j))],
        out_specs=[pl.BlockSpec((mp, tn), lambda j: (0, j)), pl.BlockSpec((ms, tn), lambda j: (0, j)),
                   pl.BlockSpec((rows, qkv_dim), lambda j: (j, 0)),
                   pl.BlockSpec((rows, rest), lambda j: (j, 0)),
                   pl.BlockSpec((rows, AB_PAD), lambda j: (j, 0))],
        out_shape=[jax.ShapeDtypeStruct((mp, n), F32), jax.ShapeDtypeStruct((ms, n), F32),
                   jax.ShapeDtypeStruct((d, qkv_dim), BF16), jax.ShapeDtypeStruct((d, rest), BF16),
                   jax.ShapeDtypeStruct((d, AB_PAD), BF16)],
        compiler_params=pltpu.CompilerParams(dimension_semantics=("arbitrary",),
                                             vmem_limit_bytes=V7X_VMEM_LIMIT),
        name="adaln",
    )(c_p, c_s, w, b, w_in_t)


def _swiglu(x, shift, scale, wg_ref, wu_ref, wd_ref):
    u = (x * (1.0 + scale) + shift).astype(BF16)
    hg = jnp.dot(u, wg_ref[...], preferred_element_type=F32)
    hu = jnp.dot(u, wu_ref[...], preferred_element_type=F32)
    h = (_silu(hg) * hu).astype(BF16)
    return jnp.dot(h, wd_ref[...], preferred_element_type=F32)


def _stack_time(ref, t_len, d):
    if len(ref.shape) == 3:
        return jnp.concatenate([ref[:, t, :] for t in range(t_len)], axis=0)
    return jnp.concatenate([ref[:, t * d:(t + 1) * d] for t in range(t_len)], axis=0)


def _mod_rows(mod_ref, idx, t_len, d):
    m = mod_ref[:, idx * d:(idx + 1) * d]
    return jnp.concatenate([m] * t_len, axis=0)


def _sub_mod_spec(bsz, d, sub):
    return pl.BlockSpec((bsz, 3 * d), lambda *_: (0, sub), pipeline_mode=pl.Buffered(1))


def _first_tile_swiglu(x, shift, scale, w_hbm, w_scr, stages, sems):
    wg_hbm, wu_hbm, wd_hbm = w_hbm
    wg_scr, wu_scr, wd_scr = w_scr
    nbuf = stages[0].shape[0]
    cw = stages[0].shape[2]
    n = wg_hbm.shape[1] // cw

    def copies(j):
        slot = j % nbuf
        cols = pl.ds(j * cw, cw)
        return (pltpu.make_async_copy(wg_hbm.at[:, cols], stages[0].at[slot], sems[0].at[slot]),
                pltpu.make_async_copy(wu_hbm.at[:, cols], stages[1].at[slot], sems[1].at[slot]),
                pltpu.make_async_copy(wd_hbm.at[cols, :], stages[2].at[slot], sems[2].at[slot]))

    queue = [copies(j) for j in range(n)]
    ahead = nbuf - 1
    for group in queue[:ahead]:
        for copy in group:
            copy.start()
    u = (x * (1.0 + scale) + shift).astype(BF16)
    acc = None
    for j, group in enumerate(queue):
        if j + ahead < n:
            for copy in queue[j + ahead]:
                copy.start()
        for copy in group:
            copy.wait()
        slot, cols = j % nbuf, slice(j * cw, (j + 1) * cw)
        wg_scr[:, cols] = stages[0][slot].astype(BF16)
        wu_scr[:, cols] = stages[1][slot].astype(BF16)
        wd_scr[cols, :] = stages[2][slot].astype(BF16)
        hg = jnp.dot(u, wg_scr[:, cols], preferred_element_type=F32)
        hu = jnp.dot(u, wu_scr[:, cols], preferred_element_type=F32)
        part = jnp.dot((_silu(hg) * hu).astype(BF16), wd_scr[cols, :], preferred_element_type=F32)
        acc = part if acc is None else acc + part
    return acc


def _ffn_kernel(xp_ref, modp_ref, xq_ref, modq_ref, xs_ref, mods_ref, wg_hbm, wu_hbm, wd_hbm, lng_ref, lnb_ref,
                yp_ref, ys_ref, wg_scr, wu_scr, wd_scr, d_scr, stage_g, stage_u, stage_d, sem_g, sem_u, sem_d,
                *, sub, alpha, n_prompt, t_len, d):
    k = pl.program_id(0)
    g, b = _row(lng_ref, sub), _row(lnb_ref, sub)

    def matmuls():
        return _swiglu(xp_ref[...], modp_ref[0, 3 * sub + 0:3 * sub + 1, :], modp_ref[0, 3 * sub + 1:3 * sub + 2, :],
                       wg_scr, wu_scr, wd_scr)

    def finish(d_prev):
        yp_ref[...] = _post(xq_ref[...], 0.5 * d_prev, modq_ref[0, 3 * sub + 2:3 * sub + 3, :], g, b, alpha)

    @pl.when(k == 0)
    def _():
        d_scr[...] = _first_tile_swiglu(
            xp_ref[...], modp_ref[0, 3 * sub + 0:3 * sub + 1, :], modp_ref[0, 3 * sub + 1:3 * sub + 2, :],
            (wg_hbm, wu_hbm, wd_hbm), (wg_scr, wu_scr, wd_scr), (stage_g, stage_u, stage_d),
            (sem_g, sem_u, sem_d))

    @pl.when(jnp.logical_and(k > 0, k < n_prompt))
    def _():
        d_prev = d_scr[...]
        d_new = matmuls()
        finish(d_prev)
        d_scr[...] = d_new

    @pl.when(k == n_prompt)
    def _():
        finish(d_scr[...])
        bsz = xs_ref.shape[0]
        x = _stack_time(xs_ref, t_len, d)
        delta = _swiglu(x, _mod_rows(mods_ref, 0, t_len, d), _mod_rows(mods_ref, 1, t_len, d),
                        wg_scr, wu_scr, wd_scr)
        y = _post(x, 0.5 * delta, _mod_rows(mods_ref, 2, t_len, d), g, b, alpha)
        for t in range(t_len):
            if len(ys_ref.shape) == 3:
                ys_ref[:, t, :] = y[t * bsz:(t + 1) * bsz]
            else:
                ys_ref[:, t * d:(t + 1) * d] = y[t * bsz:(t + 1) * bsz]


def _ffn(xp, xs, mod_p, mod_s, wg, wu, wd, lng, lnb, *, sub, alpha, tm, ts, ys_3d):
    bp, tp, d = xp.shape
    bs = xs.shape[0]
    ys_shape = (bs, ts, d) if ys_3d else (bs, ts * d)
    f = wg.shape[1]
    assert tp % tm == 0
    per_seq = tp // tm
    n_prompt = bp * per_seq
    assert f % FFN_CHUNK == 0
    last = n_prompt - 1
    yp, ys = pl.pallas_call(
        functools.partial(_ffn_kernel, sub=sub, alpha=alpha, n_prompt=n_prompt, t_len=ts, d=d),
        grid=(n_prompt + 1,),
        in_specs=[pl.BlockSpec((tm, d), lambda k: (jnp.minimum(k, last), 0)),
                  pl.BlockSpec((1, 3 * N_SUB, d), lambda k: (jnp.minimum(k, last) // per_seq, 0, 0)),
                  pl.BlockSpec((tm, d), lambda k: (jnp.maximum(k - 1, 0), 0)),
                  pl.BlockSpec((1, 3 * N_SUB, d), lambda k: (jnp.maximum(k - 1, 0) // per_seq, 0, 0)),
                  _const_spec(xs.shape), _sub_mod_spec(bs, d, sub),
                  pl.BlockSpec(memory_space=pl.ANY), pl.BlockSpec(memory_space=pl.ANY),
                  pl.BlockSpec(memory_space=pl.ANY),
                  _const_spec(lng.shape), _const_spec(lnb.shape)],
        out_specs=[pl.BlockSpec((tm, d), lambda k: (jnp.maximum(k - 1, 0), 0)),
                   pl.BlockSpec(ys_shape, lambda k: (0,) * len(ys_shape))],
        out_shape=[jax.ShapeDtypeStruct((bp * tp, d), F32),
                   jax.ShapeDtypeStruct(ys_shape, F32)],
        scratch_shapes=[pltpu.VMEM((d, f), BF16), pltpu.VMEM((d, f), BF16), pltpu.VMEM((f, d), BF16),
                        pltpu.VMEM((tm, d), F32),
                        pltpu.VMEM((CAST_BUFFERS, d, FFN_CHUNK), F32), pltpu.VMEM((CAST_BUFFERS, d, FFN_CHUNK), F32),
                        pltpu.VMEM((CAST_BUFFERS, FFN_CHUNK, d), F32),
                        pltpu.SemaphoreType.DMA((CAST_BUFFERS,)), pltpu.SemaphoreType.DMA((CAST_BUFFERS,)),
                        pltpu.SemaphoreType.DMA((CAST_BUFFERS,))],
        compiler_params=pltpu.CompilerParams(dimension_semantics=("arbitrary",),
                                             vmem_limit_bytes=V7X_VMEM_LIMIT),
        name=f"ffn{sub}",
    )(xp.reshape(bp * tp, d), mod_p, xp.reshape(bp * tp, d), mod_p, xs, mod_s,
      wg, wu, wd, lng, lnb)
    return yp.reshape(bp, tp, d), ys


def _tri_inverse_steps(lmats, c):
    ls = [l.astype(BF16) for l in lmats]
    ms = [jnp.dot(l, l, preferred_element_type=F32) for l in ls]
    yield
    rows = lax.broadcasted_iota(jnp.int32, (c, c), 0)
    cols = lax.broadcasted_iota(jnp.int32, (c, c), 1)
    eye = (rows == cols).astype(F32)
    ps = [eye - l for l in lmats]
    power = 2
    while 2 * power < c:
        mp = [jnp.concatenate([m, p], axis=0).astype(BF16) for m, p in zip(ms, ps)]
        sq = [jnp.dot(x, x[:c], preferred_element_type=F32) for x in mp]
        yield
        ms = [x[:c] for x in sq]
        ps = [p + x[c:] for p, x in zip(ps, sq)]
        power *= 2
    pm = [jnp.dot(p.astype(BF16), m.astype(BF16), preferred_element_type=F32) for p, m in zip(ps, ms)]
    yield
    return [p + x for p, x in zip(ps, pm)]


def _gdn_local_steps(units, tril, strict):
    c = units[0][1].shape[0]
    a = [lax.dot_general(jnp.concatenate([kb, q], axis=0), k, (((1,), (1,)), ((), ())),
                         preferred_element_type=F32)
         for kb, q, k, _, _, _, _ in units]
    yield
    lmats, qks = [], []
    for a_u, (_, _, _, _, _, gcol, grow) in zip(a, units):
        decay = jnp.where(tril, jnp.exp(jnp.where(tril, gcol - grow, 0.0)), 0.0)
        lmats.append(jnp.where(strict, a_u[:c] * decay, 0.0))
        qks.append(jnp.where(tril, a_u[c:] * decay, 0.0))
    tms = yield from _tri_inverse_steps(lmats, c)
    uw = [jnp.dot(tm.astype(BF16), jnp.concatenate([vb, kbe], axis=1), preferred_element_type=F32)
          for tm, (_, _, _, vb, kbe, _, _) in zip(tms, units)]
    yield
    return [(x[:, :DN_DV], x[:, DN_DV:], qk) for x, qk in zip(uw, qks)]


def _alternate(primary, secondary):
    result = None
    done = False
    while not done:
        try:
            next(primary)
            yield
        except StopIteration as stop:
            result, done = stop.value, True
        if secondary is not None:
            try:
                next(secondary)
                yield
            except StopIteration:
                secondary = None
    if secondary is not None:
        yield from secondary
    return result


def _run_interleaved(*gens):
    live = list(gens)
    while live:
        for g in list(live):
            try:
                next(g)
            except StopIteration:
                live.remove(g)


def _gate_norm(o, og, norm_g):
    o = o * lax.rsqrt(jnp.mean(o * o, axis=-1, keepdims=True) + RMS_EPS) * norm_g
    return o * _silu(og)


def _proj_layout(dn_width, sc_width):
    off = {}
    pos = 0
    for name, width in (("og", dn_width), ("sc_b", sc_width), ("sc_c", sc_width), ("sc_h", sc_width)):
        off[name] = (pos, pos + width)
        pos += width
    return off


def _mixer_prompt_kernel(xf_ref, modf_ref, xb_ref, modb_ref, wq_ref, wr_ref, wab_ref, w_out_ref, cw_ref,
                         mw_ref, alog_ref, dtb_ref, ng_ref, lng_ref, lnb_ref,
                         y_ref, s_out_ref, cq_out_ref, cm_out_ref,
                         s_scr, cq_scr, cm_scr, gct_scr,
                         u_scr, w_scr, qk_scr, qd_scr, kd_scr, gl_scr, og_scr, osc_scr,
                         *, alpha, nseq, tt, tiles_per_group, n_tiles, off, n_conv, n_mconv):
    k = pl.program_id(0)
    rows_all = nseq * tt
    nchunk = rows_all // CHUNK
    per_seq = tt // CHUNK
    hk = DN_HEADS * DN_DK
    sub = 1
    head_lanes = [slice(h * DN_DK, (h + 1) * DN_DK) for h in range(DN_HEADS)]
    t_front = lax.rem(jnp.minimum(k, n_tiles - 1), tiles_per_group)
    t_back = lax.rem(jnp.maximum(k - 1, 0), tiles_per_group)
    keep_front = jnp.where(t_front == 0, 0.0, 1.0).astype(F32)
    keep_back = jnp.where(t_back == 0, 0.0, 1.0).astype(F32)

    def causal_conv(new_rows, hist_scr, s, w_ref, taps):
        ext = jnp.concatenate([hist_scr[s] * keep_front, new_rows], axis=0)
        acc = new_rows * _row(w_ref, taps - 1)
        for j in range(taps - 1):
            acc = acc + pltpu.roll(ext, taps - 1 - j, axis=0)[SUBLANES:] * _row(w_ref, j)
        hist_scr[s] = new_rows[tt - SUBLANES:]
        return acc

    def front():
        u = jnp.concatenate(
            [(xf_ref[s] * (1.0 + modf_ref[s, 3 * sub + 1:3 * sub + 2, :])
              + modf_ref[s, 3 * sub + 0:3 * sub + 1, :]).astype(BF16) for s in range(nseq)], axis=0)
        ab = jnp.dot(u, wab_ref[...], preferred_element_type=F32)
        pq = jnp.dot(u, wq_ref[...], preferred_element_type=F32)
        yield

        beta_all = jax.nn.sigmoid(ab)
        g = -jnp.exp(alog_ref[...]) * _softplus(ab + dtb_ref[...])
        rows = lax.broadcasted_iota(jnp.int32, (CHUNK, CHUNK), 0)
        cols = lax.broadcasted_iota(jnp.int32, (CHUNK, CHUNK), 1)
        tril = rows >= cols
        strict = rows > cols
        tril_f = tril.astype(F32)
        gc_parts, gl_parts = [], []
        for c in range(nchunk):
            gc_c = _mm_f32(tril_f, g[c * CHUNK:(c + 1) * CHUNK, :])
            gc_parts.append(gc_c)
            gl_parts.append(jnp.broadcast_to(gc_c[CHUNK - 1:CHUNK, :], (CHUNK, AB_PAD)))
        yield
        p = {}

        def project_rest(names):
            for name in names:
                lo, hi = off[name]
                p[name] = jnp.dot(u, wr_ref[:, lo:hi], preferred_element_type=F32)
                yield

        yield from project_rest(("sc_c", "sc_h"))
        gc = jnp.concatenate(gc_parts, axis=0)
        gl_rows = jnp.concatenate(gl_parts, axis=0)
        gc_t = gc.T
        for c in range(nchunk):
            gct_scr[c] = gc_t[0:SUBLANES, c * CHUNK:(c + 1) * CHUNK]

        acts = []
        for s in range(nseq):
            pq_s = pq[s * tt:(s + 1) * tt]
            acts.append(_silu(causal_conv(pq_s, cq_scr, s, cw_ref, n_conv)))
            cq_out_ref[s] = pq_s[tt - (n_conv - 1):]
        act = jnp.concatenate(acts, axis=0)

        eg_all = jnp.exp(gc)
        ekd_all = jnp.exp(gl_rows - gc)
        egl_all = jnp.exp(gl_rows)
        per_head = []
        for h, lanes in enumerate(head_lanes):
            q = _l2norm(act[:, h * DN_DK:(h + 1) * DN_DK]) * (DN_DK ** -0.5)
            kk = _l2norm(act[:, hk + h * DN_DK:hk + (h + 1) * DN_DK])
            v = act[:, 2 * hk + h * DN_DV:2 * hk + (h + 1) * DN_DV]
            beta = beta_all[:, DN_HEADS + h:DN_HEADS + h + 1]
            eg = eg_all[:, h:h + 1]
            kb = kk * beta
            per_head.append((kb.astype(BF16), q.astype(BF16), kk.astype(BF16), (v * beta).astype(BF16),
                             (kb * eg).astype(BF16)))
            qd_scr[:, lanes] = (q * eg).astype(BF16)
            kd_scr[:, lanes] = (kk * ekd_all[:, h:h + 1]).astype(BF16)
            for c in range(nchunk):
                gl_scr[c * SUBLANES:(c + 1) * SUBLANES, lanes] = jnp.broadcast_to(
                    egl_all[c * CHUNK:c * CHUNK + SUBLANES, h:h + 1], (SUBLANES, DN_DV))

        z = p["sc_c"] * p["sc_h"]
        zcs = []
        for s in range(nseq):
            z_s = z[s * tt:(s + 1) * tt]
            zcs.append(causal_conv(z_s, cm_scr, s, mw_ref, n_mconv))
            cm_out_ref[s] = z_s[tt - (n_mconv - 1):]

        where = [(c, h, lanes) for c in range(nchunk) for h, lanes in enumerate(head_lanes)]
        units = [tuple(arr[c * CHUNK:(c + 1) * CHUNK] for arr in per_head[h])
                 + (gc[c * CHUNK:(c + 1) * CHUNK, h:h + 1], gct_scr[c, h:h + 1, :])
                 for c, h, lanes in where]
        local = yield from _alternate(_gdn_local_steps(units, tril, strict), project_rest(("sc_b", "og")))
        osc_scr[...] = p["sc_b"] * jnp.concatenate(zcs, axis=0)
        og_scr[...] = p["og"]
        for (c, h, lanes), (u_c, w_c, qk_c) in zip(where, local):
            u_scr[c * CHUNK:(c + 1) * CHUNK, lanes] = u_c
            w_scr[c * CHUNK:(c + 1) * CHUNK, lanes] = w_c.astype(BF16)
            qk_scr[c * DN_HEADS + h] = qk_c.astype(BF16)

    def back():
        u_all, w_all, qd_all, kd_all = u_scr[...], w_scr[...], qd_scr[...], kd_scr[...]
        gl_all, og, o_sc = gl_scr[...], og_scr[...], osc_scr[...]
        qks = [qk_scr[i] for i in range(nchunk * DN_HEADS)]
        xs = [xb_ref[s] for s in range(nseq)]
        gates = [modb_ref[s, 3 * sub + 2:3 * sub + 3, :] for s in range(nseq)]
        states = [s_scr[i] * keep_back for i in range(nseq * DN_HEADS)]
        o_rows = [[None] * DN_HEADS for _ in range(nchunk)]
        for j in range(per_seq):
            chains = [(s, s * per_seq + j, h, lanes) for s in range(nseq) for h, lanes in enumerate(head_lanes)]
            rws = [slice(c * CHUNK, (c + 1) * CHUNK) for _, c, _, _ in chains]
            r = [jnp.dot(jnp.concatenate([w_all[rw, lanes], qd_all[rw, lanes]], axis=0),
                         states[s * DN_HEADS + h].astype(BF16), preferred_element_type=F32)
                 for (s, c, h, lanes), rw in zip(chains, rws)]
            yield
            v_new = [(u_all[rw, lanes] - r_u[:CHUNK]).astype(BF16)
                     for (s, c, h, lanes), rw, r_u in zip(chains, rws, r)]
            for (s, c, h, lanes), r_u, v_u in zip(chains, r, v_new):
                o_rows[c][h] = r_u[CHUNK:] + jnp.dot(qks[c * DN_HEADS + h], v_u, preferred_element_type=F32)
            for (s, c, h, lanes), rw, v_u in zip(chains, rws, v_new):
                i = s * DN_HEADS + h
                states[i] = (states[i] * gl_all[c * SUBLANES:c * SUBLANES + 1, lanes]
                             + lax.dot_general(kd_all[rw, lanes], v_u, (((0,), (0,)), ((), ())),
                                               preferred_element_type=F32))
            yield
        for i, st in enumerate(states):
            s_scr[i] = st
        for s in range(nseq):
            for h in range(DN_HEADS):
                s_out_ref[s, h] = states[s * DN_HEADS + h]

        o_dn = jnp.concatenate(
            [_gate_norm(jnp.concatenate([o_rows[c][h] for c in range(nchunk)], axis=0),
                        og[:, lanes], ng_ref[...]) for h, lanes in enumerate(head_lanes)], axis=1)
        dn_w = o_dn.shape[1]
        mix = (jnp.dot(o_dn.astype(BF16), w_out_ref[0:dn_w, :], preferred_element_type=F32)
               + jnp.dot(o_sc.astype(BF16), w_out_ref[dn_w:, :], preferred_element_type=F32))
        yield
        for s in range(nseq):
            y_ref[s] = _post(xs[s], mix[s * tt:(s + 1) * tt], gates[s],
                             _row(lng_ref, sub), _row(lnb_ref, sub), alpha)

    @pl.when(k == 0)
    def _():
        s_scr[...] = jnp.zeros_like(s_scr)
        cq_scr[...] = jnp.zeros_like(cq_scr)
        cm_scr[...] = jnp.zeros_like(cm_scr)
        _run_interleaved(front())

    @pl.when(jnp.logical_and(k > 0, k < n_tiles))
    def _():
        _run_interleaved(back(), front())

    @pl.when(k == n_tiles)
    def _():
        _run_interleaved(back())


def _mixer_prompt(x, mod, w_in, w_out, conv_w, mconv_w, alog, dtb, norm_g, lng, lnb, *, alpha, nseq, tt, off):
    bsz, t, d = x.shape
    assert t % tt == 0 and tt % LANES == 0 and bsz % nseq == 0
    nt = t // tt
    n_tiles = (bsz // nseq) * nt
    rows = nseq * tt
    qkv_dim = conv_w.shape[-1]
    sc_w = off["sc_c"][1] - off["sc_c"][0]
    dn_w = DN_HEADS * DN_DV
    n_conv, n_mconv = conv_w.shape[0], mconv_w.shape[0]
    assert max(n_conv, n_mconv) - 1 <= SUBLANES <= tt
    nchunk = rows // CHUNK
    kern = functools.partial(_mixer_prompt_kernel, alpha=alpha, nseq=nseq, tt=tt, tiles_per_group=nt,
                             n_tiles=n_tiles, off=off, n_conv=n_conv, n_mconv=n_mconv)

    def front_tile(k):
        return jnp.minimum(k, n_tiles - 1)

    def back_tile(k):
        return jnp.maximum(k - 1, 0)

    y, s_out, cq, cm = pl.pallas_call(
        kern,
        grid=(n_tiles + 1,),
        in_specs=[pl.BlockSpec((nseq, tt, d), lambda k: (front_tile(k) // nt, front_tile(k) % nt, 0)),
                  pl.BlockSpec((nseq, 3 * N_SUB, d), lambda k: (front_tile(k) // nt, 0, 0)),
                  pl.BlockSpec((nseq, tt, d), lambda k: (back_tile(k) // nt, back_tile(k) % nt, 0)),
                  pl.BlockSpec((nseq, 3 * N_SUB, d), lambda k: (back_tile(k) // nt, 0, 0)),
                  *[_const_spec(w.shape) for w in w_in], _const_spec(w_out.shape),
                  _const_spec(conv_w.shape), _const_spec(mconv_w.shape),
                  _const_spec(alog.shape), _const_spec(dtb.shape), _const_spec(norm_g.shape),
                  _const_spec(lng.shape), _const_spec(lnb.shape)],
        out_specs=[pl.BlockSpec((nseq, tt, d), lambda k: (back_tile(k) // nt, back_tile(k) % nt, 0)),
                   pl.BlockSpec((nseq, DN_HEADS, DN_DK, DN_DV), lambda k: (back_tile(k) // nt, 0, 0, 0)),
                   pl.BlockSpec((nseq, n_conv - 1, qkv_dim), lambda k: (front_tile(k) // nt, 0, 0)),
                   pl.BlockSpec((nseq, n_mconv - 1, sc_w), lambda k: (front_tile(k) // nt, 0, 0))],
        out_shape=[jax.ShapeDtypeStruct((bsz, t, d), F32),
                   jax.ShapeDtypeStruct((bsz, DN_HEADS, DN_DK, DN_DV), F32),
                   jax.ShapeDtypeStruct((bsz, n_conv - 1, qkv_dim), F32),
                   jax.ShapeDtypeStruct((bsz, n_mconv - 1, sc_w), F32)],
        scratch_shapes=[pltpu.VMEM((nseq * DN_HEADS, DN_DK, DN_DV), F32),
                        pltpu.VMEM((nseq, SUBLANES, qkv_dim), F32),
                        pltpu.VMEM((nseq, SUBLANES, sc_w), F32),
                        pltpu.VMEM((nchunk, SUBLANES, CHUNK), F32),
                        pltpu.VMEM((rows, dn_w), F32),
                        pltpu.VMEM((rows, dn_w), BF16),
                        pltpu.VMEM((nchunk * DN_HEADS, CHUNK, CHUNK), BF16),
                        pltpu.VMEM((rows, dn_w), BF16),
                        pltpu.VMEM((rows, dn_w), BF16),
                        pltpu.VMEM((nchunk * SUBLANES, dn_w), F32),
                        pltpu.VMEM((rows, dn_w), F32),
                        pltpu.VMEM((rows, sc_w), F32)],
        compiler_params=pltpu.CompilerParams(dimension_semantics=("arbitrary",),
                                             vmem_limit_bytes=V7X_VMEM_LIMIT),
        name="mixer_prompt",
    )(x, mod, x, mod, *w_in, w_out, conv_w, mconv_w, alog, dtb, norm_g, lng, lnb)
    return y, s_out, cq, cm


_PACK = SUBLANES


def _sample_phase_a(x_ref, mod_ref, wq_ref, wr_ref, wab_ref, cw_ref, mw_ref, alog_ref, dtb_ref, cq_in_ref, cm_in_ref,
                    cq_out_ref, cm_out_ref, og_scr, osc_scr, lhs_scr, u_scr, kd_scr, gl_scr, qk_scr,
                    *, t_len, d, off, n_conv, n_mconv):
    bsz = x_ref.shape[0]
    hk = DN_HEADS * DN_DK
    sc_w = off["sc_c"][1] - off["sc_c"][0]

    x = _stack_time(x_ref, t_len, d)
    u = (x * (1.0 + _mod_rows(mod_ref, 1, t_len, d)) + _mod_rows(mod_ref, 0, t_len, d)).astype(BF16)
    pq = jnp.dot(u, wq_ref[...], preferred_element_type=F32)
    ab = jnp.dot(u, wab_ref[...], preferred_element_type=F32)
    rest = jnp.dot(u, wr_ref[...], preferred_element_type=F32)
    p = {name: rest[:, lo:hi] for name, (lo, hi) in off.items()}

    def rows(arr, t):
        return arr[t * bsz:(t + 1) * bsz]

    seq = [cq_in_ref[j] for j in range(n_conv - 1)] + [rows(pq, t) for t in range(t_len)]
    act = []
    for t in range(t_len):
        acc = None
        for j in range(n_conv):
            term = seq[t + j] * cw_ref[j:j + 1, :]
            acc = term if acc is None else acc + term
        act.append(_silu(acc))
    for j in range(n_conv - 1):
        cq_out_ref[j] = seq[t_len + j]

    z = p["sc_c"] * p["sc_h"]
    zseq = ([cm_in_ref[:, j, :] for j in range(n_mconv - 1)]
            + [rows(z, t) for t in range(t_len)])
    zc = []
    for t in range(t_len):
        acc = None
        for j in range(n_mconv):
            term = zseq[t + j] * _row(mw_ref, j)
            acc = term if acc is None else acc + term
        zc.append(acc)
    for j in range(n_mconv - 1):
        cm_out_ref[:, j, :] = zseq[t_len + j]
    osc_scr[...] = p["sc_b"] * jnp.concatenate(zc, axis=0)
    og_scr[...] = p["og"]

    beta_all = jax.nn.sigmoid(ab)
    g_all = -jnp.exp(alog_ref[...]) * _softplus(ab + dtb_ref[...])
    zero = jnp.zeros((bsz, LANES), F32)

    for h in range(DN_HEADS):
        q = [_l2norm(a[:, h * DN_DK:(h + 1) * DN_DK]) * (DN_DK ** -0.5) for a in act]
        k = [_l2norm(a[:, hk + h * DN_DK:hk + (h + 1) * DN_DK]) for a in act]
        v = [a[:, 2 * hk + h * DN_DV:2 * hk + (h + 1) * DN_DV] for a in act]
        beta = [rows(beta_all, t)[:, DN_HEADS + h:DN_HEADS + h + 1] for t in range(t_len)]
        g = [rows(g_all, t)[:, h:h + 1] for t in range(t_len)]
        gc = [g[0]]
        for t in range(1, t_len):
            gc.append(gc[-1] + g[t])
        kb = [k[t] * beta[t] for t in range(t_len)]
        vb = [v[t] * beta[t] for t in range(t_len)]
        eg = [jnp.exp(gc[t]) for t in range(t_len)]
        kbe = [kb[t] * eg[t] for t in range(t_len)]
        lm = [[None] * t_len for _ in range(t_len)]
        n_qk = 0
        for i in range(t_len):
            for j in range(i + 1):
                dec = jnp.exp(gc[i] - gc[j])
                qk_scr[h, n_qk] = jnp.broadcast_to(_rowsum(q[i] * k[j]) * dec, (bsz, LANES))
                n_qk += 1
                if j < i:
                    lm[i][j] = _rowsum(kb[i] * k[j]) * dec
        tm = [[None] * t_len for _ in range(t_len)]
        for i in range(t_len):
            for j in range(i):
                acc = lm[i][j]
                for m in range(j + 1, i):
                    acc = acc + lm[i][m] * tm[m][j]
                tm[i][j] = -acc
        for i in range(t_len):
            ui, wi = vb[i], kbe[i]
            for j in range(i):
                ui = ui + tm[i][j] * vb[j]
                wi = wi + tm[i][j] * kbe[j]
            lhs_scr[h, pl.ds(i, bsz, stride=_PACK), :] = wi
            lhs_scr[h, pl.ds(t_len + i, bsz, stride=_PACK), :] = q[i] * eg[i]
            u_scr[h, pl.ds(i, bsz, stride=_PACK), :] = ui
            u_scr[h, pl.ds(t_len + i, bsz, stride=_PACK), :] = zero
            kd_scr[h, pl.ds(i, bsz, stride=_PACK), :] = k[i] * jnp.exp(gc[t_len - 1] - gc[i])
            kd_scr[h, pl.ds(t_len + i, bsz, stride=_PACK), :] = zero
        gl_scr[h] = jnp.broadcast_to(jnp.exp(gc[t_len - 1]), (bsz, LANES))


def _sample_phase_c(x_ref, mod_ref, w_out_ref, ng_ref, lng_ref, lnb_ref, y_ref,
                    og_scr, osc_scr, res_scr, vn_scr, qk_scr, *, alpha, t_len, d):
    bsz = x_ref.shape[0]
    sub = 1
    x = _stack_time(x_ref, t_len, d)
    gate = _mod_rows(mod_ref, 2, t_len, d)
    og = og_scr[...]
    per_t = []
    for i in range(t_len):
        heads = []
        for h in range(DN_HEADS):
            oi = res_scr[h, pl.ds(t_len + i, bsz, stride=_PACK), :]
            base = i * (i + 1) // 2
            for j in range(i + 1):
                oi = oi + qk_scr[h, base + j] * vn_scr[h, pl.ds(j, bsz, stride=_PACK), :]
            heads.append(_gate_norm(oi, og[i * bsz:(i + 1) * bsz, h * DN_DV:(h + 1) * DN_DV], ng_ref[...]))
        per_t.append(jnp.concatenate(heads, axis=1))
    o_dn = jnp.concatenate(per_t, axis=0)
    dn_w = o_dn.shape[1]
    mix = (jnp.dot(o_dn.astype(BF16), w_out_ref[0:dn_w, :], preferred_element_type=F32)
           + jnp.dot(osc_scr[...].astype(BF16), w_out_ref[dn_w:, :], preferred_element_type=F32))
    y = _post(x, mix, gate, _row(lng_ref, sub), _row(lnb_ref, sub), alpha)
    for t in range(t_len):
        y_ref[:, t * d:(t + 1) * d] = y[t * bsz:(t + 1) * bsz]


def _mixer_sample_kernel(x_ref, mod_ref, wq_ref, wr_ref, wab_ref, w_out_ref, cw_ref, mw_ref, alog_ref, dtb_ref, ng_ref,
                         lng_ref, lnb_ref, cq_in_ref, cm_in_ref, s_in_ref,
                         y_ref, cq_out_ref, cm_out_ref, s_out_ref,
                         og_scr, osc_scr, lhs_scr, u_scr, kd_scr, gl_scr, qk_scr, res_scr, vn_scr,
                         *, alpha, t_len, d, off, n_conv, n_mconv, bt):
    step = pl.program_id(0)
    assert 2 * t_len == _PACK, "packed buffers hold [T rows | T rows] per sequence"

    @pl.when(step == 0)
    def _():
        _sample_phase_a(x_ref, mod_ref, wq_ref, wr_ref, wab_ref, cw_ref, mw_ref, alog_ref, dtb_ref, cq_in_ref, cm_in_ref,
                        cq_out_ref, cm_out_ref, og_scr, osc_scr, lhs_scr, u_scr, kd_scr, gl_scr, qk_scr,
                        t_len=t_len, d=d, off=off, n_conv=n_conv, n_mconv=n_mconv)

    top = (lax.broadcasted_iota(jnp.int32, (_PACK, LANES), 0) < t_len).astype(F32)

    chains = [(bl, h) for bl in range(bt) for h in range(DN_HEADS)]
    seq0 = step * bt
    row0 = [pl.multiple_of((seq0 + bl) * _PACK, _PACK) for bl in range(bt)]
    states = [s_in_ref[bl, h] for bl, h in chains]
    r = [_mm(lhs_scr[h, pl.ds(row0[bl], _PACK), :], s) for (bl, h), s in zip(chains, states)]
    vns = []
    for (bl, h), r_u in zip(chains, r):
        res_scr[h, pl.ds(row0[bl], _PACK), :] = r_u
        vn = (u_scr[h, pl.ds(row0[bl], _PACK), :] - r_u) * top
        vn_scr[h, pl.ds(row0[bl], _PACK), :] = vn
        vns.append(vn)
    upd = [_mm_at(kd_scr[h, pl.ds(row0[bl], _PACK), :], vn) for (bl, h), vn in zip(chains, vns)]
    for (bl, h), s, s_add in zip(chains, states, upd):
        s_out_ref[bl, h] = s * gl_scr[h, pl.ds(seq0 + bl, 1), :] + s_add

    @pl.when(step == pl.num_programs(0) - 1)
    def _():
        _sample_phase_c(x_ref, mod_ref, w_out_ref, ng_ref, lng_ref, lnb_ref, y_ref,
                        og_scr, osc_scr, res_scr, vn_scr, qk_scr, alpha=alpha, t_len=t_len, d=d)


def _mixer_sample(x, mod, w_in, w_out, conv_w, mconv_w, alog, dtb, norm_g, lng, lnb, s_in, cq_in, cm_in,
                  *, alpha, off, bt, t):
    bsz, d = x.shape[0], x.shape[1] // t
    assert bsz % bt == 0
    sc_w = off["sc_c"][1] - off["sc_c"][0]
    dn_w = DN_HEADS * DN_DV
    n_conv, n_mconv = conv_w.shape[0], mconv_w.shape[0]
    n_qk = t * (t + 1) // 2
    x2 = x
    cq2 = jnp.swapaxes(cq_in, 0, 1)
    kern = functools.partial(_mixer_sample_kernel, alpha=alpha, t_len=t, d=d, off=off,
                             n_conv=n_conv, n_mconv=n_mconv, bt=bt)
    consts = (*w_in, w_out, conv_w, mconv_w, alog, dtb, norm_g, lng, lnb, cq2, cm_in)
    state_spec = pl.BlockSpec((bt, DN_HEADS, DN_DK, DN_DV), lambda i: (i, 0, 0, 0))
    packed = pltpu.VMEM((DN_HEADS, _PACK * bsz, LANES), F32)
    y, cq, cm, s_out = pl.pallas_call(
        kern,
        grid=(bsz // bt,),
        in_specs=([_const_spec(x2.shape), _sub_mod_spec(bsz, d, 1)]
                  + [_const_spec(a.shape) for a in consts] + [state_spec]),
        out_specs=[pl.BlockSpec(x2.shape, lambda i: (0, 0)),
                   pl.BlockSpec(cq2.shape, lambda i: (0, 0, 0)),
                   pl.BlockSpec(cm_in.shape, lambda i: (0, 0, 0)),
                   state_spec],
        out_shape=[jax.ShapeDtypeStruct(x2.shape, F32),
                   jax.ShapeDtypeStruct(cq2.shape, F32),
                   jax.ShapeDtypeStruct(cm_in.shape, F32),
                   jax.ShapeDtypeStruct(s_in.shape, F32)],
        scratch_shapes=[pltpu.VMEM((t * bsz, dn_w), F32),
                        pltpu.VMEM((t * bsz, sc_w), F32),
                        packed,
                        packed,
                        packed,
                        pltpu.VMEM((DN_HEADS, bsz, LANES), F32),
                        pltpu.VMEM((DN_HEADS, n_qk, bsz, LANES), F32),
                        packed,
                        packed],
        compiler_params=pltpu.CompilerParams(dimension_semantics=("arbitrary",),
                                             vmem_limit_bytes=V7X_VMEM_LIMIT),
        name="mixer_sample",
    )(x2, mod, *consts, s_in)
    return y, s_out, jnp.swapaxes(cq, 0, 1), cm


def _pad_lanes(v):
    return jnp.pad(v.reshape(1, -1), ((0, 0), (0, AB_PAD - v.shape[0])))


def kernel(x_prompt, x_sample, state_ssm, state_conv_qkv, state_conv_mix, c_prompt, c_sample, w_ada, b_ada, ln_g, ln_b, ffn1_wg, ffn1_wu, ffn1_wd, ffn2_wg, ffn2_wu, ffn2_wd, w_in, conv_qkv_w, a_log, dt_bias, dn_norm_g, conv_mix_w, w_out):
    depth = w_ada.shape[0]
    alpha = (2 * depth) ** 0.25
    bp, tp, d = x_prompt.shape
    qkv_dim = conv_qkv_w.shape[-1]
    sc_width = conv_mix_w.shape[-1]
    dn_width = DN_HEADS * DN_DV
    off = _proj_layout(dn_width, sc_width)
    tm = min(512, tp)
    nseq = 2 if bp % 2 == 0 else 1
    tt = min(512 // nseq, tp)
    bt = min(16, x_sample.shape[0])
    ts = x_sample.shape[1]

    hp, hs = x_prompt, x_sample
    outs = [[] for _ in range(6)]
    for l in range(depth):
        mod_p, mod_s, *w_in_p = _adaln(c_prompt, c_sample, w_ada[l], b_ada[l].reshape(1, -1),
                                       jnp.swapaxes(w_in[l], 0, 1), qkv_dim)
        mod_p = mod_p.reshape(bp, 3 * N_SUB, d)
        lng, lnb, mconv_w = ln_g[l][:, None, :], ln_b[l][:, None, :], conv_mix_w[l][:, None, :]
        w1 = (ffn1_wg[l], ffn1_wu[l], ffn1_wd[l])
        w2 = (ffn2_wg[l], ffn2_wu[l], ffn2_wd[l])
        w_out_b = w_out[l].astype(BF16)
        alog, dtb = _pad_lanes(a_log[l]), _pad_lanes(dt_bias[l])
        norm_g = dn_norm_g[l].reshape(1, -1)
        mix_args = (w_in_p, w_out_b, conv_qkv_w[l], mconv_w, alog, dtb, norm_g, lng, lnb)

        hp, hs = _ffn(hp, hs, mod_p, mod_s, *w1, lng, lnb, sub=0, alpha=alpha, tm=tm, ts=ts,
                      ys_3d=False)
        hp, a1, a2, a3 = _mixer_prompt(hp, mod_p, *mix_args, alpha=alpha, nseq=nseq, tt=tt, off=off)
        hs, b1, b2, b3 = _mixer_sample(hs, mod_s, *mix_args, state_ssm[l], state_conv_qkv[l],
                                       state_conv_mix[l], alpha=alpha, off=off, bt=bt, t=ts)
        hp, hs = _ffn(hp, hs, mod_p, mod_s, *w2, lng, lnb, sub=2, alpha=alpha, tm=tm, ts=ts,
                      ys_3d=True)
        for lst, val in zip(outs, (a1, a2, a3, b1, b2, b3)):
            lst.append(val)
    return (hp, hs) + tuple(jnp.stack(o) for o in outs)
```

```python
import functools

import jax
import jax.numpy as jnp
from jax import lax
from jax.experimental import pallas as pl
from jax.experimental.pallas import tpu as pltpu

F32 = jnp.float32
BF16 = jnp.bfloat16

LN_EPS = 1e-5
RMS_EPS = 1e-6
N_SUB = 3
DN_HEADS = 4
DN_DK = 128
DN_DV = 128
CHUNK = 64
LANES = 128
SUBLANES = 8
AB_PAD = LANES

V7X_VMEM_LIMIT = 60000 * 1024
CAST_BUFFERS = 3
FFN_CHUNK = 256
STATE_SLOTS = 4


def _mm(a, b):
    return jnp.dot(a.astype(BF16), b.astype(BF16), preferred_element_type=F32)


def _mm_at(a, b):
    return lax.dot_general(a.astype(BF16), b.astype(BF16), (((0,), (0,)), ((), ())),
                           preferred_element_type=F32)


def _mm_f32(a, b):
    return jnp.dot(a, b, preferred_element_type=F32, precision=lax.Precision.HIGHEST)


def _silu(x):
    return x * jax.nn.sigmoid(x)


def _softplus(x):
    return jnp.maximum(x, 0.0) + jnp.log1p(jnp.exp(-jnp.abs(x)))


def _layer_norm(y, g, b):
    mu = jnp.mean(y, axis=-1, keepdims=True)
    yc = y - mu
    var = jnp.mean(yc * yc, axis=-1, keepdims=True)
    return yc * lax.rsqrt(var + LN_EPS) * g + b


def _post(x, delta, gate, g, b, alpha):
    return _layer_norm(alpha * x + gate * delta, g, b)


def _rowsum(x):
    return jnp.sum(x, axis=-1, keepdims=True)


def _l2norm(x):
    return x * lax.rsqrt(_rowsum(x * x) + RMS_EPS)


def _const_spec(shape):
    nd = len(shape)
    return pl.BlockSpec(shape, lambda *_: (0,) * nd, pipeline_mode=pl.Buffered(1))


def _row(ref, j):
    return ref[j] if len(ref.shape) == 3 else ref[j:j + 1, :]


def _adaln_kernel(cp_ref, cs_ref, w_ref, b_ref, win_ref, op_ref, os_ref, wq_ref, wr_ref, wab_ref, *, qkv_dim):
    mp = cp_ref.shape[0]
    c = jnp.concatenate([cp_ref[...], cs_ref[...]], axis=0)
    mod = _mm(_silu(c), w_ref[...]) + b_ref[...]
    op_ref[...] = mod[:mp]
    os_ref[...] = mod[mp:]

    gates = 2 * DN_HEADS
    wq_ref[...] = win_ref[0:qkv_dim, :].T.astype(BF16)
    wr_ref[...] = win_ref[qkv_dim + gates:, :].T.astype(BF16)
    ab_rows = jnp.concatenate([win_ref[qkv_dim:qkv_dim + gates, :],
                               jnp.zeros((AB_PAD - gates, win_ref.shape[1]), F32)], axis=0)
    wab_ref[...] = ab_rows.T.astype(BF16)


def _adaln(c_p, c_s, w, b, w_in_t, qkv_dim):
    (mp, d), ms = c_p.shape, c_s.shape[0]
    n = w.shape[1]
    proj = w_in_t.shape[0]
    rest = proj - qkv_dim - 2 * DN_HEADS
    assert qkv_dim % LANES == 0 and rest % LANES == 0 and (2 * DN_HEADS) % SUBLANES == 0
    steps = max(g for g in (1, 2, 4) if n % (g * LANES) == 0 and d % (g * LANES) == 0)
    tn, rows = n // steps, d // steps
    return pl.pallas_call(
        functools.partial(_adaln_kernel, qkv_dim=qkv_dim),
        grid=(steps,),
        in_specs=[_const_spec((mp, d)), _const_spec((ms, d)),
                  pl.BlockSpec((d, tn), lambda j: (0, j)),
                  pl.BlockSpec((1, tn), lambda j: (0, j)),
                  pl.BlockSpec((proj, rows), lambda j: (0, j))],
        out_specs=[pl.BlockSpec((mp, tn), lambda j: (0, j)), pl.BlockSpec((ms, tn), lambda j: (0, j)),
                   pl.BlockSpec((rows, qkv_dim), lambda j: (j, 0)),
                   pl.BlockSpec((rows, rest), lambda j: (j, 0)),
                   pl.BlockSpec((rows, AB_PAD), lambda j: (j, 0))],
        out_shape=[jax.ShapeDtypeStruct((mp, n), F32), jax.ShapeDtypeStruct((ms, n), F32),
                   jax.ShapeDtypeStruct((d, qkv_dim), BF16), jax.ShapeDtypeStruct((d, rest), BF16),
                   jax.ShapeDtypeStruct((d, AB_PAD), BF16)],
        compiler_params=pltpu.CompilerParams(dimension_semantics=("arbitrary",),
                                             vmem_limit_bytes=V7X_VMEM_LIMIT),
        name="adaln",
    )(c_p, c_s, w, b, w_in_t)


def _swiglu(x, shift, scale, wg_ref, wu_ref, wd_ref):
    u = (x * (1.0 + scale) + shift).astype(BF16)
    hg = jnp.dot(u, wg_ref[...], preferred_element_type=F32)
    hu = jnp.dot(u, wu_ref[...], preferred_element_type=F32)
    h = (_silu(hg) * hu).astype(BF16)
    return jnp.dot(h, wd_ref[...], preferred_element_type=F32)


def _stack_time(ref, t_len, d):
    if len(ref.shape) == 3:
        return jnp.concatenate([ref[:, t, :] for t in range(t_len)], axis=0)
    return jnp.concatenate([ref[:, t * d:(t + 1) * d] for t in range(t_len)], axis=0)


def _mod_rows(mod_ref, idx, t_len, d):
    m = mod_ref[:, idx * d:(idx + 1) * d]
    return jnp.concatenate([m] * t_len, axis=0)


def _sub_mod_spec(bsz, d, sub):
    return pl.BlockSpec((bsz, 3 * d), lambda *_: (0, sub), pipeline_mode=pl.Buffered(1))


def _first_tile_swiglu(x, shift, scale, w_hbm, w_scr, stages, sems):
    wg_hbm, wu_hbm, wd_hbm = w_hbm
    wg_scr, wu_scr, wd_scr = w_scr
    nbuf = stages[0].shape[0]
    cw = stages[0].shape[2]
    n = wg_hbm.shape[1] // cw

    def copies(j):
        slot = j % nbuf
        cols = pl.ds(j * cw, cw)
        return (pltpu.make_async_copy(wg_hbm.at[:, cols], stages[0].at[slot], sems[0].at[slot]),
                pltpu.make_async_copy(wu_hbm.at[:, cols], stages[1].at[slot], sems[1].at[slot]),
                pltpu.make_async_copy(wd_hbm.at[cols, :], stages[2].at[slot], sems[2].at[slot]))

    queue = [copies(j) for j in range(n)]
    ahead = nbuf - 1
    for group in queue[:ahead]:
        for copy in group:
            copy.start()
    u = (x * (1.0 + scale) + shift).astype(BF16)
    acc = None
    for j, group in enumerate(queue):
        if j + ahead < n:
            for copy in queue[j + ahead]:
                copy.start()
        for copy in group:
            copy.wait()
        slot, cols = j % nbuf, slice(j * cw, (j + 1) * cw)
        wg_scr[:, cols] = stages[0][slot].astype(BF16)
        wu_scr[:, cols] = stages[1][slot].astype(BF16)
        wd_scr[cols, :] = stages[2][slot].astype(BF16)
        hg = jnp.dot(u, wg_scr[:, cols], preferred_element_type=F32)
        hu = jnp.dot(u, wu_scr[:, cols], preferred_element_type=F32)
        part = jnp.dot((_silu(hg) * hu).astype(BF16), wd_scr[cols, :], preferred_element_type=F32)
        acc = part if acc is None else acc + part
    return acc


def _ffn_kernel(xp_ref, modp_ref, xq_ref, modq_ref, xs_ref, mods_ref, wg_hbm, wu_hbm, wd_hbm, lng_ref, lnb_ref,
                yp_ref, ys_ref, wg_scr, wu_scr, wd_scr, d_scr, stage_g, stage_u, stage_d, sem_g, sem_u, sem_d,
                *, sub, alpha, n_prompt, t_len, d):
    k = pl.program_id(0)
    g, b = _row(lng_ref, sub), _row(lnb_ref, sub)

    def matmuls():
        return _swiglu(xp_ref[...], modp_ref[0, 3 * sub + 0:3 * sub + 1, :], modp_ref[0, 3 * sub + 1:3 * sub + 2, :],
                       wg_scr, wu_scr, wd_scr)

    def finish(d_prev):
        yp_ref[...] = _post(xq_ref[...], 0.5 * d_prev, modq_ref[0, 3 * sub + 2:3 * sub + 3, :], g, b, alpha)

    @pl.when(k == 0)
    def _():
        d_scr[...] = _first_tile_swiglu(
            xp_ref[...], modp_ref[0, 3 * sub + 0:3 * sub + 1, :], modp_ref[0, 3 * sub + 1:3 * sub + 2, :],
            (wg_hbm, wu_hbm, wd_hbm), (wg_scr, wu_scr, wd_scr), (stage_g, stage_u, stage_d),
            (sem_g, sem_u, sem_d))

    @pl.when(jnp.logical_and(k > 0, k < n_prompt))
    def _():
        d_prev = d_scr[...]
        d_new = matmuls()
        finish(d_prev)
        d_scr[...] = d_new

    @pl.when(k == n_prompt)
    def _():
        finish(d_scr[...])
        bsz = xs_ref.shape[0]
        x = _stack_time(xs_ref, t_len, d)
        delta = _swiglu(x, _mod_rows(mods_ref, 0, t_len, d), _mod_rows(mods_ref, 1, t_len, d),
                        wg_scr, wu_scr, wd_scr)
        y = _post(x, 0.5 * delta, _mod_rows(mods_ref, 2, t_len, d), g, b, alpha)
        for t in range(t_len):
            if len(ys_ref.shape) == 3:
                ys_ref[:, t, :] = y[t * bsz:(t + 1) * bsz]
            else:
                ys_ref[:, t * d:(t + 1) * d] = y[t * bsz:(t + 1) * bsz]


def _ffn(xp, xs, mod_p, mod_s, wg, wu, wd, lng, lnb, *, sub, alpha, tm, ts, ys_3d):
    bp, tp, d = xp.shape
    bs = xs.shape[0]
    ys_shape = (bs, ts, d) if ys_3d else (bs, ts * d)
    f = wg.shape[1]
    assert tp % tm == 0
    per_seq = tp // tm
    n_prompt = bp * per_seq
    assert f % FFN_CHUNK == 0
    last = n_prompt - 1
    yp, ys = pl.pallas_call(
        functools.partial(_ffn_kernel, sub=sub, alpha=alpha, n_prompt=n_prompt, t_len=ts, d=d),
        grid=(n_prompt + 1,),
        in_specs=[pl.BlockSpec((tm, d), lambda k: (jnp.minimum(k, last), 0)),
                  pl.BlockSpec((1, 3 * N_SUB, d), lambda k: (jnp.minimum(k, last) // per_seq, 0, 0)),
                  pl.BlockSpec((tm, d), lambda k: (jnp.maximum(k - 1, 0), 0)),
                  pl.BlockSpec((1, 3 * N_SUB, d), lambda k: (jnp.maximum(k - 1, 0) // per_seq, 0, 0)),
                  _const_spec(xs.shape), _sub_mod_spec(bs, d, sub),
                  pl.BlockSpec(memory_space=pl.ANY), pl.BlockSpec(memory_space=pl.ANY),
                  pl.BlockSpec(memory_space=pl.ANY),
                  _const_spec(lng.shape), _const_spec(lnb.shape)],
        out_specs=[pl.BlockSpec((tm, d), lambda k: (jnp.maximum(k - 1, 0), 0)),
                   pl.BlockSpec(ys_shape, lambda k: (0,) * len(ys_shape))],
        out_shape=[jax.ShapeDtypeStruct((bp * tp, d), F32),
                   jax.ShapeDtypeStruct(ys_shape, F32)],
        scratch_shapes=[pltpu.VMEM((d, f), BF16), pltpu.VMEM((d, f), BF16), pltpu.VMEM((f, d), BF16),
                        pltpu.VMEM((tm, d), F32),
                        pltpu.VMEM((CAST_BUFFERS, d, FFN_CHUNK), F32), pltpu.VMEM((CAST_BUFFERS, d, FFN_CHUNK), F32),
                        pltpu.VMEM((CAST_BUFFERS, FFN_CHUNK, d), F32),
                        pltpu.SemaphoreType.DMA((CAST_BUFFERS,)), pltpu.SemaphoreType.DMA((CAST_BUFFERS,)),
                        pltpu.SemaphoreType.DMA((CAST_BUFFERS,))],
        compiler_params=pltpu.CompilerParams(dimension_semantics=("arbitrary",),
                                             vmem_limit_bytes=V7X_VMEM_LIMIT),
        name=f"ffn{sub}",
    )(xp.reshape(bp * tp, d), mod_p, xp.reshape(bp * tp, d), mod_p, xs, mod_s,
      wg, wu, wd, lng, lnb)
    return yp.reshape(bp, tp, d), ys


def _tri_inverse_steps(lmats, c):
    ls = [l.astype(BF16) for l in lmats]
    ms = [jnp.dot(l, l, preferred_element_type=F32) for l in ls]
    yield
    rows = lax.broadcasted_iota(jnp.int32, (c, c), 0)
    cols = lax.broadcasted_iota(jnp.int32, (c, c), 1)
    eye = (rows == cols).astype(F32)
    ps = [eye - l for l in lmats]
    power = 2
    while 2 * power < c:
        mp = [jnp.concatenate([m, p], axis=0).astype(BF16) for m, p in zip(ms, ps)]
        sq = [jnp.dot(x, x[:c], preferred_element_type=F32) for x in mp]
        yield
        ms = [x[:c] for x in sq]
        ps = [p + x[c:] for p, x in zip(ps, sq)]
        power *= 2
    pm = [jnp.dot(p.astype(BF16), m.astype(BF16), preferred_element_type=F32) for p, m in zip(ps, ms)]
    yield
    return [p + x for p, x in zip(ps, pm)]


def _gdn_local_steps(units, tril, strict):
    c = units[0][1].shape[0]
    a = [lax.dot_general(jnp.concatenate([kb, q], axis=0), k, (((1,), (1,)), ((), ())),
                         preferred_element_type=F32)
         for kb, q, k, _, _, _, _ in units]
    yield
    lmats, qks = [], []
    for a_u, (_, _, _, _, _, gcol, grow) in zip(a, units):
        decay = jnp.where(tril, jnp.exp(jnp.where(tril, gcol - grow, 0.0)), 0.0)
        lmats.append(jnp.where(strict, a_u[:c] * decay, 0.0))
        qks.append(jnp.where(tril, a_u[c:] * decay, 0.0))
    tms = yield from _tri_inverse_steps(lmats, c)
    uw = [jnp.dot(tm.astype(BF16), jnp.concatenate([vb, kbe], axis=1), preferred_element_type=F32)
          for tm, (_, _, _, vb, kbe, _, _) in zip(tms, units)]
    yield
    return [(x[:, :DN_DV], x[:, DN_DV:], qk) for x, qk in zip(uw, qks)]


def _alternate(primary, secondary):
    result = None
    done = False
    while not done:
        try:
            next(primary)
            yield
        except StopIteration as stop:
            result, done = stop.value, True
        if secondary is not None:
            try:
                next(secondary)
                yield
            except StopIteration:
                secondary = None
    if secondary is not None:
        yield from secondary
    return result


def _run_interleaved(*gens):
    live = list(gens)
    while live:
        for g in list(live):
            try:
                next(g)
            except StopIteration:
                live.remove(g)


def _gate_norm(o, og, norm_g):
    o = o * lax.rsqrt(jnp.mean(o * o, axis=-1, keepdims=True) + RMS_EPS) * norm_g
    return o * _silu(og)


def _proj_layout(dn_width, sc_width):
    off = {}
    pos = 0
    for name, width in (("og", dn_width), ("sc_b", sc_width), ("sc_c", sc_width), ("sc_h", sc_width)):
        off[name] = (pos, pos + width)
        pos += width
    return off


def _mixer_prompt_kernel(xf_ref, modf_ref, xb_ref, modb_ref, wq_ref, wr_ref, wab_ref, w_out_ref, cw_ref,
                         mw_ref, alog_ref, dtb_ref, ng_ref, lng_ref, lnb_ref,
                         y_ref, s_out_ref, cq_out_ref, cm_out_ref,
                         s_scr, cq_scr, cm_scr, gct_scr,
                         u_scr, w_scr, qk_scr, qd_scr, kd_scr, gl_scr, og_scr, osc_scr,
                         *, alpha, nseq, tt, tiles_per_group, n_tiles, off, n_conv, n_mconv):
    k = pl.program_id(0)
    rows_all = nseq * tt
    nchunk = rows_all // CHUNK
    per_seq = tt // CHUNK
    hk = DN_HEADS * DN_DK
    sub = 1
    head_lanes = [slice(h * DN_DK, (h + 1) * DN_DK) for h in range(DN_HEADS)]
    t_front = lax.rem(jnp.minimum(k, n_tiles - 1), tiles_per_group)
    t_back = lax.rem(jnp.maximum(k - 1, 0), tiles_per_group)
    keep_front = jnp.where(t_front == 0, 0.0, 1.0).astype(F32)
    keep_back = jnp.where(t_back == 0, 0.0, 1.0).astype(F32)

    def causal_conv(new_rows, hist_scr, s, w_ref, taps):
        ext = jnp.concatenate([hist_scr[s] * keep_front, new_rows], axis=0)
        acc = new_rows * _row(w_ref, taps - 1)
        for j in range(taps - 1):
            acc = acc + pltpu.roll(ext, taps - 1 - j, axis=0)[SUBLANES:] * _row(w_ref, j)
        hist_scr[s] = new_rows[tt - SUBLANES:]
        return acc

    def front():
        u = jnp.concatenate(
            [(xf_ref[s] * (1.0 + modf_ref[s, 3 * sub + 1:3 * sub + 2, :])
              + modf_ref[s, 3 * sub + 0:3 * sub + 1, :]).astype(BF16) for s in range(nseq)], axis=0)
        ab = jnp.dot(u, wab_ref[...], preferred_element_type=F32)
        pq = jnp.dot(u, wq_ref[...], preferred_element_type=F32)
        yield

        beta_all = jax.nn.sigmoid(ab)
        g = -jnp.exp(alog_ref[...]) * _softplus(ab + dtb_ref[...])
        rows = lax.broadcasted_iota(jnp.int32, (CHUNK, CHUNK), 0)
        cols = lax.broadcasted_iota(jnp.int32, (CHUNK, CHUNK), 1)
        tril = rows >= cols
        strict = rows > cols
        tril_f = tril.astype(F32)
        gc_parts, gl_parts = [], []
        for c in range(nchunk):
            gc_c = _mm_f32(tril_f, g[c * CHUNK:(c + 1) * CHUNK, :])
            gc_parts.append(gc_c)
            gl_parts.append(jnp.broadcast_to(gc_c[CHUNK - 1:CHUNK, :], (CHUNK, AB_PAD)))
        yield
        p = {}

        def project_rest(names):
            for name in names:
                lo, hi = off[name]
                p[name] = jnp.dot(u, wr_ref[:, lo:hi], preferred_element_type=F32)
                yield

        yield from project_rest(("sc_c", "sc_h"))
        gc = jnp.concatenate(gc_parts, axis=0)
        gl_rows = jnp.concatenate(gl_parts, axis=0)
        gc_t = gc.T
        for c in range(nchunk):
            gct_scr[c] = gc_t[0:SUBLANES, c * CHUNK:(c + 1) * CHUNK]

        acts = []
        for s in range(nseq):
            pq_s = pq[s * tt:(s + 1) * tt]
            acts.append(_silu(causal_conv(pq_s, cq_scr, s, cw_ref, n_conv)))
            cq_out_ref[s] = pq_s[tt - (n_conv - 1):]
        act = jnp.concatenate(acts, axis=0)

        eg_all = jnp.exp(gc)
        ekd_all = jnp.exp(gl_rows - gc)
        egl_all = jnp.exp(gl_rows)
        per_head = []
        for h, lanes in enumerate(head_lanes):
            q = _l2norm(act[:, h * DN_DK:(h + 1) * DN_DK]) * (DN_DK ** -0.5)
            kk = _l2norm(act[:, hk + h * DN_DK:hk + (h + 1) * DN_DK])
            v = act[:, 2 * hk + h * DN_DV:2 * hk + (h + 1) * DN_DV]
            beta = beta_all[:, DN_HEADS + h:DN_HEADS + h + 1]
            eg = eg_all[:, h:h + 1]
            kb = kk * beta
            per_head.append((kb.astype(BF16), q.astype(BF16), kk.astype(BF16), (v * beta).astype(BF16),
                             (kb * eg).astype(BF16)))
            qd_scr[:, lanes] = (q * eg).astype(BF16)
            kd_scr[:, lanes] = (kk * ekd_all[:, h:h + 1]).astype(BF16)
            for c in range(nchunk):
                gl_scr[c * SUBLANES:(c + 1) * SUBLANES, lanes] = jnp.broadcast_to(
                    egl_all[c * CHUNK:c * CHUNK + SUBLANES, h:h + 1], (SUBLANES, DN_DV))

        z = p["sc_c"] * p["sc_h"]
        zcs = []
        for s in range(nseq):
            z_s = z[s * tt:(s + 1) * tt]
            zcs.append(causal_conv(z_s, cm_scr, s, mw_ref, n_mconv))
            cm_out_ref[s] = z_s[tt - (n_mconv - 1):]

        where = [(c, h, lanes) for c in range(nchunk) for h, lanes in enumerate(head_lanes)]
        units = [tuple(arr[c * CHUNK:(c + 1) * CHUNK] for arr in per_head[h])
                 + (gc[c * CHUNK:(c + 1) * CHUNK, h:h + 1], gct_scr[c, h:h + 1, :])
                 for c, h, lanes in where]
        local = yield from _alternate(_gdn_local_steps(units, tril, strict), project_rest(("sc_b", "og")))
        osc_scr[...] = p["sc_b"] * jnp.concatenate(zcs, axis=0)
        og_scr[...] = p["og"]
        for (c, h, lanes), (u_c, w_c, qk_c) in zip(where, local):
            u_scr[c * CHUNK:(c + 1) * CHUNK, lanes] = u_c
            w_scr[c * CHUNK:(c + 1) * CHUNK, lanes] = w_c.astype(BF16)
            qk_scr[c * DN_HEADS + h] = qk_c.astype(BF16)

    def back():
        u_all, w_all, qd_all, kd_all = u_scr[...], w_scr[...], qd_scr[...], kd_scr[...]
        gl_all, og, o_sc = gl_scr[...], og_scr[...], osc_scr[...]
        qks = [qk_scr[i] for i in range(nchunk * DN_HEADS)]
        xs = [xb_ref[s] for s in range(nseq)]
        gates = [modb_ref[s, 3 * sub + 2:3 * sub + 3, :] for s in range(nseq)]
        states = [s_scr[i] * keep_back for i in range(nseq * DN_HEADS)]
        o_rows = [[None] * DN_HEADS for _ in range(nchunk)]
        for j in range(per_seq):
            chains = [(s, s * per_seq + j, h, lanes) for s in range(nseq) for h, lanes in enumerate(head_lanes)]
            rws = [slice(c * CHUNK, (c + 1) * CHUNK) for _, c, _, _ in chains]
            r = [jnp.dot(jnp.concatenate([w_all[rw, lanes], qd_all[rw, lanes]], axis=0),
                         states[s * DN_HEADS + h].astype(BF16), preferred_element_type=F32)
                 for (s, c, h, lanes), rw in zip(chains, rws)]
            yield
            v_new = [(u_all[rw, lanes] - r_u[:CHUNK]).astype(BF16)
                     for (s, c, h, lanes), rw, r_u in zip(chains, rws, r)]
            for (s, c, h, lanes), r_u, v_u in zip(chains, r, v_new):
                o_rows[c][h] = r_u[CHUNK:] + jnp.dot(qks[c * DN_HEADS + h], v_u, preferred_element_type=F32)
            for (s, c, h, lanes), rw, v_u in zip(chains, rws, v_new):
                i = s * DN_HEADS + h
                states[i] = (states[i] * gl_all[c * SUBLANES:c * SUBLANES + 1, lanes]
                             + lax.dot_general(kd_all[rw, lanes], v_u, (((0,), (0,)), ((), ())),
                                               preferred_element_type=F32))
            yield
        for i, st in enumerate(states):
            s_scr[i] = st
        for s in range(nseq):
            for h in range(DN_HEADS):
                s_out_ref[s, h] = states[s * DN_HEADS + h]

        o_dn = jnp.concatenate(
            [_gate_norm(jnp.concatenate([o_rows[c][h] for c in range(nchunk)], axis=0),
                        og[:, lanes], ng_ref[...]) for h, lanes in enumerate(head_lanes)], axis=1)
        dn_w = o_dn.shape[1]
        mix = (jnp.dot(o_dn.astype(BF16), w_out_ref[0:dn_w, :], preferred_element_type=F32)
               + jnp.dot(o_sc.astype(BF16), w_out_ref[dn_w:, :], preferred_element_type=F32))
        yield
        for s in range(nseq):
            y_ref[s] = _post(xs[s], mix[s * tt:(s + 1) * tt], gates[s],
                             _row(lng_ref, sub), _row(lnb_ref, sub), alpha)

    @pl.when(k == 0)
    def _():
        s_scr[...] = jnp.zeros_like(s_scr)
        cq_scr[...] = jnp.zeros_like(cq_scr)
        cm_scr[...] = jnp.zeros_like(cm_scr)
        _run_interleaved(front())

    @pl.when(jnp.logical_and(k > 0, k < n_tiles))
    def _():
        _run_interleaved(back(), front())

    @pl.when(k == n_tiles)
    def _():
        _run_interleaved(back())


def _mixer_prompt(x, mod, w_in, w_out, conv_w, mconv_w, alog, dtb, norm_g, lng, lnb, *, alpha, nseq, tt, off):
    bsz, t, d = x.shape
    assert t % tt == 0 and tt % LANES == 0 and bsz % nseq == 0
    nt = t // tt
    n_tiles = (bsz // nseq) * nt
    rows = nseq * tt
    qkv_dim = conv_w.shape[-1]
    sc_w = off["sc_c"][1] - off["sc_c"][0]
    dn_w = DN_HEADS * DN_DV
    n_conv, n_mconv = conv_w.shape[0], mconv_w.shape[0]
    assert max(n_conv, n_mconv) - 1 <= SUBLANES <= tt
    nchunk = rows // CHUNK
    kern = functools.partial(_mixer_prompt_kernel, alpha=alpha, nseq=nseq, tt=tt, tiles_per_group=nt,
                             n_tiles=n_tiles, off=off, n_conv=n_conv, n_mconv=n_mconv)

    def front_tile(k):
        return jnp.minimum(k, n_tiles - 1)

    def back_tile(k):
        return jnp.maximum(k - 1, 0)

    y, s_out, cq, cm = pl.pallas_call(
        kern,
        grid=(n_tiles + 1,),
        in_specs=[pl.BlockSpec((nseq, tt, d), lambda k: (front_tile(k) // nt, front_tile(k) % nt, 0)),
                  pl.BlockSpec((nseq, 3 * N_SUB, d), lambda k: (front_tile(k) // nt, 0, 0)),
                  pl.BlockSpec((nseq, tt, d), lambda k: (back_tile(k) // nt, back_tile(k) % nt, 0)),
                  pl.BlockSpec((nseq, 3 * N_SUB, d), lambda k: (back_tile(k) // nt, 0, 0)),
                  *[_const_spec(w.shape) for w in w_in], _const_spec(w_out.shape),
                  _const_spec(conv_w.shape), _const_spec(mconv_w.shape),
                  _const_spec(alog.shape), _const_spec(dtb.shape), _const_spec(norm_g.shape),
                  _const_spec(lng.shape), _const_spec(lnb.shape)],
        out_specs=[pl.BlockSpec((nseq, tt, d), lambda k: (back_tile(k) // nt, back_tile(k) % nt, 0)),
                   pl.BlockSpec((nseq, DN_HEADS, DN_DK, DN_DV), lambda k: (back_tile(k) // nt, 0, 0, 0)),
                   pl.BlockSpec((nseq, n_conv - 1, qkv_dim), lambda k: (front_tile(k) // nt, 0, 0)),
                   pl.BlockSpec((nseq, n_mconv - 1, sc_w), lambda k: (front_tile(k) // nt, 0, 0))],
        out_shape=[jax.ShapeDtypeStruct((bsz, t, d), F32),
                   jax.ShapeDtypeStruct((bsz, DN_HEADS, DN_DK, DN_DV), F32),
                   jax.ShapeDtypeStruct((bsz, n_conv - 1, qkv_dim), F32),
                   jax.ShapeDtypeStruct((bsz, n_mconv - 1, sc_w), F32)],
        scratch_shapes=[pltpu.VMEM((nseq * DN_HEADS, DN_DK, DN_DV), F32),
                        pltpu.VMEM((nseq, SUBLANES, qkv_dim), F32),
                        pltpu.VMEM((nseq, SUBLANES, sc_w), F32),
                        pltpu.VMEM((nchunk, SUBLANES, CHUNK), F32),
                        pltpu.VMEM((rows, dn_w), F32),
                        pltpu.VMEM((rows, dn_w), BF16),
                        pltpu.VMEM((nchunk * DN_HEADS, CHUNK, CHUNK), BF16),
                        pltpu.VMEM((rows, dn_w), BF16),
                        pltpu.VMEM((rows, dn_w), BF16),
                        pltpu.VMEM((nchunk * SUBLANES, dn_w), F32),
                        pltpu.VMEM((rows, dn_w), F32),
                        pltpu.VMEM((rows, sc_w), F32)],
        compiler_params=pltpu.CompilerParams(dimension_semantics=("arbitrary",),
                                             vmem_limit_bytes=V7X_VMEM_LIMIT),
        name="mixer_prompt",
    )(x, mod, x, mod, *w_in, w_out, conv_w, mconv_w, alog, dtb, norm_g, lng, lnb)
    return y, s_out, cq, cm


_PACK = SUBLANES


def _sample_phase_a(x_ref, mod_ref, wq_ref, wr_ref, wab_ref, cw_ref, mw_ref, alog_ref, dtb_ref, cq_in_ref, cm_in_ref,
                    cq_out_ref, cm_out_ref, og_scr, osc_scr, lhs_scr, u_scr, kd_scr, gl_scr, qk_scr,
                    *, t_len, d, off, n_conv, n_mconv):
    bsz = x_ref.shape[0]
    hk = DN_HEADS * DN_DK
    sc_w = off["sc_c"][1] - off["sc_c"][0]

    x = _stack_time(x_ref, t_len, d)
    u = (x * (1.0 + _mod_rows(mod_ref, 1, t_len, d)) + _mod_rows(mod_ref, 0, t_len, d)).astype(BF16)
    pq = jnp.dot(u, wq_ref[...], preferred_element_type=F32)
    ab = jnp.dot(u, wab_ref[...], preferred_element_type=F32)
    rest = jnp.dot(u, wr_ref[...], preferred_element_type=F32)
    p = {name: rest[:, lo:hi] for name, (lo, hi) in off.items()}

    def rows(arr, t):
        return arr[t * bsz:(t + 1) * bsz]

    seq = [cq_in_ref[j] for j in range(n_conv - 1)] + [rows(pq, t) for t in range(t_len)]
    act = []
    for t in range(t_len):
        acc = None
        for j in range(n_conv):
            term = seq[t + j] * cw_ref[j:j + 1, :]
            acc = term if acc is None else acc + term
        act.append(_silu(acc))
    for j in range(n_conv - 1):
        cq_out_ref[j] = seq[t_len + j]

    z = p["sc_c"] * p["sc_h"]
    zseq = ([cm_in_ref[:, j, :] for j in range(n_mconv - 1)]
            + [rows(z, t) for t in range(t_len)])
    zc = []
    for t in range(t_len):
        acc = None
        for j in range(n_mconv):
            term = zseq[t + j] * _row(mw_ref, j)
            acc = term if acc is None else acc + term
        zc.append(acc)
    for j in range(n_mconv - 1):
        cm_out_ref[:, j, :] = zseq[t_len + j]
    osc_scr[...] = p["sc_b"] * jnp.concatenate(zc, axis=0)
    og_scr[...] = p["og"]

    beta_all = jax.nn.sigmoid(ab)
    g_all = -jnp.exp(alog_ref[...]) * _softplus(ab + dtb_ref[...])
    zero = jnp.zeros((bsz, LANES), F32)

    for h in range(DN_HEADS):
        q = [_l2norm(a[:, h * DN_DK:(h + 1) * DN_DK]) * (DN_DK ** -0.5) for a in act]
        k = [_l2norm(a[:, hk + h * DN_DK:hk + (h + 1) * DN_DK]) for a in act]
        v = [a[:, 2 * hk + h * DN_DV:2 * hk + (h + 1) * DN_DV] for a in act]
        beta = [rows(beta_all, t)[:, DN_HEADS + h:DN_HEADS + h + 1] for t in range(t_len)]
        g = [rows(g_all, t)[:, h:h + 1] for t in range(t_len)]
        gc = [g[0]]
        for t in range(1, t_len):
            gc.append(gc[-1] + g[t])
        kb = [k[t] * beta[t] for t in range(t_len)]
        vb = [v[t] * beta[t] for t in range(t_len)]
        eg = [jnp.exp(gc[t]) for t in range(t_len)]
        kbe = [kb[t] * eg[t] for t in range(t_len)]
        lm = [[None] * t_len for _ in range(t_len)]
        n_qk = 0
        for i in range(t_len):
            for j in range(i + 1):
                dec = jnp.exp(gc[i] - gc[j])
                qk_scr[h, n_qk] = jnp.broadcast_to(_rowsum(q[i] * k[j]) * dec, (bsz, LANES))
                n_qk += 1
                if j < i:
                    lm[i][j] = _rowsum(kb[i] * k[j]) * dec
        tm = [[None] * t_len for _ in range(t_len)]
        for i in range(t_len):
            for j in range(i):
                acc = lm[i][j]
                for m in range(j + 1, i):
                    acc = acc + lm[i][m] * tm[m][j]
                tm[i][j] = -acc
        for i in range(t_len):
            ui, wi = vb[i], kbe[i]
            for j in range(i):
                ui = ui + tm[i][j] * vb[j]
                wi = wi + tm[i][j] * kbe[j]
            lhs_scr[h, pl.ds(i, bsz, stride=_PACK), :] = wi
            lhs_scr[h, pl.ds(t_len + i, bsz, stride=_PACK), :] = q[i] * eg[i]
            u_scr[h, pl.ds(i, bsz, stride=_PACK), :] = ui
            u_scr[h, pl.ds(t_len + i, bsz, stride=_PACK), :] = zero
            kd_scr[h, pl.ds(i, bsz, stride=_PACK), :] = k[i] * jnp.exp(gc[t_len - 1] - gc[i])
            kd_scr[h, pl.ds(t_len + i, bsz, stride=_PACK), :] = zero
        gl_scr[h] = jnp.broadcast_to(jnp.exp(gc[t_len - 1]), (bsz, LANES))


def _sample_phase_c(x_ref, mod_ref, w_out_ref, ng_ref, lng_ref, lnb_ref, y_ref,
                    og_scr, osc_scr, res_scr, vn_scr, qk_scr, *, alpha, t_len, d):
    bsz = x_ref.shape[0]
    sub = 1
    x = _stack_time(x_ref, t_len, d)
    gate = _mod_rows(mod_ref, 2, t_len, d)
    og = og_scr[...]
    per_t = []
    for i in range(t_len):
        heads = []
        for h in range(DN_HEADS):
            oi = res_scr[h, pl.ds(t_len + i, bsz, stride=_PACK), :]
            base = i * (i + 1) // 2
            for j in range(i + 1):
                oi = oi + qk_scr[h, base + j] * vn_scr[h, pl.ds(j, bsz, stride=_PACK), :]
            heads.append(_gate_norm(oi, og[i * bsz:(i + 1) * bsz, h * DN_DV:(h + 1) * DN_DV], ng_ref[...]))
        per_t.append(jnp.concatenate(heads, axis=1))
    o_dn = jnp.concatenate(per_t, axis=0)
    dn_w = o_dn.shape[1]
    mix = (jnp.dot(o_dn.astype(BF16), w_out_ref[0:dn_w, :], preferred_element_type=F32)
           + jnp.dot(osc_scr[...].astype(BF16), w_out_ref[dn_w:, :], preferred_element_type=F32))
    y = _post(x, mix, gate, _row(lng_ref, sub), _row(lnb_ref, sub), alpha)
    for t in range(t_len):
        y_ref[:, t * d:(t + 1) * d] = y[t * bsz:(t + 1) * bsz]


def _mixer_sample_kernel(x_ref, mod_ref, wq_ref, wr_ref, wab_ref, w_out_ref, cw_ref, mw_ref, alog_ref, dtb_ref, ng_ref,
                         lng_ref, lnb_ref, cq_in_ref, cm_in_ref, s_in_hbm,
                         y_ref, cq_out_ref, cm_out_ref, s_out_hbm,
                         og_scr, osc_scr, lhs_scr, u_scr, kd_scr, gl_scr, qk_scr, res_scr, vn_scr,
                         s_buf, sem_in, sem_out,
                         *, alpha, t_len, d, off, n_conv, n_mconv, bt, n_blocks):
    step = pl.program_id(0)
    assert 2 * t_len == _PACK, "packed buffers hold [T rows | T rows] per sequence"
    n_slots = s_buf.shape[0]

    def fetch(block, slot):
        return pltpu.make_async_copy(s_in_hbm.at[pl.ds(block * bt, bt)], s_buf.at[slot], sem_in.at[slot])

    def write_back(block, slot):
        return pltpu.make_async_copy(s_buf.at[slot], s_out_hbm.at[pl.ds(block * bt, bt)], sem_out.at[slot])

    @pl.when(step == 0)
    def _():
        for b in range(min(n_slots, n_blocks)):
            fetch(b, b).start()
        _sample_phase_a(x_ref, mod_ref, wq_ref, wr_ref, wab_ref, cw_ref, mw_ref, alog_ref, dtb_ref, cq_in_ref, cm_in_ref,
                        cq_out_ref, cm_out_ref, og_scr, osc_scr, lhs_scr, u_scr, kd_scr, gl_scr, qk_scr,
                        t_len=t_len, d=d, off=off, n_conv=n_conv, n_mconv=n_mconv)

    top = (lax.broadcasted_iota(jnp.int32, (_PACK, LANES), 0) < t_len).astype(F32)

    chains = [(bl, h) for bl in range(bt) for h in range(DN_HEADS)]
    seq0 = step * bt
    row0 = [pl.multiple_of((seq0 + bl) * _PACK, _PACK) for bl in range(bt)]
    slot = lax.rem(step, n_slots)
    fetch(step, slot).wait()
    states = [s_buf[slot, bl, h] for bl, h in chains]
    r = [_mm(lhs_scr[h, pl.ds(row0[bl], _PACK), :], s) for (bl, h), s in zip(chains, states)]
    vns = []
    for (bl, h), r_u in zip(chains, r):
        res_scr[h, pl.ds(row0[bl], _PACK), :] = r_u
        vn = (u_scr[h, pl.ds(row0[bl], _PACK), :] - r_u) * top
        vn_scr[h, pl.ds(row0[bl], _PACK), :] = vn
        vns.append(vn)
    upd = [_mm_at(kd_scr[h, pl.ds(row0[bl], _PACK), :], vn) for (bl, h), vn in zip(chains, vns)]
    for (bl, h), s_add in zip(chains, upd):
        s_buf[slot, bl, h] = s_buf[slot, bl, h] * gl_scr[h, pl.ds(seq0 + bl, 1), :] + s_add
    write_back(step, slot).start()

    @pl.when(step > 0)
    def _():
        prev = step - 1
        prev_slot = lax.rem(prev, n_slots)
        write_back(prev, prev_slot).wait()

        @pl.when(prev + n_slots < n_blocks)
        def _():
            fetch(prev + n_slots, prev_slot).start()

    @pl.when(step == n_blocks - 1)
    def _():
        _sample_phase_c(x_ref, mod_ref, w_out_ref, ng_ref, lng_ref, lnb_ref, y_ref,
                        og_scr, osc_scr, res_scr, vn_scr, qk_scr, alpha=alpha, t_len=t_len, d=d)
        write_back(step, slot).wait()


def _mixer_sample(x, mod, w_in, w_out, conv_w, mconv_w, alog, dtb, norm_g, lng, lnb, s_in, cq_in, cm_in,
                  *, alpha, off, bt, t):
    bsz, d = x.shape[0], x.shape[1] // t
    assert bsz % bt == 0
    sc_w = off["sc_c"][1] - off["sc_c"][0]
    dn_w = DN_HEADS * DN_DV
    n_conv, n_mconv = conv_w.shape[0], mconv_w.shape[0]
    n_qk = t * (t + 1) // 2
    x2 = x
    cq2 = jnp.swapaxes(cq_in, 0, 1)
    kern = functools.partial(_mixer_sample_kernel, alpha=alpha, t_len=t, d=d, off=off,
                             n_conv=n_conv, n_mconv=n_mconv, bt=bt, n_blocks=bsz // bt)
    consts = (*w_in, w_out, conv_w, mconv_w, alog, dtb, norm_g, lng, lnb, cq2, cm_in)
    state_spec = pl.BlockSpec(memory_space=pl.ANY)
    packed = pltpu.VMEM((DN_HEADS, _PACK * bsz, LANES), F32)
    y, cq, cm, s_out = pl.pallas_call(
        kern,
        grid=(bsz // bt,),
        in_specs=([_const_spec(x2.shape), _sub_mod_spec(bsz, d, 1)]
                  + [_const_spec(a.shape) for a in consts] + [state_spec]),
        out_specs=[pl.BlockSpec(x2.shape, lambda i: (0, 0)),
                   pl.BlockSpec(cq2.shape, lambda i: (0, 0, 0)),
                   pl.BlockSpec(cm_in.shape, lambda i: (0, 0, 0)),
                   state_spec],
        out_shape=[jax.ShapeDtypeStruct(x2.shape, F32),
                   jax.ShapeDtypeStruct(cq2.shape, F32),
                   jax.ShapeDtypeStruct(cm_in.shape, F32),
                   jax.ShapeDtypeStruct(s_in.shape, F32)],
        scratch_shapes=[pltpu.VMEM((t * bsz, dn_w), F32),
                        pltpu.VMEM((t * bsz, sc_w), F32),
                        packed,
                        packed,
                        packed,
                        pltpu.VMEM((DN_HEADS, bsz, LANES), F32),
                        pltpu.VMEM((DN_HEADS, n_qk, bsz, LANES), F32),
                        packed,
                        packed,
                        pltpu.VMEM((STATE_SLOTS, bt, DN_HEADS, DN_DK, DN_DV), F32),
                        pltpu.SemaphoreType.DMA((STATE_SLOTS,)),
                        pltpu.SemaphoreType.DMA((STATE_SLOTS,))],
        compiler_params=pltpu.CompilerParams(dimension_semantics=("arbitrary",),
                                             vmem_limit_bytes=V7X_VMEM_LIMIT),
        name="mixer_sample",
    )(x2, mod, *consts, s_in)
    return y, s_out, jnp.swapaxes(cq, 0, 1), cm


def _pad_lanes(v):
    return jnp.pad(v.reshape(1, -1), ((0, 0), (0, AB_PAD - v.shape[0])))


def kernel(x_prompt, x_sample, state_ssm, state_conv_qkv, state_conv_mix, c_prompt, c_sample, w_ada, b_ada, ln_g, ln_b, ffn1_wg, ffn1_wu, ffn1_wd, ffn2_wg, ffn2_wu, ffn2_wd, w_in, conv_qkv_w, a_log, dt_bias, dn_norm_g, conv_mix_w, w_out):
    depth = w_ada.shape[0]
    alpha = (2 * depth) ** 0.25
    bp, tp, d = x_prompt.shape
    qkv_dim = conv_qkv_w.shape[-1]
    sc_width = conv_mix_w.shape[-1]
    dn_width = DN_HEADS * DN_DV
    off = _proj_layout(dn_width, sc_width)
    tm = min(512, tp)
    nseq = 2 if bp % 2 == 0 else 1
    tt = min(512 // nseq, tp)
    bt = min(16, x_sample.shape[0])
    ts = x_sample.shape[1]

    hp, hs = x_prompt, x_sample
    outs = [[] for _ in range(6)]
    for l in range(depth):
        mod_p, mod_s, *w_in_p = _adaln(c_prompt, c_sample, w_ada[l], b_ada[l].reshape(1, -1),
                                       jnp.swapaxes(w_in[l], 0, 1), qkv_dim)
        mod_p = mod_p.reshape(bp, 3 * N_SUB, d)
        lng, lnb, mconv_w = ln_g[l][:, None, :], ln_b[l][:, None, :], conv_mix_w[l][:, None, :]
        w1 = (ffn1_wg[l], ffn1_wu[l], ffn1_wd[l])
        w2 = (ffn2_wg[l], ffn2_wu[l], ffn2_wd[l])
        w_out_b = w_out[l].astype(BF16)
        alog, dtb = _pad_lanes(a_log[l]), _pad_lanes(dt_bias[l])
        norm_g = dn_norm_g[l].reshape(1, -1)
        mix_args = (w_in_p, w_out_b, conv_qkv_w[l], mconv_w, alog, dtb, norm_g, lng, lnb)

        hp, hs = _ffn(hp, hs, mod_p, mod_s, *w1, lng, lnb, sub=0, alpha=alpha, tm=tm, ts=ts,
                      ys_3d=False)
        hp, a1, a2, a3 = _mixer_prompt(hp, mod_p, *mix_args, alpha=alpha, nseq=nseq, tt=tt, off=off)
        hs, b1, b2, b3 = _mixer_sample(hs, mod_s, *mix_args, state_ssm[l], state_conv_qkv[l],
                                       state_conv_mix[l], alpha=alpha, off=off, bt=bt, t=ts)
        hp, hs = _ffn(hp, hs, mod_p, mod_s, *w2, lng, lnb, sub=2, alpha=alpha, tm=tm, ts=ts,
                      ys_3d=True)
        for lst, val in zip(outs, (a1, a2, a3, b1, b2, b3)):
            lst.append(val)
    return (hp, hs) + tuple(jnp.stack(o) for o in outs)
```

```python
import functools

import jax
import jax.numpy as jnp
from jax import lax
from jax.experimental import pallas as pl
from jax.experimental.pallas import tpu as pltpu

F32 = jnp.float32
BF16 = jnp.bfloat16

LN_EPS = 1e-5
RMS_EPS = 1e-6
N_SUB = 3
DN_HEADS = 4
DN_DK = 128
DN_DV = 128
CHUNK = 64
LANES = 128
SUBLANES = 8
AB_PAD = LANES

V7X_VMEM_LIMIT = 60000 * 1024
CAST_BUFFERS = 3
FFN_CHUNK = 256
STATE_SLOTS = 4


def _mm(a, b):
    return jnp.dot(a.astype(BF16), b.astype(BF16), preferred_element_type=F32)


def _mm_at(a, b):
    return lax.dot_general(a.astype(BF16), b.astype(BF16), (((0,), (0,)), ((), ())),
                           preferred_element_type=F32)


def _mm_f32(a, b):
    return jnp.dot(a, b, preferred_element_type=F32, precision=lax.Precision.HIGHEST)


def _silu(x):
    return x * jax.nn.sigmoid(x)


def _softplus(x):
    return jnp.maximum(x, 0.0) + jnp.log1p(jnp.exp(-jnp.abs(x)))


def _layer_norm(y, g, b):
    mu = jnp.mean(y, axis=-1, keepdims=True)
    yc = y - mu
    var = jnp.mean(yc * yc, axis=-1, keepdims=True)
    return yc * lax.rsqrt(var + LN_EPS) * g + b


def _post(x, delta, gate, g, b, alpha):
    return _layer_norm(alpha * x + gate * delta, g, b)


def _rowsum(x):
    return jnp.sum(x, axis=-1, keepdims=True)


def _l2norm(x):
    return x * lax.rsqrt(_rowsum(x * x) + RMS_EPS)


def _const_spec(shape):
    nd = len(shape)
    return pl.BlockSpec(shape, lambda *_: (0,) * nd, pipeline_mode=pl.Buffered(1))


def _lane_vector(smem_ref, layer):
    lane = lax.broadcasted_iota(jnp.int32, (1, AB_PAD), 1)
    out = jnp.zeros((1, AB_PAD), F32)
    for h in range(smem_ref.shape[1]):
        out = jnp.where(lane == h, smem_ref[layer, h], out)
    return out


def _row(ref, j):
    return ref[j] if len(ref.shape) == 3 else ref[j:j + 1, :]


def _adaln_kernel(cp_ref, cs_ref, w_ref, b_ref, win_ref, op_ref, os_ref, wq_ref, wr_ref, wab_ref, *, qkv_dim):
    mp = cp_ref.shape[0]
    c = jnp.concatenate([cp_ref[...], cs_ref[...]], axis=0)
    mod = _mm(_silu(c), w_ref[...]) + b_ref[...]
    op_ref[...] = mod[:mp]
    os_ref[...] = mod[mp:]

    gates = 2 * DN_HEADS
    wq_ref[...] = win_ref[0:qkv_dim, :].T.astype(BF16)
    wr_ref[...] = win_ref[qkv_dim + gates:, :].T.astype(BF16)
    ab_rows = jnp.concatenate([win_ref[qkv_dim:qkv_dim + gates, :],
                               jnp.zeros((AB_PAD - gates, win_ref.shape[1]), F32)], axis=0)
    wab_ref[...] = ab_rows.T.astype(BF16)


def _adaln(c_p, c_s, w, b, w_in_t, qkv_dim):
    (mp, d), ms = c_p.shape, c_s.shape[0]
    n = w.shape[1]
    proj = w_in_t.shape[0]
    rest = proj - qkv_dim - 2 * DN_HEADS
    assert qkv_dim % LANES == 0 and rest % LANES == 0 and (2 * DN_HEADS) % SUBLANES == 0
    steps = max(g for g in (1, 2, 4) if n % (g * LANES) == 0 and d % (g * LANES) == 0)
    tn, rows = n // steps, d // steps
    return pl.pallas_call(
        functools.partial(_adaln_kernel, qkv_dim=qkv_dim),
        grid=(steps,),
        in_specs=[_const_spec((mp, d)), _const_spec((ms, d)),
                  pl.BlockSpec((d, tn), lambda j: (0, j)),
                  pl.BlockSpec((1, tn), lambda j: (0, j)),
                  pl.BlockSpec((proj, rows), lambda j: (0, j))],
        out_specs=[pl.BlockSpec((mp, tn), lambda j: (0, j)), pl.BlockSpec((ms, tn), lambda j: (0, j)),
                   pl.BlockSpec((rows, qkv_dim), lambda j: (j, 0)),
                   pl.BlockSpec((rows, rest), lambda j: (j, 0)),
                   pl.BlockSpec((rows, AB_PAD), lambda j: (j, 0))],
        out_shape=[jax.ShapeDtypeStruct((mp, n), F32), jax.ShapeDtypeStruct((ms, n), F32),
                   jax.ShapeDtypeStruct((d, qkv_dim), BF16), jax.ShapeDtypeStruct((d, rest), BF16),
                   jax.ShapeDtypeStruct((d, AB_PAD), BF16)],
        compiler_params=pltpu.CompilerParams(dimension_semantics=("arbitrary",),
                                             vmem_limit_bytes=V7X_VMEM_LIMIT),
        name="adaln",
    )(c_p, c_s, w, b, w_in_t)


def _swiglu(x, shift, scale, wg_ref, wu_ref, wd_ref):
    u = (x * (1.0 + scale) + shift).astype(BF16)
    hg = jnp.dot(u, wg_ref[...], preferred_element_type=F32)
    hu = jnp.dot(u, wu_ref[...], preferred_element_type=F32)
    h = (_silu(hg) * hu).astype(BF16)
    return jnp.dot(h, wd_ref[...], preferred_element_type=F32)


def _stack_time(ref, t_len, d):
    if len(ref.shape) == 3:
        return jnp.concatenate([ref[:, t, :] for t in range(t_len)], axis=0)
    return jnp.concatenate([ref[:, t * d:(t + 1) * d] for t in range(t_len)], axis=0)


def _mod_rows(mod_ref, idx, t_len, d):
    m = mod_ref[:, idx * d:(idx + 1) * d]
    return jnp.concatenate([m] * t_len, axis=0)


def _sub_mod_spec(bsz, d, sub):
    return pl.BlockSpec((bsz, 3 * d), lambda *_: (0, sub), pipeline_mode=pl.Buffered(1))


def _first_tile_swiglu(x, shift, scale, w_hbm, w_scr, stages, sems):
    wg_hbm, wu_hbm, wd_hbm = w_hbm
    wg_scr, wu_scr, wd_scr = w_scr
    nbuf = stages[0].shape[0]
    cw = stages[0].shape[2]
    n = wg_hbm.shape[1] // cw

    def copies(j):
        slot = j % nbuf
        cols = pl.ds(j * cw, cw)
        return (pltpu.make_async_copy(wg_hbm.at[:, cols], stages[0].at[slot], sems[0].at[slot]),
                pltpu.make_async_copy(wu_hbm.at[:, cols], stages[1].at[slot], sems[1].at[slot]),
                pltpu.make_async_copy(wd_hbm.at[cols, :], stages[2].at[slot], sems[2].at[slot]))

    queue = [copies(j) for j in range(n)]
    ahead = nbuf - 1
    for group in queue[:ahead]:
        for copy in group:
            copy.start()
    u = (x * (1.0 + scale) + shift).astype(BF16)
    acc = None
    for j, group in enumerate(queue):
        if j + ahead < n:
            for copy in queue[j + ahead]:
                copy.start()
        for copy in group:
            copy.wait()
        slot, cols = j % nbuf, slice(j * cw, (j + 1) * cw)
        wg_scr[:, cols] = stages[0][slot].astype(BF16)
        wu_scr[:, cols] = stages[1][slot].astype(BF16)
        wd_scr[cols, :] = stages[2][slot].astype(BF16)
        hg = jnp.dot(u, wg_scr[:, cols], preferred_element_type=F32)
        hu = jnp.dot(u, wu_scr[:, cols], preferred_element_type=F32)
        part = jnp.dot((_silu(hg) * hu).astype(BF16), wd_scr[cols, :], preferred_element_type=F32)
        acc = part if acc is None else acc + part
    return acc


def _ffn_kernel(xp_ref, xq_ref, modp_ref, xs_ref, mods_ref, wg_hbm, wu_hbm, wd_hbm, lng_ref, lnb_ref,
                yp_ref, ys_ref, wg_scr, wu_scr, wd_scr, d_scr, stage_g, stage_u, stage_d, sem_g, sem_u, sem_d,
                *, sub, alpha, n_prompt, per_seq, t_len, d):
    k = pl.program_id(0)
    g, b = _row(lng_ref, sub), _row(lnb_ref, sub)
    seq_now = jnp.minimum(k, n_prompt - 1) // per_seq
    seq_prev = jnp.maximum(k - 1, 0) // per_seq

    def mod_row(seq, idx):
        return modp_ref[pl.ds(seq, 1), idx * d:(idx + 1) * d]

    def matmuls():
        return _swiglu(xp_ref[...], mod_row(seq_now, 0), mod_row(seq_now, 1), wg_scr, wu_scr, wd_scr)

    def finish(d_prev):
        yp_ref[...] = _post(xq_ref[...], 0.5 * d_prev, mod_row(seq_prev, 2), g, b, alpha)

    @pl.when(k == 0)
    def _():
        d_scr[...] = _first_tile_swiglu(
            xp_ref[...], mod_row(0, 0), mod_row(0, 1),
            (wg_hbm, wu_hbm, wd_hbm), (wg_scr, wu_scr, wd_scr), (stage_g, stage_u, stage_d),
            (sem_g, sem_u, sem_d))

    @pl.when(jnp.logical_and(k > 0, k < n_prompt))
    def _():
        d_prev = d_scr[...]
        d_new = matmuls()
        finish(d_prev)
        d_scr[...] = d_new

    @pl.when(k == n_prompt)
    def _():
        finish(d_scr[...])
        bsz = xs_ref.shape[0]
        x = _stack_time(xs_ref, t_len, d)
        delta = _swiglu(x, _mod_rows(mods_ref, 0, t_len, d), _mod_rows(mods_ref, 1, t_len, d),
                        wg_scr, wu_scr, wd_scr)
        y = _post(x, 0.5 * delta, _mod_rows(mods_ref, 2, t_len, d), g, b, alpha)
        for t in range(t_len):
            if len(ys_ref.shape) == 3:
                ys_ref[:, t, :] = y[t * bsz:(t + 1) * bsz]
            else:
                ys_ref[:, t * d:(t + 1) * d] = y[t * bsz:(t + 1) * bsz]


def _ffn(xp, xs, mod_p, mod_s, wg, wu, wd, lng, lnb, *, sub, alpha, tm, ts, ys_3d):
    bp, tp, d = xp.shape
    bs = xs.shape[0]
    ys_shape = (bs, ts, d) if ys_3d else (bs, ts * d)
    f = wg.shape[1]
    assert tp % tm == 0
    per_seq = tp // tm
    n_prompt = bp * per_seq
    assert f % FFN_CHUNK == 0
    last = n_prompt - 1
    yp, ys = pl.pallas_call(
        functools.partial(_ffn_kernel, sub=sub, alpha=alpha, n_prompt=n_prompt, per_seq=per_seq, t_len=ts, d=d),
        grid=(n_prompt + 1,),
        in_specs=[pl.BlockSpec((tm, d), lambda k: (jnp.minimum(k, last), 0)),
                  pl.BlockSpec((tm, d), lambda k: (jnp.maximum(k - 1, 0), 0)),
                  _sub_mod_spec(bp, d, sub),
                  _const_spec(xs.shape), _sub_mod_spec(bs, d, sub),
                  pl.BlockSpec(memory_space=pl.ANY), pl.BlockSpec(memory_space=pl.ANY),
                  pl.BlockSpec(memory_space=pl.ANY),
                  _const_spec(lng.shape), _const_spec(lnb.shape)],
        out_specs=[pl.BlockSpec((tm, d), lambda k: (jnp.maximum(k - 1, 0), 0)),
                   pl.BlockSpec(ys_shape, lambda k: (0,) * len(ys_shape))],
        out_shape=[jax.ShapeDtypeStruct((bp * tp, d), F32),
                   jax.ShapeDtypeStruct(ys_shape, F32)],
        scratch_shapes=[pltpu.VMEM((d, f), BF16), pltpu.VMEM((d, f), BF16), pltpu.VMEM((f, d), BF16),
                        pltpu.VMEM((tm, d), F32),
                        pltpu.VMEM((CAST_BUFFERS, d, FFN_CHUNK), F32), pltpu.VMEM((CAST_BUFFERS, d, FFN_CHUNK), F32),
                        pltpu.VMEM((CAST_BUFFERS, FFN_CHUNK, d), F32),
                        pltpu.SemaphoreType.DMA((CAST_BUFFERS,)), pltpu.SemaphoreType.DMA((CAST_BUFFERS,)),
                        pltpu.SemaphoreType.DMA((CAST_BUFFERS,))],
        compiler_params=pltpu.CompilerParams(dimension_semantics=("arbitrary",),
                                             vmem_limit_bytes=V7X_VMEM_LIMIT),
        name=f"ffn{sub}",
    )(xp.reshape(bp * tp, d), xp.reshape(bp * tp, d), mod_p, xs, mod_s,
      wg, wu, wd, lng, lnb)
    return yp.reshape(bp, tp, d), ys


def _tri_inverse_steps(lmats, c):
    ls = [l.astype(BF16) for l in lmats]
    ms = [jnp.dot(l, l, preferred_element_type=F32) for l in ls]
    yield
    rows = lax.broadcasted_iota(jnp.int32, (c, c), 0)
    cols = lax.broadcasted_iota(jnp.int32, (c, c), 1)
    eye = (rows == cols).astype(F32)
    ps = [eye - l for l in lmats]
    power = 2
    while 2 * power < c:
        mp = [jnp.concatenate([m, p], axis=0).astype(BF16) for m, p in zip(ms, ps)]
        sq = [jnp.dot(x, x[:c], preferred_element_type=F32) for x in mp]
        yield
        ms = [x[:c] for x in sq]
        ps = [p + x[c:] for p, x in zip(ps, sq)]
        power *= 2
    pm = [jnp.dot(p.astype(BF16), m.astype(BF16), preferred_element_type=F32) for p, m in zip(ps, ms)]
    yield
    return [p + x for p, x in zip(ps, pm)]


def _gdn_local_steps(units, tril, strict):
    c = units[0][1].shape[0]
    a = [lax.dot_general(jnp.concatenate([kb, q], axis=0), k, (((1,), (1,)), ((), ())),
                         preferred_element_type=F32)
         for kb, q, k, _, _, _, _ in units]
    yield
    lmats, qks = [], []
    for a_u, (_, _, _, _, _, gcol, grow) in zip(a, units):
        decay = jnp.where(tril, jnp.exp(jnp.where(tril, gcol - grow, 0.0)), 0.0)
        lmats.append(jnp.where(strict, a_u[:c] * decay, 0.0))
        qks.append(jnp.where(tril, a_u[c:] * decay, 0.0))
    tms = yield from _tri_inverse_steps(lmats, c)
    uw = [jnp.dot(tm.astype(BF16), jnp.concatenate([vb, kbe], axis=1), preferred_element_type=F32)
          for tm, (_, _, _, vb, kbe, _, _) in zip(tms, units)]
    yield
    return [(x[:, :DN_DV], x[:, DN_DV:], qk) for x, qk in zip(uw, qks)]


def _alternate(primary, secondary):
    result = None
    done = False
    while not done:
        try:
            next(primary)
            yield
        except StopIteration as stop:
            result, done = stop.value, True
        if secondary is not None:
            try:
                next(secondary)
                yield
            except StopIteration:
                secondary = None
    if secondary is not None:
        yield from secondary
    return result


def _run_interleaved(*gens):
    live = list(gens)
    while live:
        for g in list(live):
            try:
                next(g)
            except StopIteration:
                live.remove(g)


def _gate_norm(o, og, norm_g):
    o = o * lax.rsqrt(jnp.mean(o * o, axis=-1, keepdims=True) + RMS_EPS) * norm_g
    return o * _silu(og)


def _proj_layout(dn_width, sc_width):
    off = {}
    pos = 0
    for name, width in (("og", dn_width), ("sc_b", sc_width), ("sc_c", sc_width), ("sc_h", sc_width)):
        off[name] = (pos, pos + width)
        pos += width
    return off


def _mixer_prompt_kernel(xf_ref, xb_ref, mod_ref, wq_ref, wr_ref, wab_ref, w_out_ref, cw_ref,
                         mw_ref, alog_ref, dtb_ref, ng_ref, lng_ref, lnb_ref,
                         y_ref, s_out_ref, cq_out_ref, cm_out_ref,
                         s_scr, cq_scr, cm_scr, gct_scr,
                         u_scr, w_scr, qk_scr, qd_scr, kd_scr, gl_scr, og_scr, osc_scr,
                         *, alpha, nseq, tt, tiles_per_group, n_tiles, off, n_conv, n_mconv, layer):
    k = pl.program_id(0)
    d = xf_ref.shape[-1]
    seq_front = (jnp.minimum(k, n_tiles - 1) // tiles_per_group) * nseq
    seq_back = (jnp.maximum(k - 1, 0) // tiles_per_group) * nseq

    def mod_row(seq, idx):
        return mod_ref[pl.ds(seq, 1), idx * d:(idx + 1) * d]

    rows_all = nseq * tt
    nchunk = rows_all // CHUNK
    per_seq = tt // CHUNK
    hk = DN_HEADS * DN_DK
    sub = 1
    head_lanes = [slice(h * DN_DK, (h + 1) * DN_DK) for h in range(DN_HEADS)]
    t_front = lax.rem(jnp.minimum(k, n_tiles - 1), tiles_per_group)
    t_back = lax.rem(jnp.maximum(k - 1, 0), tiles_per_group)
    keep_front = jnp.where(t_front == 0, 0.0, 1.0).astype(F32)
    keep_back = jnp.where(t_back == 0, 0.0, 1.0).astype(F32)

    def causal_conv(new_rows, hist_scr, s, w_ref, taps):
        ext = jnp.concatenate([hist_scr[s] * keep_front, new_rows], axis=0)
        acc = new_rows * _row(w_ref, taps - 1)
        for j in range(taps - 1):
            acc = acc + pltpu.roll(ext, taps - 1 - j, axis=0)[SUBLANES:] * _row(w_ref, j)
        hist_scr[s] = new_rows[tt - SUBLANES:]
        return acc

    def front():
        u = jnp.concatenate(
            [(xf_ref[s] * (1.0 + mod_row(seq_front + s, 1))
              + mod_row(seq_front + s, 0)).astype(BF16) for s in range(nseq)], axis=0)
        ab = jnp.dot(u, wab_ref[...], preferred_element_type=F32)
        pq = jnp.dot(u, wq_ref[...], preferred_element_type=F32)
        yield

        beta_all = jax.nn.sigmoid(ab)
        g = -jnp.exp(_lane_vector(alog_ref, layer)) * _softplus(ab + _lane_vector(dtb_ref, layer))
        rows = lax.broadcasted_iota(jnp.int32, (CHUNK, CHUNK), 0)
        cols = lax.broadcasted_iota(jnp.int32, (CHUNK, CHUNK), 1)
        tril = rows >= cols
        strict = rows > cols
        tril_f = tril.astype(F32)
        gc_parts, gl_parts = [], []
        for c in range(nchunk):
            gc_c = _mm_f32(tril_f, g[c * CHUNK:(c + 1) * CHUNK, :])
            gc_parts.append(gc_c)
            gl_parts.append(jnp.broadcast_to(gc_c[CHUNK - 1:CHUNK, :], (CHUNK, AB_PAD)))
        yield
        p = {}

        def project_rest(names):
            for name in names:
                lo, hi = off[name]
                p[name] = jnp.dot(u, wr_ref[:, lo:hi], preferred_element_type=F32)
                yield

        yield from project_rest(("sc_c", "sc_h"))
        gc = jnp.concatenate(gc_parts, axis=0)
        gl_rows = jnp.concatenate(gl_parts, axis=0)
        gc_t = gc.T
        for c in range(nchunk):
            gct_scr[c] = gc_t[0:SUBLANES, c * CHUNK:(c + 1) * CHUNK]

        acts = []
        for s in range(nseq):
            pq_s = pq[s * tt:(s + 1) * tt]
            acts.append(_silu(causal_conv(pq_s, cq_scr, s, cw_ref, n_conv)))
            for j in range(n_conv - 1):
                row = tt - (n_conv - 1) + j
                cq_out_ref[j, pl.ds(seq_front + s, 1), :] = pq_s[row:row + 1]
        act = jnp.concatenate(acts, axis=0)

        eg_all = jnp.exp(gc)
        ekd_all = jnp.exp(gl_rows - gc)
        egl_all = jnp.exp(gl_rows)
        per_head = []
        for h, lanes in enumerate(head_lanes):
            q = _l2norm(act[:, h * DN_DK:(h + 1) * DN_DK]) * (DN_DK ** -0.5)
            kk = _l2norm(act[:, hk + h * DN_DK:hk + (h + 1) * DN_DK])
            v = act[:, 2 * hk + h * DN_DV:2 * hk + (h + 1) * DN_DV]
            beta = beta_all[:, DN_HEADS + h:DN_HEADS + h + 1]
            eg = eg_all[:, h:h + 1]
            kb = kk * beta
            per_head.append((kb.astype(BF16), q.astype(BF16), kk.astype(BF16), (v * beta).astype(BF16),
                             (kb * eg).astype(BF16)))
            qd_scr[:, lanes] = (q * eg).astype(BF16)
            kd_scr[:, lanes] = (kk * ekd_all[:, h:h + 1]).astype(BF16)
            for c in range(nchunk):
                gl_scr[c * SUBLANES:(c + 1) * SUBLANES, lanes] = jnp.broadcast_to(
                    egl_all[c * CHUNK:c * CHUNK + SUBLANES, h:h + 1], (SUBLANES, DN_DV))

        z = p["sc_c"] * p["sc_h"]
        zcs = []
        for s in range(nseq):
            z_s = z[s * tt:(s + 1) * tt]
            zcs.append(causal_conv(z_s, cm_scr, s, mw_ref, n_mconv))
            cm_out_ref[s] = z_s[tt - (n_mconv - 1):]

        where = [(c, h, lanes) for c in range(nchunk) for h, lanes in enumerate(head_lanes)]
        units = [tuple(arr[c * CHUNK:(c + 1) * CHUNK] for arr in per_head[h])
                 + (gc[c * CHUNK:(c + 1) * CHUNK, h:h + 1], gct_scr[c, h:h + 1, :])
                 for c, h, lanes in where]
        local = yield from _alternate(_gdn_local_steps(units, tril, strict), project_rest(("sc_b", "og")))
        osc_scr[...] = p["sc_b"] * jnp.concatenate(zcs, axis=0)
        og_scr[...] = p["og"]
        for (c, h, lanes), (u_c, w_c, qk_c) in zip(where, local):
            u_scr[c * CHUNK:(c + 1) * CHUNK, lanes] = u_c
            w_scr[c * CHUNK:(c + 1) * CHUNK, lanes] = w_c.astype(BF16)
            qk_scr[c * DN_HEADS + h] = qk_c.astype(BF16)

    def back():
        u_all, w_all, qd_all, kd_all = u_scr[...], w_scr[...], qd_scr[...], kd_scr[...]
        gl_all, og, o_sc = gl_scr[...], og_scr[...], osc_scr[...]
        qks = [qk_scr[i] for i in range(nchunk * DN_HEADS)]
        xs = [xb_ref[s] for s in range(nseq)]
        gates = [mod_row(seq_back + s, 2) for s in range(nseq)]
        states = [s_scr[i] * keep_back for i in range(nseq * DN_HEADS)]
        o_rows = [[None] * DN_HEADS for _ in range(nchunk)]
        for j in range(per_seq):
            chains = [(s, s * per_seq + j, h, lanes) for s in range(nseq) for h, lanes in enumerate(head_lanes)]
            rws = [slice(c * CHUNK, (c + 1) * CHUNK) for _, c, _, _ in chains]
            r = [jnp.dot(jnp.concatenate([w_all[rw, lanes], qd_all[rw, lanes]], axis=0),
                         states[s * DN_HEADS + h].astype(BF16), preferred_element_type=F32)
                 for (s, c, h, lanes), rw in zip(chains, rws)]
            yield
            v_new = [(u_all[rw, lanes] - r_u[:CHUNK]).astype(BF16)
                     for (s, c, h, lanes), rw, r_u in zip(chains, rws, r)]
            for (s, c, h, lanes), r_u, v_u in zip(chains, r, v_new):
                o_rows[c][h] = r_u[CHUNK:] + jnp.dot(qks[c * DN_HEADS + h], v_u, preferred_element_type=F32)
            for (s, c, h, lanes), rw, v_u in zip(chains, rws, v_new):
                i = s * DN_HEADS + h
                states[i] = (states[i] * gl_all[c * SUBLANES:c * SUBLANES + 1, lanes]
                             + lax.dot_general(kd_all[rw, lanes], v_u, (((0,), (0,)), ((), ())),
                                               preferred_element_type=F32))
            yield
        for i, st in enumerate(states):
            s_scr[i] = st
        for s in range(nseq):
            for h in range(DN_HEADS):
                s_out_ref[s, h] = states[s * DN_HEADS + h]

        o_dn = jnp.concatenate(
            [_gate_norm(jnp.concatenate([o_rows[c][h] for c in range(nchunk)], axis=0),
                        og[:, lanes], ng_ref[...]) for h, lanes in enumerate(head_lanes)], axis=1)
        dn_w = o_dn.shape[1]
        mix = (jnp.dot(o_dn.astype(BF16), w_out_ref[0:dn_w, :], preferred_element_type=F32)
               + jnp.dot(o_sc.astype(BF16), w_out_ref[dn_w:, :], preferred_element_type=F32))
        yield
        for s in range(nseq):
            y_ref[s] = _post(xs[s], mix[s * tt:(s + 1) * tt], gates[s],
                             _row(lng_ref, sub), _row(lnb_ref, sub), alpha)

    @pl.when(k == 0)
    def _():
        s_scr[...] = jnp.zeros_like(s_scr)
        cq_scr[...] = jnp.zeros_like(cq_scr)
        cm_scr[...] = jnp.zeros_like(cm_scr)
        _run_interleaved(front())

    @pl.when(jnp.logical_and(k > 0, k < n_tiles))
    def _():
        _run_interleaved(back(), front())

    @pl.when(k == n_tiles)
    def _():
        _run_interleaved(back())


def _mixer_prompt(x, mod, w_in, w_out, conv_w, mconv_w, alog, dtb, norm_g, lng, lnb, *, alpha, nseq, tt, off, layer):
    bsz, t, d = x.shape
    assert t % tt == 0 and tt % LANES == 0 and bsz % nseq == 0
    nt = t // tt
    n_tiles = (bsz // nseq) * nt
    rows = nseq * tt
    qkv_dim = conv_w.shape[-1]
    sc_w = off["sc_c"][1] - off["sc_c"][0]
    dn_w = DN_HEADS * DN_DV
    n_conv, n_mconv = conv_w.shape[0], mconv_w.shape[0]
    assert max(n_conv, n_mconv) - 1 <= SUBLANES <= tt
    nchunk = rows // CHUNK
    kern = functools.partial(_mixer_prompt_kernel, alpha=alpha, nseq=nseq, tt=tt, tiles_per_group=nt,
                             n_tiles=n_tiles, off=off, n_conv=n_conv, n_mconv=n_mconv, layer=layer)

    def front_tile(k):
        return jnp.minimum(k, n_tiles - 1)

    def back_tile(k):
        return jnp.maximum(k - 1, 0)

    y, s_out, cq, cm = pl.pallas_call(
        kern,
        grid=(n_tiles + 1,),
        in_specs=[pl.BlockSpec((nseq, tt, d), lambda k: (front_tile(k) // nt, front_tile(k) % nt, 0)),
                  pl.BlockSpec((nseq, tt, d), lambda k: (back_tile(k) // nt, back_tile(k) % nt, 0)),
                  _sub_mod_spec(bsz, d, 1),
                  *[_const_spec(w.shape) for w in w_in], _const_spec(w_out.shape),
                  _const_spec(conv_w.shape), _const_spec(mconv_w.shape),
                  pl.BlockSpec(memory_space=pltpu.SMEM), pl.BlockSpec(memory_space=pltpu.SMEM),
                  _const_spec(norm_g.shape), _const_spec(lng.shape), _const_spec(lnb.shape)],
        out_specs=[pl.BlockSpec((nseq, tt, d), lambda k: (back_tile(k) // nt, back_tile(k) % nt, 0)),
                   pl.BlockSpec((nseq, DN_HEADS, DN_DK, DN_DV), lambda k: (back_tile(k) // nt, 0, 0, 0)),
                   pl.BlockSpec((n_conv - 1, bsz, qkv_dim), lambda k: (0, 0, 0)),
                   pl.BlockSpec((nseq, n_mconv - 1, sc_w), lambda k: (front_tile(k) // nt, 0, 0))],
        out_shape=[jax.ShapeDtypeStruct((bsz, t, d), F32),
                   jax.ShapeDtypeStruct((bsz, DN_HEADS, DN_DK, DN_DV), F32),
                   jax.ShapeDtypeStruct((n_conv - 1, bsz, qkv_dim), F32),
                   jax.ShapeDtypeStruct((bsz, n_mconv - 1, sc_w), F32)],
        scratch_shapes=[pltpu.VMEM((nseq * DN_HEADS, DN_DK, DN_DV), F32),
                        pltpu.VMEM((nseq, SUBLANES, qkv_dim), F32),
                        pltpu.VMEM((nseq, SUBLANES, sc_w), F32),
                        pltpu.VMEM((nchunk, SUBLANES, CHUNK), F32),
                        pltpu.VMEM((rows, dn_w), F32),
                        pltpu.VMEM((rows, dn_w), BF16),
                        pltpu.VMEM((nchunk * DN_HEADS, CHUNK, CHUNK), BF16),
                        pltpu.VMEM((rows, dn_w), BF16),
                        pltpu.VMEM((rows, dn_w), BF16),
                        pltpu.VMEM((nchunk * SUBLANES, dn_w), F32),
                        pltpu.VMEM((rows, dn_w), F32),
                        pltpu.VMEM((rows, sc_w), F32)],
        compiler_params=pltpu.CompilerParams(dimension_semantics=("arbitrary",),
                                             vmem_limit_bytes=V7X_VMEM_LIMIT),
        name="mixer_prompt",
    )(x, x, mod, *w_in, w_out, conv_w, mconv_w, alog, dtb, norm_g, lng, lnb)
    return y, s_out, jnp.swapaxes(cq, 0, 1), cm


_PACK = SUBLANES


def _sample_phase_a(x_ref, mod_ref, wq_ref, wr_ref, wab_ref, cw_ref, mw_ref, alog_ref, dtb_ref, cq_in_ref, cm_in_ref,
                    cq_out_ref, cm_out_ref, og_scr, osc_scr, lhs_scr, u_scr, kd_scr, gl_scr, qk_scr,
                    *, t_len, d, off, n_conv, n_mconv, layer):
    bsz = x_ref.shape[0]
    hk = DN_HEADS * DN_DK
    sc_w = off["sc_c"][1] - off["sc_c"][0]

    x = _stack_time(x_ref, t_len, d)
    u = (x * (1.0 + _mod_rows(mod_ref, 1, t_len, d)) + _mod_rows(mod_ref, 0, t_len, d)).astype(BF16)
    pq = jnp.dot(u, wq_ref[...], preferred_element_type=F32)
    ab = jnp.dot(u, wab_ref[...], preferred_element_type=F32)
    rest = jnp.dot(u, wr_ref[...], preferred_element_type=F32)
    p = {name: rest[:, lo:hi] for name, (lo, hi) in off.items()}

    def rows(arr, t):
        return arr[t * bsz:(t + 1) * bsz]

    seq = [cq_in_ref[j] for j in range(n_conv - 1)] + [rows(pq, t) for t in range(t_len)]
    act = []
    for t in range(t_len):
        acc = None
        for j in range(n_conv):
            term = seq[t + j] * cw_ref[j:j + 1, :]
            acc = term if acc is None else acc + term
        act.append(_silu(acc))
    for j in range(n_conv - 1):
        cq_out_ref[j] = seq[t_len + j]

    z = p["sc_c"] * p["sc_h"]
    zseq = ([cm_in_ref[:, j, :] for j in range(n_mconv - 1)]
            + [rows(z, t) for t in range(t_len)])
    zc = []
    for t in range(t_len):
        acc = None
        for j in range(n_mconv):
            term = zseq[t + j] * _row(mw_ref, j)
            acc = term if acc is None else acc + term
        zc.append(acc)
    for j in range(n_mconv - 1):
        cm_out_ref[:, j, :] = zseq[t_len + j]
    osc_scr[...] = p["sc_b"] * jnp.concatenate(zc, axis=0)
    og_scr[...] = p["og"]

    beta_all = jax.nn.sigmoid(ab)
    g_all = -jnp.exp(_lane_vector(alog_ref, layer)) * _softplus(ab + _lane_vector(dtb_ref, layer))
    zero = jnp.zeros((bsz, LANES), F32)

    for h in range(DN_HEADS):
        q = [_l2norm(a[:, h * DN_DK:(h + 1) * DN_DK]) * (DN_DK ** -0.5) for a in act]
        k = [_l2norm(a[:, hk + h * DN_DK:hk + (h + 1) * DN_DK]) for a in act]
        v = [a[:, 2 * hk + h * DN_DV:2 * hk + (h + 1) * DN_DV] for a in act]
        beta = [rows(beta_all, t)[:, DN_HEADS + h:DN_HEADS + h + 1] for t in range(t_len)]
        g = [rows(g_all, t)[:, h:h + 1] for t in range(t_len)]
        gc = [g[0]]
        for t in range(1, t_len):
            gc.append(gc[-1] + g[t])
        kb = [k[t] * beta[t] for t in range(t_len)]
        vb = [v[t] * beta[t] for t in range(t_len)]
        eg = [jnp.exp(gc[t]) for t in range(t_len)]
        kbe = [kb[t] * eg[t] for t in range(t_len)]
        lm = [[None] * t_len for _ in range(t_len)]
        n_qk = 0
        for i in range(t_len):
            for j in range(i + 1):
                dec = jnp.exp(gc[i] - gc[j])
                qk_scr[h, n_qk] = jnp.broadcast_to(_rowsum(q[i] * k[j]) * dec, (bsz, LANES))
                n_qk += 1
                if j < i:
                    lm[i][j] = _rowsum(kb[i] * k[j]) * dec
        tm = [[None] * t_len for _ in range(t_len)]
        for i in range(t_len):
            for j in range(i):
                acc = lm[i][j]
                for m in range(j + 1, i):
                    acc = acc + lm[i][m] * tm[m][j]
                tm[i][j] = -acc
        for i in range(t_len):
            ui, wi = vb[i], kbe[i]
            for j in range(i):
                ui = ui + tm[i][j] * vb[j]
                wi = wi + tm[i][j] * kbe[j]
            lhs_scr[h, pl.ds(i, bsz, stride=_PACK), :] = wi
            lhs_scr[h, pl.ds(t_len + i, bsz, stride=_PACK), :] = q[i] * eg[i]
            u_scr[h, pl.ds(i, bsz, stride=_PACK), :] = ui
            u_scr[h, pl.ds(t_len + i, bsz, stride=_PACK), :] = zero
            kd_scr[h, pl.ds(i, bsz, stride=_PACK), :] = k[i] * jnp.exp(gc[t_len - 1] - gc[i])
            kd_scr[h, pl.ds(t_len + i, bsz, stride=_PACK), :] = zero
        gl_scr[h] = jnp.broadcast_to(jnp.exp(gc[t_len - 1]), (bsz, LANES))


def _sample_phase_c(x_ref, mod_ref, w_out_ref, ng_ref, lng_ref, lnb_ref, y_ref,
                    og_scr, osc_scr, res_scr, vn_scr, qk_scr, *, alpha, t_len, d):
    bsz = x_ref.shape[0]
    sub = 1
    x = _stack_time(x_ref, t_len, d)
    gate = _mod_rows(mod_ref, 2, t_len, d)
    og = og_scr[...]
    per_t = []
    for i in range(t_len):
        heads = []
        for h in range(DN_HEADS):
            oi = res_scr[h, pl.ds(t_len + i, bsz, stride=_PACK), :]
            base = i * (i + 1) // 2
            for j in range(i + 1):
                oi = oi + qk_scr[h, base + j] * vn_scr[h, pl.ds(j, bsz, stride=_PACK), :]
            heads.append(_gate_norm(oi, og[i * bsz:(i + 1) * bsz, h * DN_DV:(h + 1) * DN_DV], ng_ref[...]))
        per_t.append(jnp.concatenate(heads, axis=1))
    o_dn = jnp.concatenate(per_t, axis=0)
    dn_w = o_dn.shape[1]
    mix = (jnp.dot(o_dn.astype(BF16), w_out_ref[0:dn_w, :], preferred_element_type=F32)
           + jnp.dot(osc_scr[...].astype(BF16), w_out_ref[dn_w:, :], preferred_element_type=F32))
    y = _post(x, mix, gate, _row(lng_ref, sub), _row(lnb_ref, sub), alpha)
    for t in range(t_len):
        y_ref[:, t * d:(t + 1) * d] = y[t * bsz:(t + 1) * bsz]


def _mixer_sample_kernel(x_ref, mod_ref, wq_ref, wr_ref, wab_ref, w_out_ref, cw_ref, mw_ref, alog_ref, dtb_ref, ng_ref,
                         lng_ref, lnb_ref, cq_in_ref, cm_in_ref, s_in_hbm,
                         y_ref, cq_out_ref, cm_out_ref, s_out_hbm,
                         og_scr, osc_scr, lhs_scr, u_scr, kd_scr, gl_scr, qk_scr, res_scr, vn_scr,
                         s_buf, sem_in, sem_out,
                         *, alpha, t_len, d, off, n_conv, n_mconv, bt, n_blocks, layer):
    step = pl.program_id(0)
    assert 2 * t_len == _PACK, "packed buffers hold [T rows | T rows] per sequence"
    n_slots = s_buf.shape[0]

    def fetch(block, slot):
        return pltpu.make_async_copy(s_in_hbm.at[pl.ds(block * bt, bt)], s_buf.at[slot], sem_in.at[slot])

    def write_back(block, slot):
        return pltpu.make_async_copy(s_buf.at[slot], s_out_hbm.at[pl.ds(block * bt, bt)], sem_out.at[slot])

    @pl.when(step == 0)
    def _():
        for b in range(min(n_slots, n_blocks)):
            fetch(b, b).start()
        _sample_phase_a(x_ref, mod_ref, wq_ref, wr_ref, wab_ref, cw_ref, mw_ref, alog_ref, dtb_ref, cq_in_ref, cm_in_ref,
                        cq_out_ref, cm_out_ref, og_scr, osc_scr, lhs_scr, u_scr, kd_scr, gl_scr, qk_scr,
                        t_len=t_len, d=d, off=off, n_conv=n_conv, n_mconv=n_mconv, layer=layer)

    top =(lax.broadcasted_iota(jnp.int32, (_PACK, LANES), 0) < t_len).astype(F32)

    chains = [(bl, h) for bl in range(bt) for h in range(DN_HEADS)]
    seq0 = step * bt
    row0 = [pl.multiple_of((seq0 + bl) * _PACK, _PACK) for bl in range(bt)]
    slot = lax.rem(step, n_slots)
    fetch(step, slot).wait()
    states = [s_buf[slot, bl, h] for bl, h in chains]
    r = [_mm(lhs_scr[h, pl.ds(row0[bl], _PACK), :], s) for (bl, h), s in zip(chains, states)]
    vns = []
    for (bl, h), r_u in zip(chains, r):
        res_scr[h, pl.ds(row0[bl], _PACK), :] = r_u
        vn = (u_scr[h, pl.ds(row0[bl], _PACK), :] - r_u) * top
        vn_scr[h, pl.ds(row0[bl], _PACK), :] = vn
        vns.append(vn)
    upd = [_mm_at(kd_scr[h, pl.ds(row0[bl], _PACK), :], vn) for (bl, h), vn in zip(chains, vns)]
    for (bl, h), s_add in zip(chains, upd):
        s_buf[slot, bl, h] = s_buf[slot, bl, h] * gl_scr[h, pl.ds(seq0 + bl, 1), :] + s_add
    write_back(step, slot).start()

    @pl.when(step > 0)
    def _():
        prev = step - 1
        prev_slot = lax.rem(prev, n_slots)
        write_back(prev, prev_slot).wait()

        @pl.when(prev + n_slots < n_blocks)
        def _():
            fetch(prev + n_slots, prev_slot).start()

    @pl.when(step == n_blocks - 1)
    def _():
        _sample_phase_c(x_ref, mod_ref, w_out_ref, ng_ref, lng_ref, lnb_ref, y_ref,
                        og_scr, osc_scr, res_scr, vn_scr, qk_scr, alpha=alpha, t_len=t_len, d=d)
        write_back(step, slot).wait()


def _mixer_sample(x, mod, w_in, w_out, conv_w, mconv_w, alog, dtb, norm_g, lng, lnb, s_in, cq_in, cm_in,
                  *, alpha, off, bt, t, layer):
    bsz, d = x.shape[0], x.shape[1] // t
    assert bsz % bt == 0
    sc_w = off["sc_c"][1] - off["sc_c"][0]
    dn_w = DN_HEADS * DN_DV
    n_conv, n_mconv = conv_w.shape[0], mconv_w.shape[0]
    n_qk = t * (t + 1) // 2
    x2 = x
    cq2 = jnp.swapaxes(cq_in, 0, 1)
    kern = functools.partial(_mixer_sample_kernel, alpha=alpha, t_len=t, d=d, off=off,
                             n_conv=n_conv, n_mconv=n_mconv, bt=bt, n_blocks=bsz // bt, layer=layer)
    consts = (*w_in, w_out, conv_w, mconv_w, alog, dtb, norm_g, lng, lnb, cq2, cm_in)
    state_spec = pl.BlockSpec(memory_space=pl.ANY)
    packed = pltpu.VMEM((DN_HEADS, _PACK * bsz, LANES), F32)
    y, cq, cm, s_out = pl.pallas_call(
        kern,
        grid=(bsz // bt,),
        in_specs=([_const_spec(x2.shape), _sub_mod_spec(bsz, d, 1)]
                  + [pl.BlockSpec(memory_space=pltpu.SMEM) if a is alog or a is dtb else _const_spec(a.shape)
                     for a in consts] + [state_spec]),
        out_specs=[pl.BlockSpec(x2.shape, lambda i: (0, 0)),
                   pl.BlockSpec(cq2.shape, lambda i: (0, 0, 0)),
                   pl.BlockSpec(cm_in.shape, lambda i: (0, 0, 0)),
                   state_spec],
        out_shape=[jax.ShapeDtypeStruct(x2.shape, F32),
                   jax.ShapeDtypeStruct(cq2.shape, F32),
                   jax.ShapeDtypeStruct(cm_in.shape, F32),
                   jax.ShapeDtypeStruct(s_in.shape, F32)],
        scratch_shapes=[pltpu.VMEM((t * bsz, dn_w), F32),
                        pltpu.VMEM((t * bsz, sc_w), F32),
                        packed,
                        packed,
                        packed,
                        pltpu.VMEM((DN_HEADS, bsz, LANES), F32),
                        pltpu.VMEM((DN_HEADS, n_qk, bsz, LANES), F32),
                        packed,
                        packed,
                        pltpu.VMEM((STATE_SLOTS, bt, DN_HEADS, DN_DK, DN_DV), F32),
                        pltpu.SemaphoreType.DMA((STATE_SLOTS,)),
                        pltpu.SemaphoreType.DMA((STATE_SLOTS,))],
        compiler_params=pltpu.CompilerParams(dimension_semantics=("arbitrary",),
                                             vmem_limit_bytes=V7X_VMEM_LIMIT),
        name="mixer_sample",
    )(x2, mod, *consts, s_in)
    return y, s_out, jnp.swapaxes(cq, 0, 1), cm


def kernel(x_prompt, x_sample, state_ssm, state_conv_qkv, state_conv_mix, c_prompt, c_sample, w_ada, b_ada, ln_g, ln_b, ffn1_wg, ffn1_wu, ffn1_wd, ffn2_wg, ffn2_wu, ffn2_wd, w_in, conv_qkv_w, a_log, dt_bias, dn_norm_g, conv_mix_w, w_out):
    depth = w_ada.shape[0]
    alpha = (2 * depth) ** 0.25
    bp, tp, d = x_prompt.shape
    qkv_dim = conv_qkv_w.shape[-1]
    sc_width = conv_mix_w.shape[-1]
    dn_width = DN_HEADS * DN_DV
    off = _proj_layout(dn_width, sc_width)
    tm = min(512, tp)
    nseq = 2 if bp % 2 == 0 else 1
    tt = min(512 // nseq, tp)
    bt = min(16, x_sample.shape[0])
    ts = x_sample.shape[1]

    hp, hs = x_prompt, x_sample
    outs = [[] for _ in range(6)]
    for l in range(depth):
        mod_p, mod_s, *w_in_p = _adaln(c_prompt, c_sample, w_ada[l], b_ada[l].reshape(1, -1),
                                       jnp.swapaxes(w_in[l], 0, 1), qkv_dim)
        lng, lnb, mconv_w = ln_g[l][:, None, :], ln_b[l][:, None, :], conv_mix_w[l][:, None, :]
        w1 = (ffn1_wg[l], ffn1_wu[l], ffn1_wd[l])
        w2 = (ffn2_wg[l], ffn2_wu[l], ffn2_wd[l])
        w_out_b = w_out[l].astype(BF16)
        norm_g = dn_norm_g[l].reshape(1, -1)
        mix_args = (w_in_p, w_out_b, conv_qkv_w[l], mconv_w, a_log, dt_bias, norm_g, lng, lnb)

        hp, hs = _ffn(hp, hs, mod_p, mod_s, *w1, lng, lnb, sub=0, alpha=alpha, tm=tm, ts=ts,
                      ys_3d=False)
        hp, a1, a2, a3 = _mixer_prompt(hp, mod_p, *mix_args, alpha=alpha, nseq=nseq, tt=tt, off=off, layer=l)
        hs, b1, b2, b3 = _mixer_sample(hs, mod_s, *mix_args, state_ssm[l], state_conv_qkv[l],
                                       state_conv_mix[l], alpha=alpha, off=off, bt=bt, t=ts, layer=l)
        hp, hs = _ffn(hp, hs, mod_p, mod_s, *w2, lng, lnb, sub=2, alpha=alpha, tm=tm, ts=ts,
                      ys_3d=True)
        for lst, val in zip(outs, (a1, a2, a3, b1, b2, b3)):
            lst.append(val)
    return (hp, hs) + tuple(jnp.stack(o) for o in outs)
```

```python
import functools

import jax
import jax.numpy as jnp
from jax import lax
from jax.experimental import pallas as pl
from jax.experimental.pallas import tpu as pltpu

F32 = jnp.float32
BF16 = jnp.bfloat16

LN_EPS = 1e-5
RMS_EPS = 1e-6
N_SUB = 3
DN_HEADS = 4
DN_DK = 128
DN_DV = 128
CHUNK = 64
LANES = 128
SUBLANES = 8
AB_PAD = LANES

V7X_VMEM_LIMIT = 60000 * 1024
CAST_BUFFERS = 3
FFN_CHUNK = 256
STATE_SLOTS = 4


def _mm(a, b):
    return jnp.dot(a.astype(BF16), b.astype(BF16), preferred_element_type=F32)


def _mm_at(a, b):
    return lax.dot_general(a.astype(BF16), b.astype(BF16), (((0,), (0,)), ((), ())),
                           preferred_element_type=F32)


def _mm_f32(a, b):
    return jnp.dot(a, b, preferred_element_type=F32, precision=lax.Precision.HIGHEST)


def _silu(x):
    return x * jax.nn.sigmoid(x)


def _softplus(x):
    return jnp.maximum(x, 0.0) + jnp.log1p(jnp.exp(-jnp.abs(x)))


def _layer_norm(y, g, b):
    mu = jnp.mean(y, axis=-1, keepdims=True)
    yc = y - mu
    var = jnp.mean(yc * yc, axis=-1, keepdims=True)
    return yc * lax.rsqrt(var + LN_EPS) * g + b


def _post(x, delta, gate, g, b, alpha):
    return _layer_norm(alpha * x + gate * delta, g, b)


def _rowsum(x):
    return jnp.sum(x, axis=-1, keepdims=True)


def _l2norm(x):
    return x * lax.rsqrt(_rowsum(x * x) + RMS_EPS)


def _const_spec(shape):
    nd = len(shape)
    return pl.BlockSpec(shape, lambda *_: (0,) * nd, pipeline_mode=pl.Buffered(1))


def _lane_vector(smem_ref, layer):
    lane = lax.broadcasted_iota(jnp.int32, (1, AB_PAD), 1)
    out = jnp.zeros((1, AB_PAD), F32)
    for h in range(smem_ref.shape[1]):
        out = jnp.where(lane == h, smem_ref[layer, h], out)
    return out


def _row(ref, j):
    return ref[j] if len(ref.shape) == 3 else ref[j:j + 1, :]


def _adaln_kernel(cp_ref, cs_ref, w_ref, b_ref, win_ref, op_ref, os_ref, wq_ref, wr_ref, wab_ref, *, qkv_dim):
    mp = cp_ref.shape[0]
    c = jnp.concatenate([cp_ref[...], cs_ref[...]], axis=0)
    mod = _mm(_silu(c), w_ref[...]) + b_ref[...]
    op_ref[...] = mod[:mp]
    os_ref[...] = mod[mp:]

    gates = 2 * DN_HEADS
    wq_ref[...] = win_ref[0:qkv_dim, :].T.astype(BF16)
    wr_ref[...] = win_ref[qkv_dim + gates:, :].T.astype(BF16)
    ab_rows = jnp.concatenate([win_ref[qkv_dim:qkv_dim + gates, :],
                               jnp.zeros((AB_PAD - gates, win_ref.shape[1]), F32)], axis=0)
    wab_ref[...] = ab_rows.T.astype(BF16)


def _adaln(c_p, c_s, w, b, w_in_t, qkv_dim):
    (mp, d), ms = c_p.shape, c_s.shape[0]
    n = w.shape[1]
    proj = w_in_t.shape[0]
    rest = proj - qkv_dim - 2 * DN_HEADS
    assert qkv_dim % LANES == 0 and rest % LANES == 0 and (2 * DN_HEADS) % SUBLANES == 0
    steps = max(g for g in (1, 2, 4) if n % (g * LANES) == 0 and d % (g * LANES) == 0)
    tn, rows = n // steps, d // steps
    return pl.pallas_call(
        functools.partial(_adaln_kernel, qkv_dim=qkv_dim),
        grid=(steps,),
        in_specs=[_const_spec((mp, d)), _const_spec((ms, d)),
                  pl.BlockSpec((d, tn), lambda j: (0, j)),
                  pl.BlockSpec((1, tn), lambda j: (0, j)),
                  pl.BlockSpec((proj, rows), lambda j: (0, j))],
        out_specs=[pl.BlockSpec((mp, tn), lambda j: (0, j)), pl.BlockSpec((ms, tn), lambda j: (0, j)),
                   pl.BlockSpec((rows, qkv_dim), lambda j: (j, 0)),
                   pl.BlockSpec((rows, rest), lambda j: (j, 0)),
                   pl.BlockSpec((rows, AB_PAD), lambda j: (j, 0))],
        out_shape=[jax.ShapeDtypeStruct((mp, n), F32), jax.ShapeDtypeStruct((ms, n), F32),
                   jax.ShapeDtypeStruct((d, qkv_dim), BF16), jax.ShapeDtypeStruct((d, rest), BF16),
                   jax.ShapeDtypeStruct((d, AB_PAD), BF16)],
        compiler_params=pltpu.CompilerParams(dimension_semantics=("arbitrary",),
                                             vmem_limit_bytes=V7X_VMEM_LIMIT),
        name="adaln",
    )(c_p, c_s, w, b, w_in_t)


def _swiglu(x, shift, scale, wg_ref, wu_ref, wd_ref):
    u = (x * (1.0 + scale) + shift).astype(BF16)
    hg = jnp.dot(u, wg_ref[...], preferred_element_type=F32)
    hu = jnp.dot(u, wu_ref[...], preferred_element_type=F32)
    h = (_silu(hg) * hu).astype(BF16)
    return jnp.dot(h, wd_ref[...], preferred_element_type=F32)


def _stack_time(ref, t_len, d):
    if len(ref.shape) == 3:
        return jnp.concatenate([ref[:, t, :] for t in range(t_len)], axis=0)
    return jnp.concatenate([ref[:, t * d:(t + 1) * d] for t in range(t_len)], axis=0)


def _mod_rows(mod_ref, idx, t_len, d):
    m = mod_ref[:, idx * d:(idx + 1) * d]
    return jnp.concatenate([m] * t_len, axis=0)


def _sub_mod_spec(bsz, d, sub):
    return pl.BlockSpec((bsz, 3 * d), lambda *_: (0, sub), pipeline_mode=pl.Buffered(1))


def _first_tile_swiglu(x, shift, scale, w_hbm, w_scr, stages, sems):
    wg_hbm, wu_hbm, wd_hbm = w_hbm
    wg_scr, wu_scr, wd_scr = w_scr
    nbuf = stages[0].shape[0]
    cw = stages[0].shape[2]
    n = wg_hbm.shape[1] // cw

    def copies(j):
        slot = j % nbuf
        cols = pl.ds(j * cw, cw)
        return (pltpu.make_async_copy(wg_hbm.at[:, cols], stages[0].at[slot], sems[0].at[slot]),
                pltpu.make_async_copy(wu_hbm.at[:, cols], stages[1].at[slot], sems[1].at[slot]),
                pltpu.make_async_copy(wd_hbm.at[cols, :], stages[2].at[slot], sems[2].at[slot]))

    queue = [copies(j) for j in range(n)]
    ahead = nbuf - 1
    for group in queue[:ahead]:
        for copy in group:
            copy.start()
    u = (x * (1.0 + scale) + shift).astype(BF16)
    acc = None
    for j, group in enumerate(queue):
        if j + ahead < n:
            for copy in queue[j + ahead]:
                copy.start()
        for copy in group:
            copy.wait()
        slot, cols = j % nbuf, slice(j * cw, (j + 1) * cw)
        wg_scr[:, cols] = stages[0][slot].astype(BF16)
        wu_scr[:, cols] = stages[1][slot].astype(BF16)
        wd_scr[cols, :] = stages[2][slot].astype(BF16)
        hg = jnp.dot(u, wg_scr[:, cols], preferred_element_type=F32)
        hu = jnp.dot(u, wu_scr[:, cols], preferred_element_type=F32)
        part = jnp.dot((_silu(hg) * hu).astype(BF16), wd_scr[cols, :], preferred_element_type=F32)
        acc = part if acc is None else acc + part
    return acc


def _ffn_kernel(xp_ref, xq_ref, modp_ref, xs_ref, mods_ref, wg_hbm, wu_hbm, wd_hbm, lng_ref, lnb_ref,
                yp_ref, ys_ref, wg_scr, wu_scr, wd_scr, d_scr, stage_g, stage_u, stage_d, sem_g, sem_u, sem_d,
                *, sub, alpha, n_prompt, per_seq, t_len, d):
    k = pl.program_id(0)
    g, b = _row(lng_ref, sub), _row(lnb_ref, sub)
    seq_now = jnp.minimum(k, n_prompt - 1) // per_seq
    seq_prev = jnp.maximum(k - 1, 0) // per_seq

    def mod_row(seq, idx):
        return modp_ref[pl.ds(seq, 1), idx * d:(idx + 1) * d]

    def matmuls():
        return _swiglu(xp_ref[...], mod_row(seq_now, 0), mod_row(seq_now, 1), wg_scr, wu_scr, wd_scr)

    def finish(d_prev):
        yp_ref[...] = _post(xq_ref[...], 0.5 * d_prev, mod_row(seq_prev, 2), g, b, alpha)

    @pl.when(k == 0)
    def _():
        d_scr[...] = _first_tile_swiglu(
            xp_ref[...], mod_row(0, 0), mod_row(0, 1),
            (wg_hbm, wu_hbm, wd_hbm), (wg_scr, wu_scr, wd_scr), (stage_g, stage_u, stage_d),
            (sem_g, sem_u, sem_d))

    @pl.when(jnp.logical_and(k > 0, k < n_prompt))
    def _():
        d_prev = d_scr[...]
        d_new = matmuls()
        finish(d_prev)
        d_scr[...] = d_new

    @pl.when(k == n_prompt)
    def _():
        finish(d_scr[...])
        bsz = xs_ref.shape[0]
        x = _stack_time(xs_ref, t_len, d)
        delta = _swiglu(x, _mod_rows(mods_ref, 0, t_len, d), _mod_rows(mods_ref, 1, t_len, d),
                        wg_scr, wu_scr, wd_scr)
        y = _post(x, 0.5 * delta, _mod_rows(mods_ref, 2, t_len, d), g, b, alpha)
        for t in range(t_len):
            if len(ys_ref.shape) == 3:
                ys_ref[:, t, :] = y[t * bsz:(t + 1) * bsz]
            else:
                ys_ref[:, t * d:(t + 1) * d] = y[t * bsz:(t + 1) * bsz]


def _ffn(xp, xs, mod_p, mod_s, wg, wu, wd, lng, lnb, *, sub, alpha, tm, ts, ys_3d):
    bp, tp, d = xp.shape
    bs = xs.shape[0]
    ys_shape = (bs, ts, d) if ys_3d else (bs, ts * d)
    f = wg.shape[1]
    assert tp % tm == 0
    per_seq = tp // tm
    n_prompt = bp * per_seq
    assert f % FFN_CHUNK == 0
    last = n_prompt - 1
    yp, ys = pl.pallas_call(
        functools.partial(_ffn_kernel, sub=sub, alpha=alpha, n_prompt=n_prompt, per_seq=per_seq, t_len=ts, d=d),
        grid=(n_prompt + 1,),
        in_specs=[pl.BlockSpec((tm, d), lambda k: (jnp.minimum(k, last), 0)),
                  pl.BlockSpec((tm, d), lambda k: (jnp.maximum(k - 1, 0), 0)),
                  _sub_mod_spec(bp, d, sub),
                  _const_spec(xs.shape), _sub_mod_spec(bs, d, sub),
                  pl.BlockSpec(memory_space=pl.ANY), pl.BlockSpec(memory_space=pl.ANY),
                  pl.BlockSpec(memory_space=pl.ANY),
                  _const_spec(lng.shape), _const_spec(lnb.shape)],
        out_specs=[pl.BlockSpec((tm, d), lambda k: (jnp.maximum(k - 1, 0), 0)),
                   pl.BlockSpec(ys_shape, lambda k: (0,) * len(ys_shape))],
        out_shape=[jax.ShapeDtypeStruct((bp * tp, d), F32),
                   jax.ShapeDtypeStruct(ys_shape, F32)],
        scratch_shapes=[pltpu.VMEM((d, f), BF16), pltpu.VMEM((d, f), BF16), pltpu.VMEM((f, d), BF16),
                        pltpu.VMEM((tm, d), F32),
                        pltpu.VMEM((CAST_BUFFERS, d, FFN_CHUNK), F32), pltpu.VMEM((CAST_BUFFERS, d, FFN_CHUNK), F32),
                        pltpu.VMEM((CAST_BUFFERS, FFN_CHUNK, d), F32),
                        pltpu.SemaphoreType.DMA((CAST_BUFFERS,)), pltpu.SemaphoreType.DMA((CAST_BUFFERS,)),
                        pltpu.SemaphoreType.DMA((CAST_BUFFERS,))],
        compiler_params=pltpu.CompilerParams(dimension_semantics=("arbitrary",),
                                             vmem_limit_bytes=V7X_VMEM_LIMIT),
        name=f"ffn{sub}",
    )(xp.reshape(bp * tp, d), xp.reshape(bp * tp, d), mod_p, xs, mod_s,
      wg, wu, wd, lng, lnb)
    return yp.reshape(bp, tp, d), ys


def _tri_inverse_steps(lmats, c):
    ls = [l.astype(BF16) for l in lmats]
    ms = [jnp.dot(l, l, preferred_element_type=F32) for l in ls]
    yield
    rows = lax.broadcasted_iota(jnp.int32, (c, c), 0)
    cols = lax.broadcasted_iota(jnp.int32, (c, c), 1)
    eye = (rows == cols).astype(F32)
    ps = [eye - l for l in lmats]
    power = 2
    while 2 * power < c:
        mp = [jnp.concatenate([m, p], axis=0).astype(BF16) for m, p in zip(ms, ps)]
        sq = [jnp.dot(x, x[:c], preferred_element_type=F32) for x in mp]
        yield
        ms = [x[:c] for x in sq]
        ps = [p + x[c:] for p, x in zip(ps, sq)]
        power *= 2
    pm = [jnp.dot(p.astype(BF16), m.astype(BF16), preferred_element_type=F32) for p, m in zip(ps, ms)]
    yield
    return [p + x for p, x in zip(ps, pm)]


def _gdn_local_steps(units, tril, strict):
    c = units[0][1].shape[0]
    a = [lax.dot_general(jnp.concatenate([kb, q], axis=0), k, (((1,), (1,)), ((), ())),
                         preferred_element_type=F32)
         for kb, q, k, _, _, _, _ in units]
    yield
    lmats, qks = [], []
    for a_u, (_, _, _, _, _, gcol, grow) in zip(a, units):
        decay = jnp.where(tril, jnp.exp(jnp.where(tril, gcol - grow, 0.0)), 0.0)
        lmats.append(jnp.where(strict, a_u[:c] * decay, 0.0))
        qks.append(jnp.where(tril, a_u[c:] * decay, 0.0))
    tms = yield from _tri_inverse_steps(lmats, c)
    uw = [jnp.dot(tm.astype(BF16), jnp.concatenate([vb, kbe], axis=1), preferred_element_type=F32)
          for tm, (_, _, _, vb, kbe, _, _) in zip(tms, units)]
    yield
    return [(x[:, :DN_DV], x[:, DN_DV:], qk) for x, qk in zip(uw, qks)]


def _alternate(primary, secondary):
    result = None
    done = False
    while not done:
        try:
            next(primary)
            yield
        except StopIteration as stop:
            result, done = stop.value, True
        if secondary is not None:
            try:
                next(secondary)
                yield
            except StopIteration:
                secondary = None
    if secondary is not None:
        yield from secondary
    return result


def _run_interleaved(*gens):
    live = list(gens)
    while live:
        for g in list(live):
            try:
                next(g)
            except StopIteration:
                live.remove(g)


def _gate_norm(o, og, norm_g):
    o = o * lax.rsqrt(jnp.mean(o * o, axis=-1, keepdims=True) + RMS_EPS) * norm_g
    return o * _silu(og)


def _proj_layout(dn_width, sc_width):
    off = {}
    pos = 0
    for name, width in (("og", dn_width), ("sc_b", sc_width), ("sc_c", sc_width), ("sc_h", sc_width)):
        off[name] = (pos, pos + width)
        pos += width
    return off


def _mixer_prompt_kernel(xf_ref, xb_ref, mod_ref, wq_ref, wr_ref, wab_ref, w_out_hbm, cw_ref,
                         mw_ref, alog_ref, dtb_ref, ng_ref, lng_ref, lnb_ref,
                         y_ref, s_out_ref, cq_out_ref, cm_out_ref, w_out_ref,
                         s_scr, cq_scr, cm_scr, gct_scr,
                         u_scr, w_scr, qk_scr, qd_scr, kd_scr, gl_scr, og_scr, osc_scr, w_out_stage, w_out_sem,
                         *, alpha, nseq, tt, tiles_per_group, n_tiles, off, n_conv, n_mconv, layer):
    k = pl.program_id(0)
    d = xf_ref.shape[-1]
    seq_front = (jnp.minimum(k, n_tiles - 1) // tiles_per_group) * nseq
    seq_back = (jnp.maximum(k - 1, 0) // tiles_per_group) * nseq

    def mod_row(seq, idx):
        return mod_ref[pl.ds(seq, 1), idx * d:(idx + 1) * d]

    rows_all = nseq * tt
    nchunk = rows_all // CHUNK
    per_seq = tt // CHUNK
    hk = DN_HEADS * DN_DK
    sub = 1
    head_lanes = [slice(h * DN_DK, (h + 1) * DN_DK) for h in range(DN_HEADS)]
    t_front = lax.rem(jnp.minimum(k, n_tiles - 1), tiles_per_group)
    t_back = lax.rem(jnp.maximum(k - 1, 0), tiles_per_group)
    keep_front = jnp.where(t_front == 0, 0.0, 1.0).astype(F32)
    keep_back = jnp.where(t_back == 0, 0.0, 1.0).astype(F32)

    def causal_conv(new_rows, hist_scr, s, w_ref, taps):
        ext = jnp.concatenate([hist_scr[s] * keep_front, new_rows], axis=0)
        acc = new_rows * _row(w_ref, taps - 1)
        for j in range(taps - 1):
            acc = acc + pltpu.roll(ext, taps - 1 - j, axis=0)[SUBLANES:] * _row(w_ref, j)
        hist_scr[s] = new_rows[tt - SUBLANES:]
        return acc

    def front():
        u = jnp.concatenate(
            [(xf_ref[s] * (1.0 + mod_row(seq_front + s, 1))
              + mod_row(seq_front + s, 0)).astype(BF16) for s in range(nseq)], axis=0)
        ab = jnp.dot(u, wab_ref[...], preferred_element_type=F32)
        pq = jnp.dot(u, wq_ref[...], preferred_element_type=F32)
        yield

        beta_all = jax.nn.sigmoid(ab)
        g = -jnp.exp(_lane_vector(alog_ref, layer)) * _softplus(ab + _lane_vector(dtb_ref, layer))
        rows = lax.broadcasted_iota(jnp.int32, (CHUNK, CHUNK), 0)
        cols = lax.broadcasted_iota(jnp.int32, (CHUNK, CHUNK), 1)
        tril = rows >= cols
        strict = rows > cols
        tril_f = tril.astype(F32)
        gc_parts, gl_parts = [], []
        for c in range(nchunk):
            gc_c = _mm_f32(tril_f, g[c * CHUNK:(c + 1) * CHUNK, :])
            gc_parts.append(gc_c)
            gl_parts.append(jnp.broadcast_to(gc_c[CHUNK - 1:CHUNK, :], (CHUNK, AB_PAD)))
        yield
        p = {}

        def project_rest(names):
            for name in names:
                lo, hi = off[name]
                p[name] = jnp.dot(u, wr_ref[:, lo:hi], preferred_element_type=F32)
                yield

        yield from project_rest(("sc_c", "sc_h"))
        gc = jnp.concatenate(gc_parts, axis=0)
        gl_rows = jnp.concatenate(gl_parts, axis=0)
        gc_t = gc.T
        for c in range(nchunk):
            gct_scr[c] = gc_t[0:SUBLANES, c * CHUNK:(c + 1) * CHUNK]

        acts = []
        for s in range(nseq):
            pq_s = pq[s * tt:(s + 1) * tt]
            acts.append(_silu(causal_conv(pq_s, cq_scr, s, cw_ref, n_conv)))
            for j in range(n_conv - 1):
                row = tt - (n_conv - 1) + j
                cq_out_ref[j, pl.ds(seq_front + s, 1), :] = pq_s[row:row + 1]
        act = jnp.concatenate(acts, axis=0)

        eg_all = jnp.exp(gc)
        ekd_all = jnp.exp(gl_rows - gc)
        egl_all = jnp.exp(gl_rows)
        per_head = []
        for h, lanes in enumerate(head_lanes):
            q = _l2norm(act[:, h * DN_DK:(h + 1) * DN_DK]) * (DN_DK ** -0.5)
            kk = _l2norm(act[:, hk + h * DN_DK:hk + (h + 1) * DN_DK])
            v = act[:, 2 * hk + h * DN_DV:2 * hk + (h + 1) * DN_DV]
            beta = beta_all[:, DN_HEADS + h:DN_HEADS + h + 1]
            eg = eg_all[:, h:h + 1]
            kb = kk * beta
            per_head.append((kb.astype(BF16), q.astype(BF16), kk.astype(BF16), (v * beta).astype(BF16),
                             (kb * eg).astype(BF16)))
            qd_scr[:, lanes] = (q * eg).astype(BF16)
            kd_scr[:, lanes] = (kk * ekd_all[:, h:h + 1]).astype(BF16)
            for c in range(nchunk):
                gl_scr[c * SUBLANES:(c + 1) * SUBLANES, lanes] = jnp.broadcast_to(
                    egl_all[c * CHUNK:c * CHUNK + SUBLANES, h:h + 1], (SUBLANES, DN_DV))

        z = p["sc_c"] * p["sc_h"]
        zcs = []
        for s in range(nseq):
            z_s = z[s * tt:(s + 1) * tt]
            zcs.append(causal_conv(z_s, cm_scr, s, mw_ref, n_mconv))
            cm_out_ref[s] = z_s[tt - (n_mconv - 1):]

        where = [(c, h, lanes) for c in range(nchunk) for h, lanes in enumerate(head_lanes)]
        units = [tuple(arr[c * CHUNK:(c + 1) * CHUNK] for arr in per_head[h])
                 + (gc[c * CHUNK:(c + 1) * CHUNK, h:h + 1], gct_scr[c, h:h + 1, :])
                 for c, h, lanes in where]
        local = yield from _alternate(_gdn_local_steps(units, tril, strict), project_rest(("sc_b", "og")))
        osc_scr[...] = p["sc_b"] * jnp.concatenate(zcs, axis=0)
        og_scr[...] = p["og"]
        for (c, h, lanes), (u_c, w_c, qk_c) in zip(where, local):
            u_scr[c * CHUNK:(c + 1) * CHUNK, lanes] = u_c
            w_scr[c * CHUNK:(c + 1) * CHUNK, lanes] = w_c.astype(BF16)
            qk_scr[c * DN_HEADS + h] = qk_c.astype(BF16)

    def back():
        u_all, w_all, qd_all, kd_all = u_scr[...], w_scr[...], qd_scr[...], kd_scr[...]
        gl_all, og, o_sc = gl_scr[...], og_scr[...], osc_scr[...]
        qks = [qk_scr[i] for i in range(nchunk * DN_HEADS)]
        xs = [xb_ref[s] for s in range(nseq)]
        gates = [mod_row(seq_back + s, 2) for s in range(nseq)]
        states = [s_scr[i] * keep_back for i in range(nseq * DN_HEADS)]
        o_rows = [[None] * DN_HEADS for _ in range(nchunk)]
        for j in range(per_seq):
            chains = [(s, s * per_seq + j, h, lanes) for s in range(nseq) for h, lanes in enumerate(head_lanes)]
            rws = [slice(c * CHUNK, (c + 1) * CHUNK) for _, c, _, _ in chains]
            r = [jnp.dot(jnp.concatenate([w_all[rw, lanes], qd_all[rw, lanes]], axis=0),
                         states[s * DN_HEADS + h].astype(BF16), preferred_element_type=F32)
                 for (s, c, h, lanes), rw in zip(chains, rws)]
            yield
            v_new = [(u_all[rw, lanes] - r_u[:CHUNK]).astype(BF16)
                     for (s, c, h, lanes), rw, r_u in zip(chains, rws, r)]
            for (s, c, h, lanes), r_u, v_u in zip(chains, r, v_new):
                o_rows[c][h] = r_u[CHUNK:] + jnp.dot(qks[c * DN_HEADS + h], v_u, preferred_element_type=F32)
            for (s, c, h, lanes), rw, v_u in zip(chains, rws, v_new):
                i = s * DN_HEADS + h
                states[i] = (states[i] * gl_all[c * SUBLANES:c * SUBLANES + 1, lanes]
                             + lax.dot_general(kd_all[rw, lanes], v_u, (((0,), (0,)), ((), ())),
                                               preferred_element_type=F32))
            yield
        for i, st in enumerate(states):
            s_scr[i] = st
        for s in range(nseq):
            for h in range(DN_HEADS):
                s_out_ref[s, h] = states[s * DN_HEADS + h]

        o_dn = jnp.concatenate(
            [_gate_norm(jnp.concatenate([o_rows[c][h] for c in range(nchunk)], axis=0),
                        og[:, lanes], ng_ref[...]) for h, lanes in enumerate(head_lanes)], axis=1)
        dn_w = o_dn.shape[1]
        mix = (jnp.dot(o_dn.astype(BF16), w_out_ref[0:dn_w, :], preferred_element_type=F32)
               + jnp.dot(o_sc.astype(BF16), w_out_ref[dn_w:, :], preferred_element_type=F32))
        yield
        for s in range(nseq):
            y_ref[s] = _post(xs[s], mix[s * tt:(s + 1) * tt], gates[s],
                             _row(lng_ref, sub), _row(lnb_ref, sub), alpha)

    @pl.when(k == 0)
    def _():
        s_scr[...] = jnp.zeros_like(s_scr)
        cq_scr[...] = jnp.zeros_like(cq_scr)
        cm_scr[...] = jnp.zeros_like(cm_scr)
        w_out_copy = pltpu.make_async_copy(w_out_hbm, w_out_stage, w_out_sem.at[0])
        w_out_copy.start()
        _run_interleaved(front())
        w_out_copy.wait()
        w_out_ref[...] = w_out_stage[...].astype(BF16)

    @pl.when(jnp.logical_and(k > 0, k < n_tiles))
    def _():
        _run_interleaved(back(), front())

    @pl.when(k == n_tiles)
    def _():
        _run_interleaved(back())


def _mixer_prompt(x, mod, w_in, w_out, conv_w, mconv_w, alog, dtb, norm_g, lng, lnb, *, alpha, nseq, tt, off, layer):
    bsz, t, d = x.shape
    assert t % tt == 0 and tt % LANES == 0 and bsz % nseq == 0
    nt = t // tt
    n_tiles = (bsz // nseq) * nt
    rows = nseq * tt
    qkv_dim = conv_w.shape[-1]
    sc_w = off["sc_c"][1] - off["sc_c"][0]
    dn_w = DN_HEADS * DN_DV
    n_conv, n_mconv = conv_w.shape[0], mconv_w.shape[0]
    assert max(n_conv, n_mconv) - 1 <= SUBLANES <= tt
    nchunk = rows // CHUNK
    kern = functools.partial(_mixer_prompt_kernel, alpha=alpha, nseq=nseq, tt=tt, tiles_per_group=nt,
                             n_tiles=n_tiles, off=off, n_conv=n_conv, n_mconv=n_mconv, layer=layer)

    def front_tile(k):
        return jnp.minimum(k, n_tiles - 1)

    def back_tile(k):
        return jnp.maximum(k - 1, 0)

    y, s_out, cq, cm, w_out_b = pl.pallas_call(
        kern,
        grid=(n_tiles + 1,),
        in_specs=[pl.BlockSpec((nseq, tt, d), lambda k: (front_tile(k) // nt, front_tile(k) % nt, 0)),
                  pl.BlockSpec((nseq, tt, d), lambda k: (back_tile(k) // nt, back_tile(k) % nt, 0)),
                  _sub_mod_spec(bsz, d, 1),
                  *[_const_spec(w.shape) for w in w_in], pl.BlockSpec(memory_space=pl.ANY),
                  _const_spec(conv_w.shape), _const_spec(mconv_w.shape),
                  pl.BlockSpec(memory_space=pltpu.SMEM), pl.BlockSpec(memory_space=pltpu.SMEM),
                  _const_spec(norm_g.shape), _const_spec(lng.shape), _const_spec(lnb.shape)],
        out_specs=[pl.BlockSpec((nseq, tt, d), lambda k: (back_tile(k) // nt, back_tile(k) % nt, 0)),
                   pl.BlockSpec((nseq, DN_HEADS, DN_DK, DN_DV), lambda k: (back_tile(k) // nt, 0, 0, 0)),
                   pl.BlockSpec((n_conv - 1, bsz, qkv_dim), lambda k: (0, 0, 0)),
                   pl.BlockSpec((nseq, n_mconv - 1, sc_w), lambda k: (front_tile(k) // nt, 0, 0)),
                   pl.BlockSpec(w_out.shape, lambda k: (0, 0))],
        out_shape=[jax.ShapeDtypeStruct((bsz, t, d), F32),
                   jax.ShapeDtypeStruct((bsz, DN_HEADS, DN_DK, DN_DV), F32),
                   jax.ShapeDtypeStruct((n_conv - 1, bsz, qkv_dim), F32),
                   jax.ShapeDtypeStruct((bsz, n_mconv - 1, sc_w), F32),
                   jax.ShapeDtypeStruct(w_out.shape, BF16)],
        scratch_shapes=[pltpu.VMEM((nseq * DN_HEADS, DN_DK, DN_DV), F32),
                        pltpu.VMEM((nseq, SUBLANES, qkv_dim), F32),
                        pltpu.VMEM((nseq, SUBLANES, sc_w), F32),
                        pltpu.VMEM((nchunk, SUBLANES, CHUNK), F32),
                        pltpu.VMEM((rows, dn_w), F32),
                        pltpu.VMEM((rows, dn_w), BF16),
                        pltpu.VMEM((nchunk * DN_HEADS, CHUNK, CHUNK), BF16),
                        pltpu.VMEM((rows, dn_w), BF16),
                        pltpu.VMEM((rows, dn_w), BF16),
                        pltpu.VMEM((nchunk * SUBLANES, dn_w), F32),
                        pltpu.VMEM((rows, dn_w), F32),
                        pltpu.VMEM((rows, sc_w), F32),
                        pltpu.VMEM(w_out.shape, F32),
                        pltpu.SemaphoreType.DMA((1,))],
        compiler_params=pltpu.CompilerParams(dimension_semantics=("arbitrary",),
                                             vmem_limit_bytes=V7X_VMEM_LIMIT),
        name="mixer_prompt",
    )(x, x, mod, *w_in, w_out, conv_w, mconv_w, alog, dtb, norm_g, lng, lnb)
    return y, s_out, jnp.swapaxes(cq, 0, 1), cm, w_out_b


_PACK = SUBLANES


def _sample_phase_a(x_ref, mod_ref, wq_ref, wr_ref, wab_ref, cw_ref, mw_ref, alog_ref, dtb_ref, cq_in_ref, cm_in_ref,
                    cq_out_ref, cm_out_ref, og_scr, osc_scr, lhs_scr, u_scr, kd_scr, gl_scr, qk_scr,
                    *, t_len, d, off, n_conv, n_mconv, layer):
    bsz = x_ref.shape[0]
    hk = DN_HEADS * DN_DK
    sc_w = off["sc_c"][1] - off["sc_c"][0]

    x = _stack_time(x_ref, t_len, d)
    u = (x * (1.0 + _mod_rows(mod_ref, 1, t_len, d)) + _mod_rows(mod_ref, 0, t_len, d)).astype(BF16)
    pq = jnp.dot(u, wq_ref[...], preferred_element_type=F32)
    ab = jnp.dot(u, wab_ref[...], preferred_element_type=F32)
    rest = jnp.dot(u, wr_ref[...], preferred_element_type=F32)
    p = {name: rest[:, lo:hi] for name, (lo, hi) in off.items()}

    def rows(arr, t):
        return arr[t * bsz:(t + 1) * bsz]

    seq = [cq_in_ref[j] for j in range(n_conv - 1)] + [rows(pq, t) for t in range(t_len)]
    act = []
    for t in range(t_len):
        acc = None
        for j in range(n_conv):
            term = seq[t + j] * cw_ref[j:j + 1, :]
            acc = term if acc is None else acc + term
        act.append(_silu(acc))
    for j in range(n_conv - 1):
        cq_out_ref[j] = seq[t_len + j]

    z = p["sc_c"] * p["sc_h"]
    zseq = ([cm_in_ref[:, j, :] for j in range(n_mconv - 1)]
            + [rows(z, t) for t in range(t_len)])
    zc = []
    for t in range(t_len):
        acc = None
        for j in range(n_mconv):
            term = zseq[t + j] * _row(mw_ref, j)
            acc = term if acc is None else acc + term
        zc.append(acc)
    for j in range(n_mconv - 1):
        cm_out_ref[:, j, :] = zseq[t_len + j]
    osc_scr[...] = p["sc_b"] * jnp.concatenate(zc, axis=0)
    og_scr[...] = p["og"]

    beta_all = jax.nn.sigmoid(ab)
    g_all = -jnp.exp(_lane_vector(alog_ref, layer)) * _softplus(ab + _lane_vector(dtb_ref, layer))
    zero = jnp.zeros((bsz, LANES), F32)

    for h in range(DN_HEADS):
        q = [_l2norm(a[:, h * DN_DK:(h + 1) * DN_DK]) * (DN_DK ** -0.5) for a in act]
        k = [_l2norm(a[:, hk + h * DN_DK:hk + (h + 1) * DN_DK]) for a in act]
        v = [a[:, 2 * hk + h * DN_DV:2 * hk + (h + 1) * DN_DV] for a in act]
        beta = [rows(beta_all, t)[:, DN_HEADS + h:DN_HEADS + h + 1] for t in range(t_len)]
        g = [rows(g_all, t)[:, h:h + 1] for t in range(t_len)]
        gc = [g[0]]
        for t in range(1, t_len):
            gc.append(gc[-1] + g[t])
        kb = [k[t] * beta[t] for t in range(t_len)]
        vb = [v[t] * beta[t] for t in range(t_len)]
        eg = [jnp.exp(gc[t]) for t in range(t_len)]
        kbe = [kb[t] * eg[t] for t in range(t_len)]
        lm = [[None] * t_len for _ in range(t_len)]
        n_qk = 0
        for i in range(t_len):
            for j in range(i + 1):
                dec = jnp.exp(gc[i] - gc[j])
                qk_scr[h, n_qk] = jnp.broadcast_to(_rowsum(q[i] * k[j]) * dec, (bsz, LANES))
                n_qk += 1
                if j < i:
                    lm[i][j] = _rowsum(kb[i] * k[j]) * dec
        tm = [[None] * t_len for _ in range(t_len)]
        for i in range(t_len):
            for j in range(i):
                acc = lm[i][j]
                for m in range(j + 1, i):
                    acc = acc + lm[i][m] * tm[m][j]
                tm[i][j] = -acc
        for i in range(t_len):
            ui, wi = vb[i], kbe[i]
            for j in range(i):
                ui = ui + tm[i][j] * vb[j]
                wi = wi + tm[i][j] * kbe[j]
            lhs_scr[h, pl.ds(i, bsz, stride=_PACK), :] = wi
            lhs_scr[h, pl.ds(t_len + i, bsz, stride=_PACK), :] = q[i] * eg[i]
            u_scr[h, pl.ds(i, bsz, stride=_PACK), :] = ui
            u_scr[h, pl.ds(t_len + i, bsz, stride=_PACK), :] = zero
            kd_scr[h, pl.ds(i, bsz, stride=_PACK), :] = k[i] * jnp.exp(gc[t_len - 1] - gc[i])
            kd_scr[h, pl.ds(t_len + i, bsz, stride=_PACK), :] = zero
        gl_scr[h] = jnp.broadcast_to(jnp.exp(gc[t_len - 1]), (bsz, LANES))


def _sample_phase_c(x_ref, mod_ref, w_out_ref, ng_ref, lng_ref, lnb_ref, y_ref,
                    og_scr, osc_scr, res_scr, vn_scr, qk_scr, *, alpha, t_len, d):
    bsz = x_ref.shape[0]
    sub = 1
    x = _stack_time(x_ref, t_len, d)
    gate = _mod_rows(mod_ref, 2, t_len, d)
    og = og_scr[...]
    per_t = []
    for i in range(t_len):
        heads = []
        for h in range(DN_HEADS):
            oi = res_scr[h, pl.ds(t_len + i, bsz, stride=_PACK), :]
            base = i * (i + 1) // 2
            for j in range(i + 1):
                oi = oi + qk_scr[h, base + j] * vn_scr[h, pl.ds(j, bsz, stride=_PACK), :]
            heads.append(_gate_norm(oi, og[i * bsz:(i + 1) * bsz, h * DN_DV:(h + 1) * DN_DV], ng_ref[...]))
        per_t.append(jnp.concatenate(heads, axis=1))
    o_dn = jnp.concatenate(per_t, axis=0)
    dn_w = o_dn.shape[1]
    mix = (jnp.dot(o_dn.astype(BF16), w_out_ref[0:dn_w, :], preferred_element_type=F32)
           + jnp.dot(osc_scr[...].astype(BF16), w_out_ref[dn_w:, :], preferred_element_type=F32))
    y = _post(x, mix, gate, _row(lng_ref, sub), _row(lnb_ref, sub), alpha)
    for t in range(t_len):
        y_ref[:, t * d:(t + 1) * d] = y[t * bsz:(t + 1) * bsz]


def _mixer_sample_kernel(x_ref, mod_ref, wq_ref, wr_ref, wab_ref, w_out_ref, cw_ref, mw_ref, alog_ref, dtb_ref, ng_ref,
                         lng_ref, lnb_ref, cq_in_ref, cm_in_ref, s_in_hbm,
                         y_ref, cq_out_ref, cm_out_ref, s_out_hbm,
                         og_scr, osc_scr, lhs_scr, u_scr, kd_scr, gl_scr, qk_scr, res_scr, vn_scr,
                         s_buf, sem_in, sem_out,
                         *, alpha, t_len, d, off, n_conv, n_mconv, bt, n_blocks, layer):
    step = pl.program_id(0)
    assert 2 * t_len == _PACK, "packed buffers hold [T rows | T rows] per sequence"
    n_slots = s_buf.shape[0]

    def fetch(block, slot):
        return pltpu.make_async_copy(s_in_hbm.at[pl.ds(block * bt, bt)], s_buf.at[slot], sem_in.at[slot])

    def write_back(block, slot):
        return pltpu.make_async_copy(s_buf.at[slot], s_out_hbm.at[pl.ds(block * bt, bt)], sem_out.at[slot])

    @pl.when(step == 0)
    def _():
        for b in range(min(n_slots, n_blocks)):
            fetch(b, b).start()
        _sample_phase_a(x_ref, mod_ref, wq_ref, wr_ref, wab_ref, cw_ref, mw_ref, alog_ref, dtb_ref, cq_in_ref, cm_in_ref,
                        cq_out_ref, cm_out_ref, og_scr, osc_scr, lhs_scr, u_scr, kd_scr, gl_scr, qk_scr,
                        t_len=t_len, d=d, off=off, n_conv=n_conv, n_mconv=n_mconv, layer=layer)

    top =(lax.broadcasted_iota(jnp.int32, (_PACK, LANES), 0) < t_len).astype(F32)

    chains = [(bl, h) for bl in range(bt) for h in range(DN_HEADS)]
    seq0 = step * bt
    row0 = [pl.multiple_of((seq0 + bl) * _PACK, _PACK) for bl in range(bt)]
    slot = lax.rem(step, n_slots)
    fetch(step, slot).wait()
    states = [s_buf[slot, bl, h] for bl, h in chains]
    r = [_mm(lhs_scr[h, pl.ds(row0[bl], _PACK), :], s) for (bl, h), s in zip(chains, states)]
    vns = []
    for (bl, h), r_u in zip(chains, r):
        res_scr[h, pl.ds(row0[bl], _PACK), :] = r_u
        vn = (u_scr[h, pl.ds(row0[bl], _PACK), :] - r_u) * top
        vn_scr[h, pl.ds(row0[bl], _PACK), :] = vn
        vns.append(vn)
    upd = [_mm_at(kd_scr[h, pl.ds(row0[bl], _PACK), :], vn) for (bl, h), vn in zip(chains, vns)]
    for (bl, h), s_add in zip(chains, upd):
        s_buf[slot, bl, h] = s_buf[slot, bl, h] * gl_scr[h, pl.ds(seq0 + bl, 1), :] + s_add
    write_back(step, slot).start()

    @pl.when(step > 0)
    def _():
        prev = step - 1
        prev_slot = lax.rem(prev, n_slots)
        write_back(prev, prev_slot).wait()

        @pl.when(prev + n_slots < n_blocks)
        def _():
            fetch(prev + n_slots, prev_slot).start()

    @pl.when(step == n_blocks - 1)
    def _():
        _sample_phase_c(x_ref, mod_ref, w_out_ref, ng_ref, lng_ref, lnb_ref, y_ref,
                        og_scr, osc_scr, res_scr, vn_scr, qk_scr, alpha=alpha, t_len=t_len, d=d)
        write_back(step, slot).wait()


def _mixer_sample(x, mod, w_in, w_out, conv_w, mconv_w, alog, dtb, norm_g, lng, lnb, s_in, cq_in, cm_in,
                  *, alpha, off, bt, t, layer):
    bsz, d = x.shape[0], x.shape[1] // t
    assert bsz % bt == 0
    sc_w = off["sc_c"][1] - off["sc_c"][0]
    dn_w = DN_HEADS * DN_DV
    n_conv, n_mconv = conv_w.shape[0], mconv_w.shape[0]
    n_qk = t * (t + 1) // 2
    x2 = x
    cq2 = jnp.swapaxes(cq_in, 0, 1)
    kern = functools.partial(_mixer_sample_kernel, alpha=alpha, t_len=t, d=d, off=off,
                             n_conv=n_conv, n_mconv=n_mconv, bt=bt, n_blocks=bsz // bt, layer=layer)
    consts = (*w_in, w_out, conv_w, mconv_w, alog, dtb, norm_g, lng, lnb, cq2, cm_in)
    state_spec = pl.BlockSpec(memory_space=pl.ANY)
    packed = pltpu.VMEM((DN_HEADS, _PACK * bsz, LANES), F32)
    y, cq, cm, s_out = pl.pallas_call(
        kern,
        grid=(bsz // bt,),
        in_specs=([_const_spec(x2.shape), _sub_mod_spec(bsz, d, 1)]
                  + [pl.BlockSpec(memory_space=pltpu.SMEM) if a is alog or a is dtb else _const_spec(a.shape)
                     for a in consts] + [state_spec]),
        out_specs=[pl.BlockSpec(x2.shape, lambda i: (0, 0)),
                   pl.BlockSpec(cq2.shape, lambda i: (0, 0, 0)),
                   pl.BlockSpec(cm_in.shape, lambda i: (0, 0, 0)),
                   state_spec],
        out_shape=[jax.ShapeDtypeStruct(x2.shape, F32),
                   jax.ShapeDtypeStruct(cq2.shape, F32),
                   jax.ShapeDtypeStruct(cm_in.shape, F32),
                   jax.ShapeDtypeStruct(s_in.shape, F32)],
        scratch_shapes=[pltpu.VMEM((t * bsz, dn_w), F32),
                        pltpu.VMEM((t * bsz, sc_w), F32),
                        packed,
                        packed,
                        packed,
                        pltpu.VMEM((DN_HEADS, bsz, LANES), F32),
                        pltpu.VMEM((DN_HEADS, n_qk, bsz, LANES), F32),
                        packed,
                        packed,
                        pltpu.VMEM((STATE_SLOTS, bt, DN_HEADS, DN_DK, DN_DV), F32),
                        pltpu.SemaphoreType.DMA((STATE_SLOTS,)),
                        pltpu.SemaphoreType.DMA((STATE_SLOTS,))],
        compiler_params=pltpu.CompilerParams(dimension_semantics=("arbitrary",),
                                             vmem_limit_bytes=V7X_VMEM_LIMIT),
        name="mixer_sample",
    )(x2, mod, *consts, s_in)
    return y, s_out, jnp.swapaxes(cq, 0, 1), cm


def kernel(x_prompt, x_sample, state_ssm, state_conv_qkv, state_conv_mix, c_prompt, c_sample, w_ada, b_ada, ln_g, ln_b, ffn1_wg, ffn1_wu, ffn1_wd, ffn2_wg, ffn2_wu, ffn2_wd, w_in, conv_qkv_w, a_log, dt_bias, dn_norm_g, conv_mix_w, w_out):
    depth = w_ada.shape[0]
    alpha = (2 * depth) ** 0.25
    bp, tp, d = x_prompt.shape
    qkv_dim = conv_qkv_w.shape[-1]
    sc_width = conv_mix_w.shape[-1]
    dn_width = DN_HEADS * DN_DV
    off = _proj_layout(dn_width, sc_width)
    tm = min(512, tp)
    nseq = 2 if bp % 2 == 0 else 1
    tt = min(512 // nseq, tp)
    bt = min(16, x_sample.shape[0])
    ts = x_sample.shape[1]

    hp, hs = x_prompt, x_sample
    outs = [[] for _ in range(6)]
    for l in range(depth):
        mod_p, mod_s, *w_in_p = _adaln(c_prompt, c_sample, w_ada[l], b_ada[l].reshape(1, -1),
                                       jnp.swapaxes(w_in[l], 0, 1), qkv_dim)
        lng, lnb, mconv_w = ln_g[l][:, None, :], ln_b[l][:, None, :], conv_mix_w[l][:, None, :]
        w1 = (ffn1_wg[l], ffn1_wu[l], ffn1_wd[l])
        w2 = (ffn2_wg[l], ffn2_wu[l], ffn2_wd[l])
        norm_g = dn_norm_g[l].reshape(1, -1)
        mix_args = (conv_qkv_w[l], mconv_w, a_log, dt_bias, norm_g, lng, lnb)

        hp, hs = _ffn(hp, hs, mod_p, mod_s, *w1, lng, lnb, sub=0, alpha=alpha, tm=tm, ts=ts,
                      ys_3d=False)
        hp, a1, a2, a3, w_out_b = _mixer_prompt(hp, mod_p, w_in_p, w_out[l], *mix_args, alpha=alpha,
                                                nseq=nseq, tt=tt, off=off, layer=l)
        hs, b1, b2, b3 = _mixer_sample(hs, mod_s, w_in_p, w_out_b, *mix_args, state_ssm[l], state_conv_qkv[l],
                                       state_conv_mix[l], alpha=alpha, off=off, bt=bt, t=ts, layer=l)
        hp, hs = _ffn(hp, hs, mod_p, mod_s, *w2, lng, lnb, sub=2, alpha=alpha, tm=tm, ts=ts,
                      ys_3d=True)
        for lst, val in zip(outs, (a1, a2, a3, b1, b2, b3)):
            lst.append(val)
    return (hp, hs) + tuple(jnp.stack(o) for o in outs)
```

```python
import functools

import jax
import jax.numpy as jnp
from jax import lax
from jax.experimental import pallas as pl
from jax.experimental.pallas import tpu as pltpu

F32 = jnp.float32
BF16 = jnp.bfloat16

LN_EPS = 1e-5
RMS_EPS = 1e-6
N_SUB = 3
DN_HEADS = 4
DN_DK = 128
DN_DV = 128
CHUNK = 64
LANES = 128
SUBLANES = 8
AB_PAD = LANES

V7X_VMEM_LIMIT = 60000 * 1024
CAST_BUFFERS = 3
FFN_CHUNK = 256
STATE_SLOTS = 4


def _mm(a, b):
    return jnp.dot(a.astype(BF16), b.astype(BF16), preferred_element_type=F32)


def _mm_at(a, b):
    return lax.dot_general(a.astype(BF16), b.astype(BF16), (((0,), (0,)), ((), ())),
                           preferred_element_type=F32)


def _mm_f32(a, b):
    return jnp.dot(a, b, preferred_element_type=F32, precision=lax.Precision.HIGHEST)


def _silu(x):
    return x * jax.nn.sigmoid(x)


def _softplus(x):
    return jnp.maximum(x, 0.0) + jnp.log1p(jnp.exp(-jnp.abs(x)))


def _layer_norm(y, g, b):
    mu = jnp.mean(y, axis=-1, keepdims=True)
    yc = y - mu
    var = jnp.mean(yc * yc, axis=-1, keepdims=True)
    return yc * lax.rsqrt(var + LN_EPS) * g + b


def _post(x, delta, gate, g, b, alpha):
    return _layer_norm(alpha * x + gate * delta, g, b)


def _rowsum(x):
    return jnp.sum(x, axis=-1, keepdims=True)


def _l2norm(x):
    return x * lax.rsqrt(_rowsum(x * x) + RMS_EPS)


def _const_spec(shape):
    nd = len(shape)
    return pl.BlockSpec(shape, lambda *_: (0,) * nd, pipeline_mode=pl.Buffered(1))


def _lane_vector(smem_ref, layer):
    lane = lax.broadcasted_iota(jnp.int32, (1, AB_PAD), 1)
    out = jnp.zeros((1, AB_PAD), F32)
    for h in range(smem_ref.shape[1]):
        out = jnp.where(lane == h, smem_ref[layer, h], out)
    return out


def _row(ref, j):
    return ref[j] if len(ref.shape) == 3 else ref[j:j + 1, :]


def _adaln_kernel(cp_ref, cs_ref, w_ref, b_ref, win_ref, op_ref, os_ref, wq_ref, wr_ref, wab_ref, *, qkv_dim):
    mp = cp_ref.shape[0]
    c = jnp.concatenate([cp_ref[...], cs_ref[...]], axis=0)
    mod = _mm(_silu(c), w_ref[...]) + b_ref[...]
    op_ref[...] = mod[:mp]
    os_ref[...] = mod[mp:]

    gates = 2 * DN_HEADS
    wq_ref[...] = win_ref[0:qkv_dim, :].T.astype(BF16)
    wr_ref[...] = win_ref[qkv_dim + gates:, :].T.astype(BF16)
    ab_rows = jnp.concatenate([win_ref[qkv_dim:qkv_dim + gates, :],
                               jnp.zeros((AB_PAD - gates, win_ref.shape[1]), F32)], axis=0)
    wab_ref[...] = ab_rows.T.astype(BF16)


def _adaln(c_p, c_s, w, b, w_in_t, qkv_dim):
    (mp, d), ms = c_p.shape, c_s.shape[0]
    n = w.shape[1]
    proj = w_in_t.shape[0]
    rest = proj - qkv_dim - 2 * DN_HEADS
    assert qkv_dim % LANES == 0 and rest % LANES == 0 and (2 * DN_HEADS) % SUBLANES == 0
    steps = max(g for g in (1, 2, 4) if n % (g * LANES) == 0 and d % (g * LANES) == 0)
    tn, rows = n // steps, d // steps
    return pl.pallas_call(
        functools.partial(_adaln_kernel, qkv_dim=qkv_dim),
        grid=(steps,),
        in_specs=[_const_spec((mp, d)), _const_spec((ms, d)),
                  pl.BlockSpec((d, tn), lambda j: (0, j)),
                  pl.BlockSpec((1, tn), lambda j: (0, j)),
                  pl.BlockSpec((proj, rows), lambda j: (0, j))],
        out_specs=[pl.BlockSpec((mp, tn), lambda j: (0, j)), pl.BlockSpec((ms, tn), lambda j: (0, j)),
                   pl.BlockSpec((rows, qkv_dim), lambda j: (j, 0)),
                   pl.BlockSpec((rows, rest), lambda j: (j, 0)),
                   pl.BlockSpec((rows, AB_PAD), lambda j: (j, 0))],
        out_shape=[jax.ShapeDtypeStruct((mp, n), F32), jax.ShapeDtypeStruct((ms, n), F32),
                   jax.ShapeDtypeStruct((d, qkv_dim), BF16), jax.ShapeDtypeStruct((d, rest), BF16),
                   jax.ShapeDtypeStruct((d, AB_PAD), BF16)],
        compiler_params=pltpu.CompilerParams(dimension_semantics=("arbitrary",),
                                             vmem_limit_bytes=V7X_VMEM_LIMIT),
        name="adaln",
    )(c_p, c_s, w, b, w_in_t)


def _swiglu(x, shift, scale, wg_ref, wu_ref, wd_ref):
    u = (x * (1.0 + scale) + shift).astype(BF16)
    hg = jnp.dot(u, wg_ref[...], preferred_element_type=F32)
    hu = jnp.dot(u, wu_ref[...], preferred_element_type=F32)
    h = (_silu(hg) * hu).astype(BF16)
    return jnp.dot(h, wd_ref[...], preferred_element_type=F32)


def _stack_time(ref, t_len, d):
    if len(ref.shape) == 3:
        return jnp.concatenate([ref[:, t, :] for t in range(t_len)], axis=0)
    return jnp.concatenate([ref[:, t * d:(t + 1) * d] for t in range(t_len)], axis=0)


def _mod_rows(mod_ref, idx, t_len, d):
    m = mod_ref[:, idx * d:(idx + 1) * d]
    return jnp.concatenate([m] * t_len, axis=0)


def _sub_mod_spec(bsz, d, sub):
    return pl.BlockSpec((bsz, 3 * d), lambda *_: (0, sub), pipeline_mode=pl.Buffered(1))


def _first_tile_swiglu(x, shift, scale, w_hbm, w_scr, stages, sems):
    wg_hbm, wu_hbm, wd_hbm = w_hbm
    wg_scr, wu_scr, wd_scr = w_scr
    nbuf = stages[0].shape[0]
    cw = stages[0].shape[2]
    n = wg_hbm.shape[1] // cw

    def copies(j):
        slot = j % nbuf
        cols = pl.ds(j * cw, cw)
        return (pltpu.make_async_copy(wg_hbm.at[:, cols], stages[0].at[slot], sems[0].at[slot]),
                pltpu.make_async_copy(wu_hbm.at[:, cols], stages[1].at[slot], sems[1].at[slot]),
                pltpu.make_async_copy(wd_hbm.at[cols, :], stages[2].at[slot], sems[2].at[slot]))

    queue = [copies(j) for j in range(n)]
    ahead = nbuf - 1
    for group in queue[:ahead]:
        for copy in group:
            copy.start()
    u = (x * (1.0 + scale) + shift).astype(BF16)
    acc = None
    for j, group in enumerate(queue):
        if j + ahead < n:
            for copy in queue[j + ahead]:
                copy.start()
        for copy in group:
            copy.wait()
        slot, cols = j % nbuf, slice(j * cw, (j + 1) * cw)
        wg_scr[:, cols] = stages[0][slot].astype(BF16)
        wu_scr[:, cols] = stages[1][slot].astype(BF16)
        wd_scr[cols, :] = stages[2][slot].astype(BF16)
        hg = jnp.dot(u, wg_scr[:, cols], preferred_element_type=F32)
        hu = jnp.dot(u, wu_scr[:, cols], preferred_element_type=F32)
        part = jnp.dot((_silu(hg) * hu).astype(BF16), wd_scr[cols, :], preferred_element_type=F32)
        acc = part if acc is None else acc + part
    return acc


def _ffn_kernel(xp_ref, modp_ref, xs_ref, mods_ref, wg_hbm, wu_hbm, wd_hbm, lng_ref, lnb_ref,
                yp_ref, ys_ref, wg_scr, wu_scr, wd_scr, d_scr, xq_scr, stage_g, stage_u, stage_d, sem_g, sem_u, sem_d,
                *, sub, alpha, n_prompt, per_seq, t_len, d):
    k = pl.program_id(0)
    g, b = _row(lng_ref, sub), _row(lnb_ref, sub)
    seq_now = jnp.minimum(k, n_prompt - 1) // per_seq
    seq_prev = jnp.maximum(k - 1, 0) // per_seq

    def mod_row(seq, idx):
        return modp_ref[pl.ds(seq, 1), idx * d:(idx + 1) * d]

    def matmuls():
        return _swiglu(xp_ref[...], mod_row(seq_now, 0), mod_row(seq_now, 1), wg_scr, wu_scr, wd_scr)

    def finish(d_prev):
        yp_ref[...] = _post(xq_scr[...], 0.5 * d_prev, mod_row(seq_prev, 2), g, b, alpha)

    @pl.when(k == 0)
    def _():
        d_scr[...] = _first_tile_swiglu(
            xp_ref[...], mod_row(0, 0), mod_row(0, 1),
            (wg_hbm, wu_hbm, wd_hbm), (wg_scr, wu_scr, wd_scr), (stage_g, stage_u, stage_d),
            (sem_g, sem_u, sem_d))
        xq_scr[...] = xp_ref[...]

    @pl.when(jnp.logical_and(k > 0, k < n_prompt))
    def _():
        d_prev = d_scr[...]
        d_new = matmuls()
        finish(d_prev)
        d_scr[...] = d_new
        xq_scr[...] = xp_ref[...]

    @pl.when(k == n_prompt)
    def _():
        finish(d_scr[...])
        bsz = xs_ref.shape[0]
        x = _stack_time(xs_ref, t_len, d)
        delta = _swiglu(x, _mod_rows(mods_ref, 0, t_len, d), _mod_rows(mods_ref, 1, t_len, d),
                        wg_scr, wu_scr, wd_scr)
        y = _post(x, 0.5 * delta, _mod_rows(mods_ref, 2, t_len, d), g, b, alpha)
        for t in range(t_len):
            if len(ys_ref.shape) == 3:
                ys_ref[:, t, :] = y[t * bsz:(t + 1) * bsz]
            else:
                ys_ref[:, t * d:(t + 1) * d] = y[t * bsz:(t + 1) * bsz]


def _ffn(xp, xs, mod_p, mod_s, wg, wu, wd, lng, lnb, *, sub, alpha, tm, ts, ys_3d):
    bp, tp, d = xp.shape
    bs = xs.shape[0]
    ys_shape = (bs, ts, d) if ys_3d else (bs, ts * d)
    f = wg.shape[1]
    assert tp % tm == 0
    per_seq = tp // tm
    n_prompt = bp * per_seq
    assert f % FFN_CHUNK == 0
    last = n_prompt - 1
    yp, ys = pl.pallas_call(
        functools.partial(_ffn_kernel, sub=sub, alpha=alpha, n_prompt=n_prompt, per_seq=per_seq, t_len=ts, d=d),
        grid=(n_prompt + 1,),
        in_specs=[pl.BlockSpec((tm, d), lambda k: (jnp.minimum(k, last), 0)),
                  _sub_mod_spec(bp, d, sub),
                  _const_spec(xs.shape), _sub_mod_spec(bs, d, sub),
                  pl.BlockSpec(memory_space=pl.ANY), pl.BlockSpec(memory_space=pl.ANY),
                  pl.BlockSpec(memory_space=pl.ANY),
                  _const_spec(lng.shape), _const_spec(lnb.shape)],
        out_specs=[pl.BlockSpec((tm, d), lambda k: (jnp.maximum(k - 1, 0), 0)),
                   pl.BlockSpec(ys_shape, lambda k: (0,) * len(ys_shape))],
        out_shape=[jax.ShapeDtypeStruct((bp * tp, d), F32),
                   jax.ShapeDtypeStruct(ys_shape, F32)],
        scratch_shapes=[pltpu.VMEM((d, f), BF16), pltpu.VMEM((d, f), BF16), pltpu.VMEM((f, d), BF16),
                        pltpu.VMEM((tm, d), F32), pltpu.VMEM((tm, d), F32),
                        pltpu.VMEM((CAST_BUFFERS, d, FFN_CHUNK), F32), pltpu.VMEM((CAST_BUFFERS, d, FFN_CHUNK), F32),
                        pltpu.VMEM((CAST_BUFFERS, FFN_CHUNK, d), F32),
                        pltpu.SemaphoreType.DMA((CAST_BUFFERS,)), pltpu.SemaphoreType.DMA((CAST_BUFFERS,)),
                        pltpu.SemaphoreType.DMA((CAST_BUFFERS,))],
        compiler_params=pltpu.CompilerParams(dimension_semantics=("arbitrary",),
                                             vmem_limit_bytes=V7X_VMEM_LIMIT),
        name=f"ffn{sub}",
    )(xp.reshape(bp * tp, d), mod_p, xs, mod_s, wg, wu, wd, lng, lnb)
    return yp.reshape(bp, tp, d), ys


def _tri_inverse_steps(lmats, c):
    ls = [l.astype(BF16) for l in lmats]
    ms = [jnp.dot(l, l, preferred_element_type=F32) for l in ls]
    yield
    rows = lax.broadcasted_iota(jnp.int32, (c, c), 0)
    cols = lax.broadcasted_iota(jnp.int32, (c, c), 1)
    eye = (rows == cols).astype(F32)
    ps = [eye - l for l in lmats]
    power = 2
    while 2 * power < c:
        mp = [jnp.concatenate([m, p], axis=0).astype(BF16) for m, p in zip(ms, ps)]
        sq = [jnp.dot(x, x[:c], preferred_element_type=F32) for x in mp]
        yield
        ms = [x[:c] for x in sq]
        ps = [p + x[c:] for p, x in zip(ps, sq)]
        power *= 2
    pm = [jnp.dot(p.astype(BF16), m.astype(BF16), preferred_element_type=F32) for p, m in zip(ps, ms)]
    yield
    return [p + x for p, x in zip(ps, pm)]


def _gdn_local_steps(units, tril, strict):
    c = units[0][1].shape[0]
    a = [lax.dot_general(jnp.concatenate([kb, q], axis=0), k, (((1,), (1,)), ((), ())),
                         preferred_element_type=F32)
         for kb, q, k, _, _, _, _ in units]
    yield
    lmats, qks = [], []
    for a_u, (_, _, _, _, _, gcol, grow) in zip(a, units):
        decay = jnp.where(tril, jnp.exp(jnp.where(tril, gcol - grow, 0.0)), 0.0)
        lmats.append(jnp.where(strict, a_u[:c] * decay, 0.0))
        qks.append(jnp.where(tril, a_u[c:] * decay, 0.0))
    tms = yield from _tri_inverse_steps(lmats, c)
    uw = [jnp.dot(tm.astype(BF16), jnp.concatenate([vb, kbe], axis=1), preferred_element_type=F32)
          for tm, (_, _, _, vb, kbe, _, _) in zip(tms, units)]
    yield
    return [(x[:, :DN_DV], x[:, DN_DV:], qk) for x, qk in zip(uw, qks)]


def _alternate(primary, secondary):
    result = None
    done = False
    while not done:
        try:
            next(primary)
            yield
        except StopIteration as stop:
            result, done = stop.value, True
        if secondary is not None:
            try:
                next(secondary)
                yield
            except StopIteration:
                secondary = None
    if secondary is not None:
        yield from secondary
    return result


def _run_interleaved(*gens):
    live = list(gens)
    while live:
        for g in list(live):
            try:
                next(g)
            except StopIteration:
                live.remove(g)


def _gate_norm(o, og, norm_g):
    o = o * lax.rsqrt(jnp.mean(o * o, axis=-1, keepdims=True) + RMS_EPS) * norm_g
    return o * _silu(og)


def _proj_layout(dn_width, sc_width):
    off = {}
    pos = 0
    for name, width in (("og", dn_width), ("sc_b", sc_width), ("sc_c", sc_width), ("sc_h", sc_width)):
        off[name] = (pos, pos + width)
        pos += width
    return off


def _mixer_prompt_kernel(xf_ref, mod_ref, wq_ref, wr_ref, wab_ref, w_out_hbm, cw_ref,
                         mw_ref, alog_ref, dtb_ref, ng_ref, lng_ref, lnb_ref,
                         y_ref, s_out_ref, cq_out_ref, cm_out_ref, w_out_ref,
                         s_scr, cq_scr, cm_scr, gct_scr,
                         u_scr, w_scr, qk_scr, qd_scr, kd_scr, gl_scr, og_scr, osc_scr, xb_scr, w_out_stage, w_out_sem,
                         *, alpha, nseq, tt, tiles_per_group, n_tiles, off, n_conv, n_mconv, layer):
    k = pl.program_id(0)
    d = xf_ref.shape[-1]
    seq_front = (jnp.minimum(k, n_tiles - 1) // tiles_per_group) * nseq
    seq_back = (jnp.maximum(k - 1, 0) // tiles_per_group) * nseq

    def mod_row(seq, idx):
        return mod_ref[pl.ds(seq, 1), idx * d:(idx + 1) * d]

    rows_all = nseq * tt
    nchunk = rows_all // CHUNK
    per_seq = tt // CHUNK
    hk = DN_HEADS * DN_DK
    sub = 1
    head_lanes = [slice(h * DN_DK, (h + 1) * DN_DK) for h in range(DN_HEADS)]
    t_front = lax.rem(jnp.minimum(k, n_tiles - 1), tiles_per_group)
    t_back = lax.rem(jnp.maximum(k - 1, 0), tiles_per_group)
    keep_front = jnp.where(t_front == 0, 0.0, 1.0).astype(F32)
    keep_back = jnp.where(t_back == 0, 0.0, 1.0).astype(F32)

    def causal_conv(new_rows, hist_scr, s, w_ref, taps):
        ext = jnp.concatenate([hist_scr[s] * keep_front, new_rows], axis=0)
        acc = new_rows * _row(w_ref, taps - 1)
        for j in range(taps - 1):
            acc = acc + pltpu.roll(ext, taps - 1 - j, axis=0)[SUBLANES:] * _row(w_ref, j)
        hist_scr[s] = new_rows[tt - SUBLANES:]
        return acc

    def front():
        u = jnp.concatenate(
            [(xf_ref[s] * (1.0 + mod_row(seq_front + s, 1))
              + mod_row(seq_front + s, 0)).astype(BF16) for s in range(nseq)], axis=0)
        ab = jnp.dot(u, wab_ref[...], preferred_element_type=F32)
        pq = jnp.dot(u, wq_ref[...], preferred_element_type=F32)
        yield

        beta_all = jax.nn.sigmoid(ab)
        g = -jnp.exp(_lane_vector(alog_ref, layer)) * _softplus(ab + _lane_vector(dtb_ref, layer))
        rows = lax.broadcasted_iota(jnp.int32, (CHUNK, CHUNK), 0)
        cols = lax.broadcasted_iota(jnp.int32, (CHUNK, CHUNK), 1)
        tril = rows >= cols
        strict = rows > cols
        tril_f = tril.astype(F32)
        gc_parts, gl_parts = [], []
        for c in range(nchunk):
            gc_c = _mm_f32(tril_f, g[c * CHUNK:(c + 1) * CHUNK, :])
            gc_parts.append(gc_c)
            gl_parts.append(jnp.broadcast_to(gc_c[CHUNK - 1:CHUNK, :], (CHUNK, AB_PAD)))
        yield
        p = {}

        def project_rest(names):
            for name in names:
                lo, hi = off[name]
                p[name] = jnp.dot(u, wr_ref[:, lo:hi], preferred_element_type=F32)
                yield

        yield from project_rest(("sc_c", "sc_h"))
        gc = jnp.concatenate(gc_parts, axis=0)
        gl_rows = jnp.concatenate(gl_parts, axis=0)
        gc_t = gc.T
        for c in range(nchunk):
            gct_scr[c] = gc_t[0:SUBLANES, c * CHUNK:(c + 1) * CHUNK]

        acts = []
        for s in range(nseq):
            pq_s = pq[s * tt:(s + 1) * tt]
            acts.append(_silu(causal_conv(pq_s, cq_scr, s, cw_ref, n_conv)))
            for j in range(n_conv - 1):
                row = tt - (n_conv - 1) + j
                cq_out_ref[j, pl.ds(seq_front + s, 1), :] = pq_s[row:row + 1]
        act = jnp.concatenate(acts, axis=0)

        eg_all = jnp.exp(gc)
        ekd_all = jnp.exp(gl_rows - gc)
        egl_all = jnp.exp(gl_rows)
        per_head = []
        for h, lanes in enumerate(head_lanes):
            q = _l2norm(act[:, h * DN_DK:(h + 1) * DN_DK]) * (DN_DK ** -0.5)
            kk = _l2norm(act[:, hk + h * DN_DK:hk + (h + 1) * DN_DK])
            v = act[:, 2 * hk + h * DN_DV:2 * hk + (h + 1) * DN_DV]
            beta = beta_all[:, DN_HEADS + h:DN_HEADS + h + 1]
            eg = eg_all[:, h:h + 1]
            kb = kk * beta
            per_head.append((kb.astype(BF16), q.astype(BF16), kk.astype(BF16), (v * beta).astype(BF16),
                             (kb * eg).astype(BF16)))
            qd_scr[:, lanes] = (q * eg).astype(BF16)
            kd_scr[:, lanes] = (kk * ekd_all[:, h:h + 1]).astype(BF16)
            for c in range(nchunk):
                gl_scr[c * SUBLANES:(c + 1) * SUBLANES, lanes] = jnp.broadcast_to(
                    egl_all[c * CHUNK:c * CHUNK + SUBLANES, h:h + 1], (SUBLANES, DN_DV))

        z = p["sc_c"] * p["sc_h"]
        zcs = []
        for s in range(nseq):
            z_s = z[s * tt:(s + 1) * tt]
            zcs.append(causal_conv(z_s, cm_scr, s, mw_ref, n_mconv))
            cm_out_ref[s] = z_s[tt - (n_mconv - 1):]

        where = [(c, h, lanes) for c in range(nchunk) for h, lanes in enumerate(head_lanes)]
        units = [tuple(arr[c * CHUNK:(c + 1) * CHUNK] for arr in per_head[h])
                 + (gc[c * CHUNK:(c + 1) * CHUNK, h:h + 1], gct_scr[c, h:h + 1, :])
                 for c, h, lanes in where]
        local = yield from _alternate(_gdn_local_steps(units, tril, strict), project_rest(("sc_b", "og")))
        osc_scr[...] = p["sc_b"] * jnp.concatenate(zcs, axis=0)
        og_scr[...] = p["og"]
        for (c, h, lanes), (u_c, w_c, qk_c) in zip(where, local):
            u_scr[c * CHUNK:(c + 1) * CHUNK, lanes] = u_c
            w_scr[c * CHUNK:(c + 1) * CHUNK, lanes] = w_c.astype(BF16)
            qk_scr[c * DN_HEADS + h] = qk_c.astype(BF16)

    def back():
        u_all, w_all, qd_all, kd_all = u_scr[...], w_scr[...], qd_scr[...], kd_scr[...]
        gl_all, og, o_sc = gl_scr[...], og_scr[...], osc_scr[...]
        qks = [qk_scr[i] for i in range(nchunk * DN_HEADS)]
        gates =[mod_row(seq_back + s, 2) for s in range(nseq)]
        states = [s_scr[i] * keep_back for i in range(nseq * DN_HEADS)]
        o_rows = [[None] * DN_HEADS for _ in range(nchunk)]
        for j in range(per_seq):
            chains = [(s, s * per_seq + j, h, lanes) for s in range(nseq) for h, lanes in enumerate(head_lanes)]
            rws = [slice(c * CHUNK, (c + 1) * CHUNK) for _, c, _, _ in chains]
            r = [jnp.dot(jnp.concatenate([w_all[rw, lanes], qd_all[rw, lanes]], axis=0),
                         states[s * DN_HEADS + h].astype(BF16), preferred_element_type=F32)
                 for (s, c, h, lanes), rw in zip(chains, rws)]
            yield
            v_new = [(u_all[rw, lanes] - r_u[:CHUNK]).astype(BF16)
                     for (s, c, h, lanes), rw, r_u in zip(chains, rws, r)]
            for (s, c, h, lanes), r_u, v_u in zip(chains, r, v_new):
                o_rows[c][h] = r_u[CHUNK:] + jnp.dot(qks[c * DN_HEADS + h], v_u, preferred_element_type=F32)
            for (s, c, h, lanes), rw, v_u in zip(chains, rws, v_new):
                i = s * DN_HEADS + h
                states[i] = (states[i] * gl_all[c * SUBLANES:c * SUBLANES + 1, lanes]
                             + lax.dot_general(kd_all[rw, lanes], v_u, (((0,), (0,)), ((), ())),
                                               preferred_element_type=F32))
            yield
        for i, st in enumerate(states):
            s_scr[i] = st
        for s in range(nseq):
            for h in range(DN_HEADS):
                s_out_ref[s, h] = states[s * DN_HEADS + h]

        o_dn = jnp.concatenate(
            [_gate_norm(jnp.concatenate([o_rows[c][h] for c in range(nchunk)], axis=0),
                        og[:, lanes], ng_ref[...]) for h, lanes in enumerate(head_lanes)], axis=1)
        dn_w = o_dn.shape[1]
        mix = (jnp.dot(o_dn.astype(BF16), w_out_ref[0:dn_w, :], preferred_element_type=F32)
               + jnp.dot(o_sc.astype(BF16), w_out_ref[dn_w:, :], preferred_element_type=F32))
        yield
        for s in range(nseq):
            y_ref[s] = _post(xb_scr[s], mix[s * tt:(s + 1) * tt], gates[s],
                             _row(lng_ref, sub), _row(lnb_ref, sub), alpha)

    @pl.when(k == 0)
    def _():
        s_scr[...] = jnp.zeros_like(s_scr)
        cq_scr[...] = jnp.zeros_like(cq_scr)
        cm_scr[...] = jnp.zeros_like(cm_scr)
        w_out_copy = pltpu.make_async_copy(w_out_hbm, w_out_stage, w_out_sem.at[0])
        w_out_copy.start()
        _run_interleaved(front())
        w_out_copy.wait()
        w_out_ref[...] = w_out_stage[...].astype(BF16)
        xb_scr[...] = xf_ref[...]

    @pl.when(jnp.logical_and(k > 0, k < n_tiles))
    def _():
        _run_interleaved(back(), front())
        xb_scr[...] = xf_ref[...]

    @pl.when(k == n_tiles)
    def _():
        _run_interleaved(back())


def _mixer_prompt(x, mod, w_in, w_out, conv_w, mconv_w, alog, dtb, norm_g, lng, lnb, *, alpha, nseq, tt, off, layer):
    bsz, t, d = x.shape
    assert t % tt == 0 and tt % LANES == 0 and bsz % nseq == 0
    nt = t // tt
    n_tiles = (bsz // nseq) * nt
    rows = nseq * tt
    qkv_dim = conv_w.shape[-1]
    sc_w = off["sc_c"][1] - off["sc_c"][0]
    dn_w = DN_HEADS * DN_DV
    n_conv, n_mconv = conv_w.shape[0], mconv_w.shape[0]
    assert max(n_conv, n_mconv) - 1 <= SUBLANES <= tt
    nchunk = rows // CHUNK
    kern = functools.partial(_mixer_prompt_kernel, alpha=alpha, nseq=nseq, tt=tt, tiles_per_group=nt,
                             n_tiles=n_tiles, off=off, n_conv=n_conv, n_mconv=n_mconv, layer=layer)

    def front_tile(k):
        return jnp.minimum(k, n_tiles - 1)

    def back_tile(k):
        return jnp.maximum(k - 1, 0)

    y, s_out, cq, cm, w_out_b = pl.pallas_call(
        kern,
        grid=(n_tiles + 1,),
        in_specs=[pl.BlockSpec((nseq, tt, d), lambda k: (front_tile(k) // nt, front_tile(k) % nt, 0)),
                  _sub_mod_spec(bsz, d, 1),
                  *[_const_spec(w.shape) for w in w_in], pl.BlockSpec(memory_space=pl.ANY),
                  _const_spec(conv_w.shape), _const_spec(mconv_w.shape),
                  pl.BlockSpec(memory_space=pltpu.SMEM), pl.BlockSpec(memory_space=pltpu.SMEM),
                  _const_spec(norm_g.shape), _const_spec(lng.shape), _const_spec(lnb.shape)],
        out_specs=[pl.BlockSpec((nseq, tt, d), lambda k: (back_tile(k) // nt, back_tile(k) % nt, 0)),
                   pl.BlockSpec((nseq, DN_HEADS, DN_DK, DN_DV), lambda k: (back_tile(k) // nt, 0, 0, 0)),
                   pl.BlockSpec((n_conv - 1, bsz, qkv_dim), lambda k: (0, 0, 0)),
                   pl.BlockSpec((nseq, n_mconv - 1, sc_w), lambda k: (front_tile(k) // nt, 0, 0)),
                   pl.BlockSpec(w_out.shape, lambda k: (0, 0))],
        out_shape=[jax.ShapeDtypeStruct((bsz, t, d), F32),
                   jax.ShapeDtypeStruct((bsz, DN_HEADS, DN_DK, DN_DV), F32),
                   jax.ShapeDtypeStruct((n_conv - 1, bsz, qkv_dim), F32),
                   jax.ShapeDtypeStruct((bsz, n_mconv - 1, sc_w), F32),
                   jax.ShapeDtypeStruct(w_out.shape, BF16)],
        scratch_shapes=[pltpu.VMEM((nseq * DN_HEADS, DN_DK, DN_DV), F32),
                        pltpu.VMEM((nseq, SUBLANES, qkv_dim), F32),
                        pltpu.VMEM((nseq, SUBLANES, sc_w), F32),
                        pltpu.VMEM((nchunk, SUBLANES, CHUNK), F32),
                        pltpu.VMEM((rows, dn_w), F32),
                        pltpu.VMEM((rows, dn_w), BF16),
                        pltpu.VMEM((nchunk * DN_HEADS, CHUNK, CHUNK), BF16),
                        pltpu.VMEM((rows, dn_w), BF16),
                        pltpu.VMEM((rows, dn_w), BF16),
                        pltpu.VMEM((nchunk * SUBLANES, dn_w), F32),
                        pltpu.VMEM((rows, dn_w), F32),
                        pltpu.VMEM((rows, sc_w), F32),
                        pltpu.VMEM((nseq, tt, d), F32),
                        pltpu.VMEM(w_out.shape, F32),
                        pltpu.SemaphoreType.DMA((1,))],
        compiler_params=pltpu.CompilerParams(dimension_semantics=("arbitrary",),
                                             vmem_limit_bytes=V7X_VMEM_LIMIT),
        name="mixer_prompt",
    )(x, mod, *w_in, w_out, conv_w, mconv_w, alog, dtb, norm_g, lng, lnb)
    return y, s_out, jnp.swapaxes(cq, 0, 1), cm, w_out_b


_PACK = SUBLANES


def _sample_phase_a(x_ref, mod_ref, wq_ref, wr_ref, wab_ref, cw_ref, mw_ref, alog_ref, dtb_ref, cq_in_ref, cm_in_ref,
                    cq_out_ref, cm_out_ref, og_scr, osc_scr, lhs_scr, u_scr, kd_scr, gl_scr, qk_scr,
                    *, t_len, d, off, n_conv, n_mconv, layer):
    bsz = x_ref.shape[0]
    hk = DN_HEADS * DN_DK
    sc_w = off["sc_c"][1] - off["sc_c"][0]

    x = _stack_time(x_ref, t_len, d)
    u = (x * (1.0 + _mod_rows(mod_ref, 1, t_len, d)) + _mod_rows(mod_ref, 0, t_len, d)).astype(BF16)
    pq = jnp.dot(u, wq_ref[...], preferred_element_type=F32)
    ab = jnp.dot(u, wab_ref[...], preferred_element_type=F32)
    rest = jnp.dot(u, wr_ref[...], preferred_element_type=F32)
    p = {name: rest[:, lo:hi] for name, (lo, hi) in off.items()}

    def rows(arr, t):
        return arr[t * bsz:(t + 1) * bsz]

    seq = [cq_in_ref[j] for j in range(n_conv - 1)] + [rows(pq, t) for t in range(t_len)]
    act = []
    for t in range(t_len):
        acc = None
        for j in range(n_conv):
            term = seq[t + j] * cw_ref[j:j + 1, :]
            acc = term if acc is None else acc + term
        act.append(_silu(acc))
    for j in range(n_conv - 1):
        cq_out_ref[j] = seq[t_len + j]

    z = p["sc_c"] * p["sc_h"]
    zseq = ([cm_in_ref[:, j, :] for j in range(n_mconv - 1)]
            + [rows(z, t) for t in range(t_len)])
    zc = []
    for t in range(t_len):
        acc = None
        for j in range(n_mconv):
            term = zseq[t + j] * _row(mw_ref, j)
            acc = term if acc is None else acc + term
        zc.append(acc)
    for j in range(n_mconv - 1):
        cm_out_ref[:, j, :] = zseq[t_len + j]
    osc_scr[...] = p["sc_b"] * jnp.concatenate(zc, axis=0)
    og_scr[...] = p["og"]

    beta_all = jax.nn.sigmoid(ab)
    g_all = -jnp.exp(_lane_vector(alog_ref, layer)) * _softplus(ab + _lane_vector(dtb_ref, layer))
    zero = jnp.zeros((bsz, LANES), F32)

    for h in range(DN_HEADS):
        q = [_l2norm(a[:, h * DN_DK:(h + 1) * DN_DK]) * (DN_DK ** -0.5) for a in act]
        k = [_l2norm(a[:, hk + h * DN_DK:hk + (h + 1) * DN_DK]) for a in act]
        v = [a[:, 2 * hk + h * DN_DV:2 * hk + (h + 1) * DN_DV] for a in act]
        beta = [rows(beta_all, t)[:, DN_HEADS + h:DN_HEADS + h + 1] for t in range(t_len)]
        g = [rows(g_all, t)[:, h:h + 1] for t in range(t_len)]
        gc = [g[0]]
        for t in range(1, t_len):
            gc.append(gc[-1] + g[t])
        kb = [k[t] * beta[t] for t in range(t_len)]
        vb = [v[t] * beta[t] for t in range(t_len)]
        eg = [jnp.exp(gc[t]) for t in range(t_len)]
        kbe = [kb[t] * eg[t] for t in range(t_len)]
        lm = [[None] * t_len for _ in range(t_len)]
        n_qk = 0
        for i in range(t_len):
            for j in range(i + 1):
                dec = jnp.exp(gc[i] - gc[j])
                qk_scr[h, n_qk] = jnp.broadcast_to(_rowsum(q[i] * k[j]) * dec, (bsz, LANES))
                n_qk += 1
                if j < i:
                    lm[i][j] = _rowsum(kb[i] * k[j]) * dec
        tm = [[None] * t_len for _ in range(t_len)]
        for i in range(t_len):
            for j in range(i):
                acc = lm[i][j]
                for m in range(j + 1, i):
                    acc = acc + lm[i][m] * tm[m][j]
                tm[i][j] = -acc
        for i in range(t_len):
            ui, wi = vb[i], kbe[i]
            for j in range(i):
                ui = ui + tm[i][j] * vb[j]
                wi = wi + tm[i][j] * kbe[j]
            lhs_scr[h, pl.ds(i, bsz, stride=_PACK), :] = wi
            lhs_scr[h, pl.ds(t_len + i, bsz, stride=_PACK), :] = q[i] * eg[i]
            u_scr[h, pl.ds(i, bsz, stride=_PACK), :] = ui
            u_scr[h, pl.ds(t_len + i, bsz, stride=_PACK), :] = zero
            kd_scr[h, pl.ds(i, bsz, stride=_PACK), :] = k[i] * jnp.exp(gc[t_len - 1] - gc[i])
            kd_scr[h, pl.ds(t_len + i, bsz, stride=_PACK), :] = zero
        gl_scr[h] = jnp.broadcast_to(jnp.exp(gc[t_len - 1]), (bsz, LANES))


def _sample_phase_c(x_ref, mod_ref, w_out_ref, ng_ref, lng_ref, lnb_ref, y_ref,
                    og_scr, osc_scr, res_scr, vn_scr, qk_scr, *, alpha, t_len, d):
    bsz = x_ref.shape[0]
    sub = 1
    x = _stack_time(x_ref, t_len, d)
    gate = _mod_rows(mod_ref, 2, t_len, d)
    og = og_scr[...]
    per_t = []
    for i in range(t_len):
        heads = []
        for h in range(DN_HEADS):
            oi = res_scr[h, pl.ds(t_len + i, bsz, stride=_PACK), :]
            base = i * (i + 1) // 2
            for j in range(i + 1):
                oi = oi + qk_scr[h, base + j] * vn_scr[h, pl.ds(j, bsz, stride=_PACK), :]
            heads.append(_gate_norm(oi, og[i * bsz:(i + 1) * bsz, h * DN_DV:(h + 1) * DN_DV], ng_ref[...]))
        per_t.append(jnp.concatenate(heads, axis=1))
    o_dn = jnp.concatenate(per_t, axis=0)
    dn_w = o_dn.shape[1]
    mix = (jnp.dot(o_dn.astype(BF16), w_out_ref[0:dn_w, :], preferred_element_type=F32)
           + jnp.dot(osc_scr[...].astype(BF16), w_out_ref[dn_w:, :], preferred_element_type=F32))
    y = _post(x, mix, gate, _row(lng_ref, sub), _row(lnb_ref, sub), alpha)
    for t in range(t_len):
        y_ref[:, t * d:(t + 1) * d] = y[t * bsz:(t + 1) * bsz]


def _mixer_sample_kernel(x_ref, mod_ref, wq_ref, wr_ref, wab_ref, w_out_ref, cw_ref, mw_ref, alog_ref, dtb_ref, ng_ref,
                         lng_ref, lnb_ref, cq_in_ref, cm_in_ref, s_in_hbm,
                         y_ref, cq_out_ref, cm_out_ref, s_out_hbm,
                         og_scr, osc_scr, lhs_scr, u_scr, kd_scr, gl_scr, qk_scr, res_scr, vn_scr,
                         s_buf, sem_in, sem_out,
                         *, alpha, t_len, d, off, n_conv, n_mconv, bt, n_blocks, layer):
    step = pl.program_id(0)
    assert 2 * t_len == _PACK, "packed buffers hold [T rows | T rows] per sequence"
    n_slots = s_buf.shape[0]

    def fetch(block, slot):
        return pltpu.make_async_copy(s_in_hbm.at[pl.ds(block * bt, bt)], s_buf.at[slot], sem_in.at[slot])

    def write_back(block, slot):
        return pltpu.make_async_copy(s_buf.at[slot], s_out_hbm.at[pl.ds(block * bt, bt)], sem_out.at[slot])

    @pl.when(step == 0)
    def _():
        for b in range(min(n_slots, n_blocks)):
            fetch(b, b).start()
        _sample_phase_a(x_ref, mod_ref, wq_ref, wr_ref, wab_ref, cw_ref, mw_ref, alog_ref, dtb_ref, cq_in_ref, cm_in_ref,
                        cq_out_ref, cm_out_ref, og_scr, osc_scr, lhs_scr, u_scr, kd_scr, gl_scr, qk_scr,
                        t_len=t_len, d=d, off=off, n_conv=n_conv, n_mconv=n_mconv, layer=layer)

    top =(lax.broadcasted_iota(jnp.int32, (_PACK, LANES), 0) < t_len).astype(F32)

    chains = [(bl, h) for bl in range(bt) for h in range(DN_HEADS)]
    seq0 = step * bt
    row0 = [pl.multiple_of((seq0 + bl) * _PACK, _PACK) for bl in range(bt)]
    slot = lax.rem(step, n_slots)
    fetch(step, slot).wait()
    states = [s_buf[slot, bl, h] for bl, h in chains]
    r = [_mm(lhs_scr[h, pl.ds(row0[bl], _PACK), :], s) for (bl, h), s in zip(chains, states)]
    vns = []
    for (bl, h), r_u in zip(chains, r):
        res_scr[h, pl.ds(row0[bl], _PACK), :] = r_u
        vn = (u_scr[h, pl.ds(row0[bl], _PACK), :] - r_u) * top
        vn_scr[h, pl.ds(row0[bl], _PACK), :] = vn
        vns.append(vn)
    upd = [_mm_at(kd_scr[h, pl.ds(row0[bl], _PACK), :], vn) for (bl, h), vn in zip(chains, vns)]
    for (bl, h), s_add in zip(chains, upd):
        s_buf[slot, bl, h] = s_buf[slot, bl, h] * gl_scr[h, pl.ds(seq0 + bl, 1), :] + s_add
    write_back(step, slot).start()

    @pl.when(step > 0)
    def _():
        prev = step - 1
        prev_slot = lax.rem(prev, n_slots)
        write_back(prev, prev_slot).wait()

        @pl.when(prev + n_slots < n_blocks)
        def _():
            fetch(prev + n_slots, prev_slot).start()

    @pl.when(step == n_blocks - 1)
    def _():
        _sample_phase_c(x_ref, mod_ref, w_out_ref, ng_ref, lng_ref, lnb_ref, y_ref,
                        og_scr, osc_scr, res_scr, vn_scr, qk_scr, alpha=alpha, t_len=t_len, d=d)
        write_back(step, slot).wait()


def _mixer_sample(x, mod, w_in, w_out, conv_w, mconv_w, alog, dtb, norm_g, lng, lnb, s_in, cq_in, cm_in,
                  *, alpha, off, bt, t, layer):
    bsz, d = x.shape[0], x.shape[1] // t
    assert bsz % bt == 0
    sc_w = off["sc_c"][1] - off["sc_c"][0]
    dn_w = DN_HEADS * DN_DV
    n_conv, n_mconv = conv_w.shape[0], mconv_w.shape[0]
    n_qk = t * (t + 1) // 2
    x2 = x
    cq2 = jnp.swapaxes(cq_in, 0, 1)
    kern = functools.partial(_mixer_sample_kernel, alpha=alpha, t_len=t, d=d, off=off,
                             n_conv=n_conv, n_mconv=n_mconv, bt=bt, n_blocks=bsz // bt, layer=layer)
    consts = (*w_in, w_out, conv_w, mconv_w, alog, dtb, norm_g, lng, lnb, cq2, cm_in)
    state_spec = pl.BlockSpec(memory_space=pl.ANY)
    packed = pltpu.VMEM((DN_HEADS, _PACK * bsz, LANES), F32)
    y, cq, cm, s_out = pl.pallas_call(
        kern,
        grid=(bsz // bt,),
        in_specs=([_const_spec(x2.shape), _sub_mod_spec(bsz, d, 1)]
                  + [pl.BlockSpec(memory_space=pltpu.SMEM) if a is alog or a is dtb else _const_spec(a.shape)
                     for a in consts] + [state_spec]),
        out_specs=[pl.BlockSpec(x2.shape, lambda i: (0, 0)),
                   pl.BlockSpec(cq2.shape, lambda i: (0, 0, 0)),
                   pl.BlockSpec(cm_in.shape, lambda i: (0, 0, 0)),
                   state_spec],
        out_shape=[jax.ShapeDtypeStruct(x2.shape, F32),
                   jax.ShapeDtypeStruct(cq2.shape, F32),
                   jax.ShapeDtypeStruct(cm_in.shape, F32),
                   jax.ShapeDtypeStruct(s_in.shape, F32)],
        scratch_shapes=[pltpu.VMEM((t * bsz, dn_w), F32),
                        pltpu.VMEM((t * bsz, sc_w), F32),
                        packed,
                        packed,
                        packed,
                        pltpu.VMEM((DN_HEADS, bsz, LANES), F32),
                        pltpu.VMEM((DN_HEADS, n_qk, bsz, LANES), F32),
                        packed,
                        packed,
                        pltpu.VMEM((STATE_SLOTS, bt, DN_HEADS, DN_DK, DN_DV), F32),
                        pltpu.SemaphoreType.DMA((STATE_SLOTS,)),
                        pltpu.SemaphoreType.DMA((STATE_SLOTS,))],
        compiler_params=pltpu.CompilerParams(dimension_semantics=("arbitrary",),
                                             vmem_limit_bytes=V7X_VMEM_LIMIT),
        name="mixer_sample",
    )(x2, mod, *consts, s_in)
    return y, s_out, jnp.swapaxes(cq, 0, 1), cm


def kernel(x_prompt, x_sample, state_ssm, state_conv_qkv, state_conv_mix, c_prompt, c_sample, w_ada, b_ada, ln_g, ln_b, ffn1_wg, ffn1_wu, ffn1_wd, ffn2_wg, ffn2_wu, ffn2_wd, w_in, conv_qkv_w, a_log, dt_bias, dn_norm_g, conv_mix_w, w_out):
    depth = w_ada.shape[0]
    alpha = (2 * depth) ** 0.25
    bp, tp, d = x_prompt.shape
    qkv_dim = conv_qkv_w.shape[-1]
    sc_width = conv_mix_w.shape[-1]
    dn_width = DN_HEADS * DN_DV
    off = _proj_layout(dn_width, sc_width)
    tm = min(512, tp)
    nseq = 2 if bp % 2 == 0 else 1
    tt = min(512 // nseq, tp)
    bt = min(16, x_sample.shape[0])
    ts = x_sample.shape[1]

    hp, hs = x_prompt, x_sample
    outs = [[] for _ in range(6)]
    for l in range(depth):
        mod_p, mod_s, *w_in_p = _adaln(c_prompt, c_sample, w_ada[l], b_ada[l].reshape(1, -1),
                                       jnp.swapaxes(w_in[l], 0, 1), qkv_dim)
        lng, lnb, mconv_w = ln_g[l][:, None, :], ln_b[l][:, None, :], conv_mix_w[l][:, None, :]
        w1 = (ffn1_wg[l], ffn1_wu[l], ffn1_wd[l])
        w2 = (ffn2_wg[l], ffn2_wu[l], ffn2_wd[l])
        norm_g = dn_norm_g[l].reshape(1, -1)
        mix_args = (conv_qkv_w[l], mconv_w, a_log, dt_bias, norm_g, lng, lnb)

        hp, hs = _ffn(hp, hs, mod_p, mod_s, *w1, lng, lnb, sub=0, alpha=alpha, tm=tm, ts=ts,
                      ys_3d=False)
        hp, a1, a2, a3, w_out_b = _mixer_prompt(hp, mod_p, w_in_p, w_out[l], *mix_args, alpha=alpha,
                                                nseq=nseq, tt=tt, off=off, layer=l)
        hs, b1, b2, b3 = _mixer_sample(hs, mod_s, w_in_p, w_out_b, *mix_args, state_ssm[l], state_conv_qkv[l],
                                       state_conv_mix[l], alpha=alpha, off=off, bt=bt, t=ts, layer=l)
        hp, hs = _ffn(hp, hs, mod_p, mod_s, *w2, lng, lnb, sub=2, alpha=alpha, tm=tm, ts=ts,
                      ys_3d=True)
        for lst, val in zip(outs, (a1, a2, a3, b1, b2, b3)):
            lst.append(val)
    return (hp, hs) + tuple(jnp.stack(o) for o in outs)
```

```python
import functools

import jax
import jax.numpy as jnp
from jax import lax
from jax.experimental import pallas as pl
from jax.experimental.pallas import tpu as pltpu

F32 = jnp.float32
BF16 = jnp.bfloat16

LN_EPS = 1e-5
RMS_EPS = 1e-6
N_SUB = 3
DN_HEADS = 4
DN_DK = 128
DN_DV = 128
CHUNK = 64
LANES = 128
SUBLANES = 8
AB_PAD = LANES

V7X_VMEM_LIMIT = 60000 * 1024
CAST_BUFFERS = 3
FFN_CHUNK = 256
STATE_SLOTS = 4


def _mm(a, b):
    return jnp.dot(a.astype(BF16), b.astype(BF16), preferred_element_type=F32)


def _mm_at(a, b):
    return lax.dot_general(a.astype(BF16), b.astype(BF16), (((0,), (0,)), ((), ())),
                           preferred_element_type=F32)


def _mm_f32(a, b):
    return jnp.dot(a, b, preferred_element_type=F32, precision=lax.Precision.HIGHEST)


def _silu(x):
    return x * jax.nn.sigmoid(x)


def _softplus(x):
    return jnp.maximum(x, 0.0) + jnp.log1p(jnp.exp(-jnp.abs(x)))


def _layer_norm(y, g, b):
    mu = jnp.mean(y, axis=-1, keepdims=True)
    yc = y - mu
    var = jnp.mean(yc * yc, axis=-1, keepdims=True)
    return yc * lax.rsqrt(var + LN_EPS) * g + b


def _post(x, delta, gate, g, b, alpha):
    return _layer_norm(alpha * x + gate * delta, g, b)


def _rowsum(x):
    return jnp.sum(x, axis=-1, keepdims=True)


def _l2norm(x):
    return x * lax.rsqrt(_rowsum(x * x) + RMS_EPS)


def _const_spec(shape):
    nd = len(shape)
    return pl.BlockSpec(shape, lambda *_: (0,) * nd, pipeline_mode=pl.Buffered(1))


def _lane_vector(smem_ref, layer):
    lane = lax.broadcasted_iota(jnp.int32, (1, AB_PAD), 1)
    out = jnp.zeros((1, AB_PAD), F32)
    for h in range(smem_ref.shape[1]):
        out = jnp.where(lane == h, smem_ref[layer, h], out)
    return out


def _row(ref, j):
    return ref[j] if len(ref.shape) == 3 else ref[j:j + 1, :]


def _adaln_kernel(cp_ref, cs_ref, w_ref, b_ref, win_ref, op_ref, os_ref, wq_ref, wr_ref, wab_ref, *, qkv_dim):
    mp = cp_ref.shape[0]
    c = jnp.concatenate([cp_ref[...], cs_ref[...]], axis=0)
    mod = _mm(_silu(c), w_ref[...]) + b_ref[...]
    op_ref[...] = mod[:mp]
    os_ref[...] = mod[mp:]

    gates = 2 * DN_HEADS
    wq_ref[...] = win_ref[0:qkv_dim, :].T.astype(BF16)
    wr_ref[...] = win_ref[qkv_dim + gates:, :].T.astype(BF16)
    ab_rows = jnp.concatenate([win_ref[qkv_dim:qkv_dim + gates, :],
                               jnp.zeros((AB_PAD - gates, win_ref.shape[1]), F32)], axis=0)
    wab_ref[...] = ab_rows.T.astype(BF16)


def _adaln(c_p, c_s, w, b, w_in_t, qkv_dim):
    (mp, d), ms = c_p.shape, c_s.shape[0]
    n = w.shape[1]
    proj = w_in_t.shape[0]
    rest = proj - qkv_dim - 2 * DN_HEADS
    assert qkv_dim % LANES == 0 and rest % LANES == 0 and (2 * DN_HEADS) % SUBLANES == 0
    steps = max(g for g in (1, 2, 4) if n % (g * LANES) == 0 and d % (g * LANES) == 0)
    tn, rows = n // steps, d // steps
    return pl.pallas_call(
        functools.partial(_adaln_kernel, qkv_dim=qkv_dim),
        grid=(steps,),
        in_specs=[_const_spec((mp, d)), _const_spec((ms, d)),
                  pl.BlockSpec((d, tn), lambda j: (0, j)),
                  pl.BlockSpec((1, tn), lambda j: (0, j)),
                  pl.BlockSpec((proj, rows), lambda j: (0, j))],
        out_specs=[pl.BlockSpec((mp, tn), lambda j: (0, j)), pl.BlockSpec((ms, tn), lambda j: (0, j)),
                   pl.BlockSpec((rows, qkv_dim), lambda j: (j, 0)),
                   pl.BlockSpec((rows, rest), lambda j: (j, 0)),
                   pl.BlockSpec((rows, AB_PAD), lambda j: (j, 0))],
        out_shape=[jax.ShapeDtypeStruct((mp, n), F32), jax.ShapeDtypeStruct((ms, n), F32),
                   jax.ShapeDtypeStruct((d, qkv_dim), BF16), jax.ShapeDtypeStruct((d, rest), BF16),
                   jax.ShapeDtypeStruct((d, AB_PAD), BF16)],
        compiler_params=pltpu.CompilerParams(dimension_semantics=("arbitrary",),
                                             vmem_limit_bytes=V7X_VMEM_LIMIT),
        name="adaln",
    )(c_p, c_s, w, b, w_in_t)


def _swiglu(x, shift, scale, wg_ref, wu_ref, wd_ref):
    u = (x * (1.0 + scale) + shift).astype(BF16)
    hg = jnp.dot(u, wg_ref[...], preferred_element_type=F32)
    hu = jnp.dot(u, wu_ref[...], preferred_element_type=F32)
    h = (_silu(hg) * hu).astype(BF16)
    return jnp.dot(h, wd_ref[...], preferred_element_type=F32)


def _stack_time(ref, t_len, d):
    if len(ref.shape) == 3:
        return jnp.concatenate([ref[:, t, :] for t in range(t_len)], axis=0)
    return jnp.concatenate([ref[:, t * d:(t + 1) * d] for t in range(t_len)], axis=0)


def _mod_rows(mod_ref, idx, t_len, d):
    m = mod_ref[:, idx * d:(idx + 1) * d]
    return jnp.concatenate([m] * t_len, axis=0)


def _sub_mod_spec(bsz, d, sub):
    return pl.BlockSpec((bsz, 3 * d), lambda *_: (0, sub), pipeline_mode=pl.Buffered(1))


def _first_tile_swiglu(x, shift, scale, w_hbm, w_scr, stages, sems):
    wg_hbm, wu_hbm, wd_hbm = w_hbm
    wg_scr, wu_scr, wd_scr = w_scr
    nbuf = stages[0].shape[0]
    cw = stages[0].shape[2]
    n = wg_hbm.shape[1] // cw

    def copies(j):
        slot = j % nbuf
        cols = pl.ds(j * cw, cw)
        return (pltpu.make_async_copy(wg_hbm.at[:, cols], stages[0].at[slot], sems[0].at[slot]),
                pltpu.make_async_copy(wu_hbm.at[:, cols], stages[1].at[slot], sems[1].at[slot]),
                pltpu.make_async_copy(wd_hbm.at[cols, :], stages[2].at[slot], sems[2].at[slot]))

    queue = [copies(j) for j in range(n)]
    ahead = nbuf - 1
    for group in queue[:ahead]:
        for copy in group:
            copy.start()
    u = (x * (1.0 + scale) + shift).astype(BF16)
    acc = None
    for j, group in enumerate(queue):
        if j + ahead < n:
            for copy in queue[j + ahead]:
                copy.start()
        for copy in group:
            copy.wait()
        slot, cols = j % nbuf, slice(j * cw, (j + 1) * cw)
        wg_scr[:, cols] = stages[0][slot].astype(BF16)
        wu_scr[:, cols] = stages[1][slot].astype(BF16)
        wd_scr[cols, :] = stages[2][slot].astype(BF16)
        hg = jnp.dot(u, wg_scr[:, cols], preferred_element_type=F32)
        hu = jnp.dot(u, wu_scr[:, cols], preferred_element_type=F32)
        part = jnp.dot((_silu(hg) * hu).astype(BF16), wd_scr[cols, :], preferred_element_type=F32)
        acc = part if acc is None else acc + part
    return acc


def _ffn_kernel(xp_ref, xq_ref, modp_ref, xs_ref, mods_ref, wg_hbm, wu_hbm, wd_hbm, lng_ref, lnb_ref,
                yp_ref, ys_ref, wg_scr, wu_scr, wd_scr, d_scr, stage_g, stage_u, stage_d, sem_g, sem_u, sem_d,
                *, sub, alpha, n_prompt, per_seq, t_len, d):
    k = pl.program_id(0)
    g, b = _row(lng_ref, sub), _row(lnb_ref, sub)
    seq_now = jnp.minimum(k, n_prompt - 1) // per_seq
    seq_prev = jnp.maximum(k - 1, 0) // per_seq

    def mod_row(seq, idx):
        return modp_ref[pl.ds(seq, 1), idx * d:(idx + 1) * d]

    def matmuls():
        return _swiglu(xp_ref[...], mod_row(seq_now, 0), mod_row(seq_now, 1), wg_scr, wu_scr, wd_scr)

    def finish(d_prev):
        yp_ref[...] = _post(xq_ref[...], 0.5 * d_prev, mod_row(seq_prev, 2), g, b, alpha)

    @pl.when(k == 0)
    def _():
        d_scr[...] = _first_tile_swiglu(
            xp_ref[...], mod_row(0, 0), mod_row(0, 1),
            (wg_hbm, wu_hbm, wd_hbm), (wg_scr, wu_scr, wd_scr), (stage_g, stage_u, stage_d),
            (sem_g, sem_u, sem_d))

    @pl.when(jnp.logical_and(k > 0, k < n_prompt))
    def _():
        d_prev = d_scr[...]
        d_new = matmuls()
        finish(d_prev)
        d_scr[...] = d_new

    @pl.when(k == n_prompt)
    def _():
        finish(d_scr[...])
        bsz = xs_ref.shape[0]
        x = _stack_time(xs_ref, t_len, d)
        delta = _swiglu(x, _mod_rows(mods_ref, 0, t_len, d), _mod_rows(mods_ref, 1, t_len, d),
                        wg_scr, wu_scr, wd_scr)
        y = _post(x, 0.5 * delta, _mod_rows(mods_ref, 2, t_len, d), g, b, alpha)
        for t in range(t_len):
            if len(ys_ref.shape) == 3:
                ys_ref[:, t, :] = y[t * bsz:(t + 1) * bsz]
            else:
                ys_ref[:, t * d:(t + 1) * d] = y[t * bsz:(t + 1) * bsz]


def _ffn(xp, xs, mod_p, mod_s, wg, wu, wd, lng, lnb, *, sub, alpha, tm, ts, ys_3d):
    bp, tp, d = xp.shape
    bs = xs.shape[0]
    ys_shape = (bs, ts, d) if ys_3d else (bs, ts * d)
    f = wg.shape[1]
    assert tp % tm == 0
    per_seq = tp // tm
    n_prompt = bp * per_seq
    assert f % FFN_CHUNK == 0
    last = n_prompt - 1
    yp, ys = pl.pallas_call(
        functools.partial(_ffn_kernel, sub=sub, alpha=alpha, n_prompt=n_prompt, per_seq=per_seq, t_len=ts, d=d),
        grid=(n_prompt + 1,),
        in_specs=[pl.BlockSpec((tm, d), lambda k: (jnp.minimum(k, last), 0)),
                  pl.BlockSpec((tm, d), lambda k: (jnp.maximum(k - 1, 0), 0)),
                  _sub_mod_spec(bp, d, sub),
                  _const_spec(xs.shape), _sub_mod_spec(bs, d, sub),
                  pl.BlockSpec(memory_space=pl.ANY), pl.BlockSpec(memory_space=pl.ANY),
                  pl.BlockSpec(memory_space=pl.ANY),
                  _const_spec(lng.shape), _const_spec(lnb.shape)],
        out_specs=[pl.BlockSpec((tm, d), lambda k: (jnp.maximum(k - 1, 0), 0)),
                   pl.BlockSpec(ys_shape, lambda k: (0,) * len(ys_shape))],
        out_shape=[jax.ShapeDtypeStruct((bp * tp, d), F32),
                   jax.ShapeDtypeStruct(ys_shape, F32)],
        scratch_shapes=[pltpu.VMEM((d, f), BF16), pltpu.VMEM((d, f), BF16), pltpu.VMEM((f, d), BF16),
                        pltpu.VMEM((tm, d), F32),
                        pltpu.VMEM((CAST_BUFFERS, d, FFN_CHUNK), F32), pltpu.VMEM((CAST_BUFFERS, d, FFN_CHUNK), F32),
                        pltpu.VMEM((CAST_BUFFERS, FFN_CHUNK, d), F32),
                        pltpu.SemaphoreType.DMA((CAST_BUFFERS,)), pltpu.SemaphoreType.DMA((CAST_BUFFERS,)),
                        pltpu.SemaphoreType.DMA((CAST_BUFFERS,))],
        compiler_params=pltpu.CompilerParams(dimension_semantics=("arbitrary",),
                                             vmem_limit_bytes=V7X_VMEM_LIMIT),
        name=f"ffn{sub}",
    )(xp.reshape(bp * tp, d), xp.reshape(bp * tp, d), mod_p, xs, mod_s,
      wg, wu, wd, lng, lnb)
    return yp.reshape(bp, tp, d), ys


def _tri_inverse_steps(lmats, c):
    ls = [l.astype(BF16) for l in lmats]
    ms = [jnp.dot(l, l, preferred_element_type=F32) for l in ls]
    yield
    rows = lax.broadcasted_iota(jnp.int32, (c, c), 0)
    cols = lax.broadcasted_iota(jnp.int32, (c, c), 1)
    eye = (rows == cols).astype(F32)
    ps = [eye - l for l in lmats]
    power = 2
    while 2 * power < c:
        mp = [jnp.concatenate([m, p], axis=0).astype(BF16) for m, p in zip(ms, ps)]
        sq = [jnp.dot(x, x[:c], preferred_element_type=F32) for x in mp]
        yield
        ms = [x[:c] for x in sq]
        ps = [p + x[c:] for p, x in zip(ps, sq)]
        power *= 2
    pm = [jnp.dot(p.astype(BF16), m.astype(BF16), preferred_element_type=F32) for p, m in zip(ps, ms)]
    yield
    return [p + x for p, x in zip(ps, pm)]


def _gdn_local_steps(units, tril, strict):
    c = units[0][1].shape[0]
    a = [lax.dot_general(jnp.concatenate([kb, q], axis=0), k, (((1,), (1,)), ((), ())),
                         preferred_element_type=F32)
         for kb, q, k, _, _, _, _ in units]
    yield
    lmats, qks = [], []
    for a_u, (_, _, _, _, _, gcol, grow) in zip(a, units):
        decay = jnp.where(tril, jnp.exp(jnp.where(tril, gcol - grow, 0.0)), 0.0)
        lmats.append(jnp.where(strict, a_u[:c] * decay, 0.0))
        qks.append(jnp.where(tril, a_u[c:] * decay, 0.0))
    tms = yield from _tri_inverse_steps(lmats, c)
    uw = [jnp.dot(tm.astype(BF16), jnp.concatenate([vb, kbe], axis=1), preferred_element_type=F32)
          for tm, (_, _, _, vb, kbe, _, _) in zip(tms, units)]
    yield
    return [(x[:, :DN_DV], x[:, DN_DV:], qk) for x, qk in zip(uw, qks)]


def _alternate(primary, secondary):
    result = None
    done = False
    while not done:
        try:
            next(primary)
            yield
        except StopIteration as stop:
            result, done = stop.value, True
        if secondary is not None:
            try:
                next(secondary)
                yield
            except StopIteration:
                secondary = None
    if secondary is not None:
        yield from secondary
    return result


def _run_interleaved(*gens):
    live = list(gens)
    while live:
        for g in list(live):
            try:
                next(g)
            except StopIteration:
                live.remove(g)


def _gate_norm(o, og, norm_g):
    o = o * lax.rsqrt(jnp.mean(o * o, axis=-1, keepdims=True) + RMS_EPS) * norm_g
    return o * _silu(og)


def _proj_layout(dn_width, sc_width):
    off = {}
    pos = 0
    for name, width in (("og", dn_width), ("sc_b", sc_width), ("sc_c", sc_width), ("sc_h", sc_width)):
        off[name] = (pos, pos + width)
        pos += width
    return off


def _mixer_prompt_kernel(xf_ref, xb_ref, mod_ref, wq_ref, wr_ref, wab_ref, w_out_hbm, cw_ref,
                         mw_ref, alog_ref, dtb_ref, ng_ref, lng_ref, lnb_ref,
                         y_ref, s_out_ref, cq_out_ref, cm_out_ref, w_out_ref,
                         s_scr, cq_scr, cm_scr, gct_scr,
                         u_scr, w_scr, qk_scr, qd_scr, kd_scr, gl_scr, og_scr, osc_scr, w_out_stage, w_out_sem,
                         *, alpha, nseq, tt, tiles_per_group, n_tiles, off, n_conv, n_mconv, layer):
    k = pl.program_id(0)
    d = xf_ref.shape[-1]
    seq_front = (jnp.minimum(k, n_tiles - 1) // tiles_per_group) * nseq
    seq_back = (jnp.maximum(k - 1, 0) // tiles_per_group) * nseq

    def mod_row(seq, idx):
        return mod_ref[pl.ds(seq, 1), idx * d:(idx + 1) * d]

    rows_all = nseq * tt
    nchunk = rows_all // CHUNK
    per_seq = tt // CHUNK
    hk = DN_HEADS * DN_DK
    sub = 1
    head_lanes = [slice(h * DN_DK, (h + 1) * DN_DK) for h in range(DN_HEADS)]
    t_front = lax.rem(jnp.minimum(k, n_tiles - 1), tiles_per_group)
    t_back = lax.rem(jnp.maximum(k - 1, 0), tiles_per_group)
    keep_front = jnp.where(t_front == 0, 0.0, 1.0).astype(F32)
    keep_back = jnp.where(t_back == 0, 0.0, 1.0).astype(F32)

    def causal_conv(new_rows, hist_scr, s, w_ref, taps):
        ext = jnp.concatenate([hist_scr[s] * keep_front, new_rows], axis=0)
        acc = new_rows * _row(w_ref, taps - 1)
        for j in range(taps - 1):
            acc = acc + pltpu.roll(ext, taps - 1 - j, axis=0)[SUBLANES:] * _row(w_ref, j)
        hist_scr[s] = new_rows[tt - SUBLANES:]
        return acc

    def front():
        u = jnp.concatenate(
            [(xf_ref[s] * (1.0 + mod_row(seq_front + s, 1))
              + mod_row(seq_front + s, 0)).astype(BF16) for s in range(nseq)], axis=0)
        ab = jnp.dot(u, wab_ref[...], preferred_element_type=F32)
        pq = jnp.dot(u, wq_ref[...], preferred_element_type=F32)
        yield

        beta_all = jax.nn.sigmoid(ab)
        g = -jnp.exp(_lane_vector(alog_ref, layer)) * _softplus(ab + _lane_vector(dtb_ref, layer))
        rows = lax.broadcasted_iota(jnp.int32, (CHUNK, CHUNK), 0)
        cols = lax.broadcasted_iota(jnp.int32, (CHUNK, CHUNK), 1)
        tril = rows >= cols
        strict = rows > cols
        tril_f = tril.astype(F32)
        gc_parts, gl_parts = [], []
        for c in range(nchunk):
            gc_c = _mm_f32(tril_f, g[c * CHUNK:(c + 1) * CHUNK, :])
            gc_parts.append(gc_c)
            gl_parts.append(jnp.broadcast_to(gc_c[CHUNK - 1:CHUNK, :], (CHUNK, AB_PAD)))
        yield
        p = {}

        def project_rest(names):
            for name in names:
                lo, hi = off[name]
                p[name] = jnp.dot(u, wr_ref[:, lo:hi], preferred_element_type=F32)
                yield

        yield from project_rest(("sc_c", "sc_h"))
        gc = jnp.concatenate(gc_parts, axis=0)
        gl_rows = jnp.concatenate(gl_parts, axis=0)
        gc_t = gc.T
        for c in range(nchunk):
            gct_scr[c] = gc_t[0:SUBLANES, c * CHUNK:(c + 1) * CHUNK]

        acts = []
        for s in range(nseq):
            pq_s = pq[s * tt:(s + 1) * tt]
            acts.append(_silu(causal_conv(pq_s, cq_scr, s, cw_ref, n_conv)))
            for j in range(n_conv - 1):
                row = tt - (n_conv - 1) + j
                cq_out_ref[j, pl.ds(seq_front + s, 1), :] = pq_s[row:row + 1]
        act = jnp.concatenate(acts, axis=0)

        eg_all = jnp.exp(gc)
        ekd_all = jnp.exp(gl_rows - gc)
        egl_all = jnp.exp(gl_rows)
        per_head = []
        for h, lanes in enumerate(head_lanes):
            q = _l2norm(act[:, h * DN_DK:(h + 1) * DN_DK]) * (DN_DK ** -0.5)
            kk = _l2norm(act[:, hk + h * DN_DK:hk + (h + 1) * DN_DK])
            v = act[:, 2 * hk + h * DN_DV:2 * hk + (h + 1) * DN_DV]
            beta = beta_all[:, DN_HEADS + h:DN_HEADS + h + 1]
            eg = eg_all[:, h:h + 1]
            kb = kk * beta
            per_head.append((kb.astype(BF16), q.astype(BF16), kk.astype(BF16), (v * beta).astype(BF16),
                             (kb * eg).astype(BF16)))
            qd_scr[:, lanes] = (q * eg).astype(BF16)
            kd_scr[:, lanes] = (kk * ekd_all[:, h:h + 1]).astype(BF16)
            for c in range(nchunk):
                gl_scr[c * SUBLANES:(c + 1) * SUBLANES, lanes] = jnp.broadcast_to(
                    egl_all[c * CHUNK:c * CHUNK + SUBLANES, h:h + 1], (SUBLANES, DN_DV))

        z = p["sc_c"] * p["sc_h"]
        zcs = []
        for s in range(nseq):
            z_s = z[s * tt:(s + 1) * tt]
            zcs.append(causal_conv(z_s, cm_scr, s, mw_ref, n_mconv))
            cm_out_ref[seq_front + s] = z_s[tt - (n_mconv - 1):]

        where = [(c, h, lanes) for c in range(nchunk) for h, lanes in enumerate(head_lanes)]
        units = [tuple(arr[c * CHUNK:(c + 1) * CHUNK] for arr in per_head[h])
                 + (gc[c * CHUNK:(c + 1) * CHUNK, h:h + 1], gct_scr[c, h:h + 1, :])
                 for c, h, lanes in where]
        local = yield from _alternate(_gdn_local_steps(units, tril, strict), project_rest(("sc_b", "og")))
        osc_scr[...] = p["sc_b"] * jnp.concatenate(zcs, axis=0)
        og_scr[...] = p["og"]
        for (c, h, lanes), (u_c, w_c, qk_c) in zip(where, local):
            u_scr[c * CHUNK:(c + 1) * CHUNK, lanes] = u_c
            w_scr[c * CHUNK:(c + 1) * CHUNK, lanes] = w_c.astype(BF16)
            qk_scr[c * DN_HEADS + h] = qk_c.astype(BF16)

    def back():
        u_all, w_all, qd_all, kd_all = u_scr[...], w_scr[...], qd_scr[...], kd_scr[...]
        gl_all, og, o_sc = gl_scr[...], og_scr[...], osc_scr[...]
        qks = [qk_scr[i] for i in range(nchunk * DN_HEADS)]
        xs = [xb_ref[s] for s in range(nseq)]
        gates = [mod_row(seq_back + s, 2) for s in range(nseq)]
        states = [s_scr[i] * keep_back for i in range(nseq * DN_HEADS)]
        o_rows = [[None] * DN_HEADS for _ in range(nchunk)]
        for j in range(per_seq):
            chains = [(s, s * per_seq + j, h, lanes) for s in range(nseq) for h, lanes in enumerate(head_lanes)]
            rws = [slice(c * CHUNK, (c + 1) * CHUNK) for _, c, _, _ in chains]
            r = [jnp.dot(jnp.concatenate([w_all[rw, lanes], qd_all[rw, lanes]], axis=0),
                         states[s * DN_HEADS + h].astype(BF16), preferred_element_type=F32)
                 for (s, c, h, lanes), rw in zip(chains, rws)]
            yield
            v_new = [(u_all[rw, lanes] - r_u[:CHUNK]).astype(BF16)
                     for (s, c, h, lanes), rw, r_u in zip(chains, rws, r)]
            for (s, c, h, lanes), r_u, v_u in zip(chains, r, v_new):
                o_rows[c][h] = r_u[CHUNK:] + jnp.dot(qks[c * DN_HEADS + h], v_u, preferred_element_type=F32)
            for (s, c, h, lanes), rw, v_u in zip(chains, rws, v_new):
                i = s * DN_HEADS + h
                states[i] = (states[i] * gl_all[c * SUBLANES:c * SUBLANES + 1, lanes]
                             + lax.dot_general(kd_all[rw, lanes], v_u, (((0,), (0,)), ((), ())),
                                               preferred_element_type=F32))
            yield
        for i, st in enumerate(states):
            s_scr[i] = st
        for s in range(nseq):
            for h in range(DN_HEADS):
                s_out_ref[seq_back + s, h] = states[s * DN_HEADS + h]

        o_dn = jnp.concatenate(
            [_gate_norm(jnp.concatenate([o_rows[c][h] for c in range(nchunk)], axis=0),
                        og[:, lanes], ng_ref[...]) for h, lanes in enumerate(head_lanes)], axis=1)
        dn_w = o_dn.shape[1]
        mix = (jnp.dot(o_dn.astype(BF16), w_out_ref[0:dn_w, :], preferred_element_type=F32)
               + jnp.dot(o_sc.astype(BF16), w_out_ref[dn_w:, :], preferred_element_type=F32))
        yield
        for s in range(nseq):
            y_ref[s] = _post(xs[s], mix[s * tt:(s + 1) * tt], gates[s],
                             _row(lng_ref, sub), _row(lnb_ref, sub), alpha)

    @pl.when(k == 0)
    def _():
        s_scr[...] = jnp.zeros_like(s_scr)
        cq_scr[...] = jnp.zeros_like(cq_scr)
        cm_scr[...] = jnp.zeros_like(cm_scr)
        w_out_copy = pltpu.make_async_copy(w_out_hbm, w_out_stage, w_out_sem.at[0])
        w_out_copy.start()
        _run_interleaved(front())
        w_out_copy.wait()
        w_out_ref[...] = w_out_stage[...].astype(BF16)

    @pl.when(jnp.logical_and(k > 0, k < n_tiles))
    def _():
        _run_interleaved(back(), front())

    @pl.when(k == n_tiles)
    def _():
        _run_interleaved(back())


def _mixer_prompt(x, mod, w_in, w_out, conv_w, mconv_w, alog, dtb, norm_g, lng, lnb, *, alpha, nseq, tt, off, layer):
    bsz, t, d = x.shape
    assert t % tt == 0 and tt % LANES == 0 and bsz % nseq == 0
    nt = t // tt
    n_tiles = (bsz // nseq) * nt
    rows = nseq * tt
    qkv_dim = conv_w.shape[-1]
    sc_w = off["sc_c"][1] - off["sc_c"][0]
    dn_w = DN_HEADS * DN_DV
    n_conv, n_mconv = conv_w.shape[0], mconv_w.shape[0]
    assert max(n_conv, n_mconv) - 1 <= SUBLANES <= tt
    nchunk = rows // CHUNK
    kern = functools.partial(_mixer_prompt_kernel, alpha=alpha, nseq=nseq, tt=tt, tiles_per_group=nt,
                             n_tiles=n_tiles, off=off, n_conv=n_conv, n_mconv=n_mconv, layer=layer)

    def front_tile(k):
        return jnp.minimum(k, n_tiles - 1)

    def back_tile(k):
        return jnp.maximum(k - 1, 0)

    y, s_out, cq, cm, w_out_b = pl.pallas_call(
        kern,
        grid=(n_tiles + 1,),
        in_specs=[pl.BlockSpec((nseq, tt, d), lambda k: (front_tile(k) // nt, front_tile(k) % nt, 0)),
                  pl.BlockSpec((nseq, tt, d), lambda k: (back_tile(k) // nt, back_tile(k) % nt, 0)),
                  _sub_mod_spec(bsz, d, 1),
                  *[_const_spec(w.shape) for w in w_in], pl.BlockSpec(memory_space=pl.ANY),
                  _const_spec(conv_w.shape), _const_spec(mconv_w.shape),
                  pl.BlockSpec(memory_space=pltpu.SMEM), pl.BlockSpec(memory_space=pltpu.SMEM),
                  _const_spec(norm_g.shape), _const_spec(lng.shape), _const_spec(lnb.shape)],
        out_specs=[pl.BlockSpec((nseq, tt, d), lambda k: (back_tile(k) // nt, back_tile(k) % nt, 0)),
                   pl.BlockSpec((bsz, DN_HEADS, DN_DK, DN_DV), lambda k: (0, 0, 0, 0)),
                   pl.BlockSpec((n_conv - 1, bsz, qkv_dim), lambda k: (0, 0, 0)),
                   pl.BlockSpec((bsz, n_mconv - 1, sc_w), lambda k: (0, 0, 0)),
                   pl.BlockSpec(w_out.shape, lambda k: (0, 0))],
        out_shape=[jax.ShapeDtypeStruct((bsz, t, d), F32),
                   jax.ShapeDtypeStruct((bsz, DN_HEADS, DN_DK, DN_DV), F32),
                   jax.ShapeDtypeStruct((n_conv - 1, bsz, qkv_dim), F32),
                   jax.ShapeDtypeStruct((bsz, n_mconv - 1, sc_w), F32),
                   jax.ShapeDtypeStruct(w_out.shape, BF16)],
        scratch_shapes=[pltpu.VMEM((nseq * DN_HEADS, DN_DK, DN_DV), F32),
                        pltpu.VMEM((nseq, SUBLANES, qkv_dim), F32),
                        pltpu.VMEM((nseq, SUBLANES, sc_w), F32),
                        pltpu.VMEM((nchunk, SUBLANES, CHUNK), F32),
                        pltpu.VMEM((rows, dn_w), F32),
                        pltpu.VMEM((rows, dn_w), BF16),
                        pltpu.VMEM((nchunk * DN_HEADS, CHUNK, CHUNK), BF16),
                        pltpu.VMEM((rows, dn_w), BF16),
                        pltpu.VMEM((rows, dn_w), BF16),
                        pltpu.VMEM((nchunk * SUBLANES, dn_w), F32),
                        pltpu.VMEM((rows, dn_w), F32),
                        pltpu.VMEM((rows, sc_w), F32),
                        pltpu.VMEM(w_out.shape, F32),
                        pltpu.SemaphoreType.DMA((1,))],
        compiler_params=pltpu.CompilerParams(dimension_semantics=("arbitrary",),
                                             vmem_limit_bytes=V7X_VMEM_LIMIT),
        name="mixer_prompt",
    )(x, x, mod, *w_in, w_out, conv_w, mconv_w, alog, dtb, norm_g, lng, lnb)
    return y, s_out, jnp.swapaxes(cq, 0, 1), cm, w_out_b


_PACK = SUBLANES


def _sample_phase_a(x_ref, mod_ref, wq_ref, wr_ref, wab_ref, cw_ref, mw_ref, alog_ref, dtb_ref, cq_in_ref, cm_in_ref,
                    cq_out_ref, cm_out_ref, og_scr, osc_scr, lhs_scr, u_scr, kd_scr, gl_scr, qk_scr,
                    *, t_len, d, off, n_conv, n_mconv, layer):
    bsz = x_ref.shape[0]
    hk = DN_HEADS * DN_DK
    sc_w = off["sc_c"][1] - off["sc_c"][0]

    x = _stack_time(x_ref, t_len, d)
    u = (x * (1.0 + _mod_rows(mod_ref, 1, t_len, d)) + _mod_rows(mod_ref, 0, t_len, d)).astype(BF16)
    pq = jnp.dot(u, wq_ref[...], preferred_element_type=F32)
    ab = jnp.dot(u, wab_ref[...], preferred_element_type=F32)
    rest = jnp.dot(u, wr_ref[...], preferred_element_type=F32)
    p = {name: rest[:, lo:hi] for name, (lo, hi) in off.items()}

    def rows(arr, t):
        return arr[t * bsz:(t + 1) * bsz]

    seq = [cq_in_ref[j] for j in range(n_conv - 1)] + [rows(pq, t) for t in range(t_len)]
    act = []
    for t in range(t_len):
        acc = None
        for j in range(n_conv):
            term = seq[t + j] * cw_ref[j:j + 1, :]
            acc = term if acc is None else acc + term
        act.append(_silu(acc))
    for j in range(n_conv - 1):
        cq_out_ref[j] = seq[t_len + j]

    z = p["sc_c"] * p["sc_h"]
    zseq = ([cm_in_ref[:, j, :] for j in range(n_mconv - 1)]
            + [rows(z, t) for t in range(t_len)])
    zc = []
    for t in range(t_len):
        acc = None
        for j in range(n_mconv):
            term = zseq[t + j] * _row(mw_ref, j)
            acc = term if acc is None else acc + term
        zc.append(acc)
    for j in range(n_mconv - 1):
        cm_out_ref[:, j, :] = zseq[t_len + j]
    osc_scr[...] = p["sc_b"] * jnp.concatenate(zc, axis=0)
    og_scr[...] = p["og"]

    beta_all = jax.nn.sigmoid(ab)
    g_all = -jnp.exp(_lane_vector(alog_ref, layer)) * _softplus(ab + _lane_vector(dtb_ref, layer))
    zero = jnp.zeros((bsz, LANES), F32)

    for h in range(DN_HEADS):
        q = [_l2norm(a[:, h * DN_DK:(h + 1) * DN_DK]) * (DN_DK ** -0.5) for a in act]
        k = [_l2norm(a[:, hk + h * DN_DK:hk + (h + 1) * DN_DK]) for a in act]
        v = [a[:, 2 * hk + h * DN_DV:2 * hk + (h + 1) * DN_DV] for a in act]
        beta = [rows(beta_all, t)[:, DN_HEADS + h:DN_HEADS + h + 1] for t in range(t_len)]
        g = [rows(g_all, t)[:, h:h + 1] for t in range(t_len)]
        gc = [g[0]]
        for t in range(1, t_len):
            gc.append(gc[-1] + g[t])
        kb = [k[t] * beta[t] for t in range(t_len)]
        vb = [v[t] * beta[t] for t in range(t_len)]
        eg = [jnp.exp(gc[t]) for t in range(t_len)]
        kbe = [kb[t] * eg[t] for t in range(t_len)]
        lm = [[None] * t_len for _ in range(t_len)]
        n_qk = 0
        for i in range(t_len):
            for j in range(i + 1):
                dec = jnp.exp(gc[i] - gc[j])
                qk_scr[h, n_qk] = jnp.broadcast_to(_rowsum(q[i] * k[j]) * dec, (bsz, LANES))
                n_qk += 1
                if j < i:
                    lm[i][j] = _rowsum(kb[i] * k[j]) * dec
        tm = [[None] * t_len for _ in range(t_len)]
        for i in range(t_len):
            for j in range(i):
                acc = lm[i][j]
                for m in range(j + 1, i):
                    acc = acc + lm[i][m] * tm[m][j]
                tm[i][j] = -acc
        for i in range(t_len):
            ui, wi = vb[i], kbe[i]
            for j in range(i):
                ui = ui + tm[i][j] * vb[j]
                wi = wi + tm[i][j] * kbe[j]
            lhs_scr[h, pl.ds(i, bsz, stride=_PACK), :] = wi
            lhs_scr[h, pl.ds(t_len + i, bsz, stride=_PACK), :] = q[i] * eg[i]
            u_scr[h, pl.ds(i, bsz, stride=_PACK), :] = ui
            u_scr[h, pl.ds(t_len + i, bsz, stride=_PACK), :] = zero
            kd_scr[h, pl.ds(i, bsz, stride=_PACK), :] = k[i] * jnp.exp(gc[t_len - 1] - gc[i])
            kd_scr[h, pl.ds(t_len + i, bsz, stride=_PACK), :] = zero
        gl_scr[h] = jnp.broadcast_to(jnp.exp(gc[t_len - 1]), (bsz, LANES))


def _sample_phase_c(x_ref, mod_ref, w_out_ref, ng_ref, lng_ref, lnb_ref, y_ref,
                    og_scr, osc_scr, res_scr, vn_scr, qk_scr, *, alpha, t_len, d):
    bsz = x_ref.shape[0]
    sub = 1
    x = _stack_time(x_ref, t_len, d)
    gate = _mod_rows(mod_ref, 2, t_len, d)
    og = og_scr[...]
    per_t = []
    for i in range(t_len):
        heads = []
        for h in range(DN_HEADS):
            oi = res_scr[h, pl.ds(t_len + i, bsz, stride=_PACK), :]
            base = i * (i + 1) // 2
            for j in range(i + 1):
                oi = oi + qk_scr[h, base + j] * vn_scr[h, pl.ds(j, bsz, stride=_PACK), :]
            heads.append(_gate_norm(oi, og[i * bsz:(i + 1) * bsz, h * DN_DV:(h + 1) * DN_DV], ng_ref[...]))
        per_t.append(jnp.concatenate(heads, axis=1))
    o_dn = jnp.concatenate(per_t, axis=0)
    dn_w = o_dn.shape[1]
    mix = (jnp.dot(o_dn.astype(BF16), w_out_ref[0:dn_w, :], preferred_element_type=F32)
           + jnp.dot(osc_scr[...].astype(BF16), w_out_ref[dn_w:, :], preferred_element_type=F32))
    y = _post(x, mix, gate, _row(lng_ref, sub), _row(lnb_ref, sub), alpha)
    for t in range(t_len):
        y_ref[:, t * d:(t + 1) * d] = y[t * bsz:(t + 1) * bsz]


def _mixer_sample_kernel(x_ref, mod_ref, wq_ref, wr_ref, wab_ref, w_out_ref, cw_ref, mw_ref, alog_ref, dtb_ref, ng_ref,
                         lng_ref, lnb_ref, cq_in_ref, cm_in_ref, s_in_hbm,
                         y_ref, cq_out_ref, cm_out_ref, s_out_hbm,
                         og_scr, osc_scr, lhs_scr, u_scr, kd_scr, gl_scr, qk_scr, res_scr, vn_scr,
                         s_buf, sem_in, sem_out,
                         *, alpha, t_len, d, off, n_conv, n_mconv, bt, n_blocks, layer):
    step = pl.program_id(0)
    assert 2 * t_len == _PACK, "packed buffers hold [T rows | T rows] per sequence"
    n_slots = s_buf.shape[0]

    def fetch(block, slot):
        return pltpu.make_async_copy(s_in_hbm.at[pl.ds(block * bt, bt)], s_buf.at[slot], sem_in.at[slot])

    def write_back(block, slot):
        return pltpu.make_async_copy(s_buf.at[slot], s_out_hbm.at[pl.ds(block * bt, bt)], sem_out.at[slot])

    @pl.when(step == 0)
    def _():
        for b in range(min(n_slots, n_blocks)):
            fetch(b, b).start()
        _sample_phase_a(x_ref, mod_ref, wq_ref, wr_ref, wab_ref, cw_ref, mw_ref, alog_ref, dtb_ref, cq_in_ref, cm_in_ref,
                        cq_out_ref, cm_out_ref, og_scr, osc_scr, lhs_scr, u_scr, kd_scr, gl_scr, qk_scr,
                        t_len=t_len, d=d, off=off, n_conv=n_conv, n_mconv=n_mconv, layer=layer)

    top =(lax.broadcasted_iota(jnp.int32, (_PACK, LANES), 0) < t_len).astype(F32)

    chains = [(bl, h) for bl in range(bt) for h in range(DN_HEADS)]
    seq0 = step * bt
    row0 = [pl.multiple_of((seq0 + bl) * _PACK, _PACK) for bl in range(bt)]
    slot = lax.rem(step, n_slots)
    fetch(step, slot).wait()
    states = [s_buf[slot, bl, h] for bl, h in chains]
    r = [_mm(lhs_scr[h, pl.ds(row0[bl], _PACK), :], s) for (bl, h), s in zip(chains, states)]
    vns = []
    for (bl, h), r_u in zip(chains, r):
        res_scr[h, pl.ds(row0[bl], _PACK), :] = r_u
        vn = (u_scr[h, pl.ds(row0[bl], _PACK), :] - r_u) * top
        vn_scr[h, pl.ds(row0[bl], _PACK), :] = vn
        vns.append(vn)
    upd = [_mm_at(kd_scr[h, pl.ds(row0[bl], _PACK), :], vn) for (bl, h), vn in zip(chains, vns)]
    for (bl, h), s_add in zip(chains, upd):
        s_buf[slot, bl, h] = s_buf[slot, bl, h] * gl_scr[h, pl.ds(seq0 + bl, 1), :] + s_add
    write_back(step, slot).start()

    @pl.when(step > 0)
    def _():
        prev = step - 1
        prev_slot = lax.rem(prev, n_slots)
        write_back(prev, prev_slot).wait()

        @pl.when(prev + n_slots < n_blocks)
        def _():
            fetch(prev + n_slots, prev_slot).start()

    @pl.when(step == n_blocks - 1)
    def _():
        _sample_phase_c(x_ref, mod_ref, w_out_ref, ng_ref, lng_ref, lnb_ref, y_ref,
                        og_scr, osc_scr, res_scr, vn_scr, qk_scr, alpha=alpha, t_len=t_len, d=d)
        write_back(step, slot).wait()


def _mixer_sample(x, mod, w_in, w_out, conv_w, mconv_w, alog, dtb, norm_g, lng, lnb, s_in, cq_in, cm_in,
                  *, alpha, off, bt, t, layer):
    bsz, d = x.shape[0], x.shape[1] // t
    assert bsz % bt == 0
    sc_w = off["sc_c"][1] - off["sc_c"][0]
    dn_w = DN_HEADS * DN_DV
    n_conv, n_mconv = conv_w.shape[0], mconv_w.shape[0]
    n_qk = t * (t + 1) // 2
    x2 = x
    cq2 = jnp.swapaxes(cq_in, 0, 1)
    kern = functools.partial(_mixer_sample_kernel, alpha=alpha, t_len=t, d=d, off=off,
                             n_conv=n_conv, n_mconv=n_mconv, bt=bt, n_blocks=bsz // bt, layer=layer)
    consts = (*w_in, w_out, conv_w, mconv_w, alog, dtb, norm_g, lng, lnb, cq2, cm_in)
    state_spec = pl.BlockSpec(memory_space=pl.ANY)
    packed = pltpu.VMEM((DN_HEADS, _PACK * bsz, LANES), F32)
    y, cq, cm, s_out = pl.pallas_call(
        kern,
        grid=(bsz // bt,),
        in_specs=([_const_spec(x2.shape), _sub_mod_spec(bsz, d, 1)]
                  + [pl.BlockSpec(memory_space=pltpu.SMEM) if a is alog or a is dtb else _const_spec(a.shape)
                     for a in consts] + [state_spec]),
        out_specs=[pl.BlockSpec(x2.shape, lambda i: (0, 0)),
                   pl.BlockSpec(cq2.shape, lambda i: (0, 0, 0)),
                   pl.BlockSpec(cm_in.shape, lambda i: (0, 0, 0)),
                   state_spec],
        out_shape=[jax.ShapeDtypeStruct(x2.shape, F32),
                   jax.ShapeDtypeStruct(cq2.shape, F32),
                   jax.ShapeDtypeStruct(cm_in.shape, F32),
                   jax.ShapeDtypeStruct(s_in.shape, F32)],
        scratch_shapes=[pltpu.VMEM((t * bsz, dn_w), F32),
                        pltpu.VMEM((t * bsz, sc_w), F32),
                        packed,
                        packed,
                        packed,
                        pltpu.VMEM((DN_HEADS, bsz, LANES), F32),
                        pltpu.VMEM((DN_HEADS, n_qk, bsz, LANES), F32),
                        packed,
                        packed,
                        pltpu.VMEM((STATE_SLOTS, bt, DN_HEADS, DN_DK, DN_DV), F32),
                        pltpu.SemaphoreType.DMA((STATE_SLOTS,)),
                        pltpu.SemaphoreType.DMA((STATE_SLOTS,))],
        compiler_params=pltpu.CompilerParams(dimension_semantics=("arbitrary",),
                                             vmem_limit_bytes=V7X_VMEM_LIMIT),
        name="mixer_sample",
    )(x2, mod, *consts, s_in)
    return y, s_out, jnp.swapaxes(cq, 0, 1), cm


def kernel(x_prompt, x_sample, state_ssm, state_conv_qkv, state_conv_mix, c_prompt, c_sample, w_ada, b_ada, ln_g, ln_b, ffn1_wg, ffn1_wu, ffn1_wd, ffn2_wg, ffn2_wu, ffn2_wd, w_in, conv_qkv_w, a_log, dt_bias, dn_norm_g, conv_mix_w, w_out):
    depth = w_ada.shape[0]
    alpha = (2 * depth) ** 0.25
    bp, tp, d = x_prompt.shape
    qkv_dim = conv_qkv_w.shape[-1]
    sc_width = conv_mix_w.shape[-1]
    dn_width = DN_HEADS * DN_DV
    off = _proj_layout(dn_width, sc_width)
    tm = min(512, tp)
    nseq = 2 if bp % 2 == 0 else 1
    tt = min(512 // nseq, tp)
    bt = min(16, x_sample.shape[0])
    ts = x_sample.shape[1]

    hp, hs = x_prompt, x_sample
    outs = [[] for _ in range(6)]
    for l in range(depth):
        mod_p, mod_s, *w_in_p = _adaln(c_prompt, c_sample, w_ada[l], b_ada[l].reshape(1, -1),
                                       jnp.swapaxes(w_in[l], 0, 1), qkv_dim)
        lng, lnb, mconv_w = ln_g[l][:, None, :], ln_b[l][:, None, :], conv_mix_w[l][:, None, :]
        w1 = (ffn1_wg[l], ffn1_wu[l], ffn1_wd[l])
        w2 = (ffn2_wg[l], ffn2_wu[l], ffn2_wd[l])
        norm_g = dn_norm_g[l].reshape(1, -1)
        mix_args = (conv_qkv_w[l], mconv_w, a_log, dt_bias, norm_g, lng, lnb)

        hp, hs = _ffn(hp, hs, mod_p, mod_s, *w1, lng, lnb, sub=0, alpha=alpha, tm=tm, ts=ts,
                      ys_3d=False)
        hp, a1, a2, a3, w_out_b = _mixer_prompt(hp, mod_p, w_in_p, w_out[l], *mix_args, alpha=alpha,
                                                nseq=nseq, tt=tt, off=off, layer=l)
        hs, b1, b2, b3 = _mixer_sample(hs, mod_s, w_in_p, w_out_b, *mix_args, state_ssm[l], state_conv_qkv[l],
                                       state_conv_mix[l], alpha=alpha, off=off, bt=bt, t=ts, layer=l)
        hp, hs = _ffn(hp, hs, mod_p, mod_s, *w2, lng, lnb, sub=2, alpha=alpha, tm=tm, ts=ts,
                      ys_3d=True)
        for lst, val in zip(outs, (a1, a2, a3, b1, b2, b3)):
            lst.append(val)
    return (hp, hs) + tuple(jnp.stack(o) for o in outs)
```

```python
import functools

import jax
import jax.numpy as jnp
from jax import lax
from jax.experimental import pallas as pl
from jax.experimental.pallas import tpu as pltpu

F32 = jnp.float32
BF16 = jnp.bfloat16

LN_EPS = 1e-5
RMS_EPS = 1e-6
N_SUB = 3
DN_HEADS = 4
DN_DK = 128
DN_DV = 128
CHUNK = 64
LANES = 128
SUBLANES = 8
AB_PAD = LANES

V7X_VMEM_LIMIT = 60000 * 1024
CAST_BUFFERS = 3
FFN_CHUNK = 256
STATE_SLOTS = 4


def _mm(a, b):
    return jnp.dot(a.astype(BF16), b.astype(BF16), preferred_element_type=F32)


def _mm_at(a, b):
    return lax.dot_general(a.astype(BF16), b.astype(BF16), (((0,), (0,)), ((), ())),
                           preferred_element_type=F32)


def _mm_f32(a, b):
    return jnp.dot(a, b, preferred_element_type=F32, precision=lax.Precision.HIGHEST)


def _silu(x):
    return x * jax.nn.sigmoid(x)


def _softplus(x):
    return jnp.maximum(x, 0.0) + jnp.log1p(jnp.exp(-jnp.abs(x)))


def _layer_norm(y, g, b):
    mu = jnp.mean(y, axis=-1, keepdims=True)
    yc = y - mu
    var = jnp.mean(yc * yc, axis=-1, keepdims=True)
    return yc * lax.rsqrt(var + LN_EPS) * g + b


def _post(x, delta, gate, g, b, alpha):
    return _layer_norm(alpha * x + gate * delta, g, b)


def _rowsum(x):
    return jnp.sum(x, axis=-1, keepdims=True)


def _l2norm(x):
    return x * lax.rsqrt(_rowsum(x * x) + RMS_EPS)


def _const_spec(shape):
    nd = len(shape)
    return pl.BlockSpec(shape, lambda *_: (0,) * nd, pipeline_mode=pl.Buffered(1))


def _lane_vector(smem_ref, layer):
    lane = lax.broadcasted_iota(jnp.int32, (1, AB_PAD), 1)
    out = jnp.zeros((1, AB_PAD), F32)
    for h in range(smem_ref.shape[1]):
        out = jnp.where(lane == h, smem_ref[layer, h], out)
    return out


def _row(ref, j):
    return ref[j] if len(ref.shape) == 3 else ref[j:j + 1, :]


def _adaln_kernel(cp_ref, cs_ref, w_ref, b_ref, win_ref, op_ref, os_ref, wq_ref, wr_ref, wab_ref, *, qkv_dim):
    mp = cp_ref.shape[0]
    c = jnp.concatenate([cp_ref[...], cs_ref[...]], axis=0)
    mod = _mm(_silu(c), w_ref[...]) + b_ref[...]
    op_ref[...] = mod[:mp]
    os_ref[...] = mod[mp:]

    gates = 2 * DN_HEADS
    wq_ref[...] = win_ref[0:qkv_dim, :].T.astype(BF16)
    wr_ref[...] = win_ref[qkv_dim + gates:, :].T.astype(BF16)
    ab_rows = jnp.concatenate([win_ref[qkv_dim:qkv_dim + gates, :],
                               jnp.zeros((AB_PAD - gates, win_ref.shape[1]), F32)], axis=0)
    wab_ref[...] = ab_rows.T.astype(BF16)


def _adaln(c_p, c_s, w, b, w_in_t, qkv_dim):
    (mp, d), ms = c_p.shape, c_s.shape[0]
    n = w.shape[1]
    proj = w_in_t.shape[0]
    rest = proj - qkv_dim - 2 * DN_HEADS
    assert qkv_dim % LANES == 0 and rest % LANES == 0 and (2 * DN_HEADS) % SUBLANES == 0
    steps = max(g for g in (1, 2, 4) if n % (g * LANES) == 0 and d % (g * LANES) == 0)
    tn, rows = n // steps, d // steps
    return pl.pallas_call(
        functools.partial(_adaln_kernel, qkv_dim=qkv_dim),
        grid=(steps,),
        in_specs=[_const_spec((mp, d)), _const_spec((ms, d)),
                  pl.BlockSpec((d, tn), lambda j: (0, j)),
                  pl.BlockSpec((1, tn), lambda j: (0, j)),
                  pl.BlockSpec((proj, rows), lambda j: (0, j))],
        out_specs=[pl.BlockSpec((mp, tn), lambda j: (0, j)), pl.BlockSpec((ms, tn), lambda j: (0, j)),
                   pl.BlockSpec((rows, qkv_dim), lambda j: (j, 0)),
                   pl.BlockSpec((rows, rest), lambda j: (j, 0)),
                   pl.BlockSpec((rows, AB_PAD), lambda j: (j, 0))],
        out_shape=[jax.ShapeDtypeStruct((mp, n), F32), jax.ShapeDtypeStruct((ms, n), F32),
                   jax.ShapeDtypeStruct((d, qkv_dim), BF16), jax.ShapeDtypeStruct((d, rest), BF16),
                   jax.ShapeDtypeStruct((d, AB_PAD), BF16)],
        compiler_params=pltpu.CompilerParams(dimension_semantics=("arbitrary",),
                                             vmem_limit_bytes=V7X_VMEM_LIMIT),
        name="adaln",
    )(c_p, c_s, w, b, w_in_t)


def _swiglu(x, shift, scale, wg_ref, wu_ref, wd_ref):
    u = (x * (1.0 + scale) + shift).astype(BF16)
    hg = jnp.dot(u, wg_ref[...], preferred_element_type=F32)
    hu = jnp.dot(u, wu_ref[...], preferred_element_type=F32)
    h = (_silu(hg) * hu).astype(BF16)
    return jnp.dot(h, wd_ref[...], preferred_element_type=F32)


def _stack_time(ref, t_len, d):
    if len(ref.shape) == 3:
        return jnp.concatenate([ref[:, t, :] for t in range(t_len)], axis=0)
    return jnp.concatenate([ref[:, t * d:(t + 1) * d] for t in range(t_len)], axis=0)


def _mod_rows(mod_ref, idx, t_len, d):
    m = mod_ref[:, idx * d:(idx + 1) * d]
    return jnp.concatenate([m] * t_len, axis=0)


def _sub_mod_spec(bsz, d, sub):
    return pl.BlockSpec((bsz, 3 * d), lambda *_: (0, sub), pipeline_mode=pl.Buffered(1))


def _first_tile_swiglu(x, shift, scale, w_hbm, w_scr, stages, sems):
    wg_hbm, wu_hbm, wd_hbm = w_hbm
    wg_scr, wu_scr, wd_scr = w_scr
    nbuf = stages[0].shape[0]
    cw = stages[0].shape[2]
    n = wg_hbm.shape[1] // cw

    def copies(j):
        slot = j % nbuf
        cols = pl.ds(j * cw, cw)
        return (pltpu.make_async_copy(wg_hbm.at[:, cols], stages[0].at[slot], sems[0].at[slot]),
                pltpu.make_async_copy(wu_hbm.at[:, cols], stages[1].at[slot], sems[1].at[slot]),
                pltpu.make_async_copy(wd_hbm.at[cols, :], stages[2].at[slot], sems[2].at[slot]))

    queue = [copies(j) for j in range(n)]
    ahead = nbuf - 1
    for group in queue[:ahead]:
        for copy in group:
            copy.start()
    u = (x * (1.0 + scale) + shift).astype(BF16)
    acc = None
    for j, group in enumerate(queue):
        if j + ahead < n:
            for copy in queue[j + ahead]:
                copy.start()
        for copy in group:
            copy.wait()
        slot, cols = j % nbuf, slice(j * cw, (j + 1) * cw)
        wg_scr[:, cols] = stages[0][slot].astype(BF16)
        wu_scr[:, cols] = stages[1][slot].astype(BF16)
        wd_scr[cols, :] = stages[2][slot].astype(BF16)
        hg = jnp.dot(u, wg_scr[:, cols], preferred_element_type=F32)
        hu = jnp.dot(u, wu_scr[:, cols], preferred_element_type=F32)
        part = jnp.dot((_silu(hg) * hu).astype(BF16), wd_scr[cols, :], preferred_element_type=F32)
        acc = part if acc is None else acc + part
    return acc


def _ffn_kernel(xp_ref, xq_ref, modp_ref, xs_hbm, mods_hbm, wg_hbm, wu_hbm, wd_hbm, lng_ref, lnb_ref,
                yp_ref, ys_ref, wg_scr, wu_scr, wd_scr, d_scr, stage_g, stage_u, stage_d, sem_g, sem_u, sem_d,
                xs_ref, mods_ref, sem_late,
                *, sub, alpha, n_prompt, per_seq, t_len, d):
    k = pl.program_id(0)
    g, b = _row(lng_ref, sub), _row(lnb_ref, sub)
    seq_now = jnp.minimum(k, n_prompt - 1) // per_seq
    seq_prev = jnp.maximum(k - 1, 0) // per_seq

    def mod_row(seq, idx):
        return modp_ref[pl.ds(seq, 1), idx * d:(idx + 1) * d]

    def late_copies():
        return (pltpu.make_async_copy(xs_hbm, xs_ref, sem_late.at[0]),
                pltpu.make_async_copy(mods_hbm.at[:, sub * 3 * d:(sub + 1) * 3 * d], mods_ref, sem_late.at[1]))

    def matmuls():
        return _swiglu(xp_ref[...], mod_row(seq_now, 0), mod_row(seq_now, 1), wg_scr, wu_scr, wd_scr)

    def finish(d_prev):
        yp_ref[...] = _post(xq_ref[...], 0.5 * d_prev, mod_row(seq_prev, 2), g, b, alpha)

    @pl.when(k == 0)
    def _():
        d_scr[...] = _first_tile_swiglu(
            xp_ref[...], mod_row(0, 0), mod_row(0, 1),
            (wg_hbm, wu_hbm, wd_hbm), (wg_scr, wu_scr, wd_scr), (stage_g, stage_u, stage_d),
            (sem_g, sem_u, sem_d))
        for copy in late_copies():
            copy.start()

    @pl.when(jnp.logical_and(k > 0, k < n_prompt))
    def _():
        d_prev = d_scr[...]
        d_new = matmuls()
        finish(d_prev)
        d_scr[...] = d_new

    @pl.when(k == n_prompt)
    def _():
        for copy in late_copies():
            copy.wait()
        finish(d_scr[...])
        bsz = xs_ref.shape[0]
        x = _stack_time(xs_ref, t_len, d)
        delta = _swiglu(x, _mod_rows(mods_ref, 0, t_len, d), _mod_rows(mods_ref, 1, t_len, d),
                        wg_scr, wu_scr, wd_scr)
        y = _post(x, 0.5 * delta, _mod_rows(mods_ref, 2, t_len, d), g, b, alpha)
        for t in range(t_len):
            if len(ys_ref.shape) == 3:
                ys_ref[:, t, :] = y[t * bsz:(t + 1) * bsz]
            else:
                ys_ref[:, t * d:(t + 1) * d] = y[t * bsz:(t + 1) * bsz]


def _ffn(xp, xs, mod_p, mod_s, wg, wu, wd, lng, lnb, *, sub, alpha, tm, ts, ys_3d):
    bp, tp, d = xp.shape
    bs = xs.shape[0]
    ys_shape = (bs, ts, d) if ys_3d else (bs, ts * d)
    f = wg.shape[1]
    assert tp % tm == 0
    per_seq = tp // tm
    n_prompt = bp * per_seq
    assert f % FFN_CHUNK == 0
    last = n_prompt - 1
    yp, ys = pl.pallas_call(
        functools.partial(_ffn_kernel, sub=sub, alpha=alpha, n_prompt=n_prompt, per_seq=per_seq, t_len=ts, d=d),
        grid=(n_prompt + 1,),
        in_specs=[pl.BlockSpec((tm, d), lambda k: (jnp.minimum(k, last), 0)),
                  pl.BlockSpec((tm, d), lambda k: (jnp.maximum(k - 1, 0), 0)),
                  _sub_mod_spec(bp, d, sub),
                  pl.BlockSpec(memory_space=pl.ANY), pl.BlockSpec(memory_space=pl.ANY),
                  pl.BlockSpec(memory_space=pl.ANY), pl.BlockSpec(memory_space=pl.ANY),
                  pl.BlockSpec(memory_space=pl.ANY),
                  _const_spec(lng.shape), _const_spec(lnb.shape)],
        out_specs=[pl.BlockSpec((tm, d), lambda k: (jnp.maximum(k - 1, 0), 0)),
                   pl.BlockSpec(ys_shape, lambda k: (0,) * len(ys_shape))],
        out_shape=[jax.ShapeDtypeStruct((bp * tp, d), F32),
                   jax.ShapeDtypeStruct(ys_shape, F32)],
        scratch_shapes=[pltpu.VMEM((d, f), BF16), pltpu.VMEM((d, f), BF16), pltpu.VMEM((f, d), BF16),
                        pltpu.VMEM((tm, d), F32),
                        pltpu.VMEM((CAST_BUFFERS, d, FFN_CHUNK), F32), pltpu.VMEM((CAST_BUFFERS, d, FFN_CHUNK), F32),
                        pltpu.VMEM((CAST_BUFFERS, FFN_CHUNK, d), F32),
                        pltpu.SemaphoreType.DMA((CAST_BUFFERS,)), pltpu.SemaphoreType.DMA((CAST_BUFFERS,)),
                        pltpu.SemaphoreType.DMA((CAST_BUFFERS,)),
                        pltpu.VMEM(xs.shape, F32), pltpu.VMEM((bs, 3 * d), F32), pltpu.SemaphoreType.DMA((2,))],
        compiler_params=pltpu.CompilerParams(dimension_semantics=("arbitrary",),
                                             vmem_limit_bytes=V7X_VMEM_LIMIT),
        name=f"ffn{sub}",
    )(xp.reshape(bp * tp, d), xp.reshape(bp * tp, d), mod_p, xs, mod_s,
      wg, wu, wd, lng, lnb)
    return yp.reshape(bp, tp, d), ys


def _tri_inverse_steps(lmats, c):
    ls = [l.astype(BF16) for l in lmats]
    ms = [jnp.dot(l, l, preferred_element_type=F32) for l in ls]
    yield
    rows = lax.broadcasted_iota(jnp.int32, (c, c), 0)
    cols = lax.broadcasted_iota(jnp.int32, (c, c), 1)
    eye = (rows == cols).astype(F32)
    ps = [eye - l for l in lmats]
    power = 2
    while 2 * power < c:
        mp = [jnp.concatenate([m, p], axis=0).astype(BF16) for m, p in zip(ms, ps)]
        sq = [jnp.dot(x, x[:c], preferred_element_type=F32) for x in mp]
        yield
        ms = [x[:c] for x in sq]
        ps = [p + x[c:] for p, x in zip(ps, sq)]
        power *= 2
    pm = [jnp.dot(p.astype(BF16), m.astype(BF16), preferred_element_type=F32) for p, m in zip(ps, ms)]
    yield
    return [p + x for p, x in zip(ps, pm)]


def _gdn_local_steps(units, tril, strict):
    c = units[0][1].shape[0]
    a = [lax.dot_general(jnp.concatenate([kb, q], axis=0), k, (((1,), (1,)), ((), ())),
                         preferred_element_type=F32)
         for kb, q, k, _, _, _, _ in units]
    yield
    lmats, qks = [], []
    for a_u, (_, _, _, _, _, gcol, grow) in zip(a, units):
        decay = jnp.where(tril, jnp.exp(jnp.where(tril, gcol - grow, 0.0)), 0.0)
        lmats.append(jnp.where(strict, a_u[:c] * decay, 0.0))
        qks.append(jnp.where(tril, a_u[c:] * decay, 0.0))
    tms = yield from _tri_inverse_steps(lmats, c)
    uw = [jnp.dot(tm.astype(BF16), jnp.concatenate([vb, kbe], axis=1), preferred_element_type=F32)
          for tm, (_, _, _, vb, kbe, _, _) in zip(tms, units)]
    yield
    return [(x[:, :DN_DV], x[:, DN_DV:], qk) for x, qk in zip(uw, qks)]


def _alternate(primary, secondary):
    result = None
    done = False
    while not done:
        try:
            next(primary)
            yield
        except StopIteration as stop:
            result, done = stop.value, True
        if secondary is not None:
            try:
                next(secondary)
                yield
            except StopIteration:
                secondary = None
    if secondary is not None:
        yield from secondary
    return result


def _run_interleaved(*gens):
    live = list(gens)
    while live:
        for g in list(live):
            try:
                next(g)
            except StopIteration:
                live.remove(g)


def _gate_norm(o, og, norm_g):
    o = o * lax.rsqrt(jnp.mean(o * o, axis=-1, keepdims=True) + RMS_EPS) * norm_g
    return o * _silu(og)


def _proj_layout(dn_width, sc_width):
    off = {}
    pos = 0
    for name, width in (("og", dn_width), ("sc_b", sc_width), ("sc_c", sc_width), ("sc_h", sc_width)):
        off[name] = (pos, pos + width)
        pos += width
    return off


def _mixer_prompt_kernel(xf_ref, xb_ref, mod_ref, wq_ref, wr_ref, wab_ref, w_out_hbm, cw_ref,
                         mw_ref, alog_ref, dtb_ref, ng_ref, lng_ref, lnb_ref,
                         y_ref, s_out_ref, cq_out_ref, cm_out_ref, w_out_ref,
                         s_scr, cq_scr, cm_scr, gct_scr,
                         u_scr, w_scr, qk_scr, qd_scr, kd_scr, gl_scr, og_scr, osc_scr, w_out_stage, w_out_sem,
                         *, alpha, nseq, tt, tiles_per_group, n_tiles, off, n_conv, n_mconv, layer):
    k = pl.program_id(0)
    d = xf_ref.shape[-1]
    seq_front = (jnp.minimum(k, n_tiles - 1) // tiles_per_group) * nseq
    seq_back = (jnp.maximum(k - 1, 0) // tiles_per_group) * nseq

    def mod_row(seq, idx):
        return mod_ref[pl.ds(seq, 1), idx * d:(idx + 1) * d]

    rows_all = nseq * tt
    nchunk = rows_all // CHUNK
    per_seq = tt // CHUNK
    hk = DN_HEADS * DN_DK
    sub = 1
    head_lanes = [slice(h * DN_DK, (h + 1) * DN_DK) for h in range(DN_HEADS)]
    t_front = lax.rem(jnp.minimum(k, n_tiles - 1), tiles_per_group)
    t_back = lax.rem(jnp.maximum(k - 1, 0), tiles_per_group)
    keep_front = jnp.where(t_front == 0, 0.0, 1.0).astype(F32)
    keep_back = jnp.where(t_back == 0, 0.0, 1.0).astype(F32)

    def causal_conv(new_rows, hist_scr, s, w_ref, taps):
        ext = jnp.concatenate([hist_scr[s] * keep_front, new_rows], axis=0)
        acc = new_rows * _row(w_ref, taps - 1)
        for j in range(taps - 1):
            acc = acc + pltpu.roll(ext, taps - 1 - j, axis=0)[SUBLANES:] * _row(w_ref, j)
        hist_scr[s] = new_rows[tt - SUBLANES:]
        return acc

    def front():
        u = jnp.concatenate(
            [(xf_ref[s] * (1.0 + mod_row(seq_front + s, 1))
              + mod_row(seq_front + s, 0)).astype(BF16) for s in range(nseq)], axis=0)
        ab = jnp.dot(u, wab_ref[...], preferred_element_type=F32)
        pq = jnp.dot(u, wq_ref[...], preferred_element_type=F32)
        yield

        beta_all = jax.nn.sigmoid(ab)
        g = -jnp.exp(_lane_vector(alog_ref, layer)) * _softplus(ab + _lane_vector(dtb_ref, layer))
        rows = lax.broadcasted_iota(jnp.int32, (CHUNK, CHUNK), 0)
        cols = lax.broadcasted_iota(jnp.int32, (CHUNK, CHUNK), 1)
        tril = rows >= cols
        strict = rows > cols
        tril_f = tril.astype(F32)
        gc_parts, gl_parts = [], []
        for c in range(nchunk):
            gc_c = _mm_f32(tril_f, g[c * CHUNK:(c + 1) * CHUNK, :])
            gc_parts.append(gc_c)
            gl_parts.append(jnp.broadcast_to(gc_c[CHUNK - 1:CHUNK, :], (CHUNK, AB_PAD)))
        yield
        p = {}

        def project_rest(names):
            for name in names:
                lo, hi = off[name]
                p[name] = jnp.dot(u, wr_ref[:, lo:hi], preferred_element_type=F32)
                yield

        yield from project_rest(("sc_c", "sc_h"))
        gc = jnp.concatenate(gc_parts, axis=0)
        gl_rows = jnp.concatenate(gl_parts, axis=0)
        gc_t = gc.T
        for c in range(nchunk):
            gct_scr[c] = gc_t[0:SUBLANES, c * CHUNK:(c + 1) * CHUNK]

        acts = []
        for s in range(nseq):
            pq_s = pq[s * tt:(s + 1) * tt]
            acts.append(_silu(causal_conv(pq_s, cq_scr, s, cw_ref, n_conv)))
            for j in range(n_conv - 1):
                row = tt - (n_conv - 1) + j
                cq_out_ref[j, pl.ds(seq_front + s, 1), :] = pq_s[row:row + 1]
        act = jnp.concatenate(acts, axis=0)

        eg_all = jnp.exp(gc)
        ekd_all = jnp.exp(gl_rows - gc)
        egl_all = jnp.exp(gl_rows)
        per_head = []
        for h, lanes in enumerate(head_lanes):
            q = _l2norm(act[:, h * DN_DK:(h + 1) * DN_DK]) * (DN_DK ** -0.5)
            kk = _l2norm(act[:, hk + h * DN_DK:hk + (h + 1) * DN_DK])
            v = act[:, 2 * hk + h * DN_DV:2 * hk + (h + 1) * DN_DV]
            beta = beta_all[:, DN_HEADS + h:DN_HEADS + h + 1]
            eg = eg_all[:, h:h + 1]
            kb = kk * beta
            per_head.append((kb.astype(BF16), q.astype(BF16), kk.astype(BF16), (v * beta).astype(BF16),
                             (kb * eg).astype(BF16)))
            qd_scr[:, lanes] = (q * eg).astype(BF16)
            kd_scr[:, lanes] = (kk * ekd_all[:, h:h + 1]).astype(BF16)
            for c in range(nchunk):
                gl_scr[c * SUBLANES:(c + 1) * SUBLANES, lanes] = jnp.broadcast_to(
                    egl_all[c * CHUNK:c * CHUNK + SUBLANES, h:h + 1], (SUBLANES, DN_DV))

        z = p["sc_c"] * p["sc_h"]
        zcs = []
        for s in range(nseq):
            z_s = z[s * tt:(s + 1) * tt]
            zcs.append(causal_conv(z_s, cm_scr, s, mw_ref, n_mconv))
            cm_out_ref[seq_front + s] = z_s[tt - (n_mconv - 1):]

        where = [(c, h, lanes) for c in range(nchunk) for h, lanes in enumerate(head_lanes)]
        units = [tuple(arr[c * CHUNK:(c + 1) * CHUNK] for arr in per_head[h])
                 + (gc[c * CHUNK:(c + 1) * CHUNK, h:h + 1], gct_scr[c, h:h + 1, :])
                 for c, h, lanes in where]
        local = yield from _alternate(_gdn_local_steps(units, tril, strict), project_rest(("sc_b", "og")))
        osc_scr[...] = p["sc_b"] * jnp.concatenate(zcs, axis=0)
        og_scr[...] = p["og"]
        for (c, h, lanes), (u_c, w_c, qk_c) in zip(where, local):
            u_scr[c * CHUNK:(c + 1) * CHUNK, lanes] = u_c
            w_scr[c * CHUNK:(c + 1) * CHUNK, lanes] = w_c.astype(BF16)
            qk_scr[c * DN_HEADS + h] = qk_c.astype(BF16)

    def back():
        u_all, w_all, qd_all, kd_all = u_scr[...], w_scr[...], qd_scr[...], kd_scr[...]
        gl_all, og, o_sc = gl_scr[...], og_scr[...], osc_scr[...]
        qks = [qk_scr[i] for i in range(nchunk * DN_HEADS)]
        xs = [xb_ref[s] for s in range(nseq)]
        gates = [mod_row(seq_back + s, 2) for s in range(nseq)]
        states = [s_scr[i] * keep_back for i in range(nseq * DN_HEADS)]
        o_rows = [[None] * DN_HEADS for _ in range(nchunk)]
        for j in range(per_seq):
            chains = [(s, s * per_seq + j, h, lanes) for s in range(nseq) for h, lanes in enumerate(head_lanes)]
            rws = [slice(c * CHUNK, (c + 1) * CHUNK) for _, c, _, _ in chains]
            r = [jnp.dot(jnp.concatenate([w_all[rw, lanes], qd_all[rw, lanes]], axis=0),
                         states[s * DN_HEADS + h].astype(BF16), preferred_element_type=F32)
                 for (s, c, h, lanes), rw in zip(chains, rws)]
            yield
            v_new = [(u_all[rw, lanes] - r_u[:CHUNK]).astype(BF16)
                     for (s, c, h, lanes), rw, r_u in zip(chains, rws, r)]
            for (s, c, h, lanes), r_u, v_u in zip(chains, r, v_new):
                o_rows[c][h] = r_u[CHUNK:] + jnp.dot(qks[c * DN_HEADS + h], v_u, preferred_element_type=F32)
            for (s, c, h, lanes), rw, v_u in zip(chains, rws, v_new):
                i = s * DN_HEADS + h
                states[i] = (states[i] * gl_all[c * SUBLANES:c * SUBLANES + 1, lanes]
                             + lax.dot_general(kd_all[rw, lanes], v_u, (((0,), (0,)), ((), ())),
                                               preferred_element_type=F32))
            yield
        for i, st in enumerate(states):
            s_scr[i] = st
        for s in range(nseq):
            for h in range(DN_HEADS):
                s_out_ref[seq_back + s, h] = states[s * DN_HEADS + h]

        o_dn = jnp.concatenate(
            [_gate_norm(jnp.concatenate([o_rows[c][h] for c in range(nchunk)], axis=0),
                        og[:, lanes], ng_ref[...]) for h, lanes in enumerate(head_lanes)], axis=1)
        dn_w = o_dn.shape[1]
        mix = (jnp.dot(o_dn.astype(BF16), w_out_ref[0:dn_w, :], preferred_element_type=F32)
               + jnp.dot(o_sc.astype(BF16), w_out_ref[dn_w:, :], preferred_element_type=F32))
        yield
        for s in range(nseq):
            y_ref[s] = _post(xs[s], mix[s * tt:(s + 1) * tt], gates[s],
                             _row(lng_ref, sub), _row(lnb_ref, sub), alpha)

    @pl.when(k == 0)
    def _():
        s_scr[...] = jnp.zeros_like(s_scr)
        cq_scr[...] = jnp.zeros_like(cq_scr)
        cm_scr[...] = jnp.zeros_like(cm_scr)
        w_out_copy = pltpu.make_async_copy(w_out_hbm, w_out_stage, w_out_sem.at[0])
        w_out_copy.start()
        _run_interleaved(front())
        w_out_copy.wait()
        w_out_ref[...] = w_out_stage[...].astype(BF16)

    @pl.when(jnp.logical_and(k > 0, k < n_tiles))
    def _():
        _run_interleaved(back(), front())

    @pl.when(k == n_tiles)
    def _():
        _run_interleaved(back())


def _mixer_prompt(x, mod, w_in, w_out, conv_w, mconv_w, alog, dtb, norm_g, lng, lnb, *, alpha, nseq, tt, off, layer):
    bsz, t, d = x.shape
    assert t % tt == 0 and tt % LANES == 0 and bsz % nseq == 0
    nt = t // tt
    n_tiles = (bsz // nseq) * nt
    rows = nseq * tt
    qkv_dim = conv_w.shape[-1]
    sc_w = off["sc_c"][1] - off["sc_c"][0]
    dn_w = DN_HEADS * DN_DV
    n_conv, n_mconv = conv_w.shape[0], mconv_w.shape[0]
    assert max(n_conv, n_mconv) - 1 <= SUBLANES <= tt
    nchunk = rows // CHUNK
    kern = functools.partial(_mixer_prompt_kernel, alpha=alpha, nseq=nseq, tt=tt, tiles_per_group=nt,
                             n_tiles=n_tiles, off=off, n_conv=n_conv, n_mconv=n_mconv, layer=layer)

    def front_tile(k):
        return jnp.minimum(k, n_tiles - 1)

    def back_tile(k):
        return jnp.maximum(k - 1, 0)

    y, s_out, cq, cm, w_out_b = pl.pallas_call(
        kern,
        grid=(n_tiles + 1,),
        in_specs=[pl.BlockSpec((nseq, tt, d), lambda k: (front_tile(k) // nt, front_tile(k) % nt, 0)),
                  pl.BlockSpec((nseq, tt, d), lambda k: (back_tile(k) // nt, back_tile(k) % nt, 0)),
                  _sub_mod_spec(bsz, d, 1),
                  *[_const_spec(w.shape) for w in w_in], pl.BlockSpec(memory_space=pl.ANY),
                  _const_spec(conv_w.shape), _const_spec(mconv_w.shape),
                  pl.BlockSpec(memory_space=pltpu.SMEM), pl.BlockSpec(memory_space=pltpu.SMEM),
                  _const_spec(norm_g.shape), _const_spec(lng.shape), _const_spec(lnb.shape)],
        out_specs=[pl.BlockSpec((nseq, tt, d), lambda k: (back_tile(k) // nt, back_tile(k) % nt, 0)),
                   pl.BlockSpec((bsz, DN_HEADS, DN_DK, DN_DV), lambda k: (0, 0, 0, 0)),
                   pl.BlockSpec((n_conv - 1, bsz, qkv_dim), lambda k: (0, 0, 0)),
                   pl.BlockSpec((bsz, n_mconv - 1, sc_w), lambda k: (0, 0, 0)),
                   pl.BlockSpec(w_out.shape, lambda k: (0, 0))],
        out_shape=[jax.ShapeDtypeStruct((bsz, t, d), F32),
                   jax.ShapeDtypeStruct((bsz, DN_HEADS, DN_DK, DN_DV), F32),
                   jax.ShapeDtypeStruct((n_conv - 1, bsz, qkv_dim), F32),
                   jax.ShapeDtypeStruct((bsz, n_mconv - 1, sc_w), F32),
                   jax.ShapeDtypeStruct(w_out.shape, BF16)],
        scratch_shapes=[pltpu.VMEM((nseq * DN_HEADS, DN_DK, DN_DV), F32),
                        pltpu.VMEM((nseq, SUBLANES, qkv_dim), F32),
                        pltpu.VMEM((nseq, SUBLANES, sc_w), F32),
                        pltpu.VMEM((nchunk, SUBLANES, CHUNK), F32),
                        pltpu.VMEM((rows, dn_w), F32),
                        pltpu.VMEM((rows, dn_w), BF16),
                        pltpu.VMEM((nchunk * DN_HEADS, CHUNK, CHUNK), BF16),
                        pltpu.VMEM((rows, dn_w), BF16),
                        pltpu.VMEM((rows, dn_w), BF16),
                        pltpu.VMEM((nchunk * SUBLANES, dn_w), F32),
                        pltpu.VMEM((rows, dn_w), F32),
                        pltpu.VMEM((rows, sc_w), F32),
                        pltpu.VMEM(w_out.shape, F32),
                        pltpu.SemaphoreType.DMA((1,))],
        compiler_params=pltpu.CompilerParams(dimension_semantics=("arbitrary",),
                                             vmem_limit_bytes=V7X_VMEM_LIMIT),
        name="mixer_prompt",
    )(x, x, mod, *w_in, w_out, conv_w, mconv_w, alog, dtb, norm_g, lng, lnb)
    return y, s_out, jnp.swapaxes(cq, 0, 1), cm, w_out_b


_PACK = SUBLANES


def _sample_phase_a(x_ref, mod_ref, wq_ref, wr_ref, wab_ref, cw_ref, mw_ref, alog_ref, dtb_ref, cq_in_ref, cm_in_ref,
                    cq_out_ref, cm_out_ref, og_scr, osc_scr, lhs_scr, u_scr, kd_scr, gl_scr, qk_scr,
                    *, t_len, d, off, n_conv, n_mconv, layer):
    bsz = x_ref.shape[0]
    hk = DN_HEADS * DN_DK
    sc_w = off["sc_c"][1] - off["sc_c"][0]

    x = _stack_time(x_ref, t_len, d)
    u = (x * (1.0 + _mod_rows(mod_ref, 1, t_len, d)) + _mod_rows(mod_ref, 0, t_len, d)).astype(BF16)
    pq = jnp.dot(u, wq_ref[...], preferred_element_type=F32)
    ab = jnp.dot(u, wab_ref[...], preferred_element_type=F32)
    rest = jnp.dot(u, wr_ref[...], preferred_element_type=F32)
    p = {name: rest[:, lo:hi] for name, (lo, hi) in off.items()}

    def rows(arr, t):
        return arr[t * bsz:(t + 1) * bsz]

    seq = [cq_in_ref[j] for j in range(n_conv - 1)] + [rows(pq, t) for t in range(t_len)]
    act = []
    for t in range(t_len):
        acc = None
        for j in range(n_conv):
            term = seq[t + j] * cw_ref[j:j + 1, :]
            acc = term if acc is None else acc + term
        act.append(_silu(acc))
    for j in range(n_conv - 1):
        cq_out_ref[j] = seq[t_len + j]

    z = p["sc_c"] * p["sc_h"]
    zseq = ([cm_in_ref[:, j, :] for j in range(n_mconv - 1)]
            + [rows(z, t) for t in range(t_len)])
    zc = []
    for t in range(t_len):
        acc = None
        for j in range(n_mconv):
            term = zseq[t + j] * _row(mw_ref, j)
            acc = term if acc is None else acc + term
        zc.append(acc)
    for j in range(n_mconv - 1):
        cm_out_ref[:, j, :] = zseq[t_len + j]
    osc_scr[...] = p["sc_b"] * jnp.concatenate(zc, axis=0)
    og_scr[...] = p["og"]

    beta_all = jax.nn.sigmoid(ab)
    g_all = -jnp.exp(_lane_vector(alog_ref, layer)) * _softplus(ab + _lane_vector(dtb_ref, layer))
    zero = jnp.zeros((bsz, LANES), F32)

    for h in range(DN_HEADS):
        q = [_l2norm(a[:, h * DN_DK:(h + 1) * DN_DK]) * (DN_DK ** -0.5) for a in act]
        k = [_l2norm(a[:, hk + h * DN_DK:hk + (h + 1) * DN_DK]) for a in act]
        v = [a[:, 2 * hk + h * DN_DV:2 * hk + (h + 1) * DN_DV] for a in act]
        beta = [rows(beta_all, t)[:, DN_HEADS + h:DN_HEADS + h + 1] for t in range(t_len)]
        g = [rows(g_all, t)[:, h:h + 1] for t in range(t_len)]
        gc = [g[0]]
        for t in range(1, t_len):
            gc.append(gc[-1] + g[t])
        kb = [k[t] * beta[t] for t in range(t_len)]
        vb = [v[t] * beta[t] for t in range(t_len)]
        eg = [jnp.exp(gc[t]) for t in range(t_len)]
        kbe = [kb[t] * eg[t] for t in range(t_len)]
        lm = [[None] * t_len for _ in range(t_len)]
        n_qk = 0
        for i in range(t_len):
            for j in range(i + 1):
                dec = jnp.exp(gc[i] - gc[j])
                qk_scr[h, n_qk] = jnp.broadcast_to(_rowsum(q[i] * k[j]) * dec, (bsz, LANES))
                n_qk += 1
                if j < i:
                    lm[i][j] = _rowsum(kb[i] * k[j]) * dec
        tm = [[None] * t_len for _ in range(t_len)]
        for i in range(t_len):
            for j in range(i):
                acc = lm[i][j]
                for m in range(j + 1, i):
                    acc = acc + lm[i][m] * tm[m][j]
                tm[i][j] = -acc
        for i in range(t_len):
            ui, wi = vb[i], kbe[i]
            for j in range(i):
                ui = ui + tm[i][j] * vb[j]
                wi = wi + tm[i][j] * kbe[j]
            lhs_scr[h, pl.ds(i, bsz, stride=_PACK), :] = wi
            lhs_scr[h, pl.ds(t_len + i, bsz, stride=_PACK), :] = q[i] * eg[i]
            u_scr[h, pl.ds(i, bsz, stride=_PACK), :] = ui
            u_scr[h, pl.ds(t_len + i, bsz, stride=_PACK), :] = zero
            kd_scr[h, pl.ds(i, bsz, stride=_PACK), :] = k[i] * jnp.exp(gc[t_len - 1] - gc[i])
            kd_scr[h, pl.ds(t_len + i, bsz, stride=_PACK), :] = zero
        gl_scr[h] = jnp.broadcast_to(jnp.exp(gc[t_len - 1]), (bsz, LANES))


def _sample_phase_c(x_ref, mod_ref, w_out_ref, ng_ref, lng_ref, lnb_ref, y_ref,
                    og_scr, osc_scr, res_scr, vn_scr, qk_scr, *, alpha, t_len, d):
    bsz = x_ref.shape[0]
    sub = 1
    x = _stack_time(x_ref, t_len, d)
    gate = _mod_rows(mod_ref, 2, t_len, d)
    og = og_scr[...]
    per_t = []
    for i in range(t_len):
        heads = []
        for h in range(DN_HEADS):
            oi = res_scr[h, pl.ds(t_len + i, bsz, stride=_PACK), :]
            base = i * (i + 1) // 2
            for j in range(i + 1):
                oi = oi + qk_scr[h, base + j] * vn_scr[h, pl.ds(j, bsz, stride=_PACK), :]
            heads.append(_gate_norm(oi, og[i * bsz:(i + 1) * bsz, h * DN_DV:(h + 1) * DN_DV], ng_ref[...]))
        per_t.append(jnp.concatenate(heads, axis=1))
    o_dn = jnp.concatenate(per_t, axis=0)
    dn_w = o_dn.shape[1]
    mix = (jnp.dot(o_dn.astype(BF16), w_out_ref[0:dn_w, :], preferred_element_type=F32)
           + jnp.dot(osc_scr[...].astype(BF16), w_out_ref[dn_w:, :], preferred_element_type=F32))
    y = _post(x, mix, gate, _row(lng_ref, sub), _row(lnb_ref, sub), alpha)
    for t in range(t_len):
        y_ref[:, t * d:(t + 1) * d] = y[t * bsz:(t + 1) * bsz]


def _mixer_sample_kernel(x_ref, mod_ref, wq_ref, wr_ref, wab_ref, w_out_ref, cw_ref, mw_ref, alog_ref, dtb_ref, ng_ref,
                         lng_ref, lnb_ref, cq_in_ref, cm_in_ref, s_in_hbm,
                         y_ref, cq_out_ref, cm_out_ref, s_out_hbm,
                         og_scr, osc_scr, lhs_scr, u_scr, kd_scr, gl_scr, qk_scr, res_scr, vn_scr,
                         s_buf, sem_in, sem_out,
                         *, alpha, t_len, d, off, n_conv, n_mconv, bt, n_blocks, layer):
    step = pl.program_id(0)
    assert 2 * t_len == _PACK, "packed buffers hold [T rows | T rows] per sequence"
    n_slots = s_buf.shape[0]

    def fetch(block, slot):
        return pltpu.make_async_copy(s_in_hbm.at[pl.ds(block * bt, bt)], s_buf.at[slot], sem_in.at[slot])

    def write_back(block, slot):
        return pltpu.make_async_copy(s_buf.at[slot], s_out_hbm.at[pl.ds(block * bt, bt)], sem_out.at[slot])

    @pl.when(step == 0)
    def _():
        for b in range(min(n_slots, n_blocks)):
            fetch(b, b).start()
        _sample_phase_a(x_ref, mod_ref, wq_ref, wr_ref, wab_ref, cw_ref, mw_ref, alog_ref, dtb_ref, cq_in_ref, cm_in_ref,
                        cq_out_ref, cm_out_ref, og_scr, osc_scr, lhs_scr, u_scr, kd_scr, gl_scr, qk_scr,
                        t_len=t_len, d=d, off=off, n_conv=n_conv, n_mconv=n_mconv, layer=layer)

    top =(lax.broadcasted_iota(jnp.int32, (_PACK, LANES), 0) < t_len).astype(F32)

    chains = [(bl, h) for bl in range(bt) for h in range(DN_HEADS)]
    seq0 = step * bt
    row0 = [pl.multiple_of((seq0 + bl) * _PACK, _PACK) for bl in range(bt)]
    slot = lax.rem(step, n_slots)
    fetch(step, slot).wait()
    states = [s_buf[slot, bl, h] for bl, h in chains]
    r = [_mm(lhs_scr[h, pl.ds(row0[bl], _PACK), :], s) for (bl, h), s in zip(chains, states)]
    vns = []
    for (bl, h), r_u in zip(chains, r):
        res_scr[h, pl.ds(row0[bl], _PACK), :] = r_u
        vn = (u_scr[h, pl.ds(row0[bl], _PACK), :] - r_u) * top
        vn_scr[h, pl.ds(row0[bl], _PACK), :] = vn
        vns.append(vn)
    upd = [_mm_at(kd_scr[h, pl.ds(row0[bl], _PACK), :], vn) for (bl, h), vn in zip(chains, vns)]
    for (bl, h), s_add in zip(chains, upd):
        s_buf[slot, bl, h] = s_buf[slot, bl, h] * gl_scr[h, pl.ds(seq0 + bl, 1), :] + s_add
    write_back(step, slot).start()

    @pl.when(step > 0)
    def _():
        prev = step - 1
        prev_slot = lax.rem(prev, n_slots)
        write_back(prev, prev_slot).wait()

        @pl.when(prev + n_slots < n_blocks)
        def _():
            fetch(prev + n_slots, prev_slot).start()

    @pl.when(step == n_blocks - 1)
    def _():
        _sample_phase_c(x_ref, mod_ref, w_out_ref, ng_ref, lng_ref, lnb_ref, y_ref,
                        og_scr, osc_scr, res_scr, vn_scr, qk_scr, alpha=alpha, t_len=t_len, d=d)
        write_back(step, slot).wait()


def _mixer_sample(x, mod, w_in, w_out, conv_w, mconv_w, alog, dtb, norm_g, lng, lnb, s_in, cq_in, cm_in,
                  *, alpha, off, bt, t, layer):
    bsz, d = x.shape[0], x.shape[1] // t
    assert bsz % bt == 0
    sc_w = off["sc_c"][1] - off["sc_c"][0]
    dn_w = DN_HEADS * DN_DV
    n_conv, n_mconv = conv_w.shape[0], mconv_w.shape[0]
    n_qk = t * (t + 1) // 2
    x2 = x
    cq2 = jnp.swapaxes(cq_in, 0, 1)
    kern = functools.partial(_mixer_sample_kernel, alpha=alpha, t_len=t, d=d, off=off,
                             n_conv=n_conv, n_mconv=n_mconv, bt=bt, n_blocks=bsz // bt, layer=layer)
    consts = (*w_in, w_out, conv_w, mconv_w, alog, dtb, norm_g, lng, lnb, cq2, cm_in)
    state_spec = pl.BlockSpec(memory_space=pl.ANY)
    packed = pltpu.VMEM((DN_HEADS, _PACK * bsz, LANES), F32)
    y, cq, cm, s_out = pl.pallas_call(
        kern,
        grid=(bsz // bt,),
        in_specs=([_const_spec(x2.shape), _sub_mod_spec(bsz, d, 1)]
                  + [pl.BlockSpec(memory_space=pltpu.SMEM) if a is alog or a is dtb else _const_spec(a.shape)
                     for a in consts] + [state_spec]),
        out_specs=[pl.BlockSpec(x2.shape, lambda i: (0, 0)),
                   pl.BlockSpec(cq2.shape, lambda i: (0, 0, 0)),
                   pl.BlockSpec(cm_in.shape, lambda i: (0, 0, 0)),
                   state_spec],
        out_shape=[jax.ShapeDtypeStruct(x2.shape, F32),
                   jax.ShapeDtypeStruct(cq2.shape, F32),
                   jax.ShapeDtypeStruct(cm_in.shape, F32),
                   jax.ShapeDtypeStruct(s_in.shape, F32)],
        scratch_shapes=[pltpu.VMEM((t * bsz, dn_w), F32),
                        pltpu.VMEM((t * bsz, sc_w), F32),
                        packed,
                        packed,
                        packed,
                        pltpu.VMEM((DN_HEADS, bsz, LANES), F32),
                        pltpu.VMEM((DN_HEADS, n_qk, bsz, LANES), F32),
                        packed,
                        packed,
                        pltpu.VMEM((STATE_SLOTS, bt, DN_HEADS, DN_DK, DN_DV), F32),
                        pltpu.SemaphoreType.DMA((STATE_SLOTS,)),
                        pltpu.SemaphoreType.DMA((STATE_SLOTS,))],
        compiler_params=pltpu.CompilerParams(dimension_semantics=("arbitrary",),
                                             vmem_limit_bytes=V7X_VMEM_LIMIT),
        name="mixer_sample",
    )(x2, mod, *consts, s_in)
    return y, s_out, jnp.swapaxes(cq, 0, 1), cm


def kernel(x_prompt, x_sample, state_ssm, state_conv_qkv, state_conv_mix, c_prompt, c_sample, w_ada, b_ada, ln_g, ln_b, ffn1_wg, ffn1_wu, ffn1_wd, ffn2_wg, ffn2_wu, ffn2_wd, w_in, conv_qkv_w, a_log, dt_bias, dn_norm_g, conv_mix_w, w_out):
    depth = w_ada.shape[0]
    alpha = (2 * depth) ** 0.25
    bp, tp, d = x_prompt.shape
    qkv_dim = conv_qkv_w.shape[-1]
    sc_width = conv_mix_w.shape[-1]
    dn_width = DN_HEADS * DN_DV
    off = _proj_layout(dn_width, sc_width)
    tm = min(512, tp)
    nseq = 2 if bp % 2 == 0 else 1
    tt = min(512 // nseq, tp)
    bt = min(16, x_sample.shape[0])
    ts = x_sample.shape[1]

    hp, hs = x_prompt, x_sample
    outs = [[] for _ in range(6)]
    for l in range(depth):
        mod_p, mod_s, *w_in_p = _adaln(c_prompt, c_sample, w_ada[l], b_ada[l].reshape(1, -1),
                                       jnp.swapaxes(w_in[l], 0, 1), qkv_dim)
        lng, lnb, mconv_w = ln_g[l][:, None, :], ln_b[l][:, None, :], conv_mix_w[l][:, None, :]
        w1 = (ffn1_wg[l], ffn1_wu[l], ffn1_wd[l])
        w2 = (ffn2_wg[l], ffn2_wu[l], ffn2_wd[l])
        norm_g = dn_norm_g[l].reshape(1, -1)
        mix_args = (conv_qkv_w[l], mconv_w, a_log, dt_bias, norm_g, lng, lnb)

        hp, hs = _ffn(hp, hs, mod_p, mod_s, *w1, lng, lnb, sub=0, alpha=alpha, tm=tm, ts=ts,
                      ys_3d=False)
        hp, a1, a2, a3, w_out_b = _mixer_prompt(hp, mod_p, w_in_p, w_out[l], *mix_args, alpha=alpha,
                                                nseq=nseq, tt=tt, off=off, layer=l)
        hs, b1, b2, b3 = _mixer_sample(hs, mod_s, w_in_p, w_out_b, *mix_args, state_ssm[l], state_conv_qkv[l],
                                       state_conv_mix[l], alpha=alpha, off=off, bt=bt, t=ts, layer=l)
        hp, hs = _ffn(hp, hs, mod_p, mod_s, *w2, lng, lnb, sub=2, alpha=alpha, tm=tm, ts=ts,
                      ys_3d=True)
        for lst, val in zip(outs, (a1, a2, a3, b1, b2, b3)):
            lst.append(val)
    return (hp, hs) + tuple(jnp.stack(o) for o in outs)
```
